```python
import math
import jax, jax.numpy as jnp
from jax import lax
import numpy as np

D_MODEL = 1024
BATCH = 8
SEQ = 2048
DEPTH = 2

SSD_EXPAND = 2
SSD_D_INNER = SSD_EXPAND * D_MODEL
SSD_HEAD_DIM = 64
SSD_N_HEADS = SSD_D_INNER // SSD_HEAD_DIM
SSD_N_GROUPS = 4
SSD_HEADS_PER_GROUP = SSD_N_HEADS // SSD_N_GROUPS
SSD_D_STATE = 128
SSD_CONV_WIDTH = 4
SSD_CHUNK = 128
SSD_CONV_DIM = SSD_D_INNER + 2 * SSD_N_GROUPS * SSD_D_STATE

S5_WIDTH = D_MODEL
S5_GROUP = 16
S5_N_GROUPS = S5_WIDTH // S5_GROUP
S5_STATE = 64
S5_MIN_STEP = 0.001
S5_MAX_STEP = 0.1
S5_MAX_REAL = -1e-4

FFN_HIDDEN = 2816

RMS_EPS = 1e-6

IN_PROJ_DIM = SSD_D_INNER + SSD_CONV_DIM + SSD_N_HEADS + S5_WIDTH + 2 * D_MODEL
SPLITS = list(np.cumsum([SSD_D_INNER, SSD_CONV_DIM, SSD_N_HEADS, S5_WIDTH, D_MODEL]))

kernel_name = "hybrid_macaron_gated_ssd_s5"


def rmsnorm(x, g):
    xf = x.astype(jnp.float32)
    xf = xf * lax.rsqrt(jnp.mean(xf * xf, axis=-1, keepdims=True) + RMS_EPS)
    return (xf * g.astype(jnp.float32)).astype(x.dtype)


def swiglu(x, w_gate, w_up, w_down):
    return (jax.nn.silu(x @ w_gate) * (x @ w_up)) @ w_down


def causal_depthwise_conv(u, w, b):
    k, c = w.shape
    out = lax.conv_general_dilated(
        u, w[:, None, :].astype(u.dtype), window_strides=(1,), padding=[(k - 1, 0)],
        dimension_numbers=("NWC", "WIO", "NWC"), feature_group_count=c)
    return out + b.astype(u.dtype)


def segsum(a):
    t = a.shape[-1]
    cs = jnp.cumsum(a, axis=-1)
    seg = cs[..., :, None] - cs[..., None, :]
    mask = jnp.tril(jnp.ones((t, t), dtype=bool))
    return jnp.where(mask, seg, -jnp.inf)


def ssd_chunked(xdt, adt, bmat, cmat):
    b, l, h, p = xdt.shape
    g, n = bmat.shape[2], bmat.shape[3]
    r = h // g
    q = SSD_CHUNK
    c = l // q
    X = xdt.reshape(b, c, q, g, r, p)
    A = adt.reshape(b, c, q, g, r).transpose(0, 3, 4, 1, 2)
    Bc = bmat.reshape(b, c, q, g, n)
    Cc = cmat.reshape(b, c, q, g, n)
    a_cum = jnp.cumsum(A, axis=-1)
    decay_in = jnp.exp(segsum(A))
    cb = jnp.einsum("bclgn,bcsgn->bcgls", Cc, Bc)
    y_diag = jnp.einsum("bcgls,bgrcls,bcsgrp->bclgrp", cb, decay_in, X)
    decay_states = jnp.exp(a_cum[..., -1:] - a_cum)
    states = jnp.einsum("bclgn,bgrcl,bclgrp->bcgrpn", Bc, decay_states, X)
    states = jnp.concatenate([jnp.zeros_like(states[:, :1]), states], axis=1)
    chunk_a = jnp.pad(a_cum[..., -1], ((0, 0), (0, 0), (0, 0), (1, 0)))
    chunk_decay = jnp.exp(segsum(chunk_a))
    new_states = jnp.einsum("bgrzc,bcgrpn->bzgrpn", chunk_decay, states)
    states = new_states[:, :-1]
    y_off = jnp.einsum("bclgn,bcgrpn,bgrcl->bclgrp", Cc, states, jnp.exp(a_cum))
    return (y_diag + y_off).reshape(b, l, h, p)


def ssd_branch(z, xbc, dt_raw, conv_w, conv_b, dt_bias, a_log, d_skip, norm_g):
    b, l, _ = z.shape
    xbc = jax.nn.silu(causal_depthwise_conv(xbc, conv_w, conv_b)).astype(jnp.float32)
    xs = xbc[..., :SSD_D_INNER].reshape(b, l, SSD_N_HEADS, SSD_HEAD_DIM)
    bm = xbc[..., SSD_D_INNER:SSD_D_INNER + SSD_N_GROUPS * SSD_D_STATE].reshape(b, l, SSD_N_GROUPS, SSD_D_STATE)
    cm = xbc[..., SSD_D_INNER + SSD_N_GROUPS * SSD_D_STATE:].reshape(b, l, SSD_N_GROUPS, SSD_D_STATE)
    dt = jax.nn.softplus(dt_raw.astype(jnp.float32) + dt_bias.astype(jnp.float32))
    a = -jnp.exp(a_log.astype(jnp.float32))
    y = ssd_chunked(xs * dt[..., None], a * dt, bm, cm) + d_skip.astype(jnp.float32)[:, None] * xs
    y = y.reshape(b, l, SSD_D_INNER)
    y = y * jax.nn.silu(z.astype(jnp.float32))
    yg = y.reshape(b, l, SSD_N_GROUPS, SSD_D_INNER // SSD_N_GROUPS)
    yg = yg * lax.rsqrt(jnp.mean(yg * yg, axis=-1, keepdims=True) + RMS_EPS)
    y = yg.reshape(b, l, SSD_D_INNER) * norm_g.astype(jnp.float32)
    return y.astype(z.dtype)


def s5_branch(u, lam_re, lam_im, b_re, b_im, c_re, c_im, log_step, d_skip, w_glu):
    b, l, _ = u.shape
    uf = u.astype(jnp.float32)
    ug = uf.reshape(b, l, S5_N_GROUPS, S5_GROUP)
    lam = lax.complex(jnp.minimum(lam_re.astype(jnp.float32), S5_MAX_REAL), lam_im.astype(jnp.float32))
    step = jnp.exp(log_step.astype(jnp.float32))[:, None]
    lam_bar = jnp.exp(lam * step)
    b_bar = ((lam_bar - 1.0) / lam)[..., None] * lax.complex(b_re.astype(jnp.float32), b_im.astype(jnp.float32))
    bu = jnp.einsum("blgc,gnc->blgn", ug.astype(jnp.complex64), b_bar)
    a_elems = jnp.broadcast_to(lam_bar, bu.shape)

    def combine(e1, e2):
        a1, s1 = e1
        a2, s2 = e2
        return a2 * a1, a2 * s1 + s2

    _, states = lax.associative_scan(combine, (a_elems, bu), axis=1)
    c = lax.complex(c_re.astype(jnp.float32), c_im.astype(jnp.float32))
    y = jnp.real(jnp.einsum("blgn,gcn->blgc", states, c)).reshape(b, l, S5_WIDTH)
    y = y + d_skip.astype(jnp.float32) * uf
    y = jax.nn.gelu(y).astype(u.dtype)
    val, gate = jnp.split(y @ w_glu, 2, axis=-1)
    return val * jax.nn.sigmoid(gate)


def mixer(u, w_in, ssd_conv_w, ssd_conv_b, ssd_dt_bias, ssd_a_log, ssd_d, ssd_norm_g, w_branch_a,
          s5_lambda_re, s5_lambda_im, s5_b_re, s5_b_im, s5_c_re, s5_c_im, s5_log_step, s5_d, s5_w_glu,
          w_branch_b, w_out):
    proj = u @ w_in
    z, xbc, dt_raw, u5, g_a, g_b = jnp.split(proj, SPLITS, axis=-1)
    y_a = ssd_branch(z, xbc, dt_raw, ssd_conv_w, ssd_conv_b, ssd_dt_bias, ssd_a_log, ssd_d, ssd_norm_g) @ w_branch_a
    y_b = s5_branch(u5, s5_lambda_re, s5_lambda_im, s5_b_re, s5_b_im, s5_c_re, s5_c_im,
                    s5_log_step, s5_d, s5_w_glu) @ w_branch_b
    merged = jax.nn.sigmoid(g_a) * y_a + jax.nn.sigmoid(g_b) * y_b
    return merged @ w_out


def _fwd_setup_inputs(seed: int = 0) -> dict:
    key = jax.random.key(seed)
    ks = iter(jax.random.split(key, 40))
    f32 = jnp.float32
    L = DEPTH

    def normal(shape, scale):
        return jax.random.normal(next(ks), shape, f32) * scale

    def gain(shape):
        return 1.0 + 0.02 * jax.random.normal(next(ks), shape, f32)

    x = jax.random.normal(next(ks), (BATCH, SEQ, D_MODEL), f32)
    d_ = D_MODEL
    dt_init = jnp.exp(jax.random.uniform(next(ks), (L, SSD_N_HEADS), f32,
                                         math.log(0.001), math.log(0.1)))
    n_idx = jnp.arange(S5_STATE, dtype=f32)
    return {
        "x": x,
        "ffn1_pre_g": gain((L, d_)),
        "ffn1_post_g": gain((L, d_)),
        "ffn1_w_gate": normal((L, d_, FFN_HIDDEN), d_ ** -0.5),
        "ffn1_w_up": normal((L, d_, FFN_HIDDEN), d_ ** -0.5),
        "ffn1_w_down": normal((L, FFN_HIDDEN, d_), FFN_HIDDEN ** -0.5),
        "mix_pre_g": gain((L, d_)),
        "mix_post_g": gain((L, d_)),
        "w_in": normal((L, d_, IN_PROJ_DIM), d_ ** -0.5),
        "ssd_conv_w": normal((L, SSD_CONV_WIDTH, SSD_CONV_DIM), SSD_CONV_WIDTH ** -0.5),
        "ssd_conv_b": normal((L, SSD_CONV_DIM), 0.02),
        "ssd_dt_bias": dt_init + jnp.log(-jnp.expm1(-dt_init)),
        "ssd_a_log": jnp.log(jax.random.uniform(next(ks), (L, SSD_N_HEADS), f32, 1.0, 16.0)),
        "ssd_d": gain((L, SSD_N_HEADS)),
        "ssd_norm_g": gain((L, SSD_D_INNER)),
        "w_branch_a": normal((L, SSD_D_INNER, d_), SSD_D_INNER ** -0.5),
        "s5_lambda_re": -0.5 + 0.01 * jax.random.normal(next(ks), (L, S5_N_GROUPS, S5_STATE), f32),
        "s5_lambda_im": math.pi * n_idx + 0.01 * jax.random.normal(next(ks), (L, S5_N_GROUPS, S5_STATE), f32),
        "s5_b_re": normal((L, S5_N_GROUPS, S5_STATE, S5_GROUP), (2 * S5_GROUP) ** -0.5),
        "s5_b_im": normal((L, S5_N_GROUPS, S5_STATE, S5_GROUP), (2 * S5_GROUP) ** -0.5),
        "s5_c_re": normal((L, S5_N_GROUPS, S5_GROUP, S5_STATE), (2 * S5_STATE) ** -0.5),
        "s5_c_im": normal((L, S5_N_GROUPS, S5_GROUP, S5_STATE), (2 * S5_STATE) ** -0.5),
        "s5_log_step": jax.random.uniform(next(ks), (L, S5_N_GROUPS), f32,
                                          math.log(S5_MIN_STEP), math.log(S5_MAX_STEP)),
        "s5_d": normal((L, S5_WIDTH), 1.0),
        "s5_w_glu": normal((L, S5_WIDTH, 2 * S5_WIDTH), S5_WIDTH ** -0.5),
        "w_branch_b": normal((L, S5_WIDTH, d_), S5_WIDTH ** -0.5),
        "w_out": normal((L, d_, d_), d_ ** -0.5),
        "ffn2_pre_g": gain((L, d_)),
        "ffn2_post_g": gain((L, d_)),
        "ffn2_w_gate": normal((L, d_, FFN_HIDDEN), d_ ** -0.5),
        "ffn2_w_up": normal((L, d_, FFN_HIDDEN), d_ ** -0.5),
        "ffn2_w_down": normal((L, FFN_HIDDEN, d_), FFN_HIDDEN ** -0.5),
    }


def _fwd_reference(x, ffn1_pre_g, ffn1_post_g, ffn1_w_gate, ffn1_w_up, ffn1_w_down,
              mix_pre_g, mix_post_g, w_in, ssd_conv_w, ssd_conv_b, ssd_dt_bias, ssd_a_log, ssd_d,
              ssd_norm_g, w_branch_a, s5_lambda_re, s5_lambda_im, s5_b_re, s5_b_im, s5_c_re, s5_c_im,
              s5_log_step, s5_d, s5_w_glu, w_branch_b, w_out,
              ffn2_pre_g, ffn2_post_g, ffn2_w_gate, ffn2_w_up, ffn2_w_down):
    h = x
    for i in range(DEPTH):
        f = swiglu(rmsnorm(h, ffn1_pre_g[i]), ffn1_w_gate[i], ffn1_w_up[i], ffn1_w_down[i])
        h = h + 0.5 * rmsnorm(f, ffn1_post_g[i])
        m = mixer(rmsnorm(h, mix_pre_g[i]), w_in[i], ssd_conv_w[i], ssd_conv_b[i], ssd_dt_bias[i],
                  ssd_a_log[i], ssd_d[i], ssd_norm_g[i], w_branch_a[i],
                  s5_lambda_re[i], s5_lambda_im[i], s5_b_re[i], s5_b_im[i], s5_c_re[i], s5_c_im[i],
                  s5_log_step[i], s5_d[i], s5_w_glu[i], w_branch_b[i], w_out[i])
        h = h + rmsnorm(m, mix_post_g[i])
        f = swiglu(rmsnorm(h, ffn2_pre_g[i]), ffn2_w_gate[i], ffn2_w_up[i], ffn2_w_down[i])
        h = h + 0.5 * rmsnorm(f, ffn2_post_g[i])
    return h


import jax as _jax
import jax.numpy as _jnp

TWIN_FORMAT = 'train_step'
FWD_PARAMS = ['x', 'ffn1_pre_g', 'ffn1_post_g', 'ffn1_w_gate', 'ffn1_w_up', 'ffn1_w_down', 'mix_pre_g', 'mix_post_g', 'w_in', 'ssd_conv_w', 'ssd_conv_b', 'ssd_dt_bias', 'ssd_a_log', 'ssd_d', 'ssd_norm_g', 'w_branch_a', 's5_lambda_re', 's5_lambda_im', 's5_b_re', 's5_b_im', 's5_c_re', 's5_c_im', 's5_log_step', 's5_d', 's5_w_glu', 'w_branch_b', 'w_out', 'ffn2_pre_g', 'ffn2_post_g', 'ffn2_w_gate', 'ffn2_w_up', 'ffn2_w_down']
TWIN_WEIGHTS = ['ffn1_pre_g', 'ffn1_post_g', 'ffn1_w_gate', 'ffn1_w_up', 'ffn1_w_down', 'mix_pre_g', 'mix_post_g', 'w_in', 'ssd_conv_w', 'ssd_conv_b', 'ssd_dt_bias', 'ssd_a_log', 'ssd_d', 'ssd_norm_g', 'w_branch_a', 's5_lambda_re', 's5_lambda_im', 's5_b_re', 's5_b_im', 's5_c_re', 's5_c_im', 's5_log_step', 's5_d', 's5_w_glu', 'w_branch_b', 'w_out', 'ffn2_pre_g', 'ffn2_post_g', 'ffn2_w_gate', 'ffn2_w_up', 'ffn2_w_down']
TWIN_DIFF_INPUT = 'x'
TWIN_INPUTS = ['x', 'ffn1_pre_g', 'ffn1_post_g', 'ffn1_w_gate', 'ffn1_w_up', 'ffn1_w_down', 'mix_pre_g', 'mix_post_g', 'w_in', 'ssd_conv_w', 'ssd_conv_b', 'ssd_dt_bias', 'ssd_a_log', 'ssd_d', 'ssd_norm_g', 'w_branch_a', 's5_lambda_re', 's5_lambda_im', 's5_b_re', 's5_b_im', 's5_c_re', 's5_c_im', 's5_log_step', 's5_d', 's5_w_glu', 'w_branch_b', 'w_out', 'ffn2_pre_g', 'ffn2_post_g', 'ffn2_w_gate', 'ffn2_w_up', 'ffn2_w_down', 'loss_target', 'm_ffn1_pre_g', 'm_ffn1_post_g', 'm_ffn1_w_gate', 'm_ffn1_w_up', 'm_ffn1_w_down', 'm_mix_pre_g', 'm_mix_post_g', 'm_w_in', 'm_ssd_conv_w', 'm_ssd_conv_b', 'm_ssd_dt_bias', 'm_ssd_a_log', 'm_ssd_d', 'm_ssd_norm_g', 'm_w_branch_a', 'm_s5_lambda_re', 'm_s5_lambda_im', 'm_s5_b_re', 'm_s5_b_im', 'm_s5_c_re', 'm_s5_c_im', 'm_s5_log_step', 'm_s5_d', 'm_s5_w_glu', 'm_w_branch_b', 'm_w_out', 'm_ffn2_pre_g', 'm_ffn2_post_g', 'm_ffn2_w_gate', 'm_ffn2_w_up', 'm_ffn2_w_down', 'v_ffn1_pre_g', 'v_ffn1_post_g', 'v_ffn1_w_gate', 'v_ffn1_w_up', 'v_ffn1_w_down', 'v_mix_pre_g', 'v_mix_post_g', 'v_w_in', 'v_ssd_conv_w', 'v_ssd_conv_b', 'v_ssd_dt_bias', 'v_ssd_a_log', 'v_ssd_d', 'v_ssd_norm_g', 'v_w_branch_a', 'v_s5_lambda_re', 'v_s5_lambda_im', 'v_s5_b_re', 'v_s5_b_im', 'v_s5_c_re', 'v_s5_c_im', 'v_s5_log_step', 'v_s5_d', 'v_s5_w_glu', 'v_w_branch_b', 'v_w_out', 'v_ffn2_pre_g', 'v_ffn2_post_g', 'v_ffn2_w_gate', 'v_ffn2_w_up', 'v_ffn2_w_down']
TWIN_OUTPUTS = ['loss', 'grad_x', 'grad_ffn1_pre_g', 'grad_ffn1_post_g', 'grad_ffn1_w_gate', 'grad_ffn1_w_up', 'grad_ffn1_w_down', 'grad_mix_pre_g', 'grad_mix_post_g', 'grad_w_in', 'grad_ssd_conv_w', 'grad_ssd_conv_b', 'grad_ssd_dt_bias', 'grad_ssd_a_log', 'grad_ssd_d', 'grad_ssd_norm_g', 'grad_w_branch_a', 'grad_s5_lambda_re', 'grad_s5_lambda_im', 'grad_s5_b_re', 'grad_s5_b_im', 'grad_s5_c_re', 'grad_s5_c_im', 'grad_s5_log_step', 'grad_s5_d', 'grad_s5_w_glu', 'grad_w_branch_b', 'grad_w_out', 'grad_ffn2_pre_g', 'grad_ffn2_post_g', 'grad_ffn2_w_gate', 'grad_ffn2_w_up', 'grad_ffn2_w_down', 'delta_ffn1_pre_g', 'delta_ffn1_post_g', 'delta_ffn1_w_gate', 'delta_ffn1_w_up', 'delta_ffn1_w_down', 'delta_mix_pre_g', 'delta_mix_post_g', 'delta_w_in', 'delta_ssd_conv_w', 'delta_ssd_conv_b', 'delta_ssd_dt_bias', 'delta_ssd_a_log', 'delta_ssd_d', 'delta_ssd_norm_g', 'delta_w_branch_a', 'delta_s5_lambda_re', 'delta_s5_lambda_im', 'delta_s5_b_re', 'delta_s5_b_im', 'delta_s5_c_re', 'delta_s5_c_im', 'delta_s5_log_step', 'delta_s5_d', 'delta_s5_w_glu', 'delta_w_branch_b', 'delta_w_out', 'delta_ffn2_pre_g', 'delta_ffn2_post_g', 'delta_ffn2_w_gate', 'delta_ffn2_w_up', 'delta_ffn2_w_down', 'new_m_ffn1_pre_g', 'new_m_ffn1_post_g', 'new_m_ffn1_w_gate', 'new_m_ffn1_w_up', 'new_m_ffn1_w_down', 'new_m_mix_pre_g', 'new_m_mix_post_g', 'new_m_w_in', 'new_m_ssd_conv_w', 'new_m_ssd_conv_b', 'new_m_ssd_dt_bias', 'new_m_ssd_a_log', 'new_m_ssd_d', 'new_m_ssd_norm_g', 'new_m_w_branch_a', 'new_m_s5_lambda_re', 'new_m_s5_lambda_im', 'new_m_s5_b_re', 'new_m_s5_b_im', 'new_m_s5_c_re', 'new_m_s5_c_im', 'new_m_s5_log_step', 'new_m_s5_d', 'new_m_s5_w_glu', 'new_m_w_branch_b', 'new_m_w_out', 'new_m_ffn2_pre_g', 'new_m_ffn2_post_g', 'new_m_ffn2_w_gate', 'new_m_ffn2_w_up', 'new_m_ffn2_w_down', 'new_v_ffn1_pre_g', 'new_v_ffn1_post_g', 'new_v_ffn1_w_gate', 'new_v_ffn1_w_up', 'new_v_ffn1_w_down', 'new_v_mix_pre_g', 'new_v_mix_post_g', 'new_v_w_in', 'new_v_ssd_conv_w', 'new_v_ssd_conv_b', 'new_v_ssd_dt_bias', 'new_v_ssd_a_log', 'new_v_ssd_d', 'new_v_ssd_norm_g', 'new_v_w_branch_a', 'new_v_s5_lambda_re', 'new_v_s5_lambda_im', 'new_v_s5_b_re', 'new_v_s5_b_im', 'new_v_s5_c_re', 'new_v_s5_c_im', 'new_v_s5_log_step', 'new_v_s5_d', 'new_v_s5_w_glu', 'new_v_w_branch_b', 'new_v_w_out', 'new_v_ffn2_pre_g', 'new_v_ffn2_post_g', 'new_v_ffn2_w_gate', 'new_v_ffn2_w_up', 'new_v_ffn2_w_down']
TWIN_LEAF_KINDS = {'loss': 'loss', 'grad_x': 'grad_x', 'grad_ffn1_pre_g': 'grad_w', 'grad_ffn1_post_g': 'grad_w', 'grad_ffn1_w_gate': 'grad_w', 'grad_ffn1_w_up': 'grad_w', 'grad_ffn1_w_down': 'grad_w', 'grad_mix_pre_g': 'grad_w', 'grad_mix_post_g': 'grad_w', 'grad_w_in': 'grad_w', 'grad_ssd_conv_w': 'grad_w', 'grad_ssd_conv_b': 'grad_w', 'grad_ssd_dt_bias': 'grad_w', 'grad_ssd_a_log': 'grad_w', 'grad_ssd_d': 'grad_w', 'grad_ssd_norm_g': 'grad_w', 'grad_w_branch_a': 'grad_w', 'grad_s5_lambda_re': 'grad_w', 'grad_s5_lambda_im': 'grad_w', 'grad_s5_b_re': 'grad_w', 'grad_s5_b_im': 'grad_w', 'grad_s5_c_re': 'grad_w', 'grad_s5_c_im': 'grad_w', 'grad_s5_log_step': 'grad_w', 'grad_s5_d': 'grad_w', 'grad_s5_w_glu': 'grad_w', 'grad_w_branch_b': 'grad_w', 'grad_w_out': 'grad_w', 'grad_ffn2_pre_g': 'grad_w', 'grad_ffn2_post_g': 'grad_w', 'grad_ffn2_w_gate': 'grad_w', 'grad_ffn2_w_up': 'grad_w', 'grad_ffn2_w_down': 'grad_w', 'delta_ffn1_pre_g': 'delta_w', 'delta_ffn1_post_g': 'delta_w', 'delta_ffn1_w_gate': 'delta_w', 'delta_ffn1_w_up': 'delta_w', 'delta_ffn1_w_down': 'delta_w', 'delta_mix_pre_g': 'delta_w', 'delta_mix_post_g': 'delta_w', 'delta_w_in': 'delta_w', 'delta_ssd_conv_w': 'delta_w', 'delta_ssd_conv_b': 'delta_w', 'delta_ssd_dt_bias': 'delta_w', 'delta_ssd_a_log': 'delta_w', 'delta_ssd_d': 'delta_w', 'delta_ssd_norm_g': 'delta_w', 'delta_w_branch_a': 'delta_w', 'delta_s5_lambda_re': 'delta_w', 'delta_s5_lambda_im': 'delta_w', 'delta_s5_b_re': 'delta_w', 'delta_s5_b_im': 'delta_w', 'delta_s5_c_re': 'delta_w', 'delta_s5_c_im': 'delta_w', 'delta_s5_log_step': 'delta_w', 'delta_s5_d': 'delta_w', 'delta_s5_w_glu': 'delta_w', 'delta_w_branch_b': 'delta_w', 'delta_w_out': 'delta_w', 'delta_ffn2_pre_g': 'delta_w', 'delta_ffn2_post_g': 'delta_w', 'delta_ffn2_w_gate': 'delta_w', 'delta_ffn2_w_up': 'delta_w', 'delta_ffn2_w_down': 'delta_w', 'new_m_ffn1_pre_g': 'new_m', 'new_m_ffn1_post_g': 'new_m', 'new_m_ffn1_w_gate': 'new_m', 'new_m_ffn1_w_up': 'new_m', 'new_m_ffn1_w_down': 'new_m', 'new_m_mix_pre_g': 'new_m', 'new_m_mix_post_g': 'new_m', 'new_m_w_in': 'new_m', 'new_m_ssd_conv_w': 'new_m', 'new_m_ssd_conv_b': 'new_m', 'new_m_ssd_dt_bias': 'new_m', 'new_m_ssd_a_log': 'new_m', 'new_m_ssd_d': 'new_m', 'new_m_ssd_norm_g': 'new_m', 'new_m_w_branch_a': 'new_m', 'new_m_s5_lambda_re': 'new_m', 'new_m_s5_lambda_im': 'new_m', 'new_m_s5_b_re': 'new_m', 'new_m_s5_b_im': 'new_m', 'new_m_s5_c_re': 'new_m', 'new_m_s5_c_im': 'new_m', 'new_m_s5_log_step': 'new_m', 'new_m_s5_d': 'new_m', 'new_m_s5_w_glu': 'new_m', 'new_m_w_branch_b': 'new_m', 'new_m_w_out': 'new_m', 'new_m_ffn2_pre_g': 'new_m', 'new_m_ffn2_post_g': 'new_m', 'new_m_ffn2_w_gate': 'new_m', 'new_m_ffn2_w_up': 'new_m', 'new_m_ffn2_w_down': 'new_m', 'new_v_ffn1_pre_g': 'new_v', 'new_v_ffn1_post_g': 'new_v', 'new_v_ffn1_w_gate': 'new_v', 'new_v_ffn1_w_up': 'new_v', 'new_v_ffn1_w_down': 'new_v', 'new_v_mix_pre_g': 'new_v', 'new_v_mix_post_g': 'new_v', 'new_v_w_in': 'new_v', 'new_v_ssd_conv_w': 'new_v', 'new_v_ssd_conv_b': 'new_v', 'new_v_ssd_dt_bias': 'new_v', 'new_v_ssd_a_log': 'new_v', 'new_v_ssd_d': 'new_v', 'new_v_ssd_norm_g': 'new_v', 'new_v_w_branch_a': 'new_v', 'new_v_s5_lambda_re': 'new_v', 'new_v_s5_lambda_im': 'new_v', 'new_v_s5_b_re': 'new_v', 'new_v_s5_b_im': 'new_v', 'new_v_s5_c_re': 'new_v', 'new_v_s5_c_im': 'new_v', 'new_v_s5_log_step': 'new_v', 'new_v_s5_d': 'new_v', 'new_v_s5_w_glu': 'new_v', 'new_v_w_branch_b': 'new_v', 'new_v_w_out': 'new_v', 'new_v_ffn2_pre_g': 'new_v', 'new_v_ffn2_post_g': 'new_v', 'new_v_ffn2_w_gate': 'new_v', 'new_v_ffn2_w_up': 'new_v', 'new_v_ffn2_w_down': 'new_v'}


def _forward(args):
    return _fwd_reference(*[args[k] for k in FWD_PARAMS])


def _output_shape():
    out = _jax.eval_shape(lambda: _forward(_fwd_setup_inputs(0)))
    return out.shape, out.dtype

N_MICROBATCH = 1
ADAM_LR = 0.001
ADAM_B1 = 0.9
ADAM_B2 = 0.999
ADAM_EPS = 1e-08
ADAM_WD = 0.01
ADAM_STEP = 10
PER_EXAMPLE_BATCH_AXIS = {'x': 0, 'loss_target': 0}
SHARED_INPUTS = []
_WEIGHT_DTYPES = {'ffn1_pre_g': _jnp.float32, 'ffn1_post_g': _jnp.float32, 'ffn1_w_gate': _jnp.float32, 'ffn1_w_up': _jnp.float32, 'ffn1_w_down': _jnp.float32, 'mix_pre_g': _jnp.float32, 'mix_post_g': _jnp.float32, 'w_in': _jnp.float32, 'ssd_conv_w': _jnp.float32, 'ssd_conv_b': _jnp.float32, 'ssd_dt_bias': _jnp.float32, 'ssd_a_log': _jnp.float32, 'ssd_d': _jnp.float32, 'ssd_norm_g': _jnp.float32, 'w_branch_a': _jnp.float32, 's5_lambda_re': _jnp.float32, 's5_lambda_im': _jnp.float32, 's5_b_re': _jnp.float32, 's5_b_im': _jnp.float32, 's5_c_re': _jnp.float32, 's5_c_im': _jnp.float32, 's5_log_step': _jnp.float32, 's5_d': _jnp.float32, 's5_w_glu': _jnp.float32, 'w_branch_b': _jnp.float32, 'w_out': _jnp.float32, 'ffn2_pre_g': _jnp.float32, 'ffn2_post_g': _jnp.float32, 'ffn2_w_gate': _jnp.float32, 'ffn2_w_up': _jnp.float32, 'ffn2_w_down': _jnp.float32}
MOMENT_SCALE = {'ffn1_pre_g': 4.449116e-01, 'ffn1_post_g': 3.900295e+00, 'ffn1_w_gate': 1.905159e-01, 'ffn1_w_up': 1.987507e-01, 'ffn1_w_down': 3.310627e-01, 'mix_pre_g': 6.374666e-01, 'mix_post_g': 1.607308e+01, 'w_in': 2.247844e-01, 'ssd_conv_w': 3.209443e-01, 'ssd_conv_b': 9.362074e-01, 'ssd_dt_bias': 4.516318e-01, 'ssd_a_log': 1.605356e+00, 'ssd_d': 1.933940e+00, 'ssd_norm_g': 4.915587e-01, 'w_branch_a': 7.015325e-01, 's5_lambda_re': 9.253745e-03, 's5_lambda_im': 1.183969e-02, 's5_b_re': 5.946861e-03, 's5_b_im': 6.659082e-03, 's5_c_re': 1.408697e-02, 's5_c_im': 1.208450e-02, 's5_log_step': 5.035745e+00, 's5_d': 5.804466e-01, 's5_w_glu': 4.023514e-01, 'w_branch_b': 5.550807e-01, 'w_out': 9.223334e-01, 'ffn2_pre_g': 3.511684e-01, 'ffn2_post_g': 3.962911e+00, 'ffn2_w_gate': 1.266030e-01, 'ffn2_w_up': 1.646550e-01, 'ffn2_w_down': 2.720891e-01}


def _to_microbatches(a, axis):
    t = _jnp.moveaxis(a, axis, 0)
    t = t.reshape((N_MICROBATCH, t.shape[0] // N_MICROBATCH) + t.shape[1:])
    return _jnp.moveaxis(t, 1, axis + 1)


def setup_inputs(seed: int = 0) -> dict:
    inp = _fwd_setup_inputs(seed)
    key = _jax.random.fold_in(_jax.random.key(seed), 7919)
    shape, _ = _output_shape()
    out = dict(inp)
    out["loss_target"] = _jax.random.normal(_jax.random.fold_in(key, 0), shape, _jnp.float32)
    for i, name in enumerate(TWIN_WEIGHTS):
        w = inp[name].astype(_jnp.float32)
        if MOMENT_SCALE is None:
            s = _jnp.sqrt(_jnp.mean(_jnp.square(w)) + 1e-30)
        else:
            s = MOMENT_SCALE[name]
        km, kv = _jax.random.split(_jax.random.fold_in(key, i + 1))
        out[name] = w
        out["m_" + name] = s * _jax.random.normal(km, w.shape, _jnp.float32)
        out["v_" + name] = (s * s) * _jax.random.uniform(kv, w.shape, _jnp.float32, 0.5, 1.5)
    if N_MICROBATCH > 1:
        for name, axis in PER_EXAMPLE_BATCH_AXIS.items():
            out[name] = _to_microbatches(out[name], axis)
    return {'x': out['x'], 'ffn1_pre_g': out['ffn1_pre_g'], 'ffn1_post_g': out['ffn1_post_g'], 'ffn1_w_gate': out['ffn1_w_gate'], 'ffn1_w_up': out['ffn1_w_up'], 'ffn1_w_down': out['ffn1_w_down'], 'mix_pre_g': out['mix_pre_g'], 'mix_post_g': out['mix_post_g'], 'w_in': out['w_in'], 'ssd_conv_w': out['ssd_conv_w'], 'ssd_conv_b': out['ssd_conv_b'], 'ssd_dt_bias': out['ssd_dt_bias'], 'ssd_a_log': out['ssd_a_log'], 'ssd_d': out['ssd_d'], 'ssd_norm_g': out['ssd_norm_g'], 'w_branch_a': out['w_branch_a'], 's5_lambda_re': out['s5_lambda_re'], 's5_lambda_im': out['s5_lambda_im'], 's5_b_re': out['s5_b_re'], 's5_b_im': out['s5_b_im'], 's5_c_re': out['s5_c_re'], 's5_c_im': out['s5_c_im'], 's5_log_step': out['s5_log_step'], 's5_d': out['s5_d'], 's5_w_glu': out['s5_w_glu'], 'w_branch_b': out['w_branch_b'], 'w_out': out['w_out'], 'ffn2_pre_g': out['ffn2_pre_g'], 'ffn2_post_g': out['ffn2_post_g'], 'ffn2_w_gate': out['ffn2_w_gate'], 'ffn2_w_up': out['ffn2_w_up'], 'ffn2_w_down': out['ffn2_w_down'], 'loss_target': out['loss_target'], 'm_ffn1_pre_g': out['m_ffn1_pre_g'], 'm_ffn1_post_g': out['m_ffn1_post_g'], 'm_ffn1_w_gate': out['m_ffn1_w_gate'], 'm_ffn1_w_up': out['m_ffn1_w_up'], 'm_ffn1_w_down': out['m_ffn1_w_down'], 'm_mix_pre_g': out['m_mix_pre_g'], 'm_mix_post_g': out['m_mix_post_g'], 'm_w_in': out['m_w_in'], 'm_ssd_conv_w': out['m_ssd_conv_w'], 'm_ssd_conv_b': out['m_ssd_conv_b'], 'm_ssd_dt_bias': out['m_ssd_dt_bias'], 'm_ssd_a_log': out['m_ssd_a_log'], 'm_ssd_d': out['m_ssd_d'], 'm_ssd_norm_g': out['m_ssd_norm_g'], 'm_w_branch_a': out['m_w_branch_a'], 'm_s5_lambda_re': out['m_s5_lambda_re'], 'm_s5_lambda_im': out['m_s5_lambda_im'], 'm_s5_b_re': out['m_s5_b_re'], 'm_s5_b_im': out['m_s5_b_im'], 'm_s5_c_re': out['m_s5_c_re'], 'm_s5_c_im': out['m_s5_c_im'], 'm_s5_log_step': out['m_s5_log_step'], 'm_s5_d': out['m_s5_d'], 'm_s5_w_glu': out['m_s5_w_glu'], 'm_w_branch_b': out['m_w_branch_b'], 'm_w_out': out['m_w_out'], 'm_ffn2_pre_g': out['m_ffn2_pre_g'], 'm_ffn2_post_g': out['m_ffn2_post_g'], 'm_ffn2_w_gate': out['m_ffn2_w_gate'], 'm_ffn2_w_up': out['m_ffn2_w_up'], 'm_ffn2_w_down': out['m_ffn2_w_down'], 'v_ffn1_pre_g': out['v_ffn1_pre_g'], 'v_ffn1_post_g': out['v_ffn1_post_g'], 'v_ffn1_w_gate': out['v_ffn1_w_gate'], 'v_ffn1_w_up': out['v_ffn1_w_up'], 'v_ffn1_w_down': out['v_ffn1_w_down'], 'v_mix_pre_g': out['v_mix_pre_g'], 'v_mix_post_g': out['v_mix_post_g'], 'v_w_in': out['v_w_in'], 'v_ssd_conv_w': out['v_ssd_conv_w'], 'v_ssd_conv_b': out['v_ssd_conv_b'], 'v_ssd_dt_bias': out['v_ssd_dt_bias'], 'v_ssd_a_log': out['v_ssd_a_log'], 'v_ssd_d': out['v_ssd_d'], 'v_ssd_norm_g': out['v_ssd_norm_g'], 'v_w_branch_a': out['v_w_branch_a'], 'v_s5_lambda_re': out['v_s5_lambda_re'], 'v_s5_lambda_im': out['v_s5_lambda_im'], 'v_s5_b_re': out['v_s5_b_re'], 'v_s5_b_im': out['v_s5_b_im'], 'v_s5_c_re': out['v_s5_c_re'], 'v_s5_c_im': out['v_s5_c_im'], 'v_s5_log_step': out['v_s5_log_step'], 'v_s5_d': out['v_s5_d'], 'v_s5_w_glu': out['v_s5_w_glu'], 'v_w_branch_b': out['v_w_branch_b'], 'v_w_out': out['v_w_out'], 'v_ffn2_pre_g': out['v_ffn2_pre_g'], 'v_ffn2_post_g': out['v_ffn2_post_g'], 'v_ffn2_w_gate': out['v_ffn2_w_gate'], 'v_ffn2_w_up': out['v_ffn2_w_up'], 'v_ffn2_w_down': out['v_ffn2_w_down']}


def _loss(weights, diff, rest, loss_target):
    with _jax.named_scope("forward"):
        args = {**rest, TWIN_DIFF_INPUT: diff, **{k: w.astype(_WEIGHT_DTYPES[k]) for k, w in weights.items()}}
        y = _forward(args)
    with _jax.named_scope("loss_head"):
        err = _jnp.square(y.astype(_jnp.float32) - loss_target)
        return 0.5 * _jnp.sum(_jnp.mean(err, axis=-1)) if err.ndim else 0.5 * err


def _adamw(w, g, m, v):
    m = ADAM_B1 * m + (1.0 - ADAM_B1) * g
    v = ADAM_B2 * v + (1.0 - ADAM_B2) * _jnp.square(g)
    m_hat = m / (1.0 - ADAM_B1 ** ADAM_STEP)
    v_hat = v / (1.0 - ADAM_B2 ** ADAM_STEP)
    delta = -ADAM_LR * (m_hat / (_jnp.sqrt(v_hat) + ADAM_EPS) + ADAM_WD * w)
    return delta, m, v


def reference(x, ffn1_pre_g, ffn1_post_g, ffn1_w_gate, ffn1_w_up, ffn1_w_down, mix_pre_g, mix_post_g, w_in, ssd_conv_w, ssd_conv_b, ssd_dt_bias, ssd_a_log, ssd_d, ssd_norm_g, w_branch_a, s5_lambda_re, s5_lambda_im, s5_b_re, s5_b_im, s5_c_re, s5_c_im, s5_log_step, s5_d, s5_w_glu, w_branch_b, w_out, ffn2_pre_g, ffn2_post_g, ffn2_w_gate, ffn2_w_up, ffn2_w_down, loss_target, m_ffn1_pre_g, m_ffn1_post_g, m_ffn1_w_gate, m_ffn1_w_up, m_ffn1_w_down, m_mix_pre_g, m_mix_post_g, m_w_in, m_ssd_conv_w, m_ssd_conv_b, m_ssd_dt_bias, m_ssd_a_log, m_ssd_d, m_ssd_norm_g, m_w_branch_a, m_s5_lambda_re, m_s5_lambda_im, m_s5_b_re, m_s5_b_im, m_s5_c_re, m_s5_c_im, m_s5_log_step, m_s5_d, m_s5_w_glu, m_w_branch_b, m_w_out, m_ffn2_pre_g, m_ffn2_post_g, m_ffn2_w_gate, m_ffn2_w_up, m_ffn2_w_down, v_ffn1_pre_g, v_ffn1_post_g, v_ffn1_w_gate, v_ffn1_w_up, v_ffn1_w_down, v_mix_pre_g, v_mix_post_g, v_w_in, v_ssd_conv_w, v_ssd_conv_b, v_ssd_dt_bias, v_ssd_a_log, v_ssd_d, v_ssd_norm_g, v_w_branch_a, v_s5_lambda_re, v_s5_lambda_im, v_s5_b_re, v_s5_b_im, v_s5_c_re, v_s5_c_im, v_s5_log_step, v_s5_d, v_s5_w_glu, v_w_branch_b, v_w_out, v_ffn2_pre_g, v_ffn2_post_g, v_ffn2_w_gate, v_ffn2_w_up, v_ffn2_w_down):
    given = dict(x=x, ffn1_pre_g=ffn1_pre_g, ffn1_post_g=ffn1_post_g, ffn1_w_gate=ffn1_w_gate, ffn1_w_up=ffn1_w_up, ffn1_w_down=ffn1_w_down, mix_pre_g=mix_pre_g, mix_post_g=mix_post_g, w_in=w_in, ssd_conv_w=ssd_conv_w, ssd_conv_b=ssd_conv_b, ssd_dt_bias=ssd_dt_bias, ssd_a_log=ssd_a_log, ssd_d=ssd_d, ssd_norm_g=ssd_norm_g, w_branch_a=w_branch_a, s5_lambda_re=s5_lambda_re, s5_lambda_im=s5_lambda_im, s5_b_re=s5_b_re, s5_b_im=s5_b_im, s5_c_re=s5_c_re, s5_c_im=s5_c_im, s5_log_step=s5_log_step, s5_d=s5_d, s5_w_glu=s5_w_glu, w_branch_b=w_branch_b, w_out=w_out, ffn2_pre_g=ffn2_pre_g, ffn2_post_g=ffn2_post_g, ffn2_w_gate=ffn2_w_gate, ffn2_w_up=ffn2_w_up, ffn2_w_down=ffn2_w_down, loss_target=loss_target, m_ffn1_pre_g=m_ffn1_pre_g, m_ffn1_post_g=m_ffn1_post_g, m_ffn1_w_gate=m_ffn1_w_gate, m_ffn1_w_up=m_ffn1_w_up, m_ffn1_w_down=m_ffn1_w_down, m_mix_pre_g=m_mix_pre_g, m_mix_post_g=m_mix_post_g, m_w_in=m_w_in, m_ssd_conv_w=m_ssd_conv_w, m_ssd_conv_b=m_ssd_conv_b, m_ssd_dt_bias=m_ssd_dt_bias, m_ssd_a_log=m_ssd_a_log, m_ssd_d=m_ssd_d, m_ssd_norm_g=m_ssd_norm_g, m_w_branch_a=m_w_branch_a, m_s5_lambda_re=m_s5_lambda_re, m_s5_lambda_im=m_s5_lambda_im, m_s5_b_re=m_s5_b_re, m_s5_b_im=m_s5_b_im, m_s5_c_re=m_s5_c_re, m_s5_c_im=m_s5_c_im, m_s5_log_step=m_s5_log_step, m_s5_d=m_s5_d, m_s5_w_glu=m_s5_w_glu, m_w_branch_b=m_w_branch_b, m_w_out=m_w_out, m_ffn2_pre_g=m_ffn2_pre_g, m_ffn2_post_g=m_ffn2_post_g, m_ffn2_w_gate=m_ffn2_w_gate, m_ffn2_w_up=m_ffn2_w_up, m_ffn2_w_down=m_ffn2_w_down, v_ffn1_pre_g=v_ffn1_pre_g, v_ffn1_post_g=v_ffn1_post_g, v_ffn1_w_gate=v_ffn1_w_gate, v_ffn1_w_up=v_ffn1_w_up, v_ffn1_w_down=v_ffn1_w_down, v_mix_pre_g=v_mix_pre_g, v_mix_post_g=v_mix_post_g, v_w_in=v_w_in, v_ssd_conv_w=v_ssd_conv_w, v_ssd_conv_b=v_ssd_conv_b, v_ssd_dt_bias=v_ssd_dt_bias, v_ssd_a_log=v_ssd_a_log, v_ssd_d=v_ssd_d, v_ssd_norm_g=v_ssd_norm_g, v_w_branch_a=v_w_branch_a, v_s5_lambda_re=v_s5_lambda_re, v_s5_lambda_im=v_s5_lambda_im, v_s5_b_re=v_s5_b_re, v_s5_b_im=v_s5_b_im, v_s5_c_re=v_s5_c_re, v_s5_c_im=v_s5_c_im, v_s5_log_step=v_s5_log_step, v_s5_d=v_s5_d, v_s5_w_glu=v_s5_w_glu, v_w_branch_b=v_w_branch_b, v_w_out=v_w_out, v_ffn2_pre_g=v_ffn2_pre_g, v_ffn2_post_g=v_ffn2_post_g, v_ffn2_w_gate=v_ffn2_w_gate, v_ffn2_w_up=v_ffn2_w_up, v_ffn2_w_down=v_ffn2_w_down)
    weights = {n: given[n] for n in TWIN_WEIGHTS}
    shared = {n: given[n] for n in SHARED_INPUTS}
    per_example = {n: given[n] for n in ['x']}
    grad_fn = _jax.value_and_grad(_loss, argnums=(0, 1))

    def one_microbatch(ex, loss_target):
        ex = dict(ex)
        diff = ex.pop(TWIN_DIFF_INPUT)
        return grad_fn(weights, diff, {**shared, **ex}, loss_target)

    if N_MICROBATCH == 1:
        loss, (grad_w, grad_x) = one_microbatch(per_example, given["loss_target"])
    else:
        def body(carry, xs):
            loss_sum, grad_sum = carry
            l_k, (gw_k, gx_k) = one_microbatch(xs[0], xs[1])
            with _jax.named_scope("update"):
                return (loss_sum + l_k, _jax.tree.map(_jnp.add, grad_sum, gw_k)), gx_k

        init = (_jnp.zeros((), _jnp.float32), _jax.tree.map(_jnp.zeros_like, weights))
        (loss, grad_w), grad_x = _jax.lax.scan(body, init, (per_example, given["loss_target"]))
    with _jax.named_scope("update"):
        delta_w, new_m, new_v = {}, {}, {}
        for n in TWIN_WEIGHTS:
            delta_w[n], new_m[n], new_v[n] = _adamw(weights[n], grad_w[n], given["m_" + n], given["v_" + n])
    return (loss, grad_x, *[grad_w[n] for n in TWIN_WEIGHTS], *[delta_w[n] for n in TWIN_WEIGHTS],
            *[new_m[n] for n in TWIN_WEIGHTS], *[new_v[n] for n in TWIN_WEIGHTS])
```

```python
import functools
import math

import numpy as np
import jax
import jax.numpy as jnp
from jax import lax
from jax.experimental import pallas as pl
from jax.experimental.pallas import tpu as pltpu

F32 = jnp.float32
BF16 = jnp.bfloat16
MESH = pl.DeviceIdType.MESH
HIGHEST = lax.Precision.HIGHEST

D_MODEL = 1024
DEPTH = 2
FFN_HIDDEN = 2816
SSD_D_INNER = 2048
SSD_HEADS = 32
SSD_HEAD_DIM = 64
SSD_GROUPS = 4
SSD_STATE = 128
SSD_CHUNK = 128
SSD_CONV_DIM = 3072
SSD_CONV_WIDTH = 4
S5_WIDTH = 1024
S5_GROUP = 16
S5_GROUPS = 64
S5_STATE = 64
S5_MAX_REAL = -1e-4
S5_BLOCKS = 8
RMS_EPS = 1e-6
N_DEV = 8
LANES = 1024

ADAM_LR = 0.001
ADAM_B1 = 0.9
ADAM_B2 = 0.999
ADAM_EPS = 1e-08
ADAM_WD = 0.01
ADAM_STEP = 10

VMEM_LIMIT_BYTES = 48 * 1024 * 1024

WEIGHTS = ['ffn1_pre_g', 'ffn1_post_g', 'ffn1_w_gate', 'ffn1_w_up', 'ffn1_w_down', 'mix_pre_g', 'mix_post_g',
           'w_in', 'ssd_conv_w', 'ssd_conv_b', 'ssd_dt_bias', 'ssd_a_log', 'ssd_d', 'ssd_norm_g', 'w_branch_a',
           's5_lambda_re', 's5_lambda_im', 's5_b_re', 's5_b_im', 's5_c_re', 's5_c_im', 's5_log_step', 's5_d',
           's5_w_glu', 'w_branch_b', 'w_out', 'ffn2_pre_g', 'ffn2_post_g', 'ffn2_w_gate', 'ffn2_w_up',
           'ffn2_w_down']
SHARDED = {'ffn1_w_gate': 2, 'ffn1_w_up': 2, 'ffn1_w_down': 1, 'w_in': 2, 'ssd_conv_w': 2, 'w_branch_a': 1,
           's5_w_glu': 2, 'w_branch_b': 1, 'w_out': 1, 'ffn2_w_gate': 2, 'ffn2_w_up': 2, 'ffn2_w_down': 1}
SHARDED_ORDER = [n for n in WEIGHTS if n in SHARDED]
SMALL_ORDER = [n for n in WEIGHTS if n not in SHARDED]


def _pcall(body, **kw):
    return pl.pallas_call(body, **kw)


def _params(*sem):
    return pltpu.CompilerParams(dimension_semantics=sem, vmem_limit_bytes=VMEM_LIMIT_BYTES)


def _tile(n, pref, align=128):
    if n <= pref:
        return n
    t = (pref // align) * align
    while t >= align:
        if n % t == 0:
            return t
        t -= align
    return n


def _rms(x, g):
    return x * lax.rsqrt(jnp.mean(x * x, axis=-1, keepdims=True) + RMS_EPS) * g


def _sigmoid(x):
    return 1.0 / (1.0 + jnp.exp(-x))


def _silu(x):
    return x * _sigmoid(x)


def _gelu(x):
    return 0.5 * x * (1.0 + jnp.tanh(math.sqrt(2.0 / math.pi) * (x + 0.044715 * (x * x * x))))


def _softplus(x):
    return jnp.maximum(x, 0.0) + jnp.log(1.0 + jnp.exp(-jnp.abs(x)))


def _dot(a, b, dims):
    return lax.dot_general(a.astype(BF16), b.astype(BF16), (dims, ((), ())), preferred_element_type=F32)


_NN = ((1,), (0,))
_NT = ((1,), (1,))
_TN = ((0,), (0,))


@jax.custom_vjp
def _bdot_nn(a, b):
    return _dot(a, b, _NN)


_bdot_nn.defvjp(lambda a, b: (_dot(a, b, _NN), (a, b)),
                lambda r, g: (_dot(g, r[1], _NT), _dot(r[0], g, _TN)))


@jax.custom_vjp
def _bdot_nt(a, b):
    return _dot(a, b, _NT)


_bdot_nt.defvjp(lambda a, b: (_dot(a, b, _NT), (a, b)),
                lambda r, g: (_dot(g, r[1], _NN), _dot(g, r[0], _TN)))


@jax.custom_vjp
def _bdot_tn(a, b):
    return _dot(a, b, _TN)


_bdot_tn.defvjp(lambda a, b: (_dot(a, b, _TN), (a, b)),
                lambda r, g: (_dot(r[1], g, _NT), _dot(r[0], g, _NN)))


def _fdot(a, b, dims=_NN):
    return lax.dot_general(a, b, (dims, ((), ())), precision=HIGHEST, preferred_element_type=F32)


def _mm(a, b, *, name, ta=False, tb=False, out_dtype=F32, tm=512, tn=512, tk=2048):
    m, k = (a.shape[1], a.shape[0]) if ta else a.shape
    n = b.shape[0] if tb else b.shape[1]
    assert k == (b.shape[1] if tb else b.shape[0]), (a.shape, b.shape, ta, tb)
    tm, tn, tk = _tile(m, tm), _tile(n, tn), _tile(k, tk)
    nk = k // tk
    a_spec = pl.BlockSpec((tk, tm), lambda i, j, kk: (kk, i)) if ta else pl.BlockSpec((tm, tk), lambda i, j, kk: (i, kk))
    b_spec = pl.BlockSpec((tn, tk), lambda i, j, kk: (j, kk)) if tb else pl.BlockSpec((tk, tn), lambda i, j, kk: (kk, j))
    dims = ((0 if ta else 1,), (1 if tb else 0,))

    def body(a_ref, b_ref, o_ref, acc_ref):
        kk = pl.program_id(2)

        @pl.when(kk == 0)
        def _():
            acc_ref[...] = jnp.zeros_like(acc_ref)

        acc_ref[...] += _dot(a_ref[...], b_ref[...], dims)

        @pl.when(kk == nk - 1)
        def _():
            o_ref[...] = acc_ref[...].astype(o_ref.dtype)

    return _pcall(
        body, name=name, grid=(m // tm, n // tn, nk),
        in_specs=[a_spec, b_spec], out_specs=pl.BlockSpec((tm, tn), lambda i, j, kk: (i, j)),
        out_shape=jax.ShapeDtypeStruct((m, n), out_dtype),
        scratch_shapes=[pltpu.VMEM((tm, tn), F32)],
        compiler_params=_params("parallel", "parallel", "arbitrary"),
    )(a, b)


def _rspec(tm, w, cb=0, percol=False):
    return pl.BlockSpec((tm, w), (lambda j, i: (i, cb + j)) if percol else (lambda j, i: (i, cb)))


def _bspec(w, cb=0, percol=False, rows=1):
    return pl.BlockSpec((rows, w), (lambda j, i: (0, cb + j)) if percol else (lambda j, i: (0, cb)))


def _rows(fn, *, name, nrow, ncol=1, ins, outs):
    n_in = len(ins)
    accs = [o[2] for o in outs]

    def body(*refs):
        vals = fn(*[r[...] for r in refs[:n_in]])
        if not isinstance(vals, (tuple, list)):
            vals = (vals,)
        i = pl.program_id(1)
        for ref, val, acc in zip(refs[n_in:], vals, accs):
            if acc:
                @pl.when(i == 0)
                def _(ref=ref):
                    ref[...] = jnp.zeros_like(ref)

                ref[...] += jnp.broadcast_to(val, ref.shape).astype(ref.dtype)
            else:
                ref[...] = val.astype(ref.dtype)

    res = _pcall(
        body, name=name, grid=(ncol, nrow),
        in_specs=[s for _, s in ins], out_specs=[o[1] for o in outs], out_shape=[o[0] for o in outs],
        compiler_params=_params("parallel", "arbitrary"),
    )(*[a for a, _ in ins])
    return res


def _sds(shape, dtype=F32):
    return jax.ShapeDtypeStruct(shape, dtype)


def _rms_fwd(h, g, *, name, tm):
    t, d = h.shape
    return _rows(lambda x, gg: _rms(x, gg), name=name, nrow=t // tm,
                 ins=[(h, _rspec(tm, d)), (g, _bspec(d))],
                 outs=[(_sds((t, d), BF16), _rspec(tm, d), False)])[0]


def _resid_fwd(h, f, g, scale, *, name, tm):
    t, d = h.shape
    return _rows(lambda x, ff, gg: x + scale * _rms(ff, gg), name=name, nrow=t // tm,
                 ins=[(h, _rspec(tm, d)), (f, _rspec(tm, d)), (g, _bspec(d))],
                 outs=[(_sds((t, d)), _rspec(tm, d), False)])[0]


def _resid_bwd(f, g, dh, scale, *, name, tm):
    t, d = f.shape

    def fn(ff, gg, dd):
        _, vjp = jax.vjp(lambda a, b: scale * _rms(a, b), ff, gg)
        return vjp(dd)

    return _rows(fn, name=name, nrow=t // tm,
                 ins=[(f, _rspec(tm, d)), (g, _bspec(d)), (dh, _rspec(tm, d))],
                 outs=[(_sds((t, d), BF16), _rspec(tm, d), False), (_sds((1, d)), _bspec(d), True)])


def _rms_bwd(h, g, dh, dxns, *, name, tm):
    t, d = h.shape

    def fn(x, gg, dd, *dx):
        _, vjp = jax.vjp(_rms, x, gg)
        tot = dx[0]
        for more in dx[1:]:
            tot = tot + more
        dxx, dg = vjp(tot)
        return dd + dxx, dg

    return _rows(fn, name=name, nrow=t // tm,
                 ins=[(h, _rspec(tm, d)), (g, _bspec(d)), (dh, _rspec(tm, d))] + [(x, _rspec(tm, d)) for x in dxns],
                 outs=[(_sds((t, d)), _rspec(tm, d), False), (_sds((1, d)), _bspec(d), True)])


def _ffn_fwd(h, p, tag, tm):
    t = h.shape[0]
    xn = _rms_fwd(h, p['pre_g'], name=f"{tag}_rms", tm=tm)
    ab = _mm(xn, p['w_gu'], name=f"{tag}_up")
    hh = _rows(lambda a, b: _silu(a) * b, name=f"{tag}_act", nrow=t // tm,
               ins=[(ab, _rspec(tm, FFN_HIDDEN, 0)), (ab, _rspec(tm, FFN_HIDDEN, 1))],
               outs=[(_sds((t, FFN_HIDDEN), BF16), _rspec(tm, FFN_HIDDEN), False)])[0]
    f = _mm(hh, p['w_down'], name=f"{tag}_down")
    out = _resid_fwd(h, f, p['post_g'], 0.5, name=f"{tag}_res", tm=tm)
    return out, (h, xn, ab, hh, f)


def _ffn_bwd(dh, p, saved, tag, tm):
    h, xn, ab, hh, f = saved
    t = h.shape[0]
    df, dpost = _resid_bwd(f, p['post_g'], dh, 0.5, name=f"{tag}_res_bwd", tm=tm)
    dhh = _mm(df, p['w_down'], tb=True, name=f"{tag}_down_dx")
    dwd = _mm(hh, df, ta=True, name=f"{tag}_down_dw")

    def act_bwd(a, b, d):
        _, vjp = jax.vjp(lambda aa, bb: _silu(aa) * bb, a, b)
        da, db = vjp(d)
        return jnp.concatenate([da, db], axis=1)

    dab = _rows(act_bwd, name=f"{tag}_act_bwd", nrow=t // tm,
                ins=[(ab, _rspec(tm, FFN_HIDDEN, 0)), (ab, _rspec(tm, FFN_HIDDEN, 1)), (dhh, _rspec(tm, FFN_HIDDEN))],
                outs=[(_sds((t, 2 * FFN_HIDDEN), BF16), _rspec(tm, 2 * FFN_HIDDEN), False)])[0]
    dxn = _mm(dab, p['w_gu'], tb=True, name=f"{tag}_up_dx")
    dwgu = _mm(xn, dab, ta=True, name=f"{tag}_up_dw")
    dh_in, dpre = _rms_bwd(h, p['pre_g'], dh, [dxn], name=f"{tag}_rms_bwd", tm=tm)
    grads = {'pre_g': dpre, 'post_g': dpost, 'w_gate': dwgu[:, :FFN_HIDDEN], 'w_up': dwgu[:, FFN_HIDDEN:],
             'w_down': dwd}
    return dh_in, grads


CONV_COLS = 256


def _shift_down(x, s):
    rows = lax.broadcasted_iota(jnp.int32, x.shape, 0)
    return jnp.where(rows >= s, pltpu.roll(x, s, axis=0), 0.0)


def _shift_up(x, s):
    t = x.shape[0]
    rows = lax.broadcasted_iota(jnp.int32, x.shape, 0)
    return jnp.where(rows < t - s, pltpu.roll(x, t - s, axis=0), 0.0)


def _conv_fwd(proj, col0, w, b, *, name):
    t = proj.shape[0]
    c = w.shape[1]
    cb0 = col0 // CONV_COLS

    def body(x_ref, w_ref, b_ref, o_ref):
        x = x_ref[...]
        acc = x * w_ref[3:4, :] + b_ref[...]
        for k in range(SSD_CONV_WIDTH - 1):
            acc = acc + _shift_down(x, SSD_CONV_WIDTH - 1 - k) * w_ref[k:k + 1, :]
        o_ref[...] = _silu(acc)

    return _pcall(
        body, name=name, grid=(c // CONV_COLS,),
        in_specs=[pl.BlockSpec((t, CONV_COLS), lambda j: (0, cb0 + j)),
                  pl.BlockSpec((SSD_CONV_WIDTH, CONV_COLS), lambda j: (0, j)),
                  pl.BlockSpec((1, CONV_COLS), lambda j: (0, j))],
        out_specs=pl.BlockSpec((t, CONV_COLS), lambda j: (0, j)),
        out_shape=_sds((t, c)), compiler_params=_params("parallel"),
    )(proj, w, b)


def _conv_bwd(proj, col0, w, b, dout, *, name):
    t = proj.shape[0]
    c = w.shape[1]
    cb0 = col0 // CONV_COLS

    def body(x_ref, w_ref, b_ref, d_ref, dx_ref, dw_ref, db_ref):
        x = x_ref[...]
        shifted = [_shift_down(x, SSD_CONV_WIDTH - 1 - k) for k in range(SSD_CONV_WIDTH - 1)] + [x]
        pre = b_ref[...] + shifted[3] * w_ref[3:4, :]
        for k in range(SSD_CONV_WIDTH - 1):
            pre = pre + shifted[k] * w_ref[k:k + 1, :]
        sg = _sigmoid(pre)
        dpre = d_ref[...] * (sg * (1.0 + pre * (1.0 - sg)))
        dx = dpre * w_ref[3:4, :]
        for k in range(SSD_CONV_WIDTH - 1):
            dx = dx + _shift_up(dpre, SSD_CONV_WIDTH - 1 - k) * w_ref[k:k + 1, :]
        dx_ref[...] = dx.astype(dx_ref.dtype)
        for k in range(SSD_CONV_WIDTH):
            dw_ref[k:k + 1, :] = jnp.sum(dpre * shifted[k], axis=0, keepdims=True)
        db_ref[...] = jnp.sum(dpre, axis=0, keepdims=True)

    return _pcall(
        body, name=name, grid=(c // CONV_COLS,),
        in_specs=[pl.BlockSpec((t, CONV_COLS), lambda j: (0, cb0 + j)),
                  pl.BlockSpec((SSD_CONV_WIDTH, CONV_COLS), lambda j: (0, j)),
                  pl.BlockSpec((1, CONV_COLS), lambda j: (0, j)),
                  pl.BlockSpec((t, CONV_COLS), lambda j: (0, j))],
        out_specs=[pl.BlockSpec((t, CONV_COLS), lambda j: (0, j)),
                   pl.BlockSpec((SSD_CONV_WIDTH, CONV_COLS), lambda j: (0, j)),
                   pl.BlockSpec((1, CONV_COLS), lambda j: (0, j))],
        out_shape=[_sds((t, c), BF16), _sds((SSD_CONV_WIDTH, c)), _sds((1, c))],
        compiler_params=_params("parallel"),
    )(proj, w, b, dout)


HALF = 256
HEADS_PER_HALF = 4
PAD_HEADS = 128


def _head_expanders():
    k = lax.broadcasted_iota(jnp.int32, (PAD_HEADS, HALF), 0)
    j = lax.broadcasted_iota(jnp.int32, (PAD_HEADS, HALF), 1)
    kt = lax.broadcasted_iota(jnp.int32, (HALF, PAD_HEADS), 1)
    jt = lax.broadcasted_iota(jnp.int32, (HALF, PAD_HEADS), 0)
    es, ets = [], []
    for half in range(2):
        es.append(jnp.where(k == j // SSD_HEAD_DIM + half * HEADS_PER_HALF, 1.0, 0.0).astype(F32))
        ets.append(jnp.where(kt == jt // SSD_HEAD_DIM + half * HEADS_PER_HALF, 1.0, 0.0).astype(F32))
    return es, ets


def _ssd_chunk(x_lo, x_hi, bm, cm, dtr, dtb8, alog8, dsk8, s_lo, s_hi):
    q = x_lo.shape[0]
    es, ets = _head_expanders()
    rowmean = lambda v: jnp.sum(v, axis=0, keepdims=True) * 0.125
    dt = _softplus(dtr + rowmean(dtb8))
    a = -jnp.exp(rowmean(alog8))
    adt = a * dt
    adt_tot8 = jnp.broadcast_to(jnp.sum(adt, axis=0, keepdims=True), (8, PAD_HEADS))
    ll = lax.broadcasted_iota(jnp.int32, (q, q), 0)
    ss = lax.broadcasted_iota(jnp.int32, (q, q), 1)
    ltri = jnp.where(ll >= ss, 1.0, 0.0).astype(F32)
    lane = lax.broadcasted_iota(jnp.int32, (1, HALF), 1)
    cb = _bdot_nt(cm, bm)
    outs = []
    for half, (x, s_in) in enumerate(((x_lo, s_lo), (x_hi, s_hi))):
        e, et = es[half], ets[half]
        dtf = _fdot(dt, e)
        af = _fdot(adt, e)
        dskf = rowmean(_fdot(dsk8, e))
        acum = _fdot(ltri, af)
        alast = jnp.sum(af, axis=0, keepdims=True)
        xdt = x * dtf
        ydiag = jnp.zeros((q, HALF), F32)
        for r in range(HEADS_PER_HALF):
            sel = lane == r * SSD_HEAD_DIM
            ac_r = jnp.sum(jnp.where(sel, acum, 0.0), axis=1, keepdims=True)
            a_r = jnp.sum(jnp.where(sel, af, 0.0), axis=1, keepdims=True)
            arow = jnp.sum(jnp.where(ll <= ss, a_r, 0.0), axis=0, keepdims=True)
            decay = jnp.exp(jnp.where(ll >= ss, ac_r - arow, -jnp.inf))
            yh = _bdot_nn(cb * decay, xdt)
            ydiag = ydiag + jnp.where(lane // SSD_HEAD_DIM == r, yh, 0.0)
        st = _bdot_tn(xdt * jnp.exp(alast - acum), bm)
        yoff = _bdot_nt(cm, s_in) * jnp.exp(acum)
        y = ydiag + yoff + dskf * x
        alast_col = jnp.sum(_fdot(et, adt_tot8, _NT), axis=1, keepdims=True) * 0.125
        outs.append((y, jnp.exp(alast_col) * s_in + st))
    return outs[0][0], outs[1][0], outs[0][1], outs[1][1]


def _ssd_specs(t, rev):
    q = SSD_CHUNK
    nc = t // q
    ci = (lambda c: nc - 1 - c) if rev else (lambda c: c)
    xcol0 = SSD_D_INNER // SSD_STATE
    return dict(
        x_lo=pl.BlockSpec((q, HALF), lambda g, c: (ci(c), 2 * g)),
        x_hi=pl.BlockSpec((q, HALF), lambda g, c: (ci(c), 2 * g + 1)),
        bm=pl.BlockSpec((q, SSD_STATE), lambda g, c: (ci(c), xcol0 + g)),
        cm=pl.BlockSpec((q, SSD_STATE), lambda g, c: (ci(c), xcol0 + SSD_GROUPS + g)),
        dt=pl.BlockSpec((None, q, PAD_HEADS), lambda g, c: (g, ci(c), 0)),
        par=pl.BlockSpec((None, 8, PAD_HEADS), lambda g, c: (g, 0, 0)),
        st=pl.BlockSpec((None, None, 2, HALF, SSD_STATE), lambda g, c: (ci(c), g, 0, 0, 0)),
        y=pl.BlockSpec((q, 2 * HALF), lambda g, c: (ci(c), g)),
        grp=pl.BlockSpec((q, SSD_STATE), lambda g, c: (ci(c), g)),
    )


def _ssd_fwd(xc, dt4, dtb, alog, dsk, *, name):
    t = xc.shape[0]
    nc = t // SSD_CHUNK
    sp = _ssd_specs(t, False)

    def body(xl, xh, bm, cm, dt, p0, p1, p2, y_ref, sin_ref, st_ref):
        @pl.when(pl.program_id(1) == 0)
        def _():
            st_ref[...] = jnp.zeros_like(st_ref)

        sin_ref[...] = st_ref[...]
        y_lo, y_hi, so_lo, so_hi = _ssd_chunk(xl[...], xh[...], bm[...], cm[...], dt[...], p0[...], p1[...],
                                              p2[...], st_ref[0], st_ref[1])
        y_ref[:, :HALF] = y_lo
        y_ref[:, HALF:] = y_hi
        st_ref[0] = so_lo
        st_ref[1] = so_hi

    return _pcall(
        body, name=name, grid=(SSD_GROUPS, nc),
        in_specs=[sp['x_lo'], sp['x_hi'], sp['bm'], sp['cm'], sp['dt'], sp['par'], sp['par'], sp['par']],
        out_specs=[sp['y'], sp['st']],
        out_shape=[_sds((t, SSD_D_INNER)), _sds((nc, SSD_GROUPS, 2, HALF, SSD_STATE))],
        scratch_shapes=[pltpu.VMEM((2, HALF, SSD_STATE), F32)],
        compiler_params=_params("parallel", "arbitrary"),
    )(xc, xc, xc, xc, dt4, dtb, alog, dsk)


def _ssd_bwd(xc, dt4, dtb, alog, dsk, sin, dy, *, name):
    t = xc.shape[0]
    nc = t // SSD_CHUNK
    sp = _ssd_specs(t, True)

    def body(xl, xh, bm, cm, dt, p0, p1, p2, sin_ref, dy_ref,
             dx_ref, db_ref, dc_ref, ddt_ref, dp0, dp1, dp2, dst_ref):
        first = pl.program_id(1) == 0

        @pl.when(first)
        def _():
            dst_ref[...] = jnp.zeros_like(dst_ref)

        _, vjp = jax.vjp(_ssd_chunk, xl[...], xh[...], bm[...], cm[...], dt[...], p0[...], p1[...], p2[...],
                         sin_ref[0], sin_ref[1])
        dxl, dxh, dbm, dcm, ddt, g0, g1, g2, ds_lo, ds_hi = vjp(
            (dy_ref[:, :HALF], dy_ref[:, HALF:], dst_ref[0], dst_ref[1]))
        dx_ref[:, :HALF] = dxl
        dx_ref[:, HALF:] = dxh
        db_ref[...] = dbm
        dc_ref[...] = dcm
        ddt_ref[...] = ddt
        dst_ref[0] = ds_lo
        dst_ref[1] = ds_hi
        for ref, g in ((dp0, g0), (dp1, g1), (dp2, g2)):
            tot = jnp.broadcast_to(jnp.sum(g, axis=0, keepdims=True), ref.shape)

            @pl.when(first)
            def _(ref=ref):
                ref[...] = jnp.zeros_like(ref)

            ref[...] += tot

    return _pcall(
        body, name=name, grid=(SSD_GROUPS, nc),
        in_specs=[sp['x_lo'], sp['x_hi'], sp['bm'], sp['cm'], sp['dt'], sp['par'], sp['par'], sp['par'],
                  sp['st'], sp['y']],
        out_specs=[sp['y'], sp['grp'], sp['grp'], sp['dt'], sp['par'], sp['par'], sp['par']],
        out_shape=[_sds((t, SSD_D_INNER)), _sds((t, SSD_GROUPS * SSD_STATE)), _sds((t, SSD_GROUPS * SSD_STATE)),
                   _sds((SSD_GROUPS, t, PAD_HEADS))] + [_sds((SSD_GROUPS, 8, PAD_HEADS))] * 3,
        scratch_shapes=[pltpu.VMEM((2, HALF, SSD_STATE), F32)],
        compiler_params=_params("parallel", "arbitrary"),
    )(xc, xc, xc, xc, dt4, dtb, alog, dsk, sin, dy)


def _gatenorm(y, z, g):
    v = y * _silu(z)
    return v * lax.rsqrt(jnp.mean(v * v, axis=-1, keepdims=True) + RMS_EPS) * g


def _gmm(terms, *, trans_w, add=None, name, out_dtype=F32, tm=512):
    t = terms[0][0].shape[0]
    tm = _tile(t, tm)
    _, kw, nw = terms[0][1].shape
    wi, wo = (nw, kw) if trans_w else (kw, nw)
    signs = [s for _, _, s in terms]
    n = len(terms)

    def body(*refs):
        acc = None
        for i in range(n):
            prod = _dot(refs[2 * i][...], refs[2 * i + 1][...], _NT if trans_w else _NN)
            prod = prod if signs[i] > 0 else -prod
            acc = prod if acc is None else acc + prod
        o_ref = refs[-1]
        if add is not None:
            acc = acc + refs[2 * n][...]
        o_ref[...] = acc.astype(o_ref.dtype)

    in_specs, args = [], []
    for x, w, _ in terms:
        in_specs += [pl.BlockSpec((tm, wi), lambda i, gb: (i, gb)), pl.BlockSpec((None, kw, nw), lambda i, gb: (gb, 0, 0))]
        args += [x, w]
    if add is not None:
        in_specs.append(pl.BlockSpec((tm, wo), lambda i, gb: (i, gb)))
        args.append(add)
    return _pcall(
        body, name=name, grid=(t // tm, S5_BLOCKS), in_specs=in_specs,
        out_specs=pl.BlockSpec((tm, wo), lambda i, gb: (i, gb)), out_shape=_sds((t, S5_BLOCKS * wo), out_dtype),
        compiler_params=_params("parallel", "parallel"),
    )(*args)


def _gmm_tn(x, dy, sign, *, name, tm=512):
    t = x.shape[0]
    tm = _tile(t, tm)
    kw, nw = x.shape[1] // S5_BLOCKS, dy.shape[1] // S5_BLOCKS

    def body(x_ref, d_ref, o_ref):
        @pl.when(pl.program_id(1) == 0)
        def _():
            o_ref[...] = jnp.zeros_like(o_ref)

        prod = _dot(x_ref[...], d_ref[...], _TN)
        o_ref[...] += prod if sign > 0 else -prod

    return _pcall(
        body, name=name, grid=(S5_BLOCKS, t // tm),
        in_specs=[pl.BlockSpec((tm, kw), lambda gb, i: (i, gb)), pl.BlockSpec((tm, nw), lambda gb, i: (i, gb))],
        out_specs=pl.BlockSpec((None, kw, nw), lambda gb, i: (gb, 0, 0)), out_shape=_sds((S5_BLOCKS, kw, nw)),
        compiler_params=_params("parallel", "arbitrary"),
    )(x, dy)


SCAN_COLS = 256
SCAN_UNROLL = 8


def _cmul(ar, ai, br, bi):
    return ar * br - ai * bi, ar * bi + ai * br


def _segment_power(ar, ai, n):
    assert n & (n - 1) == 0
    for _ in range(n.bit_length() - 1):
        ar, ai = _cmul(ar, ai, ar, ai)
    return ar, ai


def _carry_in(fr, fi, pr, pi, reverse):
    rows = lax.broadcasted_iota(jnp.int32, fr.shape, 0)
    cr = jnp.zeros_like(fr[0:1])
    ci = jnp.zeros_like(cr)
    outr = jnp.zeros_like(fr)
    outi = jnp.zeros_like(fr)
    order = range(6, -1, -1) if reverse else range(1, 8)
    for j in order:
        src = j + 1 if reverse else j - 1
        nr, ni = _cmul(pr[0:1], pi[0:1], cr, ci)
        cr, ci = nr + fr[src:src + 1], ni + fi[src:src + 1]
        outr = jnp.where(rows == j, cr, outr)
        outi = jnp.where(rows == j, ci, outi)
    return outr, outi


def _scan_fwd(bur, bui, lr, li, *, name):
    t, w = bur.shape
    nrt = t // 8

    def body(br_ref, bi_ref, lr_ref, li_ref, sr_ref, si_ref):
        ar = jnp.broadcast_to(lr_ref[...], (8, SCAN_COLS))
        ai = jnp.broadcast_to(li_ref[...], (8, SCAN_COLS))

        def step(r, s, store):
            rows = pl.ds(pl.multiple_of(r * 8, 8), 8)
            nr, ni = _cmul(ar, ai, s[0], s[1])
            nr, ni = nr + br_ref[rows, :], ni + bi_ref[rows, :]
            if store:
                sr_ref[rows, :] = nr
                si_ref[rows, :] = ni
            return nr, ni

        zero = (jnp.zeros((8, SCAN_COLS), F32), jnp.zeros((8, SCAN_COLS), F32))
        fr, fi = lax.fori_loop(0, nrt, lambda r, s: step(r, s, False), zero, unroll=SCAN_UNROLL)
        pr, pi = _segment_power(ar, ai, nrt)
        init = _carry_in(fr, fi, pr, pi, False)
        lax.fori_loop(0, nrt, lambda r, s: step(r, s, True), init, unroll=SCAN_UNROLL)

    col = pl.BlockSpec((t, SCAN_COLS), lambda j: (0, j))
    row = pl.BlockSpec((1, SCAN_COLS), lambda j: (0, j))
    return _pcall(
        body, name=name, grid=(w // SCAN_COLS,), in_specs=[col, col, row, row], out_specs=[col, col],
        out_shape=[_sds((t, w)), _sds((t, w))], compiler_params=_params("parallel"),
    )(bur, bui, lr, li)


def _scan_bwd(dr, di, sr, si, lr, li, *, name):
    t, w = dr.shape
    nrt = t // 8

    def body(dr_ref, di_ref, sr_ref, si_ref, lr_ref, li_ref, gr_ref, gi_ref, dlr_ref, dli_ref):
        ar = jnp.broadcast_to(lr_ref[...], (8, SCAN_COLS))
        ai = -jnp.broadcast_to(li_ref[...], (8, SCAN_COLS))
        zero = jnp.zeros((8, SCAN_COLS), F32)

        def step1(k, g):
            rows = pl.ds(pl.multiple_of((nrt - 1 - k) * 8, 8), 8)
            nr, ni = _cmul(ar, ai, g[0], g[1])
            return nr + dr_ref[rows, :], ni + di_ref[rows, :]

        fr, fi = lax.fori_loop(0, nrt, step1, (zero, zero), unroll=SCAN_UNROLL)
        pr, pi = _segment_power(ar, ai, nrt)
        init = _carry_in(fr, fi, pr, pi, True)

        def step2(k, carry):
            gr, gi, accr, acci = carry
            r = nrt - 1 - k
            rows = pl.ds(pl.multiple_of(r * 8, 8), 8)
            prev = pl.ds(pl.multiple_of(jnp.maximum(r - 1, 0) * 8, 8), 8)
            nr, ni = _cmul(ar, ai, gr, gi)
            nr, ni = nr + dr_ref[rows, :], ni + di_ref[rows, :]
            gr_ref[rows, :] = nr
            gi_ref[rows, :] = ni
            keep = jnp.where(r > 0, 1.0, 0.0)
            pr_, pi_ = sr_ref[prev, :] * keep, si_ref[prev, :] * keep
            return nr, ni, accr + (pr_ * nr + pi_ * ni), acci + (pr_ * ni - pi_ * nr)

        _, _, accr, acci = lax.fori_loop(0, nrt, step2, (init[0], init[1], zero, zero), unroll=SCAN_UNROLL)
        last = pl.ds((nrt - 1) * 8, 8)
        pr_, pi_ = _shift_down(sr_ref[last, :], 1), _shift_down(si_ref[last, :], 1)
        g0r, g0i = gr_ref[0:8, :], gi_ref[0:8, :]
        accr = accr + (pr_ * g0r + pi_ * g0i)
        acci = acci + (pr_ * g0i - pi_ * g0r)
        dlr_ref[...] = jnp.sum(accr, axis=0, keepdims=True)
        dli_ref[...] = jnp.sum(acci, axis=0, keepdims=True)

    col = pl.BlockSpec((t, SCAN_COLS), lambda j: (0, j))
    row = pl.BlockSpec((1, SCAN_COLS), lambda j: (0, j))
    return _pcall(
        body, name=name, grid=(w // SCAN_COLS,), in_specs=[col, col, col, col, row, row],
        out_specs=[col, col, row, row], out_shape=[_sds((t, w)), _sds((t, w)), _sds((1, w)), _sds((1, w))],
        compiler_params=_params("parallel"),
    )(dr, di, sr, si, lr, li)


def _s5_expander():
    n = lax.broadcasted_iota(jnp.int32, (S5_STATE, S5_STATE * S5_GROUP), 0)
    j = lax.broadcasted_iota(jnp.int32, (S5_STATE, S5_STATE * S5_GROUP), 1)
    return jnp.where(n == j // S5_GROUP, 1.0, 0.0).astype(F32)


def _s5_discretise(lam_re, lam_im, log_step, b_re, b_im):
    lr = jnp.minimum(lam_re, S5_MAX_REAL)
    step = jnp.exp(log_step)
    mag = jnp.exp(lr * step)
    ang = lam_im * step
    lbr, lbi = mag * jnp.cos(ang), mag * jnp.sin(ang)
    p, q = lbr - 1.0, lbi
    den = lr * lr + lam_im * lam_im
    cr, ci = (p * lr + q * lam_im) / den, (q * lr - p * lam_im) / den
    e = _s5_expander()
    cre, cie = _fdot(cr, e), _fdot(ci, e)
    return lbr, lbi, cre * b_re - cie * b_im, cre * b_im + cie * b_re


def _s5_params_fwd(lam_re, lam_im, log_step, b_re, b_im, *, name):
    g, n, w = S5_GROUPS, S5_STATE, S5_STATE * S5_GROUP

    def body(a, b, c, d, e, o0, o1, o2, o3):
        for ref, val in zip((o0, o1, o2, o3), _s5_discretise(a[...], b[...], c[...], d[...], e[...])):
            ref[...] = val

    return _pcall(body, name=name, out_shape=[_sds((g, n)), _sds((g, n)), _sds((g, w)), _sds((g, w))])(
        lam_re, lam_im, log_step, b_re, b_im)


def _s5_params_bwd(lam_re, lam_im, log_step, b_re, b_im, cts, *, name):
    g, n, w = S5_GROUPS, S5_STATE, S5_STATE * S5_GROUP

    def body(a, b, c, d, e, c0, c1, c2, c3, o0, o1, o2, o3, o4):
        _, vjp = jax.vjp(_s5_discretise, a[...], b[...], c[...], d[...], e[...])
        for ref, val in zip((o0, o1, o2, o3, o4), vjp((c0[...], c1[...], c2[...], c3[...]))):
            ref[...] = val

    return _pcall(body, name=name,
                  out_shape=[_sds((g, n)), _sds((g, n)), _sds((g, 1)), _sds((g, w)), _sds((g, w))])(
        lam_re, lam_im, log_step, b_re, b_im, *cts)


def _perm(a):
    t, c = a.shape
    return a.reshape(8, t // 8, c).transpose(1, 0, 2).reshape(t, c)


def _unperm(a):
    t, c = a.shape
    return a.reshape(t // 8, 8, c).transpose(1, 0, 2).reshape(t, c)


def _blockdiag(m, rows_inner, cols_inner):
    m = m.reshape(S5_BLOCKS, 8, rows_inner, cols_inner)
    eye = jnp.eye(8, dtype=m.dtype)
    out = m[:, :, :, None, :] * eye[None, :, None, :, None]
    return out.reshape(S5_BLOCKS, 8 * rows_inner, 8 * cols_inner)


def _blockdiag_extract(m, rows_inner, cols_inner):
    m = m.reshape(S5_BLOCKS, 8, rows_inner, 8, cols_inner)
    d = jnp.diagonal(m, axis1=1, axis2=3)
    return d.transpose(0, 3, 1, 2).reshape(S5_GROUPS, rows_inner, cols_inner)


Z0, XBC0, GA0, GB0 = 0, SSD_D_INNER, SSD_D_INNER + SSD_CONV_DIM, SSD_D_INNER + SSD_CONV_DIM + D_MODEL
BIG = GB0 + D_MODEL


def _mixer_fwd(h, p, tm):
    t = h.shape[0]
    nrow = t // tm
    u = _rms_fwd(h, p['pre_g'], name="mix_rms", tm=tm)
    u_p = _perm(u)
    proj = _mm(u, p['w_big'], name="mix_in")
    dtr = _mm(u, p['w_dt'], name="mix_in_dt")
    u5 = _mm(u_p, p['w_u5'], name="mix_in_s5")
    xc = _conv_fwd(proj, XBC0, p['conv_w'], p['conv_b'], name="ssd_conv")
    dt4 = jnp.pad(dtr.reshape(t, SSD_GROUPS, 8).transpose(1, 0, 2), ((0, 0), (0, 0), (0, PAD_HEADS - 8)))
    y_ssd, s_in = _ssd_fwd(xc, dt4, p['dt_bias8'], p['a_log8'], p['d8'], name="ssd_scan")
    gw = SSD_D_INNER // SSD_GROUPS
    ya = _rows(_gatenorm, name="ssd_gate", nrow=nrow, ncol=SSD_GROUPS,
               ins=[(y_ssd, _rspec(tm, gw, 0, True)), (proj, _rspec(tm, gw, Z0 // gw, True)),
                    (p['norm_g'], _bspec(gw, 0, True))],
               outs=[(_sds((t, SSD_D_INNER), BF16), _rspec(tm, gw, 0, True), False)])[0]
    y_a = _mm(ya, p['w_a'], name="mix_a")
    lbr, lbi, bbr, bbi = _s5_params_fwd(p['lam_re'], p['lam_im'], p['log_step'], p['b_re'], p['b_im'], name="s5_par")
    bd = lambda m: _blockdiag(m.reshape(S5_GROUPS, S5_STATE, S5_GROUP).transpose(0, 2, 1), S5_GROUP, S5_STATE).astype(BF16)
    bre, bim = bd(bbr), bd(bbi)
    cre = _blockdiag(p['c_re'].transpose(0, 2, 1), S5_STATE, S5_GROUP).astype(BF16)
    cim = _blockdiag(p['c_im'].transpose(0, 2, 1), S5_STATE, S5_GROUP).astype(BF16)
    lr, li = lbr.reshape(1, -1), lbi.reshape(1, -1)
    bur = _gmm([(u5, bre, 1)], trans_w=False, name="s5_bu_re")
    bui = _gmm([(u5, bim, 1)], trans_w=False, name="s5_bu_im")
    sr, si = _scan_fwd(bur, bui, lr, li, name="s5_scan")
    y5 = _gmm([(sr, cre, 1), (si, cim, -1)], trans_w=False, name="s5_out")
    y5g = _rows(lambda a, b, d: _gelu(a + d * b), name="s5_act", nrow=nrow,
                ins=[(y5, _rspec(tm, S5_WIDTH)), (u5, _rspec(tm, S5_WIDTH)), (p['s5_d'], _bspec(S5_WIDTH))],
                outs=[(_sds((t, S5_WIDTH), BF16), _rspec(tm, S5_WIDTH), False)])[0]
    vg = _mm(y5g, p['w_glu'], name="s5_glu")
    ybin = _rows(lambda a, b: a * _sigmoid(b), name="s5_glu_act", nrow=nrow,
                 ins=[(vg, _rspec(tm, S5_WIDTH, 0)), (vg, _rspec(tm, S5_WIDTH, 1))],
                 outs=[(_sds((t, S5_WIDTH), BF16), _rspec(tm, S5_WIDTH), False)])[0]
    y_b = _unperm(_mm(ybin, p['w_b'], name="mix_b"))
    merged = _rows(lambda ga, gb, a, b: _sigmoid(ga) * a + _sigmoid(gb) * b, name="mix_merge", nrow=nrow,
                   ins=[(proj, _rspec(tm, D_MODEL, GA0 // D_MODEL)), (proj, _rspec(tm, D_MODEL, GB0 // D_MODEL)),
                        (y_a, _rspec(tm, D_MODEL)), (y_b, _rspec(tm, D_MODEL))],
                   outs=[(_sds((t, D_MODEL), BF16), _rspec(tm, D_MODEL), False)])[0]
    m = _mm(merged, p['w_out'], name="mix_out")
    out = _resid_fwd(h, m, p['post_g'], 1.0, name="mix_res", tm=tm)
    saved = dict(h=h, u=u, u_p=u_p, proj=proj, u5=u5, xc=xc, dt4=dt4, s_in=s_in, y_ssd=y_ssd, ya=ya, y_a=y_a,
                 bre=bre, bim=bim, cre=cre, cim=cim, lr=lr, li=li, sr=sr, si=si, y5=y5, y5g=y5g, vg=vg, ybin=ybin,
                 y_b=y_b, merged=merged, m=m)
    return out, saved


def _mixer_bwd(dh, p, s, tm):
    t = dh.shape[0]
    nrow = t // tm
    proj = s['proj']
    dm, dpost = _resid_bwd(s['m'], p['post_g'], dh, 1.0, name="mix_res_bwd", tm=tm)
    dmerged = _mm(dm, p['w_out'], tb=True, name="mix_out_dx")
    dw_out = _mm(s['merged'], dm, ta=True, name="mix_out_dw")

    def merge_bwd(ga, gb, a, b, d):
        _, vjp = jax.vjp(lambda ga_, gb_, a_, b_: _sigmoid(ga_) * a_ + _sigmoid(gb_) * b_, ga, gb, a, b)
        dga, dgb, da, db = vjp(d)
        return jnp.concatenate([dga, dgb], axis=1), da, db

    dgab, dy_a, dy_b = _rows(
        merge_bwd, name="mix_merge_bwd", nrow=nrow,
        ins=[(proj, _rspec(tm, D_MODEL, GA0 // D_MODEL)), (proj, _rspec(tm, D_MODEL, GB0 // D_MODEL)),
             (s['y_a'], _rspec(tm, D_MODEL)), (s['y_b'], _rspec(tm, D_MODEL)), (dmerged, _rspec(tm, D_MODEL))],
        outs=[(_sds((t, 2 * D_MODEL), BF16), _rspec(tm, 2 * D_MODEL), False),
              (_sds((t, D_MODEL), BF16), _rspec(tm, D_MODEL), False),
              (_sds((t, D_MODEL), BF16), _rspec(tm, D_MODEL), False)])
    dya = _mm(dy_a, p['w_a'], tb=True, name="mix_a_dx")
    dw_a = _mm(s['ya'], dy_a, ta=True, name="mix_a_dw")
    gw = SSD_D_INNER // SSD_GROUPS

    def gate_bwd(y, z, g, d):
        _, vjp = jax.vjp(_gatenorm, y, z, g)
        return vjp(d)

    dy_ssd, dz, dnorm = _rows(
        gate_bwd, name="ssd_gate_bwd", nrow=nrow, ncol=SSD_GROUPS,
        ins=[(s['y_ssd'], _rspec(tm, gw, 0, True)), (proj, _rspec(tm, gw, Z0 // gw, True)),
             (p['norm_g'], _bspec(gw, 0, True)), (dya, _rspec(tm, gw, 0, True))],
        outs=[(_sds((t, SSD_D_INNER)), _rspec(tm, gw, 0, True), False),
              (_sds((t, SSD_D_INNER), BF16), _rspec(tm, gw, 0, True), False),
              (_sds((1, SSD_D_INNER)), _bspec(gw, 0, True), True)])
    dxs, dbm, dcm, ddt4, ddtb, dalog, ddsk = _ssd_bwd(s['xc'], s['dt4'], p['dt_bias8'], p['a_log8'], p['d8'],
                                                      s['s_in'], dy_ssd, name="ssd_scan_bwd")
    dxc = jnp.concatenate([dxs, dbm, dcm], axis=1)
    dxbc, dconv_w, dconv_b = _conv_bwd(proj, XBC0, p['conv_w'], p['conv_b'], dxc, name="ssd_conv_bwd")
    ddtr = ddt4[:, :, :8].transpose(1, 0, 2).reshape(t, SSD_HEADS)
    dy_bp = _perm(dy_b)
    dybin = _mm(dy_bp, p['w_b'], tb=True, name="mix_b_dx")
    dw_b = _mm(s['ybin'], dy_bp, ta=True, name="mix_b_dw")

    def glu_bwd(a, b, d):
        _, vjp = jax.vjp(lambda a_, b_: a_ * _sigmoid(b_), a, b)
        da, db = vjp(d)
        return jnp.concatenate([da, db], axis=1)

    dvg = _rows(glu_bwd, name="s5_glu_act_bwd", nrow=nrow,
                ins=[(s['vg'], _rspec(tm, S5_WIDTH, 0)), (s['vg'], _rspec(tm, S5_WIDTH, 1)), (dybin, _rspec(tm, S5_WIDTH))],
                outs=[(_sds((t, 2 * S5_WIDTH), BF16), _rspec(tm, 2 * S5_WIDTH), False)])[0]
    dy5g = _mm(dvg, p['w_glu'], tb=True, name="s5_glu_dx")
    dw_glu = _mm(s['y5g'], dvg, ta=True, name="s5_glu_dw")

    def act_bwd(a, b, d, g):
        _, vjp = jax.vjp(lambda a_, b_, d_: _gelu(a_ + d_ * b_), a, b, d)
        return vjp(g)

    dy5, du5_direct, ds5d = _rows(
        act_bwd, name="s5_act_bwd", nrow=nrow,
        ins=[(s['y5'], _rspec(tm, S5_WIDTH)), (s['u5'], _rspec(tm, S5_WIDTH)), (p['s5_d'], _bspec(S5_WIDTH)),
             (dy5g, _rspec(tm, S5_WIDTH))],
        outs=[(_sds((t, S5_WIDTH), BF16), _rspec(tm, S5_WIDTH), False), (_sds((t, S5_WIDTH)), _rspec(tm, S5_WIDTH), False),
              (_sds((1, S5_WIDTH)), _bspec(S5_WIDTH), True)])
    dsdr = _gmm([(dy5, s['cre'], 1)], trans_w=True, name="s5_out_dx_re")
    dsdi = _gmm([(dy5, s['cim'], -1)], trans_w=True, name="s5_out_dx_im")
    dcre = _gmm_tn(s['sr'], dy5, 1, name="s5_out_dw_re")
    dcim = _gmm_tn(s['si'], dy5, -1, name="s5_out_dw_im")
    gr, gi, dlr, dli = _scan_bwd(dsdr, dsdi, s['sr'], s['si'], s['lr'], s['li'], name="s5_scan_bwd")
    dbre = _gmm_tn(s['u5'], gr, 1, name="s5_bu_dw_re")
    dbim = _gmm_tn(s['u5'], gi, 1, name="s5_bu_dw_im")
    du5 = _gmm([(gr, s['bre'], 1), (gi, s['bim'], 1)], trans_w=True, add=du5_direct, name="s5_bu_dx", out_dtype=BF16)
    du_p = _mm(du5, p['w_u5'], tb=True, name="mix_in_s5_dx")
    dw_u5 = _mm(s['u_p'], du5, ta=True, name="mix_in_s5_dw")
    ext_b = lambda m: _blockdiag_extract(m, S5_GROUP, S5_STATE).transpose(0, 2, 1).reshape(S5_GROUPS, S5_STATE * S5_GROUP)
    dlam_re, dlam_im, dlog_step, db_re, db_im = _s5_params_bwd(
        p['lam_re'], p['lam_im'], p['log_step'], p['b_re'], p['b_im'],
        (dlr.reshape(S5_GROUPS, S5_STATE), dli.reshape(S5_GROUPS, S5_STATE), ext_b(dbre), ext_b(dbim)), name="s5_par_bwd")
    dc_re = _blockdiag_extract(dcre, S5_STATE, S5_GROUP).transpose(0, 2, 1)
    dc_im = _blockdiag_extract(dcim, S5_STATE, S5_GROUP).transpose(0, 2, 1)
    dproj = jnp.concatenate([dz, dxbc, dgab], axis=1)
    du_big = _mm(dproj, p['w_big'], tb=True, name="mix_in_dx")
    du_dt = _mm(ddtr, p['w_dt'], tb=True, name="mix_in_dt_dx")
    dw_big = _mm(s['u'], dproj, ta=True, name="mix_in_dw")
    dw_dt = _mm(s['u'], ddtr, ta=True, name="mix_in_dt_dw")
    dh_in, dpre = _rms_bwd(s['h'], p['pre_g'], dh, [du_big, du_dt, _unperm(du_p)], name="mix_rms_bwd", tm=tm)
    dw_in = jnp.concatenate([dw_big[:, :GA0], dw_dt, dw_u5, dw_big[:, GA0:]], axis=1)
    grads = {
        'mix_pre_g': dpre, 'mix_post_g': dpost, 'w_in': dw_in, 'ssd_conv_w': dconv_w, 'ssd_conv_b': dconv_b,
        'ssd_dt_bias': ddtb[:, 0, :8].reshape(-1), 'ssd_a_log': dalog[:, 0, :8].reshape(-1),
        'ssd_d': ddsk[:, 0, :8].reshape(-1), 'ssd_norm_g': dnorm, 'w_branch_a': dw_a,
        's5_lambda_re': dlam_re, 's5_lambda_im': dlam_im,
        's5_b_re': db_re.reshape(S5_GROUPS, S5_STATE, S5_GROUP), 's5_b_im': db_im.reshape(S5_GROUPS, S5_STATE, S5_GROUP),
        's5_c_re': dc_re, 's5_c_im': dc_im, 's5_log_step': dlog_step.reshape(-1), 's5_d': ds5d,
        's5_w_glu': dw_glu, 'w_branch_b': dw_b, 'w_out': dw_out,
    }
    return dh_in, grads


HBM_SPEC = pl.BlockSpec(memory_space=pltpu.HBM)


def _place():
    return lax.axis_index("x"), lax.axis_index("y"), lax.axis_index("c")


def _all_gather(shard, *, name):
    r, l = shard.shape

    def body(x_ref, out_ref, send_sems, recv_sems, local_sem):
        x, y, c = _place()
        me, sibling = (x, y, c), (x, y, 1 - c)
        chips = [(1 - x, y), (x, 1 - y), (1 - x, 1 - y)]

        def slot(px, py, pc):
            return out_ref.at[4 * px + 2 * py + pc]

        def copy(k, block, to, src=None):
            return pltpu.make_async_remote_copy(
                src_ref=slot(*block) if src is None else src, dst_ref=slot(*block),
                send_sem=send_sems.at[k], recv_sem=recv_sems.at[k], device_id=to, device_id_type=MESH)

        mine = pltpu.make_async_copy(x_ref, slot(*me), local_sem)
        mine.start()
        first = [copy(0, me, sibling, src=x_ref)]
        first += [copy(1 + j, me, (*chip, c), src=x_ref) for j, chip in enumerate(chips)]
        for cp in first:
            cp.start()
        passed = [copy(4 + j, (*chip, c), sibling) for j, chip in enumerate(chips)]
        for j, chip in enumerate(chips):
            copy(1 + j, (*chip, c), me).wait_recv()
            passed[j].start()
        copy(0, sibling, me).wait_recv()
        for j, chip in enumerate(chips):
            copy(4 + j, (*chip, 1 - c), me).wait_recv()
        for cp in first + passed:
            cp.wait_send()
        mine.wait()

    return _pcall(
        body, name=name, in_specs=[HBM_SPEC], out_specs=HBM_SPEC,
        out_shape=jax.ShapeDtypeStruct((N_DEV, r, l), shard.dtype),
        scratch_shapes=[pltpu.SemaphoreType.DMA((7,)), pltpu.SemaphoreType.DMA((7,)), pltpu.SemaphoreType.DMA],
    )(shard)


def _exchange_sibling(packed, *, name):
    _, nchip, r, l = packed.shape

    def body(p_ref, q_ref, send_sem, recv_sem):
        x, y, c = _place()
        cp = pltpu.make_async_remote_copy(src_ref=p_ref.at[1 - c], dst_ref=q_ref, send_sem=send_sem, recv_sem=recv_sem,
                                          device_id=(x, y, 1 - c), device_id_type=MESH)
        cp.start()
        cp.wait()

    return _pcall(
        body, name=name, in_specs=[HBM_SPEC], out_specs=HBM_SPEC,
        out_shape=jax.ShapeDtypeStruct((nchip, r, l), packed.dtype),
        scratch_shapes=[pltpu.SemaphoreType.DMA, pltpu.SemaphoreType.DMA],
    )(packed)


def _pair_sum(packed, got, *, name):
    _, nchip, r, l = packed.shape
    tr = _tile(r, 512, 16)
    c = lax.axis_index("c").astype(jnp.int32).reshape(1)

    def body(c_ref, p_ref, q_ref, o_ref):
        o_ref[...] = (p_ref[...].astype(F32) + q_ref[...].astype(F32)).astype(o_ref.dtype)

    return _pcall(
        body, name=name,
        grid_spec=pltpu.PrefetchScalarGridSpec(
            num_scalar_prefetch=1, grid=(nchip, r // tr),
            in_specs=[pl.BlockSpec((None, None, tr, l), lambda k, i, cr: (cr[0], k, i, 0)),
                      pl.BlockSpec((None, tr, l), lambda k, i, cr: (k, i, 0))],
            out_specs=pl.BlockSpec((None, tr, l), lambda k, i, cr: (k, i, 0))),
        out_shape=jax.ShapeDtypeStruct((nchip, r, l), packed.dtype),
        compiler_params=_params("parallel", "parallel"),
    )(c, packed, got)


def _exchange_chips(part, *, name):
    nchip, r, l = part.shape

    def body(p_ref, g_ref, send_sems, recv_sems, local_sem):
        x, y, c = _place()
        mine = 2 * x + y
        own = pltpu.make_async_copy(p_ref.at[mine], g_ref.at[mine], local_sem)
        own.start()
        copies = []
        for j, (px, py) in enumerate([(1 - x, y), (x, 1 - y), (1 - x, 1 - y)]):
            copies.append(pltpu.make_async_remote_copy(
                src_ref=p_ref.at[2 * px + py], dst_ref=g_ref.at[mine], send_sem=send_sems.at[j],
                recv_sem=recv_sems.at[j], device_id=(px, py, c), device_id_type=MESH))
        for cp in copies:
            cp.start()
        for cp in copies:
            cp.wait()
        own.wait()

    return _pcall(
        body, name=name, in_specs=[HBM_SPEC], out_specs=HBM_SPEC,
        out_shape=jax.ShapeDtypeStruct((nchip, r, l), part.dtype),
        scratch_shapes=[pltpu.SemaphoreType.DMA((3,)), pltpu.SemaphoreType.DMA((3,)), pltpu.SemaphoreType.DMA],
    )(part)


def _sum_slots(g, *, name):
    n, r, l = g.shape
    tr = _tile(r, 512, 16)

    def body(g_ref, o_ref):
        acc = g_ref[0].astype(F32)
        for k in range(1, n):
            acc = acc + g_ref[k].astype(F32)
        o_ref[...] = acc

    return _pcall(
        body, name=name, grid=(r // tr,), in_specs=[pl.BlockSpec((n, tr, l), lambda i: (0, i, 0))],
        out_specs=pl.BlockSpec((tr, l), lambda i: (i, 0)), out_shape=_sds((r, l)),
        compiler_params=_params("parallel"),
    )(g)


def _pad_rows(a, mult):
    pad = (-a.shape[0]) % mult
    return a if pad == 0 else jnp.concatenate([a, jnp.zeros((pad,) + a.shape[1:], a.dtype)], axis=0)


def _gather_weights(w):
    packed = _pad_rows(jnp.concatenate([w[n].astype(BF16).reshape(-1, LANES) for n in SHARDED_ORDER], axis=0), 16)
    gathered = _all_gather(packed, name="gather_weights")
    full, r0 = {}, 0
    for n in SHARDED_ORDER:
        lyr, a, b = w[n].shape
        rows = lyr * a * b // LANES
        blocks = gathered[:, r0:r0 + rows].reshape(N_DEV, lyr, a, b)
        r0 += rows
        if SHARDED[n] == 2:
            full[n] = blocks.transpose(1, 2, 0, 3).reshape(lyr, a, N_DEV * b)
        else:
            full[n] = blocks.transpose(1, 0, 2, 3).reshape(lyr, N_DEV * a, b)
    lyr, a, b = w['ssd_conv_w'].shape
    taps = _all_gather(_pad_rows(w['ssd_conv_w'].reshape(-1, LANES), 8), name="gather_conv_taps")
    taps = taps[:, :lyr * a * b // LANES].reshape(N_DEV, lyr, a, b)
    full['ssd_conv_w'] = taps.transpose(1, 2, 0, 3).reshape(lyr, a, N_DEV * b)
    return full


def _scatter_grads(grads, shapes):
    blocks = []
    for n in SHARDED_ORDER:
        lyr, a, b = shapes[n]
        g = grads[n].astype(BF16)
        if SHARDED[n] == 2:
            g = g.reshape(lyr, a, N_DEV, b).transpose(2, 0, 1, 3)
        else:
            g = g.reshape(lyr, N_DEV, a, b).transpose(1, 0, 2, 3)
        blocks.append(g.reshape(N_DEV, -1, LANES))
    packed = jnp.concatenate(blocks, axis=1)
    pad = (-packed.shape[1]) % 16
    if pad:
        packed = jnp.concatenate([packed, jnp.zeros((N_DEV, pad, LANES), BF16)], axis=1)
    r = packed.shape[1]
    packed = packed.reshape(4, 2, r, LANES).transpose(1, 0, 2, 3)
    got = _exchange_sibling(packed, name="reduce_sibling")
    part = _pair_sum(packed, got, name="reduce_pair_sum")
    slots = _exchange_chips(part, name="reduce_chips")
    total = _sum_slots(slots, name="reduce_sum")
    out, r0 = {}, 0
    for n in SHARDED_ORDER:
        lyr, a, b = shapes[n]
        rows = lyr * a * b // LANES
        out[n] = total[r0:r0 + rows].reshape(lyr, a, b)
        r0 += rows
    return out


def _reduce_small(grads, shapes):
    flat = jnp.concatenate([grads[n].astype(F32).reshape(-1) for n in SMALL_ORDER])
    pad = (-flat.shape[0]) % (8 * LANES)
    flat = jnp.concatenate([flat, jnp.zeros((pad,), F32)]).reshape(-1, LANES)
    total = _sum_slots(_all_gather(flat, name="gather_small_grads"), name="sum_small_grads").reshape(-1)
    out, o = {}, 0
    for n in SMALL_ORDER:
        size = int(np.prod(shapes[n]))
        out[n] = total[o:o + size].reshape(shapes[n])
        o += size
    return out


def _adamw(w, g, m, v, *, name):
    shape = w.shape
    lanes = shape[-1] if (shape[-1] >= 128 or w.size % LANES) else LANES
    as2d = lambda a: a.reshape(-1, lanes)
    w2, g2, m2, v2 = as2d(w), as2d(g), as2d(m), as2d(v)
    r = w2.shape[0]
    tr = _tile(r, 256, 8)

    def body(w_ref, g_ref, m_ref, v_ref, d_ref, mo_ref, vo_ref):
        gg = g_ref[...]
        mn = ADAM_B1 * m_ref[...] + (1.0 - ADAM_B1) * gg
        vn = ADAM_B2 * v_ref[...] + (1.0 - ADAM_B2) * (gg * gg)
        m_hat = mn / (1.0 - ADAM_B1 ** ADAM_STEP)
        v_hat = vn / (1.0 - ADAM_B2 ** ADAM_STEP)
        d_ref[...] = -ADAM_LR * (m_hat / (jnp.sqrt(v_hat) + ADAM_EPS) + ADAM_WD * w_ref[...])
        mo_ref[...] = mn
        vo_ref[...] = vn

    spec = pl.BlockSpec((tr, lanes), lambda i: (i, 0))
    d, mo, vo = _pcall(
        body, name=name, grid=(r // tr,), in_specs=[spec] * 4, out_specs=[spec] * 3,
        out_shape=[_sds((r, lanes))] * 3, compiler_params=_params("parallel"),
    )(w2, g2, m2, v2)
    return d.reshape(shape), mo.reshape(shape), vo.reshape(shape)


def _layer_params(w, full, i):
    row = lambda a: a.astype(F32).reshape(1, -1)
    head8 = lambda a: jnp.broadcast_to(
        jnp.pad(a.astype(F32).reshape(SSD_GROUPS, 1, 8), ((0, 0), (0, 0), (0, PAD_HEADS - 8))), (SSD_GROUPS, 8, PAD_HEADS))
    w_in = full['w_in'][i]
    s = np.cumsum([SSD_D_INNER, SSD_CONV_DIM, SSD_HEADS, S5_WIDTH, D_MODEL])
    ffn = lambda k: dict(pre_g=row(w[f'{k}_pre_g'][i]), post_g=row(w[f'{k}_post_g'][i]),
                         w_gu=jnp.concatenate([full[f'{k}_w_gate'][i], full[f'{k}_w_up'][i]], axis=1),
                         w_down=full[f'{k}_w_down'][i])
    mix = dict(
        pre_g=row(w['mix_pre_g'][i]), post_g=row(w['mix_post_g'][i]),
        w_big=jnp.concatenate([w_in[:, :s[1]], w_in[:, s[3]:]], axis=1), w_dt=w_in[:, s[1]:s[2]], w_u5=w_in[:, s[2]:s[3]],
        conv_w=full['ssd_conv_w'][i].astype(F32), conv_b=row(w['ssd_conv_b'][i]),
        dt_bias8=head8(w['ssd_dt_bias'][i]), a_log8=head8(w['ssd_a_log'][i]), d8=head8(w['ssd_d'][i]),
        norm_g=row(w['ssd_norm_g'][i]), w_a=full['w_branch_a'][i],
        lam_re=w['s5_lambda_re'][i], lam_im=w['s5_lambda_im'][i], log_step=w['s5_log_step'][i].reshape(S5_GROUPS, 1),
        b_re=w['s5_b_re'][i].reshape(S5_GROUPS, -1), b_im=w['s5_b_im'][i].reshape(S5_GROUPS, -1),
        c_re=w['s5_c_re'][i], c_im=w['s5_c_im'][i], s5_d=row(w['s5_d'][i]),
        w_glu=full['s5_w_glu'][i], w_b=full['w_branch_b'][i], w_out=full['w_out'][i])
    return ffn('ffn1'), mix, ffn('ffn2')


def _loss_head(h, target, *, tm):
    t, d = h.shape

    def fn(y, tgt):
        err = y - tgt
        return err * (1.0 / d), jnp.sum(0.5 * jnp.sum(err * err, axis=-1, keepdims=True) * (1.0 / d), axis=0, keepdims=True)

    dy, loss = _rows(fn, name="loss_head", nrow=t // tm,
                     ins=[(h, _rspec(tm, d)), (target, _rspec(tm, d))],
                     outs=[(_sds((t, d)), _rspec(tm, d), False), (_sds((1, 128)), _bspec(128), True)])
    return dy, loss[0, 0]


def kernel(*args):
    n_w = len(WEIGHTS)
    x, target = args[0], args[1 + n_w]
    w = dict(zip(WEIGHTS, args[1:1 + n_w]))
    m = dict(zip(WEIGHTS, args[2 + n_w:2 + 2 * n_w]))
    v = dict(zip(WEIGHTS, args[2 + 2 * n_w:2 + 3 * n_w]))
    t = x.shape[1]
    tm = _tile(t, 256, 8)
    h = x.reshape(t, D_MODEL)

    full = _gather_weights(w)
    layers = [_layer_params(w, full, i) for i in range(DEPTH)]
    saved = []
    for p1, pm, p2 in layers:
        h, s1 = _ffn_fwd(h, p1, "ffn1", tm)
        h, sm = _mixer_fwd(h, pm, tm)
        h, s2 = _ffn_fwd(h, p2, "ffn2", tm)
        saved.append((s1, sm, s2))
    dh, loss_local = _loss_head(h, target.reshape(t, D_MODEL), tm=tm)
    loss = lax.psum(loss_local, ("x", "y", "c"))

    per_layer = []
    for (p1, pm, p2), (s1, sm, s2) in zip(reversed(layers), reversed(saved)):
        dh, g2 = _ffn_bwd(dh, p2, s2, "ffn2", tm)
        dh, gm = _mixer_bwd(dh, pm, sm, tm)
        dh, g1 = _ffn_bwd(dh, p1, s1, "ffn1", tm)
        g = dict(gm)
        g.update({f'ffn1_{k}': val for k, val in g1.items()})
        g.update({f'ffn2_{k}': val for k, val in g2.items()})
        per_layer.append(g)
    per_layer.reverse()
    shapes = {n: w[n].shape for n in WEIGHTS}
    stacked = {}
    for n in WEIGHTS:
        target_shape = ((shapes[n][0], shapes[n][1], shapes[n][2] * N_DEV) if SHARDED.get(n) == 2 else
                        (shapes[n][0], shapes[n][1] * N_DEV, shapes[n][2]) if SHARDED.get(n) == 1 else shapes[n])
        stacked[n] = jnp.stack([per_layer[i][n].reshape(target_shape[1:]) for i in range(DEPTH)])
    grad_w = _scatter_grads(stacked, shapes)
    grad_w.update(_reduce_small(stacked, shapes))

    delta, new_m, new_v = {}, {}, {}
    for n in WEIGHTS:
        delta[n], new_m[n], new_v[n] = _adamw(w[n], grad_w[n], m[n], v[n], name=f"adamw_{n}")
    return (loss, dh.reshape(x.shape), *[grad_w[n] for n in WEIGHTS], *[delta[n] for n in WEIGHTS],
            *[new_m[n] for n in WEIGHTS], *[new_v[n] for n in WEIGHTS])
```

```python
import functools
import math

import numpy as np
import jax
import jax.numpy as jnp
from jax import lax
from jax.experimental import pallas as pl
from jax.experimental.pallas import tpu as pltpu

F32 = jnp.float32
BF16 = jnp.bfloat16
MESH = pl.DeviceIdType.MESH
HIGHEST = lax.Precision.HIGHEST

D_MODEL = 1024
DEPTH = 2
FFN_HIDDEN = 2816
SSD_D_INNER = 2048
SSD_HEADS = 32
SSD_HEAD_DIM = 64
SSD_GROUPS = 4
SSD_STATE = 128
SSD_CHUNK = 128
SSD_CONV_DIM = 3072
SSD_CONV_WIDTH = 4
S5_WIDTH = 1024
S5_GROUP = 16
S5_GROUPS = 64
S5_STATE = 64
S5_MAX_REAL = -1e-4
S5_BLOCKS = 8
RMS_EPS = 1e-6
N_DEV = 8
LANES = 1024

ADAM_LR = 0.001
ADAM_B1 = 0.9
ADAM_B2 = 0.999
ADAM_EPS = 1e-08
ADAM_WD = 0.01
ADAM_STEP = 10

VMEM_LIMIT_BYTES = 48 * 1024 * 1024

WEIGHTS = ['ffn1_pre_g', 'ffn1_post_g', 'ffn1_w_gate', 'ffn1_w_up', 'ffn1_w_down', 'mix_pre_g', 'mix_post_g',
           'w_in', 'ssd_conv_w', 'ssd_conv_b', 'ssd_dt_bias', 'ssd_a_log', 'ssd_d', 'ssd_norm_g', 'w_branch_a',
           's5_lambda_re', 's5_lambda_im', 's5_b_re', 's5_b_im', 's5_c_re', 's5_c_im', 's5_log_step', 's5_d',
           's5_w_glu', 'w_branch_b', 'w_out', 'ffn2_pre_g', 'ffn2_post_g', 'ffn2_w_gate', 'ffn2_w_up',
           'ffn2_w_down']
SHARDED = {'ffn1_w_gate': 2, 'ffn1_w_up': 2, 'ffn1_w_down': 1, 'w_in': 2, 'ssd_conv_w': 2, 'w_branch_a': 1,
           's5_w_glu': 2, 'w_branch_b': 1, 'w_out': 1, 'ffn2_w_gate': 2, 'ffn2_w_up': 2, 'ffn2_w_down': 1}
SHARDED_ORDER = [n for n in WEIGHTS if n in SHARDED]
SMALL_ORDER = [n for n in WEIGHTS if n not in SHARDED or n == 'ssd_conv_w']


def _pcall(body, **kw):
    return pl.pallas_call(body, **kw)


def _params(*sem):
    return pltpu.CompilerParams(dimension_semantics=sem, vmem_limit_bytes=VMEM_LIMIT_BYTES)


def _tile(n, pref, align=128):
    if n <= pref:
        return n
    t = (pref // align) * align
    while t >= align:
        if n % t == 0:
            return t
        t -= align
    return n


def _rms(x, g):
    return x * lax.rsqrt(jnp.mean(x * x, axis=-1, keepdims=True) + RMS_EPS) * g


def _sigmoid(x):
    return 1.0 / (1.0 + jnp.exp(-x))


def _silu(x):
    return x * _sigmoid(x)


def _gelu(x):
    return 0.5 * x * (1.0 + jnp.tanh(math.sqrt(2.0 / math.pi) * (x + 0.044715 * (x * x * x))))


def _softplus(x):
    return jnp.maximum(x, 0.0) + jnp.log(1.0 + jnp.exp(-jnp.abs(x)))


def _dot(a, b, dims):
    return lax.dot_general(a.astype(BF16), b.astype(BF16), (dims, ((), ())), preferred_element_type=F32)


_NN = ((1,), (0,))
_NT = ((1,), (1,))
_TN = ((0,), (0,))


@jax.custom_vjp
def _bdot_nn(a, b):
    return _dot(a, b, _NN)


_bdot_nn.defvjp(lambda a, b: (_dot(a, b, _NN), (a, b)),
                lambda r, g: (_dot(g, r[1], _NT), _dot(r[0], g, _TN)))


@jax.custom_vjp
def _bdot_nt(a, b):
    return _dot(a, b, _NT)


_bdot_nt.defvjp(lambda a, b: (_dot(a, b, _NT), (a, b)),
                lambda r, g: (_dot(g, r[1], _NN), _dot(g, r[0], _TN)))


@jax.custom_vjp
def _bdot_tn(a, b):
    return _dot(a, b, _TN)


_bdot_tn.defvjp(lambda a, b: (_dot(a, b, _TN), (a, b)),
                lambda r, g: (_dot(r[1], g, _NT), _dot(r[0], g, _NN)))


def _fdot(a, b, dims=_NN):
    return lax.dot_general(a, b, (dims, ((), ())), precision=HIGHEST, preferred_element_type=F32)


def _mm(a, b, *, name, ta=False, tb=False, out_dtype=F32, tm=512, tn=512, tk=2048, shards=None):
    m, k = (a.shape[1], a.shape[0]) if ta else a.shape
    n = b.shape[0] if tb else b.shape[1]
    assert k == (b.shape[1] if tb else b.shape[0]), (a.shape, b.shape, ta, tb)
    if shards is not None and shards[0] == 'rows':
        tm = min(tm, m // N_DEV)
    if shards is not None and shards[0] == 'cols':
        tn = n // N_DEV
    tm, tn, tk = _tile(m, tm), _tile(n, tn), _tile(k, tk)
    nk = k // tk
    a_spec = pl.BlockSpec((tk, tm), lambda i, j, kk: (kk, i)) if ta else pl.BlockSpec((tm, tk), lambda i, j, kk: (i, kk))
    b_spec = pl.BlockSpec((tn, tk), lambda i, j, kk: (j, kk)) if tb else pl.BlockSpec((tk, tn), lambda i, j, kk: (kk, j))
    dims = ((0 if ta else 1,), (1 if tb else 0,))
    in_specs, args, aliases = [a_spec, b_spec], [a, b], {}
    out_spec = pl.BlockSpec((tm, tn), lambda i, j, kk: (i, j))
    out_shape = jax.ShapeDtypeStruct((m, n), out_dtype)
    if shards is not None:
        axis, layer, buf = shards
        if axis == 'rows':
            per = m // N_DEV // tm
            out_shape = jax.ShapeDtypeStruct((N_DEV, DEPTH, m // N_DEV, n), out_dtype)
            out_spec = pl.BlockSpec((None, None, tm, tn), lambda i, j, kk: (i // per, layer, i % per, j))
        else:
            out_shape = jax.ShapeDtypeStruct((N_DEV, DEPTH, m, n // N_DEV), out_dtype)
            out_spec = pl.BlockSpec((None, None, tm, tn), lambda i, j, kk: (j, layer, i, 0))
        if buf is not None:
            in_specs.append(pl.BlockSpec(memory_space=pl.ANY))
            args.append(buf)
            aliases = {2: 0}

    def body(a_ref, b_ref, *rest):
        o_ref, acc_ref = rest[-2:]
        kk = pl.program_id(2)

        @pl.when(kk == 0)
        def _():
            acc_ref[...] = jnp.zeros_like(acc_ref)

        acc_ref[...] += _dot(a_ref[...], b_ref[...], dims)

        @pl.when(kk == nk - 1)
        def _():
            o_ref[...] = acc_ref[...].astype(o_ref.dtype)

    return _pcall(
        body, name=name, grid=(m // tm, n // tn, nk),
        in_specs=in_specs, out_specs=out_spec, out_shape=out_shape, input_output_aliases=aliases,
        scratch_shapes=[pltpu.VMEM((tm, tn), F32)],
        compiler_params=_params("parallel", "parallel", "arbitrary"),
    )(*args)


def _rspec(tm, w, cb=0, percol=False):
    return pl.BlockSpec((tm, w), (lambda j, i: (i, cb + j)) if percol else (lambda j, i: (i, cb)))


def _bspec(w, cb=0, percol=False, rows=1):
    return pl.BlockSpec((rows, w), (lambda j, i: (0, cb + j)) if percol else (lambda j, i: (0, cb)))


def _rows(fn, *, name, nrow, ncol=1, ins, outs):
    n_in = len(ins)
    accs = [o[2] for o in outs]

    def body(*refs):
        vals = fn(*[r[...] for r in refs[:n_in]])
        if not isinstance(vals, (tuple, list)):
            vals = (vals,)
        i = pl.program_id(1)
        for ref, val, acc in zip(refs[n_in:], vals, accs):
            if acc:
                @pl.when(i == 0)
                def _(ref=ref):
                    ref[...] = jnp.zeros_like(ref)

                ref[...] += jnp.broadcast_to(val, ref.shape).astype(ref.dtype)
            else:
                ref[...] = val.astype(ref.dtype)

    res = _pcall(
        body, name=name, grid=(ncol, nrow),
        in_specs=[s for _, s in ins], out_specs=[o[1] for o in outs], out_shape=[o[0] for o in outs],
        compiler_params=_params("parallel", "arbitrary"),
    )(*[a for a, _ in ins])
    return res


def _sds(shape, dtype=F32):
    return jax.ShapeDtypeStruct(shape, dtype)


def _rms_fwd(h, g, *, name, tm):
    t, d = h.shape
    return _rows(lambda x, gg: _rms(x, gg), name=name, nrow=t // tm,
                 ins=[(h, _rspec(tm, d)), (g, _bspec(d))],
                 outs=[(_sds((t, d), BF16), _rspec(tm, d), False)])[0]


def _resid_fwd(h, f, g, scale, *, name, tm):
    t, d = h.shape
    return _rows(lambda x, ff, gg: x + scale * _rms(ff, gg), name=name, nrow=t // tm,
                 ins=[(h, _rspec(tm, d)), (f, _rspec(tm, d)), (g, _bspec(d))],
                 outs=[(_sds((t, d)), _rspec(tm, d), False)])[0]


def _resid_bwd(f, g, dh, scale, *, name, tm):
    t, d = f.shape

    def fn(ff, gg, dd):
        _, vjp = jax.vjp(lambda a, b: scale * _rms(a, b), ff, gg)
        return vjp(dd)

    return _rows(fn, name=name, nrow=t // tm,
                 ins=[(f, _rspec(tm, d)), (g, _bspec(d)), (dh, _rspec(tm, d))],
                 outs=[(_sds((t, d), BF16), _rspec(tm, d), False), (_sds((1, d)), _bspec(d), True)])


def _rms_bwd(h, g, dh, dxns, *, name, tm):
    t, d = h.shape

    def fn(x, gg, dd, *dx):
        _, vjp = jax.vjp(_rms, x, gg)
        tot = dx[0]
        for more in dx[1:]:
            tot = tot + more
        dxx, dg = vjp(tot)
        return dd + dxx, dg

    return _rows(fn, name=name, nrow=t // tm,
                 ins=[(h, _rspec(tm, d)), (g, _bspec(d)), (dh, _rspec(tm, d))] + [(x, _rspec(tm, d)) for x in dxns],
                 outs=[(_sds((t, d)), _rspec(tm, d), False), (_sds((1, d)), _bspec(d), True)])


NB = FFN_HIDDEN // N_DEV
MM_ROWS = 512


def _aliased(bufs):
    def build(first):
        live = [(o, b) for o, b in enumerate(bufs) if b is not None]
        return ([pl.BlockSpec(memory_space=pl.ANY)] * len(live), [b for _, b in live],
                {first + i: o for i, (o, _) in enumerate(live)})
    return build


def _ffn_up(xn, wg, wu, layer, *, name):
    t = xn.shape[0]
    tm = _tile(t, MM_ROWS)
    wspec = pl.BlockSpec((None, None, D_MODEL, NB), lambda i, j: (j, layer, 0, 0))

    def body(x_ref, g_ref, u_ref, ab_ref, hh_ref):
        x = x_ref[...]
        a, b = _dot(x, g_ref[...], _NN), _dot(x, u_ref[...], _NN)
        ab_ref[0] = a
        ab_ref[1] = b
        hh_ref[...] = (_silu(a) * b).astype(hh_ref.dtype)

    return _pcall(
        body, name=name, grid=(t // tm, N_DEV),
        in_specs=[pl.BlockSpec((tm, D_MODEL), lambda i, j: (i, 0)), wspec, wspec],
        out_specs=[pl.BlockSpec((None, 2, tm, NB), lambda i, j: (j, 0, i, 0)),
                   pl.BlockSpec((None, tm, NB), lambda i, j: (j, i, 0))],
        out_shape=[_sds((N_DEV, 2, t, NB)), _sds((N_DEV, t, NB), BF16)],
        compiler_params=_params("parallel", "parallel"),
    )(xn, wg, wu)


def _ffn_down(hh, wd, layer, *, name, tn=512):
    t = hh.shape[1]
    tm = _tile(t, MM_ROWS)

    def body(h_ref, w_ref, o_ref, acc_ref):
        kk = pl.program_id(2)

        @pl.when(kk == 0)
        def _():
            acc_ref[...] = jnp.zeros_like(acc_ref)

        acc_ref[...] += _dot(h_ref[...], w_ref[...], _NN)

        @pl.when(kk == N_DEV - 1)
        def _():
            o_ref[...] = acc_ref[...]

    return _pcall(
        body, name=name, grid=(t // tm, D_MODEL // tn, N_DEV),
        in_specs=[pl.BlockSpec((None, tm, NB), lambda i, j, kk: (kk, i, 0)),
                  pl.BlockSpec((None, None, NB, tn), lambda i, j, kk: (kk, layer, 0, j))],
        out_specs=pl.BlockSpec((tm, tn), lambda i, j, kk: (i, j)), out_shape=_sds((t, D_MODEL)),
        scratch_shapes=[pltpu.VMEM((tm, tn), F32)],
        compiler_params=_params("parallel", "parallel", "arbitrary"),
    )(hh, wd)


def _ffn_down_dx(df, wd, ab, layer, *, name):
    t = df.shape[0]
    tm = _tile(t, MM_ROWS)

    def body(d_ref, w_ref, ab_ref, o_ref):
        dhh = _dot(d_ref[...], w_ref[...], _NT)
        _, vjp = jax.vjp(lambda a, b: _silu(a) * b, ab_ref[0], ab_ref[1])
        da, db = vjp(dhh)
        o_ref[0] = da.astype(o_ref.dtype)
        o_ref[1] = db.astype(o_ref.dtype)

    blk = pl.BlockSpec((None, 2, tm, NB), lambda i, j: (j, 0, i, 0))
    return _pcall(
        body, name=name, grid=(t // tm, N_DEV),
        in_specs=[pl.BlockSpec((tm, D_MODEL), lambda i, j: (i, 0)),
                  pl.BlockSpec((None, None, NB, D_MODEL), lambda i, j: (j, layer, 0, 0)), blk],
        out_specs=blk, out_shape=_sds((N_DEV, 2, t, NB), BF16), compiler_params=_params("parallel", "parallel"),
    )(df, wd, ab)


def _ffn_down_dw(hh, df, layer, buf, *, name, tn=512):
    t = df.shape[0]
    tk = _tile(t, 2048)
    nk = t // tk
    extra = _aliased([buf])(2)

    def body(h_ref, d_ref, *rest):
        o_ref, acc_ref = rest[-2:]
        kk = pl.program_id(2)

        @pl.when(kk == 0)
        def _():
            acc_ref[...] = jnp.zeros_like(acc_ref)

        acc_ref[...] += _dot(h_ref[...], d_ref[...], _TN)

        @pl.when(kk == nk - 1)
        def _():
            o_ref[...] = acc_ref[...].astype(o_ref.dtype)

    return _pcall(
        body, name=name, grid=(N_DEV, D_MODEL // tn, nk),
        in_specs=[pl.BlockSpec((None, tk, NB), lambda j, n, kk: (j, kk, 0)),
                  pl.BlockSpec((tk, tn), lambda j, n, kk: (kk, n))] + extra[0],
        out_specs=pl.BlockSpec((None, None, NB, tn), lambda j, n, kk: (j, layer, 0, n)),
        out_shape=_sds((N_DEV, DEPTH, NB, D_MODEL), BF16), input_output_aliases=extra[2],
        scratch_shapes=[pltpu.VMEM((NB, tn), F32)],
        compiler_params=_params("parallel", "parallel", "arbitrary"),
    )(hh, df, *extra[1])


def _ffn_up_dx(dab, wg, wu, layer, *, name):
    t = dab.shape[2]
    tm = _tile(t, MM_ROWS)
    wspec = pl.BlockSpec((None, None, D_MODEL, NB), lambda i, j: (j, layer, 0, 0))

    def body(d_ref, g_ref, u_ref, o_ref):
        @pl.when(pl.program_id(1) == 0)
        def _():
            o_ref[...] = jnp.zeros_like(o_ref)

        o_ref[...] += _dot(d_ref[0], g_ref[...], _NT) + _dot(d_ref[1], u_ref[...], _NT)

    return _pcall(
        body, name=name, grid=(t // tm, N_DEV),
        in_specs=[pl.BlockSpec((None, 2, tm, NB), lambda i, j: (j, 0, i, 0)), wspec, wspec],
        out_specs=pl.BlockSpec((tm, D_MODEL), lambda i, j: (i, 0)), out_shape=_sds((t, D_MODEL)),
        compiler_params=_params("parallel", "arbitrary"),
    )(dab, wg, wu)


def _ffn_up_dw(xn, dab, layer, buf_g, buf_u, *, name):
    t = xn.shape[0]
    tk = _tile(t, 2048)
    nk = t // tk
    extra = _aliased([buf_g, buf_u])(2)

    def body(x_ref, d_ref, *rest):
        og_ref, ou_ref, accg_ref, accu_ref = rest[-4:]
        kk = pl.program_id(1)

        @pl.when(kk == 0)
        def _():
            accg_ref[...] = jnp.zeros_like(accg_ref)
            accu_ref[...] = jnp.zeros_like(accu_ref)

        x = x_ref[...]
        accg_ref[...] += _dot(x, d_ref[0], _TN)
        accu_ref[...] += _dot(x, d_ref[1], _TN)

        @pl.when(kk == nk - 1)
        def _():
            og_ref[...] = accg_ref[...].astype(og_ref.dtype)
            ou_ref[...] = accu_ref[...].astype(ou_ref.dtype)

    out = pl.BlockSpec((None, None, D_MODEL, NB), lambda j, kk: (j, layer, 0, 0))
    return _pcall(
        body, name=name, grid=(N_DEV, nk),
        in_specs=[pl.BlockSpec((tk, D_MODEL), lambda j, kk: (kk, 0)),
                  pl.BlockSpec((None, 2, tk, NB), lambda j, kk: (j, 0, kk, 0))] + extra[0],
        out_specs=[out, out], out_shape=[_sds((N_DEV, DEPTH, D_MODEL, NB), BF16)] * 2, input_output_aliases=extra[2],
        scratch_shapes=[pltpu.VMEM((D_MODEL, NB), F32)] * 2,
        compiler_params=_params("parallel", "arbitrary"),
    )(xn, dab, *extra[1])


def _ffn_fwd(h, p, tag, tm):
    xn = _rms_fwd(h, p['pre_g'], name=f"{tag}_rms", tm=tm)
    ab, hh = _ffn_up(xn, p['w_gate'], p['w_up'], p['layer'], name=f"{tag}_up")
    f = _ffn_down(hh, p['w_down'], p['layer'], name=f"{tag}_down")
    out = _resid_fwd(h, f, p['post_g'], 0.5, name=f"{tag}_res", tm=tm)
    return out, (h, xn, ab, hh, f)


def _ffn_bwd(dh, p, saved, tag, tm, bufs):
    h, xn, ab, hh, f = saved
    layer = p['layer']
    df, dpost = _resid_bwd(f, p['post_g'], dh, 0.5, name=f"{tag}_res_bwd", tm=tm)
    dab = _ffn_down_dx(df, p['w_down'], ab, layer, name=f"{tag}_down_dx")
    bufs[f'{tag}_w_down'] = _ffn_down_dw(hh, df, layer, bufs.get(f'{tag}_w_down'), name=f"{tag}_down_dw")
    dxn = _ffn_up_dx(dab, p['w_gate'], p['w_up'], layer, name=f"{tag}_up_dx")
    bufs[f'{tag}_w_gate'], bufs[f'{tag}_w_up'] = _ffn_up_dw(
        xn, dab, layer, bufs.get(f'{tag}_w_gate'), bufs.get(f'{tag}_w_up'), name=f"{tag}_up_dw")
    dh_in, dpre = _rms_bwd(h, p['pre_g'], dh, [dxn], name=f"{tag}_rms_bwd", tm=tm)
    return dh_in, {f'{tag}_pre_g': dpre, f'{tag}_post_g': dpost}


CONV_COLS = 256


def _shift_down(x, s):
    rows = lax.broadcasted_iota(jnp.int32, x.shape, 0)
    return jnp.where(rows >= s, pltpu.roll(x, s, axis=0), 0.0)


def _shift_up(x, s):
    t = x.shape[0]
    rows = lax.broadcasted_iota(jnp.int32, x.shape, 0)
    return jnp.where(rows < t - s, pltpu.roll(x, t - s, axis=0), 0.0)


def _conv_fwd(proj, col0, w, b, *, name):
    t = proj.shape[0]
    c = w.shape[1]
    cb0 = col0 // CONV_COLS

    def body(x_ref, w_ref, b_ref, o_ref):
        x = x_ref[...]
        acc = x * w_ref[3:4, :] + b_ref[...]
        for k in range(SSD_CONV_WIDTH - 1):
            acc = acc + _shift_down(x, SSD_CONV_WIDTH - 1 - k) * w_ref[k:k + 1, :]
        o_ref[...] = _silu(acc)

    return _pcall(
        body, name=name, grid=(c // CONV_COLS,),
        in_specs=[pl.BlockSpec((t, CONV_COLS), lambda j: (0, cb0 + j)),
                  pl.BlockSpec((SSD_CONV_WIDTH, CONV_COLS), lambda j: (0, j)),
                  pl.BlockSpec((1, CONV_COLS), lambda j: (0, j))],
        out_specs=pl.BlockSpec((t, CONV_COLS), lambda j: (0, j)),
        out_shape=_sds((t, c)), compiler_params=_params("parallel"),
    )(proj, w, b)


def _conv_bwd(proj, col0, w, b, dout, *, name):
    t = proj.shape[0]
    c = w.shape[1]
    cb0 = col0 // CONV_COLS

    def body(x_ref, w_ref, b_ref, d_ref, dx_ref, dw_ref, db_ref):
        x = x_ref[...]
        shifted = [_shift_down(x, SSD_CONV_WIDTH - 1 - k) for k in range(SSD_CONV_WIDTH - 1)] + [x]
        pre = b_ref[...] + shifted[3] * w_ref[3:4, :]
        for k in range(SSD_CONV_WIDTH - 1):
            pre = pre + shifted[k] * w_ref[k:k + 1, :]
        sg = _sigmoid(pre)
        dpre = d_ref[...] * (sg * (1.0 + pre * (1.0 - sg)))
        dx = dpre * w_ref[3:4, :]
        for k in range(SSD_CONV_WIDTH - 1):
            dx = dx + _shift_up(dpre, SSD_CONV_WIDTH - 1 - k) * w_ref[k:k + 1, :]
        dx_ref[...] = dx.astype(dx_ref.dtype)
        for k in range(SSD_CONV_WIDTH):
            dw_ref[k:k + 1, :] = jnp.sum(dpre * shifted[k], axis=0, keepdims=True)
        db_ref[...] = jnp.sum(dpre, axis=0, keepdims=True)

    return _pcall(
        body, name=name, grid=(c // CONV_COLS,),
        in_specs=[pl.BlockSpec((t, CONV_COLS), lambda j: (0, cb0 + j)),
                  pl.BlockSpec((SSD_CONV_WIDTH, CONV_COLS), lambda j: (0, j)),
                  pl.BlockSpec((1, CONV_COLS), lambda j: (0, j)),
                  pl.BlockSpec((t, CONV_COLS), lambda j: (0, j))],
        out_specs=[pl.BlockSpec((t, CONV_COLS), lambda j: (0, j)),
                   pl.BlockSpec((SSD_CONV_WIDTH, CONV_COLS), lambda j: (0, j)),
                   pl.BlockSpec((1, CONV_COLS), lambda j: (0, j))],
        out_shape=[_sds((t, c), BF16), _sds((SSD_CONV_WIDTH, c)), _sds((1, c))],
        compiler_params=_params("parallel"),
    )(proj, w, b, dout)


HALF = 256
HEADS_PER_HALF = 4
PAD_HEADS = 128


def _head_expanders():
    k = lax.broadcasted_iota(jnp.int32, (PAD_HEADS, HALF), 0)
    j = lax.broadcasted_iota(jnp.int32, (PAD_HEADS, HALF), 1)
    kt = lax.broadcasted_iota(jnp.int32, (HALF, PAD_HEADS), 1)
    jt = lax.broadcasted_iota(jnp.int32, (HALF, PAD_HEADS), 0)
    es, ets = [], []
    for half in range(2):
        es.append(jnp.where(k == j // SSD_HEAD_DIM + half * HEADS_PER_HALF, 1.0, 0.0).astype(F32))
        ets.append(jnp.where(kt == jt // SSD_HEAD_DIM + half * HEADS_PER_HALF, 1.0, 0.0).astype(F32))
    return es, ets


def _ssd_chunk(x_lo, x_hi, bm, cm, dtr, dtb8, alog8, dsk8, s_lo, s_hi):
    q = x_lo.shape[0]
    es, ets = _head_expanders()
    rowmean = lambda v: jnp.sum(v, axis=0, keepdims=True) * 0.125
    dt = _softplus(dtr + rowmean(dtb8))
    a = -jnp.exp(rowmean(alog8))
    adt = a * dt
    adt_tot8 = jnp.broadcast_to(jnp.sum(adt, axis=0, keepdims=True), (8, PAD_HEADS))
    ll = lax.broadcasted_iota(jnp.int32, (q, q), 0)
    ss = lax.broadcasted_iota(jnp.int32, (q, q), 1)
    ltri = jnp.where(ll >= ss, 1.0, 0.0).astype(F32)
    lane = lax.broadcasted_iota(jnp.int32, (1, HALF), 1)
    cb = _bdot_nt(cm, bm)
    outs = []
    for half, (x, s_in) in enumerate(((x_lo, s_lo), (x_hi, s_hi))):
        e, et = es[half], ets[half]
        dtf = _fdot(dt, e)
        af = _fdot(adt, e)
        dskf = rowmean(_fdot(dsk8, e))
        acum = _fdot(ltri, af)
        alast = jnp.sum(af, axis=0, keepdims=True)
        xdt = x * dtf
        ydiag = jnp.zeros((q, HALF), F32)
        for r in range(HEADS_PER_HALF):
            sel = lane == r * SSD_HEAD_DIM
            ac_r = jnp.sum(jnp.where(sel, acum, 0.0), axis=1, keepdims=True)
            a_r = jnp.sum(jnp.where(sel, af, 0.0), axis=1, keepdims=True)
            arow = jnp.sum(jnp.where(ll <= ss, a_r, 0.0), axis=0, keepdims=True)
            decay = jnp.exp(jnp.where(ll >= ss, ac_r - arow, -jnp.inf))
            yh = _bdot_nn(cb * decay, xdt)
            ydiag = ydiag + jnp.where(lane // SSD_HEAD_DIM == r, yh, 0.0)
        st = _bdot_tn(xdt * jnp.exp(alast - acum), bm)
        yoff = _bdot_nt(cm, s_in) * jnp.exp(acum)
        y = ydiag + yoff + dskf * x
        alast_col = jnp.sum(_fdot(et, adt_tot8, _NT), axis=1, keepdims=True) * 0.125
        outs.append((y, jnp.exp(alast_col) * s_in + st))
    return outs[0][0], outs[1][0], outs[0][1], outs[1][1]


def _ssd_specs(t, rev):
    q = SSD_CHUNK
    nc = t // q
    ci = (lambda c: nc - 1 - c) if rev else (lambda c: c)
    xcol0 = SSD_D_INNER // SSD_STATE
    return dict(
        x_lo=pl.BlockSpec((q, HALF), lambda g, c: (ci(c), 2 * g)),
        x_hi=pl.BlockSpec((q, HALF), lambda g, c: (ci(c), 2 * g + 1)),
        bm=pl.BlockSpec((q, SSD_STATE), lambda g, c: (ci(c), xcol0 + g)),
        cm=pl.BlockSpec((q, SSD_STATE), lambda g, c: (ci(c), xcol0 + SSD_GROUPS + g)),
        dt=pl.BlockSpec((None, q, PAD_HEADS), lambda g, c: (g, ci(c), 0)),
        par=pl.BlockSpec((None, 8, PAD_HEADS), lambda g, c: (g, 0, 0)),
        st=pl.BlockSpec((None, None, 2, HALF, SSD_STATE), lambda g, c: (ci(c), g, 0, 0, 0)),
        y=pl.BlockSpec((q, 2 * HALF), lambda g, c: (ci(c), g)),
        grp=pl.BlockSpec((q, SSD_STATE), lambda g, c: (ci(c), g)),
    )


def _ssd_fwd(xc, dt4, dtb, alog, dsk, *, name):
    t = xc.shape[0]
    nc = t // SSD_CHUNK
    sp = _ssd_specs(t, False)

    def body(xl, xh, bm, cm, dt, p0, p1, p2, y_ref, sin_ref, st_ref):
        @pl.when(pl.program_id(1) == 0)
        def _():
            st_ref[...] = jnp.zeros_like(st_ref)

        sin_ref[...] = st_ref[...]
        y_lo, y_hi, so_lo, so_hi = _ssd_chunk(xl[...], xh[...], bm[...], cm[...], dt[...], p0[...], p1[...],
                                              p2[...], st_ref[0], st_ref[1])
        y_ref[:, :HALF] = y_lo
        y_ref[:, HALF:] = y_hi
        st_ref[0] = so_lo
        st_ref[1] = so_hi

    return _pcall(
        body, name=name, grid=(SSD_GROUPS, nc),
        in_specs=[sp['x_lo'], sp['x_hi'], sp['bm'], sp['cm'], sp['dt'], sp['par'], sp['par'], sp['par']],
        out_specs=[sp['y'], sp['st']],
        out_shape=[_sds((t, SSD_D_INNER)), _sds((nc, SSD_GROUPS, 2, HALF, SSD_STATE))],
        scratch_shapes=[pltpu.VMEM((2, HALF, SSD_STATE), F32)],
        compiler_params=_params("parallel", "arbitrary"),
    )(xc, xc, xc, xc, dt4, dtb, alog, dsk)


def _ssd_bwd(xc, dt4, dtb, alog, dsk, sin, dy, *, name):
    t = xc.shape[0]
    nc = t // SSD_CHUNK
    sp = _ssd_specs(t, True)

    def body(xl, xh, bm, cm, dt, p0, p1, p2, sin_ref, dy_ref,
             dx_ref, db_ref, dc_ref, ddt_ref, dp0, dp1, dp2, dst_ref):
        first = pl.program_id(1) == 0

        @pl.when(first)
        def _():
            dst_ref[...] = jnp.zeros_like(dst_ref)

        _, vjp = jax.vjp(_ssd_chunk, xl[...], xh[...], bm[...], cm[...], dt[...], p0[...], p1[...], p2[...],
                         sin_ref[0], sin_ref[1])
        dxl, dxh, dbm, dcm, ddt, g0, g1, g2, ds_lo, ds_hi = vjp(
            (dy_ref[:, :HALF], dy_ref[:, HALF:], dst_ref[0], dst_ref[1]))
        dx_ref[:, :HALF] = dxl
        dx_ref[:, HALF:] = dxh
        db_ref[...] = dbm
        dc_ref[...] = dcm
        ddt_ref[...] = ddt
        dst_ref[0] = ds_lo
        dst_ref[1] = ds_hi
        for ref, g in ((dp0, g0), (dp1, g1), (dp2, g2)):
            tot = jnp.broadcast_to(jnp.sum(g, axis=0, keepdims=True), ref.shape)

            @pl.when(first)
            def _(ref=ref):
                ref[...] = jnp.zeros_like(ref)

            ref[...] += tot

    return _pcall(
        body, name=name, grid=(SSD_GROUPS, nc),
        in_specs=[sp['x_lo'], sp['x_hi'], sp['bm'], sp['cm'], sp['dt'], sp['par'], sp['par'], sp['par'],
                  sp['st'], sp['y']],
        out_specs=[sp['y'], sp['grp'], sp['grp'], sp['dt'], sp['par'], sp['par'], sp['par']],
        out_shape=[_sds((t, SSD_D_INNER)), _sds((t, SSD_GROUPS * SSD_STATE)), _sds((t, SSD_GROUPS * SSD_STATE)),
                   _sds((SSD_GROUPS, t, PAD_HEADS))] + [_sds((SSD_GROUPS, 8, PAD_HEADS))] * 3,
        scratch_shapes=[pltpu.VMEM((2, HALF, SSD_STATE), F32)],
        compiler_params=_params("parallel", "arbitrary"),
    )(xc, xc, xc, xc, dt4, dtb, alog, dsk, sin, dy)


def _gatenorm(y, z, g):
    v = y * _silu(z)
    return v * lax.rsqrt(jnp.mean(v * v, axis=-1, keepdims=True) + RMS_EPS) * g


def _gmm(terms, *, trans_w, add=None, name, out_dtype=F32, tm=512):
    t = terms[0][0].shape[0]
    tm = _tile(t, tm)
    _, kw, nw = terms[0][1].shape
    wi, wo = (nw, kw) if trans_w else (kw, nw)
    signs = [s for _, _, s in terms]
    n = len(terms)

    def body(*refs):
        acc = None
        for i in range(n):
            prod = _dot(refs[2 * i][...], refs[2 * i + 1][...], _NT if trans_w else _NN)
            prod = prod if signs[i] > 0 else -prod
            acc = prod if acc is None else acc + prod
        o_ref = refs[-1]
        if add is not None:
            acc = acc + refs[2 * n][...]
        o_ref[...] = acc.astype(o_ref.dtype)

    in_specs, args = [], []
    for x, w, _ in terms:
        in_specs += [pl.BlockSpec((tm, wi), lambda i, gb: (i, gb)), pl.BlockSpec((None, kw, nw), lambda i, gb: (gb, 0, 0))]
        args += [x, w]
    if add is not None:
        in_specs.append(pl.BlockSpec((tm, wo), lambda i, gb: (i, gb)))
        args.append(add)
    return _pcall(
        body, name=name, grid=(t // tm, S5_BLOCKS), in_specs=in_specs,
        out_specs=pl.BlockSpec((tm, wo), lambda i, gb: (i, gb)), out_shape=_sds((t, S5_BLOCKS * wo), out_dtype),
        compiler_params=_params("parallel", "parallel"),
    )(*args)


def _gmm_tn(x, dy, sign, *, name, tm=512):
    t = x.shape[0]
    tm = _tile(t, tm)
    kw, nw = x.shape[1] // S5_BLOCKS, dy.shape[1] // S5_BLOCKS

    def body(x_ref, d_ref, o_ref):
        @pl.when(pl.program_id(1) == 0)
        def _():
            o_ref[...] = jnp.zeros_like(o_ref)

        prod = _dot(x_ref[...], d_ref[...], _TN)
        o_ref[...] += prod if sign > 0 else -prod

    return _pcall(
        body, name=name, grid=(S5_BLOCKS, t // tm),
        in_specs=[pl.BlockSpec((tm, kw), lambda gb, i: (i, gb)), pl.BlockSpec((tm, nw), lambda gb, i: (i, gb))],
        out_specs=pl.BlockSpec((None, kw, nw), lambda gb, i: (gb, 0, 0)), out_shape=_sds((S5_BLOCKS, kw, nw)),
        compiler_params=_params("parallel", "arbitrary"),
    )(x, dy)


SCAN_COLS = 256
SCAN_UNROLL = 8


def _cmul(ar, ai, br, bi):
    return ar * br - ai * bi, ar * bi + ai * br


def _segment_power(ar, ai, n):
    assert n & (n - 1) == 0
    for _ in range(n.bit_length() - 1):
        ar, ai = _cmul(ar, ai, ar, ai)
    return ar, ai


def _carry_in(fr, fi, pr, pi, reverse):
    rows = lax.broadcasted_iota(jnp.int32, fr.shape, 0)
    cr = jnp.zeros_like(fr[0:1])
    ci = jnp.zeros_like(cr)
    outr = jnp.zeros_like(fr)
    outi = jnp.zeros_like(fr)
    order = range(6, -1, -1) if reverse else range(1, 8)
    for j in order:
        src = j + 1 if reverse else j - 1
        nr, ni = _cmul(pr[0:1], pi[0:1], cr, ci)
        cr, ci = nr + fr[src:src + 1], ni + fi[src:src + 1]
        outr = jnp.where(rows == j, cr, outr)
        outi = jnp.where(rows == j, ci, outi)
    return outr, outi


def _scan_fwd(bur, bui, lr, li, *, name):
    t, w = bur.shape
    nrt = t // 8

    def body(br_ref, bi_ref, lr_ref, li_ref, sr_ref, si_ref):
        ar = jnp.broadcast_to(lr_ref[...], (8, SCAN_COLS))
        ai = jnp.broadcast_to(li_ref[...], (8, SCAN_COLS))

        def step(r, s, store):
            rows = pl.ds(pl.multiple_of(r * 8, 8), 8)
            nr, ni = _cmul(ar, ai, s[0], s[1])
            nr, ni = nr + br_ref[rows, :], ni + bi_ref[rows, :]
            if store:
                sr_ref[rows, :] = nr
                si_ref[rows, :] = ni
            return nr, ni

        zero = (jnp.zeros((8, SCAN_COLS), F32), jnp.zeros((8, SCAN_COLS), F32))
        fr, fi = lax.fori_loop(0, nrt, lambda r, s: step(r, s, False), zero, unroll=SCAN_UNROLL)
        pr, pi = _segment_power(ar, ai, nrt)
        init = _carry_in(fr, fi, pr, pi, False)
        lax.fori_loop(0, nrt, lambda r, s: step(r, s, True), init, unroll=SCAN_UNROLL)

    col = pl.BlockSpec((t, SCAN_COLS), lambda j: (0, j))
    row = pl.BlockSpec((1, SCAN_COLS), lambda j: (0, j))
    return _pcall(
        body, name=name, grid=(w // SCAN_COLS,), in_specs=[col, col, row, row], out_specs=[col, col],
        out_shape=[_sds((t, w)), _sds((t, w))], compiler_params=_params("parallel"),
    )(bur, bui, lr, li)


def _scan_bwd(dr, di, sr, si, lr, li, *, name):
    t, w = dr.shape
    nrt = t // 8

    def body(dr_ref, di_ref, sr_ref, si_ref, lr_ref, li_ref, gr_ref, gi_ref, dlr_ref, dli_ref):
        ar = jnp.broadcast_to(lr_ref[...], (8, SCAN_COLS))
        ai = -jnp.broadcast_to(li_ref[...], (8, SCAN_COLS))
        zero = jnp.zeros((8, SCAN_COLS), F32)

        def step1(k, g):
            rows = pl.ds(pl.multiple_of((nrt - 1 - k) * 8, 8), 8)
            nr, ni = _cmul(ar, ai, g[0], g[1])
            return nr + dr_ref[rows, :], ni + di_ref[rows, :]

        fr, fi = lax.fori_loop(0, nrt, step1, (zero, zero), unroll=SCAN_UNROLL)
        pr, pi = _segment_power(ar, ai, nrt)
        init = _carry_in(fr, fi, pr, pi, True)

        def step2(k, carry):
            gr, gi, accr, acci = carry
            r = nrt - 1 - k
            rows = pl.ds(pl.multiple_of(r * 8, 8), 8)
            prev = pl.ds(pl.multiple_of(jnp.maximum(r - 1, 0) * 8, 8), 8)
            nr, ni = _cmul(ar, ai, gr, gi)
            nr, ni = nr + dr_ref[rows, :], ni + di_ref[rows, :]
            gr_ref[rows, :] = nr
            gi_ref[rows, :] = ni
            keep = jnp.where(r > 0, 1.0, 0.0)
            pr_, pi_ = sr_ref[prev, :] * keep, si_ref[prev, :] * keep
            return nr, ni, accr + (pr_ * nr + pi_ * ni), acci + (pr_ * ni - pi_ * nr)

        _, _, accr, acci = lax.fori_loop(0, nrt, step2, (init[0], init[1], zero, zero), unroll=SCAN_UNROLL)
        last = pl.ds((nrt - 1) * 8, 8)
        pr_, pi_ = _shift_down(sr_ref[last, :], 1), _shift_down(si_ref[last, :], 1)
        g0r, g0i = gr_ref[0:8, :], gi_ref[0:8, :]
        accr = accr + (pr_ * g0r + pi_ * g0i)
        acci = acci + (pr_ * g0i - pi_ * g0r)
        dlr_ref[...] = jnp.sum(accr, axis=0, keepdims=True)
        dli_ref[...] = jnp.sum(acci, axis=0, keepdims=True)

    col = pl.BlockSpec((t, SCAN_COLS), lambda j: (0, j))
    row = pl.BlockSpec((1, SCAN_COLS), lambda j: (0, j))
    return _pcall(
        body, name=name, grid=(w // SCAN_COLS,), in_specs=[col, col, col, col, row, row],
        out_specs=[col, col, row, row], out_shape=[_sds((t, w)), _sds((t, w)), _sds((1, w)), _sds((1, w))],
        compiler_params=_params("parallel"),
    )(dr, di, sr, si, lr, li)


def _s5_expander():
    n = lax.broadcasted_iota(jnp.int32, (S5_STATE, S5_STATE * S5_GROUP), 0)
    j = lax.broadcasted_iota(jnp.int32, (S5_STATE, S5_STATE * S5_GROUP), 1)
    return jnp.where(n == j // S5_GROUP, 1.0, 0.0).astype(F32)


def _s5_discretise(lam_re, lam_im, log_step, b_re, b_im):
    lr = jnp.minimum(lam_re, S5_MAX_REAL)
    step = jnp.exp(log_step)
    mag = jnp.exp(lr * step)
    ang = lam_im * step
    lbr, lbi = mag * jnp.cos(ang), mag * jnp.sin(ang)
    p, q = lbr - 1.0, lbi
    den = lr * lr + lam_im * lam_im
    cr, ci = (p * lr + q * lam_im) / den, (q * lr - p * lam_im) / den
    e = _s5_expander()
    cre, cie = _fdot(cr, e), _fdot(ci, e)
    return lbr, lbi, cre * b_re - cie * b_im, cre * b_im + cie * b_re


def _s5_params_fwd(lam_re, lam_im, log_step, b_re, b_im, *, name):
    g, n, w = S5_GROUPS, S5_STATE, S5_STATE * S5_GROUP

    def body(a, b, c, d, e, o0, o1, o2, o3):
        for ref, val in zip((o0, o1, o2, o3), _s5_discretise(a[...], b[...], c[...], d[...], e[...])):
            ref[...] = val

    return _pcall(body, name=name, out_shape=[_sds((g, n)), _sds((g, n)), _sds((g, w)), _sds((g, w))])(
        lam_re, lam_im, log_step, b_re, b_im)


def _s5_params_bwd(lam_re, lam_im, log_step, b_re, b_im, cts, *, name):
    g, n, w = S5_GROUPS, S5_STATE, S5_STATE * S5_GROUP

    def body(a, b, c, d, e, c0, c1, c2, c3, o0, o1, o2, o3, o4):
        _, vjp = jax.vjp(_s5_discretise, a[...], b[...], c[...], d[...], e[...])
        for ref, val in zip((o0, o1, o2, o3, o4), vjp((c0[...], c1[...], c2[...], c3[...]))):
            ref[...] = val

    return _pcall(body, name=name,
                  out_shape=[_sds((g, n)), _sds((g, n)), _sds((g, 1)), _sds((g, w)), _sds((g, w))])(
        lam_re, lam_im, log_step, b_re, b_im, *cts)


def _perm(a):
    t, c = a.shape
    return a.reshape(8, t // 8, c).transpose(1, 0, 2).reshape(t, c)


def _unperm(a):
    t, c = a.shape
    return a.reshape(t // 8, 8, c).transpose(1, 0, 2).reshape(t, c)


def _blockdiag(m, rows_inner, cols_inner):
    m = m.reshape(S5_BLOCKS, 8, rows_inner, cols_inner)
    eye = jnp.eye(8, dtype=m.dtype)
    out = m[:, :, :, None, :] * eye[None, :, None, :, None]
    return out.reshape(S5_BLOCKS, 8 * rows_inner, 8 * cols_inner)


def _blockdiag_extract(m, rows_inner, cols_inner):
    m = m.reshape(S5_BLOCKS, 8, rows_inner, 8, cols_inner)
    d = jnp.diagonal(m, axis1=1, axis2=3)
    return d.transpose(0, 3, 1, 2).reshape(S5_GROUPS, rows_inner, cols_inner)


Z0, XBC0, GA0, GB0 = 0, SSD_D_INNER, SSD_D_INNER + SSD_CONV_DIM, SSD_D_INNER + SSD_CONV_DIM + D_MODEL
BIG = GB0 + D_MODEL


def _mixer_fwd(h, p, tm):
    t = h.shape[0]
    nrow = t // tm
    u = _rms_fwd(h, p['pre_g'], name="mix_rms", tm=tm)
    u_p = _perm(u)
    proj = _mm(u, p['w_big'], name="mix_in")
    dtr = _mm(u, p['w_dt'], name="mix_in_dt")
    u5 = _mm(u_p, p['w_u5'], name="mix_in_s5")
    xc = _conv_fwd(proj, XBC0, p['conv_w'], p['conv_b'], name="ssd_conv")
    dt4 = jnp.pad(dtr.reshape(t, SSD_GROUPS, 8).transpose(1, 0, 2), ((0, 0), (0, 0), (0, PAD_HEADS - 8)))
    y_ssd, s_in = _ssd_fwd(xc, dt4, p['dt_bias8'], p['a_log8'], p['d8'], name="ssd_scan")
    gw = SSD_D_INNER // SSD_GROUPS
    ya = _rows(_gatenorm, name="ssd_gate", nrow=nrow, ncol=SSD_GROUPS,
               ins=[(y_ssd, _rspec(tm, gw, 0, True)), (proj, _rspec(tm, gw, Z0 // gw, True)),
                    (p['norm_g'], _bspec(gw, 0, True))],
               outs=[(_sds((t, SSD_D_INNER), BF16), _rspec(tm, gw, 0, True), False)])[0]
    y_a = _mm(ya, p['w_a'], name="mix_a")
    lbr, lbi, bbr, bbi = _s5_params_fwd(p['lam_re'], p['lam_im'], p['log_step'], p['b_re'], p['b_im'], name="s5_par")
    bd = lambda m: _blockdiag(m.reshape(S5_GROUPS, S5_STATE, S5_GROUP).transpose(0, 2, 1), S5_GROUP, S5_STATE).astype(BF16)
    bre, bim = bd(bbr), bd(bbi)
    cre = _blockdiag(p['c_re'].transpose(0, 2, 1), S5_STATE, S5_GROUP).astype(BF16)
    cim = _blockdiag(p['c_im'].transpose(0, 2, 1), S5_STATE, S5_GROUP).astype(BF16)
    lr, li = lbr.reshape(1, -1), lbi.reshape(1, -1)
    bur = _gmm([(u5, bre, 1)], trans_w=False, name="s5_bu_re")
    bui = _gmm([(u5, bim, 1)], trans_w=False, name="s5_bu_im")
    sr, si = _scan_fwd(bur, bui, lr, li, name="s5_scan")
    y5 = _gmm([(sr, cre, 1), (si, cim, -1)], trans_w=False, name="s5_out")
    y5g = _rows(lambda a, b, d: _gelu(a + d * b), name="s5_act", nrow=nrow,
                ins=[(y5, _rspec(tm, S5_WIDTH)), (u5, _rspec(tm, S5_WIDTH)), (p['s5_d'], _bspec(S5_WIDTH))],
                outs=[(_sds((t, S5_WIDTH), BF16), _rspec(tm, S5_WIDTH), False)])[0]
    vg = _mm(y5g, p['w_glu'], name="s5_glu")
    ybin = _rows(lambda a, b: a * _sigmoid(b), name="s5_glu_act", nrow=nrow,
                 ins=[(vg, _rspec(tm, S5_WIDTH, 0)), (vg, _rspec(tm, S5_WIDTH, 1))],
                 outs=[(_sds((t, S5_WIDTH), BF16), _rspec(tm, S5_WIDTH), False)])[0]
    y_b = _unperm(_mm(ybin, p['w_b'], name="mix_b"))
    merged = _rows(lambda ga, gb, a, b: _sigmoid(ga) * a + _sigmoid(gb) * b, name="mix_merge", nrow=nrow,
                   ins=[(proj, _rspec(tm, D_MODEL, GA0 // D_MODEL)), (proj, _rspec(tm, D_MODEL, GB0 // D_MODEL)),
                        (y_a, _rspec(tm, D_MODEL)), (y_b, _rspec(tm, D_MODEL))],
                   outs=[(_sds((t, D_MODEL), BF16), _rspec(tm, D_MODEL), False)])[0]
    m = _mm(merged, p['w_out'], name="mix_out")
    out = _resid_fwd(h, m, p['post_g'], 1.0, name="mix_res", tm=tm)
    saved = dict(h=h, u=u, u_p=u_p, proj=proj, u5=u5, xc=xc, dt4=dt4, s_in=s_in, y_ssd=y_ssd, ya=ya, y_a=y_a,
                 bre=bre, bim=bim, cre=cre, cim=cim, lr=lr, li=li, sr=sr, si=si, y5=y5, y5g=y5g, vg=vg, ybin=ybin,
                 y_b=y_b, merged=merged, m=m)
    return out, saved


def _mixer_bwd(dh, p, s, tm, bufs):
    t = dh.shape[0]
    nrow = t // tm
    proj = s['proj']
    layer = p['layer']

    def grad_mm(a, b, wname, axis, name):
        bufs[wname] = _mm(a, b, ta=True, name=name, out_dtype=BF16, shards=(axis, layer, bufs.get(wname)))

    dm, dpost = _resid_bwd(s['m'], p['post_g'], dh, 1.0, name="mix_res_bwd", tm=tm)
    dmerged = _mm(dm, p['w_out'], tb=True, name="mix_out_dx")
    grad_mm(s['merged'], dm, 'w_out', 'rows', "mix_out_dw")

    def merge_bwd(ga, gb, a, b, d):
        _, vjp = jax.vjp(lambda ga_, gb_, a_, b_: _sigmoid(ga_) * a_ + _sigmoid(gb_) * b_, ga, gb, a, b)
        dga, dgb, da, db = vjp(d)
        return jnp.concatenate([dga, dgb], axis=1), da, db

    dgab, dy_a, dy_b = _rows(
        merge_bwd, name="mix_merge_bwd", nrow=nrow,
        ins=[(proj, _rspec(tm, D_MODEL, GA0 // D_MODEL)), (proj, _rspec(tm, D_MODEL, GB0 // D_MODEL)),
             (s['y_a'], _rspec(tm, D_MODEL)), (s['y_b'], _rspec(tm, D_MODEL)), (dmerged, _rspec(tm, D_MODEL))],
        outs=[(_sds((t, 2 * D_MODEL), BF16), _rspec(tm, 2 * D_MODEL), False),
              (_sds((t, D_MODEL), BF16), _rspec(tm, D_MODEL), False),
              (_sds((t, D_MODEL), BF16), _rspec(tm, D_MODEL), False)])
    dya = _mm(dy_a, p['w_a'], tb=True, name="mix_a_dx")
    grad_mm(s['ya'], dy_a, 'w_branch_a', 'rows', "mix_a_dw")
    gw = SSD_D_INNER // SSD_GROUPS

    def gate_bwd(y, z, g, d):
        _, vjp = jax.vjp(_gatenorm, y, z, g)
        return vjp(d)

    dy_ssd, dz, dnorm = _rows(
        gate_bwd, name="ssd_gate_bwd", nrow=nrow, ncol=SSD_GROUPS,
        ins=[(s['y_ssd'], _rspec(tm, gw, 0, True)), (proj, _rspec(tm, gw, Z0 // gw, True)),
             (p['norm_g'], _bspec(gw, 0, True)), (dya, _rspec(tm, gw, 0, True))],
        outs=[(_sds((t, SSD_D_INNER)), _rspec(tm, gw, 0, True), False),
              (_sds((t, SSD_D_INNER), BF16), _rspec(tm, gw, 0, True), False),
              (_sds((1, SSD_D_INNER)), _bspec(gw, 0, True), True)])
    dxs, dbm, dcm, ddt4, ddtb, dalog, ddsk = _ssd_bwd(s['xc'], s['dt4'], p['dt_bias8'], p['a_log8'], p['d8'],
                                                      s['s_in'], dy_ssd, name="ssd_scan_bwd")
    dxc = jnp.concatenate([dxs, dbm, dcm], axis=1)
    dxbc, dconv_w, dconv_b = _conv_bwd(proj, XBC0, p['conv_w'], p['conv_b'], dxc, name="ssd_conv_bwd")
    ddtr = ddt4[:, :, :8].transpose(1, 0, 2).reshape(t, SSD_HEADS)
    dy_bp = _perm(dy_b)
    dybin = _mm(dy_bp, p['w_b'], tb=True, name="mix_b_dx")
    grad_mm(s['ybin'], dy_bp, 'w_branch_b', 'rows', "mix_b_dw")

    def glu_bwd(a, b, d):
        _, vjp = jax.vjp(lambda a_, b_: a_ * _sigmoid(b_), a, b)
        da, db = vjp(d)
        return jnp.concatenate([da, db], axis=1)

    dvg = _rows(glu_bwd, name="s5_glu_act_bwd", nrow=nrow,
                ins=[(s['vg'], _rspec(tm, S5_WIDTH, 0)), (s['vg'], _rspec(tm, S5_WIDTH, 1)), (dybin, _rspec(tm, S5_WIDTH))],
                outs=[(_sds((t, 2 * S5_WIDTH), BF16), _rspec(tm, 2 * S5_WIDTH), False)])[0]
    dy5g = _mm(dvg, p['w_glu'], tb=True, name="s5_glu_dx")
    grad_mm(s['y5g'], dvg, 's5_w_glu', 'cols', "s5_glu_dw")

    def act_bwd(a, b, d, g):
        _, vjp = jax.vjp(lambda a_, b_, d_: _gelu(a_ + d_ * b_), a, b, d)
        return vjp(g)

    dy5, du5_direct, ds5d = _rows(
        act_bwd, name="s5_act_bwd", nrow=nrow,
        ins=[(s['y5'], _rspec(tm, S5_WIDTH)), (s['u5'], _rspec(tm, S5_WIDTH)), (p['s5_d'], _bspec(S5_WIDTH)),
             (dy5g, _rspec(tm, S5_WIDTH))],
        outs=[(_sds((t, S5_WIDTH), BF16), _rspec(tm, S5_WIDTH), False), (_sds((t, S5_WIDTH)), _rspec(tm, S5_WIDTH), False),
              (_sds((1, S5_WIDTH)), _bspec(S5_WIDTH), True)])
    dsdr = _gmm([(dy5, s['cre'], 1)], trans_w=True, name="s5_out_dx_re")
    dsdi = _gmm([(dy5, s['cim'], -1)], trans_w=True, name="s5_out_dx_im")
    dcre = _gmm_tn(s['sr'], dy5, 1, name="s5_out_dw_re")
    dcim = _gmm_tn(s['si'], dy5, -1, name="s5_out_dw_im")
    gr, gi, dlr, dli = _scan_bwd(dsdr, dsdi, s['sr'], s['si'], s['lr'], s['li'], name="s5_scan_bwd")
    dbre = _gmm_tn(s['u5'], gr, 1, name="s5_bu_dw_re")
    dbim = _gmm_tn(s['u5'], gi, 1, name="s5_bu_dw_im")
    du5 = _gmm([(gr, s['bre'], 1), (gi, s['bim'], 1)], trans_w=True, add=du5_direct, name="s5_bu_dx", out_dtype=BF16)
    du_p = _mm(du5, p['w_u5'], tb=True, name="mix_in_s5_dx")
    dw_u5 = _mm(s['u_p'], du5, ta=True, name="mix_in_s5_dw", out_dtype=BF16)
    ext_b = lambda m: _blockdiag_extract(m, S5_GROUP, S5_STATE).transpose(0, 2, 1).reshape(S5_GROUPS, S5_STATE * S5_GROUP)
    dlam_re, dlam_im, dlog_step, db_re, db_im = _s5_params_bwd(
        p['lam_re'], p['lam_im'], p['log_step'], p['b_re'], p['b_im'],
        (dlr.reshape(S5_GROUPS, S5_STATE), dli.reshape(S5_GROUPS, S5_STATE), ext_b(dbre), ext_b(dbim)), name="s5_par_bwd")
    dc_re = _blockdiag_extract(dcre, S5_STATE, S5_GROUP).transpose(0, 2, 1)
    dc_im = _blockdiag_extract(dcim, S5_STATE, S5_GROUP).transpose(0, 2, 1)
    dproj = jnp.concatenate([dz, dxbc, dgab], axis=1)
    du_big = _mm(dproj, p['w_big'], tb=True, name="mix_in_dx")
    du_dt = _mm(ddtr, p['w_dt'], tb=True, name="mix_in_dt_dx")
    dw_big = _mm(s['u'], dproj, ta=True, name="mix_in_dw", out_dtype=BF16)
    dw_dt = _mm(s['u'], ddtr, ta=True, name="mix_in_dt_dw", out_dtype=BF16)
    dh_in, dpre = _rms_bwd(s['h'], p['pre_g'], dh, [du_big, du_dt, _unperm(du_p)], name="mix_rms_bwd", tm=tm)
    dw_in = jnp.concatenate([dw_big[:, :GA0], dw_dt, dw_u5, dw_big[:, GA0:]], axis=1)
    bufs.setdefault('w_in_layers', {})[layer] = dw_in.reshape(D_MODEL, N_DEV, -1).transpose(1, 0, 2)
    grads = {
        'mix_pre_g': dpre, 'mix_post_g': dpost, 'ssd_conv_w': dconv_w, 'ssd_conv_b': dconv_b,
        'ssd_dt_bias': ddtb[:, 0, :8].reshape(-1), 'ssd_a_log': dalog[:, 0, :8].reshape(-1),
        'ssd_d': ddsk[:, 0, :8].reshape(-1), 'ssd_norm_g': dnorm,
        's5_lambda_re': dlam_re, 's5_lambda_im': dlam_im,
        's5_b_re': db_re.reshape(S5_GROUPS, S5_STATE, S5_GROUP), 's5_b_im': db_im.reshape(S5_GROUPS, S5_STATE, S5_GROUP),
        's5_c_re': dc_re, 's5_c_im': dc_im, 's5_log_step': dlog_step.reshape(-1), 's5_d': ds5d,
    }
    return dh_in, grads


HBM_SPEC = pl.BlockSpec(memory_space=pltpu.HBM)


def _place():
    return lax.axis_index("x"), lax.axis_index("y"), lax.axis_index("c")


def _all_gather(shards, *, name):
    n = len(shards)

    def body(*refs):
        x_refs, out_refs = refs[:n], refs[n:2 * n]
        send_sems, recv_sems, local_sems = refs[2 * n:]
        x, y, c = _place()
        me, sibling = (x, y, c), (x, y, 1 - c)
        chips = [(1 - x, y), (x, 1 - y), (1 - x, 1 - y)]

        def slot(o, px, py, pc):
            return out_refs[o].at[4 * px + 2 * py + pc]

        def copy(o, k, block, to, src=None):
            return pltpu.make_async_remote_copy(
                src_ref=slot(o, *block) if src is None else src, dst_ref=slot(o, *block),
                send_sem=send_sems.at[7 * o + k], recv_sem=recv_sems.at[7 * o + k], device_id=to, device_id_type=MESH)

        mine = [pltpu.make_async_copy(x_refs[o], slot(o, *me), local_sems.at[o]) for o in range(n)]
        for cp in mine:
            cp.start()
        first = []
        for j, chip in enumerate(chips):
            first += [copy(o, 1 + j, me, (*chip, c), src=x_refs[o]) for o in range(n)]
        first += [copy(o, 0, me, sibling, src=x_refs[o]) for o in range(n)]
        for cp in first:
            cp.start()
        passed = []
        for j, chip in enumerate(chips):
            for o in range(n):
                copy(o, 1 + j, (*chip, c), me).wait_recv()
                passed.append(copy(o, 4 + j, (*chip, c), sibling))
                passed[-1].start()
        for o in range(n):
            copy(o, 0, sibling, me).wait_recv()
        for j, chip in enumerate(chips):
            for o in range(n):
                copy(o, 4 + j, (*chip, 1 - c), me).wait_recv()
        for cp in first + passed:
            cp.wait_send()
        for cp in mine:
            cp.wait()

    return _pcall(
        body, name=name, in_specs=[HBM_SPEC] * n, out_specs=[HBM_SPEC] * n,
        out_shape=[jax.ShapeDtypeStruct((N_DEV,) + s.shape, s.dtype) for s in shards],
        scratch_shapes=[pltpu.SemaphoreType.DMA((7 * n,)), pltpu.SemaphoreType.DMA((7 * n,)), pltpu.SemaphoreType.DMA((n,))],
    )(*shards)


N_CHIPS = 4


def _exchange_sibling(grads, *, name):
    n = len(grads)

    def body(*refs):
        p_refs, q_refs = refs[:n], refs[n:2 * n]
        send_sems, recv_sems = refs[2 * n:]
        x, y, c = _place()
        copies = [pltpu.make_async_remote_copy(
            src_ref=p_refs[o].at[k, 1 - c], dst_ref=q_refs[o].at[k], send_sem=send_sems.at[N_CHIPS * o + k],
            recv_sem=recv_sems.at[N_CHIPS * o + k], device_id=(x, y, 1 - c), device_id_type=MESH)
            for o in range(n) for k in range(N_CHIPS)]
        for cp in copies:
            cp.start()
        for cp in copies:
            cp.wait()

    return _pcall(
        body, name=name, in_specs=[HBM_SPEC] * n, out_specs=[HBM_SPEC] * n,
        out_shape=[jax.ShapeDtypeStruct((N_CHIPS,) + g.shape[2:], g.dtype) for g in grads],
        scratch_shapes=[pltpu.SemaphoreType.DMA((N_CHIPS * n,)), pltpu.SemaphoreType.DMA((N_CHIPS * n,))],
    )(*grads)


def _pair_sum(own, got, *, name):
    _, _, r, l = own.shape
    tr = _tile(r, 512, 16)
    c = lax.axis_index("c").astype(jnp.int32).reshape(1)

    def body(c_ref, p_ref, q_ref, o_ref):
        o_ref[...] = (p_ref[...].astype(F32) + q_ref[...].astype(F32)).astype(o_ref.dtype)

    return _pcall(
        body, name=name,
        grid_spec=pltpu.PrefetchScalarGridSpec(
            num_scalar_prefetch=1, grid=(N_CHIPS, r // tr),
            in_specs=[pl.BlockSpec((None, None, tr, l), lambda k, i, cr: (k, cr[0], i, 0)),
                      pl.BlockSpec((None, tr, l), lambda k, i, cr: (k, i, 0))],
            out_specs=pl.BlockSpec((None, tr, l), lambda k, i, cr: (k, i, 0))),
        out_shape=jax.ShapeDtypeStruct((N_CHIPS, r, l), own.dtype),
        compiler_params=_params("parallel", "parallel"),
    )(c, own, got)


def _exchange_chips(parts, *, name):
    n = len(parts)

    def body(*refs):
        p_refs, g_refs = refs[:n], refs[n:2 * n]
        send_sems, recv_sems, local_sems = refs[2 * n:]
        x, y, c = _place()
        mine = 2 * x + y
        own = [pltpu.make_async_copy(p_refs[o].at[mine], g_refs[o].at[mine], local_sems.at[o]) for o in range(n)]
        for cp in own:
            cp.start()
        copies = []
        for j, (px, py) in enumerate([(1 - x, y), (x, 1 - y), (1 - x, 1 - y)]):
            copies += [pltpu.make_async_remote_copy(
                src_ref=p_refs[o].at[2 * px + py], dst_ref=g_refs[o].at[mine], send_sem=send_sems.at[3 * o + j],
                recv_sem=recv_sems.at[3 * o + j], device_id=(px, py, c), device_id_type=MESH) for o in range(n)]
        for cp in copies:
            cp.start()
        for cp in copies:
            cp.wait()
        for cp in own:
            cp.wait()

    return _pcall(
        body, name=name, in_specs=[HBM_SPEC] * n, out_specs=[HBM_SPEC] * n,
        out_shape=[jax.ShapeDtypeStruct(p.shape, p.dtype) for p in parts],
        scratch_shapes=[pltpu.SemaphoreType.DMA((3 * n,)), pltpu.SemaphoreType.DMA((3 * n,)), pltpu.SemaphoreType.DMA((n,))],
    )(*parts)


def _sum_slots(g, *, name):
    n, r, l = g.shape
    tr = _tile(r, 512, 16)

    def body(g_ref, o_ref):
        acc = g_ref[0].astype(F32)
        for k in range(1, n):
            acc = acc + g_ref[k].astype(F32)
        o_ref[...] = acc

    return _pcall(
        body, name=name, grid=(r // tr,), in_specs=[pl.BlockSpec((n, tr, l), lambda i: (0, i, 0))],
        out_specs=pl.BlockSpec((tr, l), lambda i: (i, 0)), out_shape=_sds((r, l)),
        compiler_params=_params("parallel"),
    )(g)


def _gather_weights(w):
    names = list(SHARDED_ORDER)
    shards = [w[n] if n == 'ssd_conv_w' else w[n].astype(BF16) for n in names]
    return dict(zip(names, _all_gather(shards, name="gather_weights")))


def _scatter_grads(bufs):
    names = [n for n in SHARDED_ORDER if n != 'ssd_conv_w']
    own = [bufs[n].reshape((N_CHIPS, 2) + bufs[n].shape[1:]) for n in names]
    got = _exchange_sibling(own, name="reduce_sibling")
    flat = lambda a, lead: a.reshape(lead + (-1, a.shape[-1]))
    parts = [_pair_sum(flat(o, (N_CHIPS, 2)), flat(g, (N_CHIPS,)), name=f"reduce_pair_sum_{n}").reshape(g.shape)
             for n, o, g in zip(names, own, got)]
    return dict(zip(names, _exchange_chips(parts, name="reduce_chips")))


def _reduce_small(grads, shapes):
    flat = jnp.concatenate([grads[n].astype(F32).reshape(-1) for n in SMALL_ORDER])
    pad = (-flat.shape[0]) % (8 * LANES)
    flat = jnp.concatenate([flat, jnp.zeros((pad,), F32)]).reshape(-1, LANES)
    gathered = _all_gather([flat], name="gather_small_grads")[0]
    total = _sum_slots(gathered, name="sum_small_grads").reshape(-1)
    out, o = {}, 0
    for n in SMALL_ORDER:
        size = int(np.prod(shapes[n]))
        out[n] = total[o:o + size].reshape(shapes[n])
        o += size
    return out


def _adamw(w, g, m, v, *, name, slots=False):
    shape = w.shape
    lanes = shape[-1] if (shape[-1] >= 128 or w.size % LANES) else LANES
    as2d = lambda a: a.reshape(-1, lanes)
    w2, m2, v2 = as2d(w), as2d(m), as2d(v)
    g2 = g.reshape(N_CHIPS, -1, lanes) if slots else as2d(g)
    r = w2.shape[0]
    tr = _tile(r, 256, 16 if slots else 8)

    def body(w_ref, g_ref, m_ref, v_ref, go_ref, d_ref, mo_ref, vo_ref):
        if slots:
            gg = g_ref[0].astype(F32)
            for k in range(1, N_CHIPS):
                gg = gg + g_ref[k].astype(F32)
        else:
            gg = g_ref[...]
        go_ref[...] = gg
        mn = ADAM_B1 * m_ref[...] + (1.0 - ADAM_B1) * gg
        vn = ADAM_B2 * v_ref[...] + (1.0 - ADAM_B2) * (gg * gg)
        m_hat = mn / (1.0 - ADAM_B1 ** ADAM_STEP)
        v_hat = vn / (1.0 - ADAM_B2 ** ADAM_STEP)
        d_ref[...] = -ADAM_LR * (m_hat / (jnp.sqrt(v_hat) + ADAM_EPS) + ADAM_WD * w_ref[...])
        mo_ref[...] = mn
        vo_ref[...] = vn

    spec = pl.BlockSpec((tr, lanes), lambda i: (i, 0))
    g_spec = pl.BlockSpec((N_CHIPS, tr, lanes), lambda i: (0, i, 0)) if slots else spec
    res = _pcall(
        body, name=name, grid=(r // tr,), in_specs=[spec, g_spec, spec, spec], out_specs=[spec] * 4,
        out_shape=[_sds((r, lanes))] * 4, compiler_params=_params("parallel"),
    )(w2, g2, m2, v2)
    return tuple(a.reshape(shape) for a in res)


def _layer_params(w, g, i):
    row = lambda a: a.astype(F32).reshape(1, -1)
    head8 = lambda a: jnp.broadcast_to(
        jnp.pad(a.astype(F32).reshape(SSD_GROUPS, 1, 8), ((0, 0), (0, 0), (0, PAD_HEADS - 8))), (SSD_GROUPS, 8, PAD_HEADS))
    by_rows = lambda n: g[n][:, i].reshape(-1, g[n].shape[-1])
    by_cols = lambda n: g[n][:, i].transpose(1, 0, 2).reshape(g[n].shape[2], -1)
    w_in = by_cols('w_in')
    s = np.cumsum([SSD_D_INNER, SSD_CONV_DIM, SSD_HEADS, S5_WIDTH, D_MODEL])
    ffn = lambda k: dict(layer=i, pre_g=row(w[f'{k}_pre_g'][i]), post_g=row(w[f'{k}_post_g'][i]),
                         w_gate=g[f'{k}_w_gate'], w_up=g[f'{k}_w_up'], w_down=g[f'{k}_w_down'])
    mix = dict(
        layer=i, pre_g=row(w['mix_pre_g'][i]), post_g=row(w['mix_post_g'][i]),
        w_big=jnp.concatenate([w_in[:, :s[1]], w_in[:, s[3]:]], axis=1), w_dt=w_in[:, s[1]:s[2]], w_u5=w_in[:, s[2]:s[3]],
        conv_w=by_cols('ssd_conv_w'), conv_b=row(w['ssd_conv_b'][i]),
        dt_bias8=head8(w['ssd_dt_bias'][i]), a_log8=head8(w['ssd_a_log'][i]), d8=head8(w['ssd_d'][i]),
        norm_g=row(w['ssd_norm_g'][i]), w_a=by_rows('w_branch_a'),
        lam_re=w['s5_lambda_re'][i], lam_im=w['s5_lambda_im'][i], log_step=w['s5_log_step'][i].reshape(S5_GROUPS, 1),
        b_re=w['s5_b_re'][i].reshape(S5_GROUPS, -1), b_im=w['s5_b_im'][i].reshape(S5_GROUPS, -1),
        c_re=w['s5_c_re'][i], c_im=w['s5_c_im'][i], s5_d=row(w['s5_d'][i]),
        w_glu=by_cols('s5_w_glu'), w_b=by_rows('w_branch_b'), w_out=by_rows('w_out'))
    return ffn('ffn1'), mix, ffn('ffn2')


def _loss_head(h, target, *, tm):
    t, d = h.shape

    def fn(y, tgt):
        err = y - tgt
        return err * (1.0 / d), jnp.sum(0.5 * jnp.sum(err * err, axis=-1, keepdims=True) * (1.0 / d), axis=0, keepdims=True)

    dy, loss = _rows(fn, name="loss_head", nrow=t // tm,
                     ins=[(h, _rspec(tm, d)), (target, _rspec(tm, d))],
                     outs=[(_sds((t, d)), _rspec(tm, d), False), (_sds((1, 128)), _bspec(128), True)])
    return dy, loss[0, 0]


def _forward_backward(h, target, w, g):
    t = h.shape[0]
    tm = _tile(t, 256, 8)
    layers = [_layer_params(w, g, i) for i in range(DEPTH)]
    saved = []
    for p1, pm, p2 in layers:
        h, s1 = _ffn_fwd(h, p1, "ffn1", tm)
        h, sm = _mixer_fwd(h, pm, tm)
        h, s2 = _ffn_fwd(h, p2, "ffn2", tm)
        saved.append((s1, sm, s2))
    dh, loss = _loss_head(h, target, tm=tm)
    bufs, small = {}, [None] * DEPTH
    for i in reversed(range(DEPTH)):
        (p1, pm, p2), (s1, sm, s2) = layers[i], saved[i]
        dh, g2 = _ffn_bwd(dh, p2, s2, "ffn2", tm, bufs)
        dh, gm = _mixer_bwd(dh, pm, sm, tm, bufs)
        dh, g1 = _ffn_bwd(dh, p1, s1, "ffn1", tm, bufs)
        small[i] = {**gm, **g1, **g2}
    w_in_layers = bufs.pop('w_in_layers')
    bufs['w_in'] = jnp.stack([w_in_layers[i] for i in range(DEPTH)], axis=1)
    shapes = _small_shapes(w)
    stacked = {n: jnp.stack([small[i][n].reshape(shapes[n][1:]) for i in range(DEPTH)]) for n in SMALL_ORDER}
    return loss, dh, bufs, stacked


def _small_shapes(w):
    return {n: (w[n].shape[:-1] + (SSD_CONV_DIM,) if n == 'ssd_conv_w' else w[n].shape) for n in SMALL_ORDER}


def kernel(*args):
    n_w = len(WEIGHTS)
    x, target = args[0], args[1 + n_w]
    w = dict(zip(WEIGHTS, args[1:1 + n_w]))
    m = dict(zip(WEIGHTS, args[2 + n_w:2 + 2 * n_w]))
    v = dict(zip(WEIGHTS, args[2 + 2 * n_w:2 + 3 * n_w]))
    t = x.shape[1]

    g = _gather_weights(w)
    loss_local, dx, bufs, small = _forward_backward(x.reshape(t, D_MODEL), target.reshape(t, D_MODEL), w, g)
    loss = lax.psum(loss_local, ("x", "y", "c"))
    slots = _scatter_grads(bufs)
    small = _reduce_small(small, _small_shapes(w))
    me = 4 * lax.axis_index("x") + 2 * lax.axis_index("y") + lax.axis_index("c")
    cols = w['ssd_conv_w'].shape[-1]
    small['ssd_conv_w'] = lax.dynamic_slice_in_dim(small['ssd_conv_w'], me * cols, cols, axis=2)

    grad, delta, new_m, new_v = {}, {}, {}, {}
    for n in WEIGHTS:
        sharded = n in slots
        grad[n], delta[n], new_m[n], new_v[n] = _adamw(w[n], slots[n] if sharded else small[n], m[n], v[n],
                                                       name=f"adamw_{n}", slots=sharded)
    return (loss, dx.reshape(x.shape), *[grad[n] for n in WEIGHTS], *[delta[n] for n in WEIGHTS],
            *[new_m[n] for n in WEIGHTS], *[new_v[n] for n in WEIGHTS])
```

```python
import functools
import math

import numpy as np
import jax
import jax.numpy as jnp
from jax import lax
from jax.experimental import pallas as pl
from jax.experimental.pallas import tpu as pltpu
from jax.experimental.pallas import tpu_sc as plsc

F32 = jnp.float32
BF16 = jnp.bfloat16
MESH = pl.DeviceIdType.MESH
HIGHEST = lax.Precision.HIGHEST

D_MODEL = 1024
DEPTH = 2
FFN_HIDDEN = 2816
SSD_D_INNER = 2048
SSD_HEADS = 32
SSD_HEAD_DIM = 64
SSD_GROUPS = 4
SSD_STATE = 128
SSD_CHUNK = 128
SSD_CONV_DIM = 3072
SSD_CONV_WIDTH = 4
S5_WIDTH = 1024
S5_GROUP = 16
S5_GROUPS = 64
S5_STATE = 64
S5_MAX_REAL = -1e-4
S5_BLOCKS = 8
RMS_EPS = 1e-6
N_DEV = 8
LANES = 1024

ADAM_LR = 0.001
ADAM_B1 = 0.9
ADAM_B2 = 0.999
ADAM_EPS = 1e-08
ADAM_WD = 0.01
ADAM_STEP = 10

VMEM_LIMIT_BYTES = 48 * 1024 * 1024

WEIGHTS = ['ffn1_pre_g', 'ffn1_post_g', 'ffn1_w_gate', 'ffn1_w_up', 'ffn1_w_down', 'mix_pre_g', 'mix_post_g',
           'w_in', 'ssd_conv_w', 'ssd_conv_b', 'ssd_dt_bias', 'ssd_a_log', 'ssd_d', 'ssd_norm_g', 'w_branch_a',
           's5_lambda_re', 's5_lambda_im', 's5_b_re', 's5_b_im', 's5_c_re', 's5_c_im', 's5_log_step', 's5_d',
           's5_w_glu', 'w_branch_b', 'w_out', 'ffn2_pre_g', 'ffn2_post_g', 'ffn2_w_gate', 'ffn2_w_up',
           'ffn2_w_down']
SHARDED = {'ffn1_w_gate': 2, 'ffn1_w_up': 2, 'ffn1_w_down': 1, 'w_in': 2, 'ssd_conv_w': 2, 'w_branch_a': 1,
           's5_w_glu': 2, 'w_branch_b': 1, 'w_out': 1, 'ffn2_w_gate': 2, 'ffn2_w_up': 2, 'ffn2_w_down': 1}
SHARDED_ORDER = [n for n in WEIGHTS if n in SHARDED]
SMALL_ORDER = [n for n in WEIGHTS if n not in SHARDED or n == 'ssd_conv_w']


def _pcall(body, **kw):
    return pl.pallas_call(body, **kw)


def _scall(body, *, name, out_type, scratch_types, collective_id):
    return pl.kernel(body, out_type=out_type, mesh=plsc.ScalarSubcoreMesh(axis_name="sequencer", num_cores=1),
                     scratch_types=scratch_types, name=name,
                     compiler_params=pltpu.CompilerParams(collective_id=collective_id))


def _params(*sem):
    return pltpu.CompilerParams(dimension_semantics=sem, vmem_limit_bytes=VMEM_LIMIT_BYTES)


def _tile(n, pref, align=128):
    if n <= pref:
        return n
    t = (pref // align) * align
    while t >= align:
        if n % t == 0:
            return t
        t -= align
    return n


def _rms(x, g):
    return x * lax.rsqrt(jnp.mean(x * x, axis=-1, keepdims=True) + RMS_EPS) * g


def _sigmoid(x):
    return 1.0 / (1.0 + jnp.exp(-x))


def _silu(x):
    return x * _sigmoid(x)


def _gelu(x):
    return 0.5 * x * (1.0 + jnp.tanh(math.sqrt(2.0 / math.pi) * (x + 0.044715 * (x * x * x))))


def _softplus(x):
    return jnp.maximum(x, 0.0) + jnp.log(1.0 + jnp.exp(-jnp.abs(x)))


def _dot(a, b, dims):
    return lax.dot_general(a.astype(BF16), b.astype(BF16), (dims, ((), ())), preferred_element_type=F32)


_NN = ((1,), (0,))
_NT = ((1,), (1,))
_TN = ((0,), (0,))


@jax.custom_vjp
def _bdot_nn(a, b):
    return _dot(a, b, _NN)


_bdot_nn.defvjp(lambda a, b: (_dot(a, b, _NN), (a, b)),
                lambda r, g: (_dot(g, r[1], _NT), _dot(r[0], g, _TN)))


@jax.custom_vjp
def _bdot_nt(a, b):
    return _dot(a, b, _NT)


_bdot_nt.defvjp(lambda a, b: (_dot(a, b, _NT), (a, b)),
                lambda r, g: (_dot(g, r[1], _NN), _dot(g, r[0], _TN)))


@jax.custom_vjp
def _bdot_tn(a, b):
    return _dot(a, b, _TN)


_bdot_tn.defvjp(lambda a, b: (_dot(a, b, _TN), (a, b)),
                lambda r, g: (_dot(r[1], g, _NT), _dot(r[0], g, _NN)))


def _fdot(a, b, dims=_NN):
    return lax.dot_general(a, b, (dims, ((), ())), precision=HIGHEST, preferred_element_type=F32)


def _mm(a, b, *, name, ta=False, tb=False, out_dtype=F32, tm=512, tn=512, tk=2048, shards=None):
    m, k = (a.shape[1], a.shape[0]) if ta else a.shape
    n = b.shape[0] if tb else b.shape[1]
    assert k == (b.shape[1] if tb else b.shape[0]), (a.shape, b.shape, ta, tb)
    if shards == 'rows':
        tm = min(tm, m // N_DEV)
    if shards == 'cols':
        tn = n // N_DEV
    tm, tn, tk = _tile(m, tm), _tile(n, tn), _tile(k, tk)
    nk = k // tk
    a_spec = pl.BlockSpec((tk, tm), lambda i, j, kk: (kk, i)) if ta else pl.BlockSpec((tm, tk), lambda i, j, kk: (i, kk))
    b_spec = pl.BlockSpec((tn, tk), lambda i, j, kk: (j, kk)) if tb else pl.BlockSpec((tk, tn), lambda i, j, kk: (kk, j))
    dims = ((0 if ta else 1,), (1 if tb else 0,))
    out_spec = pl.BlockSpec((tm, tn), lambda i, j, kk: (i, j))
    out_shape = jax.ShapeDtypeStruct((m, n), out_dtype)
    if shards == 'rows':
        per = m // N_DEV // tm
        out_shape = jax.ShapeDtypeStruct((N_DEV, 1, m // N_DEV, n), out_dtype)
        out_spec = pl.BlockSpec((None, None, tm, tn), lambda i, j, kk: (i // per, 0, i % per, j))
    elif shards == 'cols':
        out_shape = jax.ShapeDtypeStruct((N_DEV, 1, m, n // N_DEV), out_dtype)
        out_spec = pl.BlockSpec((None, None, tm, tn), lambda i, j, kk: (j, 0, i, 0))

    def body(a_ref, b_ref, o_ref, acc_ref):
        kk = pl.program_id(2)

        @pl.when(kk == 0)
        def _():
            acc_ref[...] = jnp.zeros_like(acc_ref)

        acc_ref[...] += _dot(a_ref[...], b_ref[...], dims)

        @pl.when(kk == nk - 1)
        def _():
            o_ref[...] = acc_ref[...].astype(o_ref.dtype)

    return _pcall(
        body, name=name, grid=(m // tm, n // tn, nk),
        in_specs=[a_spec, b_spec], out_specs=out_spec, out_shape=out_shape,
        scratch_shapes=[pltpu.VMEM((tm, tn), F32)],
        compiler_params=_params("parallel", "parallel", "arbitrary"),
    )(a, b)


def _rspec(tm, w, cb=0, percol=False):
    return pl.BlockSpec((tm, w), (lambda j, i: (i, cb + j)) if percol else (lambda j, i: (i, cb)))


def _bspec(w, cb=0, percol=False, rows=1):
    return pl.BlockSpec((rows, w), (lambda j, i: (0, cb + j)) if percol else (lambda j, i: (0, cb)))


def _rows(fn, *, name, nrow, ncol=1, ins, outs):
    n_in = len(ins)
    accs = [o[2] for o in outs]

    def body(*refs):
        vals = fn(*[r[...] for r in refs[:n_in]])
        if not isinstance(vals, (tuple, list)):
            vals = (vals,)
        i = pl.program_id(1)
        for ref, val, acc in zip(refs[n_in:], vals, accs):
            if acc:
                @pl.when(i == 0)
                def _(ref=ref):
                    ref[...] = jnp.zeros_like(ref)

                ref[...] += jnp.broadcast_to(val, ref.shape).astype(ref.dtype)
            else:
                ref[...] = val.astype(ref.dtype)

    res = _pcall(
        body, name=name, grid=(ncol, nrow),
        in_specs=[s for _, s in ins], out_specs=[o[1] for o in outs], out_shape=[o[0] for o in outs],
        compiler_params=_params("parallel", "arbitrary"),
    )(*[a for a, _ in ins])
    return res


def _sds(shape, dtype=F32):
    return jax.ShapeDtypeStruct(shape, dtype)


def _rms_fwd(h, g, *, name, tm):
    t, d = h.shape
    return _rows(lambda x, gg: _rms(x, gg), name=name, nrow=t // tm,
                 ins=[(h, _rspec(tm, d)), (g, _bspec(d))],
                 outs=[(_sds((t, d), BF16), _rspec(tm, d), False)])[0]


def _resid_fwd(h, f, g, scale, *, name, tm):
    t, d = h.shape
    return _rows(lambda x, ff, gg: x + scale * _rms(ff, gg), name=name, nrow=t // tm,
                 ins=[(h, _rspec(tm, d)), (f, _rspec(tm, d)), (g, _bspec(d))],
                 outs=[(_sds((t, d)), _rspec(tm, d), False)])[0]


def _resid_bwd(f, g, dh, scale, *, name, tm):
    t, d = f.shape

    def fn(ff, gg, dd):
        _, vjp = jax.vjp(lambda a, b: scale * _rms(a, b), ff, gg)
        return vjp(dd)

    return _rows(fn, name=name, nrow=t // tm,
                 ins=[(f, _rspec(tm, d)), (g, _bspec(d)), (dh, _rspec(tm, d))],
                 outs=[(_sds((t, d), BF16), _rspec(tm, d), False), (_sds((1, d)), _bspec(d), True)])


def _rms_bwd(h, g, dh, dxns, *, name, tm):
    t, d = h.shape

    def fn(x, gg, dd, *dx):
        _, vjp = jax.vjp(_rms, x, gg)
        tot = dx[0]
        for more in dx[1:]:
            tot = tot + more
        dxx, dg = vjp(tot)
        return dd + dxx, dg

    return _rows(fn, name=name, nrow=t // tm,
                 ins=[(h, _rspec(tm, d)), (g, _bspec(d)), (dh, _rspec(tm, d))] + [(x, _rspec(tm, d)) for x in dxns],
                 outs=[(_sds((t, d)), _rspec(tm, d), False), (_sds((1, d)), _bspec(d), True)])


NB = FFN_HIDDEN // N_DEV
MM_ROWS = 512


def _ffn_up(xn, wg, wu, *, name):
    t = xn.shape[0]
    tm = _tile(t, MM_ROWS)
    wspec = pl.BlockSpec((None, None, D_MODEL, NB), lambda i, j: (j, 0, 0, 0))

    def body(x_ref, g_ref, u_ref, ab_ref, hh_ref):
        x = x_ref[...]
        a, b = _dot(x, g_ref[...], _NN), _dot(x, u_ref[...], _NN)
        ab_ref[0] = a
        ab_ref[1] = b
        hh_ref[...] = (_silu(a) * b).astype(hh_ref.dtype)

    return _pcall(
        body, name=name, grid=(t // tm, N_DEV),
        in_specs=[pl.BlockSpec((tm, D_MODEL), lambda i, j: (i, 0)), wspec, wspec],
        out_specs=[pl.BlockSpec((None, 2, tm, NB), lambda i, j: (j, 0, i, 0)),
                   pl.BlockSpec((None, tm, NB), lambda i, j: (j, i, 0))],
        out_shape=[_sds((N_DEV, 2, t, NB)), _sds((N_DEV, t, NB), BF16)],
        compiler_params=_params("parallel", "parallel"),
    )(xn, wg, wu)


def _ffn_down(hh, wd, *, name, tn=512):
    t = hh.shape[1]
    tm = _tile(t, MM_ROWS)

    def body(h_ref, w_ref, o_ref, acc_ref):
        kk = pl.program_id(2)

        @pl.when(kk == 0)
        def _():
            acc_ref[...] = jnp.zeros_like(acc_ref)

        acc_ref[...] += _dot(h_ref[...], w_ref[...], _NN)

        @pl.when(kk == N_DEV - 1)
        def _():
            o_ref[...] = acc_ref[...]

    return _pcall(
        body, name=name, grid=(t // tm, D_MODEL // tn, N_DEV),
        in_specs=[pl.BlockSpec((None, tm, NB), lambda i, j, kk: (kk, i, 0)),
                  pl.BlockSpec((None, None, NB, tn), lambda i, j, kk: (kk, 0, 0, j))],
        out_specs=pl.BlockSpec((tm, tn), lambda i, j, kk: (i, j)), out_shape=_sds((t, D_MODEL)),
        scratch_shapes=[pltpu.VMEM((tm, tn), F32)],
        compiler_params=_params("parallel", "parallel", "arbitrary"),
    )(hh, wd)


def _ffn_down_dx(df, wd, ab, *, name):
    t = df.shape[0]
    tm = _tile(t, MM_ROWS)

    def body(d_ref, w_ref, ab_ref, o_ref):
        dhh = _dot(d_ref[...], w_ref[...], _NT)
        _, vjp = jax.vjp(lambda a, b: _silu(a) * b, ab_ref[0], ab_ref[1])
        da, db = vjp(dhh)
        o_ref[0] = da.astype(o_ref.dtype)
        o_ref[1] = db.astype(o_ref.dtype)

    blk = pl.BlockSpec((None, 2, tm, NB), lambda i, j: (j, 0, i, 0))
    return _pcall(
        body, name=name, grid=(t // tm, N_DEV),
        in_specs=[pl.BlockSpec((tm, D_MODEL), lambda i, j: (i, 0)),
                  pl.BlockSpec((None, None, NB, D_MODEL), lambda i, j: (j, 0, 0, 0)), blk],
        out_specs=blk, out_shape=_sds((N_DEV, 2, t, NB), BF16), compiler_params=_params("parallel", "parallel"),
    )(df, wd, ab)


def _ffn_down_dw(hh, df, *, name, tn=512):
    t = df.shape[0]
    tk = _tile(t, 2048)
    nk = t // tk

    def body(h_ref, d_ref, o_ref, acc_ref):
        kk = pl.program_id(2)

        @pl.when(kk == 0)
        def _():
            acc_ref[...] = jnp.zeros_like(acc_ref)

        acc_ref[...] += _dot(h_ref[...], d_ref[...], _TN)

        @pl.when(kk == nk - 1)
        def _():
            o_ref[...] = acc_ref[...].astype(o_ref.dtype)

    return _pcall(
        body, name=name, grid=(N_DEV, D_MODEL // tn, nk),
        in_specs=[pl.BlockSpec((None, tk, NB), lambda j, n, kk: (j, kk, 0)),
                  pl.BlockSpec((tk, tn), lambda j, n, kk: (kk, n))],
        out_specs=pl.BlockSpec((None, None, NB, tn), lambda j, n, kk: (j, 0, 0, n)),
        out_shape=_sds((N_DEV, 1, NB, D_MODEL), BF16),
        scratch_shapes=[pltpu.VMEM((NB, tn), F32)],
        compiler_params=_params("parallel", "parallel", "arbitrary"),
    )(hh, df)


def _ffn_up_dx(dab, wg, wu, *, name):
    t = dab.shape[2]
    tm = _tile(t, MM_ROWS)
    wspec = pl.BlockSpec((None, None, D_MODEL, NB), lambda i, j: (j, 0, 0, 0))

    def body(d_ref, g_ref, u_ref, o_ref):
        @pl.when(pl.program_id(1) == 0)
        def _():
            o_ref[...] = jnp.zeros_like(o_ref)

        o_ref[...] += _dot(d_ref[0], g_ref[...], _NT) + _dot(d_ref[1], u_ref[...], _NT)

    return _pcall(
        body, name=name, grid=(t // tm, N_DEV),
        in_specs=[pl.BlockSpec((None, 2, tm, NB), lambda i, j: (j, 0, i, 0)), wspec, wspec],
        out_specs=pl.BlockSpec((tm, D_MODEL), lambda i, j: (i, 0)), out_shape=_sds((t, D_MODEL)),
        compiler_params=_params("parallel", "arbitrary"),
    )(dab, wg, wu)


def _ffn_up_dw(xn, dab, *, name):
    t = xn.shape[0]
    tk = _tile(t, 2048)
    nk = t // tk

    def body(x_ref, d_ref, og_ref, ou_ref, accg_ref, accu_ref):
        kk = pl.program_id(1)

        @pl.when(kk == 0)
        def _():
            accg_ref[...] = jnp.zeros_like(accg_ref)
            accu_ref[...] = jnp.zeros_like(accu_ref)

        x = x_ref[...]
        accg_ref[...] += _dot(x, d_ref[0], _TN)
        accu_ref[...] += _dot(x, d_ref[1], _TN)

        @pl.when(kk == nk - 1)
        def _():
            og_ref[...] = accg_ref[...].astype(og_ref.dtype)
            ou_ref[...] = accu_ref[...].astype(ou_ref.dtype)

    out = pl.BlockSpec((None, None, D_MODEL, NB), lambda j, kk: (j, 0, 0, 0))
    return _pcall(
        body, name=name, grid=(N_DEV, nk),
        in_specs=[pl.BlockSpec((tk, D_MODEL), lambda j, kk: (kk, 0)),
                  pl.BlockSpec((None, 2, tk, NB), lambda j, kk: (j, 0, kk, 0))],
        out_specs=[out, out], out_shape=[_sds((N_DEV, 1, D_MODEL, NB), BF16)] * 2,
        scratch_shapes=[pltpu.VMEM((D_MODEL, NB), F32)] * 2,
        compiler_params=_params("parallel", "arbitrary"),
    )(xn, dab)


def _ffn_fwd(h, p, tag, tm):
    xn = _rms_fwd(h, p['pre_g'], name=f"{tag}_rms", tm=tm)
    ab, hh = _ffn_up(xn, p['w_gate'], p['w_up'], name=f"{tag}_up")
    f = _ffn_down(hh, p['w_down'], name=f"{tag}_down")
    out = _resid_fwd(h, f, p['post_g'], 0.5, name=f"{tag}_res", tm=tm)
    return out, (h, xn, ab, hh, f)


def _ffn_bwd(dh, p, saved, tag, tm):
    h, xn, ab, hh, f = saved
    df, dpost = _resid_bwd(f, p['post_g'], dh, 0.5, name=f"{tag}_res_bwd", tm=tm)
    dab = _ffn_down_dx(df, p['w_down'], ab, name=f"{tag}_down_dx")
    bufs = {f'{tag}_w_down': _ffn_down_dw(hh, df, name=f"{tag}_down_dw")}
    dxn = _ffn_up_dx(dab, p['w_gate'], p['w_up'], name=f"{tag}_up_dx")
    bufs[f'{tag}_w_gate'], bufs[f'{tag}_w_up'] = _ffn_up_dw(xn, dab, name=f"{tag}_up_dw")
    dh_in, dpre = _rms_bwd(h, p['pre_g'], dh, [dxn], name=f"{tag}_rms_bwd", tm=tm)
    return dh_in, bufs, {f'{tag}_pre_g': dpre, f'{tag}_post_g': dpost}


CONV_COLS = 256


def _shift_down(x, s):
    rows = lax.broadcasted_iota(jnp.int32, x.shape, 0)
    return jnp.where(rows >= s, pltpu.roll(x, s, axis=0), 0.0)


def _shift_up(x, s):
    t = x.shape[0]
    rows = lax.broadcasted_iota(jnp.int32, x.shape, 0)
    return jnp.where(rows < t - s, pltpu.roll(x, t - s, axis=0), 0.0)


def _conv_fwd(proj, col0, w, b, *, name):
    t = proj.shape[0]
    c = w.shape[1]
    cb0 = col0 // CONV_COLS

    def body(x_ref, w_ref, b_ref, o_ref):
        x = x_ref[...]
        acc = x * w_ref[3:4, :] + b_ref[...]
        for k in range(SSD_CONV_WIDTH - 1):
            acc = acc + _shift_down(x, SSD_CONV_WIDTH - 1 - k) * w_ref[k:k + 1, :]
        o_ref[...] = _silu(acc)

    return _pcall(
        body, name=name, grid=(c // CONV_COLS,),
        in_specs=[pl.BlockSpec((t, CONV_COLS), lambda j: (0, cb0 + j)),
                  pl.BlockSpec((SSD_CONV_WIDTH, CONV_COLS), lambda j: (0, j)),
                  pl.BlockSpec((1, CONV_COLS), lambda j: (0, j))],
        out_specs=pl.BlockSpec((t, CONV_COLS), lambda j: (0, j)),
        out_shape=_sds((t, c)), compiler_params=_params("parallel"),
    )(proj, w, b)


def _conv_bwd(proj, col0, w, b, dout, *, name):
    t = proj.shape[0]
    c = w.shape[1]
    cb0 = col0 // CONV_COLS

    def body(x_ref, w_ref, b_ref, d_ref, dx_ref, dw_ref, db_ref):
        x = x_ref[...]
        shifted = [_shift_down(x, SSD_CONV_WIDTH - 1 - k) for k in range(SSD_CONV_WIDTH - 1)] + [x]
        pre = b_ref[...] + shifted[3] * w_ref[3:4, :]
        for k in range(SSD_CONV_WIDTH - 1):
            pre = pre + shifted[k] * w_ref[k:k + 1, :]
        sg = _sigmoid(pre)
        dpre = d_ref[...] * (sg * (1.0 + pre * (1.0 - sg)))
        dx = dpre * w_ref[3:4, :]
        for k in range(SSD_CONV_WIDTH - 1):
            dx = dx + _shift_up(dpre, SSD_CONV_WIDTH - 1 - k) * w_ref[k:k + 1, :]
        dx_ref[...] = dx.astype(dx_ref.dtype)
        for k in range(SSD_CONV_WIDTH):
            dw_ref[k:k + 1, :] = jnp.sum(dpre * shifted[k], axis=0, keepdims=True)
        db_ref[...] = jnp.sum(dpre, axis=0, keepdims=True)

    return _pcall(
        body, name=name, grid=(c // CONV_COLS,),
        in_specs=[pl.BlockSpec((t, CONV_COLS), lambda j: (0, cb0 + j)),
                  pl.BlockSpec((SSD_CONV_WIDTH, CONV_COLS), lambda j: (0, j)),
                  pl.BlockSpec((1, CONV_COLS), lambda j: (0, j)),
                  pl.BlockSpec((t, CONV_COLS), lambda j: (0, j))],
        out_specs=[pl.BlockSpec((t, CONV_COLS), lambda j: (0, j)),
                   pl.BlockSpec((SSD_CONV_WIDTH, CONV_COLS), lambda j: (0, j)),
                   pl.BlockSpec((1, CONV_COLS), lambda j: (0, j))],
        out_shape=[_sds((t, c), BF16), _sds((SSD_CONV_WIDTH, c)), _sds((1, c))],
        compiler_params=_params("parallel"),
    )(proj, w, b, dout)


HALF = 256
HEADS_PER_HALF = 4
PAD_HEADS = 128


def _head_expanders():
    k = lax.broadcasted_iota(jnp.int32, (PAD_HEADS, HALF), 0)
    j = lax.broadcasted_iota(jnp.int32, (PAD_HEADS, HALF), 1)
    kt = lax.broadcasted_iota(jnp.int32, (HALF, PAD_HEADS), 1)
    jt = lax.broadcasted_iota(jnp.int32, (HALF, PAD_HEADS), 0)
    es, ets = [], []
    for half in range(2):
        es.append(jnp.where(k == j // SSD_HEAD_DIM + half * HEADS_PER_HALF, 1.0, 0.0).astype(F32))
        ets.append(jnp.where(kt == jt // SSD_HEAD_DIM + half * HEADS_PER_HALF, 1.0, 0.0).astype(F32))
    return es, ets


def _ssd_chunk(x_lo, x_hi, bm, cm, dtr, dtb8, alog8, dsk8, s_lo, s_hi):
    q = x_lo.shape[0]
    es, ets = _head_expanders()
    rowmean = lambda v: jnp.sum(v, axis=0, keepdims=True) * 0.125
    dt = _softplus(dtr + rowmean(dtb8))
    a = -jnp.exp(rowmean(alog8))
    adt = a * dt
    adt_tot8 = jnp.broadcast_to(jnp.sum(adt, axis=0, keepdims=True), (8, PAD_HEADS))
    ll = lax.broadcasted_iota(jnp.int32, (q, q), 0)
    ss = lax.broadcasted_iota(jnp.int32, (q, q), 1)
    ltri = jnp.where(ll >= ss, 1.0, 0.0).astype(F32)
    lane = lax.broadcasted_iota(jnp.int32, (1, HALF), 1)
    cb = _bdot_nt(cm, bm)
    outs = []
    for half, (x, s_in) in enumerate(((x_lo, s_lo), (x_hi, s_hi))):
        e, et = es[half], ets[half]
        dtf = _fdot(dt, e)
        af = _fdot(adt, e)
        dskf = rowmean(_fdot(dsk8, e))
        acum = _fdot(ltri, af)
        alast = jnp.sum(af, axis=0, keepdims=True)
        xdt = x * dtf
        ydiag = jnp.zeros((q, HALF), F32)
        for r in range(HEADS_PER_HALF):
            sel = lane == r * SSD_HEAD_DIM
            ac_r = jnp.sum(jnp.where(sel, acum, 0.0), axis=1, keepdims=True)
            a_r = jnp.sum(jnp.where(sel, af, 0.0), axis=1, keepdims=True)
            arow = jnp.sum(jnp.where(ll <= ss, a_r, 0.0), axis=0, keepdims=True)
            decay = jnp.exp(jnp.where(ll >= ss, ac_r - arow, -jnp.inf))
            yh = _bdot_nn(cb * decay, xdt)
            ydiag = ydiag + jnp.where(lane // SSD_HEAD_DIM == r, yh, 0.0)
        st = _bdot_tn(xdt * jnp.exp(alast - acum), bm)
        yoff = _bdot_nt(cm, s_in) * jnp.exp(acum)
        y = ydiag + yoff + dskf * x
        alast_col = jnp.sum(_fdot(et, adt_tot8, _NT), axis=1, keepdims=True) * 0.125
        outs.append((y, jnp.exp(alast_col) * s_in + st))
    return outs[0][0], outs[1][0], outs[0][1], outs[1][1]


def _ssd_specs(t, rev):
    q = SSD_CHUNK
    nc = t // q
    ci = (lambda c: nc - 1 - c) if rev else (lambda c: c)
    xcol0 = SSD_D_INNER // SSD_STATE
    return dict(
        x_lo=pl.BlockSpec((q, HALF), lambda g, c: (ci(c), 2 * g)),
        x_hi=pl.BlockSpec((q, HALF), lambda g, c: (ci(c), 2 * g + 1)),
        bm=pl.BlockSpec((q, SSD_STATE), lambda g, c: (ci(c), xcol0 + g)),
        cm=pl.BlockSpec((q, SSD_STATE), lambda g, c: (ci(c), xcol0 + SSD_GROUPS + g)),
        dt=pl.BlockSpec((None, q, PAD_HEADS), lambda g, c: (g, ci(c), 0)),
        par=pl.BlockSpec((None, 8, PAD_HEADS), lambda g, c: (g, 0, 0)),
        st=pl.BlockSpec((None, None, 2, HALF, SSD_STATE), lambda g, c: (ci(c), g, 0, 0, 0)),
        y=pl.BlockSpec((q, 2 * HALF), lambda g, c: (ci(c), g)),
        grp=pl.BlockSpec((q, SSD_STATE), lambda g, c: (ci(c), g)),
    )


def _ssd_fwd(xc, dt4, dtb, alog, dsk, *, name):
    t = xc.shape[0]
    nc = t // SSD_CHUNK
    sp = _ssd_specs(t, False)

    def body(xl, xh, bm, cm, dt, p0, p1, p2, y_ref, sin_ref, st_ref):
        @pl.when(pl.program_id(1) == 0)
        def _():
            st_ref[...] = jnp.zeros_like(st_ref)

        sin_ref[...] = st_ref[...]
        y_lo, y_hi, so_lo, so_hi = _ssd_chunk(xl[...], xh[...], bm[...], cm[...], dt[...], p0[...], p1[...],
                                              p2[...], st_ref[0], st_ref[1])
        y_ref[:, :HALF] = y_lo
        y_ref[:, HALF:] = y_hi
        st_ref[0] = so_lo
        st_ref[1] = so_hi

    return _pcall(
        body, name=name, grid=(SSD_GROUPS, nc),
        in_specs=[sp['x_lo'], sp['x_hi'], sp['bm'], sp['cm'], sp['dt'], sp['par'], sp['par'], sp['par']],
        out_specs=[sp['y'], sp['st']],
        out_shape=[_sds((t, SSD_D_INNER)), _sds((nc, SSD_GROUPS, 2, HALF, SSD_STATE))],
        scratch_shapes=[pltpu.VMEM((2, HALF, SSD_STATE), F32)],
        compiler_params=_params("parallel", "arbitrary"),
    )(xc, xc, xc, xc, dt4, dtb, alog, dsk)


def _ssd_bwd(xc, dt4, dtb, alog, dsk, sin, dy, *, name):
    t = xc.shape[0]
    nc = t // SSD_CHUNK
    sp = _ssd_specs(t, True)

    def body(xl, xh, bm, cm, dt, p0, p1, p2, sin_ref, dy_ref,
             dx_ref, db_ref, dc_ref, ddt_ref, dp0, dp1, dp2, dst_ref):
        first = pl.program_id(1) == 0

        @pl.when(first)
        def _():
            dst_ref[...] = jnp.zeros_like(dst_ref)

        _, vjp = jax.vjp(_ssd_chunk, xl[...], xh[...], bm[...], cm[...], dt[...], p0[...], p1[...], p2[...],
                         sin_ref[0], sin_ref[1])
        dxl, dxh, dbm, dcm, ddt, g0, g1, g2, ds_lo, ds_hi = vjp(
            (dy_ref[:, :HALF], dy_ref[:, HALF:], dst_ref[0], dst_ref[1]))
        dx_ref[:, :HALF] = dxl
        dx_ref[:, HALF:] = dxh
        db_ref[...] = dbm
        dc_ref[...] = dcm
        ddt_ref[...] = ddt
        dst_ref[0] = ds_lo
        dst_ref[1] = ds_hi
        for ref, g in ((dp0, g0), (dp1, g1), (dp2, g2)):
            tot = jnp.broadcast_to(jnp.sum(g, axis=0, keepdims=True), ref.shape)

            @pl.when(first)
            def _(ref=ref):
                ref[...] = jnp.zeros_like(ref)

            ref[...] += tot

    return _pcall(
        body, name=name, grid=(SSD_GROUPS, nc),
        in_specs=[sp['x_lo'], sp['x_hi'], sp['bm'], sp['cm'], sp['dt'], sp['par'], sp['par'], sp['par'],
                  sp['st'], sp['y']],
        out_specs=[sp['y'], sp['grp'], sp['grp'], sp['dt'], sp['par'], sp['par'], sp['par']],
        out_shape=[_sds((t, SSD_D_INNER)), _sds((t, SSD_GROUPS * SSD_STATE)), _sds((t, SSD_GROUPS * SSD_STATE)),
                   _sds((SSD_GROUPS, t, PAD_HEADS))] + [_sds((SSD_GROUPS, 8, PAD_HEADS))] * 3,
        scratch_shapes=[pltpu.VMEM((2, HALF, SSD_STATE), F32)],
        compiler_params=_params("parallel", "arbitrary"),
    )(xc, xc, xc, xc, dt4, dtb, alog, dsk, sin, dy)


def _gatenorm(y, z, g):
    v = y * _silu(z)
    return v * lax.rsqrt(jnp.mean(v * v, axis=-1, keepdims=True) + RMS_EPS) * g


def _gmm(terms, *, trans_w, add=None, name, out_dtype=F32, tm=512):
    t = terms[0][0].shape[0]
    tm = _tile(t, tm)
    _, kw, nw = terms[0][1].shape
    wi, wo = (nw, kw) if trans_w else (kw, nw)
    signs = [s for _, _, s in terms]
    n = len(terms)

    def body(*refs):
        acc = None
        for i in range(n):
            prod = _dot(refs[2 * i][...], refs[2 * i + 1][...], _NT if trans_w else _NN)
            prod = prod if signs[i] > 0 else -prod
            acc = prod if acc is None else acc + prod
        o_ref = refs[-1]
        if add is not None:
            acc = acc + refs[2 * n][...]
        o_ref[...] = acc.astype(o_ref.dtype)

    in_specs, args = [], []
    for x, w, _ in terms:
        in_specs += [pl.BlockSpec((tm, wi), lambda i, gb: (i, gb)), pl.BlockSpec((None, kw, nw), lambda i, gb: (gb, 0, 0))]
        args += [x, w]
    if add is not None:
        in_specs.append(pl.BlockSpec((tm, wo), lambda i, gb: (i, gb)))
        args.append(add)
    return _pcall(
        body, name=name, grid=(t // tm, S5_BLOCKS), in_specs=in_specs,
        out_specs=pl.BlockSpec((tm, wo), lambda i, gb: (i, gb)), out_shape=_sds((t, S5_BLOCKS * wo), out_dtype),
        compiler_params=_params("parallel", "parallel"),
    )(*args)


def _gmm_tn(x, dy, sign, *, name, tm=512):
    t = x.shape[0]
    tm = _tile(t, tm)
    kw, nw = x.shape[1] // S5_BLOCKS, dy.shape[1] // S5_BLOCKS

    def body(x_ref, d_ref, o_ref):
        @pl.when(pl.program_id(1) == 0)
        def _():
            o_ref[...] = jnp.zeros_like(o_ref)

        prod = _dot(x_ref[...], d_ref[...], _TN)
        o_ref[...] += prod if sign > 0 else -prod

    return _pcall(
        body, name=name, grid=(S5_BLOCKS, t // tm),
        in_specs=[pl.BlockSpec((tm, kw), lambda gb, i: (i, gb)), pl.BlockSpec((tm, nw), lambda gb, i: (i, gb))],
        out_specs=pl.BlockSpec((None, kw, nw), lambda gb, i: (gb, 0, 0)), out_shape=_sds((S5_BLOCKS, kw, nw)),
        compiler_params=_params("parallel", "arbitrary"),
    )(x, dy)


SCAN_COLS = 256
SCAN_UNROLL = 8


def _cmul(ar, ai, br, bi):
    return ar * br - ai * bi, ar * bi + ai * br


def _segment_power(ar, ai, n):
    assert n & (n - 1) == 0
    for _ in range(n.bit_length() - 1):
        ar, ai = _cmul(ar, ai, ar, ai)
    return ar, ai


def _carry_in(fr, fi, pr, pi, reverse):
    rows = lax.broadcasted_iota(jnp.int32, fr.shape, 0)
    cr = jnp.zeros_like(fr[0:1])
    ci = jnp.zeros_like(cr)
    outr = jnp.zeros_like(fr)
    outi = jnp.zeros_like(fr)
    order = range(6, -1, -1) if reverse else range(1, 8)
    for j in order:
        src = j + 1 if reverse else j - 1
        nr, ni = _cmul(pr[0:1], pi[0:1], cr, ci)
        cr, ci = nr + fr[src:src + 1], ni + fi[src:src + 1]
        outr = jnp.where(rows == j, cr, outr)
        outi = jnp.where(rows == j, ci, outi)
    return outr, outi


def _scan_fwd(bur, bui, lr, li, *, name):
    t, w = bur.shape
    nrt = t // 8

    def body(br_ref, bi_ref, lr_ref, li_ref, sr_ref, si_ref):
        ar = jnp.broadcast_to(lr_ref[...], (8, SCAN_COLS))
        ai = jnp.broadcast_to(li_ref[...], (8, SCAN_COLS))

        def step(r, s, store):
            rows = pl.ds(pl.multiple_of(r * 8, 8), 8)
            nr, ni = _cmul(ar, ai, s[0], s[1])
            nr, ni = nr + br_ref[rows, :], ni + bi_ref[rows, :]
            if store:
                sr_ref[rows, :] = nr
                si_ref[rows, :] = ni
            return nr, ni

        zero = (jnp.zeros((8, SCAN_COLS), F32), jnp.zeros((8, SCAN_COLS), F32))
        fr, fi = lax.fori_loop(0, nrt, lambda r, s: step(r, s, False), zero, unroll=SCAN_UNROLL)
        pr, pi = _segment_power(ar, ai, nrt)
        init = _carry_in(fr, fi, pr, pi, False)
        lax.fori_loop(0, nrt, lambda r, s: step(r, s, True), init, unroll=SCAN_UNROLL)

    col = pl.BlockSpec((t, SCAN_COLS), lambda j: (0, j))
    row = pl.BlockSpec((1, SCAN_COLS), lambda j: (0, j))
    return _pcall(
        body, name=name, grid=(w // SCAN_COLS,), in_specs=[col, col, row, row], out_specs=[col, col],
        out_shape=[_sds((t, w)), _sds((t, w))], compiler_params=_params("parallel"),
    )(bur, bui, lr, li)


def _scan_bwd(dr, di, sr, si, lr, li, *, name):
    t, w = dr.shape
    nrt = t // 8

    def body(dr_ref, di_ref, sr_ref, si_ref, lr_ref, li_ref, gr_ref, gi_ref, dlr_ref, dli_ref):
        ar = jnp.broadcast_to(lr_ref[...], (8, SCAN_COLS))
        ai = -jnp.broadcast_to(li_ref[...], (8, SCAN_COLS))
        zero = jnp.zeros((8, SCAN_COLS), F32)

        def step1(k, g):
            rows = pl.ds(pl.multiple_of((nrt - 1 - k) * 8, 8), 8)
            nr, ni = _cmul(ar, ai, g[0], g[1])
            return nr + dr_ref[rows, :], ni + di_ref[rows, :]

        fr, fi = lax.fori_loop(0, nrt, step1, (zero, zero), unroll=SCAN_UNROLL)
        pr, pi = _segment_power(ar, ai, nrt)
        init = _carry_in(fr, fi, pr, pi, True)

        def step2(k, carry):
            gr, gi, accr, acci = carry
            r = nrt - 1 - k
            rows = pl.ds(pl.multiple_of(r * 8, 8), 8)
            prev = pl.ds(pl.multiple_of(jnp.maximum(r - 1, 0) * 8, 8), 8)
            nr, ni = _cmul(ar, ai, gr, gi)
            nr, ni = nr + dr_ref[rows, :], ni + di_ref[rows, :]
            gr_ref[rows, :] = nr
            gi_ref[rows, :] = ni
            keep = jnp.where(r > 0, 1.0, 0.0)
            pr_, pi_ = sr_ref[prev, :] * keep, si_ref[prev, :] * keep
            return nr, ni, accr + (pr_ * nr + pi_ * ni), acci + (pr_ * ni - pi_ * nr)

        _, _, accr, acci = lax.fori_loop(0, nrt, step2, (init[0], init[1], zero, zero), unroll=SCAN_UNROLL)
        last = pl.ds((nrt - 1) * 8, 8)
        pr_, pi_ = _shift_down(sr_ref[last, :], 1), _shift_down(si_ref[last, :], 1)
        g0r, g0i = gr_ref[0:8, :], gi_ref[0:8, :]
        accr = accr + (pr_ * g0r + pi_ * g0i)
        acci = acci + (pr_ * g0i - pi_ * g0r)
        dlr_ref[...] = jnp.sum(accr, axis=0, keepdims=True)
        dli_ref[...] = jnp.sum(acci, axis=0, keepdims=True)

    col = pl.BlockSpec((t, SCAN_COLS), lambda j: (0, j))
    row = pl.BlockSpec((1, SCAN_COLS), lambda j: (0, j))
    return _pcall(
        body, name=name, grid=(w // SCAN_COLS,), in_specs=[col, col, col, col, row, row],
        out_specs=[col, col, row, row], out_shape=[_sds((t, w)), _sds((t, w)), _sds((1, w)), _sds((1, w))],
        compiler_params=_params("parallel"),
    )(dr, di, sr, si, lr, li)


def _s5_expander():
    n = lax.broadcasted_iota(jnp.int32, (S5_STATE, S5_STATE * S5_GROUP), 0)
    j = lax.broadcasted_iota(jnp.int32, (S5_STATE, S5_STATE * S5_GROUP), 1)
    return jnp.where(n == j // S5_GROUP, 1.0, 0.0).astype(F32)


def _s5_discretise(lam_re, lam_im, log_step, b_re, b_im):
    lr = jnp.minimum(lam_re, S5_MAX_REAL)
    step = jnp.exp(log_step)
    mag = jnp.exp(lr * step)
    ang = lam_im * step
    lbr, lbi = mag * jnp.cos(ang), mag * jnp.sin(ang)
    p, q = lbr - 1.0, lbi
    den = lr * lr + lam_im * lam_im
    cr, ci = (p * lr + q * lam_im) / den, (q * lr - p * lam_im) / den
    e = _s5_expander()
    cre, cie = _fdot(cr, e), _fdot(ci, e)
    return lbr, lbi, cre * b_re - cie * b_im, cre * b_im + cie * b_re


def _s5_params_fwd(lam_re, lam_im, log_step, b_re, b_im, *, name):
    g, n, w = S5_GROUPS, S5_STATE, S5_STATE * S5_GROUP

    def body(a, b, c, d, e, o0, o1, o2, o3):
        for ref, val in zip((o0, o1, o2, o3), _s5_discretise(a[...], b[...], c[...], d[...], e[...])):
            ref[...] = val

    return _pcall(body, name=name, out_shape=[_sds((g, n)), _sds((g, n)), _sds((g, w)), _sds((g, w))])(
        lam_re, lam_im, log_step, b_re, b_im)


def _s5_params_bwd(lam_re, lam_im, log_step, b_re, b_im, cts, *, name):
    g, n, w = S5_GROUPS, S5_STATE, S5_STATE * S5_GROUP

    def body(a, b, c, d, e, c0, c1, c2, c3, o0, o1, o2, o3, o4):
        _, vjp = jax.vjp(_s5_discretise, a[...], b[...], c[...], d[...], e[...])
        for ref, val in zip((o0, o1, o2, o3, o4), vjp((c0[...], c1[...], c2[...], c3[...]))):
            ref[...] = val

    return _pcall(body, name=name,
                  out_shape=[_sds((g, n)), _sds((g, n)), _sds((g, 1)), _sds((g, w)), _sds((g, w))])(
        lam_re, lam_im, log_step, b_re, b_im, *cts)


def _perm(a):
    t, c = a.shape
    return a.reshape(8, t // 8, c).transpose(1, 0, 2).reshape(t, c)


def _unperm(a):
    t, c = a.shape
    return a.reshape(t // 8, 8, c).transpose(1, 0, 2).reshape(t, c)


def _blockdiag(m, rows_inner, cols_inner):
    m = m.reshape(S5_BLOCKS, 8, rows_inner, cols_inner)
    eye = jnp.eye(8, dtype=m.dtype)
    out = m[:, :, :, None, :] * eye[None, :, None, :, None]
    return out.reshape(S5_BLOCKS, 8 * rows_inner, 8 * cols_inner)


def _blockdiag_extract(m, rows_inner, cols_inner):
    m = m.reshape(S5_BLOCKS, 8, rows_inner, 8, cols_inner)
    d = jnp.diagonal(m, axis1=1, axis2=3)
    return d.transpose(0, 3, 1, 2).reshape(S5_GROUPS, rows_inner, cols_inner)


Z0, XBC0, GA0, GB0 = 0, SSD_D_INNER, SSD_D_INNER + SSD_CONV_DIM, SSD_D_INNER + SSD_CONV_DIM + D_MODEL
BIG = GB0 + D_MODEL


def _mixer_fwd(h, p, tm):
    t = h.shape[0]
    nrow = t // tm
    u = _rms_fwd(h, p['pre_g'], name="mix_rms", tm=tm)
    u_p = _perm(u)
    proj = _mm(u, p['w_big'], name="mix_in")
    dtr = _mm(u, p['w_dt'], name="mix_in_dt")
    u5 = _mm(u_p, p['w_u5'], name="mix_in_s5")
    xc = _conv_fwd(proj, XBC0, p['conv_w'], p['conv_b'], name="ssd_conv")
    dt4 = jnp.pad(dtr.reshape(t, SSD_GROUPS, 8).transpose(1, 0, 2), ((0, 0), (0, 0), (0, PAD_HEADS - 8)))
    y_ssd, s_in = _ssd_fwd(xc, dt4, p['dt_bias8'], p['a_log8'], p['d8'], name="ssd_scan")
    gw = SSD_D_INNER // SSD_GROUPS
    ya = _rows(_gatenorm, name="ssd_gate", nrow=nrow, ncol=SSD_GROUPS,
               ins=[(y_ssd, _rspec(tm, gw, 0, True)), (proj, _rspec(tm, gw, Z0 // gw, True)),
                    (p['norm_g'], _bspec(gw, 0, True))],
               outs=[(_sds((t, SSD_D_INNER), BF16), _rspec(tm, gw, 0, True), False)])[0]
    y_a = _mm(ya, p['w_a'], name="mix_a")
    lbr, lbi, bbr, bbi = _s5_params_fwd(p['lam_re'], p['lam_im'], p['log_step'], p['b_re'], p['b_im'], name="s5_par")
    bd = lambda m: _blockdiag(m.reshape(S5_GROUPS, S5_STATE, S5_GROUP).transpose(0, 2, 1), S5_GROUP, S5_STATE).astype(BF16)
    bre, bim = bd(bbr), bd(bbi)
    cre = _blockdiag(p['c_re'].transpose(0, 2, 1), S5_STATE, S5_GROUP).astype(BF16)
    cim = _blockdiag(p['c_im'].transpose(0, 2, 1), S5_STATE, S5_GROUP).astype(BF16)
    lr, li = lbr.reshape(1, -1), lbi.reshape(1, -1)
    bur = _gmm([(u5, bre, 1)], trans_w=False, name="s5_bu_re")
    bui = _gmm([(u5, bim, 1)], trans_w=False, name="s5_bu_im")
    sr, si = _scan_fwd(bur, bui, lr, li, name="s5_scan")
    y5 = _gmm([(sr, cre, 1), (si, cim, -1)], trans_w=False, name="s5_out")
    y5g = _rows(lambda a, b, d: _gelu(a + d * b), name="s5_act", nrow=nrow,
                ins=[(y5, _rspec(tm, S5_WIDTH)), (u5, _rspec(tm, S5_WIDTH)), (p['s5_d'], _bspec(S5_WIDTH))],
                outs=[(_sds((t, S5_WIDTH), BF16), _rspec(tm, S5_WIDTH), False)])[0]
    vg = _mm(y5g, p['w_glu'], name="s5_glu")
    ybin = _rows(lambda a, b: a * _sigmoid(b), name="s5_glu_act", nrow=nrow,
                 ins=[(vg, _rspec(tm, S5_WIDTH, 0)), (vg, _rspec(tm, S5_WIDTH, 1))],
                 outs=[(_sds((t, S5_WIDTH), BF16), _rspec(tm, S5_WIDTH), False)])[0]
    y_b = _unperm(_mm(ybin, p['w_b'], name="mix_b"))
    merged = _rows(lambda ga, gb, a, b: _sigmoid(ga) * a + _sigmoid(gb) * b, name="mix_merge", nrow=nrow,
                   ins=[(proj, _rspec(tm, D_MODEL, GA0 // D_MODEL)), (proj, _rspec(tm, D_MODEL, GB0 // D_MODEL)),
                        (y_a, _rspec(tm, D_MODEL)), (y_b, _rspec(tm, D_MODEL))],
                   outs=[(_sds((t, D_MODEL), BF16), _rspec(tm, D_MODEL), False)])[0]
    m = _mm(merged, p['w_out'], name="mix_out")
    out = _resid_fwd(h, m, p['post_g'], 1.0, name="mix_res", tm=tm)
    saved = dict(h=h, u=u, u_p=u_p, proj=proj, u5=u5, xc=xc, dt4=dt4, s_in=s_in, y_ssd=y_ssd, ya=ya, y_a=y_a,
                 bre=bre, bim=bim, cre=cre, cim=cim, lr=lr, li=li, sr=sr, si=si, y5=y5, y5g=y5g, vg=vg, ybin=ybin,
                 y_b=y_b, merged=merged, m=m)
    return out, saved


def _mixer_bwd(dh, p, s, tm):
    t = dh.shape[0]
    nrow = t // tm
    proj = s['proj']
    bufs = {}

    def grad_mm(a, b, wname, axis, name):
        bufs[wname] = _mm(a, b, ta=True, name=name, out_dtype=BF16, shards=axis)

    dm, dpost = _resid_bwd(s['m'], p['post_g'], dh, 1.0, name="mix_res_bwd", tm=tm)
    dmerged = _mm(dm, p['w_out'], tb=True, name="mix_out_dx")
    grad_mm(s['merged'], dm, 'w_out', 'rows', "mix_out_dw")

    def merge_bwd(ga, gb, a, b, d):
        _, vjp = jax.vjp(lambda ga_, gb_, a_, b_: _sigmoid(ga_) * a_ + _sigmoid(gb_) * b_, ga, gb, a, b)
        dga, dgb, da, db = vjp(d)
        return jnp.concatenate([dga, dgb], axis=1), da, db

    dgab, dy_a, dy_b = _rows(
        merge_bwd, name="mix_merge_bwd", nrow=nrow,
        ins=[(proj, _rspec(tm, D_MODEL, GA0 // D_MODEL)), (proj, _rspec(tm, D_MODEL, GB0 // D_MODEL)),
             (s['y_a'], _rspec(tm, D_MODEL)), (s['y_b'], _rspec(tm, D_MODEL)), (dmerged, _rspec(tm, D_MODEL))],
        outs=[(_sds((t, 2 * D_MODEL), BF16), _rspec(tm, 2 * D_MODEL), False),
              (_sds((t, D_MODEL), BF16), _rspec(tm, D_MODEL), False),
              (_sds((t, D_MODEL), BF16), _rspec(tm, D_MODEL), False)])
    dya = _mm(dy_a, p['w_a'], tb=True, name="mix_a_dx")
    grad_mm(s['ya'], dy_a, 'w_branch_a', 'rows', "mix_a_dw")
    gw = SSD_D_INNER // SSD_GROUPS

    def gate_bwd(y, z, g, d):
        _, vjp = jax.vjp(_gatenorm, y, z, g)
        return vjp(d)

    dy_ssd, dz, dnorm = _rows(
        gate_bwd, name="ssd_gate_bwd", nrow=nrow, ncol=SSD_GROUPS,
        ins=[(s['y_ssd'], _rspec(tm, gw, 0, True)), (proj, _rspec(tm, gw, Z0 // gw, True)),
             (p['norm_g'], _bspec(gw, 0, True)), (dya, _rspec(tm, gw, 0, True))],
        outs=[(_sds((t, SSD_D_INNER)), _rspec(tm, gw, 0, True), False),
              (_sds((t, SSD_D_INNER), BF16), _rspec(tm, gw, 0, True), False),
              (_sds((1, SSD_D_INNER)), _bspec(gw, 0, True), True)])
    dxs, dbm, dcm, ddt4, ddtb, dalog, ddsk = _ssd_bwd(s['xc'], s['dt4'], p['dt_bias8'], p['a_log8'], p['d8'],
                                                      s['s_in'], dy_ssd, name="ssd_scan_bwd")
    dxc = jnp.concatenate([dxs, dbm, dcm], axis=1)
    dxbc, dconv_w, dconv_b = _conv_bwd(proj, XBC0, p['conv_w'], p['conv_b'], dxc, name="ssd_conv_bwd")
    ddtr = ddt4[:, :, :8].transpose(1, 0, 2).reshape(t, SSD_HEADS)
    dy_bp = _perm(dy_b)
    dybin = _mm(dy_bp, p['w_b'], tb=True, name="mix_b_dx")
    grad_mm(s['ybin'], dy_bp, 'w_branch_b', 'rows', "mix_b_dw")

    def glu_bwd(a, b, d):
        _, vjp = jax.vjp(lambda a_, b_: a_ * _sigmoid(b_), a, b)
        da, db = vjp(d)
        return jnp.concatenate([da, db], axis=1)

    dvg = _rows(glu_bwd, name="s5_glu_act_bwd", nrow=nrow,
                ins=[(s['vg'], _rspec(tm, S5_WIDTH, 0)), (s['vg'], _rspec(tm, S5_WIDTH, 1)), (dybin, _rspec(tm, S5_WIDTH))],
                outs=[(_sds((t, 2 * S5_WIDTH), BF16), _rspec(tm, 2 * S5_WIDTH), False)])[0]
    dy5g = _mm(dvg, p['w_glu'], tb=True, name="s5_glu_dx")
    grad_mm(s['y5g'], dvg, 's5_w_glu', 'cols', "s5_glu_dw")

    def act_bwd(a, b, d, g):
        _, vjp = jax.vjp(lambda a_, b_, d_: _gelu(a_ + d_ * b_), a, b, d)
        return vjp(g)

    dy5, du5_direct, ds5d = _rows(
        act_bwd, name="s5_act_bwd", nrow=nrow,
        ins=[(s['y5'], _rspec(tm, S5_WIDTH)), (s['u5'], _rspec(tm, S5_WIDTH)), (p['s5_d'], _bspec(S5_WIDTH)),
             (dy5g, _rspec(tm, S5_WIDTH))],
        outs=[(_sds((t, S5_WIDTH), BF16), _rspec(tm, S5_WIDTH), False), (_sds((t, S5_WIDTH)), _rspec(tm, S5_WIDTH), False),
              (_sds((1, S5_WIDTH)), _bspec(S5_WIDTH), True)])
    dsdr = _gmm([(dy5, s['cre'], 1)], trans_w=True, name="s5_out_dx_re")
    dsdi = _gmm([(dy5, s['cim'], -1)], trans_w=True, name="s5_out_dx_im")
    dcre = _gmm_tn(s['sr'], dy5, 1, name="s5_out_dw_re")
    dcim = _gmm_tn(s['si'], dy5, -1, name="s5_out_dw_im")
    gr, gi, dlr, dli = _scan_bwd(dsdr, dsdi, s['sr'], s['si'], s['lr'], s['li'], name="s5_scan_bwd")
    dbre = _gmm_tn(s['u5'], gr, 1, name="s5_bu_dw_re")
    dbim = _gmm_tn(s['u5'], gi, 1, name="s5_bu_dw_im")
    du5 = _gmm([(gr, s['bre'], 1), (gi, s['bim'], 1)], trans_w=True, add=du5_direct, name="s5_bu_dx", out_dtype=BF16)
    du_p = _mm(du5, p['w_u5'], tb=True, name="mix_in_s5_dx")
    dw_u5 = _mm(s['u_p'], du5, ta=True, name="mix_in_s5_dw", out_dtype=BF16)
    ext_b = lambda m: _blockdiag_extract(m, S5_GROUP, S5_STATE).transpose(0, 2, 1).reshape(S5_GROUPS, S5_STATE * S5_GROUP)
    dlam_re, dlam_im, dlog_step, db_re, db_im = _s5_params_bwd(
        p['lam_re'], p['lam_im'], p['log_step'], p['b_re'], p['b_im'],
        (dlr.reshape(S5_GROUPS, S5_STATE), dli.reshape(S5_GROUPS, S5_STATE), ext_b(dbre), ext_b(dbim)), name="s5_par_bwd")
    dc_re = _blockdiag_extract(dcre, S5_STATE, S5_GROUP).transpose(0, 2, 1)
    dc_im = _blockdiag_extract(dcim, S5_STATE, S5_GROUP).transpose(0, 2, 1)
    dproj = jnp.concatenate([dz, dxbc, dgab], axis=1)
    du_big = _mm(dproj, p['w_big'], tb=True, name="mix_in_dx")
    du_dt = _mm(ddtr, p['w_dt'], tb=True, name="mix_in_dt_dx")
    dw_big = _mm(s['u'], dproj, ta=True, name="mix_in_dw", out_dtype=BF16)
    dw_dt = _mm(s['u'], ddtr, ta=True, name="mix_in_dt_dw", out_dtype=BF16)
    dh_in, dpre = _rms_bwd(s['h'], p['pre_g'], dh, [du_big, du_dt, _unperm(du_p)], name="mix_rms_bwd", tm=tm)
    dw_in = jnp.concatenate([dw_big[:, :GA0], dw_dt, dw_u5, dw_big[:, GA0:]], axis=1)
    bufs['w_in'] = dw_in.reshape(D_MODEL, N_DEV, -1).transpose(1, 0, 2)[:, None]
    grads = {
        'mix_pre_g': dpre, 'mix_post_g': dpost, 'ssd_conv_w': dconv_w, 'ssd_conv_b': dconv_b,
        'ssd_dt_bias': ddtb[:, 0, :8].reshape(-1), 'ssd_a_log': dalog[:, 0, :8].reshape(-1),
        'ssd_d': ddsk[:, 0, :8].reshape(-1), 'ssd_norm_g': dnorm,
        's5_lambda_re': dlam_re, 's5_lambda_im': dlam_im,
        's5_b_re': db_re.reshape(S5_GROUPS, S5_STATE, S5_GROUP), 's5_b_im': db_im.reshape(S5_GROUPS, S5_STATE, S5_GROUP),
        's5_c_re': dc_re, 's5_c_im': dc_im, 's5_log_step': dlog_step.reshape(-1), 's5_d': ds5d,
    }
    return dh_in, bufs, grads


HBM_SPEC = pl.BlockSpec(memory_space=pltpu.HBM)


def _place():
    return lax.axis_index("x"), lax.axis_index("y"), lax.axis_index("c")


GATHER_COLLECTIVE_ID = 1


def _all_gather(shards, *, name, on_sequencer=False):
    n = len(shards)

    def body(*refs):
        x_refs, out_refs = refs[:n], refs[n:2 * n]
        send_sems, recv_sems, local_sems = refs[2 * n:]
        x, y, c = _place()
        me, sibling = (x, y, c), (x, y, 1 - c)
        chips = [(1 - x, y), (x, 1 - y), (1 - x, 1 - y)]
        if on_sequencer:
            _handshake([sibling] + [(*chip, c) for chip in chips])

        def slot(o, px, py, pc):
            return out_refs[o].at[4 * px + 2 * py + pc]

        def copy(o, k, block, to, src=None):
            return pltpu.make_async_remote_copy(
                src_ref=slot(o, *block) if src is None else src, dst_ref=slot(o, *block),
                send_sem=send_sems.at[7 * o + k], recv_sem=recv_sems.at[7 * o + k], device_id=to, device_id_type=MESH)

        mine = [pltpu.make_async_copy(x_refs[o], slot(o, *me), local_sems.at[o]) for o in range(n)]
        for cp in mine:
            cp.start()
        first = []
        for j, chip in enumerate(chips):
            first += [copy(o, 1 + j, me, (*chip, c), src=x_refs[o]) for o in range(n)]
        first += [copy(o, 0, me, sibling, src=x_refs[o]) for o in range(n)]
        for cp in first:
            cp.start()
        passed = []
        for j, chip in enumerate(chips):
            for o in range(n):
                copy(o, 1 + j, (*chip, c), me).wait_recv()
                passed.append(copy(o, 4 + j, (*chip, c), sibling))
                passed[-1].start()
        for o in range(n):
            copy(o, 0, sibling, me).wait_recv()
        for j, chip in enumerate(chips):
            for o in range(n):
                copy(o, 4 + j, (*chip, 1 - c), me).wait_recv()
        for cp in first + passed:
            cp.wait_send()
        for cp in mine:
            cp.wait()

    out_shape = [jax.ShapeDtypeStruct((N_DEV,) + s.shape, s.dtype) for s in shards]
    sems = [pltpu.SemaphoreType.DMA((7 * n,)), pltpu.SemaphoreType.DMA((7 * n,)), pltpu.SemaphoreType.DMA((n,))]
    if on_sequencer:
        return _scall(body, name=name, out_type=out_shape, scratch_types=sems, collective_id=GATHER_COLLECTIVE_ID)(*shards)
    return _pcall(body, name=name, in_specs=[HBM_SPEC] * n, out_specs=[HBM_SPEC] * n, out_shape=out_shape,
                  scratch_shapes=sems)(*shards)


N_CHIPS = 4


SIBLING_COLLECTIVE_ID = 2
CHIPS_COLLECTIVE_ID = 3


def _handshake(peers):
    barrier = pltpu.get_barrier_semaphore()
    for peer in peers:
        pl.semaphore_signal(barrier, inc=1, device_id=peer, device_id_type=MESH)
    pl.semaphore_wait(barrier, len(peers))


def _exchange_sibling(grads, *, name):
    n = len(grads)

    def body(*refs):
        p_refs, q_refs = refs[:n], refs[n:2 * n]
        send_sems, recv_sems = refs[2 * n:]
        x, y, c = _place()
        _handshake([(x, y, 1 - c)])
        copies = [pltpu.make_async_remote_copy(
            src_ref=p_refs[o].at[k, 1 - c], dst_ref=q_refs[o].at[k], send_sem=send_sems.at[N_CHIPS * o + k],
            recv_sem=recv_sems.at[N_CHIPS * o + k], device_id=(x, y, 1 - c), device_id_type=MESH)
            for o in range(n) for k in range(N_CHIPS)]
        for cp in copies:
            cp.start()
        for cp in copies:
            cp.wait()

    return _scall(
        body, name=name, out_type=[jax.ShapeDtypeStruct((N_CHIPS,) + g.shape[2:], g.dtype) for g in grads],
        scratch_types=[pltpu.SemaphoreType.DMA((N_CHIPS * n,)), pltpu.SemaphoreType.DMA((N_CHIPS * n,))],
        collective_id=SIBLING_COLLECTIVE_ID,
    )(*grads)


def _pair_sum(own, got, *, name):
    _, _, r, l = own.shape
    tr = _tile(r, 512, 16)
    c = lax.axis_index("c").astype(jnp.int32).reshape(1)

    def body(c_ref, p_ref, q_ref, o_ref):
        o_ref[...] = (p_ref[...].astype(F32) + q_ref[...].astype(F32)).astype(o_ref.dtype)

    return _pcall(
        body, name=name,
        grid_spec=pltpu.PrefetchScalarGridSpec(
            num_scalar_prefetch=1, grid=(N_CHIPS, r // tr),
            in_specs=[pl.BlockSpec((None, None, tr, l), lambda k, i, cr: (k, cr[0], i, 0)),
                      pl.BlockSpec((None, tr, l), lambda k, i, cr: (k, i, 0))],
            out_specs=pl.BlockSpec((None, tr, l), lambda k, i, cr: (k, i, 0))),
        out_shape=jax.ShapeDtypeStruct((N_CHIPS, r, l), own.dtype),
        compiler_params=_params("parallel", "parallel"),
    )(c, own, got)


def _exchange_chips(parts, *, name):
    n = len(parts)

    def body(*refs):
        p_refs, g_refs = refs[:n], refs[n:2 * n]
        send_sems, recv_sems, local_sems = refs[2 * n:]
        x, y, c = _place()
        mine = 2 * x + y
        chips = [(1 - x, y), (x, 1 - y), (1 - x, 1 - y)]
        _handshake([(*chip, c) for chip in chips])
        own = [pltpu.make_async_copy(p_refs[o].at[mine], g_refs[o].at[mine], local_sems.at[o]) for o in range(n)]
        for cp in own:
            cp.start()
        copies = []
        for j, (px, py) in enumerate(chips):
            copies += [pltpu.make_async_remote_copy(
                src_ref=p_refs[o].at[2 * px + py], dst_ref=g_refs[o].at[mine], send_sem=send_sems.at[3 * o + j],
                recv_sem=recv_sems.at[3 * o + j], device_id=(px, py, c), device_id_type=MESH) for o in range(n)]
        for cp in copies:
            cp.start()
        for cp in copies:
            cp.wait()
        for cp in own:
            cp.wait()

    return _scall(
        body, name=name, out_type=[jax.ShapeDtypeStruct(p.shape, p.dtype) for p in parts],
        scratch_types=[pltpu.SemaphoreType.DMA((3 * n,)), pltpu.SemaphoreType.DMA((3 * n,)), pltpu.SemaphoreType.DMA((n,))],
        collective_id=CHIPS_COLLECTIVE_ID,
    )(*parts)


def _sum_slots(g, *, name):
    n, r, l = g.shape
    tr = _tile(r, 512, 16)

    def body(g_ref, o_ref):
        acc = g_ref[0].astype(F32)
        for k in range(1, n):
            acc = acc + g_ref[k].astype(F32)
        o_ref[...] = acc

    return _pcall(
        body, name=name, grid=(r // tr,), in_specs=[pl.BlockSpec((n, tr, l), lambda i: (0, i, 0))],
        out_specs=pl.BlockSpec((tr, l), lambda i: (i, 0)), out_shape=_sds((r, l)),
        compiler_params=_params("parallel"),
    )(g)


SUBLAYERS = (('ffn1', ['ffn1_w_gate', 'ffn1_w_up', 'ffn1_w_down']),
             ('mix', ['w_in', 'ssd_conv_w', 'w_branch_a', 's5_w_glu', 'w_branch_b', 'w_out']),
             ('ffn2', ['ffn2_w_gate', 'ffn2_w_up', 'ffn2_w_down']))


def _gather_weights(w):
    layers = []
    for i in range(DEPTH):
        g = {}
        for tag, names in SUBLAYERS:
            shards = [w[n][i:i + 1] if n == 'ssd_conv_w' else w[n][i:i + 1].astype(BF16) for n in names]
            g.update(zip(names, _all_gather(shards, name=f"gather_{tag}", on_sequencer=bool(layers or g))))
        layers.append(g)
    return layers


class _ReduceScatter:
    @staticmethod
    def sibling(tag, bufs):
        names = list(bufs)
        own = [bufs[n].reshape((N_CHIPS, 2) + bufs[n].shape[1:]) for n in names]
        return (tag, names), (own, _exchange_sibling(own, name=f"reduce_sibling_{tag}"))

    @staticmethod
    def chips(meta, arrays):
        (tag, names), (own, got) = meta, arrays
        flat = lambda a, lead: a.reshape(lead + (-1, a.shape[-1]))
        parts = [_pair_sum(flat(o, (N_CHIPS, 2)), flat(g, (N_CHIPS,)), name=f"reduce_pair_sum_{n}").reshape(g.shape)
                 for n, o, g in zip(names, own, got)]
        return names, _exchange_chips(parts, name=f"reduce_chips_{tag}")

    @staticmethod
    def done(names, slots):
        return dict(zip(names, slots))


def _reduce_small(grads, shapes):
    flat = jnp.concatenate([grads[n].astype(F32).reshape(-1) for n in SMALL_ORDER])
    pad = (-flat.shape[0]) % (8 * LANES)
    flat = jnp.concatenate([flat, jnp.zeros((pad,), F32)]).reshape(-1, LANES)
    gathered = _all_gather([flat], name="gather_small_grads")[0]
    total = _sum_slots(gathered, name="sum_small_grads").reshape(-1)
    out, o = {}, 0
    for n in SMALL_ORDER:
        size = int(np.prod(shapes[n]))
        out[n] = total[o:o + size].reshape(shapes[n])
        o += size
    return out


def _adamw(w, g, m, v, *, name, slots=False):
    shape = w.shape
    if slots:
        lyr, rows, lanes = shape
        w2, m2, v2 = w, m, v
        tr = _tile(rows, 256, 16)
        nrt = rows // tr
        grid = (lyr, nrt)
        spec = pl.BlockSpec((None, tr, lanes), lambda l, i: (l, i, 0))
        g_specs = [pl.BlockSpec((N_CHIPS, None, tr, lanes),
                                lambda l, i, k=k: (0, 0, jnp.where(l == k, i, jnp.where(l > k, nrt - 1, 0)), 0))
                   for k in range(lyr)]
        g_args = list(g)
        out_shape = [_sds(shape)] * 4
    else:
        lanes = shape[-1] if (shape[-1] >= 128 or w.size % LANES) else LANES
        as2d = lambda a: a.reshape(-1, lanes)
        w2, m2, v2 = as2d(w), as2d(m), as2d(v)
        r = w2.shape[0]
        tr = _tile(r, 256, 8)
        grid = (1, r // tr)
        spec = pl.BlockSpec((tr, lanes), lambda l, i: (i, 0))
        g_specs, g_args = [spec], [as2d(g)]
        out_shape = [_sds((r, lanes))] * 4
    n_g = len(g_args)

    def body(w_ref, *rest):
        g_refs = rest[:n_g]
        m_ref, v_ref, go_ref, d_ref, mo_ref, vo_ref = rest[n_g:]
        if slots:
            gg = None
            for k, g_ref in enumerate(g_refs):
                tot = g_ref[0].astype(F32)
                for c in range(1, N_CHIPS):
                    tot = tot + g_ref[c].astype(F32)
                gg = tot if gg is None else jnp.where(pl.program_id(0) == k, tot, gg)
        else:
            gg = g_refs[0][...]
        go_ref[...] = gg
        mn = ADAM_B1 * m_ref[...] + (1.0 - ADAM_B1) * gg
        vn = ADAM_B2 * v_ref[...] + (1.0 - ADAM_B2) * (gg * gg)
        m_hat = mn / (1.0 - ADAM_B1 ** ADAM_STEP)
        v_hat = vn / (1.0 - ADAM_B2 ** ADAM_STEP)
        d_ref[...] = -ADAM_LR * (m_hat / (jnp.sqrt(v_hat) + ADAM_EPS) + ADAM_WD * w_ref[...])
        mo_ref[...] = mn
        vo_ref[...] = vn

    res = _pcall(
        body, name=name, grid=grid, in_specs=[spec] + g_specs + [spec, spec], out_specs=[spec] * 4,
        out_shape=out_shape, compiler_params=_params("arbitrary", "arbitrary"),
    )(w2, *g_args, m2, v2)
    return tuple(a.reshape(shape) for a in res)


def _sublayer_params(w, g, i, k):
    row = lambda a: a.astype(F32).reshape(1, -1)
    if k != 'mix':
        return dict(layer=i, pre_g=row(w[f'{k}_pre_g'][i]), post_g=row(w[f'{k}_post_g'][i]),
                    w_gate=g[f'{k}_w_gate'], w_up=g[f'{k}_w_up'], w_down=g[f'{k}_w_down'])
    head8 = lambda a: jnp.broadcast_to(
        jnp.pad(a.astype(F32).reshape(SSD_GROUPS, 1, 8), ((0, 0), (0, 0), (0, PAD_HEADS - 8))), (SSD_GROUPS, 8, PAD_HEADS))
    by_rows = lambda n: g[n].reshape(-1, g[n].shape[-1])
    by_cols = lambda n: g[n][:, 0].transpose(1, 0, 2).reshape(g[n].shape[2], -1)
    w_in = by_cols('w_in')
    s = np.cumsum([SSD_D_INNER, SSD_CONV_DIM, SSD_HEADS, S5_WIDTH, D_MODEL])
    return dict(
        layer=i, pre_g=row(w['mix_pre_g'][i]), post_g=row(w['mix_post_g'][i]),
        w_big=jnp.concatenate([w_in[:, :s[1]], w_in[:, s[3]:]], axis=1), w_dt=w_in[:, s[1]:s[2]], w_u5=w_in[:, s[2]:s[3]],
        conv_w=by_cols('ssd_conv_w'), conv_b=row(w['ssd_conv_b'][i]),
        dt_bias8=head8(w['ssd_dt_bias'][i]), a_log8=head8(w['ssd_a_log'][i]), d8=head8(w['ssd_d'][i]),
        norm_g=row(w['ssd_norm_g'][i]), w_a=by_rows('w_branch_a'),
        lam_re=w['s5_lambda_re'][i], lam_im=w['s5_lambda_im'][i], log_step=w['s5_log_step'][i].reshape(S5_GROUPS, 1),
        b_re=w['s5_b_re'][i].reshape(S5_GROUPS, -1), b_im=w['s5_b_im'][i].reshape(S5_GROUPS, -1),
        c_re=w['s5_c_re'][i], c_im=w['s5_c_im'][i], s5_d=row(w['s5_d'][i]),
        w_glu=by_cols('s5_w_glu'), w_b=by_rows('w_branch_b'), w_out=by_rows('w_out'))


def _loss_head(h, target, *, tm):
    t, d = h.shape

    def fn(y, tgt):
        err = y - tgt
        return err * (1.0 / d), jnp.sum(0.5 * jnp.sum(err * err, axis=-1, keepdims=True) * (1.0 / d), axis=0, keepdims=True)

    dy, loss = _rows(fn, name="loss_head", nrow=t // tm,
                     ins=[(h, _rspec(tm, d)), (target, _rspec(tm, d))],
                     outs=[(_sds((t, d)), _rspec(tm, d), False), (_sds((1, 128)), _bspec(128), True)])
    return dy, loss[0, 0]


def _forward_backward(h, target, w, g, rs):
    t = h.shape[0]
    tm = _tile(t, 256, 8)
    layers, saved = [], []
    for i in range(DEPTH):
        gi, ps, ss = dict(g[i]), [], []
        for tag, names in SUBLAYERS:
            tied, h = lax.optimization_barrier(([gi[n] for n in names], h))
            gi.update(zip(names, tied))
            p = _sublayer_params(w, gi, i, tag)
            h, s = _mixer_fwd(h, p, tm) if tag == 'mix' else _ffn_fwd(h, p, tag, tm)
            ps.append(p)
            ss.append(s)
        layers.append(ps)
        saved.append(ss)
    dh, loss = _loss_head(h, target, tm=tm)
    reduced, small = [{} for _ in range(DEPTH)], [{} for _ in range(DEPTH)]
    in_sibling, in_chips = None, None

    def advance(dh, new):
        nonlocal in_sibling, in_chips
        if in_chips is not None:
            layer, names, slots = in_chips
            slots, dh = lax.optimization_barrier((slots, dh))
            reduced[layer].update(rs.done(names, slots))
        in_chips = None
        if in_sibling is not None:
            layer, meta, arrays = in_sibling
            arrays, dh = lax.optimization_barrier((arrays, dh))
            in_chips = (layer,) + tuple(rs.chips(meta, arrays))
        in_sibling = new
        return dh

    for i in reversed(range(DEPTH)):
        for k in reversed(range(len(SUBLAYERS))):
            tag = SUBLAYERS[k][0]
            if tag == 'mix':
                dh, bufs, grads = _mixer_bwd(dh, layers[i][k], saved[i][k], tm)
            else:
                dh, bufs, grads = _ffn_bwd(dh, layers[i][k], saved[i][k], tag, tm)
            small[i].update(grads)
            dh = advance(dh, (i,) + tuple(rs.sibling(tag, bufs)))
    dh = advance(advance(dh, None), None)
    shapes = _small_shapes(w)
    stacked = {n: jnp.stack([small[i][n].reshape(shapes[n][1:]) for i in range(DEPTH)]) for n in SMALL_ORDER}
    return loss, dh, reduced, stacked


def _small_shapes(w):
    return {n: (w[n].shape[:-1] + (SSD_CONV_DIM,) if n == 'ssd_conv_w' else w[n].shape) for n in SMALL_ORDER}


def kernel(*args):
    n_w = len(WEIGHTS)
    x, target = args[0], args[1 + n_w]
    w = dict(zip(WEIGHTS, args[1:1 + n_w]))
    m = dict(zip(WEIGHTS, args[2 + n_w:2 + 2 * n_w]))
    v = dict(zip(WEIGHTS, args[2 + 2 * n_w:2 + 3 * n_w]))
    t = x.shape[1]

    g = _gather_weights(w)
    loss_local, dx, slots, small = _forward_backward(x.reshape(t, D_MODEL), target.reshape(t, D_MODEL), w, g,
                                                     _ReduceScatter)
    loss = lax.psum(loss_local, ("x", "y", "c"))
    small = _reduce_small(small, _small_shapes(w))
    me = 4 * lax.axis_index("x") + 2 * lax.axis_index("y") + lax.axis_index("c")
    cols = w['ssd_conv_w'].shape[-1]
    small['ssd_conv_w'] = lax.dynamic_slice_in_dim(small['ssd_conv_w'], me * cols, cols, axis=2)

    grad, delta, new_m, new_v = {}, {}, {}, {}
    for n in WEIGHTS:
        sharded = n in slots[0]
        grad[n], delta[n], new_m[n], new_v[n] = _adamw(
            w[n], [slots[i][n] for i in range(DEPTH)] if sharded else small[n], m[n], v[n], name=f"adamw_{n}",
            slots=sharded)
    return (loss, dx.reshape(x.shape), *[grad[n] for n in WEIGHTS], *[delta[n] for n in WEIGHTS],
            *[new_m[n] for n in WEIGHTS], *[new_v[n] for n in WEIGHTS])
```

```python
import functools
import math

import numpy as np
import jax
import jax.numpy as jnp
from jax import lax
from jax.experimental import pallas as pl
from jax.experimental.pallas import tpu as pltpu
from jax.experimental.pallas import tpu_sc as plsc

F32 = jnp.float32
BF16 = jnp.bfloat16
MESH = pl.DeviceIdType.MESH
HIGHEST = lax.Precision.HIGHEST

D_MODEL = 1024
DEPTH = 2
FFN_HIDDEN = 2816
SSD_D_INNER = 2048
SSD_HEADS = 32
SSD_HEAD_DIM = 64
SSD_GROUPS = 4
SSD_STATE = 128
SSD_CHUNK = 128
SSD_CONV_DIM = 3072
SSD_CONV_WIDTH = 4
S5_WIDTH = 1024
S5_GROUP = 16
S5_GROUPS = 64
S5_STATE = 64
S5_MAX_REAL = -1e-4
S5_BLOCKS = 8
RMS_EPS = 1e-6
N_DEV = 8
LANES = 1024

ADAM_LR = 0.001
ADAM_B1 = 0.9
ADAM_B2 = 0.999
ADAM_EPS = 1e-08
ADAM_WD = 0.01
ADAM_STEP = 10

VMEM_LIMIT_BYTES = 48 * 1024 * 1024

WEIGHTS = ['ffn1_pre_g', 'ffn1_post_g', 'ffn1_w_gate', 'ffn1_w_up', 'ffn1_w_down', 'mix_pre_g', 'mix_post_g',
           'w_in', 'ssd_conv_w', 'ssd_conv_b', 'ssd_dt_bias', 'ssd_a_log', 'ssd_d', 'ssd_norm_g', 'w_branch_a',
           's5_lambda_re', 's5_lambda_im', 's5_b_re', 's5_b_im', 's5_c_re', 's5_c_im', 's5_log_step', 's5_d',
           's5_w_glu', 'w_branch_b', 'w_out', 'ffn2_pre_g', 'ffn2_post_g', 'ffn2_w_gate', 'ffn2_w_up',
           'ffn2_w_down']
SHARDED = {'ffn1_w_gate': 2, 'ffn1_w_up': 2, 'ffn1_w_down': 1, 'w_in': 2, 'ssd_conv_w': 2, 'w_branch_a': 1,
           's5_w_glu': 2, 'w_branch_b': 1, 'w_out': 1, 'ffn2_w_gate': 2, 'ffn2_w_up': 2, 'ffn2_w_down': 1}
SHARDED_ORDER = [n for n in WEIGHTS if n in SHARDED]
SMALL_ORDER = [n for n in WEIGHTS if n not in SHARDED or n == 'ssd_conv_w']


def _pcall(body, **kw):
    return pl.pallas_call(body, **kw)


def _scall(body, *, name, out_type, scratch_types, collective_id):
    return pl.kernel(body, out_type=out_type, mesh=plsc.ScalarSubcoreMesh(axis_name="sequencer", num_cores=1),
                     scratch_types=scratch_types, name=name,
                     compiler_params=pltpu.CompilerParams(collective_id=collective_id))


def _params(*sem):
    return pltpu.CompilerParams(dimension_semantics=sem, vmem_limit_bytes=VMEM_LIMIT_BYTES)


def _tile(n, pref, align=128):
    if n <= pref:
        return n
    t = (pref // align) * align
    while t >= align:
        if n % t == 0:
            return t
        t -= align
    return n


def _rms(x, g):
    return x * lax.rsqrt(jnp.mean(x * x, axis=-1, keepdims=True) + RMS_EPS) * g


def _sigmoid(x):
    return 1.0 / (1.0 + jnp.exp(-x))


def _silu(x):
    return x * _sigmoid(x)


def _gelu(x):
    return 0.5 * x * (1.0 + jnp.tanh(math.sqrt(2.0 / math.pi) * (x + 0.044715 * (x * x * x))))


def _softplus(x):
    return jnp.maximum(x, 0.0) + jnp.log(1.0 + jnp.exp(-jnp.abs(x)))


def _dot(a, b, dims):
    return lax.dot_general(a.astype(BF16), b.astype(BF16), (dims, ((), ())), preferred_element_type=F32)


_NN = ((1,), (0,))
_NT = ((1,), (1,))
_TN = ((0,), (0,))


@jax.custom_vjp
def _bdot_nn(a, b):
    return _dot(a, b, _NN)


_bdot_nn.defvjp(lambda a, b: (_dot(a, b, _NN), (a, b)),
                lambda r, g: (_dot(g, r[1], _NT), _dot(r[0], g, _TN)))


@jax.custom_vjp
def _bdot_nt(a, b):
    return _dot(a, b, _NT)


_bdot_nt.defvjp(lambda a, b: (_dot(a, b, _NT), (a, b)),
                lambda r, g: (_dot(g, r[1], _NN), _dot(g, r[0], _TN)))


@jax.custom_vjp
def _bdot_tn(a, b):
    return _dot(a, b, _TN)


_bdot_tn.defvjp(lambda a, b: (_dot(a, b, _TN), (a, b)),
                lambda r, g: (_dot(r[1], g, _NT), _dot(r[0], g, _NN)))


def _fdot(a, b, dims=_NN):
    return lax.dot_general(a, b, (dims, ((), ())), precision=HIGHEST, preferred_element_type=F32)


def _sel3(x, sel, dims, x_first):
    p1 = x.astype(BF16)
    r1 = x - p1.astype(F32)
    p2 = r1.astype(BF16)
    p3 = (r1 - p2.astype(F32)).astype(BF16)
    sel = sel.astype(BF16)
    out = None
    for piece in (p1, p2, p3):
        d = lax.dot_general(*((piece, sel) if x_first else (sel, piece)), (dims, ((), ())), preferred_element_type=F32)
        out = d if out is None else out + d
    return out


@jax.custom_vjp
def _sel_right(x, sel):
    return _sel3(x, sel, _NN, True)


_sel_right.defvjp(lambda x, sel: (_sel3(x, sel, _NN, True), sel),
                  lambda sel, g: (_sel3(g, sel, _NT, True), jnp.zeros_like(sel)))


@jax.custom_vjp
def _sel_left(sel, x):
    return _sel3(x, sel, _NN, False)


_sel_left.defvjp(lambda sel, x: (_sel3(x, sel, _NN, False), sel),
                 lambda sel, g: (jnp.zeros_like(sel), _sel3(g, sel, _TN, False)))


@jax.custom_vjp
def _sel_left_nt(sel, x):
    return _sel3(x, sel, _NT, False)


_sel_left_nt.defvjp(lambda sel, x: (_sel3(x, sel, _NT, False), sel),
                    lambda sel, g: (jnp.zeros_like(sel), _sel3(g, sel, _TN, True)))


def _mm(a, b, *, name, ta=False, tb=False, out_dtype=F32, tm=512, tn=512, tk=2048, shards=None):
    m, k = (a.shape[1], a.shape[0]) if ta else a.shape
    n = b.shape[0] if tb else b.shape[1]
    assert k == (b.shape[1] if tb else b.shape[0]), (a.shape, b.shape, ta, tb)
    if shards == 'rows':
        tm = min(tm, m // N_DEV)
    if shards == 'cols':
        tn = n // N_DEV
    tm, tn, tk = _tile(m, tm), _tile(n, tn), _tile(k, tk)
    nk = k // tk
    a_spec = pl.BlockSpec((tk, tm), lambda i, j, kk: (kk, i)) if ta else pl.BlockSpec((tm, tk), lambda i, j, kk: (i, kk))
    b_spec = pl.BlockSpec((tn, tk), lambda i, j, kk: (j, kk)) if tb else pl.BlockSpec((tk, tn), lambda i, j, kk: (kk, j))
    dims = ((0 if ta else 1,), (1 if tb else 0,))
    out_spec = pl.BlockSpec((tm, tn), lambda i, j, kk: (i, j))
    out_shape = jax.ShapeDtypeStruct((m, n), out_dtype)
    if shards == 'rows':
        per = m // N_DEV // tm
        out_shape = jax.ShapeDtypeStruct((N_DEV, 1, m // N_DEV, n), out_dtype)
        out_spec = pl.BlockSpec((None, None, tm, tn), lambda i, j, kk: (i // per, 0, i % per, j))
    elif shards == 'cols':
        out_shape = jax.ShapeDtypeStruct((N_DEV, 1, m, n // N_DEV), out_dtype)
        out_spec = pl.BlockSpec((None, None, tm, tn), lambda i, j, kk: (j, 0, i, 0))

    def body(a_ref, b_ref, o_ref, acc_ref):
        kk = pl.program_id(2)

        @pl.when(kk == 0)
        def _():
            acc_ref[...] = jnp.zeros_like(acc_ref)

        acc_ref[...] += _dot(a_ref[...], b_ref[...], dims)

        @pl.when(kk == nk - 1)
        def _():
            o_ref[...] = acc_ref[...].astype(o_ref.dtype)

    return _pcall(
        body, name=name, grid=(m // tm, n // tn, nk),
        in_specs=[a_spec, b_spec], out_specs=out_spec, out_shape=out_shape,
        scratch_shapes=[pltpu.VMEM((tm, tn), F32)],
        compiler_params=_params("parallel", "parallel", "arbitrary"),
    )(a, b)


def _rspec(tm, w, cb=0, percol=False):
    return pl.BlockSpec((tm, w), (lambda j, i: (i, cb + j)) if percol else (lambda j, i: (i, cb)))


def _bspec(w, cb=0, percol=False, rows=1):
    return pl.BlockSpec((rows, w), (lambda j, i: (0, cb + j)) if percol else (lambda j, i: (0, cb)))


def _rows(fn, *, name, nrow, ncol=1, ins, outs):
    n_in = len(ins)
    accs = [o[2] for o in outs]

    def body(*refs):
        vals = fn(*[r[...] for r in refs[:n_in]])
        if not isinstance(vals, (tuple, list)):
            vals = (vals,)
        i = pl.program_id(1)
        for ref, val, acc in zip(refs[n_in:], vals, accs):
            if acc:
                @pl.when(i == 0)
                def _(ref=ref):
                    ref[...] = jnp.zeros_like(ref)

                ref[...] += jnp.broadcast_to(val, ref.shape).astype(ref.dtype)
            else:
                ref[...] = val.astype(ref.dtype)

    res = _pcall(
        body, name=name, grid=(ncol, nrow),
        in_specs=[s for _, s in ins], out_specs=[o[1] for o in outs], out_shape=[o[0] for o in outs],
        compiler_params=_params("parallel", "arbitrary"),
    )(*[a for a, _ in ins])
    return res


def _sds(shape, dtype=F32):
    return jax.ShapeDtypeStruct(shape, dtype)


def _rms_fwd(h, g, *, name, tm):
    t, d = h.shape
    return _rows(lambda x, gg: _rms(x, gg), name=name, nrow=t // tm,
                 ins=[(h, _rspec(tm, d)), (g, _bspec(d))],
                 outs=[(_sds((t, d), BF16), _rspec(tm, d), False)])[0]


def _resid_fwd(h, f, g, scale, *, name, tm):
    t, d = h.shape
    return _rows(lambda x, ff, gg: x + scale * _rms(ff, gg), name=name, nrow=t // tm,
                 ins=[(h, _rspec(tm, d)), (f, _rspec(tm, d)), (g, _bspec(d))],
                 outs=[(_sds((t, d)), _rspec(tm, d), False)])[0]


def _resid_bwd(f, g, dh, scale, *, name, tm):
    t, d = f.shape

    def fn(ff, gg, dd):
        _, vjp = jax.vjp(lambda a, b: scale * _rms(a, b), ff, gg)
        return vjp(dd)

    return _rows(fn, name=name, nrow=t // tm,
                 ins=[(f, _rspec(tm, d)), (g, _bspec(d)), (dh, _rspec(tm, d))],
                 outs=[(_sds((t, d), BF16), _rspec(tm, d), False), (_sds((1, d)), _bspec(d), True)])


def _rms_bwd(h, g, dh, dxns, *, name, tm):
    t, d = h.shape

    def fn(x, gg, dd, *dx):
        _, vjp = jax.vjp(_rms, x, gg)
        tot = dx[0]
        for more in dx[1:]:
            tot = tot + more
        dxx, dg = vjp(tot)
        return dd + dxx, dg

    return _rows(fn, name=name, nrow=t // tm,
                 ins=[(h, _rspec(tm, d)), (g, _bspec(d)), (dh, _rspec(tm, d))] + [(x, _rspec(tm, d)) for x in dxns],
                 outs=[(_sds((t, d)), _rspec(tm, d), False), (_sds((1, d)), _bspec(d), True)])


NB = FFN_HIDDEN // N_DEV
MM_ROWS = 512


def _ffn_up(xn, wg, wu, *, name):
    t = xn.shape[0]
    tm = _tile(t, MM_ROWS)
    wspec = pl.BlockSpec((None, None, D_MODEL, NB), lambda i, j: (j, 0, 0, 0))

    def body(x_ref, g_ref, u_ref, ab_ref, hh_ref):
        x = x_ref[...]
        a, b = _dot(x, g_ref[...], _NN), _dot(x, u_ref[...], _NN)
        ab_ref[0] = a
        ab_ref[1] = b
        hh_ref[...] = (_silu(a) * b).astype(hh_ref.dtype)

    return _pcall(
        body, name=name, grid=(t // tm, N_DEV),
        in_specs=[pl.BlockSpec((tm, D_MODEL), lambda i, j: (i, 0)), wspec, wspec],
        out_specs=[pl.BlockSpec((None, 2, tm, NB), lambda i, j: (j, 0, i, 0)),
                   pl.BlockSpec((None, tm, NB), lambda i, j: (j, i, 0))],
        out_shape=[_sds((N_DEV, 2, t, NB)), _sds((N_DEV, t, NB), BF16)],
        compiler_params=_params("parallel", "parallel"),
    )(xn, wg, wu)


def _ffn_down(hh, wd, *, name, tn=512):
    t = hh.shape[1]
    tm = _tile(t, MM_ROWS)

    def body(h_ref, w_ref, o_ref, acc_ref):
        kk = pl.program_id(2)

        @pl.when(kk == 0)
        def _():
            acc_ref[...] = jnp.zeros_like(acc_ref)

        acc_ref[...] += _dot(h_ref[...], w_ref[...], _NN)

        @pl.when(kk == N_DEV - 1)
        def _():
            o_ref[...] = acc_ref[...]

    return _pcall(
        body, name=name, grid=(t // tm, D_MODEL // tn, N_DEV),
        in_specs=[pl.BlockSpec((None, tm, NB), lambda i, j, kk: (kk, i, 0)),
                  pl.BlockSpec((None, None, NB, tn), lambda i, j, kk: (kk, 0, 0, j))],
        out_specs=pl.BlockSpec((tm, tn), lambda i, j, kk: (i, j)), out_shape=_sds((t, D_MODEL)),
        scratch_shapes=[pltpu.VMEM((tm, tn), F32)],
        compiler_params=_params("parallel", "parallel", "arbitrary"),
    )(hh, wd)


def _ffn_down_dx(df, wd, ab, *, name):
    t = df.shape[0]
    tm = _tile(t, MM_ROWS)

    def body(d_ref, w_ref, ab_ref, o_ref):
        dhh = _dot(d_ref[...], w_ref[...], _NT)
        _, vjp = jax.vjp(lambda a, b: _silu(a) * b, ab_ref[0], ab_ref[1])
        da, db = vjp(dhh)
        o_ref[0] = da.astype(o_ref.dtype)
        o_ref[1] = db.astype(o_ref.dtype)

    blk = pl.BlockSpec((None, 2, tm, NB), lambda i, j: (j, 0, i, 0))
    return _pcall(
        body, name=name, grid=(t // tm, N_DEV),
        in_specs=[pl.BlockSpec((tm, D_MODEL), lambda i, j: (i, 0)),
                  pl.BlockSpec((None, None, NB, D_MODEL), lambda i, j: (j, 0, 0, 0)), blk],
        out_specs=blk, out_shape=_sds((N_DEV, 2, t, NB), BF16), compiler_params=_params("parallel", "parallel"),
    )(df, wd, ab)


def _ffn_down_dw(hh, df, *, name, tn=512):
    t = df.shape[0]
    tk = _tile(t, 2048)
    nk = t // tk

    def body(h_ref, d_ref, o_ref, acc_ref):
        kk = pl.program_id(2)

        @pl.when(kk == 0)
        def _():
            acc_ref[...] = jnp.zeros_like(acc_ref)

        acc_ref[...] += _dot(h_ref[...], d_ref[...], _TN)

        @pl.when(kk == nk - 1)
        def _():
            o_ref[...] = acc_ref[...].astype(o_ref.dtype)

    return _pcall(
        body, name=name, grid=(N_DEV, D_MODEL // tn, nk),
        in_specs=[pl.BlockSpec((None, tk, NB), lambda j, n, kk: (j, kk, 0)),
                  pl.BlockSpec((tk, tn), lambda j, n, kk: (kk, n))],
        out_specs=pl.BlockSpec((None, None, NB, tn), lambda j, n, kk: (j, 0, 0, n)),
        out_shape=_sds((N_DEV, 1, NB, D_MODEL), BF16),
        scratch_shapes=[pltpu.VMEM((NB, tn), F32)],
        compiler_params=_params("parallel", "parallel", "arbitrary"),
    )(hh, df)


def _ffn_up_dx(dab, wg, wu, *, name):
    t = dab.shape[2]
    tm = _tile(t, MM_ROWS)
    wspec = pl.BlockSpec((None, None, D_MODEL, NB), lambda i, j: (j, 0, 0, 0))

    def body(d_ref, g_ref, u_ref, o_ref):
        @pl.when(pl.program_id(1) == 0)
        def _():
            o_ref[...] = jnp.zeros_like(o_ref)

        o_ref[...] += _dot(d_ref[0], g_ref[...], _NT) + _dot(d_ref[1], u_ref[...], _NT)

    return _pcall(
        body, name=name, grid=(t // tm, N_DEV),
        in_specs=[pl.BlockSpec((None, 2, tm, NB), lambda i, j: (j, 0, i, 0)), wspec, wspec],
        out_specs=pl.BlockSpec((tm, D_MODEL), lambda i, j: (i, 0)), out_shape=_sds((t, D_MODEL)),
        compiler_params=_params("parallel", "arbitrary"),
    )(dab, wg, wu)


def _ffn_up_dw(xn, dab, *, name):
    t = xn.shape[0]
    tk = _tile(t, 2048)
    nk = t // tk

    def body(x_ref, d_ref, og_ref, ou_ref, accg_ref, accu_ref):
        kk = pl.program_id(1)

        @pl.when(kk == 0)
        def _():
            accg_ref[...] = jnp.zeros_like(accg_ref)
            accu_ref[...] = jnp.zeros_like(accu_ref)

        x = x_ref[...]
        accg_ref[...] += _dot(x, d_ref[0], _TN)
        accu_ref[...] += _dot(x, d_ref[1], _TN)

        @pl.when(kk == nk - 1)
        def _():
            og_ref[...] = accg_ref[...].astype(og_ref.dtype)
            ou_ref[...] = accu_ref[...].astype(ou_ref.dtype)

    out = pl.BlockSpec((None, None, D_MODEL, NB), lambda j, kk: (j, 0, 0, 0))
    return _pcall(
        body, name=name, grid=(N_DEV, nk),
        in_specs=[pl.BlockSpec((tk, D_MODEL), lambda j, kk: (kk, 0)),
                  pl.BlockSpec((None, 2, tk, NB), lambda j, kk: (j, 0, kk, 0))],
        out_specs=[out, out], out_shape=[_sds((N_DEV, 1, D_MODEL, NB), BF16)] * 2,
        scratch_shapes=[pltpu.VMEM((D_MODEL, NB), F32)] * 2,
        compiler_params=_params("parallel", "arbitrary"),
    )(xn, dab)


def _ffn_fwd(h, p, tag, tm):
    xn = _rms_fwd(h, p['pre_g'], name=f"{tag}_rms", tm=tm)
    ab, hh = _ffn_up(xn, p['w_gate'], p['w_up'], name=f"{tag}_up")
    f = _ffn_down(hh, p['w_down'], name=f"{tag}_down")
    out = _resid_fwd(h, f, p['post_g'], 0.5, name=f"{tag}_res", tm=tm)
    return out, (h, xn, ab, hh, f)


def _ffn_bwd(dh, p, saved, tag, tm, after_first):
    h, xn, ab, hh, f = saved
    df, dpost = _resid_bwd(f, p['post_g'], dh, 0.5, name=f"{tag}_res_bwd", tm=tm)
    df = after_first(df)
    dab = _ffn_down_dx(df, p['w_down'], ab, name=f"{tag}_down_dx")
    bufs = {f'{tag}_w_down': _ffn_down_dw(hh, df, name=f"{tag}_down_dw")}
    dxn = _ffn_up_dx(dab, p['w_gate'], p['w_up'], name=f"{tag}_up_dx")
    bufs[f'{tag}_w_gate'], bufs[f'{tag}_w_up'] = _ffn_up_dw(xn, dab, name=f"{tag}_up_dw")
    dh_in, dpre = _rms_bwd(h, p['pre_g'], dh, [dxn], name=f"{tag}_rms_bwd", tm=tm)
    return dh_in, bufs, {f'{tag}_pre_g': dpre, f'{tag}_post_g': dpost}


CONV_COLS = 256


def _shift_down(x, s):
    rows = lax.broadcasted_iota(jnp.int32, x.shape, 0)
    return jnp.where(rows >= s, pltpu.roll(x, s, axis=0), 0.0)


def _shift_up(x, s):
    t = x.shape[0]
    rows = lax.broadcasted_iota(jnp.int32, x.shape, 0)
    return jnp.where(rows < t - s, pltpu.roll(x, t - s, axis=0), 0.0)


def _conv_fwd(proj, col0, w, b, *, name):
    t = proj.shape[0]
    c = w.shape[1]
    cb0 = col0 // CONV_COLS

    def body(x_ref, w_ref, b_ref, o_ref):
        x = x_ref[...]
        acc = x * w_ref[3:4, :] + b_ref[...]
        for k in range(SSD_CONV_WIDTH - 1):
            acc = acc + _shift_down(x, SSD_CONV_WIDTH - 1 - k) * w_ref[k:k + 1, :]
        o_ref[...] = _silu(acc)

    return _pcall(
        body, name=name, grid=(c // CONV_COLS,),
        in_specs=[pl.BlockSpec((t, CONV_COLS), lambda j: (0, cb0 + j)),
                  pl.BlockSpec((SSD_CONV_WIDTH, CONV_COLS), lambda j: (0, j)),
                  pl.BlockSpec((1, CONV_COLS), lambda j: (0, j))],
        out_specs=pl.BlockSpec((t, CONV_COLS), lambda j: (0, j)),
        out_shape=_sds((t, c)), compiler_params=_params("parallel"),
    )(proj, w, b)


def _conv_bwd(proj, col0, w, b, dout, *, name):
    t = proj.shape[0]
    c = w.shape[1]
    cb0 = col0 // CONV_COLS

    def body(x_ref, w_ref, b_ref, d_ref, dx_ref, dw_ref, db_ref):
        x = x_ref[...]
        shifted = [_shift_down(x, SSD_CONV_WIDTH - 1 - k) for k in range(SSD_CONV_WIDTH - 1)] + [x]
        pre = b_ref[...] + shifted[3] * w_ref[3:4, :]
        for k in range(SSD_CONV_WIDTH - 1):
            pre = pre + shifted[k] * w_ref[k:k + 1, :]
        sg = _sigmoid(pre)
        dpre = d_ref[...] * (sg * (1.0 + pre * (1.0 - sg)))
        dx = dpre * w_ref[3:4, :]
        for k in range(SSD_CONV_WIDTH - 1):
            dx = dx + _shift_up(dpre, SSD_CONV_WIDTH - 1 - k) * w_ref[k:k + 1, :]
        dx_ref[...] = dx.astype(dx_ref.dtype)
        for k in range(SSD_CONV_WIDTH):
            dw_ref[k:k + 1, :] = jnp.sum(dpre * shifted[k], axis=0, keepdims=True)
        db_ref[...] = jnp.sum(dpre, axis=0, keepdims=True)

    return _pcall(
        body, name=name, grid=(c // CONV_COLS,),
        in_specs=[pl.BlockSpec((t, CONV_COLS), lambda j: (0, cb0 + j)),
                  pl.BlockSpec((SSD_CONV_WIDTH, CONV_COLS), lambda j: (0, j)),
                  pl.BlockSpec((1, CONV_COLS), lambda j: (0, j)),
                  pl.BlockSpec((t, CONV_COLS), lambda j: (0, j))],
        out_specs=[pl.BlockSpec((t, CONV_COLS), lambda j: (0, j)),
                   pl.BlockSpec((SSD_CONV_WIDTH, CONV_COLS), lambda j: (0, j)),
                   pl.BlockSpec((1, CONV_COLS), lambda j: (0, j))],
        out_shape=[_sds((t, c), BF16), _sds((SSD_CONV_WIDTH, c)), _sds((1, c))],
        compiler_params=_params("parallel"),
    )(proj, w, b, dout)


HALF = 256
HEADS_PER_HALF = 4
PAD_HEADS = 128


def _head_expanders():
    k = lax.broadcasted_iota(jnp.int32, (PAD_HEADS, HALF), 0)
    j = lax.broadcasted_iota(jnp.int32, (PAD_HEADS, HALF), 1)
    kt = lax.broadcasted_iota(jnp.int32, (HALF, PAD_HEADS), 1)
    jt = lax.broadcasted_iota(jnp.int32, (HALF, PAD_HEADS), 0)
    es, ets = [], []
    for half in range(2):
        es.append(jnp.where(k == j // SSD_HEAD_DIM + half * HEADS_PER_HALF, 1.0, 0.0).astype(F32))
        ets.append(jnp.where(kt == jt // SSD_HEAD_DIM + half * HEADS_PER_HALF, 1.0, 0.0).astype(F32))
    return es, ets


def _ssd_chunk(x_lo, x_hi, bm, cm, dtr, dtb8, alog8, dsk8, s_lo, s_hi):
    q = x_lo.shape[0]
    es, ets = _head_expanders()
    rowmean = lambda v: jnp.sum(v, axis=0, keepdims=True) * 0.125
    dt = _softplus(dtr + rowmean(dtb8))
    a = -jnp.exp(rowmean(alog8))
    adt = a * dt
    adt_tot8 = jnp.broadcast_to(jnp.sum(adt, axis=0, keepdims=True), (8, PAD_HEADS))
    ll = lax.broadcasted_iota(jnp.int32, (q, q), 0)
    ss = lax.broadcasted_iota(jnp.int32, (q, q), 1)
    ltri = jnp.where(ll >= ss, 1.0, 0.0).astype(F32)
    lane = lax.broadcasted_iota(jnp.int32, (1, HALF), 1)
    cb = _bdot_nt(cm, bm)
    outs = []
    for half, (x, s_in) in enumerate(((x_lo, s_lo), (x_hi, s_hi))):
        e, et = es[half], ets[half]
        dtf = _sel_right(dt, e)
        af = _sel_right(adt, e)
        dskf = rowmean(_sel_right(dsk8, e))
        acum = _sel_left(ltri, af)
        alast = jnp.sum(af, axis=0, keepdims=True)
        xdt = x * dtf
        ydiag = jnp.zeros((q, HALF), F32)
        for r in range(HEADS_PER_HALF):
            sel = lane == r * SSD_HEAD_DIM
            ac_r = jnp.sum(jnp.where(sel, acum, 0.0), axis=1, keepdims=True)
            a_r = jnp.sum(jnp.where(sel, af, 0.0), axis=1, keepdims=True)
            arow = jnp.sum(jnp.where(ll <= ss, a_r, 0.0), axis=0, keepdims=True)
            decay = jnp.exp(jnp.where(ll >= ss, ac_r - arow, -jnp.inf))
            yh = _bdot_nn(cb * decay, xdt)
            ydiag = ydiag + jnp.where(lane // SSD_HEAD_DIM == r, yh, 0.0)
        st = _bdot_tn(xdt * jnp.exp(alast - acum), bm)
        yoff = _bdot_nt(cm, s_in) * jnp.exp(acum)
        y = ydiag + yoff + dskf * x
        alast_col = jnp.sum(_sel_left_nt(et, adt_tot8), axis=1, keepdims=True) * 0.125
        outs.append((y, jnp.exp(alast_col) * s_in + st))
    return outs[0][0], outs[1][0], outs[0][1], outs[1][1]


def _ssd_specs(t, rev):
    q = SSD_CHUNK
    nc = t // q
    ci = (lambda c: nc - 1 - c) if rev else (lambda c: c)
    xcol0 = SSD_D_INNER // SSD_STATE
    return dict(
        x_lo=pl.BlockSpec((q, HALF), lambda g, c: (ci(c), 2 * g)),
        x_hi=pl.BlockSpec((q, HALF), lambda g, c: (ci(c), 2 * g + 1)),
        bm=pl.BlockSpec((q, SSD_STATE), lambda g, c: (ci(c), xcol0 + g)),
        cm=pl.BlockSpec((q, SSD_STATE), lambda g, c: (ci(c), xcol0 + SSD_GROUPS + g)),
        dt=pl.BlockSpec((None, q, PAD_HEADS), lambda g, c: (g, ci(c), 0)),
        par=pl.BlockSpec((None, 8, PAD_HEADS), lambda g, c: (g, 0, 0)),
        st=pl.BlockSpec((None, None, 2, HALF, SSD_STATE), lambda g, c: (ci(c), g, 0, 0, 0)),
        y=pl.BlockSpec((q, 2 * HALF), lambda g, c: (ci(c), g)),
        grp=pl.BlockSpec((q, SSD_STATE), lambda g, c: (ci(c), g)),
    )


def _ssd_fwd(xc, dt4, dtb, alog, dsk, *, name):
    t = xc.shape[0]
    nc = t // SSD_CHUNK
    sp = _ssd_specs(t, False)

    def body(xl, xh, bm, cm, dt, p0, p1, p2, y_ref, sin_ref, st_ref):
        @pl.when(pl.program_id(1) == 0)
        def _():
            st_ref[...] = jnp.zeros_like(st_ref)

        sin_ref[...] = st_ref[...]
        y_lo, y_hi, so_lo, so_hi = _ssd_chunk(xl[...], xh[...], bm[...], cm[...], dt[...], p0[...], p1[...],
                                              p2[...], st_ref[0], st_ref[1])
        y_ref[:, :HALF] = y_lo
        y_ref[:, HALF:] = y_hi
        st_ref[0] = so_lo
        st_ref[1] = so_hi

    return _pcall(
        body, name=name, grid=(SSD_GROUPS, nc),
        in_specs=[sp['x_lo'], sp['x_hi'], sp['bm'], sp['cm'], sp['dt'], sp['par'], sp['par'], sp['par']],
        out_specs=[sp['y'], sp['st']],
        out_shape=[_sds((t, SSD_D_INNER)), _sds((nc, SSD_GROUPS, 2, HALF, SSD_STATE))],
        scratch_shapes=[pltpu.VMEM((2, HALF, SSD_STATE), F32)],
        compiler_params=_params("parallel", "arbitrary"),
    )(xc, xc, xc, xc, dt4, dtb, alog, dsk)


def _ssd_bwd(xc, dt4, dtb, alog, dsk, sin, dy, *, name):
    t = xc.shape[0]
    nc = t // SSD_CHUNK
    sp = _ssd_specs(t, True)

    def body(xl, xh, bm, cm, dt, p0, p1, p2, sin_ref, dy_ref,
             dx_ref, db_ref, dc_ref, ddt_ref, dp0, dp1, dp2, dst_ref):
        first = pl.program_id(1) == 0

        @pl.when(first)
        def _():
            dst_ref[...] = jnp.zeros_like(dst_ref)

        _, vjp = jax.vjp(_ssd_chunk, xl[...], xh[...], bm[...], cm[...], dt[...], p0[...], p1[...], p2[...],
                         sin_ref[0], sin_ref[1])
        dxl, dxh, dbm, dcm, ddt, g0, g1, g2, ds_lo, ds_hi = vjp(
            (dy_ref[:, :HALF], dy_ref[:, HALF:], dst_ref[0], dst_ref[1]))
        dx_ref[:, :HALF] = dxl
        dx_ref[:, HALF:] = dxh
        db_ref[...] = dbm
        dc_ref[...] = dcm
        ddt_ref[...] = ddt
        dst_ref[0] = ds_lo
        dst_ref[1] = ds_hi
        for ref, g in ((dp0, g0), (dp1, g1), (dp2, g2)):
            tot = jnp.broadcast_to(jnp.sum(g, axis=0, keepdims=True), ref.shape)

            @pl.when(first)
            def _(ref=ref):
                ref[...] = jnp.zeros_like(ref)

            ref[...] += tot

    return _pcall(
        body, name=name, grid=(SSD_GROUPS, nc),
        in_specs=[sp['x_lo'], sp['x_hi'], sp['bm'], sp['cm'], sp['dt'], sp['par'], sp['par'], sp['par'],
                  sp['st'], sp['y']],
        out_specs=[sp['y'], sp['grp'], sp['grp'], sp['dt'], sp['par'], sp['par'], sp['par']],
        out_shape=[_sds((t, SSD_D_INNER)), _sds((t, SSD_GROUPS * SSD_STATE)), _sds((t, SSD_GROUPS * SSD_STATE)),
                   _sds((SSD_GROUPS, t, PAD_HEADS))] + [_sds((SSD_GROUPS, 8, PAD_HEADS))] * 3,
        scratch_shapes=[pltpu.VMEM((2, HALF, SSD_STATE), F32)],
        compiler_params=_params("parallel", "arbitrary"),
    )(xc, xc, xc, xc, dt4, dtb, alog, dsk, sin, dy)


def _gatenorm(y, z, g):
    v = y * _silu(z)
    return v * lax.rsqrt(jnp.mean(v * v, axis=-1, keepdims=True) + RMS_EPS) * g


S5_CH = S5_WIDTH // S5_BLOCKS
S5_ST = S5_CH * S5_STATE // S5_GROUP
SCAN_UNROLL = 8


def _cmul(ar, ai, br, bi):
    return ar * br - ai * bi, ar * bi + ai * br


def _segment_power(ar, ai, n):
    assert n & (n - 1) == 0
    for _ in range(n.bit_length() - 1):
        ar, ai = _cmul(ar, ai, ar, ai)
    return ar, ai


def _carry_in(fr, fi, pr, pi, reverse):
    rows = lax.broadcasted_iota(jnp.int32, fr.shape, 0)
    cr = jnp.zeros_like(fr[0:1])
    ci = jnp.zeros_like(cr)
    outr = jnp.zeros_like(fr)
    outi = jnp.zeros_like(fr)
    order = range(6, -1, -1) if reverse else range(1, 8)
    for j in order:
        src = j + 1 if reverse else j - 1
        nr, ni = _cmul(pr[0:1], pi[0:1], cr, ci)
        cr, ci = nr + fr[src:src + 1], ni + fi[src:src + 1]
        outr = jnp.where(rows == j, cr, outr)
        outi = jnp.where(rows == j, ci, outi)
    return outr, outi


def _s5_specs(t):
    return dict(ch=pl.BlockSpec((t, S5_CH), lambda j: (0, j)), st=pl.BlockSpec((t, S5_ST), lambda j: (0, j)),
                lam=pl.BlockSpec((1, S5_ST), lambda j: (0, j)),
                b=pl.BlockSpec((None, S5_CH, S5_ST), lambda j: (j, 0, 0)),
                c=pl.BlockSpec((None, S5_ST, S5_CH), lambda j: (j, 0, 0)))


def _s5_fwd(u5, bre, bim, cre, cim, lr, li, *, name):
    t = u5.shape[0]
    nrt = t // 8

    def body(u_ref, bre_ref, bim_ref, cre_ref, cim_ref, lr_ref, li_ref, sr_ref, si_ref, y_ref, br_ref, bi_ref):
        u = u_ref[...]
        br_ref[...] = _dot(u, bre_ref[...], _NN)
        bi_ref[...] = _dot(u, bim_ref[...], _NN)
        ar = jnp.broadcast_to(lr_ref[...], (8, S5_ST))
        ai = jnp.broadcast_to(li_ref[...], (8, S5_ST))

        def step(r, s, store):
            rows = pl.ds(pl.multiple_of(r * 8, 8), 8)
            nr, ni = _cmul(ar, ai, s[0], s[1])
            nr, ni = nr + br_ref[rows, :], ni + bi_ref[rows, :]
            if store:
                sr_ref[rows, :] = nr
                si_ref[rows, :] = ni
            return nr, ni

        zero = (jnp.zeros((8, S5_ST), F32), jnp.zeros((8, S5_ST), F32))
        fr, fi = lax.fori_loop(0, nrt, lambda r, s: step(r, s, False), zero, unroll=SCAN_UNROLL)
        pr, pi = _segment_power(ar, ai, nrt)
        init = _carry_in(fr, fi, pr, pi, False)
        lax.fori_loop(0, nrt, lambda r, s: step(r, s, True), init, unroll=SCAN_UNROLL)
        y_ref[...] = _dot(sr_ref[...], cre_ref[...], _NN) - _dot(si_ref[...], cim_ref[...], _NN)

    sp = _s5_specs(t)
    w = S5_BLOCKS * S5_ST
    return _pcall(
        body, name=name, grid=(S5_BLOCKS,),
        in_specs=[sp['ch'], sp['b'], sp['b'], sp['c'], sp['c'], sp['lam'], sp['lam']],
        out_specs=[sp['st'], sp['st'], sp['ch']], out_shape=[_sds((t, w)), _sds((t, w)), _sds((t, S5_WIDTH))],
        scratch_shapes=[pltpu.VMEM((t, S5_ST), F32)] * 2, compiler_params=_params("parallel"),
    )(u5, bre, bim, cre, cim, lr, li)


def _s5_bwd(dy, du_direct, u5, sr, si, bre, bim, cre, cim, lr, li, *, name):
    t = u5.shape[0]
    nrt = t // 8

    def body(dy_ref, dd_ref, u_ref, sr_ref, si_ref, bre_ref, bim_ref, cre_ref, cim_ref, lr_ref, li_ref,
             du_ref, dbre_ref, dbim_ref, dcre_ref, dcim_ref, dlr_ref, dli_ref, gr_ref, gi_ref):
        dyv = dy_ref[...]
        gr_ref[...] = _dot(dyv, cre_ref[...], _NT)
        gi_ref[...] = -_dot(dyv, cim_ref[...], _NT)
        dcre_ref[...] = _dot(sr_ref[...], dyv, _TN)
        dcim_ref[...] = -_dot(si_ref[...], dyv, _TN)
        dr_ref, di_ref = gr_ref, gi_ref
        ar = jnp.broadcast_to(lr_ref[...], (8, S5_ST))
        ai = -jnp.broadcast_to(li_ref[...], (8, S5_ST))
        zero = jnp.zeros((8, S5_ST), F32)

        def step1(k, g):
            rows = pl.ds(pl.multiple_of((nrt - 1 - k) * 8, 8), 8)
            nr, ni = _cmul(ar, ai, g[0], g[1])
            return nr + dr_ref[rows, :], ni + di_ref[rows, :]

        fr, fi = lax.fori_loop(0, nrt, step1, (zero, zero), unroll=SCAN_UNROLL)
        pr, pi = _segment_power(ar, ai, nrt)
        init = _carry_in(fr, fi, pr, pi, True)

        def step2(k, carry):
            gr, gi, accr, acci = carry
            r = nrt - 1 - k
            rows = pl.ds(pl.multiple_of(r * 8, 8), 8)
            prev = pl.ds(pl.multiple_of(jnp.maximum(r - 1, 0) * 8, 8), 8)
            nr, ni = _cmul(ar, ai, gr, gi)
            nr, ni = nr + dr_ref[rows, :], ni + di_ref[rows, :]
            gr_ref[rows, :] = nr
            gi_ref[rows, :] = ni
            keep = jnp.where(r > 0, 1.0, 0.0)
            pr_, pi_ = sr_ref[prev, :] * keep, si_ref[prev, :] * keep
            return nr, ni, accr + (pr_ * nr + pi_ * ni), acci + (pr_ * ni - pi_ * nr)

        _, _, accr, acci = lax.fori_loop(0, nrt, step2, (init[0], init[1], zero, zero), unroll=SCAN_UNROLL)
        last = pl.ds((nrt - 1) * 8, 8)
        pr_, pi_ = _shift_down(sr_ref[last, :], 1), _shift_down(si_ref[last, :], 1)
        g0r, g0i = gr_ref[0:8, :], gi_ref[0:8, :]
        accr = accr + (pr_ * g0r + pi_ * g0i)
        acci = acci + (pr_ * g0i - pi_ * g0r)
        dlr_ref[...] = jnp.sum(accr, axis=0, keepdims=True)
        dli_ref[...] = jnp.sum(acci, axis=0, keepdims=True)
        u = u_ref[...]
        dbre_ref[...] = _dot(u, gr_ref[...], _TN)
        dbim_ref[...] = _dot(u, gi_ref[...], _TN)
        du = dd_ref[...] + _dot(gr_ref[...], bre_ref[...], _NT) + _dot(gi_ref[...], bim_ref[...], _NT)
        du_ref[...] = du.astype(du_ref.dtype)

    sp = _s5_specs(t)
    w = S5_BLOCKS * S5_ST
    return _pcall(
        body, name=name, grid=(S5_BLOCKS,),
        in_specs=[sp['ch'], sp['ch'], sp['ch'], sp['st'], sp['st'], sp['b'], sp['b'], sp['c'], sp['c'], sp['lam'], sp['lam']],
        out_specs=[sp['ch'], sp['b'], sp['b'], sp['c'], sp['c'], sp['lam'], sp['lam']],
        out_shape=[_sds((t, S5_WIDTH), BF16), _sds((S5_BLOCKS, S5_CH, S5_ST)), _sds((S5_BLOCKS, S5_CH, S5_ST)),
                   _sds((S5_BLOCKS, S5_ST, S5_CH)), _sds((S5_BLOCKS, S5_ST, S5_CH)), _sds((1, w)), _sds((1, w))],
        scratch_shapes=[pltpu.VMEM((t, S5_ST), F32)] * 2, compiler_params=_params("parallel"),
    )(dy, du_direct, u5, sr, si, bre, bim, cre, cim, lr, li)


def _s5_expander():
    n = lax.broadcasted_iota(jnp.int32, (S5_STATE, S5_STATE * S5_GROUP), 0)
    j = lax.broadcasted_iota(jnp.int32, (S5_STATE, S5_STATE * S5_GROUP), 1)
    return jnp.where(n == j // S5_GROUP, 1.0, 0.0).astype(F32)


def _s5_discretise(lam_re, lam_im, log_step, b_re, b_im):
    lr = jnp.minimum(lam_re, S5_MAX_REAL)
    step = jnp.exp(log_step)
    mag = jnp.exp(lr * step)
    ang = lam_im * step
    lbr, lbi = mag * jnp.cos(ang), mag * jnp.sin(ang)
    p, q = lbr - 1.0, lbi
    den = lr * lr + lam_im * lam_im
    cr, ci = (p * lr + q * lam_im) / den, (q * lr - p * lam_im) / den
    e = _s5_expander()
    cre, cie = _fdot(cr, e), _fdot(ci, e)
    return lbr, lbi, cre * b_re - cie * b_im, cre * b_im + cie * b_re


def _s5_params_fwd(lam_re, lam_im, log_step, b_re, b_im, *, name):
    g, n, w = S5_GROUPS, S5_STATE, S5_STATE * S5_GROUP

    def body(a, b, c, d, e, o0, o1, o2, o3):
        for ref, val in zip((o0, o1, o2, o3), _s5_discretise(a[...], b[...], c[...], d[...], e[...])):
            ref[...] = val

    return _pcall(body, name=name, out_shape=[_sds((g, n)), _sds((g, n)), _sds((g, w)), _sds((g, w))])(
        lam_re, lam_im, log_step, b_re, b_im)


def _s5_params_bwd(lam_re, lam_im, log_step, b_re, b_im, cts, *, name):
    g, n, w = S5_GROUPS, S5_STATE, S5_STATE * S5_GROUP

    def body(a, b, c, d, e, c0, c1, c2, c3, o0, o1, o2, o3, o4):
        _, vjp = jax.vjp(_s5_discretise, a[...], b[...], c[...], d[...], e[...])
        for ref, val in zip((o0, o1, o2, o3, o4), vjp((c0[...], c1[...], c2[...], c3[...]))):
            ref[...] = val

    return _pcall(body, name=name,
                  out_shape=[_sds((g, n)), _sds((g, n)), _sds((g, 1)), _sds((g, w)), _sds((g, w))])(
        lam_re, lam_im, log_step, b_re, b_im, *cts)


def _perm(a):
    t, c = a.shape
    return a.reshape(8, t // 8, c).transpose(1, 0, 2).reshape(t, c)


def _unperm(a):
    t, c = a.shape
    return a.reshape(t // 8, 8, c).transpose(1, 0, 2).reshape(t, c)


def _blockdiag(m, rows_inner, cols_inner):
    m = m.reshape(S5_BLOCKS, 8, rows_inner, cols_inner)
    eye = jnp.eye(8, dtype=m.dtype)
    out = m[:, :, :, None, :] * eye[None, :, None, :, None]
    return out.reshape(S5_BLOCKS, 8 * rows_inner, 8 * cols_inner)


def _blockdiag_extract(m, rows_inner, cols_inner):
    m = m.reshape(S5_BLOCKS, 8, rows_inner, 8, cols_inner)
    d = jnp.diagonal(m, axis1=1, axis2=3)
    return d.transpose(0, 3, 1, 2).reshape(S5_GROUPS, rows_inner, cols_inner)


Z0, XBC0, GA0, GB0 = 0, SSD_D_INNER, SSD_D_INNER + SSD_CONV_DIM, SSD_D_INNER + SSD_CONV_DIM + D_MODEL
BIG = GB0 + D_MODEL


def _mixer_fwd(h, p, tm):
    t = h.shape[0]
    nrow = t // tm
    u = _rms_fwd(h, p['pre_g'], name="mix_rms", tm=tm)
    u_p = _perm(u)
    proj = _mm(u, p['w_big'], name="mix_in")
    dtr = _mm(u, p['w_dt'], name="mix_in_dt")
    u5 = _mm(u_p, p['w_u5'], name="mix_in_s5")
    xc = _conv_fwd(proj, XBC0, p['conv_w'], p['conv_b'], name="ssd_conv")
    dt4 = jnp.pad(dtr.reshape(t, SSD_GROUPS, 8).transpose(1, 0, 2), ((0, 0), (0, 0), (0, PAD_HEADS - 8)))
    y_ssd, s_in = _ssd_fwd(xc, dt4, p['dt_bias8'], p['a_log8'], p['d8'], name="ssd_scan")
    gw = SSD_D_INNER // SSD_GROUPS
    ya = _rows(_gatenorm, name="ssd_gate", nrow=nrow, ncol=SSD_GROUPS,
               ins=[(y_ssd, _rspec(tm, gw, 0, True)), (proj, _rspec(tm, gw, Z0 // gw, True)),
                    (p['norm_g'], _bspec(gw, 0, True))],
               outs=[(_sds((t, SSD_D_INNER), BF16), _rspec(tm, gw, 0, True), False)])[0]
    y_a = _mm(ya, p['w_a'], name="mix_a")
    lbr, lbi, bbr, bbi = _s5_params_fwd(p['lam_re'], p['lam_im'], p['log_step'], p['b_re'], p['b_im'], name="s5_par")
    bd = lambda m: _blockdiag(m.reshape(S5_GROUPS, S5_STATE, S5_GROUP).transpose(0, 2, 1), S5_GROUP, S5_STATE).astype(BF16)
    bre, bim = bd(bbr), bd(bbi)
    cre = _blockdiag(p['c_re'].transpose(0, 2, 1), S5_STATE, S5_GROUP).astype(BF16)
    cim = _blockdiag(p['c_im'].transpose(0, 2, 1), S5_STATE, S5_GROUP).astype(BF16)
    lr, li = lbr.reshape(1, -1), lbi.reshape(1, -1)
    sr, si, y5 = _s5_fwd(u5, bre, bim, cre, cim, lr, li, name="s5_scan")
    y5g = _rows(lambda a, b, d: _gelu(a + d * b), name="s5_act", nrow=nrow,
                ins=[(y5, _rspec(tm, S5_WIDTH)), (u5, _rspec(tm, S5_WIDTH)), (p['s5_d'], _bspec(S5_WIDTH))],
                outs=[(_sds((t, S5_WIDTH), BF16), _rspec(tm, S5_WIDTH), False)])[0]
    vg = _mm(y5g, p['w_glu'], name="s5_glu")
    ybin = _rows(lambda a, b: a * _sigmoid(b), name="s5_glu_act", nrow=nrow,
                 ins=[(vg, _rspec(tm, S5_WIDTH, 0)), (vg, _rspec(tm, S5_WIDTH, 1))],
                 outs=[(_sds((t, S5_WIDTH), BF16), _rspec(tm, S5_WIDTH), False)])[0]
    y_b = _unperm(_mm(ybin, p['w_b'], name="mix_b"))
    merged = _rows(lambda ga, gb, a, b: _sigmoid(ga) * a + _sigmoid(gb) * b, name="mix_merge", nrow=nrow,
                   ins=[(proj, _rspec(tm, D_MODEL, GA0 // D_MODEL)), (proj, _rspec(tm, D_MODEL, GB0 // D_MODEL)),
                        (y_a, _rspec(tm, D_MODEL)), (y_b, _rspec(tm, D_MODEL))],
                   outs=[(_sds((t, D_MODEL), BF16), _rspec(tm, D_MODEL), False)])[0]
    m = _mm(merged, p['w_out'], name="mix_out")
    out = _resid_fwd(h, m, p['post_g'], 1.0, name="mix_res", tm=tm)
    saved = dict(h=h, u=u, u_p=u_p, proj=proj, u5=u5, xc=xc, dt4=dt4, s_in=s_in, y_ssd=y_ssd, ya=ya, y_a=y_a,
                 bre=bre, bim=bim, cre=cre, cim=cim, lr=lr, li=li, sr=sr, si=si, y5=y5, y5g=y5g, vg=vg, ybin=ybin,
                 y_b=y_b, merged=merged, m=m)
    return out, saved


def _mixer_bwd(dh, p, s, tm, after_first):
    t = dh.shape[0]
    nrow = t // tm
    proj = s['proj']
    bufs = {}

    def grad_mm(a, b, wname, axis, name):
        bufs[wname] = _mm(a, b, ta=True, name=name, out_dtype=BF16, shards=axis)

    dm, dpost = _resid_bwd(s['m'], p['post_g'], dh, 1.0, name="mix_res_bwd", tm=tm)
    dm = after_first(dm)
    dmerged = _mm(dm, p['w_out'], tb=True, name="mix_out_dx")
    grad_mm(s['merged'], dm, 'w_out', 'rows', "mix_out_dw")

    def merge_bwd(ga, gb, a, b, d):
        _, vjp = jax.vjp(lambda ga_, gb_, a_, b_: _sigmoid(ga_) * a_ + _sigmoid(gb_) * b_, ga, gb, a, b)
        dga, dgb, da, db = vjp(d)
        return jnp.concatenate([dga, dgb], axis=1), da, db

    dgab, dy_a, dy_b = _rows(
        merge_bwd, name="mix_merge_bwd", nrow=nrow,
        ins=[(proj, _rspec(tm, D_MODEL, GA0 // D_MODEL)), (proj, _rspec(tm, D_MODEL, GB0 // D_MODEL)),
             (s['y_a'], _rspec(tm, D_MODEL)), (s['y_b'], _rspec(tm, D_MODEL)), (dmerged, _rspec(tm, D_MODEL))],
        outs=[(_sds((t, 2 * D_MODEL), BF16), _rspec(tm, 2 * D_MODEL), False),
              (_sds((t, D_MODEL), BF16), _rspec(tm, D_MODEL), False),
              (_sds((t, D_MODEL), BF16), _rspec(tm, D_MODEL), False)])
    dya = _mm(dy_a, p['w_a'], tb=True, name="mix_a_dx")
    grad_mm(s['ya'], dy_a, 'w_branch_a', 'rows', "mix_a_dw")
    gw = SSD_D_INNER // SSD_GROUPS

    def gate_bwd(y, z, g, d):
        _, vjp = jax.vjp(_gatenorm, y, z, g)
        return vjp(d)

    dy_ssd, dz, dnorm = _rows(
        gate_bwd, name="ssd_gate_bwd", nrow=nrow, ncol=SSD_GROUPS,
        ins=[(s['y_ssd'], _rspec(tm, gw, 0, True)), (proj, _rspec(tm, gw, Z0 // gw, True)),
             (p['norm_g'], _bspec(gw, 0, True)), (dya, _rspec(tm, gw, 0, True))],
        outs=[(_sds((t, SSD_D_INNER)), _rspec(tm, gw, 0, True), False),
              (_sds((t, SSD_D_INNER), BF16), _rspec(tm, gw, 0, True), False),
              (_sds((1, SSD_D_INNER)), _bspec(gw, 0, True), True)])
    dxs, dbm, dcm, ddt4, ddtb, dalog, ddsk = _ssd_bwd(s['xc'], s['dt4'], p['dt_bias8'], p['a_log8'], p['d8'],
                                                      s['s_in'], dy_ssd, name="ssd_scan_bwd")
    dxc = jnp.concatenate([dxs, dbm, dcm], axis=1)
    dxbc, dconv_w, dconv_b = _conv_bwd(proj, XBC0, p['conv_w'], p['conv_b'], dxc, name="ssd_conv_bwd")
    ddtr = ddt4[:, :, :8].transpose(1, 0, 2).reshape(t, SSD_HEADS)
    dy_bp = _perm(dy_b)
    dybin = _mm(dy_bp, p['w_b'], tb=True, name="mix_b_dx")
    grad_mm(s['ybin'], dy_bp, 'w_branch_b', 'rows', "mix_b_dw")

    def glu_bwd(a, b, d):
        _, vjp = jax.vjp(lambda a_, b_: a_ * _sigmoid(b_), a, b)
        da, db = vjp(d)
        return jnp.concatenate([da, db], axis=1)

    dvg = _rows(glu_bwd, name="s5_glu_act_bwd", nrow=nrow,
                ins=[(s['vg'], _rspec(tm, S5_WIDTH, 0)), (s['vg'], _rspec(tm, S5_WIDTH, 1)), (dybin, _rspec(tm, S5_WIDTH))],
                outs=[(_sds((t, 2 * S5_WIDTH), BF16), _rspec(tm, 2 * S5_WIDTH), False)])[0]
    dy5g = _mm(dvg, p['w_glu'], tb=True, name="s5_glu_dx")
    grad_mm(s['y5g'], dvg, 's5_w_glu', 'cols', "s5_glu_dw")

    def act_bwd(a, b, d, g):
        _, vjp = jax.vjp(lambda a_, b_, d_: _gelu(a_ + d_ * b_), a, b, d)
        return vjp(g)

    dy5, du5_direct, ds5d = _rows(
        act_bwd, name="s5_act_bwd", nrow=nrow,
        ins=[(s['y5'], _rspec(tm, S5_WIDTH)), (s['u5'], _rspec(tm, S5_WIDTH)), (p['s5_d'], _bspec(S5_WIDTH)),
             (dy5g, _rspec(tm, S5_WIDTH))],
        outs=[(_sds((t, S5_WIDTH), BF16), _rspec(tm, S5_WIDTH), False), (_sds((t, S5_WIDTH)), _rspec(tm, S5_WIDTH), False),
              (_sds((1, S5_WIDTH)), _bspec(S5_WIDTH), True)])
    du5, dbre, dbim, dcre, dcim, dlr, dli = _s5_bwd(dy5, du5_direct, s['u5'], s['sr'], s['si'], s['bre'], s['bim'],
                                                     s['cre'], s['cim'], s['lr'], s['li'], name="s5_scan_bwd")
    du_p = _mm(du5, p['w_u5'], tb=True, name="mix_in_s5_dx")
    dw_u5 = _mm(s['u_p'], du5, ta=True, name="mix_in_s5_dw", out_dtype=BF16)
    ext_b = lambda m: _blockdiag_extract(m, S5_GROUP, S5_STATE).transpose(0, 2, 1).reshape(S5_GROUPS, S5_STATE * S5_GROUP)
    dlam_re, dlam_im, dlog_step, db_re, db_im = _s5_params_bwd(
        p['lam_re'], p['lam_im'], p['log_step'], p['b_re'], p['b_im'],
        (dlr.reshape(S5_GROUPS, S5_STATE), dli.reshape(S5_GROUPS, S5_STATE), ext_b(dbre), ext_b(dbim)), name="s5_par_bwd")
    dc_re = _blockdiag_extract(dcre, S5_STATE, S5_GROUP).transpose(0, 2, 1)
    dc_im = _blockdiag_extract(dcim, S5_STATE, S5_GROUP).transpose(0, 2, 1)
    dproj = jnp.concatenate([dz, dxbc, dgab], axis=1)
    du_big = _mm(dproj, p['w_big'], tb=True, name="mix_in_dx")
    du_dt = _mm(ddtr, p['w_dt'], tb=True, name="mix_in_dt_dx")
    dw_big = _mm(s['u'], dproj, ta=True, name="mix_in_dw", out_dtype=BF16)
    dw_dt = _mm(s['u'], ddtr, ta=True, name="mix_in_dt_dw", out_dtype=BF16)
    dh_in, dpre = _rms_bwd(s['h'], p['pre_g'], dh, [du_big, du_dt, _unperm(du_p)], name="mix_rms_bwd", tm=tm)
    dw_in = jnp.concatenate([dw_big[:, :GA0], dw_dt, dw_u5, dw_big[:, GA0:]], axis=1)
    bufs['w_in'] = dw_in.reshape(D_MODEL, N_DEV, -1).transpose(1, 0, 2)[:, None]
    grads = {
        'mix_pre_g': dpre, 'mix_post_g': dpost, 'ssd_conv_w': dconv_w, 'ssd_conv_b': dconv_b,
        'ssd_dt_bias': ddtb[:, 0, :8].reshape(-1), 'ssd_a_log': dalog[:, 0, :8].reshape(-1),
        'ssd_d': ddsk[:, 0, :8].reshape(-1), 'ssd_norm_g': dnorm,
        's5_lambda_re': dlam_re, 's5_lambda_im': dlam_im,
        's5_b_re': db_re.reshape(S5_GROUPS, S5_STATE, S5_GROUP), 's5_b_im': db_im.reshape(S5_GROUPS, S5_STATE, S5_GROUP),
        's5_c_re': dc_re, 's5_c_im': dc_im, 's5_log_step': dlog_step.reshape(-1), 's5_d': ds5d,
    }
    return dh_in, bufs, grads


HBM_SPEC = pl.BlockSpec(memory_space=pltpu.HBM)


def _place():
    return lax.axis_index("x"), lax.axis_index("y"), lax.axis_index("c")


GATHER_COLLECTIVE_ID = 1


def _all_gather(shards, *, name, on_sequencer=False):
    n = len(shards)

    def body(*refs):
        x_refs, out_refs = refs[:n], refs[n:2 * n]
        send_sems, recv_sems, local_sems = refs[2 * n:]
        x, y, c = _place()
        me, sibling = (x, y, c), (x, y, 1 - c)
        chips = [(1 - x, y), (x, 1 - y), (1 - x, 1 - y)]
        if on_sequencer:
            _handshake([sibling] + [(*chip, c) for chip in chips])

        def slot(o, px, py, pc):
            return out_refs[o].at[4 * px + 2 * py + pc]

        def copy(o, k, block, to, src=None):
            return pltpu.make_async_remote_copy(
                src_ref=slot(o, *block) if src is None else src, dst_ref=slot(o, *block),
                send_sem=send_sems.at[7 * o + k], recv_sem=recv_sems.at[7 * o + k], device_id=to, device_id_type=MESH)

        mine = [pltpu.make_async_copy(x_refs[o], slot(o, *me), local_sems.at[o]) for o in range(n)]
        for cp in mine:
            cp.start()
        first = []
        for j, chip in enumerate(chips):
            first += [copy(o, 1 + j, me, (*chip, c), src=x_refs[o]) for o in range(n)]
        first += [copy(o, 0, me, sibling, src=x_refs[o]) for o in range(n)]
        for cp in first:
            cp.start()
        passed = []
        for j, chip in enumerate(chips):
            for o in range(n):
                copy(o, 1 + j, (*chip, c), me).wait_recv()
                passed.append(copy(o, 4 + j, (*chip, c), sibling))
                passed[-1].start()
        for o in range(n):
            copy(o, 0, sibling, me).wait_recv()
        for j, chip in enumerate(chips):
            for o in range(n):
                copy(o, 4 + j, (*chip, 1 - c), me).wait_recv()
        for cp in first + passed:
            cp.wait_send()
        for cp in mine:
            cp.wait()

    out_shape = [jax.ShapeDtypeStruct((N_DEV,) + s.shape, s.dtype) for s in shards]
    sems = [pltpu.SemaphoreType.DMA((7 * n,)), pltpu.SemaphoreType.DMA((7 * n,)), pltpu.SemaphoreType.DMA((n,))]
    if on_sequencer:
        return _scall(body, name=name, out_type=out_shape, scratch_types=sems, collective_id=GATHER_COLLECTIVE_ID)(*shards)
    return _pcall(body, name=name, in_specs=[HBM_SPEC] * n, out_specs=[HBM_SPEC] * n, out_shape=out_shape,
                  scratch_shapes=sems)(*shards)


N_CHIPS = 4


SIBLING_COLLECTIVE_ID = 2
CHIPS_COLLECTIVE_ID = 3


def _handshake(peers):
    barrier = pltpu.get_barrier_semaphore()
    for peer in peers:
        pl.semaphore_signal(barrier, inc=1, device_id=peer, device_id_type=MESH)
    pl.semaphore_wait(barrier, len(peers))


def _exchange_sibling(grads, *, name):
    n = len(grads)

    def body(*refs):
        p_refs, q_refs = refs[:n], refs[n:2 * n]
        send_sems, recv_sems = refs[2 * n:]
        x, y, c = _place()
        _handshake([(x, y, 1 - c)])
        copies = [pltpu.make_async_remote_copy(
            src_ref=p_refs[o].at[k, 1 - c], dst_ref=q_refs[o].at[k], send_sem=send_sems.at[N_CHIPS * o + k],
            recv_sem=recv_sems.at[N_CHIPS * o + k], device_id=(x, y, 1 - c), device_id_type=MESH)
            for o in range(n) for k in range(N_CHIPS)]
        for cp in copies:
            cp.start()
        for cp in copies:
            cp.wait()

    return _scall(
        body, name=name, out_type=[jax.ShapeDtypeStruct((N_CHIPS,) + g.shape[2:], g.dtype) for g in grads],
        scratch_types=[pltpu.SemaphoreType.DMA((N_CHIPS * n,)), pltpu.SemaphoreType.DMA((N_CHIPS * n,))],
        collective_id=SIBLING_COLLECTIVE_ID,
    )(*grads)


def _pair_sum(own, got, *, name):
    _, _, r, l = own.shape
    tr = _tile(r, 512, 16)
    c = lax.axis_index("c").astype(jnp.int32).reshape(1)

    def body(c_ref, p_ref, q_ref, o_ref):
        o_ref[...] = (p_ref[...].astype(F32) + q_ref[...].astype(F32)).astype(o_ref.dtype)

    return _pcall(
        body, name=name,
        grid_spec=pltpu.PrefetchScalarGridSpec(
            num_scalar_prefetch=1, grid=(N_CHIPS, r // tr),
            in_specs=[pl.BlockSpec((None, None, tr, l), lambda k, i, cr: (k, cr[0], i, 0)),
                      pl.BlockSpec((None, tr, l), lambda k, i, cr: (k, i, 0))],
            out_specs=pl.BlockSpec((None, tr, l), lambda k, i, cr: (k, i, 0))),
        out_shape=jax.ShapeDtypeStruct((N_CHIPS, r, l), own.dtype),
        compiler_params=_params("parallel", "parallel"),
    )(c, own, got)


def _exchange_chips(parts, *, name):
    n = len(parts)

    def body(*refs):
        p_refs, g_refs = refs[:n], refs[n:2 * n]
        send_sems, recv_sems, local_sems = refs[2 * n:]
        x, y, c = _place()
        mine = 2 * x + y
        chips = [(1 - x, y), (x, 1 - y), (1 - x, 1 - y)]
        _handshake([(*chip, c) for chip in chips])
        own = [pltpu.make_async_copy(p_refs[o].at[mine], g_refs[o].at[mine], local_sems.at[o]) for o in range(n)]
        for cp in own:
            cp.start()
        copies = []
        for j, (px, py) in enumerate(chips):
            copies += [pltpu.make_async_remote_copy(
                src_ref=p_refs[o].at[2 * px + py], dst_ref=g_refs[o].at[mine], send_sem=send_sems.at[3 * o + j],
                recv_sem=recv_sems.at[3 * o + j], device_id=(px, py, c), device_id_type=MESH) for o in range(n)]
        for cp in copies:
            cp.start()
        for cp in copies:
            cp.wait()
        for cp in own:
            cp.wait()

    return _scall(
        body, name=name, out_type=[jax.ShapeDtypeStruct(p.shape, p.dtype) for p in parts],
        scratch_types=[pltpu.SemaphoreType.DMA((3 * n,)), pltpu.SemaphoreType.DMA((3 * n,)), pltpu.SemaphoreType.DMA((n,))],
        collective_id=CHIPS_COLLECTIVE_ID,
    )(*parts)


def _sum_slots(g, *, name):
    n, r, l = g.shape
    tr = _tile(r, 512, 16)

    def body(g_ref, o_ref):
        acc = g_ref[0].astype(F32)
        for k in range(1, n):
            acc = acc + g_ref[k].astype(F32)
        o_ref[...] = acc

    return _pcall(
        body, name=name, grid=(r // tr,), in_specs=[pl.BlockSpec((n, tr, l), lambda i: (0, i, 0))],
        out_specs=pl.BlockSpec((tr, l), lambda i: (i, 0)), out_shape=_sds((r, l)),
        compiler_params=_params("parallel"),
    )(g)


SUBLAYERS = (('ffn1', ['ffn1_w_gate', 'ffn1_w_up', 'ffn1_w_down']),
             ('mix', ['w_in', 'ssd_conv_w', 'w_branch_a', 's5_w_glu', 'w_branch_b', 'w_out']),
             ('ffn2', ['ffn2_w_gate', 'ffn2_w_up', 'ffn2_w_down']))


def _gather_weights(w):
    layers, first = [], None
    for i in range(DEPTH):
        g = {}
        for tag, names in SUBLAYERS:
            shards = [w[n][i:i + 1] if n == 'ssd_conv_w' else w[n][i:i + 1].astype(BF16) for n in names]
            if first is None:
                first = got = _all_gather(shards, name=f"gather_{tag}")
            else:
                shards, first = lax.optimization_barrier((shards, first))
                got = _all_gather(shards, name=f"gather_{tag}", on_sequencer=True)
            g.update(zip(names, got))
        layers.append(g)
    layers[0].update(zip(SUBLAYERS[0][1], first))
    return layers


class _ReduceScatter:
    @staticmethod
    def sibling(tag, bufs):
        names = list(bufs)
        own = [bufs[n].reshape((N_CHIPS, 2) + bufs[n].shape[1:]) for n in names]
        return (tag, names), (own, _exchange_sibling(own, name=f"reduce_sibling_{tag}"))

    @staticmethod
    def chips(meta, arrays):
        (tag, names), (own, got) = meta, arrays
        flat = lambda a, lead: a.reshape(lead + (-1, a.shape[-1]))
        parts = [_pair_sum(flat(o, (N_CHIPS, 2)), flat(g, (N_CHIPS,)), name=f"reduce_pair_sum_{n}").reshape(g.shape)
                 for n, o, g in zip(names, own, got)]
        return names, _exchange_chips(parts, name=f"reduce_chips_{tag}")

    @staticmethod
    def done(names, slots):
        return dict(zip(names, slots))

    @staticmethod
    def small(grads):
        return _reduce_small(grads)


def _reduce_small(grads):
    flat = jnp.concatenate([grads[n].astype(F32).reshape(-1) for n in SMALL_ORDER])
    pad = (-flat.shape[0]) % (8 * LANES)
    flat = jnp.concatenate([flat, jnp.zeros((pad,), F32)]).reshape(-1, LANES)
    gathered = _all_gather([flat], name="gather_small_grads", on_sequencer=True)[0]
    total = _sum_slots(gathered, name="sum_small_grads").reshape(-1)
    out, o = {}, 0
    for n in SMALL_ORDER:
        out[n] = total[o:o + grads[n].size].reshape(grads[n].shape)
        o += grads[n].size
    return out


def _adamw(w, g, m, v, *, name, slots=False):
    shape = w.shape
    if slots:
        lyr, rows, lanes = shape
        w2, m2, v2 = w, m, v
        tr = _tile(rows, 256, 16)
        nrt = rows // tr
        grid = (lyr, nrt)
        spec = pl.BlockSpec((None, tr, lanes), lambda l, i: (l, i, 0))
        g_specs = [pl.BlockSpec((N_CHIPS, None, tr, lanes),
                                lambda l, i, k=k: (0, 0, jnp.where(l == k, i, jnp.where(l > k, nrt - 1, 0)), 0))
                   for k in range(lyr)]
        g_args = list(g)
        out_shape = [_sds(shape)] * 4
    else:
        lanes = shape[-1] if (shape[-1] >= 128 or w.size % LANES) else LANES
        as2d = lambda a: a.reshape(-1, lanes)
        w2, m2, v2 = as2d(w), as2d(m), as2d(v)
        r = w2.shape[0]
        tr = _tile(r, 256, 8)
        grid = (1, r // tr)
        spec = pl.BlockSpec((tr, lanes), lambda l, i: (i, 0))
        g_specs, g_args = [spec], [as2d(g)]
        out_shape = [_sds((r, lanes))] * 4
    n_g = len(g_args)

    def body(w_ref, *rest):
        g_refs = rest[:n_g]
        m_ref, v_ref, go_ref, d_ref, mo_ref, vo_ref = rest[n_g:]
        if slots:
            gg = None
            for k, g_ref in enumerate(g_refs):
                tot = g_ref[0].astype(F32)
                for c in range(1, N_CHIPS):
                    tot = tot + g_ref[c].astype(F32)
                gg = tot if gg is None else jnp.where(pl.program_id(0) == k, tot, gg)
        else:
            gg = g_refs[0][...]
        go_ref[...] = gg
        mn = ADAM_B1 * m_ref[...] + (1.0 - ADAM_B1) * gg
        vn = ADAM_B2 * v_ref[...] + (1.0 - ADAM_B2) * (gg * gg)
        m_hat = mn / (1.0 - ADAM_B1 ** ADAM_STEP)
        v_hat = vn / (1.0 - ADAM_B2 ** ADAM_STEP)
        d_ref[...] = -ADAM_LR * (m_hat / (jnp.sqrt(v_hat) + ADAM_EPS) + ADAM_WD * w_ref[...])
        mo_ref[...] = mn
        vo_ref[...] = vn

    res = _pcall(
        body, name=name, grid=grid, in_specs=[spec] + g_specs + [spec, spec], out_specs=[spec] * 4,
        out_shape=out_shape, compiler_params=_params("arbitrary", "arbitrary"),
    )(w2, *g_args, m2, v2)
    return tuple(a.reshape(shape) for a in res)


def _sublayer_params(w, g, i, k):
    row = lambda a: a.astype(F32).reshape(1, -1)
    if k != 'mix':
        return dict(layer=i, pre_g=row(w[f'{k}_pre_g'][i]), post_g=row(w[f'{k}_post_g'][i]),
                    w_gate=g[f'{k}_w_gate'], w_up=g[f'{k}_w_up'], w_down=g[f'{k}_w_down'])
    head8 = lambda a: jnp.broadcast_to(
        jnp.pad(a.astype(F32).reshape(SSD_GROUPS, 1, 8), ((0, 0), (0, 0), (0, PAD_HEADS - 8))), (SSD_GROUPS, 8, PAD_HEADS))
    by_rows = lambda n: g[n].reshape(-1, g[n].shape[-1])
    by_cols = lambda n: g[n][:, 0].transpose(1, 0, 2).reshape(g[n].shape[2], -1)
    w_in = by_cols('w_in')
    s = np.cumsum([SSD_D_INNER, SSD_CONV_DIM, SSD_HEADS, S5_WIDTH, D_MODEL])
    return dict(
        layer=i, pre_g=row(w['mix_pre_g'][i]), post_g=row(w['mix_post_g'][i]),
        w_big=jnp.concatenate([w_in[:, :s[1]], w_in[:, s[3]:]], axis=1), w_dt=w_in[:, s[1]:s[2]], w_u5=w_in[:, s[2]:s[3]],
        conv_w=by_cols('ssd_conv_w'), conv_b=row(w['ssd_conv_b'][i]),
        dt_bias8=head8(w['ssd_dt_bias'][i]), a_log8=head8(w['ssd_a_log'][i]), d8=head8(w['ssd_d'][i]),
        norm_g=row(w['ssd_norm_g'][i]), w_a=by_rows('w_branch_a'),
        lam_re=w['s5_lambda_re'][i], lam_im=w['s5_lambda_im'][i], log_step=w['s5_log_step'][i].reshape(S5_GROUPS, 1),
        b_re=w['s5_b_re'][i].reshape(S5_GROUPS, -1), b_im=w['s5_b_im'][i].reshape(S5_GROUPS, -1),
        c_re=w['s5_c_re'][i], c_im=w['s5_c_im'][i], s5_d=row(w['s5_d'][i]),
        w_glu=by_cols('s5_w_glu'), w_b=by_rows('w_branch_b'), w_out=by_rows('w_out'))


def _loss_head(h, target, *, tm):
    t, d = h.shape

    def fn(y, tgt):
        err = y - tgt
        return err * (1.0 / d), jnp.sum(0.5 * jnp.sum(err * err, axis=-1, keepdims=True) * (1.0 / d), axis=0, keepdims=True)

    dy, loss = _rows(fn, name="loss_head", nrow=t // tm,
                     ins=[(h, _rspec(tm, d)), (target, _rspec(tm, d))],
                     outs=[(_sds((t, d)), _rspec(tm, d), False), (_sds((1, 128)), _bspec(128), True)])
    return dy, loss[0, 0]


def _forward_backward(h, target, w, g, rs):
    t = h.shape[0]
    tm = _tile(t, 256, 8)
    layers, saved = [], []
    for i in range(DEPTH):
        gi, ps, ss = dict(g[i]), [], []
        for tag, names in SUBLAYERS:
            tied, h = lax.optimization_barrier(([gi[n] for n in names], h))
            gi.update(zip(names, tied))
            p = _sublayer_params(w, gi, i, tag)
            h, s = _mixer_fwd(h, p, tm) if tag == 'mix' else _ffn_fwd(h, p, tag, tm)
            ps.append(p)
            ss.append(s)
        layers.append(ps)
        saved.append(ss)
    dh, loss = _loss_head(h, target, tm=tm)
    reduced, small = [{} for _ in range(DEPTH)], [{} for _ in range(DEPTH)]
    in_sibling, in_chips = None, None

    def start_chips(x):
        nonlocal in_sibling, in_chips
        if in_sibling is not None:
            layer, meta, arrays = in_sibling
            arrays, x = lax.optimization_barrier((arrays, x))
            in_sibling, in_chips = None, (layer,) + tuple(rs.chips(meta, arrays))
        return x

    def finish_chips(x):
        nonlocal in_chips
        if in_chips is not None:
            layer, names, slots = in_chips
            slots, x = lax.optimization_barrier((slots, x))
            reduced[layer].update(rs.done(names, slots))
            in_chips = None
        return x

    for i in reversed(range(DEPTH)):
        for k in reversed(range(len(SUBLAYERS))):
            tag = SUBLAYERS[k][0]
            if tag == 'mix':
                dh, bufs, grads = _mixer_bwd(dh, layers[i][k], saved[i][k], tm, start_chips)
            else:
                dh, bufs, grads = _ffn_bwd(dh, layers[i][k], saved[i][k], tag, tm, start_chips)
            small[i].update(grads)
            dh = finish_chips(dh)
            in_sibling = (i,) + tuple(rs.sibling(tag, bufs))
            if tag == 'mix' and i + 1 < DEPTH:
                small[i + 1], dh = lax.optimization_barrier((small[i + 1], dh))
        small[i] = rs.small(small[i])
    dh = finish_chips(start_chips(dh))
    shapes = {n: (w[n].shape[:-1] + (SSD_CONV_DIM,) if n == 'ssd_conv_w' else w[n].shape) for n in SMALL_ORDER}
    stacked = {n: jnp.stack([small[i][n].reshape(shapes[n][1:]) for i in range(DEPTH)]) for n in SMALL_ORDER}
    return loss, dh, reduced, stacked


def kernel(*args):
    n_w = len(WEIGHTS)
    x, target = args[0], args[1 + n_w]
    w = dict(zip(WEIGHTS, args[1:1 + n_w]))
    m = dict(zip(WEIGHTS, args[2 + n_w:2 + 2 * n_w]))
    v = dict(zip(WEIGHTS, args[2 + 2 * n_w:2 + 3 * n_w]))
    t = x.shape[1]

    g = _gather_weights(w)
    loss_local, dx, slots, small = _forward_backward(x.reshape(t, D_MODEL), target.reshape(t, D_MODEL), w, g,
                                                     _ReduceScatter)
    loss = lax.psum(loss_local, ("x", "y", "c"))
    me = 4 * lax.axis_index("x") + 2 * lax.axis_index("y") + lax.axis_index("c")
    cols = w['ssd_conv_w'].shape[-1]
    small['ssd_conv_w'] = lax.dynamic_slice_in_dim(small['ssd_conv_w'], me * cols, cols, axis=2)

    grad, delta, new_m, new_v = {}, {}, {}, {}
    for n in WEIGHTS:
        sharded = n in slots[0]
        grad[n], delta[n], new_m[n], new_v[n] = _adamw(
            w[n], [slots[i][n] for i in range(DEPTH)] if sharded else small[n], m[n], v[n], name=f"adamw_{n}",
            slots=sharded)
    return (loss, dx.reshape(x.shape), *[grad[n] for n in WEIGHTS], *[delta[n] for n in WEIGHTS],
            *[new_m[n] for n in WEIGHTS], *[new_v[n] for n in WEIGHTS])
```

```python
import functools
import math

import numpy as np
import jax
import jax.numpy as jnp
from jax import lax
from jax.experimental import pallas as pl
from jax.experimental.pallas import tpu as pltpu
from jax.experimental.pallas import tpu_sc as plsc

F32 = jnp.float32
BF16 = jnp.bfloat16
MESH = pl.DeviceIdType.MESH
HIGHEST = lax.Precision.HIGHEST

D_MODEL = 1024
DEPTH = 2
FFN_HIDDEN = 2816
SSD_D_INNER = 2048
SSD_HEADS = 32
SSD_HEAD_DIM = 64
SSD_GROUPS = 4
SSD_STATE = 128
SSD_CHUNK = 128
SSD_CONV_DIM = 3072
SSD_CONV_WIDTH = 4
S5_WIDTH = 1024
S5_GROUP = 16
S5_GROUPS = 64
S5_STATE = 64
S5_MAX_REAL = -1e-4
S5_BLOCKS = 8
RMS_EPS = 1e-6
N_DEV = 8
LANES = 1024

ADAM_LR = 0.001
ADAM_B1 = 0.9
ADAM_B2 = 0.999
ADAM_EPS = 1e-08
ADAM_WD = 0.01
ADAM_STEP = 10

VMEM_LIMIT_BYTES = 48 * 1024 * 1024

WEIGHTS = ['ffn1_pre_g', 'ffn1_post_g', 'ffn1_w_gate', 'ffn1_w_up', 'ffn1_w_down', 'mix_pre_g', 'mix_post_g',
           'w_in', 'ssd_conv_w', 'ssd_conv_b', 'ssd_dt_bias', 'ssd_a_log', 'ssd_d', 'ssd_norm_g', 'w_branch_a',
           's5_lambda_re', 's5_lambda_im', 's5_b_re', 's5_b_im', 's5_c_re', 's5_c_im', 's5_log_step', 's5_d',
           's5_w_glu', 'w_branch_b', 'w_out', 'ffn2_pre_g', 'ffn2_post_g', 'ffn2_w_gate', 'ffn2_w_up',
           'ffn2_w_down']
SHARDED = {'ffn1_w_gate': 2, 'ffn1_w_up': 2, 'ffn1_w_down': 1, 'w_in': 2, 'ssd_conv_w': 2, 'w_branch_a': 1,
           's5_w_glu': 2, 'w_branch_b': 1, 'w_out': 1, 'ffn2_w_gate': 2, 'ffn2_w_up': 2, 'ffn2_w_down': 1}
SHARDED_ORDER = [n for n in WEIGHTS if n in SHARDED]
SMALL_ORDER = [n for n in WEIGHTS if n not in SHARDED or n == 'ssd_conv_w']


def _pcall(body, **kw):
    return pl.pallas_call(body, **kw)


def _scall(body, *, name, out_type, scratch_types, collective_id):
    return pl.kernel(body, out_type=out_type, mesh=plsc.ScalarSubcoreMesh(axis_name="sequencer", num_cores=1),
                     scratch_types=scratch_types, name=name,
                     compiler_params=pltpu.CompilerParams(collective_id=collective_id))


def _params(*sem):
    return pltpu.CompilerParams(dimension_semantics=sem, vmem_limit_bytes=VMEM_LIMIT_BYTES)


def _tile(n, pref, align=128):
    if n <= pref:
        return n
    t = (pref // align) * align
    while t >= align:
        if n % t == 0:
            return t
        t -= align
    return n


def _rms(x, g):
    return x * lax.rsqrt(jnp.mean(x * x, axis=-1, keepdims=True) + RMS_EPS) * g


def _sigmoid(x):
    return 1.0 / (1.0 + jnp.exp(-x))


def _silu(x):
    return x * _sigmoid(x)


def _gelu(x):
    return 0.5 * x * (1.0 + jnp.tanh(math.sqrt(2.0 / math.pi) * (x + 0.044715 * (x * x * x))))


def _softplus(x):
    return jnp.maximum(x, 0.0) + jnp.log(1.0 + jnp.exp(-jnp.abs(x)))


def _dot(a, b, dims):
    return lax.dot_general(a.astype(BF16), b.astype(BF16), (dims, ((), ())), preferred_element_type=F32)


_NN = ((1,), (0,))
_NT = ((1,), (1,))
_TN = ((0,), (0,))


@jax.custom_vjp
def _bdot_nn(a, b):
    return _dot(a, b, _NN)


_bdot_nn.defvjp(lambda a, b: (_dot(a, b, _NN), (a, b)),
                lambda r, g: (_dot(g, r[1], _NT), _dot(r[0], g, _TN)))


@jax.custom_vjp
def _bdot_nt(a, b):
    return _dot(a, b, _NT)


_bdot_nt.defvjp(lambda a, b: (_dot(a, b, _NT), (a, b)),
                lambda r, g: (_dot(g, r[1], _NN), _dot(g, r[0], _TN)))


@jax.custom_vjp
def _bdot_tn(a, b):
    return _dot(a, b, _TN)


_bdot_tn.defvjp(lambda a, b: (_dot(a, b, _TN), (a, b)),
                lambda r, g: (_dot(r[1], g, _NT), _dot(r[0], g, _NN)))


def _fdot(a, b, dims=_NN):
    return lax.dot_general(a, b, (dims, ((), ())), precision=HIGHEST, preferred_element_type=F32)


def _sel3(x, sel, dims, x_first):
    p1 = x.astype(BF16)
    r1 = x - p1.astype(F32)
    p2 = r1.astype(BF16)
    p3 = (r1 - p2.astype(F32)).astype(BF16)
    sel = sel.astype(BF16)
    out = None
    for piece in (p1, p2, p3):
        d = lax.dot_general(*((piece, sel) if x_first else (sel, piece)), (dims, ((), ())), preferred_element_type=F32)
        out = d if out is None else out + d
    return out


@jax.custom_vjp
def _sel_right(x, sel):
    return _sel3(x, sel, _NN, True)


_sel_right.defvjp(lambda x, sel: (_sel3(x, sel, _NN, True), sel),
                  lambda sel, g: (_sel3(g, sel, _NT, True), jnp.zeros_like(sel)))


@jax.custom_vjp
def _sel_left(sel, x):
    return _sel3(x, sel, _NN, False)


_sel_left.defvjp(lambda sel, x: (_sel3(x, sel, _NN, False), sel),
                 lambda sel, g: (jnp.zeros_like(sel), _sel3(g, sel, _TN, False)))


@jax.custom_vjp
def _sel_left_nt(sel, x):
    return _sel3(x, sel, _NT, False)


_sel_left_nt.defvjp(lambda sel, x: (_sel3(x, sel, _NT, False), sel),
                    lambda sel, g: (jnp.zeros_like(sel), _sel3(g, sel, _TN, True)))


def _mm(a, b, *, name, ta=False, tb=False, out_dtype=F32, tm=2048, tn=512, tk=2048, shards=None):
    m, k = (a.shape[1], a.shape[0]) if ta else a.shape
    n = b.shape[0] if tb else b.shape[1]
    assert k == (b.shape[1] if tb else b.shape[0]), (a.shape, b.shape, ta, tb)
    if shards == 'rows':
        tm = min(tm, m // N_DEV)
    if shards == 'cols':
        tn = n // N_DEV
    tm, tn, tk = _tile(m, tm), _tile(n, tn), _tile(k, tk)
    nk = k // tk
    a_spec = pl.BlockSpec((tk, tm), lambda i, j, kk: (kk, i)) if ta else pl.BlockSpec((tm, tk), lambda i, j, kk: (i, kk))
    b_spec = pl.BlockSpec((tn, tk), lambda i, j, kk: (j, kk)) if tb else pl.BlockSpec((tk, tn), lambda i, j, kk: (kk, j))
    dims = ((0 if ta else 1,), (1 if tb else 0,))
    out_spec = pl.BlockSpec((tm, tn), lambda i, j, kk: (i, j))
    out_shape = jax.ShapeDtypeStruct((m, n), out_dtype)
    if shards == 'rows':
        per = m // N_DEV // tm
        out_shape = jax.ShapeDtypeStruct((N_DEV, 1, m // N_DEV, n), out_dtype)
        out_spec = pl.BlockSpec((None, None, tm, tn), lambda i, j, kk: (i // per, 0, i % per, j))
    elif shards == 'cols':
        out_shape = jax.ShapeDtypeStruct((N_DEV, 1, m, n // N_DEV), out_dtype)
        out_spec = pl.BlockSpec((None, None, tm, tn), lambda i, j, kk: (j, 0, i, 0))

    def body(a_ref, b_ref, o_ref, acc_ref):
        kk = pl.program_id(2)

        @pl.when(kk == 0)
        def _():
            acc_ref[...] = jnp.zeros_like(acc_ref)

        acc_ref[...] += _dot(a_ref[...], b_ref[...], dims)

        @pl.when(kk == nk - 1)
        def _():
            o_ref[...] = acc_ref[...].astype(o_ref.dtype)

    return _pcall(
        body, name=name, grid=(m // tm, n // tn, nk),
        in_specs=[a_spec, b_spec], out_specs=out_spec, out_shape=out_shape,
        scratch_shapes=[pltpu.VMEM((tm, tn), F32)],
        compiler_params=_params("parallel", "parallel", "arbitrary"),
    )(a, b)


def _rspec(tm, w, cb=0, percol=False):
    return pl.BlockSpec((tm, w), (lambda j, i: (i, cb + j)) if percol else (lambda j, i: (i, cb)))


def _bspec(w, cb=0, percol=False, rows=1):
    return pl.BlockSpec((rows, w), (lambda j, i: (0, cb + j)) if percol else (lambda j, i: (0, cb)))


def _rows(fn, *, name, nrow, ncol=1, ins, outs):
    n_in = len(ins)
    accs = [o[2] for o in outs]

    def body(*refs):
        vals = fn(*[r[...] for r in refs[:n_in]])
        if not isinstance(vals, (tuple, list)):
            vals = (vals,)
        i = pl.program_id(1)
        for ref, val, acc in zip(refs[n_in:], vals, accs):
            if acc:
                @pl.when(i == 0)
                def _(ref=ref):
                    ref[...] = jnp.zeros_like(ref)

                ref[...] += jnp.broadcast_to(val, ref.shape).astype(ref.dtype)
            else:
                ref[...] = val.astype(ref.dtype)

    res = _pcall(
        body, name=name, grid=(ncol, nrow),
        in_specs=[s for _, s in ins], out_specs=[o[1] for o in outs], out_shape=[o[0] for o in outs],
        compiler_params=_params("parallel", "arbitrary"),
    )(*[a for a, _ in ins])
    return res


def _sds(shape, dtype=F32):
    return jax.ShapeDtypeStruct(shape, dtype)


def _rms_fwd(h, g, *, name, tm):
    t, d = h.shape
    return _rows(lambda x, gg: _rms(x, gg), name=name, nrow=t // tm,
                 ins=[(h, _rspec(tm, d)), (g, _bspec(d))],
                 outs=[(_sds((t, d), BF16), _rspec(tm, d), False)])[0]


def _resid_fwd(h, f, g, scale, *, name, tm):
    t, d = h.shape
    return _rows(lambda x, ff, gg: x + scale * _rms(ff, gg), name=name, nrow=t // tm,
                 ins=[(h, _rspec(tm, d)), (f, _rspec(tm, d)), (g, _bspec(d))],
                 outs=[(_sds((t, d)), _rspec(tm, d), False)])[0]


def _resid_bwd(f, g, dh, scale, *, name, tm):
    t, d = f.shape

    def fn(ff, gg, dd):
        _, vjp = jax.vjp(lambda a, b: scale * _rms(a, b), ff, gg)
        return vjp(dd)

    return _rows(fn, name=name, nrow=t // tm,
                 ins=[(f, _rspec(tm, d)), (g, _bspec(d)), (dh, _rspec(tm, d))],
                 outs=[(_sds((t, d), BF16), _rspec(tm, d), False), (_sds((1, d)), _bspec(d), True)])


def _rms_bwd(h, g, dh, dxns, *, name, tm):
    t, d = h.shape

    def fn(x, gg, dd, *dx):
        _, vjp = jax.vjp(_rms, x, gg)
        tot = dx[0]
        for more in dx[1:]:
            tot = tot + more
        dxx, dg = vjp(tot)
        return dd + dxx, dg

    return _rows(fn, name=name, nrow=t // tm,
                 ins=[(h, _rspec(tm, d)), (g, _bspec(d)), (dh, _rspec(tm, d))] + [(x, _rspec(tm, d)) for x in dxns],
                 outs=[(_sds((t, d)), _rspec(tm, d), False), (_sds((1, d)), _bspec(d), True)])


NB = FFN_HIDDEN // N_DEV
MM_ROWS = 2048


def _ffn_up(xn, wg, wu, *, name):
    t = xn.shape[0]
    tm = _tile(t, MM_ROWS)
    wspec = pl.BlockSpec((None, None, D_MODEL, NB), lambda i, j: (j, 0, 0, 0))

    def body(x_ref, g_ref, u_ref, ab_ref, hh_ref):
        x = x_ref[...]
        a, b = _dot(x, g_ref[...], _NN), _dot(x, u_ref[...], _NN)
        ab_ref[0] = a
        ab_ref[1] = b
        hh_ref[...] = (_silu(a) * b).astype(hh_ref.dtype)

    return _pcall(
        body, name=name, grid=(t // tm, N_DEV),
        in_specs=[pl.BlockSpec((tm, D_MODEL), lambda i, j: (i, 0)), wspec, wspec],
        out_specs=[pl.BlockSpec((None, 2, tm, NB), lambda i, j: (j, 0, i, 0)),
                   pl.BlockSpec((None, tm, NB), lambda i, j: (j, i, 0))],
        out_shape=[_sds((N_DEV, 2, t, NB)), _sds((N_DEV, t, NB), BF16)],
        compiler_params=_params("parallel", "parallel"),
    )(xn, wg, wu)


def _ffn_down(hh, wd, *, name):
    t = hh.shape[1]
    tm = _tile(t, 512)

    def body(h_ref, w_ref, o_ref):
        acc = _dot(h_ref[0], w_ref[0, 0], _NN)
        for k in range(1, N_DEV):
            acc = acc + _dot(h_ref[k], w_ref[k, 0], _NN)
        o_ref[...] = acc

    return _pcall(
        body, name=name, grid=(t // tm,),
        in_specs=[pl.BlockSpec((N_DEV, tm, NB), lambda i: (0, i, 0)),
                  pl.BlockSpec((N_DEV, 1, NB, D_MODEL), lambda i: (0, 0, 0, 0))],
        out_specs=pl.BlockSpec((tm, D_MODEL), lambda i: (i, 0)), out_shape=_sds((t, D_MODEL)),
        compiler_params=_params("parallel"),
    )(hh, wd)


def _ffn_down_dx(df, wd, ab, *, name):
    t = df.shape[0]
    tm = _tile(t, MM_ROWS)

    def body(d_ref, w_ref, ab_ref, o_ref):
        dhh = _dot(d_ref[...], w_ref[...], _NT)
        _, vjp = jax.vjp(lambda a, b: _silu(a) * b, ab_ref[0], ab_ref[1])
        da, db = vjp(dhh)
        o_ref[0] = da.astype(o_ref.dtype)
        o_ref[1] = db.astype(o_ref.dtype)

    blk = pl.BlockSpec((None, 2, tm, NB), lambda i, j: (j, 0, i, 0))
    return _pcall(
        body, name=name, grid=(t // tm, N_DEV),
        in_specs=[pl.BlockSpec((tm, D_MODEL), lambda i, j: (i, 0)),
                  pl.BlockSpec((None, None, NB, D_MODEL), lambda i, j: (j, 0, 0, 0)), blk],
        out_specs=blk, out_shape=_sds((N_DEV, 2, t, NB), BF16), compiler_params=_params("parallel", "parallel"),
    )(df, wd, ab)


def _ffn_down_dw(hh, df, *, name, tn=512):
    t = df.shape[0]
    tk = _tile(t, 2048)
    nk = t // tk

    def body(h_ref, d_ref, o_ref, acc_ref):
        kk = pl.program_id(2)

        @pl.when(kk == 0)
        def _():
            acc_ref[...] = jnp.zeros_like(acc_ref)

        acc_ref[...] += _dot(h_ref[...], d_ref[...], _TN)

        @pl.when(kk == nk - 1)
        def _():
            o_ref[...] = acc_ref[...].astype(o_ref.dtype)

    return _pcall(
        body, name=name, grid=(N_DEV, D_MODEL // tn, nk),
        in_specs=[pl.BlockSpec((None, tk, NB), lambda j, n, kk: (j, kk, 0)),
                  pl.BlockSpec((tk, tn), lambda j, n, kk: (kk, n))],
        out_specs=pl.BlockSpec((None, None, NB, tn), lambda j, n, kk: (j, 0, 0, n)),
        out_shape=_sds((N_DEV, 1, NB, D_MODEL), BF16),
        scratch_shapes=[pltpu.VMEM((NB, tn), F32)],
        compiler_params=_params("parallel", "parallel", "arbitrary"),
    )(hh, df)


def _ffn_up_dx(dab, wg, wu, *, name):
    t = dab.shape[2]
    tm = _tile(t, MM_ROWS // 2)
    wspec = pl.BlockSpec((None, None, D_MODEL, NB), lambda i, j: (j, 0, 0, 0))

    def body(d_ref, g_ref, u_ref, o_ref):
        @pl.when(pl.program_id(1) == 0)
        def _():
            o_ref[...] = jnp.zeros_like(o_ref)

        o_ref[...] += _dot(d_ref[0], g_ref[...], _NT) + _dot(d_ref[1], u_ref[...], _NT)

    return _pcall(
        body, name=name, grid=(t // tm, N_DEV),
        in_specs=[pl.BlockSpec((None, 2, tm, NB), lambda i, j: (j, 0, i, 0)), wspec, wspec],
        out_specs=pl.BlockSpec((tm, D_MODEL), lambda i, j: (i, 0)), out_shape=_sds((t, D_MODEL)),
        compiler_params=_params("parallel", "arbitrary"),
    )(dab, wg, wu)


def _ffn_up_dw(xn, dab, *, name):
    t = xn.shape[0]

    def body(x_ref, d_ref, og_ref, ou_ref, xt_ref):
        @pl.when(pl.program_id(0) == 0)
        def _():
            xt_ref[...] = x_ref[...].T

        xt = xt_ref[...]
        og_ref[...] = _dot(xt, d_ref[0], _NN).astype(og_ref.dtype)
        ou_ref[...] = _dot(xt, d_ref[1], _NN).astype(ou_ref.dtype)

    out = pl.BlockSpec((None, None, D_MODEL, NB), lambda j: (j, 0, 0, 0))
    return _pcall(
        body, name=name, grid=(N_DEV,),
        in_specs=[pl.BlockSpec((t, D_MODEL), lambda j: (0, 0)), pl.BlockSpec((None, 2, t, NB), lambda j: (j, 0, 0, 0))],
        out_specs=[out, out], out_shape=[_sds((N_DEV, 1, D_MODEL, NB), BF16)] * 2,
        scratch_shapes=[pltpu.VMEM((D_MODEL, t), BF16)], compiler_params=_params("arbitrary"),
    )(xn, dab)


def _ffn_fwd(h, p, tag, tm):
    xn = _rms_fwd(h, p['pre_g'], name=f"{tag}_rms", tm=tm)
    ab, hh = _ffn_up(xn, p['w_gate'], p['w_up'], name=f"{tag}_up")
    f = _ffn_down(hh, p['w_down'], name=f"{tag}_down")
    out = _resid_fwd(h, f, p['post_g'], 0.5, name=f"{tag}_res", tm=tm)
    return out, (h, xn, ab, hh, f)


def _ffn_bwd(dh, p, saved, tag, tm, after_first):
    h, xn, ab, hh, f = saved
    df, dpost = _resid_bwd(f, p['post_g'], dh, 0.5, name=f"{tag}_res_bwd", tm=tm)
    df = after_first(df)
    dab = _ffn_down_dx(df, p['w_down'], ab, name=f"{tag}_down_dx")
    bufs = {f'{tag}_w_down': _ffn_down_dw(hh, df, name=f"{tag}_down_dw")}
    dxn = _ffn_up_dx(dab, p['w_gate'], p['w_up'], name=f"{tag}_up_dx")
    bufs[f'{tag}_w_gate'], bufs[f'{tag}_w_up'] = _ffn_up_dw(xn, dab, name=f"{tag}_up_dw")
    dh_in, dpre = _rms_bwd(h, p['pre_g'], dh, [dxn], name=f"{tag}_rms_bwd", tm=tm)
    return dh_in, bufs, {f'{tag}_pre_g': dpre, f'{tag}_post_g': dpost}


CONV_COLS = 256


def _shift_down(x, s):
    rows = lax.broadcasted_iota(jnp.int32, x.shape, 0)
    return jnp.where(rows >= s, pltpu.roll(x, s, axis=0), 0.0)


def _shift_up(x, s):
    t = x.shape[0]
    rows = lax.broadcasted_iota(jnp.int32, x.shape, 0)
    return jnp.where(rows < t - s, pltpu.roll(x, t - s, axis=0), 0.0)


def _conv_fwd(proj, col0, w, b, *, name):
    t = proj.shape[0]
    c = w.shape[1]
    cb0 = col0 // CONV_COLS

    def body(x_ref, w_ref, b_ref, o_ref):
        x = x_ref[...]
        acc = x * w_ref[3:4, :] + b_ref[...]
        for k in range(SSD_CONV_WIDTH - 1):
            acc = acc + _shift_down(x, SSD_CONV_WIDTH - 1 - k) * w_ref[k:k + 1, :]
        o_ref[...] = _silu(acc)

    return _pcall(
        body, name=name, grid=(c // CONV_COLS,),
        in_specs=[pl.BlockSpec((t, CONV_COLS), lambda j: (0, cb0 + j)),
                  pl.BlockSpec((SSD_CONV_WIDTH, CONV_COLS), lambda j: (0, j)),
                  pl.BlockSpec((1, CONV_COLS), lambda j: (0, j))],
        out_specs=pl.BlockSpec((t, CONV_COLS), lambda j: (0, j)),
        out_shape=_sds((t, c)), compiler_params=_params("parallel"),
    )(proj, w, b)


def _conv_bwd(proj, col0, w, b, dout, *, name):
    t = proj.shape[0]
    c = w.shape[1]
    cb0 = col0 // CONV_COLS

    def body(x_ref, w_ref, b_ref, d_ref, dx_ref, dw_ref, db_ref):
        x = x_ref[...]
        shifted = [_shift_down(x, SSD_CONV_WIDTH - 1 - k) for k in range(SSD_CONV_WIDTH - 1)] + [x]
        pre = b_ref[...] + shifted[3] * w_ref[3:4, :]
        for k in range(SSD_CONV_WIDTH - 1):
            pre = pre + shifted[k] * w_ref[k:k + 1, :]
        sg = _sigmoid(pre)
        dpre = d_ref[...] * (sg * (1.0 + pre * (1.0 - sg)))
        dx = dpre * w_ref[3:4, :]
        for k in range(SSD_CONV_WIDTH - 1):
            dx = dx + _shift_up(dpre, SSD_CONV_WIDTH - 1 - k) * w_ref[k:k + 1, :]
        dx_ref[...] = dx.astype(dx_ref.dtype)
        for k in range(SSD_CONV_WIDTH):
            dw_ref[k:k + 1, :] = jnp.sum(dpre * shifted[k], axis=0, keepdims=True)
        db_ref[...] = jnp.sum(dpre, axis=0, keepdims=True)

    return _pcall(
        body, name=name, grid=(c // CONV_COLS,),
        in_specs=[pl.BlockSpec((t, CONV_COLS), lambda j: (0, cb0 + j)),
                  pl.BlockSpec((SSD_CONV_WIDTH, CONV_COLS), lambda j: (0, j)),
                  pl.BlockSpec((1, CONV_COLS), lambda j: (0, j)),
                  pl.BlockSpec((t, CONV_COLS), lambda j: (0, j))],
        out_specs=[pl.BlockSpec((t, CONV_COLS), lambda j: (0, j)),
                   pl.BlockSpec((SSD_CONV_WIDTH, CONV_COLS), lambda j: (0, j)),
                   pl.BlockSpec((1, CONV_COLS), lambda j: (0, j))],
        out_shape=[_sds((t, c), BF16), _sds((SSD_CONV_WIDTH, c)), _sds((1, c))],
        compiler_params=_params("parallel"),
    )(proj, w, b, dout)


HALF = 256
HEADS_PER_HALF = 4
PAD_HEADS = 128


def _head_expanders():
    k = lax.broadcasted_iota(jnp.int32, (PAD_HEADS, HALF), 0)
    j = lax.broadcasted_iota(jnp.int32, (PAD_HEADS, HALF), 1)
    kt = lax.broadcasted_iota(jnp.int32, (HALF, PAD_HEADS), 1)
    jt = lax.broadcasted_iota(jnp.int32, (HALF, PAD_HEADS), 0)
    es, ets = [], []
    for half in range(2):
        es.append(jnp.where(k == j // SSD_HEAD_DIM + half * HEADS_PER_HALF, 1.0, 0.0).astype(F32))
        ets.append(jnp.where(kt == jt // SSD_HEAD_DIM + half * HEADS_PER_HALF, 1.0, 0.0).astype(F32))
    return es, ets


def _ssd_chunk(x_lo, x_hi, bm, cm, dtr, dtb8, alog8, dsk8, s_lo, s_hi):
    q = x_lo.shape[0]
    es, ets = _head_expanders()
    rowmean = lambda v: jnp.sum(v, axis=0, keepdims=True) * 0.125
    dt = _softplus(dtr + rowmean(dtb8))
    a = -jnp.exp(rowmean(alog8))
    adt = a * dt
    adt_tot8 = jnp.broadcast_to(jnp.sum(adt, axis=0, keepdims=True), (8, PAD_HEADS))
    ll = lax.broadcasted_iota(jnp.int32, (q, q), 0)
    ss = lax.broadcasted_iota(jnp.int32, (q, q), 1)
    ltri = jnp.where(ll >= ss, 1.0, 0.0).astype(F32)
    lane = lax.broadcasted_iota(jnp.int32, (1, HALF), 1)
    cb = _bdot_nt(cm, bm)
    outs = []
    for half, (x, s_in) in enumerate(((x_lo, s_lo), (x_hi, s_hi))):
        e, et = es[half], ets[half]
        dtf = _sel_right(dt, e)
        af = _sel_right(adt, e)
        dskf = rowmean(_sel_right(dsk8, e))
        acum = _sel_left(ltri, af)
        alast = jnp.sum(af, axis=0, keepdims=True)
        xdt = x * dtf
        ydiag = jnp.zeros((q, HALF), F32)
        for r in range(HEADS_PER_HALF):
            sel = lane == r * SSD_HEAD_DIM
            ac_r = jnp.sum(jnp.where(sel, acum, 0.0), axis=1, keepdims=True)
            a_r = jnp.sum(jnp.where(sel, af, 0.0), axis=1, keepdims=True)
            arow = jnp.sum(jnp.where(ll <= ss, a_r, 0.0), axis=0, keepdims=True)
            decay = jnp.exp(jnp.where(ll >= ss, ac_r - arow, -jnp.inf))
            yh = _bdot_nn(cb * decay, xdt)
            ydiag = ydiag + jnp.where(lane // SSD_HEAD_DIM == r, yh, 0.0)
        st = _bdot_tn(xdt * jnp.exp(alast - acum), bm)
        yoff = _bdot_nt(cm, s_in) * jnp.exp(acum)
        y = ydiag + yoff + dskf * x
        alast_col = jnp.sum(_sel_left_nt(et, adt_tot8), axis=1, keepdims=True) * 0.125
        outs.append((y, jnp.exp(alast_col) * s_in + st))
    return outs[0][0], outs[1][0], outs[0][1], outs[1][1]


def _ssd_specs(t, rev):
    q = SSD_CHUNK
    nc = t // q
    ci = (lambda c: nc - 1 - c) if rev else (lambda c: c)
    xcol0 = SSD_D_INNER // SSD_STATE
    return dict(
        x_lo=pl.BlockSpec((q, HALF), lambda g, c: (ci(c), 2 * g)),
        x_hi=pl.BlockSpec((q, HALF), lambda g, c: (ci(c), 2 * g + 1)),
        bm=pl.BlockSpec((q, SSD_STATE), lambda g, c: (ci(c), xcol0 + g)),
        cm=pl.BlockSpec((q, SSD_STATE), lambda g, c: (ci(c), xcol0 + SSD_GROUPS + g)),
        dt=pl.BlockSpec((None, q, PAD_HEADS), lambda g, c: (g, ci(c), 0)),
        par=pl.BlockSpec((None, 8, PAD_HEADS), lambda g, c: (g, 0, 0)),
        st=pl.BlockSpec((None, None, 2, HALF, SSD_STATE), lambda g, c: (ci(c), g, 0, 0, 0)),
        y=pl.BlockSpec((q, 2 * HALF), lambda g, c: (ci(c), g)),
        grp=pl.BlockSpec((q, SSD_STATE), lambda g, c: (ci(c), g)),
    )


def _ssd_fwd(xc, dt4, dtb, alog, dsk, *, name):
    t = xc.shape[0]
    nc = t // SSD_CHUNK
    sp = _ssd_specs(t, False)

    def body(xl, xh, bm, cm, dt, p0, p1, p2, y_ref, sin_ref, st_ref):
        @pl.when(pl.program_id(1) == 0)
        def _():
            st_ref[...] = jnp.zeros_like(st_ref)

        sin_ref[...] = st_ref[...]
        y_lo, y_hi, so_lo, so_hi = _ssd_chunk(xl[...], xh[...], bm[...], cm[...], dt[...], p0[...], p1[...],
                                              p2[...], st_ref[0], st_ref[1])
        y_ref[:, :HALF] = y_lo
        y_ref[:, HALF:] = y_hi
        st_ref[0] = so_lo
        st_ref[1] = so_hi

    return _pcall(
        body, name=name, grid=(SSD_GROUPS, nc),
        in_specs=[sp['x_lo'], sp['x_hi'], sp['bm'], sp['cm'], sp['dt'], sp['par'], sp['par'], sp['par']],
        out_specs=[sp['y'], sp['st']],
        out_shape=[_sds((t, SSD_D_INNER)), _sds((nc, SSD_GROUPS, 2, HALF, SSD_STATE))],
        scratch_shapes=[pltpu.VMEM((2, HALF, SSD_STATE), F32)],
        compiler_params=_params("parallel", "arbitrary"),
    )(xc, xc, xc, xc, dt4, dtb, alog, dsk)


def _ssd_bwd(xc, dt4, dtb, alog, dsk, sin, dy, *, name):
    t = xc.shape[0]
    nc = t // SSD_CHUNK
    sp = _ssd_specs(t, True)

    def body(xl, xh, bm, cm, dt, p0, p1, p2, sin_ref, dy_ref,
             dx_ref, db_ref, dc_ref, ddt_ref, dp0, dp1, dp2, dst_ref):
        first = pl.program_id(1) == 0

        @pl.when(first)
        def _():
            dst_ref[...] = jnp.zeros_like(dst_ref)

        _, vjp = jax.vjp(_ssd_chunk, xl[...], xh[...], bm[...], cm[...], dt[...], p0[...], p1[...], p2[...],
                         sin_ref[0], sin_ref[1])
        dxl, dxh, dbm, dcm, ddt, g0, g1, g2, ds_lo, ds_hi = vjp(
            (dy_ref[:, :HALF], dy_ref[:, HALF:], dst_ref[0], dst_ref[1]))
        dx_ref[:, :HALF] = dxl
        dx_ref[:, HALF:] = dxh
        db_ref[...] = dbm
        dc_ref[...] = dcm
        ddt_ref[...] = ddt
        dst_ref[0] = ds_lo
        dst_ref[1] = ds_hi
        for ref, g in ((dp0, g0), (dp1, g1), (dp2, g2)):
            tot = jnp.broadcast_to(jnp.sum(g, axis=0, keepdims=True), ref.shape)

            @pl.when(first)
            def _(ref=ref):
                ref[...] = jnp.zeros_like(ref)

            ref[...] += tot

    return _pcall(
        body, name=name, grid=(SSD_GROUPS, nc),
        in_specs=[sp['x_lo'], sp['x_hi'], sp['bm'], sp['cm'], sp['dt'], sp['par'], sp['par'], sp['par'],
                  sp['st'], sp['y']],
        out_specs=[sp['y'], sp['grp'], sp['grp'], sp['dt'], sp['par'], sp['par'], sp['par']],
        out_shape=[_sds((t, SSD_D_INNER)), _sds((t, SSD_GROUPS * SSD_STATE)), _sds((t, SSD_GROUPS * SSD_STATE)),
                   _sds((SSD_GROUPS, t, PAD_HEADS))] + [_sds((SSD_GROUPS, 8, PAD_HEADS))] * 3,
        scratch_shapes=[pltpu.VMEM((2, HALF, SSD_STATE), F32)],
        compiler_params=_params("parallel", "arbitrary"),
    )(xc, xc, xc, xc, dt4, dtb, alog, dsk, sin, dy)


def _gatenorm(y, z, g):
    v = y * _silu(z)
    return v * lax.rsqrt(jnp.mean(v * v, axis=-1, keepdims=True) + RMS_EPS) * g


S5_CH = S5_WIDTH // S5_BLOCKS
S5_ST = S5_CH * S5_STATE // S5_GROUP
SCAN_UNROLL = 8


def _cmul(ar, ai, br, bi):
    return ar * br - ai * bi, ar * bi + ai * br


def _segment_power(ar, ai, n):
    assert n & (n - 1) == 0
    for _ in range(n.bit_length() - 1):
        ar, ai = _cmul(ar, ai, ar, ai)
    return ar, ai


def _carry_in(fr, fi, pr, pi, reverse):
    rows = lax.broadcasted_iota(jnp.int32, fr.shape, 0)
    cr = jnp.zeros_like(fr[0:1])
    ci = jnp.zeros_like(cr)
    outr = jnp.zeros_like(fr)
    outi = jnp.zeros_like(fr)
    order = range(6, -1, -1) if reverse else range(1, 8)
    for j in order:
        src = j + 1 if reverse else j - 1
        nr, ni = _cmul(pr[0:1], pi[0:1], cr, ci)
        cr, ci = nr + fr[src:src + 1], ni + fi[src:src + 1]
        outr = jnp.where(rows == j, cr, outr)
        outi = jnp.where(rows == j, ci, outi)
    return outr, outi


def _s5_specs(t):
    return dict(ch=pl.BlockSpec((t, S5_CH), lambda j: (0, j)), st=pl.BlockSpec((t, S5_ST), lambda j: (0, j)),
                lam=pl.BlockSpec((1, S5_ST), lambda j: (0, j)),
                b=pl.BlockSpec((None, S5_CH, S5_ST), lambda j: (j, 0, 0)),
                c=pl.BlockSpec((None, S5_ST, S5_CH), lambda j: (j, 0, 0)))


def _s5_fwd(u5, bre, bim, cre, cim, lr, li, *, name):
    t = u5.shape[0]
    nrt = t // 8

    def body(u_ref, bre_ref, bim_ref, cre_ref, cim_ref, lr_ref, li_ref, sr_ref, si_ref, y_ref, br_ref, bi_ref):
        u = u_ref[...]
        br_ref[...] = _dot(u, bre_ref[...], _NN)
        bi_ref[...] = _dot(u, bim_ref[...], _NN)
        ar = jnp.broadcast_to(lr_ref[...], (8, S5_ST))
        ai = jnp.broadcast_to(li_ref[...], (8, S5_ST))

        def step(r, s, store):
            rows = pl.ds(pl.multiple_of(r * 8, 8), 8)
            nr, ni = _cmul(ar, ai, s[0], s[1])
            nr, ni = nr + br_ref[rows, :], ni + bi_ref[rows, :]
            if store:
                sr_ref[rows, :] = nr
                si_ref[rows, :] = ni
            return nr, ni

        zero = (jnp.zeros((8, S5_ST), F32), jnp.zeros((8, S5_ST), F32))
        fr, fi = lax.fori_loop(0, nrt, lambda r, s: step(r, s, False), zero, unroll=SCAN_UNROLL)
        pr, pi = _segment_power(ar, ai, nrt)
        init = _carry_in(fr, fi, pr, pi, False)
        lax.fori_loop(0, nrt, lambda r, s: step(r, s, True), init, unroll=SCAN_UNROLL)
        y_ref[...] = _dot(sr_ref[...], cre_ref[...], _NN) - _dot(si_ref[...], cim_ref[...], _NN)

    sp = _s5_specs(t)
    w = S5_BLOCKS * S5_ST
    return _pcall(
        body, name=name, grid=(S5_BLOCKS,),
        in_specs=[sp['ch'], sp['b'], sp['b'], sp['c'], sp['c'], sp['lam'], sp['lam']],
        out_specs=[sp['st'], sp['st'], sp['ch']], out_shape=[_sds((t, w)), _sds((t, w)), _sds((t, S5_WIDTH))],
        scratch_shapes=[pltpu.VMEM((t, S5_ST), F32)] * 2, compiler_params=_params("parallel"),
    )(u5, bre, bim, cre, cim, lr, li)


def _s5_bwd(dy, du_direct, u5, sr, si, bre, bim, cre, cim, lr, li, *, name):
    t = u5.shape[0]
    nrt = t // 8

    def body(dy_ref, dd_ref, u_ref, sr_ref, si_ref, bre_ref, bim_ref, cre_ref, cim_ref, lr_ref, li_ref,
             du_ref, dbre_ref, dbim_ref, dcre_ref, dcim_ref, dlr_ref, dli_ref, gr_ref, gi_ref):
        dyv = dy_ref[...]
        gr_ref[...] = _dot(dyv, cre_ref[...], _NT)
        gi_ref[...] = -_dot(dyv, cim_ref[...], _NT)
        dcre_ref[...] = _dot(sr_ref[...], dyv, _TN)
        dcim_ref[...] = -_dot(si_ref[...], dyv, _TN)
        dr_ref, di_ref = gr_ref, gi_ref
        ar = jnp.broadcast_to(lr_ref[...], (8, S5_ST))
        ai = -jnp.broadcast_to(li_ref[...], (8, S5_ST))
        zero = jnp.zeros((8, S5_ST), F32)

        def step1(k, g):
            rows = pl.ds(pl.multiple_of((nrt - 1 - k) * 8, 8), 8)
            nr, ni = _cmul(ar, ai, g[0], g[1])
            return nr + dr_ref[rows, :], ni + di_ref[rows, :]

        fr, fi = lax.fori_loop(0, nrt, step1, (zero, zero), unroll=SCAN_UNROLL)
        pr, pi = _segment_power(ar, ai, nrt)
        init = _carry_in(fr, fi, pr, pi, True)

        def step2(k, carry):
            gr, gi, accr, acci = carry
            r = nrt - 1 - k
            rows = pl.ds(pl.multiple_of(r * 8, 8), 8)
            prev = pl.ds(pl.multiple_of(jnp.maximum(r - 1, 0) * 8, 8), 8)
            nr, ni = _cmul(ar, ai, gr, gi)
            nr, ni = nr + dr_ref[rows, :], ni + di_ref[rows, :]
            gr_ref[rows, :] = nr
            gi_ref[rows, :] = ni
            keep = jnp.where(r > 0, 1.0, 0.0)
            pr_, pi_ = sr_ref[prev, :] * keep, si_ref[prev, :] * keep
            return nr, ni, accr + (pr_ * nr + pi_ * ni), acci + (pr_ * ni - pi_ * nr)

        _, _, accr, acci = lax.fori_loop(0, nrt, step2, (init[0], init[1], zero, zero), unroll=SCAN_UNROLL)
        last = pl.ds((nrt - 1) * 8, 8)
        pr_, pi_ = _shift_down(sr_ref[last, :], 1), _shift_down(si_ref[last, :], 1)
        g0r, g0i = gr_ref[0:8, :], gi_ref[0:8, :]
        accr = accr + (pr_ * g0r + pi_ * g0i)
        acci = acci + (pr_ * g0i - pi_ * g0r)
        dlr_ref[...] = jnp.sum(accr, axis=0, keepdims=True)
        dli_ref[...] = jnp.sum(acci, axis=0, keepdims=True)
        u = u_ref[...]
        dbre_ref[...] = _dot(u, gr_ref[...], _TN)
        dbim_ref[...] = _dot(u, gi_ref[...], _TN)
        du = dd_ref[...] + _dot(gr_ref[...], bre_ref[...], _NT) + _dot(gi_ref[...], bim_ref[...], _NT)
        du_ref[...] = du.astype(du_ref.dtype)

    sp = _s5_specs(t)
    w = S5_BLOCKS * S5_ST
    return _pcall(
        body, name=name, grid=(S5_BLOCKS,),
        in_specs=[sp['ch'], sp['ch'], sp['ch'], sp['st'], sp['st'], sp['b'], sp['b'], sp['c'], sp['c'], sp['lam'], sp['lam']],
        out_specs=[sp['ch'], sp['b'], sp['b'], sp['c'], sp['c'], sp['lam'], sp['lam']],
        out_shape=[_sds((t, S5_WIDTH), BF16), _sds((S5_BLOCKS, S5_CH, S5_ST)), _sds((S5_BLOCKS, S5_CH, S5_ST)),
                   _sds((S5_BLOCKS, S5_ST, S5_CH)), _sds((S5_BLOCKS, S5_ST, S5_CH)), _sds((1, w)), _sds((1, w))],
        scratch_shapes=[pltpu.VMEM((t, S5_ST), F32)] * 2, compiler_params=_params("parallel"),
    )(dy, du_direct, u5, sr, si, bre, bim, cre, cim, lr, li)


def _s5_expander():
    n = lax.broadcasted_iota(jnp.int32, (S5_STATE, S5_STATE * S5_GROUP), 0)
    j = lax.broadcasted_iota(jnp.int32, (S5_STATE, S5_STATE * S5_GROUP), 1)
    return jnp.where(n == j // S5_GROUP, 1.0, 0.0).astype(F32)


def _s5_discretise(lam_re, lam_im, log_step, b_re, b_im):
    lr = jnp.minimum(lam_re, S5_MAX_REAL)
    step = jnp.exp(log_step)
    mag = jnp.exp(lr * step)
    ang = lam_im * step
    lbr, lbi = mag * jnp.cos(ang), mag * jnp.sin(ang)
    p, q = lbr - 1.0, lbi
    den = lr * lr + lam_im * lam_im
    cr, ci = (p * lr + q * lam_im) / den, (q * lr - p * lam_im) / den
    e = _s5_expander()
    cre, cie = _fdot(cr, e), _fdot(ci, e)
    return lbr, lbi, cre * b_re - cie * b_im, cre * b_im + cie * b_re


def _s5_params_fwd(lam_re, lam_im, log_step, b_re, b_im, *, name):
    g, n, w = S5_GROUPS, S5_STATE, S5_STATE * S5_GROUP

    def body(a, b, c, d, e, o0, o1, o2, o3):
        for ref, val in zip((o0, o1, o2, o3), _s5_discretise(a[...], b[...], c[...], d[...], e[...])):
            ref[...] = val

    return _pcall(body, name=name, out_shape=[_sds((g, n)), _sds((g, n)), _sds((g, w)), _sds((g, w))])(
        lam_re, lam_im, log_step, b_re, b_im)


def _s5_params_bwd(lam_re, lam_im, log_step, b_re, b_im, cts, *, name):
    g, n, w = S5_GROUPS, S5_STATE, S5_STATE * S5_GROUP

    def body(a, b, c, d, e, c0, c1, c2, c3, o0, o1, o2, o3, o4):
        _, vjp = jax.vjp(_s5_discretise, a[...], b[...], c[...], d[...], e[...])
        for ref, val in zip((o0, o1, o2, o3, o4), vjp((c0[...], c1[...], c2[...], c3[...]))):
            ref[...] = val

    return _pcall(body, name=name,
                  out_shape=[_sds((g, n)), _sds((g, n)), _sds((g, 1)), _sds((g, w)), _sds((g, w))])(
        lam_re, lam_im, log_step, b_re, b_im, *cts)


def _perm(a):
    t, c = a.shape
    return a.reshape(8, t // 8, c).transpose(1, 0, 2).reshape(t, c)


def _unperm(a):
    t, c = a.shape
    return a.reshape(t // 8, 8, c).transpose(1, 0, 2).reshape(t, c)


def _blockdiag(m, rows_inner, cols_inner):
    m = m.reshape(S5_BLOCKS, 8, rows_inner, cols_inner)
    eye = jnp.eye(8, dtype=m.dtype)
    out = m[:, :, :, None, :] * eye[None, :, None, :, None]
    return out.reshape(S5_BLOCKS, 8 * rows_inner, 8 * cols_inner)


def _blockdiag_extract(m, rows_inner, cols_inner):
    m = m.reshape(S5_BLOCKS, 8, rows_inner, 8, cols_inner)
    d = jnp.diagonal(m, axis1=1, axis2=3)
    return d.transpose(0, 3, 1, 2).reshape(S5_GROUPS, rows_inner, cols_inner)


Z0, XBC0, GA0, GB0 = 0, SSD_D_INNER, SSD_D_INNER + SSD_CONV_DIM, SSD_D_INNER + SSD_CONV_DIM + D_MODEL
BIG = GB0 + D_MODEL


def _mixer_fwd(h, p, tm):
    t = h.shape[0]
    nrow = t // tm
    u = _rms_fwd(h, p['pre_g'], name="mix_rms", tm=tm)
    u_p = _perm(u)
    proj = _mm(u, p['w_big'], name="mix_in")
    dtr = _mm(u, p['w_dt'], name="mix_in_dt")
    u5 = _mm(u_p, p['w_u5'], name="mix_in_s5")
    xc = _conv_fwd(proj, XBC0, p['conv_w'], p['conv_b'], name="ssd_conv")
    dt4 = jnp.pad(dtr.reshape(t, SSD_GROUPS, 8).transpose(1, 0, 2), ((0, 0), (0, 0), (0, PAD_HEADS - 8)))
    y_ssd, s_in = _ssd_fwd(xc, dt4, p['dt_bias8'], p['a_log8'], p['d8'], name="ssd_scan")
    gw = SSD_D_INNER // SSD_GROUPS
    ya = _rows(_gatenorm, name="ssd_gate", nrow=nrow, ncol=SSD_GROUPS,
               ins=[(y_ssd, _rspec(tm, gw, 0, True)), (proj, _rspec(tm, gw, Z0 // gw, True)),
                    (p['norm_g'], _bspec(gw, 0, True))],
               outs=[(_sds((t, SSD_D_INNER), BF16), _rspec(tm, gw, 0, True), False)])[0]
    y_a = _mm(ya, p['w_a'], name="mix_a")
    lbr, lbi, bbr, bbi = _s5_params_fwd(p['lam_re'], p['lam_im'], p['log_step'], p['b_re'], p['b_im'], name="s5_par")
    bd = lambda m: _blockdiag(m.reshape(S5_GROUPS, S5_STATE, S5_GROUP).transpose(0, 2, 1), S5_GROUP, S5_STATE).astype(BF16)
    bre, bim = bd(bbr), bd(bbi)
    cre = _blockdiag(p['c_re'].transpose(0, 2, 1), S5_STATE, S5_GROUP).astype(BF16)
    cim = _blockdiag(p['c_im'].transpose(0, 2, 1), S5_STATE, S5_GROUP).astype(BF16)
    lr, li = lbr.reshape(1, -1), lbi.reshape(1, -1)
    sr, si, y5 = _s5_fwd(u5, bre, bim, cre, cim, lr, li, name="s5_scan")
    y5g = _rows(lambda a, b, d: _gelu(a + d * b), name="s5_act", nrow=nrow,
                ins=[(y5, _rspec(tm, S5_WIDTH)), (u5, _rspec(tm, S5_WIDTH)), (p['s5_d'], _bspec(S5_WIDTH))],
                outs=[(_sds((t, S5_WIDTH), BF16), _rspec(tm, S5_WIDTH), False)])[0]
    vg = _mm(y5g, p['w_glu'], name="s5_glu")
    ybin = _rows(lambda a, b: a * _sigmoid(b), name="s5_glu_act", nrow=nrow,
                 ins=[(vg, _rspec(tm, S5_WIDTH, 0)), (vg, _rspec(tm, S5_WIDTH, 1))],
                 outs=[(_sds((t, S5_WIDTH), BF16), _rspec(tm, S5_WIDTH), False)])[0]
    y_b = _unperm(_mm(ybin, p['w_b'], name="mix_b"))
    merged = _rows(lambda ga, gb, a, b: _sigmoid(ga) * a + _sigmoid(gb) * b, name="mix_merge", nrow=nrow,
                   ins=[(proj, _rspec(tm, D_MODEL, GA0 // D_MODEL)), (proj, _rspec(tm, D_MODEL, GB0 // D_MODEL)),
                        (y_a, _rspec(tm, D_MODEL)), (y_b, _rspec(tm, D_MODEL))],
                   outs=[(_sds((t, D_MODEL), BF16), _rspec(tm, D_MODEL), False)])[0]
    m = _mm(merged, p['w_out'], name="mix_out")
    out = _resid_fwd(h, m, p['post_g'], 1.0, name="mix_res", tm=tm)
    saved = dict(h=h, u=u, u_p=u_p, proj=proj, u5=u5, xc=xc, dt4=dt4, s_in=s_in, y_ssd=y_ssd, ya=ya, y_a=y_a,
                 bre=bre, bim=bim, cre=cre, cim=cim, lr=lr, li=li, sr=sr, si=si, y5=y5, y5g=y5g, vg=vg, ybin=ybin,
                 y_b=y_b, merged=merged, m=m)
    return out, saved


def _mixer_bwd(dh, p, s, tm, after_first):
    t = dh.shape[0]
    nrow = t // tm
    proj = s['proj']
    bufs = {}

    def grad_mm(a, b, wname, axis, name):
        bufs[wname] = _mm(a, b, ta=True, name=name, out_dtype=BF16, shards=axis)

    dm, dpost = _resid_bwd(s['m'], p['post_g'], dh, 1.0, name="mix_res_bwd", tm=tm)
    dm = after_first(dm)
    dmerged = _mm(dm, p['w_out'], tb=True, name="mix_out_dx")
    grad_mm(s['merged'], dm, 'w_out', 'rows', "mix_out_dw")

    def merge_bwd(ga, gb, a, b, d):
        _, vjp = jax.vjp(lambda ga_, gb_, a_, b_: _sigmoid(ga_) * a_ + _sigmoid(gb_) * b_, ga, gb, a, b)
        dga, dgb, da, db = vjp(d)
        return jnp.concatenate([dga, dgb], axis=1), da, db

    dgab, dy_a, dy_b = _rows(
        merge_bwd, name="mix_merge_bwd", nrow=nrow,
        ins=[(proj, _rspec(tm, D_MODEL, GA0 // D_MODEL)), (proj, _rspec(tm, D_MODEL, GB0 // D_MODEL)),
             (s['y_a'], _rspec(tm, D_MODEL)), (s['y_b'], _rspec(tm, D_MODEL)), (dmerged, _rspec(tm, D_MODEL))],
        outs=[(_sds((t, 2 * D_MODEL), BF16), _rspec(tm, 2 * D_MODEL), False),
              (_sds((t, D_MODEL), BF16), _rspec(tm, D_MODEL), False),
              (_sds((t, D_MODEL), BF16), _rspec(tm, D_MODEL), False)])
    dya = _mm(dy_a, p['w_a'], tb=True, name="mix_a_dx")
    grad_mm(s['ya'], dy_a, 'w_branch_a', 'rows', "mix_a_dw")
    gw = SSD_D_INNER // SSD_GROUPS

    def gate_bwd(y, z, g, d):
        _, vjp = jax.vjp(_gatenorm, y, z, g)
        return vjp(d)

    dy_ssd, dz, dnorm = _rows(
        gate_bwd, name="ssd_gate_bwd", nrow=nrow, ncol=SSD_GROUPS,
        ins=[(s['y_ssd'], _rspec(tm, gw, 0, True)), (proj, _rspec(tm, gw, Z0 // gw, True)),
             (p['norm_g'], _bspec(gw, 0, True)), (dya, _rspec(tm, gw, 0, True))],
        outs=[(_sds((t, SSD_D_INNER)), _rspec(tm, gw, 0, True), False),
              (_sds((t, SSD_D_INNER), BF16), _rspec(tm, gw, 0, True), False),
              (_sds((1, SSD_D_INNER)), _bspec(gw, 0, True), True)])
    dxs, dbm, dcm, ddt4, ddtb, dalog, ddsk = _ssd_bwd(s['xc'], s['dt4'], p['dt_bias8'], p['a_log8'], p['d8'],
                                                      s['s_in'], dy_ssd, name="ssd_scan_bwd")
    dxc = jnp.concatenate([dxs, dbm, dcm], axis=1)
    dxbc, dconv_w, dconv_b = _conv_bwd(proj, XBC0, p['conv_w'], p['conv_b'], dxc, name="ssd_conv_bwd")
    ddtr = ddt4[:, :, :8].transpose(1, 0, 2).reshape(t, SSD_HEADS)
    dy_bp = _perm(dy_b)
    dybin = _mm(dy_bp, p['w_b'], tb=True, name="mix_b_dx")
    grad_mm(s['ybin'], dy_bp, 'w_branch_b', 'rows', "mix_b_dw")

    def glu_bwd(a, b, d):
        _, vjp = jax.vjp(lambda a_, b_: a_ * _sigmoid(b_), a, b)
        da, db = vjp(d)
        return jnp.concatenate([da, db], axis=1)

    dvg = _rows(glu_bwd, name="s5_glu_act_bwd", nrow=nrow,
                ins=[(s['vg'], _rspec(tm, S5_WIDTH, 0)), (s['vg'], _rspec(tm, S5_WIDTH, 1)), (dybin, _rspec(tm, S5_WIDTH))],
                outs=[(_sds((t, 2 * S5_WIDTH), BF16), _rspec(tm, 2 * S5_WIDTH), False)])[0]
    dy5g = _mm(dvg, p['w_glu'], tb=True, name="s5_glu_dx")
    grad_mm(s['y5g'], dvg, 's5_w_glu', 'cols', "s5_glu_dw")

    def act_bwd(a, b, d, g):
        _, vjp = jax.vjp(lambda a_, b_, d_: _gelu(a_ + d_ * b_), a, b, d)
        return vjp(g)

    dy5, du5_direct, ds5d = _rows(
        act_bwd, name="s5_act_bwd", nrow=nrow,
        ins=[(s['y5'], _rspec(tm, S5_WIDTH)), (s['u5'], _rspec(tm, S5_WIDTH)), (p['s5_d'], _bspec(S5_WIDTH)),
             (dy5g, _rspec(tm, S5_WIDTH))],
        outs=[(_sds((t, S5_WIDTH), BF16), _rspec(tm, S5_WIDTH), False), (_sds((t, S5_WIDTH)), _rspec(tm, S5_WIDTH), False),
              (_sds((1, S5_WIDTH)), _bspec(S5_WIDTH), True)])
    du5, dbre, dbim, dcre, dcim, dlr, dli = _s5_bwd(dy5, du5_direct, s['u5'], s['sr'], s['si'], s['bre'], s['bim'],
                                                     s['cre'], s['cim'], s['lr'], s['li'], name="s5_scan_bwd")
    du_p = _mm(du5, p['w_u5'], tb=True, name="mix_in_s5_dx")
    dw_u5 = _mm(s['u_p'], du5, ta=True, name="mix_in_s5_dw", out_dtype=BF16)
    ext_b = lambda m: _blockdiag_extract(m, S5_GROUP, S5_STATE).transpose(0, 2, 1).reshape(S5_GROUPS, S5_STATE * S5_GROUP)
    dlam_re, dlam_im, dlog_step, db_re, db_im = _s5_params_bwd(
        p['lam_re'], p['lam_im'], p['log_step'], p['b_re'], p['b_im'],
        (dlr.reshape(S5_GROUPS, S5_STATE), dli.reshape(S5_GROUPS, S5_STATE), ext_b(dbre), ext_b(dbim)), name="s5_par_bwd")
    dc_re = _blockdiag_extract(dcre, S5_STATE, S5_GROUP).transpose(0, 2, 1)
    dc_im = _blockdiag_extract(dcim, S5_STATE, S5_GROUP).transpose(0, 2, 1)
    dproj = jnp.concatenate([dz, dxbc, dgab], axis=1)
    du_big = _mm(dproj, p['w_big'], tb=True, name="mix_in_dx")
    du_dt = _mm(ddtr, p['w_dt'], tb=True, name="mix_in_dt_dx")
    dw_big = _mm(s['u'], dproj, ta=True, name="mix_in_dw", out_dtype=BF16)
    dw_dt = _mm(s['u'], ddtr, ta=True, name="mix_in_dt_dw", out_dtype=BF16)
    dh_in, dpre = _rms_bwd(s['h'], p['pre_g'], dh, [du_big, du_dt, _unperm(du_p)], name="mix_rms_bwd", tm=tm)
    dw_in = jnp.concatenate([dw_big[:, :GA0], dw_dt, dw_u5, dw_big[:, GA0:]], axis=1)
    bufs['w_in'] = dw_in.reshape(D_MODEL, N_DEV, -1).transpose(1, 0, 2)[:, None]
    grads = {
        'mix_pre_g': dpre, 'mix_post_g': dpost, 'ssd_conv_w': dconv_w, 'ssd_conv_b': dconv_b,
        'ssd_dt_bias': ddtb[:, 0, :8].reshape(-1), 'ssd_a_log': dalog[:, 0, :8].reshape(-1),
        'ssd_d': ddsk[:, 0, :8].reshape(-1), 'ssd_norm_g': dnorm,
        's5_lambda_re': dlam_re, 's5_lambda_im': dlam_im,
        's5_b_re': db_re.reshape(S5_GROUPS, S5_STATE, S5_GROUP), 's5_b_im': db_im.reshape(S5_GROUPS, S5_STATE, S5_GROUP),
        's5_c_re': dc_re, 's5_c_im': dc_im, 's5_log_step': dlog_step.reshape(-1), 's5_d': ds5d,
    }
    return dh_in, bufs, grads


HBM_SPEC = pl.BlockSpec(memory_space=pltpu.HBM)


def _place():
    return lax.axis_index("x"), lax.axis_index("y"), lax.axis_index("c")


GATHER_COLLECTIVE_ID = 1


def _all_gather(shards, *, name, on_sequencer=False):
    n = len(shards)

    def body(*refs):
        x_refs, out_refs = refs[:n], refs[n:2 * n]
        send_sems, recv_sems, local_sems = refs[2 * n:]
        x, y, c = _place()
        me, sibling = (x, y, c), (x, y, 1 - c)
        chips = [(1 - x, y), (x, 1 - y), (1 - x, 1 - y)]
        if on_sequencer:
            _handshake([sibling] + [(*chip, c) for chip in chips])

        def slot(o, px, py, pc):
            return out_refs[o].at[4 * px + 2 * py + pc]

        def copy(o, k, block, to, src=None):
            return pltpu.make_async_remote_copy(
                src_ref=slot(o, *block) if src is None else src, dst_ref=slot(o, *block),
                send_sem=send_sems.at[7 * o + k], recv_sem=recv_sems.at[7 * o + k], device_id=to, device_id_type=MESH)

        mine = [pltpu.make_async_copy(x_refs[o], slot(o, *me), local_sems.at[o]) for o in range(n)]
        for cp in mine:
            cp.start()
        first = []
        for j, chip in enumerate(chips):
            first += [copy(o, 1 + j, me, (*chip, c), src=x_refs[o]) for o in range(n)]
        first += [copy(o, 0, me, sibling, src=x_refs[o]) for o in range(n)]
        for cp in first:
            cp.start()
        passed = []
        for j, chip in enumerate(chips):
            for o in range(n):
                copy(o, 1 + j, (*chip, c), me).wait_recv()
                passed.append(copy(o, 4 + j, (*chip, c), sibling))
                passed[-1].start()
        for o in range(n):
            copy(o, 0, sibling, me).wait_recv()
        for j, chip in enumerate(chips):
            for o in range(n):
                copy(o, 4 + j, (*chip, 1 - c), me).wait_recv()
        for cp in first + passed:
            cp.wait_send()
        for cp in mine:
            cp.wait()

    out_shape = [jax.ShapeDtypeStruct((N_DEV,) + s.shape, s.dtype) for s in shards]
    sems = [pltpu.SemaphoreType.DMA((7 * n,)), pltpu.SemaphoreType.DMA((7 * n,)), pltpu.SemaphoreType.DMA((n,))]
    if on_sequencer:
        return _scall(body, name=name, out_type=out_shape, scratch_types=sems, collective_id=GATHER_COLLECTIVE_ID)(*shards)
    return _pcall(body, name=name, in_specs=[HBM_SPEC] * n, out_specs=[HBM_SPEC] * n, out_shape=out_shape,
                  scratch_shapes=sems)(*shards)


N_CHIPS = 4


SIBLING_COLLECTIVE_ID = 2
CHIPS_COLLECTIVE_ID = 3


def _handshake(peers):
    barrier = pltpu.get_barrier_semaphore()
    for peer in peers:
        pl.semaphore_signal(barrier, inc=1, device_id=peer, device_id_type=MESH)
    pl.semaphore_wait(barrier, len(peers))


def _exchange_sibling(grads, *, name):
    n = len(grads)

    def body(*refs):
        p_refs, q_refs = refs[:n], refs[n:2 * n]
        send_sems, recv_sems = refs[2 * n:]
        x, y, c = _place()
        _handshake([(x, y, 1 - c)])
        copies = [pltpu.make_async_remote_copy(
            src_ref=p_refs[o].at[k, 1 - c], dst_ref=q_refs[o].at[k], send_sem=send_sems.at[N_CHIPS * o + k],
            recv_sem=recv_sems.at[N_CHIPS * o + k], device_id=(x, y, 1 - c), device_id_type=MESH)
            for o in range(n) for k in range(N_CHIPS)]
        for cp in copies:
            cp.start()
        for cp in copies:
            cp.wait()

    return _scall(
        body, name=name, out_type=[jax.ShapeDtypeStruct((N_CHIPS,) + g.shape[2:], g.dtype) for g in grads],
        scratch_types=[pltpu.SemaphoreType.DMA((N_CHIPS * n,)), pltpu.SemaphoreType.DMA((N_CHIPS * n,))],
        collective_id=SIBLING_COLLECTIVE_ID,
    )(*grads)


def _pair_sum(own, got, *, name):
    _, _, r, l = own.shape
    tr = _tile(r, 512, 16)
    c = lax.axis_index("c").astype(jnp.int32).reshape(1)

    def body(c_ref, p_ref, q_ref, o_ref):
        o_ref[...] = (p_ref[...].astype(F32) + q_ref[...].astype(F32)).astype(o_ref.dtype)

    return _pcall(
        body, name=name,
        grid_spec=pltpu.PrefetchScalarGridSpec(
            num_scalar_prefetch=1, grid=(N_CHIPS, r // tr),
            in_specs=[pl.BlockSpec((None, None, tr, l), lambda k, i, cr: (k, cr[0], i, 0)),
                      pl.BlockSpec((None, tr, l), lambda k, i, cr: (k, i, 0))],
            out_specs=pl.BlockSpec((None, tr, l), lambda k, i, cr: (k, i, 0))),
        out_shape=jax.ShapeDtypeStruct((N_CHIPS, r, l), own.dtype),
        compiler_params=_params("parallel", "parallel"),
    )(c, own, got)


def _exchange_chips(parts, *, name):
    n = len(parts)

    def body(*refs):
        p_refs, g_refs = refs[:n], refs[n:2 * n]
        send_sems, recv_sems, local_sems = refs[2 * n:]
        x, y, c = _place()
        mine = 2 * x + y
        chips = [(1 - x, y), (x, 1 - y), (1 - x, 1 - y)]
        _handshake([(*chip, c) for chip in chips])
        own = [pltpu.make_async_copy(p_refs[o].at[mine], g_refs[o].at[mine], local_sems.at[o]) for o in range(n)]
        for cp in own:
            cp.start()
        copies = []
        for j, (px, py) in enumerate(chips):
            copies += [pltpu.make_async_remote_copy(
                src_ref=p_refs[o].at[2 * px + py], dst_ref=g_refs[o].at[mine], send_sem=send_sems.at[3 * o + j],
                recv_sem=recv_sems.at[3 * o + j], device_id=(px, py, c), device_id_type=MESH) for o in range(n)]
        for cp in copies:
            cp.start()
        for cp in copies:
            cp.wait()
        for cp in own:
            cp.wait()

    return _scall(
        body, name=name, out_type=[jax.ShapeDtypeStruct(p.shape, p.dtype) for p in parts],
        scratch_types=[pltpu.SemaphoreType.DMA((3 * n,)), pltpu.SemaphoreType.DMA((3 * n,)), pltpu.SemaphoreType.DMA((n,))],
        collective_id=CHIPS_COLLECTIVE_ID,
    )(*parts)


def _sum_slots(g, *, name):
    n, r, l = g.shape
    tr = _tile(r, 512, 16)

    def body(g_ref, o_ref):
        acc = g_ref[0].astype(F32)
        for k in range(1, n):
            acc = acc + g_ref[k].astype(F32)
        o_ref[...] = acc

    return _pcall(
        body, name=name, grid=(r // tr,), in_specs=[pl.BlockSpec((n, tr, l), lambda i: (0, i, 0))],
        out_specs=pl.BlockSpec((tr, l), lambda i: (i, 0)), out_shape=_sds((r, l)),
        compiler_params=_params("parallel"),
    )(g)


SUBLAYERS = (('ffn1', ['ffn1_w_gate', 'ffn1_w_up', 'ffn1_w_down']),
             ('mix', ['w_in', 'ssd_conv_w', 'w_branch_a', 's5_w_glu', 'w_branch_b', 'w_out']),
             ('ffn2', ['ffn2_w_gate', 'ffn2_w_up', 'ffn2_w_down']))


def _gather_weights(w):
    layers, first = [], None
    for i in range(DEPTH):
        g = {}
        for tag, names in SUBLAYERS:
            shards = [w[n][i:i + 1] if n == 'ssd_conv_w' else w[n][i:i + 1].astype(BF16) for n in names]
            if first is None:
                first = got = _all_gather(shards, name=f"gather_{tag}")
            else:
                shards, first = lax.optimization_barrier((shards, first))
                got = _all_gather(shards, name=f"gather_{tag}", on_sequencer=True)
            g.update(zip(names, got))
        layers.append(g)
    layers[0].update(zip(SUBLAYERS[0][1], first))
    return layers


class _ReduceScatter:
    @staticmethod
    def sibling(tag, bufs):
        names = list(bufs)
        own = [bufs[n].reshape((N_CHIPS, 2) + bufs[n].shape[1:]) for n in names]
        return (tag, names), (own, _exchange_sibling(own, name=f"reduce_sibling_{tag}"))

    @staticmethod
    def chips(meta, arrays):
        (tag, names), (own, got) = meta, arrays
        flat = lambda a, lead: a.reshape(lead + (-1, a.shape[-1]))
        parts = [_pair_sum(flat(o, (N_CHIPS, 2)), flat(g, (N_CHIPS,)), name=f"reduce_pair_sum_{n}").reshape(g.shape)
                 for n, o, g in zip(names, own, got)]
        return names, _exchange_chips(parts, name=f"reduce_chips_{tag}")

    @staticmethod
    def done(names, slots):
        return dict(zip(names, slots))

    @staticmethod
    def small(grads):
        return _reduce_small(grads)


def _reduce_small(grads):
    flat = jnp.concatenate([grads[n].astype(F32).reshape(-1) for n in SMALL_ORDER])
    pad = (-flat.shape[0]) % (8 * LANES)
    flat = jnp.concatenate([flat, jnp.zeros((pad,), F32)]).reshape(-1, LANES)
    gathered = _all_gather([flat], name="gather_small_grads", on_sequencer=True)[0]
    total = _sum_slots(gathered, name="sum_small_grads").reshape(-1)
    out, o = {}, 0
    for n in SMALL_ORDER:
        out[n] = total[o:o + grads[n].size].reshape(grads[n].shape)
        o += grads[n].size
    return out


def _adamw(w, g, m, v, *, name, slots=False):
    shape = w.shape
    if slots:
        lyr, rows, lanes = shape
        w2, m2, v2 = w, m, v
        tr = _tile(rows, 256, 16)
        nrt = rows // tr
        grid = (lyr, nrt)
        spec = pl.BlockSpec((None, tr, lanes), lambda l, i: (l, i, 0))
        g_specs = [pl.BlockSpec((N_CHIPS, None, tr, lanes),
                                lambda l, i, k=k: (0, 0, jnp.where(l == k, i, jnp.where(l > k, nrt - 1, 0)), 0))
                   for k in range(lyr)]
        g_args = list(g)
        out_shape = [_sds(shape)] * 4
    else:
        lanes = shape[-1] if (shape[-1] >= 128 or w.size % LANES) else LANES
        as2d = lambda a: a.reshape(-1, lanes)
        w2, m2, v2 = as2d(w), as2d(m), as2d(v)
        r = w2.shape[0]
        tr = _tile(r, 256, 8)
        grid = (1, r // tr)
        spec = pl.BlockSpec((tr, lanes), lambda l, i: (i, 0))
        g_specs, g_args = [spec], [as2d(g)]
        out_shape = [_sds((r, lanes))] * 4
    n_g = len(g_args)

    def body(w_ref, *rest):
        g_refs = rest[:n_g]
        m_ref, v_ref, go_ref, d_ref, mo_ref, vo_ref = rest[n_g:]
        if slots:
            gg = None
            for k, g_ref in enumerate(g_refs):
                tot = g_ref[0].astype(F32)
                for c in range(1, N_CHIPS):
                    tot = tot + g_ref[c].astype(F32)
                gg = tot if gg is None else jnp.where(pl.program_id(0) == k, tot, gg)
        else:
            gg = g_refs[0][...]
        go_ref[...] = gg
        mn = ADAM_B1 * m_ref[...] + (1.0 - ADAM_B1) * gg
        vn = ADAM_B2 * v_ref[...] + (1.0 - ADAM_B2) * (gg * gg)
        m_hat = mn / (1.0 - ADAM_B1 ** ADAM_STEP)
        v_hat = vn / (1.0 - ADAM_B2 ** ADAM_STEP)
        d_ref[...] = -ADAM_LR * (m_hat / (jnp.sqrt(v_hat) + ADAM_EPS) + ADAM_WD * w_ref[...])
        mo_ref[...] = mn
        vo_ref[...] = vn

    res = _pcall(
        body, name=name, grid=grid, in_specs=[spec] + g_specs + [spec, spec], out_specs=[spec] * 4,
        out_shape=out_shape, compiler_params=_params("arbitrary", "arbitrary"),
    )(w2, *g_args, m2, v2)
    return tuple(a.reshape(shape) for a in res)


def _sublayer_params(w, g, i, k):
    row = lambda a: a.astype(F32).reshape(1, -1)
    if k != 'mix':
        return dict(layer=i, pre_g=row(w[f'{k}_pre_g'][i]), post_g=row(w[f'{k}_post_g'][i]),
                    w_gate=g[f'{k}_w_gate'], w_up=g[f'{k}_w_up'], w_down=g[f'{k}_w_down'])
    head8 = lambda a: jnp.broadcast_to(
        jnp.pad(a.astype(F32).reshape(SSD_GROUPS, 1, 8), ((0, 0), (0, 0), (0, PAD_HEADS - 8))), (SSD_GROUPS, 8, PAD_HEADS))
    by_rows = lambda n: g[n].reshape(-1, g[n].shape[-1])
    by_cols = lambda n: g[n][:, 0].transpose(1, 0, 2).reshape(g[n].shape[2], -1)
    w_in = by_cols('w_in')
    s = np.cumsum([SSD_D_INNER, SSD_CONV_DIM, SSD_HEADS, S5_WIDTH, D_MODEL])
    return dict(
        layer=i, pre_g=row(w['mix_pre_g'][i]), post_g=row(w['mix_post_g'][i]),
        w_big=jnp.concatenate([w_in[:, :s[1]], w_in[:, s[3]:]], axis=1), w_dt=w_in[:, s[1]:s[2]], w_u5=w_in[:, s[2]:s[3]],
        conv_w=by_cols('ssd_conv_w'), conv_b=row(w['ssd_conv_b'][i]),
        dt_bias8=head8(w['ssd_dt_bias'][i]), a_log8=head8(w['ssd_a_log'][i]), d8=head8(w['ssd_d'][i]),
        norm_g=row(w['ssd_norm_g'][i]), w_a=by_rows('w_branch_a'),
        lam_re=w['s5_lambda_re'][i], lam_im=w['s5_lambda_im'][i], log_step=w['s5_log_step'][i].reshape(S5_GROUPS, 1),
        b_re=w['s5_b_re'][i].reshape(S5_GROUPS, -1), b_im=w['s5_b_im'][i].reshape(S5_GROUPS, -1),
        c_re=w['s5_c_re'][i], c_im=w['s5_c_im'][i], s5_d=row(w['s5_d'][i]),
        w_glu=by_cols('s5_w_glu'), w_b=by_rows('w_branch_b'), w_out=by_rows('w_out'))


def _loss_head(h, target, *, tm):
    t, d = h.shape

    def fn(y, tgt):
        err = y - tgt
        return err * (1.0 / d), jnp.sum(0.5 * jnp.sum(err * err, axis=-1, keepdims=True) * (1.0 / d), axis=0, keepdims=True)

    dy, loss = _rows(fn, name="loss_head", nrow=t // tm,
                     ins=[(h, _rspec(tm, d)), (target, _rspec(tm, d))],
                     outs=[(_sds((t, d)), _rspec(tm, d), False), (_sds((1, 128)), _bspec(128), True)])
    return dy, loss[0, 0]


def _forward_backward(h, target, w, g, rs):
    t = h.shape[0]
    tm = _tile(t, 256, 8)
    layers, saved = [], []
    for i in range(DEPTH):
        gi, ps, ss = dict(g[i]), [], []
        for tag, names in SUBLAYERS:
            tied, h = lax.optimization_barrier(([gi[n] for n in names], h))
            gi.update(zip(names, tied))
            p = _sublayer_params(w, gi, i, tag)
            h, s = _mixer_fwd(h, p, tm) if tag == 'mix' else _ffn_fwd(h, p, tag, tm)
            ps.append(p)
            ss.append(s)
        layers.append(ps)
        saved.append(ss)
    dh, loss = _loss_head(h, target, tm=tm)
    reduced, small = [{} for _ in range(DEPTH)], [{} for _ in range(DEPTH)]
    in_sibling, in_chips = None, None

    def start_chips(x):
        nonlocal in_sibling, in_chips
        if in_sibling is not None:
            layer, meta, arrays = in_sibling
            arrays, x = lax.optimization_barrier((arrays, x))
            in_sibling, in_chips = None, (layer,) + tuple(rs.chips(meta, arrays))
        return x

    def finish_chips(x):
        nonlocal in_chips
        if in_chips is not None:
            layer, names, slots = in_chips
            slots, x = lax.optimization_barrier((slots, x))
            reduced[layer].update(rs.done(names, slots))
            in_chips = None
        return x

    for i in reversed(range(DEPTH)):
        for k in reversed(range(len(SUBLAYERS))):
            tag = SUBLAYERS[k][0]
            if tag == 'mix':
                dh, bufs, grads = _mixer_bwd(dh, layers[i][k], saved[i][k], tm, start_chips)
            else:
                dh, bufs, grads = _ffn_bwd(dh, layers[i][k], saved[i][k], tag, tm, start_chips)
            small[i].update(grads)
            dh = finish_chips(dh)
            in_sibling = (i,) + tuple(rs.sibling(tag, bufs))
            if tag == 'mix' and i + 1 < DEPTH:
                small[i + 1], dh = lax.optimization_barrier((small[i + 1], dh))
        small[i] = rs.small(small[i])
    dh = finish_chips(start_chips(dh))
    shapes = {n: (w[n].shape[:-1] + (SSD_CONV_DIM,) if n == 'ssd_conv_w' else w[n].shape) for n in SMALL_ORDER}
    stacked = {n: jnp.stack([small[i][n].reshape(shapes[n][1:]) for i in range(DEPTH)]) for n in SMALL_ORDER}
    return loss, dh, reduced, stacked


def kernel(*args):
    n_w = len(WEIGHTS)
    x, target = args[0], args[1 + n_w]
    w = dict(zip(WEIGHTS, args[1:1 + n_w]))
    m = dict(zip(WEIGHTS, args[2 + n_w:2 + 2 * n_w]))
    v = dict(zip(WEIGHTS, args[2 + 2 * n_w:2 + 3 * n_w]))
    t = x.shape[1]

    g = _gather_weights(w)
    loss_local, dx, slots, small = _forward_backward(x.reshape(t, D_MODEL), target.reshape(t, D_MODEL), w, g,
                                                     _ReduceScatter)
    loss = lax.psum(loss_local, ("x", "y", "c"))
    me = 4 * lax.axis_index("x") + 2 * lax.axis_index("y") + lax.axis_index("c")
    cols = w['ssd_conv_w'].shape[-1]
    small['ssd_conv_w'] = lax.dynamic_slice_in_dim(small['ssd_conv_w'], me * cols, cols, axis=2)

    grad, delta, new_m, new_v = {}, {}, {}, {}
    for n in WEIGHTS:
        sharded = n in slots[0]
        grad[n], delta[n], new_m[n], new_v[n] = _adamw(
            w[n], [slots[i][n] for i in range(DEPTH)] if sharded else small[n], m[n], v[n], name=f"adamw_{n}",
            slots=sharded)
    return (loss, dx.reshape(x.shape), *[grad[n] for n in WEIGHTS], *[delta[n] for n in WEIGHTS],
            *[new_m[n] for n in WEIGHTS], *[new_v[n] for n in WEIGHTS])
```

```python
import functools
import math

import numpy as np
import jax
import jax.numpy as jnp
from jax import lax
from jax.experimental import pallas as pl
from jax.experimental.pallas import tpu as pltpu
from jax.experimental.pallas import tpu_sc as plsc

F32 = jnp.float32
BF16 = jnp.bfloat16
MESH = pl.DeviceIdType.MESH
HIGHEST = lax.Precision.HIGHEST

D_MODEL = 1024
DEPTH = 2
FFN_HIDDEN = 2816
SSD_D_INNER = 2048
SSD_HEADS = 32
SSD_HEAD_DIM = 64
SSD_GROUPS = 4
SSD_STATE = 128
SSD_CHUNK = 128
SSD_CONV_DIM = 3072
SSD_CONV_WIDTH = 4
S5_WIDTH = 1024
S5_GROUP = 16
S5_GROUPS = 64
S5_STATE = 64
S5_MAX_REAL = -1e-4
S5_BLOCKS = 8
RMS_EPS = 1e-6
N_DEV = 8
LANES = 1024

ADAM_LR = 0.001
ADAM_B1 = 0.9
ADAM_B2 = 0.999
ADAM_EPS = 1e-08
ADAM_WD = 0.01
ADAM_STEP = 10

VMEM_LIMIT_BYTES = 48 * 1024 * 1024

WEIGHTS = ['ffn1_pre_g', 'ffn1_post_g', 'ffn1_w_gate', 'ffn1_w_up', 'ffn1_w_down', 'mix_pre_g', 'mix_post_g',
           'w_in', 'ssd_conv_w', 'ssd_conv_b', 'ssd_dt_bias', 'ssd_a_log', 'ssd_d', 'ssd_norm_g', 'w_branch_a',
           's5_lambda_re', 's5_lambda_im', 's5_b_re', 's5_b_im', 's5_c_re', 's5_c_im', 's5_log_step', 's5_d',
           's5_w_glu', 'w_branch_b', 'w_out', 'ffn2_pre_g', 'ffn2_post_g', 'ffn2_w_gate', 'ffn2_w_up',
           'ffn2_w_down']
SHARDED = {'ffn1_w_gate': 2, 'ffn1_w_up': 2, 'ffn1_w_down': 1, 'w_in': 2, 'ssd_conv_w': 2, 'w_branch_a': 1,
           's5_w_glu': 2, 'w_branch_b': 1, 'w_out': 1, 'ffn2_w_gate': 2, 'ffn2_w_up': 2, 'ffn2_w_down': 1}
SHARDED_ORDER = [n for n in WEIGHTS if n in SHARDED]
SMALL_ORDER = [n for n in WEIGHTS if n not in SHARDED or n == 'ssd_conv_w']


def _pcall(body, **kw):
    return pl.pallas_call(body, **kw)


def _scall(body, *, name, out_type, scratch_types, collective_id):
    return pl.kernel(body, out_type=out_type, mesh=plsc.ScalarSubcoreMesh(axis_name="sequencer", num_cores=1),
                     scratch_types=scratch_types, name=name,
                     compiler_params=pltpu.CompilerParams(collective_id=collective_id))


def _params(*sem):
    return pltpu.CompilerParams(dimension_semantics=sem, vmem_limit_bytes=VMEM_LIMIT_BYTES)


def _tile(n, pref, align=128):
    if n <= pref:
        return n
    t = (pref // align) * align
    while t >= align:
        if n % t == 0:
            return t
        t -= align
    return n


def _rms(x, g):
    return x * lax.rsqrt(jnp.mean(x * x, axis=-1, keepdims=True) + RMS_EPS) * g


def _sigmoid(x):
    return 1.0 / (1.0 + jnp.exp(-x))


def _silu(x):
    return x * _sigmoid(x)


def _gelu(x):
    return 0.5 * x * (1.0 + jnp.tanh(math.sqrt(2.0 / math.pi) * (x + 0.044715 * (x * x * x))))


def _softplus(x):
    return jnp.maximum(x, 0.0) + jnp.log(1.0 + jnp.exp(-jnp.abs(x)))


def _dot(a, b, dims):
    return lax.dot_general(a.astype(BF16), b.astype(BF16), (dims, ((), ())), preferred_element_type=F32)


_NN = ((1,), (0,))
_NT = ((1,), (1,))
_TN = ((0,), (0,))


@jax.custom_vjp
def _bdot_nn(a, b):
    return _dot(a, b, _NN)


_bdot_nn.defvjp(lambda a, b: (_dot(a, b, _NN), (a, b)),
                lambda r, g: (_dot(g, r[1], _NT), _dot(r[0], g, _TN)))


@jax.custom_vjp
def _bdot_nt(a, b):
    return _dot(a, b, _NT)


_bdot_nt.defvjp(lambda a, b: (_dot(a, b, _NT), (a, b)),
                lambda r, g: (_dot(g, r[1], _NN), _dot(g, r[0], _TN)))


@jax.custom_vjp
def _bdot_tn(a, b):
    return _dot(a, b, _TN)


_bdot_tn.defvjp(lambda a, b: (_dot(a, b, _TN), (a, b)),
                lambda r, g: (_dot(r[1], g, _NT), _dot(r[0], g, _NN)))


def _fdot(a, b, dims=_NN):
    return lax.dot_general(a, b, (dims, ((), ())), precision=HIGHEST, preferred_element_type=F32)


def _sel3(x, sel, dims, x_first):
    p1 = x.astype(BF16)
    r1 = x - p1.astype(F32)
    p2 = r1.astype(BF16)
    p3 = (r1 - p2.astype(F32)).astype(BF16)
    sel = sel.astype(BF16)
    out = None
    for piece in (p1, p2, p3):
        d = lax.dot_general(*((piece, sel) if x_first else (sel, piece)), (dims, ((), ())), preferred_element_type=F32)
        out = d if out is None else out + d
    return out


@jax.custom_vjp
def _sel_right(x, sel):
    return _sel3(x, sel, _NN, True)


_sel_right.defvjp(lambda x, sel: (_sel3(x, sel, _NN, True), sel),
                  lambda sel, g: (_sel3(g, sel, _NT, True), jnp.zeros_like(sel)))


@jax.custom_vjp
def _sel_left(sel, x):
    return _sel3(x, sel, _NN, False)


_sel_left.defvjp(lambda sel, x: (_sel3(x, sel, _NN, False), sel),
                 lambda sel, g: (jnp.zeros_like(sel), _sel3(g, sel, _TN, False)))


@jax.custom_vjp
def _sel_left_nt(sel, x):
    return _sel3(x, sel, _NT, False)


_sel_left_nt.defvjp(lambda sel, x: (_sel3(x, sel, _NT, False), sel),
                    lambda sel, g: (jnp.zeros_like(sel), _sel3(g, sel, _TN, True)))


def _mm(a, b, *, name, ta=False, tb=False, out_dtype=F32, tm=2048, tn=512, tk=2048, shards=None):
    m, k = (a.shape[1], a.shape[0]) if ta else a.shape
    n = b.shape[0] if tb else b.shape[1]
    assert k == (b.shape[1] if tb else b.shape[0]), (a.shape, b.shape, ta, tb)
    if shards == 'rows':
        tm = min(tm, m // N_DEV)
    if shards == 'cols':
        tn = n // N_DEV
    tm, tn, tk = _tile(m, tm), _tile(n, tn), _tile(k, tk)
    nk = k // tk
    a_spec = pl.BlockSpec((tk, tm), lambda i, j, kk: (kk, i)) if ta else pl.BlockSpec((tm, tk), lambda i, j, kk: (i, kk))
    b_spec = pl.BlockSpec((tn, tk), lambda i, j, kk: (j, kk)) if tb else pl.BlockSpec((tk, tn), lambda i, j, kk: (kk, j))
    dims = ((0 if ta else 1,), (1 if tb else 0,))
    out_spec = pl.BlockSpec((tm, tn), lambda i, j, kk: (i, j))
    out_shape = jax.ShapeDtypeStruct((m, n), out_dtype)
    if shards == 'rows':
        per = m // N_DEV // tm
        out_shape = jax.ShapeDtypeStruct((N_DEV, 1, m // N_DEV, n), out_dtype)
        out_spec = pl.BlockSpec((None, None, tm, tn), lambda i, j, kk: (i // per, 0, i % per, j))
    elif shards == 'cols':
        out_shape = jax.ShapeDtypeStruct((N_DEV, 1, m, n // N_DEV), out_dtype)
        out_spec = pl.BlockSpec((None, None, tm, tn), lambda i, j, kk: (j, 0, i, 0))

    def body(a_ref, b_ref, o_ref, acc_ref):
        kk = pl.program_id(2)

        @pl.when(kk == 0)
        def _():
            acc_ref[...] = jnp.zeros_like(acc_ref)

        acc_ref[...] += _dot(a_ref[...], b_ref[...], dims)

        @pl.when(kk == nk - 1)
        def _():
            o_ref[...] = acc_ref[...].astype(o_ref.dtype)

    return _pcall(
        body, name=name, grid=(m // tm, n // tn, nk),
        in_specs=[a_spec, b_spec], out_specs=out_spec, out_shape=out_shape,
        scratch_shapes=[pltpu.VMEM((tm, tn), F32)],
        compiler_params=_params("parallel", "parallel", "arbitrary"),
    )(a, b)


def _rspec(tm, w, cb=0, percol=False):
    return pl.BlockSpec((tm, w), (lambda j, i: (i, cb + j)) if percol else (lambda j, i: (i, cb)))


def _bspec(w, cb=0, percol=False, rows=1):
    return pl.BlockSpec((rows, w), (lambda j, i: (0, cb + j)) if percol else (lambda j, i: (0, cb)))


def _rows(fn, *, name, nrow, ncol=1, ins, outs):
    n_in = len(ins)
    accs = [o[2] for o in outs]

    def body(*refs):
        vals = fn(*[r[...] for r in refs[:n_in]])
        if not isinstance(vals, (tuple, list)):
            vals = (vals,)
        i = pl.program_id(1)
        for ref, val, acc in zip(refs[n_in:], vals, accs):
            if acc:
                @pl.when(i == 0)
                def _(ref=ref):
                    ref[...] = jnp.zeros_like(ref)

                ref[...] += jnp.broadcast_to(val, ref.shape).astype(ref.dtype)
            else:
                ref[...] = val.astype(ref.dtype)

    res = _pcall(
        body, name=name, grid=(ncol, nrow),
        in_specs=[s for _, s in ins], out_specs=[o[1] for o in outs], out_shape=[o[0] for o in outs],
        compiler_params=_params("parallel", "arbitrary"),
    )(*[a for a, _ in ins])
    return res


def _sds(shape, dtype=F32):
    return jax.ShapeDtypeStruct(shape, dtype)


def _rms_fwd(h, g, *, name, tm):
    t, d = h.shape
    return _rows(lambda x, gg: _rms(x, gg), name=name, nrow=t // tm,
                 ins=[(h, _rspec(tm, d)), (g, _bspec(d))],
                 outs=[(_sds((t, d), BF16), _rspec(tm, d), False)])[0]


def _resid_fwd(h, f, g, scale, *, name, tm):
    t, d = h.shape
    return _rows(lambda x, ff, gg: x + scale * _rms(ff, gg), name=name, nrow=t // tm,
                 ins=[(h, _rspec(tm, d)), (f, _rspec(tm, d)), (g, _bspec(d))],
                 outs=[(_sds((t, d)), _rspec(tm, d), False)])[0]


def _resid_bwd(f, g, dh, scale, *, name, tm):
    t, d = f.shape

    def fn(ff, gg, dd):
        _, vjp = jax.vjp(lambda a, b: scale * _rms(a, b), ff, gg)
        return vjp(dd)

    return _rows(fn, name=name, nrow=t // tm,
                 ins=[(f, _rspec(tm, d)), (g, _bspec(d)), (dh, _rspec(tm, d))],
                 outs=[(_sds((t, d), BF16), _rspec(tm, d), False), (_sds((1, d)), _bspec(d), True)])


def _rms_bwd(h, g, dh, dxns, *, name, tm):
    t, d = h.shape

    def fn(x, gg, dd, *dx):
        _, vjp = jax.vjp(_rms, x, gg)
        tot = dx[0]
        for more in dx[1:]:
            tot = tot + more
        dxx, dg = vjp(tot)
        return dd + dxx, dg

    return _rows(fn, name=name, nrow=t // tm,
                 ins=[(h, _rspec(tm, d)), (g, _bspec(d)), (dh, _rspec(tm, d))] + [(x, _rspec(tm, d)) for x in dxns],
                 outs=[(_sds((t, d)), _rspec(tm, d), False), (_sds((1, d)), _bspec(d), True)])


NB = FFN_HIDDEN // N_DEV
MM_ROWS = 2048


def _ffn_up(xn, wg, wu, *, name):
    t = xn.shape[0]
    tm = _tile(t, MM_ROWS)
    wspec = pl.BlockSpec((None, None, NB, D_MODEL), lambda i, j: (j, 0, 0, 0))

    def body(x_ref, g_ref, u_ref, ab_ref, hh_ref):
        x = x_ref[...]
        a, b = _dot(x, g_ref[...], _NT), _dot(x, u_ref[...], _NT)
        ab_ref[0] = a.astype(ab_ref.dtype)
        ab_ref[1] = b.astype(ab_ref.dtype)
        hh_ref[...] = (_silu(a) * b).astype(hh_ref.dtype)

    return _pcall(
        body, name=name, grid=(t // tm, N_DEV),
        in_specs=[pl.BlockSpec((tm, D_MODEL), lambda i, j: (i, 0)), wspec, wspec],
        out_specs=[pl.BlockSpec((None, 2, tm, NB), lambda i, j: (j, 0, i, 0)),
                   pl.BlockSpec((None, tm, NB), lambda i, j: (j, i, 0))],
        out_shape=[_sds((N_DEV, 2, t, NB), BF16), _sds((N_DEV, t, NB), BF16)],
        compiler_params=_params("parallel", "parallel"),
    )(xn, wg, wu)


def _ffn_down(hh, wd, *, name):
    t = hh.shape[1]
    tm = _tile(t, 512)

    def body(h_ref, w_ref, o_ref):
        acc = _dot(h_ref[0], w_ref[0, 0], _NN)
        for k in range(1, N_DEV):
            acc = acc + _dot(h_ref[k], w_ref[k, 0], _NN)
        o_ref[...] = acc

    return _pcall(
        body, name=name, grid=(t // tm,),
        in_specs=[pl.BlockSpec((N_DEV, tm, NB), lambda i: (0, i, 0)),
                  pl.BlockSpec((N_DEV, 1, NB, D_MODEL), lambda i: (0, 0, 0, 0))],
        out_specs=pl.BlockSpec((tm, D_MODEL), lambda i: (i, 0)), out_shape=_sds((t, D_MODEL)),
        compiler_params=_params("parallel"),
    )(hh, wd)


def _ffn_down_dx(df, wd, ab, *, name):
    t = df.shape[0]
    tm = _tile(t, MM_ROWS)

    def body(d_ref, w_ref, ab_ref, o_ref):
        dhh = _dot(d_ref[...], w_ref[...], _NT)
        _, vjp = jax.vjp(lambda a, b: _silu(a) * b, ab_ref[0].astype(F32), ab_ref[1].astype(F32))
        da, db = vjp(dhh)
        o_ref[0] = da.astype(o_ref.dtype)
        o_ref[1] = db.astype(o_ref.dtype)

    blk = pl.BlockSpec((None, 2, tm, NB), lambda i, j: (j, 0, i, 0))
    return _pcall(
        body, name=name, grid=(t // tm, N_DEV),
        in_specs=[pl.BlockSpec((tm, D_MODEL), lambda i, j: (i, 0)),
                  pl.BlockSpec((None, None, NB, D_MODEL), lambda i, j: (j, 0, 0, 0)), blk],
        out_specs=blk, out_shape=_sds((N_DEV, 2, t, NB), BF16), compiler_params=_params("parallel", "parallel"),
    )(df, wd, ab)


def _ffn_down_dw(hh, df, *, name, tn=512):
    t = df.shape[0]
    tk = _tile(t, 2048)
    nk = t // tk

    def body(h_ref, d_ref, o_ref, acc_ref):
        kk = pl.program_id(2)

        @pl.when(kk == 0)
        def _():
            acc_ref[...] = jnp.zeros_like(acc_ref)

        acc_ref[...] += _dot(h_ref[...], d_ref[...], _TN)

        @pl.when(kk == nk - 1)
        def _():
            o_ref[...] = acc_ref[...].astype(o_ref.dtype)

    return _pcall(
        body, name=name, grid=(N_DEV, D_MODEL // tn, nk),
        in_specs=[pl.BlockSpec((None, tk, NB), lambda j, n, kk: (j, kk, 0)),
                  pl.BlockSpec((tk, tn), lambda j, n, kk: (kk, n))],
        out_specs=pl.BlockSpec((None, None, NB, tn), lambda j, n, kk: (j, 0, 0, n)),
        out_shape=_sds((N_DEV, 1, NB, D_MODEL), BF16),
        scratch_shapes=[pltpu.VMEM((NB, tn), F32)],
        compiler_params=_params("parallel", "parallel", "arbitrary"),
    )(hh, df)


def _ffn_up_dx(dab, wg, wu, *, name):
    t = dab.shape[2]
    tm = _tile(t, MM_ROWS // 2)
    wspec = pl.BlockSpec((None, None, NB, D_MODEL), lambda i, j: (j, 0, 0, 0))

    def body(d_ref, g_ref, u_ref, o_ref):
        @pl.when(pl.program_id(1) == 0)
        def _():
            o_ref[...] = jnp.zeros_like(o_ref)

        o_ref[...] += _dot(d_ref[0], g_ref[...], _NN) + _dot(d_ref[1], u_ref[...], _NN)

    return _pcall(
        body, name=name, grid=(t // tm, N_DEV),
        in_specs=[pl.BlockSpec((None, 2, tm, NB), lambda i, j: (j, 0, i, 0)), wspec, wspec],
        out_specs=pl.BlockSpec((tm, D_MODEL), lambda i, j: (i, 0)), out_shape=_sds((t, D_MODEL)),
        compiler_params=_params("parallel", "arbitrary"),
    )(dab, wg, wu)


def _ffn_up_dw(xn, dab, *, name):
    t = xn.shape[0]

    def body(x_ref, d_ref, og_ref, ou_ref):
        x = x_ref[...]
        og_ref[...] = _dot(d_ref[0], x, _TN).astype(og_ref.dtype)
        ou_ref[...] = _dot(d_ref[1], x, _TN).astype(ou_ref.dtype)

    out = pl.BlockSpec((None, None, NB, D_MODEL), lambda j: (j, 0, 0, 0))
    return _pcall(
        body, name=name, grid=(N_DEV,),
        in_specs=[pl.BlockSpec((t, D_MODEL), lambda j: (0, 0)), pl.BlockSpec((None, 2, t, NB), lambda j: (j, 0, 0, 0))],
        out_specs=[out, out], out_shape=[_sds((N_DEV, 1, NB, D_MODEL), BF16)] * 2,
        compiler_params=_params("parallel"),
    )(xn, dab)


def _ffn_fwd(h, p, tag, tm):
    xn = _rms_fwd(h, p['pre_g'], name=f"{tag}_rms", tm=tm)
    ab, hh = _ffn_up(xn, p['w_gate'], p['w_up'], name=f"{tag}_up")
    f = _ffn_down(hh, p['w_down'], name=f"{tag}_down")
    out = _resid_fwd(h, f, p['post_g'], 0.5, name=f"{tag}_res", tm=tm)
    return out, (h, xn, ab, hh, f)


def _ffn_bwd(dh, p, saved, tag, tm, after_first):
    h, xn, ab, hh, f = saved
    df, dpost = _resid_bwd(f, p['post_g'], dh, 0.5, name=f"{tag}_res_bwd", tm=tm)
    df = after_first(df)
    dab = _ffn_down_dx(df, p['w_down'], ab, name=f"{tag}_down_dx")
    bufs = {f'{tag}_w_down': _ffn_down_dw(hh, df, name=f"{tag}_down_dw")}
    dxn = _ffn_up_dx(dab, p['w_gate'], p['w_up'], name=f"{tag}_up_dx")
    bufs[f'{tag}_w_gate'], bufs[f'{tag}_w_up'] = _ffn_up_dw(xn, dab, name=f"{tag}_up_dw")
    dh_in, dpre = _rms_bwd(h, p['pre_g'], dh, [dxn], name=f"{tag}_rms_bwd", tm=tm)
    return dh_in, bufs, {f'{tag}_pre_g': dpre, f'{tag}_post_g': dpost}


CONV_COLS = 256


def _shift_down(x, s):
    rows = lax.broadcasted_iota(jnp.int32, x.shape, 0)
    return jnp.where(rows >= s, pltpu.roll(x, s, axis=0), 0.0)


def _shift_up(x, s):
    t = x.shape[0]
    rows = lax.broadcasted_iota(jnp.int32, x.shape, 0)
    return jnp.where(rows < t - s, pltpu.roll(x, t - s, axis=0), 0.0)


def _conv_fwd(proj, col0, w, b, *, name):
    t = proj.shape[0]
    c = w.shape[1]
    cb0 = col0 // CONV_COLS

    def body(x_ref, w_ref, b_ref, o_ref):
        x = x_ref[...]
        acc = x * w_ref[3:4, :] + b_ref[...]
        for k in range(SSD_CONV_WIDTH - 1):
            acc = acc + _shift_down(x, SSD_CONV_WIDTH - 1 - k) * w_ref[k:k + 1, :]
        o_ref[...] = _silu(acc)

    return _pcall(
        body, name=name, grid=(c // CONV_COLS,),
        in_specs=[pl.BlockSpec((t, CONV_COLS), lambda j: (0, cb0 + j)),
                  pl.BlockSpec((SSD_CONV_WIDTH, CONV_COLS), lambda j: (0, j)),
                  pl.BlockSpec((1, CONV_COLS), lambda j: (0, j))],
        out_specs=pl.BlockSpec((t, CONV_COLS), lambda j: (0, j)),
        out_shape=_sds((t, c)), compiler_params=_params("parallel"),
    )(proj, w, b)


def _conv_bwd(proj, col0, w, b, dout, *, name):
    t = proj.shape[0]
    c = w.shape[1]
    cb0 = col0 // CONV_COLS

    def body(x_ref, w_ref, b_ref, d_ref, dx_ref, dw_ref, db_ref):
        x = x_ref[...]
        shifted = [_shift_down(x, SSD_CONV_WIDTH - 1 - k) for k in range(SSD_CONV_WIDTH - 1)] + [x]
        pre = b_ref[...] + shifted[3] * w_ref[3:4, :]
        for k in range(SSD_CONV_WIDTH - 1):
            pre = pre + shifted[k] * w_ref[k:k + 1, :]
        sg = _sigmoid(pre)
        dpre = d_ref[...] * (sg * (1.0 + pre * (1.0 - sg)))
        dx = dpre * w_ref[3:4, :]
        for k in range(SSD_CONV_WIDTH - 1):
            dx = dx + _shift_up(dpre, SSD_CONV_WIDTH - 1 - k) * w_ref[k:k + 1, :]
        dx_ref[...] = dx.astype(dx_ref.dtype)
        for k in range(SSD_CONV_WIDTH):
            dw_ref[k:k + 1, :] = jnp.sum(dpre * shifted[k], axis=0, keepdims=True)
        db_ref[...] = jnp.sum(dpre, axis=0, keepdims=True)

    return _pcall(
        body, name=name, grid=(c // CONV_COLS,),
        in_specs=[pl.BlockSpec((t, CONV_COLS), lambda j: (0, cb0 + j)),
                  pl.BlockSpec((SSD_CONV_WIDTH, CONV_COLS), lambda j: (0, j)),
                  pl.BlockSpec((1, CONV_COLS), lambda j: (0, j)),
                  pl.BlockSpec((t, CONV_COLS), lambda j: (0, j))],
        out_specs=[pl.BlockSpec((t, CONV_COLS), lambda j: (0, j)),
                   pl.BlockSpec((SSD_CONV_WIDTH, CONV_COLS), lambda j: (0, j)),
                   pl.BlockSpec((1, CONV_COLS), lambda j: (0, j))],
        out_shape=[_sds((t, c), BF16), _sds((SSD_CONV_WIDTH, c)), _sds((1, c))],
        compiler_params=_params("parallel"),
    )(proj, w, b, dout)


HALF = 256
HEADS_PER_HALF = 4
PAD_HEADS = 128


def _head_expanders():
    k = lax.broadcasted_iota(jnp.int32, (PAD_HEADS, HALF), 0)
    j = lax.broadcasted_iota(jnp.int32, (PAD_HEADS, HALF), 1)
    kt = lax.broadcasted_iota(jnp.int32, (HALF, PAD_HEADS), 1)
    jt = lax.broadcasted_iota(jnp.int32, (HALF, PAD_HEADS), 0)
    es, ets = [], []
    for half in range(2):
        es.append(jnp.where(k == j // SSD_HEAD_DIM + half * HEADS_PER_HALF, 1.0, 0.0).astype(F32))
        ets.append(jnp.where(kt == jt // SSD_HEAD_DIM + half * HEADS_PER_HALF, 1.0, 0.0).astype(F32))
    return es, ets


def _ssd_chunk(x_lo, x_hi, bm, cm, dtr, dtb8, alog8, dsk8, s_lo, s_hi):
    q = x_lo.shape[0]
    es, ets = _head_expanders()
    rowmean = lambda v: jnp.sum(v, axis=0, keepdims=True) * 0.125
    dt = _softplus(dtr + rowmean(dtb8))
    a = -jnp.exp(rowmean(alog8))
    adt = a * dt
    adt_tot8 = jnp.broadcast_to(jnp.sum(adt, axis=0, keepdims=True), (8, PAD_HEADS))
    ll = lax.broadcasted_iota(jnp.int32, (q, q), 0)
    ss = lax.broadcasted_iota(jnp.int32, (q, q), 1)
    ltri = jnp.where(ll >= ss, 1.0, 0.0).astype(F32)
    lane = lax.broadcasted_iota(jnp.int32, (1, HALF), 1)
    cb = _bdot_nt(cm, bm)
    outs = []
    for half, (x, s_in) in enumerate(((x_lo, s_lo), (x_hi, s_hi))):
        e, et = es[half], ets[half]
        dtf = _sel_right(dt, e)
        af = _sel_right(adt, e)
        dskf = rowmean(_sel_right(dsk8, e))
        acum = _sel_left(ltri, af)
        alast = jnp.sum(af, axis=0, keepdims=True)
        xdt = x * dtf
        ydiag = jnp.zeros((q, HALF), F32)
        for r in range(HEADS_PER_HALF):
            sel = lane == r * SSD_HEAD_DIM
            ac_r = jnp.sum(jnp.where(sel, acum, 0.0), axis=1, keepdims=True)
            a_r = jnp.sum(jnp.where(sel, af, 0.0), axis=1, keepdims=True)
            arow = jnp.sum(jnp.where(ll <= ss, a_r, 0.0), axis=0, keepdims=True)
            decay = jnp.exp(jnp.where(ll >= ss, ac_r - arow, -jnp.inf))
            yh = _bdot_nn(cb * decay, xdt)
            ydiag = ydiag + jnp.where(lane // SSD_HEAD_DIM == r, yh, 0.0)
        st = _bdot_tn(xdt * jnp.exp(alast - acum), bm)
        yoff = _bdot_nt(cm, s_in) * jnp.exp(acum)
        y = ydiag + yoff + dskf * x
        alast_col = jnp.sum(_sel_left_nt(et, adt_tot8), axis=1, keepdims=True) * 0.125
        outs.append((y, jnp.exp(alast_col) * s_in + st))
    return outs[0][0], outs[1][0], outs[0][1], outs[1][1]


def _ssd_specs(t, rev):
    q = SSD_CHUNK
    nc = t // q
    ci = (lambda c: nc - 1 - c) if rev else (lambda c: c)
    xcol0 = SSD_D_INNER // SSD_STATE
    return dict(
        x_lo=pl.BlockSpec((q, HALF), lambda g, c: (ci(c), 2 * g)),
        x_hi=pl.BlockSpec((q, HALF), lambda g, c: (ci(c), 2 * g + 1)),
        bm=pl.BlockSpec((q, SSD_STATE), lambda g, c: (ci(c), xcol0 + g)),
        cm=pl.BlockSpec((q, SSD_STATE), lambda g, c: (ci(c), xcol0 + SSD_GROUPS + g)),
        dt=pl.BlockSpec((None, q, PAD_HEADS), lambda g, c: (g, ci(c), 0)),
        par=pl.BlockSpec((None, 8, PAD_HEADS), lambda g, c: (g, 0, 0)),
        st=pl.BlockSpec((None, None, 2, HALF, SSD_STATE), lambda g, c: (ci(c), g, 0, 0, 0)),
        y=pl.BlockSpec((q, 2 * HALF), lambda g, c: (ci(c), g)),
        grp=pl.BlockSpec((q, SSD_STATE), lambda g, c: (ci(c), g)),
    )


def _ssd_fwd(xc, dt4, dtb, alog, dsk, *, name):
    t = xc.shape[0]
    nc = t // SSD_CHUNK
    sp = _ssd_specs(t, False)

    def body(xl, xh, bm, cm, dt, p0, p1, p2, y_ref, sin_ref, st_ref):
        @pl.when(pl.program_id(1) == 0)
        def _():
            st_ref[...] = jnp.zeros_like(st_ref)

        sin_ref[...] = st_ref[...]
        y_lo, y_hi, so_lo, so_hi = _ssd_chunk(xl[...], xh[...], bm[...], cm[...], dt[...], p0[...], p1[...],
                                              p2[...], st_ref[0], st_ref[1])
        y_ref[:, :HALF] = y_lo
        y_ref[:, HALF:] = y_hi
        st_ref[0] = so_lo
        st_ref[1] = so_hi

    return _pcall(
        body, name=name, grid=(SSD_GROUPS, nc),
        in_specs=[sp['x_lo'], sp['x_hi'], sp['bm'], sp['cm'], sp['dt'], sp['par'], sp['par'], sp['par']],
        out_specs=[sp['y'], sp['st']],
        out_shape=[_sds((t, SSD_D_INNER)), _sds((nc, SSD_GROUPS, 2, HALF, SSD_STATE))],
        scratch_shapes=[pltpu.VMEM((2, HALF, SSD_STATE), F32)],
        compiler_params=_params("parallel", "arbitrary"),
    )(xc, xc, xc, xc, dt4, dtb, alog, dsk)


def _ssd_bwd(xc, dt4, dtb, alog, dsk, sin, dy, *, name):
    t = xc.shape[0]
    nc = t // SSD_CHUNK
    sp = _ssd_specs(t, True)

    def body(xl, xh, bm, cm, dt, p0, p1, p2, sin_ref, dy_ref,
             dx_ref, db_ref, dc_ref, ddt_ref, dp0, dp1, dp2, dst_ref):
        first = pl.program_id(1) == 0

        @pl.when(first)
        def _():
            dst_ref[...] = jnp.zeros_like(dst_ref)

        _, vjp = jax.vjp(_ssd_chunk, xl[...], xh[...], bm[...], cm[...], dt[...], p0[...], p1[...], p2[...],
                         sin_ref[0], sin_ref[1])
        dxl, dxh, dbm, dcm, ddt, g0, g1, g2, ds_lo, ds_hi = vjp(
            (dy_ref[:, :HALF], dy_ref[:, HALF:], dst_ref[0], dst_ref[1]))
        dx_ref[:, :HALF] = dxl
        dx_ref[:, HALF:] = dxh
        db_ref[...] = dbm
        dc_ref[...] = dcm
        ddt_ref[...] = ddt
        dst_ref[0] = ds_lo
        dst_ref[1] = ds_hi
        for ref, g in ((dp0, g0), (dp1, g1), (dp2, g2)):
            tot = jnp.broadcast_to(jnp.sum(g, axis=0, keepdims=True), ref.shape)

            @pl.when(first)
            def _(ref=ref):
                ref[...] = jnp.zeros_like(ref)

            ref[...] += tot

    return _pcall(
        body, name=name, grid=(SSD_GROUPS, nc),
        in_specs=[sp['x_lo'], sp['x_hi'], sp['bm'], sp['cm'], sp['dt'], sp['par'], sp['par'], sp['par'],
                  sp['st'], sp['y']],
        out_specs=[sp['y'], sp['grp'], sp['grp'], sp['dt'], sp['par'], sp['par'], sp['par']],
        out_shape=[_sds((t, SSD_D_INNER)), _sds((t, SSD_GROUPS * SSD_STATE)), _sds((t, SSD_GROUPS * SSD_STATE)),
                   _sds((SSD_GROUPS, t, PAD_HEADS))] + [_sds((SSD_GROUPS, 8, PAD_HEADS))] * 3,
        scratch_shapes=[pltpu.VMEM((2, HALF, SSD_STATE), F32)],
        compiler_params=_params("parallel", "arbitrary"),
    )(xc, xc, xc, xc, dt4, dtb, alog, dsk, sin, dy)


def _gatenorm(y, z, g):
    v = y * _silu(z)
    return v * lax.rsqrt(jnp.mean(v * v, axis=-1, keepdims=True) + RMS_EPS) * g


S5_CH = S5_WIDTH // S5_BLOCKS
S5_ST = S5_CH * S5_STATE // S5_GROUP
SCAN_UNROLL = 8


def _cmul(ar, ai, br, bi):
    return ar * br - ai * bi, ar * bi + ai * br


def _segment_power(ar, ai, n):
    assert n & (n - 1) == 0
    for _ in range(n.bit_length() - 1):
        ar, ai = _cmul(ar, ai, ar, ai)
    return ar, ai


def _carry_in(fr, fi, pr, pi, reverse):
    rows = lax.broadcasted_iota(jnp.int32, fr.shape, 0)
    cr = jnp.zeros_like(fr[0:1])
    ci = jnp.zeros_like(cr)
    outr = jnp.zeros_like(fr)
    outi = jnp.zeros_like(fr)
    order = range(6, -1, -1) if reverse else range(1, 8)
    for j in order:
        src = j + 1 if reverse else j - 1
        nr, ni = _cmul(pr[0:1], pi[0:1], cr, ci)
        cr, ci = nr + fr[src:src + 1], ni + fi[src:src + 1]
        outr = jnp.where(rows == j, cr, outr)
        outi = jnp.where(rows == j, ci, outi)
    return outr, outi


def _s5_specs(t):
    return dict(ch=pl.BlockSpec((t, S5_CH), lambda j: (0, j)), st=pl.BlockSpec((t, S5_ST), lambda j: (0, j)),
                lam=pl.BlockSpec((1, S5_ST), lambda j: (0, j)),
                b=pl.BlockSpec((None, S5_CH, S5_ST), lambda j: (j, 0, 0)),
                c=pl.BlockSpec((None, S5_ST, S5_CH), lambda j: (j, 0, 0)))


def _s5_fwd(u5, bre, bim, cre, cim, lr, li, *, name):
    t = u5.shape[0]
    nrt = t // 8

    def body(u_ref, bre_ref, bim_ref, cre_ref, cim_ref, lr_ref, li_ref, sr_ref, si_ref, y_ref, br_ref, bi_ref):
        u = u_ref[...]
        br_ref[...] = _dot(u, bre_ref[...], _NN)
        bi_ref[...] = _dot(u, bim_ref[...], _NN)
        ar = jnp.broadcast_to(lr_ref[...], (8, S5_ST))
        ai = jnp.broadcast_to(li_ref[...], (8, S5_ST))

        def step(r, s, store):
            rows = pl.ds(pl.multiple_of(r * 8, 8), 8)
            nr, ni = _cmul(ar, ai, s[0], s[1])
            nr, ni = nr + br_ref[rows, :], ni + bi_ref[rows, :]
            if store:
                sr_ref[rows, :] = nr
                si_ref[rows, :] = ni
            return nr, ni

        zero = (jnp.zeros((8, S5_ST), F32), jnp.zeros((8, S5_ST), F32))
        fr, fi = lax.fori_loop(0, nrt, lambda r, s: step(r, s, False), zero, unroll=SCAN_UNROLL)
        pr, pi = _segment_power(ar, ai, nrt)
        init = _carry_in(fr, fi, pr, pi, False)
        lax.fori_loop(0, nrt, lambda r, s: step(r, s, True), init, unroll=SCAN_UNROLL)
        y_ref[...] = _dot(sr_ref[...], cre_ref[...], _NN) - _dot(si_ref[...], cim_ref[...], _NN)

    sp = _s5_specs(t)
    w = S5_BLOCKS * S5_ST
    return _pcall(
        body, name=name, grid=(S5_BLOCKS,),
        in_specs=[sp['ch'], sp['b'], sp['b'], sp['c'], sp['c'], sp['lam'], sp['lam']],
        out_specs=[sp['st'], sp['st'], sp['ch']], out_shape=[_sds((t, w)), _sds((t, w)), _sds((t, S5_WIDTH))],
        scratch_shapes=[pltpu.VMEM((t, S5_ST), F32)] * 2, compiler_params=_params("parallel"),
    )(u5, bre, bim, cre, cim, lr, li)


def _s5_bwd(dy, du_direct, u5, sr, si, bre, bim, cre, cim, lr, li, *, name):
    t = u5.shape[0]
    nrt = t // 8

    def body(dy_ref, dd_ref, u_ref, sr_ref, si_ref, bre_ref, bim_ref, cre_ref, cim_ref, lr_ref, li_ref,
             du_ref, dbre_ref, dbim_ref, dcre_ref, dcim_ref, dlr_ref, dli_ref, gr_ref, gi_ref):
        dyv = dy_ref[...]
        gr_ref[...] = _dot(dyv, cre_ref[...], _NT)
        gi_ref[...] = -_dot(dyv, cim_ref[...], _NT)
        dcre_ref[...] = _dot(sr_ref[...], dyv, _TN)
        dcim_ref[...] = -_dot(si_ref[...], dyv, _TN)
        dr_ref, di_ref = gr_ref, gi_ref
        ar = jnp.broadcast_to(lr_ref[...], (8, S5_ST))
        ai = -jnp.broadcast_to(li_ref[...], (8, S5_ST))
        zero = jnp.zeros((8, S5_ST), F32)

        def step1(k, g):
            rows = pl.ds(pl.multiple_of((nrt - 1 - k) * 8, 8), 8)
            nr, ni = _cmul(ar, ai, g[0], g[1])
            return nr + dr_ref[rows, :], ni + di_ref[rows, :]

        fr, fi = lax.fori_loop(0, nrt, step1, (zero, zero), unroll=SCAN_UNROLL)
        pr, pi = _segment_power(ar, ai, nrt)
        init = _carry_in(fr, fi, pr, pi, True)

        def step2(k, carry):
            gr, gi, accr, acci = carry
            r = nrt - 1 - k
            rows = pl.ds(pl.multiple_of(r * 8, 8), 8)
            prev = pl.ds(pl.multiple_of(jnp.maximum(r - 1, 0) * 8, 8), 8)
            nr, ni = _cmul(ar, ai, gr, gi)
            nr, ni = nr + dr_ref[rows, :], ni + di_ref[rows, :]
            gr_ref[rows, :] = nr
            gi_ref[rows, :] = ni
            keep = jnp.where(r > 0, 1.0, 0.0)
            pr_, pi_ = sr_ref[prev, :] * keep, si_ref[prev, :] * keep
            return nr, ni, accr + (pr_ * nr + pi_ * ni), acci + (pr_ * ni - pi_ * nr)

        _, _, accr, acci = lax.fori_loop(0, nrt, step2, (init[0], init[1], zero, zero), unroll=SCAN_UNROLL)
        last = pl.ds((nrt - 1) * 8, 8)
        pr_, pi_ = _shift_down(sr_ref[last, :], 1), _shift_down(si_ref[last, :], 1)
        g0r, g0i = gr_ref[0:8, :], gi_ref[0:8, :]
        accr = accr + (pr_ * g0r + pi_ * g0i)
        acci = acci + (pr_ * g0i - pi_ * g0r)
        dlr_ref[...] = jnp.sum(accr, axis=0, keepdims=True)
        dli_ref[...] = jnp.sum(acci, axis=0, keepdims=True)
        u = u_ref[...]
        dbre_ref[...] = _dot(u, gr_ref[...], _TN)
        dbim_ref[...] = _dot(u, gi_ref[...], _TN)
        du = dd_ref[...] + _dot(gr_ref[...], bre_ref[...], _NT) + _dot(gi_ref[...], bim_ref[...], _NT)
        du_ref[...] = du.astype(du_ref.dtype)

    sp = _s5_specs(t)
    w = S5_BLOCKS * S5_ST
    return _pcall(
        body, name=name, grid=(S5_BLOCKS,),
        in_specs=[sp['ch'], sp['ch'], sp['ch'], sp['st'], sp['st'], sp['b'], sp['b'], sp['c'], sp['c'], sp['lam'], sp['lam']],
        out_specs=[sp['ch'], sp['b'], sp['b'], sp['c'], sp['c'], sp['lam'], sp['lam']],
        out_shape=[_sds((t, S5_WIDTH), BF16), _sds((S5_BLOCKS, S5_CH, S5_ST)), _sds((S5_BLOCKS, S5_CH, S5_ST)),
                   _sds((S5_BLOCKS, S5_ST, S5_CH)), _sds((S5_BLOCKS, S5_ST, S5_CH)), _sds((1, w)), _sds((1, w))],
        scratch_shapes=[pltpu.VMEM((t, S5_ST), F32)] * 2, compiler_params=_params("parallel"),
    )(dy, du_direct, u5, sr, si, bre, bim, cre, cim, lr, li)


def _s5_expander():
    n = lax.broadcasted_iota(jnp.int32, (S5_STATE, S5_STATE * S5_GROUP), 0)
    j = lax.broadcasted_iota(jnp.int32, (S5_STATE, S5_STATE * S5_GROUP), 1)
    return jnp.where(n == j // S5_GROUP, 1.0, 0.0).astype(F32)


def _s5_discretise(lam_re, lam_im, log_step, b_re, b_im):
    lr = jnp.minimum(lam_re, S5_MAX_REAL)
    step = jnp.exp(log_step)
    mag = jnp.exp(lr * step)
    ang = lam_im * step
    lbr, lbi = mag * jnp.cos(ang), mag * jnp.sin(ang)
    p, q = lbr - 1.0, lbi
    den = lr * lr + lam_im * lam_im
    cr, ci = (p * lr + q * lam_im) / den, (q * lr - p * lam_im) / den
    e = _s5_expander()
    cre, cie = _fdot(cr, e), _fdot(ci, e)
    return lbr, lbi, cre * b_re - cie * b_im, cre * b_im + cie * b_re


def _s5_params_fwd(lam_re, lam_im, log_step, b_re, b_im, *, name):
    g, n, w = S5_GROUPS, S5_STATE, S5_STATE * S5_GROUP

    def body(a, b, c, d, e, o0, o1, o2, o3):
        for ref, val in zip((o0, o1, o2, o3), _s5_discretise(a[...], b[...], c[...], d[...], e[...])):
            ref[...] = val

    return _pcall(body, name=name, out_shape=[_sds((g, n)), _sds((g, n)), _sds((g, w)), _sds((g, w))])(
        lam_re, lam_im, log_step, b_re, b_im)


def _s5_params_bwd(lam_re, lam_im, log_step, b_re, b_im, cts, *, name):
    g, n, w = S5_GROUPS, S5_STATE, S5_STATE * S5_GROUP

    def body(a, b, c, d, e, c0, c1, c2, c3, o0, o1, o2, o3, o4):
        _, vjp = jax.vjp(_s5_discretise, a[...], b[...], c[...], d[...], e[...])
        for ref, val in zip((o0, o1, o2, o3, o4), vjp((c0[...], c1[...], c2[...], c3[...]))):
            ref[...] = val

    return _pcall(body, name=name,
                  out_shape=[_sds((g, n)), _sds((g, n)), _sds((g, 1)), _sds((g, w)), _sds((g, w))])(
        lam_re, lam_im, log_step, b_re, b_im, *cts)


def _perm(a):
    t, c = a.shape
    return a.reshape(8, t // 8, c).transpose(1, 0, 2).reshape(t, c)


def _unperm(a):
    t, c = a.shape
    return a.reshape(t // 8, 8, c).transpose(1, 0, 2).reshape(t, c)


def _blockdiag(m, rows_inner, cols_inner):
    m = m.reshape(S5_BLOCKS, 8, rows_inner, cols_inner)
    eye = jnp.eye(8, dtype=m.dtype)
    out = m[:, :, :, None, :] * eye[None, :, None, :, None]
    return out.reshape(S5_BLOCKS, 8 * rows_inner, 8 * cols_inner)


def _blockdiag_extract(m, rows_inner, cols_inner):
    m = m.reshape(S5_BLOCKS, 8, rows_inner, 8, cols_inner)
    d = jnp.diagonal(m, axis1=1, axis2=3)
    return d.transpose(0, 3, 1, 2).reshape(S5_GROUPS, rows_inner, cols_inner)


Z0, XBC0, GA0, GB0 = 0, SSD_D_INNER, SSD_D_INNER + SSD_CONV_DIM, SSD_D_INNER + SSD_CONV_DIM + D_MODEL
BIG = GB0 + D_MODEL


def _mixer_fwd(h, p, tm):
    t = h.shape[0]
    nrow = t // tm
    u = _rms_fwd(h, p['pre_g'], name="mix_rms", tm=tm)
    u_p = _perm(u)
    proj = _mm(u, p['w_big'], name="mix_in")
    dtr = _mm(u, p['w_dt'], name="mix_in_dt")
    u5 = _mm(u_p, p['w_u5'], name="mix_in_s5")
    xc = _conv_fwd(proj, XBC0, p['conv_w'], p['conv_b'], name="ssd_conv")
    dt4 = jnp.pad(dtr.reshape(t, SSD_GROUPS, 8).transpose(1, 0, 2), ((0, 0), (0, 0), (0, PAD_HEADS - 8)))
    y_ssd, s_in = _ssd_fwd(xc, dt4, p['dt_bias8'], p['a_log8'], p['d8'], name="ssd_scan")
    gw = SSD_D_INNER // SSD_GROUPS
    ya = _rows(_gatenorm, name="ssd_gate", nrow=nrow, ncol=SSD_GROUPS,
               ins=[(y_ssd, _rspec(tm, gw, 0, True)), (proj, _rspec(tm, gw, Z0 // gw, True)),
                    (p['norm_g'], _bspec(gw, 0, True))],
               outs=[(_sds((t, SSD_D_INNER), BF16), _rspec(tm, gw, 0, True), False)])[0]
    y_a = _mm(ya, p['w_a'], name="mix_a")
    lbr, lbi, bbr, bbi = _s5_params_fwd(p['lam_re'], p['lam_im'], p['log_step'], p['b_re'], p['b_im'], name="s5_par")
    bd = lambda m: _blockdiag(m.reshape(S5_GROUPS, S5_STATE, S5_GROUP).transpose(0, 2, 1), S5_GROUP, S5_STATE).astype(BF16)
    bre, bim = bd(bbr), bd(bbi)
    cre = _blockdiag(p['c_re'].transpose(0, 2, 1), S5_STATE, S5_GROUP).astype(BF16)
    cim = _blockdiag(p['c_im'].transpose(0, 2, 1), S5_STATE, S5_GROUP).astype(BF16)
    lr, li = lbr.reshape(1, -1), lbi.reshape(1, -1)
    sr, si, y5 = _s5_fwd(u5, bre, bim, cre, cim, lr, li, name="s5_scan")
    y5g = _rows(lambda a, b, d: _gelu(a + d * b), name="s5_act", nrow=nrow,
                ins=[(y5, _rspec(tm, S5_WIDTH)), (u5, _rspec(tm, S5_WIDTH)), (p['s5_d'], _bspec(S5_WIDTH))],
                outs=[(_sds((t, S5_WIDTH), BF16), _rspec(tm, S5_WIDTH), False)])[0]
    vg = _mm(y5g, p['w_glu'], name="s5_glu")
    ybin = _rows(lambda a, b: a * _sigmoid(b), name="s5_glu_act", nrow=nrow,
                 ins=[(vg, _rspec(tm, S5_WIDTH, 0)), (vg, _rspec(tm, S5_WIDTH, 1))],
                 outs=[(_sds((t, S5_WIDTH), BF16), _rspec(tm, S5_WIDTH), False)])[0]
    y_b = _unperm(_mm(ybin, p['w_b'], name="mix_b"))
    merged = _rows(lambda ga, gb, a, b: _sigmoid(ga) * a + _sigmoid(gb) * b, name="mix_merge", nrow=nrow,
                   ins=[(proj, _rspec(tm, D_MODEL, GA0 // D_MODEL)), (proj, _rspec(tm, D_MODEL, GB0 // D_MODEL)),
                        (y_a, _rspec(tm, D_MODEL)), (y_b, _rspec(tm, D_MODEL))],
                   outs=[(_sds((t, D_MODEL), BF16), _rspec(tm, D_MODEL), False)])[0]
    m = _mm(merged, p['w_out'], name="mix_out")
    out = _resid_fwd(h, m, p['post_g'], 1.0, name="mix_res", tm=tm)
    saved = dict(h=h, u=u, u_p=u_p, proj=proj, u5=u5, xc=xc, dt4=dt4, s_in=s_in, y_ssd=y_ssd, ya=ya, y_a=y_a,
                 bre=bre, bim=bim, cre=cre, cim=cim, lr=lr, li=li, sr=sr, si=si, y5=y5, y5g=y5g, vg=vg, ybin=ybin,
                 y_b=y_b, merged=merged, m=m)
    return out, saved


def _mixer_bwd(dh, p, s, tm, after_first):
    t = dh.shape[0]
    nrow = t // tm
    proj = s['proj']
    bufs = {}

    def grad_mm(a, b, wname, axis, name):
        bufs[wname] = _mm(a, b, ta=True, name=name, out_dtype=BF16, shards=axis)

    dm, dpost = _resid_bwd(s['m'], p['post_g'], dh, 1.0, name="mix_res_bwd", tm=tm)
    dm = after_first(dm)
    dmerged = _mm(dm, p['w_out'], tb=True, name="mix_out_dx")
    grad_mm(s['merged'], dm, 'w_out', 'rows', "mix_out_dw")

    def merge_bwd(ga, gb, a, b, d):
        _, vjp = jax.vjp(lambda ga_, gb_, a_, b_: _sigmoid(ga_) * a_ + _sigmoid(gb_) * b_, ga, gb, a, b)
        dga, dgb, da, db = vjp(d)
        return jnp.concatenate([dga, dgb], axis=1), da, db

    dgab, dy_a, dy_b = _rows(
        merge_bwd, name="mix_merge_bwd", nrow=nrow,
        ins=[(proj, _rspec(tm, D_MODEL, GA0 // D_MODEL)), (proj, _rspec(tm, D_MODEL, GB0 // D_MODEL)),
             (s['y_a'], _rspec(tm, D_MODEL)), (s['y_b'], _rspec(tm, D_MODEL)), (dmerged, _rspec(tm, D_MODEL))],
        outs=[(_sds((t, 2 * D_MODEL), BF16), _rspec(tm, 2 * D_MODEL), False),
              (_sds((t, D_MODEL), BF16), _rspec(tm, D_MODEL), False),
              (_sds((t, D_MODEL), BF16), _rspec(tm, D_MODEL), False)])
    dya = _mm(dy_a, p['w_a'], tb=True, name="mix_a_dx")
    grad_mm(s['ya'], dy_a, 'w_branch_a', 'rows', "mix_a_dw")
    gw = SSD_D_INNER // SSD_GROUPS

    def gate_bwd(y, z, g, d):
        _, vjp = jax.vjp(_gatenorm, y, z, g)
        return vjp(d)

    dy_ssd, dz, dnorm = _rows(
        gate_bwd, name="ssd_gate_bwd", nrow=nrow, ncol=SSD_GROUPS,
        ins=[(s['y_ssd'], _rspec(tm, gw, 0, True)), (proj, _rspec(tm, gw, Z0 // gw, True)),
             (p['norm_g'], _bspec(gw, 0, True)), (dya, _rspec(tm, gw, 0, True))],
        outs=[(_sds((t, SSD_D_INNER)), _rspec(tm, gw, 0, True), False),
              (_sds((t, SSD_D_INNER), BF16), _rspec(tm, gw, 0, True), False),
              (_sds((1, SSD_D_INNER)), _bspec(gw, 0, True), True)])
    dxs, dbm, dcm, ddt4, ddtb, dalog, ddsk = _ssd_bwd(s['xc'], s['dt4'], p['dt_bias8'], p['a_log8'], p['d8'],
                                                      s['s_in'], dy_ssd, name="ssd_scan_bwd")
    dxc = jnp.concatenate([dxs, dbm, dcm], axis=1)
    dxbc, dconv_w, dconv_b = _conv_bwd(proj, XBC0, p['conv_w'], p['conv_b'], dxc, name="ssd_conv_bwd")
    ddtr = ddt4[:, :, :8].transpose(1, 0, 2).reshape(t, SSD_HEADS)
    dy_bp = _perm(dy_b)
    dybin = _mm(dy_bp, p['w_b'], tb=True, name="mix_b_dx")
    grad_mm(s['ybin'], dy_bp, 'w_branch_b', 'rows', "mix_b_dw")

    def glu_bwd(a, b, d):
        _, vjp = jax.vjp(lambda a_, b_: a_ * _sigmoid(b_), a, b)
        da, db = vjp(d)
        return jnp.concatenate([da, db], axis=1)

    dvg = _rows(glu_bwd, name="s5_glu_act_bwd", nrow=nrow,
                ins=[(s['vg'], _rspec(tm, S5_WIDTH, 0)), (s['vg'], _rspec(tm, S5_WIDTH, 1)), (dybin, _rspec(tm, S5_WIDTH))],
                outs=[(_sds((t, 2 * S5_WIDTH), BF16), _rspec(tm, 2 * S5_WIDTH), False)])[0]
    dy5g = _mm(dvg, p['w_glu'], tb=True, name="s5_glu_dx")
    grad_mm(s['y5g'], dvg, 's5_w_glu', 'cols', "s5_glu_dw")

    def act_bwd(a, b, d, g):
        _, vjp = jax.vjp(lambda a_, b_, d_: _gelu(a_ + d_ * b_), a, b, d)
        return vjp(g)

    dy5, du5_direct, ds5d = _rows(
        act_bwd, name="s5_act_bwd", nrow=nrow,
        ins=[(s['y5'], _rspec(tm, S5_WIDTH)), (s['u5'], _rspec(tm, S5_WIDTH)), (p['s5_d'], _bspec(S5_WIDTH)),
             (dy5g, _rspec(tm, S5_WIDTH))],
        outs=[(_sds((t, S5_WIDTH), BF16), _rspec(tm, S5_WIDTH), False), (_sds((t, S5_WIDTH)), _rspec(tm, S5_WIDTH), False),
              (_sds((1, S5_WIDTH)), _bspec(S5_WIDTH), True)])
    du5, dbre, dbim, dcre, dcim, dlr, dli = _s5_bwd(dy5, du5_direct, s['u5'], s['sr'], s['si'], s['bre'], s['bim'],
                                                     s['cre'], s['cim'], s['lr'], s['li'], name="s5_scan_bwd")
    du_p = _mm(du5, p['w_u5'], tb=True, name="mix_in_s5_dx")
    dw_u5 = _mm(s['u_p'], du5, ta=True, name="mix_in_s5_dw", out_dtype=BF16)
    ext_b = lambda m: _blockdiag_extract(m, S5_GROUP, S5_STATE).transpose(0, 2, 1).reshape(S5_GROUPS, S5_STATE * S5_GROUP)
    dlam_re, dlam_im, dlog_step, db_re, db_im = _s5_params_bwd(
        p['lam_re'], p['lam_im'], p['log_step'], p['b_re'], p['b_im'],
        (dlr.reshape(S5_GROUPS, S5_STATE), dli.reshape(S5_GROUPS, S5_STATE), ext_b(dbre), ext_b(dbim)), name="s5_par_bwd")
    dc_re = _blockdiag_extract(dcre, S5_STATE, S5_GROUP).transpose(0, 2, 1)
    dc_im = _blockdiag_extract(dcim, S5_STATE, S5_GROUP).transpose(0, 2, 1)
    dproj = jnp.concatenate([dz, dxbc, dgab], axis=1)
    du_big = _mm(dproj, p['w_big'], tb=True, name="mix_in_dx")
    du_dt = _mm(ddtr, p['w_dt'], tb=True, name="mix_in_dt_dx")
    dw_big = _mm(s['u'], dproj, ta=True, name="mix_in_dw", out_dtype=BF16)
    dw_dt = _mm(s['u'], ddtr, ta=True, name="mix_in_dt_dw", out_dtype=BF16)
    dh_in, dpre = _rms_bwd(s['h'], p['pre_g'], dh, [du_big, du_dt, _unperm(du_p)], name="mix_rms_bwd", tm=tm)
    dw_in = jnp.concatenate([dw_big[:, :GA0], dw_dt, dw_u5, dw_big[:, GA0:]], axis=1)
    bufs['w_in'] = dw_in.reshape(D_MODEL, N_DEV, -1).transpose(1, 0, 2)[:, None]
    grads = {
        'mix_pre_g': dpre, 'mix_post_g': dpost, 'ssd_conv_w': dconv_w, 'ssd_conv_b': dconv_b,
        'ssd_dt_bias': ddtb[:, 0, :8].reshape(-1), 'ssd_a_log': dalog[:, 0, :8].reshape(-1),
        'ssd_d': ddsk[:, 0, :8].reshape(-1), 'ssd_norm_g': dnorm,
        's5_lambda_re': dlam_re, 's5_lambda_im': dlam_im,
        's5_b_re': db_re.reshape(S5_GROUPS, S5_STATE, S5_GROUP), 's5_b_im': db_im.reshape(S5_GROUPS, S5_STATE, S5_GROUP),
        's5_c_re': dc_re, 's5_c_im': dc_im, 's5_log_step': dlog_step.reshape(-1), 's5_d': ds5d,
    }
    return dh_in, bufs, grads


HBM_SPEC = pl.BlockSpec(memory_space=pltpu.HBM)


def _place():
    return lax.axis_index("x"), lax.axis_index("y"), lax.axis_index("c")


GATHER_COLLECTIVE_ID = 1


def _all_gather(shards, *, name, on_sequencer=False):
    n = len(shards)

    def body(*refs):
        x_refs, out_refs = refs[:n], refs[n:2 * n]
        send_sems, recv_sems, local_sems = refs[2 * n:]
        x, y, c = _place()
        me, sibling = (x, y, c), (x, y, 1 - c)
        chips = [(1 - x, y), (x, 1 - y), (1 - x, 1 - y)]
        if on_sequencer:
            _handshake([sibling] + [(*chip, c) for chip in chips])

        def slot(o, px, py, pc):
            return out_refs[o].at[4 * px + 2 * py + pc]

        def copy(o, k, block, to, src=None):
            return pltpu.make_async_remote_copy(
                src_ref=slot(o, *block) if src is None else src, dst_ref=slot(o, *block),
                send_sem=send_sems.at[7 * o + k], recv_sem=recv_sems.at[7 * o + k], device_id=to, device_id_type=MESH)

        mine = [pltpu.make_async_copy(x_refs[o], slot(o, *me), local_sems.at[o]) for o in range(n)]
        for cp in mine:
            cp.start()
        first = []
        for j, chip in enumerate(chips):
            first += [copy(o, 1 + j, me, (*chip, c), src=x_refs[o]) for o in range(n)]
        first += [copy(o, 0, me, sibling, src=x_refs[o]) for o in range(n)]
        for cp in first:
            cp.start()
        passed = []
        for j, chip in enumerate(chips):
            for o in range(n):
                copy(o, 1 + j, (*chip, c), me).wait_recv()
                passed.append(copy(o, 4 + j, (*chip, c), sibling))
                passed[-1].start()
        for o in range(n):
            copy(o, 0, sibling, me).wait_recv()
        for j, chip in enumerate(chips):
            for o in range(n):
                copy(o, 4 + j, (*chip, 1 - c), me).wait_recv()
        for cp in first + passed:
            cp.wait_send()
        for cp in mine:
            cp.wait()

    out_shape = [jax.ShapeDtypeStruct((N_DEV,) + s.shape, s.dtype) for s in shards]
    sems = [pltpu.SemaphoreType.DMA((7 * n,)), pltpu.SemaphoreType.DMA((7 * n,)), pltpu.SemaphoreType.DMA((n,))]
    if on_sequencer:
        return _scall(body, name=name, out_type=out_shape, scratch_types=sems, collective_id=GATHER_COLLECTIVE_ID)(*shards)
    return _pcall(body, name=name, in_specs=[HBM_SPEC] * n, out_specs=[HBM_SPEC] * n, out_shape=out_shape,
                  scratch_shapes=sems)(*shards)


N_CHIPS = 4


SIBLING_COLLECTIVE_ID = 2
CHIPS_COLLECTIVE_ID = 3


def _handshake(peers):
    barrier = pltpu.get_barrier_semaphore()
    for peer in peers:
        pl.semaphore_signal(barrier, inc=1, device_id=peer, device_id_type=MESH)
    pl.semaphore_wait(barrier, len(peers))


def _exchange_sibling(grads, *, name):
    n = len(grads)

    def body(*refs):
        p_refs, q_refs = refs[:n], refs[n:2 * n]
        send_sems, recv_sems = refs[2 * n:]
        x, y, c = _place()
        _handshake([(x, y, 1 - c)])
        copies = [pltpu.make_async_remote_copy(
            src_ref=p_refs[o].at[k, 1 - c], dst_ref=q_refs[o].at[k], send_sem=send_sems.at[N_CHIPS * o + k],
            recv_sem=recv_sems.at[N_CHIPS * o + k], device_id=(x, y, 1 - c), device_id_type=MESH)
            for o in range(n) for k in range(N_CHIPS)]
        for cp in copies:
            cp.start()
        for cp in copies:
            cp.wait()

    return _scall(
        body, name=name, out_type=[jax.ShapeDtypeStruct((N_CHIPS,) + g.shape[2:], g.dtype) for g in grads],
        scratch_types=[pltpu.SemaphoreType.DMA((N_CHIPS * n,)), pltpu.SemaphoreType.DMA((N_CHIPS * n,))],
        collective_id=SIBLING_COLLECTIVE_ID,
    )(*grads)


def _pair_sum(own, got, *, name):
    _, _, r, l = own.shape
    tr = _tile(r, 512, 16)
    c = lax.axis_index("c").astype(jnp.int32).reshape(1)

    def body(c_ref, p_ref, q_ref, o_ref):
        o_ref[...] = (p_ref[...].astype(F32) + q_ref[...].astype(F32)).astype(o_ref.dtype)

    return _pcall(
        body, name=name,
        grid_spec=pltpu.PrefetchScalarGridSpec(
            num_scalar_prefetch=1, grid=(N_CHIPS, r // tr),
            in_specs=[pl.BlockSpec((None, None, tr, l), lambda k, i, cr: (k, cr[0], i, 0)),
                      pl.BlockSpec((None, tr, l), lambda k, i, cr: (k, i, 0))],
            out_specs=pl.BlockSpec((None, tr, l), lambda k, i, cr: (k, i, 0))),
        out_shape=jax.ShapeDtypeStruct((N_CHIPS, r, l), own.dtype),
        compiler_params=_params("parallel", "parallel"),
    )(c, own, got)


def _exchange_chips(parts, *, name):
    n = len(parts)

    def body(*refs):
        p_refs, g_refs = refs[:n], refs[n:2 * n]
        send_sems, recv_sems, local_sems = refs[2 * n:]
        x, y, c = _place()
        mine = 2 * x + y
        chips = [(1 - x, y), (x, 1 - y), (1 - x, 1 - y)]
        _handshake([(*chip, c) for chip in chips])
        own = [pltpu.make_async_copy(p_refs[o].at[mine], g_refs[o].at[mine], local_sems.at[o]) for o in range(n)]
        for cp in own:
            cp.start()
        copies = []
        for j, (px, py) in enumerate(chips):
            copies += [pltpu.make_async_remote_copy(
                src_ref=p_refs[o].at[2 * px + py], dst_ref=g_refs[o].at[mine], send_sem=send_sems.at[3 * o + j],
                recv_sem=recv_sems.at[3 * o + j], device_id=(px, py, c), device_id_type=MESH) for o in range(n)]
        for cp in copies:
            cp.start()
        for cp in copies:
            cp.wait()
        for cp in own:
            cp.wait()

    return _scall(
        body, name=name, out_type=[jax.ShapeDtypeStruct(p.shape, p.dtype) for p in parts],
        scratch_types=[pltpu.SemaphoreType.DMA((3 * n,)), pltpu.SemaphoreType.DMA((3 * n,)), pltpu.SemaphoreType.DMA((n,))],
        collective_id=CHIPS_COLLECTIVE_ID,
    )(*parts)


def _sum_slots(g, *, name):
    n, r, l = g.shape
    tr = _tile(r, 512, 16)

    def body(g_ref, o_ref):
        acc = g_ref[0].astype(F32)
        for k in range(1, n):
            acc = acc + g_ref[k].astype(F32)
        o_ref[...] = acc

    return _pcall(
        body, name=name, grid=(r // tr,), in_specs=[pl.BlockSpec((n, tr, l), lambda i: (0, i, 0))],
        out_specs=pl.BlockSpec((tr, l), lambda i: (i, 0)), out_shape=_sds((r, l)),
        compiler_params=_params("parallel"),
    )(g)


TRANSPOSED = ('ffn1_w_gate', 'ffn1_w_up', 'ffn2_w_gate', 'ffn2_w_up')
SUBLAYERS = (('ffn1', ['ffn1_w_gate', 'ffn1_w_up', 'ffn1_w_down']),
             ('mix', ['w_in', 'ssd_conv_w', 'w_branch_a', 's5_w_glu', 'w_branch_b', 'w_out']),
             ('ffn2', ['ffn2_w_gate', 'ffn2_w_up', 'ffn2_w_down']))


def _gather_weights(w):
    layers, first = [], None
    for i in range(DEPTH):
        g = {}
        for tag, names in SUBLAYERS:
            shards = [w[n][i:i + 1] if n == 'ssd_conv_w' else
                      (w[n][i:i + 1].transpose(0, 2, 1) if n in TRANSPOSED else w[n][i:i + 1]).astype(BF16) for n in names]
            if first is None:
                first = got = _all_gather(shards, name=f"gather_{tag}")
            else:
                shards, first = lax.optimization_barrier((shards, first))
                got = _all_gather(shards, name=f"gather_{tag}", on_sequencer=True)
            g.update(zip(names, got))
        layers.append(g)
    layers[0].update(zip(SUBLAYERS[0][1], first))
    return layers


class _ReduceScatter:
    @staticmethod
    def sibling(tag, bufs):
        names = list(bufs)
        own = [bufs[n].reshape((N_CHIPS, 2) + bufs[n].shape[1:]) for n in names]
        return (tag, names), (own, _exchange_sibling(own, name=f"reduce_sibling_{tag}"))

    @staticmethod
    def chips(meta, arrays):
        (tag, names), (own, got) = meta, arrays
        flat = lambda a, lead: a.reshape(lead + (-1, a.shape[-1]))
        parts = [_pair_sum(flat(o, (N_CHIPS, 2)), flat(g, (N_CHIPS,)), name=f"reduce_pair_sum_{n}").reshape(g.shape)
                 for n, o, g in zip(names, own, got)]
        return names, _exchange_chips(parts, name=f"reduce_chips_{tag}")

    @staticmethod
    def done(names, slots):
        return dict(zip(names, slots))

    @staticmethod
    def small(grads):
        return _reduce_small(grads)


def _reduce_small(grads):
    flat = jnp.concatenate([g.astype(F32).reshape(-1) for g in grads.values()])
    pad = (-flat.shape[0]) % (8 * LANES)
    flat = jnp.concatenate([flat, jnp.zeros((pad,), F32)]).reshape(-1, LANES)
    gathered = _all_gather([flat], name="gather_small_grads", on_sequencer=True)[0]
    total = _sum_slots(gathered, name="sum_small_grads").reshape(-1)
    out, o = {}, 0
    for n, g in grads.items():
        out[n] = total[o:o + g.size].reshape(g.shape)
        o += g.size
    return out


def _adamw(w, g, m, v, *, name, slots=False):
    shape = w.shape
    if slots:
        lyr, rows, lanes = shape
        w2, m2, v2 = w, m, v
        tr = _tile(rows, 256, 16)
        nrt = rows // tr
        grid = (lyr, nrt)
        spec = pl.BlockSpec((None, tr, lanes), lambda l, i: (l, i, 0))
        g_specs = [pl.BlockSpec((N_CHIPS, None, tr, lanes),
                                lambda l, i, k=k: (0, 0, jnp.where(l == k, i, jnp.where(l > k, nrt - 1, 0)), 0))
                   for k in range(lyr)]
        g_args = list(g)
        out_shape = [_sds(shape)] * 4
    else:
        lanes = shape[-1] if (shape[-1] >= 128 or w.size % LANES) else LANES
        as2d = lambda a: a.reshape(-1, lanes)
        w2, m2, v2 = as2d(w), as2d(m), as2d(v)
        r = w2.shape[0]
        tr = _tile(r, 256, 8)
        grid = (1, r // tr)
        spec = pl.BlockSpec((tr, lanes), lambda l, i: (i, 0))
        g_specs, g_args = [spec], [as2d(g)]
        out_shape = [_sds((r, lanes))] * 4
    n_g = len(g_args)

    def body(w_ref, *rest):
        g_refs = rest[:n_g]
        m_ref, v_ref, go_ref, d_ref, mo_ref, vo_ref = rest[n_g:]
        if slots:
            gg = None
            for k, g_ref in enumerate(g_refs):
                tot = g_ref[0].astype(F32)
                for c in range(1, N_CHIPS):
                    tot = tot + g_ref[c].astype(F32)
                gg = tot if gg is None else jnp.where(pl.program_id(0) == k, tot, gg)
        else:
            gg = g_refs[0][...]
        go_ref[...] = gg
        mn = ADAM_B1 * m_ref[...] + (1.0 - ADAM_B1) * gg
        vn = ADAM_B2 * v_ref[...] + (1.0 - ADAM_B2) * (gg * gg)
        m_hat = mn / (1.0 - ADAM_B1 ** ADAM_STEP)
        v_hat = vn / (1.0 - ADAM_B2 ** ADAM_STEP)
        d_ref[...] = -ADAM_LR * (m_hat / (jnp.sqrt(v_hat) + ADAM_EPS) + ADAM_WD * w_ref[...])
        mo_ref[...] = mn
        vo_ref[...] = vn

    res = _pcall(
        body, name=name, grid=grid, in_specs=[spec] + g_specs + [spec, spec], out_specs=[spec] * 4,
        out_shape=out_shape, compiler_params=_params("arbitrary", "arbitrary"),
    )(w2, *g_args, m2, v2)
    return tuple(a.reshape(shape) for a in res)


def _sublayer_params(w, g, i, k):
    row = lambda a: a.astype(F32).reshape(1, -1)
    if k != 'mix':
        return dict(layer=i, pre_g=row(w[f'{k}_pre_g'][i]), post_g=row(w[f'{k}_post_g'][i]),
                    w_gate=g[f'{k}_w_gate'], w_up=g[f'{k}_w_up'], w_down=g[f'{k}_w_down'])
    head8 = lambda a: jnp.broadcast_to(
        jnp.pad(a.astype(F32).reshape(SSD_GROUPS, 1, 8), ((0, 0), (0, 0), (0, PAD_HEADS - 8))), (SSD_GROUPS, 8, PAD_HEADS))
    by_rows = lambda n: g[n].reshape(-1, g[n].shape[-1])
    by_cols = lambda n: g[n][:, 0].transpose(1, 0, 2).reshape(g[n].shape[2], -1)
    w_in = by_cols('w_in')
    s = np.cumsum([SSD_D_INNER, SSD_CONV_DIM, SSD_HEADS, S5_WIDTH, D_MODEL])
    return dict(
        layer=i, pre_g=row(w['mix_pre_g'][i]), post_g=row(w['mix_post_g'][i]),
        w_big=jnp.concatenate([w_in[:, :s[1]], w_in[:, s[3]:]], axis=1), w_dt=w_in[:, s[1]:s[2]], w_u5=w_in[:, s[2]:s[3]],
        conv_w=by_cols('ssd_conv_w'), conv_b=row(w['ssd_conv_b'][i]),
        dt_bias8=head8(w['ssd_dt_bias'][i]), a_log8=head8(w['ssd_a_log'][i]), d8=head8(w['ssd_d'][i]),
        norm_g=row(w['ssd_norm_g'][i]), w_a=by_rows('w_branch_a'),
        lam_re=w['s5_lambda_re'][i], lam_im=w['s5_lambda_im'][i], log_step=w['s5_log_step'][i].reshape(S5_GROUPS, 1),
        b_re=w['s5_b_re'][i].reshape(S5_GROUPS, -1), b_im=w['s5_b_im'][i].reshape(S5_GROUPS, -1),
        c_re=w['s5_c_re'][i], c_im=w['s5_c_im'][i], s5_d=row(w['s5_d'][i]),
        w_glu=by_cols('s5_w_glu'), w_b=by_rows('w_branch_b'), w_out=by_rows('w_out'))


def _loss_head(h, target, *, tm):
    t, d = h.shape

    def fn(y, tgt):
        err = y - tgt
        return err * (1.0 / d), jnp.sum(0.5 * jnp.sum(err * err, axis=-1, keepdims=True) * (1.0 / d), axis=0, keepdims=True)

    dy, loss = _rows(fn, name="loss_head", nrow=t // tm,
                     ins=[(h, _rspec(tm, d)), (target, _rspec(tm, d))],
                     outs=[(_sds((t, d)), _rspec(tm, d), False), (_sds((1, 128)), _bspec(128), True)])
    return dy, loss[0, 0]


def _forward_backward(h, target, w, g, rs):
    t = h.shape[0]
    tm = _tile(t, 256, 8)
    layers, saved = [], []
    for i in range(DEPTH):
        gi, ps, ss = dict(g[i]), [], []
        for tag, names in SUBLAYERS:
            tied, h = lax.optimization_barrier(([gi[n] for n in names], h))
            gi.update(zip(names, tied))
            p = _sublayer_params(w, gi, i, tag)
            h, s = _mixer_fwd(h, p, tm) if tag == 'mix' else _ffn_fwd(h, p, tag, tm)
            ps.append(p)
            ss.append(s)
        layers.append(ps)
        saved.append(ss)
    dh, loss = _loss_head(h, target, tm=tm)
    reduced, small = [{} for _ in range(DEPTH)], [{} for _ in range(DEPTH)]
    in_sibling, in_chips = None, None

    def start_chips(x):
        nonlocal in_sibling, in_chips
        if in_sibling is not None:
            layer, meta, arrays = in_sibling
            arrays, x = lax.optimization_barrier((arrays, x))
            in_sibling, in_chips = None, (layer,) + tuple(rs.chips(meta, arrays))
        return x

    def finish_chips(x):
        nonlocal in_chips
        if in_chips is not None:
            layer, names, slots = in_chips
            slots, x = lax.optimization_barrier((slots, x))
            reduced[layer].update(rs.done(names, slots))
            in_chips = None
        return x

    for i in reversed(range(DEPTH)):
        for k in reversed(range(len(SUBLAYERS))):
            tag = SUBLAYERS[k][0]
            if tag == 'mix':
                dh, bufs, grads = _mixer_bwd(dh, layers[i][k], saved[i][k], tm, start_chips)
            else:
                dh, bufs, grads = _ffn_bwd(dh, layers[i][k], saved[i][k], tag, tm, start_chips)
            small[i].update(grads)
            dh = finish_chips(dh)
            in_sibling = (i,) + tuple(rs.sibling(tag, bufs))
            if tag == 'mix' and i + 1 < DEPTH:
                small[i + 1], dh = lax.optimization_barrier((small[i + 1], dh))
        if i == 0:
            small[i]['loss'] = loss.reshape(1)
        small[i] = rs.small(small[i])
    loss = small[0].pop('loss')[0]
    dh = finish_chips(start_chips(dh))
    shapes = {n: (w[n].shape[:-1] + (SSD_CONV_DIM,) if n == 'ssd_conv_w' else w[n].shape) for n in SMALL_ORDER}
    stacked = {n: jnp.stack([small[i][n].reshape(shapes[n][1:]) for i in range(DEPTH)]) for n in SMALL_ORDER}
    return loss, dh, reduced, stacked


def kernel(*args):
    n_w = len(WEIGHTS)
    x, target = args[0], args[1 + n_w]
    w = dict(zip(WEIGHTS, args[1:1 + n_w]))
    m = dict(zip(WEIGHTS, args[2 + n_w:2 + 2 * n_w]))
    v = dict(zip(WEIGHTS, args[2 + 2 * n_w:2 + 3 * n_w]))
    t = x.shape[1]

    g = _gather_weights(w)
    loss, dx, slots, small = _forward_backward(x.reshape(t, D_MODEL), target.reshape(t, D_MODEL), w, g, _ReduceScatter)
    me = 4 * lax.axis_index("x") + 2 * lax.axis_index("y") + lax.axis_index("c")
    cols = w['ssd_conv_w'].shape[-1]
    small['ssd_conv_w'] = lax.dynamic_slice_in_dim(small['ssd_conv_w'], me * cols, cols, axis=2)

    grad, delta, new_m, new_v = {}, {}, {}, {}
    for n in WEIGHTS:
        sharded = n in slots[0]
        view = (lambda a: a.transpose(0, 2, 1)) if n in TRANSPOSED else (lambda a: a)
        res = _adamw(view(w[n]), [slots[i][n] for i in range(DEPTH)] if sharded else small[n], view(m[n]), view(v[n]),
                     name=f"adamw_{n}", slots=sharded)
        grad[n], delta[n], new_m[n], new_v[n] = (view(a) for a in res)
    return (loss, dx.reshape(x.shape), *[grad[n] for n in WEIGHTS], *[delta[n] for n in WEIGHTS],
            *[new_m[n] for n in WEIGHTS], *[new_v[n] for n in WEIGHTS])
```

```python
import functools
import math

import numpy as np
import jax
import jax.numpy as jnp
from jax import lax
from jax.experimental import pallas as pl
from jax.experimental.pallas import tpu as pltpu
from jax.experimental.pallas import tpu_sc as plsc

F32 = jnp.float32
BF16 = jnp.bfloat16
MESH = pl.DeviceIdType.MESH
HIGHEST = lax.Precision.HIGHEST

D_MODEL = 1024
DEPTH = 2
FFN_HIDDEN = 2816
SSD_D_INNER = 2048
SSD_HEADS = 32
SSD_HEAD_DIM = 64
SSD_GROUPS = 4
SSD_STATE = 128
SSD_CHUNK = 128
SSD_CONV_DIM = 3072
SSD_CONV_WIDTH = 4
S5_WIDTH = 1024
S5_GROUP = 16
S5_GROUPS = 64
S5_STATE = 64
S5_MAX_REAL = -1e-4
S5_BLOCKS = 8
RMS_EPS = 1e-6
N_DEV = 8
LANES = 1024

ADAM_LR = 0.001
ADAM_B1 = 0.9
ADAM_B2 = 0.999
ADAM_EPS = 1e-08
ADAM_WD = 0.01
ADAM_STEP = 10

VMEM_LIMIT_BYTES = 48 * 1024 * 1024

WEIGHTS = ['ffn1_pre_g', 'ffn1_post_g', 'ffn1_w_gate', 'ffn1_w_up', 'ffn1_w_down', 'mix_pre_g', 'mix_post_g',
           'w_in', 'ssd_conv_w', 'ssd_conv_b', 'ssd_dt_bias', 'ssd_a_log', 'ssd_d', 'ssd_norm_g', 'w_branch_a',
           's5_lambda_re', 's5_lambda_im', 's5_b_re', 's5_b_im', 's5_c_re', 's5_c_im', 's5_log_step', 's5_d',
           's5_w_glu', 'w_branch_b', 'w_out', 'ffn2_pre_g', 'ffn2_post_g', 'ffn2_w_gate', 'ffn2_w_up',
           'ffn2_w_down']
SHARDED = {'ffn1_w_gate': 2, 'ffn1_w_up': 2, 'ffn1_w_down': 1, 'w_in': 2, 'ssd_conv_w': 2, 'w_branch_a': 1,
           's5_w_glu': 2, 'w_branch_b': 1, 'w_out': 1, 'ffn2_w_gate': 2, 'ffn2_w_up': 2, 'ffn2_w_down': 1}
SHARDED_ORDER = [n for n in WEIGHTS if n in SHARDED]
SMALL_ORDER = [n for n in WEIGHTS if n not in SHARDED or n == 'ssd_conv_w']


def _pcall(body, **kw):
    return pl.pallas_call(body, **kw)


def _scall(body, *, name, out_type, scratch_types, collective_id):
    return pl.kernel(body, out_type=out_type, mesh=plsc.ScalarSubcoreMesh(axis_name="sequencer", num_cores=1),
                     scratch_types=scratch_types, name=name,
                     compiler_params=pltpu.CompilerParams(collective_id=collective_id))


def _params(*sem):
    return pltpu.CompilerParams(dimension_semantics=sem, vmem_limit_bytes=VMEM_LIMIT_BYTES)


def _tile(n, pref, align=128):
    if n <= pref:
        return n
    t = (pref // align) * align
    while t >= align:
        if n % t == 0:
            return t
        t -= align
    return n


def _rms(x, g):
    return x * lax.rsqrt(jnp.mean(x * x, axis=-1, keepdims=True) + RMS_EPS) * g


def _sigmoid(x):
    return 1.0 / (1.0 + jnp.exp(-x))


def _silu(x):
    return x * _sigmoid(x)


def _gelu(x):
    return 0.5 * x * (1.0 + jnp.tanh(math.sqrt(2.0 / math.pi) * (x + 0.044715 * (x * x * x))))


def _softplus(x):
    return jnp.maximum(x, 0.0) + jnp.log(1.0 + jnp.exp(-jnp.abs(x)))


def _dot(a, b, dims):
    return lax.dot_general(a.astype(BF16), b.astype(BF16), (dims, ((), ())), preferred_element_type=F32)


_NN = ((1,), (0,))
_NT = ((1,), (1,))
_TN = ((0,), (0,))


@jax.custom_vjp
def _bdot_nn(a, b):
    return _dot(a, b, _NN)


_bdot_nn.defvjp(lambda a, b: (_dot(a, b, _NN), (a, b)),
                lambda r, g: (_dot(g, r[1], _NT), _dot(r[0], g, _TN)))


@jax.custom_vjp
def _bdot_nt(a, b):
    return _dot(a, b, _NT)


_bdot_nt.defvjp(lambda a, b: (_dot(a, b, _NT), (a, b)),
                lambda r, g: (_dot(g, r[1], _NN), _dot(g, r[0], _TN)))


@jax.custom_vjp
def _bdot_tn(a, b):
    return _dot(a, b, _TN)


_bdot_tn.defvjp(lambda a, b: (_dot(a, b, _TN), (a, b)),
                lambda r, g: (_dot(r[1], g, _NT), _dot(r[0], g, _NN)))


def _fdot(a, b, dims=_NN):
    return lax.dot_general(a, b, (dims, ((), ())), precision=HIGHEST, preferred_element_type=F32)


def _sel3(x, sel, dims, x_first):
    p1 = x.astype(BF16)
    r1 = x - p1.astype(F32)
    p2 = r1.astype(BF16)
    p3 = (r1 - p2.astype(F32)).astype(BF16)
    sel = sel.astype(BF16)
    out = None
    for piece in (p1, p2, p3):
        d = lax.dot_general(*((piece, sel) if x_first else (sel, piece)), (dims, ((), ())), preferred_element_type=F32)
        out = d if out is None else out + d
    return out


@jax.custom_vjp
def _sel_right(x, sel):
    return _sel3(x, sel, _NN, True)


_sel_right.defvjp(lambda x, sel: (_sel3(x, sel, _NN, True), sel),
                  lambda sel, g: (_sel3(g, sel, _NT, True), jnp.zeros_like(sel)))


@jax.custom_vjp
def _sel_left(sel, x):
    return _sel3(x, sel, _NN, False)


_sel_left.defvjp(lambda sel, x: (_sel3(x, sel, _NN, False), sel),
                 lambda sel, g: (jnp.zeros_like(sel), _sel3(g, sel, _TN, False)))


@jax.custom_vjp
def _sel_left_nt(sel, x):
    return _sel3(x, sel, _NT, False)


_sel_left_nt.defvjp(lambda sel, x: (_sel3(x, sel, _NT, False), sel),
                    lambda sel, g: (jnp.zeros_like(sel), _sel3(g, sel, _TN, True)))


def _mm(a, b, *, name, ta=False, tb=False, out_dtype=F32, tm=2048, tn=512, tk=2048, shards=None):
    m, k = (a.shape[1], a.shape[0]) if ta else a.shape
    n = b.shape[0] if tb else b.shape[1]
    assert k == (b.shape[1] if tb else b.shape[0]), (a.shape, b.shape, ta, tb)
    if shards == 'rows':
        tm = min(tm, m // N_DEV)
    if shards == 'cols':
        tn = n // N_DEV
    tm, tn, tk = _tile(m, tm), _tile(n, tn), _tile(k, tk)
    nk = k // tk
    a_spec = pl.BlockSpec((tk, tm), lambda i, j, kk: (kk, i)) if ta else pl.BlockSpec((tm, tk), lambda i, j, kk: (i, kk))
    b_spec = pl.BlockSpec((tn, tk), lambda i, j, kk: (j, kk)) if tb else pl.BlockSpec((tk, tn), lambda i, j, kk: (kk, j))
    dims = ((0 if ta else 1,), (1 if tb else 0,))
    out_spec = pl.BlockSpec((tm, tn), lambda i, j, kk: (i, j))
    out_shape = jax.ShapeDtypeStruct((m, n), out_dtype)
    if shards == 'rows':
        per = m // N_DEV // tm
        out_shape = jax.ShapeDtypeStruct((N_DEV, 1, m // N_DEV, n), out_dtype)
        out_spec = pl.BlockSpec((None, None, tm, tn), lambda i, j, kk: (i // per, 0, i % per, j))
    elif shards == 'cols':
        out_shape = jax.ShapeDtypeStruct((N_DEV, 1, m, n // N_DEV), out_dtype)
        out_spec = pl.BlockSpec((None, None, tm, tn), lambda i, j, kk: (j, 0, i, 0))

    def body(a_ref, b_ref, o_ref, acc_ref):
        kk = pl.program_id(2)

        @pl.when(kk == 0)
        def _():
            acc_ref[...] = jnp.zeros_like(acc_ref)

        acc_ref[...] += _dot(a_ref[...], b_ref[...], dims)

        @pl.when(kk == nk - 1)
        def _():
            o_ref[...] = acc_ref[...].astype(o_ref.dtype)

    return _pcall(
        body, name=name, grid=(m // tm, n // tn, nk),
        in_specs=[a_spec, b_spec], out_specs=out_spec, out_shape=out_shape,
        scratch_shapes=[pltpu.VMEM((tm, tn), F32)],
        compiler_params=_params("parallel", "parallel", "arbitrary"),
    )(a, b)


def _rspec(tm, w, cb=0, percol=False):
    return pl.BlockSpec((tm, w), (lambda j, i: (i, cb + j)) if percol else (lambda j, i: (i, cb)))


def _bspec(w, cb=0, percol=False, rows=1):
    return pl.BlockSpec((rows, w), (lambda j, i: (0, cb + j)) if percol else (lambda j, i: (0, cb)))


def _rows(fn, *, name, nrow, ncol=1, ins, outs):
    n_in = len(ins)
    accs = [o[2] for o in outs]

    def body(*refs):
        vals = fn(*[r[...] for r in refs[:n_in]])
        if not isinstance(vals, (tuple, list)):
            vals = (vals,)
        i = pl.program_id(1)
        for ref, val, acc in zip(refs[n_in:], vals, accs):
            if acc:
                @pl.when(i == 0)
                def _(ref=ref):
                    ref[...] = jnp.zeros_like(ref)

                ref[...] += jnp.broadcast_to(val, ref.shape).astype(ref.dtype)
            else:
                ref[...] = val.astype(ref.dtype)

    res = _pcall(
        body, name=name, grid=(ncol, nrow),
        in_specs=[s for _, s in ins], out_specs=[o[1] for o in outs], out_shape=[o[0] for o in outs],
        compiler_params=_params("parallel", "arbitrary"),
    )(*[a for a, _ in ins])
    return res


def _sds(shape, dtype=F32):
    return jax.ShapeDtypeStruct(shape, dtype)


def _rms_fwd(h, g, *, name, tm):
    t, d = h.shape
    return _rows(lambda x, gg: _rms(x, gg), name=name, nrow=t // tm,
                 ins=[(h, _rspec(tm, d)), (g, _bspec(d))],
                 outs=[(_sds((t, d), BF16), _rspec(tm, d), False)])[0]


def _resid_fwd(h, f, g, scale, *, name, tm):
    t, d = h.shape
    return _rows(lambda x, ff, gg: x + scale * _rms(ff, gg), name=name, nrow=t // tm,
                 ins=[(h, _rspec(tm, d)), (f, _rspec(tm, d)), (g, _bspec(d))],
                 outs=[(_sds((t, d)), _rspec(tm, d), False)])[0]


def _resid_bwd(f, g, dh, scale, *, name, tm):
    t, d = f.shape

    def fn(ff, gg, dd):
        _, vjp = jax.vjp(lambda a, b: scale * _rms(a, b), ff, gg)
        return vjp(dd)

    return _rows(fn, name=name, nrow=t // tm,
                 ins=[(f, _rspec(tm, d)), (g, _bspec(d)), (dh, _rspec(tm, d))],
                 outs=[(_sds((t, d), BF16), _rspec(tm, d), False), (_sds((1, d)), _bspec(d), True)])


def _rms_bwd(h, g, dh, dxns, *, name, tm):
    t, d = h.shape

    def fn(x, gg, dd, *dx):
        _, vjp = jax.vjp(_rms, x, gg)
        tot = dx[0]
        for more in dx[1:]:
            tot = tot + more
        dxx, dg = vjp(tot)
        return dd + dxx, dg

    return _rows(fn, name=name, nrow=t // tm,
                 ins=[(h, _rspec(tm, d)), (g, _bspec(d)), (dh, _rspec(tm, d))] + [(x, _rspec(tm, d)) for x in dxns],
                 outs=[(_sds((t, d)), _rspec(tm, d), False), (_sds((1, d)), _bspec(d), True)])


NB = FFN_HIDDEN // N_DEV
MM_ROWS = 2048


def _ffn_up(xn, wg, wu, *, name):
    t = xn.shape[0]
    tm = _tile(t, MM_ROWS)
    wspec = pl.BlockSpec((None, None, NB, D_MODEL), lambda i, j: (j, 0, 0, 0))

    def body(x_ref, g_ref, u_ref, ab_ref, hh_ref):
        x = x_ref[...]
        a, b = _dot(x, g_ref[...], _NT), _dot(x, u_ref[...], _NT)
        ab_ref[0] = a.astype(ab_ref.dtype)
        ab_ref[1] = b.astype(ab_ref.dtype)
        hh_ref[...] = (_silu(a) * b).astype(hh_ref.dtype)

    return _pcall(
        body, name=name, grid=(t // tm, N_DEV),
        in_specs=[pl.BlockSpec((tm, D_MODEL), lambda i, j: (i, 0)), wspec, wspec],
        out_specs=[pl.BlockSpec((None, 2, tm, NB), lambda i, j: (j, 0, i, 0)),
                   pl.BlockSpec((None, tm, NB), lambda i, j: (j, i, 0))],
        out_shape=[_sds((N_DEV, 2, t, NB), BF16), _sds((N_DEV, t, NB), BF16)],
        compiler_params=_params("parallel", "parallel"),
    )(xn, wg, wu)


def _ffn_down(hh, wd, *, name):
    t = hh.shape[1]
    tm = _tile(t, 512)

    def body(h_ref, w_ref, o_ref):
        acc = _dot(h_ref[0], w_ref[0, 0], _NN)
        for k in range(1, N_DEV):
            acc = acc + _dot(h_ref[k], w_ref[k, 0], _NN)
        o_ref[...] = acc

    return _pcall(
        body, name=name, grid=(t // tm,),
        in_specs=[pl.BlockSpec((N_DEV, tm, NB), lambda i: (0, i, 0)),
                  pl.BlockSpec((N_DEV, 1, NB, D_MODEL), lambda i: (0, 0, 0, 0))],
        out_specs=pl.BlockSpec((tm, D_MODEL), lambda i: (i, 0)), out_shape=_sds((t, D_MODEL)),
        compiler_params=_params("parallel"),
    )(hh, wd)


def _ffn_down_dx(df, wd, ab, *, name):
    t = df.shape[0]
    tm = _tile(t, MM_ROWS)

    def body(d_ref, w_ref, ab_ref, o_ref):
        dhh = _dot(d_ref[...], w_ref[...], _NT)
        _, vjp = jax.vjp(lambda a, b: _silu(a) * b, ab_ref[0].astype(F32), ab_ref[1].astype(F32))
        da, db = vjp(dhh)
        o_ref[0] = da.astype(o_ref.dtype)
        o_ref[1] = db.astype(o_ref.dtype)

    blk = pl.BlockSpec((None, 2, tm, NB), lambda i, j: (j, 0, i, 0))
    return _pcall(
        body, name=name, grid=(t // tm, N_DEV),
        in_specs=[pl.BlockSpec((tm, D_MODEL), lambda i, j: (i, 0)),
                  pl.BlockSpec((None, None, NB, D_MODEL), lambda i, j: (j, 0, 0, 0)), blk],
        out_specs=blk, out_shape=_sds((N_DEV, 2, t, NB), BF16), compiler_params=_params("parallel", "parallel"),
    )(df, wd, ab)


def _ffn_down_dw(hh, df, *, name, tn=512):
    t = df.shape[0]
    tk = _tile(t, 2048)
    nk = t // tk

    def body(h_ref, d_ref, o_ref, acc_ref):
        kk = pl.program_id(2)

        @pl.when(kk == 0)
        def _():
            acc_ref[...] = jnp.zeros_like(acc_ref)

        acc_ref[...] += _dot(h_ref[...], d_ref[...], _TN)

        @pl.when(kk == nk - 1)
        def _():
            o_ref[...] = acc_ref[...].astype(o_ref.dtype)

    return _pcall(
        body, name=name, grid=(N_DEV, D_MODEL // tn, nk),
        in_specs=[pl.BlockSpec((None, tk, NB), lambda j, n, kk: (j, kk, 0)),
                  pl.BlockSpec((tk, tn), lambda j, n, kk: (kk, n))],
        out_specs=pl.BlockSpec((None, None, NB, tn), lambda j, n, kk: (j, 0, 0, n)),
        out_shape=_sds((N_DEV, 1, NB, D_MODEL), BF16),
        scratch_shapes=[pltpu.VMEM((NB, tn), F32)],
        compiler_params=_params("parallel", "parallel", "arbitrary"),
    )(hh, df)


def _ffn_up_dx(dab, wg, wu, *, name):
    t = dab.shape[2]
    tm = _tile(t, MM_ROWS // 2)
    wspec = pl.BlockSpec((None, None, NB, D_MODEL), lambda i, j: (j, 0, 0, 0))

    def body(d_ref, g_ref, u_ref, o_ref):
        @pl.when(pl.program_id(1) == 0)
        def _():
            o_ref[...] = jnp.zeros_like(o_ref)

        o_ref[...] += _dot(d_ref[0], g_ref[...], _NN) + _dot(d_ref[1], u_ref[...], _NN)

    return _pcall(
        body, name=name, grid=(t // tm, N_DEV),
        in_specs=[pl.BlockSpec((None, 2, tm, NB), lambda i, j: (j, 0, i, 0)), wspec, wspec],
        out_specs=pl.BlockSpec((tm, D_MODEL), lambda i, j: (i, 0)), out_shape=_sds((t, D_MODEL)),
        compiler_params=_params("parallel", "arbitrary"),
    )(dab, wg, wu)


def _ffn_up_dw(xn, dab, *, name):
    t = xn.shape[0]

    def body(x_ref, d_ref, og_ref, ou_ref):
        x = x_ref[...]
        og_ref[...] = _dot(d_ref[0], x, _TN).astype(og_ref.dtype)
        ou_ref[...] = _dot(d_ref[1], x, _TN).astype(ou_ref.dtype)

    out = pl.BlockSpec((None, None, NB, D_MODEL), lambda j: (j, 0, 0, 0))
    return _pcall(
        body, name=name, grid=(N_DEV,),
        in_specs=[pl.BlockSpec((t, D_MODEL), lambda j: (0, 0)), pl.BlockSpec((None, 2, t, NB), lambda j: (j, 0, 0, 0))],
        out_specs=[out, out], out_shape=[_sds((N_DEV, 1, NB, D_MODEL), BF16)] * 2,
        compiler_params=_params("parallel"),
    )(xn, dab)


def _ffn_fwd(h, p, tag, tm):
    xn = _rms_fwd(h, p['pre_g'], name=f"{tag}_rms", tm=tm)
    ab, hh = _ffn_up(xn, p['w_gate'], p['w_up'], name=f"{tag}_up")
    w_down, hh = lax.optimization_barrier((p['w_down'], hh))
    f = _ffn_down(hh, w_down, name=f"{tag}_down")
    out = _resid_fwd(h, f, p['post_g'], 0.5, name=f"{tag}_res", tm=tm)
    return out, (h, xn, ab, hh, f)


def _ffn_bwd(dh, p, saved, tag, tm, after_first):
    h, xn, ab, hh, f = saved
    df, dpost = _resid_bwd(f, p['post_g'], dh, 0.5, name=f"{tag}_res_bwd", tm=tm)
    df = after_first(df)
    dab = _ffn_down_dx(df, p['w_down'], ab, name=f"{tag}_down_dx")
    bufs = {f'{tag}_w_down': _ffn_down_dw(hh, df, name=f"{tag}_down_dw")}
    dxn = _ffn_up_dx(dab, p['w_gate'], p['w_up'], name=f"{tag}_up_dx")
    bufs[f'{tag}_w_gate'], bufs[f'{tag}_w_up'] = _ffn_up_dw(xn, dab, name=f"{tag}_up_dw")
    dh_in, dpre = _rms_bwd(h, p['pre_g'], dh, [dxn], name=f"{tag}_rms_bwd", tm=tm)
    return dh_in, bufs, {f'{tag}_pre_g': dpre, f'{tag}_post_g': dpost}


CONV_COLS = 256


def _shift_down(x, s):
    rows = lax.broadcasted_iota(jnp.int32, x.shape, 0)
    return jnp.where(rows >= s, pltpu.roll(x, s, axis=0), 0.0)


def _shift_up(x, s):
    t = x.shape[0]
    rows = lax.broadcasted_iota(jnp.int32, x.shape, 0)
    return jnp.where(rows < t - s, pltpu.roll(x, t - s, axis=0), 0.0)


def _conv_fwd(proj, col0, w, b, *, name):
    t = proj.shape[0]
    c = w.shape[1]
    cb0 = col0 // CONV_COLS

    def body(x_ref, w_ref, b_ref, o_ref):
        x = x_ref[...]
        acc = x * w_ref[3:4, :] + b_ref[...]
        for k in range(SSD_CONV_WIDTH - 1):
            acc = acc + _shift_down(x, SSD_CONV_WIDTH - 1 - k) * w_ref[k:k + 1, :]
        o_ref[...] = _silu(acc)

    return _pcall(
        body, name=name, grid=(c // CONV_COLS,),
        in_specs=[pl.BlockSpec((t, CONV_COLS), lambda j: (0, cb0 + j)),
                  pl.BlockSpec((SSD_CONV_WIDTH, CONV_COLS), lambda j: (0, j)),
                  pl.BlockSpec((1, CONV_COLS), lambda j: (0, j))],
        out_specs=pl.BlockSpec((t, CONV_COLS), lambda j: (0, j)),
        out_shape=_sds((t, c)), compiler_params=_params("parallel"),
    )(proj, w, b)


def _conv_bwd(proj, col0, w, b, douts, *, name):
    t = proj.shape[0]
    c = w.shape[1]
    cb0 = col0 // CONV_COLS
    first = np.cumsum([0] + [d.shape[1] // CONV_COLS for d in douts])

    def body(x_ref, w_ref, b_ref, *rest):
        d_refs, (dx_ref, dw_ref, db_ref) = rest[:len(douts)], rest[len(douts):]
        j = pl.program_id(0)
        dout = d_refs[-1][...]
        for k in range(len(douts) - 2, -1, -1):
            dout = jnp.where(j < int(first[k + 1]), d_refs[k][...], dout)
        x = x_ref[...]
        shifted = [_shift_down(x, SSD_CONV_WIDTH - 1 - k) for k in range(SSD_CONV_WIDTH - 1)] + [x]
        pre = b_ref[...] + shifted[3] * w_ref[3:4, :]
        for k in range(SSD_CONV_WIDTH - 1):
            pre = pre + shifted[k] * w_ref[k:k + 1, :]
        sg = _sigmoid(pre)
        dpre = dout * (sg * (1.0 + pre * (1.0 - sg)))
        dx = dpre * w_ref[3:4, :]
        for k in range(SSD_CONV_WIDTH - 1):
            dx = dx + _shift_up(dpre, SSD_CONV_WIDTH - 1 - k) * w_ref[k:k + 1, :]
        dx_ref[...] = dx.astype(dx_ref.dtype)
        for k in range(SSD_CONV_WIDTH):
            dw_ref[k:k + 1, :] = jnp.sum(dpre * shifted[k], axis=0, keepdims=True)
        db_ref[...] = jnp.sum(dpre, axis=0, keepdims=True)

    return _pcall(
        body, name=name, grid=(c // CONV_COLS,),
        in_specs=[pl.BlockSpec((t, CONV_COLS), lambda j: (0, cb0 + j)),
                  pl.BlockSpec((SSD_CONV_WIDTH, CONV_COLS), lambda j: (0, j)),
                  pl.BlockSpec((1, CONV_COLS), lambda j: (0, j))] +
                 [pl.BlockSpec((t, CONV_COLS), lambda j, lo=int(first[k]), hi=int(first[k + 1]): (0, jnp.clip(j, lo, hi - 1) - lo))
                  for k in range(len(douts))],
        out_specs=[pl.BlockSpec((t, CONV_COLS), lambda j: (0, j)),
                   pl.BlockSpec((SSD_CONV_WIDTH, CONV_COLS), lambda j: (0, j)),
                   pl.BlockSpec((1, CONV_COLS), lambda j: (0, j))],
        out_shape=[_sds((t, c), BF16), _sds((SSD_CONV_WIDTH, c)), _sds((1, c))],
        compiler_params=_params("arbitrary"),
    )(proj, w, b, *douts)


HALF = 256
HEADS_PER_HALF = 4
PAD_HEADS = 128


def _head_expanders():
    k = lax.broadcasted_iota(jnp.int32, (PAD_HEADS, HALF), 0)
    j = lax.broadcasted_iota(jnp.int32, (PAD_HEADS, HALF), 1)
    kt = lax.broadcasted_iota(jnp.int32, (HALF, PAD_HEADS), 1)
    jt = lax.broadcasted_iota(jnp.int32, (HALF, PAD_HEADS), 0)
    es, ets = [], []
    for half in range(2):
        es.append(jnp.where(k == j // SSD_HEAD_DIM + half * HEADS_PER_HALF, 1.0, 0.0).astype(F32))
        ets.append(jnp.where(kt == jt // SSD_HEAD_DIM + half * HEADS_PER_HALF, 1.0, 0.0).astype(F32))
    return es, ets


def _ssd_chunk(x_lo, x_hi, bm, cm, dtr, dtb8, alog8, dsk8, s_lo, s_hi):
    q = x_lo.shape[0]
    es, ets = _head_expanders()
    rowmean = lambda v: jnp.sum(v, axis=0, keepdims=True) * 0.125
    dt = _softplus(dtr + rowmean(dtb8))
    a = -jnp.exp(rowmean(alog8))
    adt = a * dt
    adt_tot8 = jnp.broadcast_to(jnp.sum(adt, axis=0, keepdims=True), (8, PAD_HEADS))
    ll = lax.broadcasted_iota(jnp.int32, (q, q), 0)
    ss = lax.broadcasted_iota(jnp.int32, (q, q), 1)
    ltri = jnp.where(ll >= ss, 1.0, 0.0).astype(F32)
    lane = lax.broadcasted_iota(jnp.int32, (1, HALF), 1)
    cb = _bdot_nt(cm, bm)
    outs = []
    for half, (x, s_in) in enumerate(((x_lo, s_lo), (x_hi, s_hi))):
        e, et = es[half], ets[half]
        dtf = _sel_right(dt, e)
        af = rowmean(_sel_right(jnp.broadcast_to(a, (8, PAD_HEADS)), e)) * dtf
        dskf = rowmean(_sel_right(dsk8, e))
        acum = _sel_left(ltri, af)
        alast = jnp.sum(af, axis=0, keepdims=True)
        xdt = x * dtf
        ydiag = jnp.zeros((q, HALF), F32)
        for r in range(HEADS_PER_HALF):
            sel = lane == r * SSD_HEAD_DIM
            ac_r = jnp.sum(jnp.where(sel, acum, 0.0), axis=1, keepdims=True)
            a_r = jnp.sum(jnp.where(sel, af, 0.0), axis=1, keepdims=True)
            arow = jnp.sum(jnp.where(ll <= ss, a_r, 0.0), axis=0, keepdims=True)
            decay = jnp.exp(jnp.where(ll >= ss, ac_r - arow, -jnp.inf))
            yh = _bdot_nn(cb * decay, xdt)
            ydiag = ydiag + jnp.where(lane // SSD_HEAD_DIM == r, yh, 0.0)
        st = _bdot_tn(xdt * jnp.exp(alast - acum), bm)
        yoff = _bdot_nt(cm, s_in) * jnp.exp(acum)
        y = ydiag + yoff + dskf * x
        alast_col = jnp.sum(_sel_left_nt(et, adt_tot8), axis=1, keepdims=True) * 0.125
        outs.append((y, jnp.exp(alast_col) * s_in + st))
    return outs[0][0], outs[1][0], outs[0][1], outs[1][1]


def _ssd_specs(t, rev):
    q = SSD_CHUNK
    nc = t // q
    ci = (lambda c: nc - 1 - c) if rev else (lambda c: c)
    xcol0 = SSD_D_INNER // SSD_STATE
    return dict(
        x_lo=pl.BlockSpec((q, HALF), lambda g, c: (ci(c), 2 * g)),
        x_hi=pl.BlockSpec((q, HALF), lambda g, c: (ci(c), 2 * g + 1)),
        bm=pl.BlockSpec((q, SSD_STATE), lambda g, c: (ci(c), xcol0 + g)),
        cm=pl.BlockSpec((q, SSD_STATE), lambda g, c: (ci(c), xcol0 + SSD_GROUPS + g)),
        dt=pl.BlockSpec((None, q, PAD_HEADS), lambda g, c: (g, ci(c), 0)),
        par=pl.BlockSpec((None, 8, PAD_HEADS), lambda g, c: (g, 0, 0)),
        st=pl.BlockSpec((None, None, 2, HALF, SSD_STATE), lambda g, c: (ci(c), g, 0, 0, 0)),
        y=pl.BlockSpec((q, 2 * HALF), lambda g, c: (ci(c), g)),
        grp=pl.BlockSpec((q, SSD_STATE), lambda g, c: (ci(c), g)),
    )


def _ssd_fwd(xc, dt4, dtb, alog, dsk, *, name):
    t = xc.shape[0]
    nc = t // SSD_CHUNK
    sp = _ssd_specs(t, False)

    def body(xl, xh, bm, cm, dt, p0, p1, p2, y_ref, sin_ref, st_ref):
        @pl.when(pl.program_id(1) == 0)
        def _():
            st_ref[...] = jnp.zeros_like(st_ref)

        sin_ref[...] = st_ref[...]
        y_lo, y_hi, so_lo, so_hi = _ssd_chunk(xl[...], xh[...], bm[...], cm[...], dt[...], p0[...], p1[...],
                                              p2[...], st_ref[0], st_ref[1])
        y_ref[:, :HALF] = y_lo
        y_ref[:, HALF:] = y_hi
        st_ref[0] = so_lo
        st_ref[1] = so_hi

    return _pcall(
        body, name=name, grid=(SSD_GROUPS, nc),
        in_specs=[sp['x_lo'], sp['x_hi'], sp['bm'], sp['cm'], sp['dt'], sp['par'], sp['par'], sp['par']],
        out_specs=[sp['y'], sp['st']],
        out_shape=[_sds((t, SSD_D_INNER)), _sds((nc, SSD_GROUPS, 2, HALF, SSD_STATE))],
        scratch_shapes=[pltpu.VMEM((2, HALF, SSD_STATE), F32)],
        compiler_params=_params("parallel", "arbitrary"),
    )(xc, xc, xc, xc, dt4, dtb, alog, dsk)


def _ssd_bwd(xc, dt4, dtb, alog, dsk, sin, dy, *, name):
    t = xc.shape[0]
    nc = t // SSD_CHUNK
    sp = _ssd_specs(t, True)

    def body(xl, xh, bm, cm, dt, p0, p1, p2, sin_ref, dy_ref,
             dx_ref, db_ref, dc_ref, ddt_ref, dp0, dp1, dp2, dst_ref):
        first = pl.program_id(1) == 0

        @pl.when(first)
        def _():
            dst_ref[...] = jnp.zeros_like(dst_ref)

        _, vjp = jax.vjp(_ssd_chunk, xl[...], xh[...], bm[...], cm[...], dt[...], p0[...], p1[...], p2[...],
                         sin_ref[0], sin_ref[1])
        dxl, dxh, dbm, dcm, ddt, g0, g1, g2, ds_lo, ds_hi = vjp(
            (dy_ref[:, :HALF], dy_ref[:, HALF:], dst_ref[0], dst_ref[1]))
        dx_ref[:, :HALF] = dxl
        dx_ref[:, HALF:] = dxh
        db_ref[...] = dbm
        dc_ref[...] = dcm
        ddt_ref[...] = ddt
        dst_ref[0] = ds_lo
        dst_ref[1] = ds_hi
        for ref, g in ((dp0, g0), (dp1, g1), (dp2, g2)):
            tot = jnp.broadcast_to(jnp.sum(g, axis=0, keepdims=True), ref.shape)

            @pl.when(first)
            def _(ref=ref):
                ref[...] = jnp.zeros_like(ref)

            ref[...] += tot

    return _pcall(
        body, name=name, grid=(SSD_GROUPS, nc),
        in_specs=[sp['x_lo'], sp['x_hi'], sp['bm'], sp['cm'], sp['dt'], sp['par'], sp['par'], sp['par'],
                  sp['st'], sp['y']],
        out_specs=[sp['y'], sp['grp'], sp['grp'], sp['dt'], sp['par'], sp['par'], sp['par']],
        out_shape=[_sds((t, SSD_D_INNER)), _sds((t, SSD_GROUPS * SSD_STATE)), _sds((t, SSD_GROUPS * SSD_STATE)),
                   _sds((SSD_GROUPS, t, PAD_HEADS))] + [_sds((SSD_GROUPS, 8, PAD_HEADS))] * 3,
        scratch_shapes=[pltpu.VMEM((2, HALF, SSD_STATE), F32)],
        compiler_params=_params("parallel", "arbitrary"),
    )(xc, xc, xc, xc, dt4, dtb, alog, dsk, sin, dy)


def _gatenorm(y, z, g):
    v = y * _silu(z)
    return v * lax.rsqrt(jnp.mean(v * v, axis=-1, keepdims=True) + RMS_EPS) * g


S5_CH = S5_WIDTH // S5_BLOCKS
S5_ST = S5_CH * S5_STATE // S5_GROUP
SCAN_UNROLL = 8


def _cmul(ar, ai, br, bi):
    return ar * br - ai * bi, ar * bi + ai * br


def _segment_power(ar, ai, n):
    assert n & (n - 1) == 0
    for _ in range(n.bit_length() - 1):
        ar, ai = _cmul(ar, ai, ar, ai)
    return ar, ai


def _carry_in(fr, fi, pr, pi, reverse):
    rows = lax.broadcasted_iota(jnp.int32, fr.shape, 0)
    cr = jnp.zeros_like(fr[0:1])
    ci = jnp.zeros_like(cr)
    outr = jnp.zeros_like(fr)
    outi = jnp.zeros_like(fr)
    order = range(6, -1, -1) if reverse else range(1, 8)
    for j in order:
        src = j + 1 if reverse else j - 1
        nr, ni = _cmul(pr[0:1], pi[0:1], cr, ci)
        cr, ci = nr + fr[src:src + 1], ni + fi[src:src + 1]
        outr = jnp.where(rows == j, cr, outr)
        outi = jnp.where(rows == j, ci, outi)
    return outr, outi


def _s5_specs(t):
    return dict(ch=pl.BlockSpec((t, S5_CH), lambda j: (0, j)), st=pl.BlockSpec((t, S5_ST), lambda j: (0, j)),
                lam=pl.BlockSpec((1, S5_ST), lambda j: (0, j)),
                b=pl.BlockSpec((None, S5_CH, S5_ST), lambda j: (j, 0, 0)),
                c=pl.BlockSpec((None, S5_ST, S5_CH), lambda j: (j, 0, 0)))


def _s5_fwd(u5, bre, bim, cre, cim, lr, li, *, name):
    t = u5.shape[0]
    nrt = t // 8

    def body(u_ref, bre_ref, bim_ref, cre_ref, cim_ref, lr_ref, li_ref, sr_ref, si_ref, y_ref, br_ref, bi_ref):
        u = u_ref[...]
        br_ref[...] = _dot(u, bre_ref[...], _NN)
        bi_ref[...] = _dot(u, bim_ref[...], _NN)
        ar = jnp.broadcast_to(lr_ref[...], (8, S5_ST))
        ai = jnp.broadcast_to(li_ref[...], (8, S5_ST))

        def step(r, s, store):
            rows = pl.ds(pl.multiple_of(r * 8, 8), 8)
            nr, ni = _cmul(ar, ai, s[0], s[1])
            nr, ni = nr + br_ref[rows, :], ni + bi_ref[rows, :]
            if store:
                sr_ref[rows, :] = nr
                si_ref[rows, :] = ni
            return nr, ni

        zero = (jnp.zeros((8, S5_ST), F32), jnp.zeros((8, S5_ST), F32))
        fr, fi = lax.fori_loop(0, nrt, lambda r, s: step(r, s, False), zero, unroll=SCAN_UNROLL)
        pr, pi = _segment_power(ar, ai, nrt)
        init = _carry_in(fr, fi, pr, pi, False)
        lax.fori_loop(0, nrt, lambda r, s: step(r, s, True), init, unroll=SCAN_UNROLL)
        y_ref[...] = _dot(sr_ref[...], cre_ref[...], _NN) - _dot(si_ref[...], cim_ref[...], _NN)

    sp = _s5_specs(t)
    w = S5_BLOCKS * S5_ST
    return _pcall(
        body, name=name, grid=(S5_BLOCKS,),
        in_specs=[sp['ch'], sp['b'], sp['b'], sp['c'], sp['c'], sp['lam'], sp['lam']],
        out_specs=[sp['st'], sp['st'], sp['ch']], out_shape=[_sds((t, w)), _sds((t, w)), _sds((t, S5_WIDTH))],
        scratch_shapes=[pltpu.VMEM((t, S5_ST), F32)] * 2, compiler_params=_params("parallel"),
    )(u5, bre, bim, cre, cim, lr, li)


def _s5_bwd(dy, du_direct, u5, sr, si, bre, bim, cre, cim, lr, li, *, name):
    t = u5.shape[0]
    nrt = t // 8

    def body(dy_ref, dd_ref, u_ref, sr_ref, si_ref, bre_ref, bim_ref, cre_ref, cim_ref, lr_ref, li_ref,
             du_ref, dbre_ref, dbim_ref, dcre_ref, dcim_ref, dlr_ref, dli_ref, gr_ref, gi_ref):
        dyv = dy_ref[...]
        gr_ref[...] = _dot(dyv, cre_ref[...], _NT)
        gi_ref[...] = -_dot(dyv, cim_ref[...], _NT)
        dcre_ref[...] = _dot(sr_ref[...], dyv, _TN)
        dcim_ref[...] = -_dot(si_ref[...], dyv, _TN)
        dr_ref, di_ref = gr_ref, gi_ref
        ar = jnp.broadcast_to(lr_ref[...], (8, S5_ST))
        ai = -jnp.broadcast_to(li_ref[...], (8, S5_ST))
        zero = jnp.zeros((8, S5_ST), F32)

        def step1(k, g):
            rows = pl.ds(pl.multiple_of((nrt - 1 - k) * 8, 8), 8)
            nr, ni = _cmul(ar, ai, g[0], g[1])
            return nr + dr_ref[rows, :], ni + di_ref[rows, :]

        fr, fi = lax.fori_loop(0, nrt, step1, (zero, zero), unroll=SCAN_UNROLL)
        pr, pi = _segment_power(ar, ai, nrt)
        init = _carry_in(fr, fi, pr, pi, True)

        def step2(k, carry):
            gr, gi, accr, acci = carry
            r = nrt - 1 - k
            rows = pl.ds(pl.multiple_of(r * 8, 8), 8)
            prev = pl.ds(pl.multiple_of(jnp.maximum(r - 1, 0) * 8, 8), 8)
            nr, ni = _cmul(ar, ai, gr, gi)
            nr, ni = nr + dr_ref[rows, :], ni + di_ref[rows, :]
            gr_ref[rows, :] = nr
            gi_ref[rows, :] = ni
            keep = jnp.where(r > 0, 1.0, 0.0)
            pr_, pi_ = sr_ref[prev, :] * keep, si_ref[prev, :] * keep
            return nr, ni, accr + (pr_ * nr + pi_ * ni), acci + (pr_ * ni - pi_ * nr)

        _, _, accr, acci = lax.fori_loop(0, nrt, step2, (init[0], init[1], zero, zero), unroll=SCAN_UNROLL)
        last = pl.ds((nrt - 1) * 8, 8)
        pr_, pi_ = _shift_down(sr_ref[last, :], 1), _shift_down(si_ref[last, :], 1)
        g0r, g0i = gr_ref[0:8, :], gi_ref[0:8, :]
        accr = accr + (pr_ * g0r + pi_ * g0i)
        acci = acci + (pr_ * g0i - pi_ * g0r)
        dlr_ref[...] = jnp.sum(accr, axis=0, keepdims=True)
        dli_ref[...] = jnp.sum(acci, axis=0, keepdims=True)
        u = u_ref[...]
        dbre_ref[...] = _dot(u, gr_ref[...], _TN)
        dbim_ref[...] = _dot(u, gi_ref[...], _TN)
        du = dd_ref[...] + _dot(gr_ref[...], bre_ref[...], _NT) + _dot(gi_ref[...], bim_ref[...], _NT)
        du_ref[...] = du.astype(du_ref.dtype)

    sp = _s5_specs(t)
    w = S5_BLOCKS * S5_ST
    return _pcall(
        body, name=name, grid=(S5_BLOCKS,),
        in_specs=[sp['ch'], sp['ch'], sp['ch'], sp['st'], sp['st'], sp['b'], sp['b'], sp['c'], sp['c'], sp['lam'], sp['lam']],
        out_specs=[sp['ch'], sp['b'], sp['b'], sp['c'], sp['c'], sp['lam'], sp['lam']],
        out_shape=[_sds((t, S5_WIDTH), BF16), _sds((S5_BLOCKS, S5_CH, S5_ST)), _sds((S5_BLOCKS, S5_CH, S5_ST)),
                   _sds((S5_BLOCKS, S5_ST, S5_CH)), _sds((S5_BLOCKS, S5_ST, S5_CH)), _sds((1, w)), _sds((1, w))],
        scratch_shapes=[pltpu.VMEM((t, S5_ST), F32)] * 2, compiler_params=_params("parallel"),
    )(dy, du_direct, u5, sr, si, bre, bim, cre, cim, lr, li)


def _s5_expander():
    n = lax.broadcasted_iota(jnp.int32, (S5_STATE, S5_STATE * S5_GROUP), 0)
    j = lax.broadcasted_iota(jnp.int32, (S5_STATE, S5_STATE * S5_GROUP), 1)
    return jnp.where(n == j // S5_GROUP, 1.0, 0.0).astype(F32)


def _s5_discretise(lam_re, lam_im, log_step, b_re, b_im):
    lr = jnp.minimum(lam_re, S5_MAX_REAL)
    step = jnp.exp(log_step)
    mag = jnp.exp(lr * step)
    ang = lam_im * step
    lbr, lbi = mag * jnp.cos(ang), mag * jnp.sin(ang)
    p, q = lbr - 1.0, lbi
    den = lr * lr + lam_im * lam_im
    cr, ci = (p * lr + q * lam_im) / den, (q * lr - p * lam_im) / den
    e = _s5_expander()
    cre, cie = _fdot(cr, e), _fdot(ci, e)
    return lbr, lbi, cre * b_re - cie * b_im, cre * b_im + cie * b_re


def _s5_params_fwd(lam_re, lam_im, log_step, b_re, b_im, *, name):
    g, n, w = lam_re.shape[0], S5_STATE, S5_STATE * S5_GROUP

    def body(a, b, c, d, e, o0, o1, o2, o3):
        for ref, val in zip((o0, o1, o2, o3), _s5_discretise(a[...], b[...], c[...], d[...], e[...])):
            ref[...] = val

    return _pcall(body, name=name, out_shape=[_sds((g, n)), _sds((g, n)), _sds((g, w)), _sds((g, w))])(
        lam_re, lam_im, log_step, b_re, b_im)


def _s5_params_bwd(lam_re, lam_im, log_step, b_re, b_im, cts, *, name):
    g, n, w = S5_GROUPS, S5_STATE, S5_STATE * S5_GROUP

    def body(a, b, c, d, e, c0, c1, c2, c3, o0, o1, o2, o3, o4):
        _, vjp = jax.vjp(_s5_discretise, a[...], b[...], c[...], d[...], e[...])
        for ref, val in zip((o0, o1, o2, o3, o4), vjp((c0[...], c1[...], c2[...], c3[...]))):
            ref[...] = val

    return _pcall(body, name=name,
                  out_shape=[_sds((g, n)), _sds((g, n)), _sds((g, 1)), _sds((g, w)), _sds((g, w))])(
        lam_re, lam_im, log_step, b_re, b_im, *cts)


def _s5_prepare(w):
    rows = DEPTH * S5_GROUPS
    lbr, lbi, bbr, bbi = _s5_params_fwd(
        w['s5_lambda_re'].reshape(rows, -1), w['s5_lambda_im'].reshape(rows, -1), w['s5_log_step'].reshape(rows, 1),
        w['s5_b_re'].reshape(rows, -1), w['s5_b_im'].reshape(rows, -1), name="s5_par")
    bd = lambda m: _blockdiag(m.reshape(rows, S5_STATE, S5_GROUP).transpose(0, 2, 1), S5_GROUP, S5_STATE).astype(BF16)
    cd = lambda m: _blockdiag(m.reshape(rows, S5_GROUP, S5_STATE).transpose(0, 2, 1), S5_STATE, S5_GROUP).astype(BF16)
    bre, bim, cre, cim = bd(bbr), bd(bbi), cd(w['s5_c_re']), cd(w['s5_c_im'])
    lr, li = lbr.reshape(DEPTH, 1, -1), lbi.reshape(DEPTH, 1, -1)
    blk = lambda a, i: a[i * S5_BLOCKS:(i + 1) * S5_BLOCKS]
    return [dict(bre=blk(bre, i), bim=blk(bim, i), cre=blk(cre, i), cim=blk(cim, i), lr=lr[i], li=li[i])
            for i in range(DEPTH)]


def _perm(a):
    t, c = a.shape
    return a.reshape(8, t // 8, c).transpose(1, 0, 2).reshape(t, c)


def _unperm(a):
    t, c = a.shape
    return a.reshape(t // 8, 8, c).transpose(1, 0, 2).reshape(t, c)


def _blockdiag(m, rows_inner, cols_inner):
    nblk = m.shape[0] // 8
    m = m.reshape(nblk, 8, rows_inner, cols_inner)
    eye = jnp.eye(8, dtype=m.dtype)
    out = m[:, :, :, None, :] * eye[None, :, None, :, None]
    return out.reshape(nblk, 8 * rows_inner, 8 * cols_inner)


def _blockdiag_extract(m, rows_inner, cols_inner):
    m = m.reshape(S5_BLOCKS, 8, rows_inner, 8, cols_inner)
    d = jnp.diagonal(m, axis1=1, axis2=3)
    return d.transpose(0, 3, 1, 2).reshape(S5_GROUPS, rows_inner, cols_inner)


Z0, XBC0, GA0, GB0 = 0, SSD_D_INNER, SSD_D_INNER + SSD_CONV_DIM, SSD_D_INNER + SSD_CONV_DIM + D_MODEL
BIG = GB0 + D_MODEL


def _mixer_fwd(h, p, tm):
    t = h.shape[0]
    nrow = t // tm
    u = _rms_fwd(h, p['pre_g'], name="mix_rms", tm=tm)
    u_p = _perm(u)
    proj = _mm(u, p['w_big'], name="mix_in")
    dtr = _mm(u, p['w_dt'], name="mix_in_dt")
    u5 = _mm(u_p, p['w_u5'], name="mix_in_s5")
    late, proj = lax.optimization_barrier((p['late'], proj))
    by_rows = lambda a: a.reshape(-1, a.shape[-1])
    p = dict(p, w_a=by_rows(late['w_branch_a']), w_b=by_rows(late['w_branch_b']), w_out=by_rows(late['w_out']),
             w_glu=late['s5_w_glu'][:, 0].transpose(1, 0, 2).reshape(late['s5_w_glu'].shape[2], -1))
    xc = _conv_fwd(proj, XBC0, p['conv_w'], p['conv_b'], name="ssd_conv")
    dt4 = jnp.pad(dtr.reshape(t, SSD_GROUPS, 8).transpose(1, 0, 2), ((0, 0), (0, 0), (0, PAD_HEADS - 8)))
    y_ssd, s_in = _ssd_fwd(xc, dt4, p['dt_bias8'], p['a_log8'], p['d8'], name="ssd_scan")
    gw = SSD_D_INNER // SSD_GROUPS
    ya = _rows(_gatenorm, name="ssd_gate", nrow=nrow, ncol=SSD_GROUPS,
               ins=[(y_ssd, _rspec(tm, gw, 0, True)), (proj, _rspec(tm, gw, Z0 // gw, True)),
                    (p['norm_g'], _bspec(gw, 0, True))],
               outs=[(_sds((t, SSD_D_INNER), BF16), _rspec(tm, gw, 0, True), False)])[0]
    y_a = _mm(ya, p['w_a'], name="mix_a")
    bre, bim, cre, cim, lr, li = (p['s5'][k] for k in ('bre', 'bim', 'cre', 'cim', 'lr', 'li'))
    sr, si, y5 = _s5_fwd(u5, bre, bim, cre, cim, lr, li, name="s5_scan")
    y5g = _rows(lambda a, b, d: _gelu(a + d * b), name="s5_act", nrow=nrow,
                ins=[(y5, _rspec(tm, S5_WIDTH)), (u5, _rspec(tm, S5_WIDTH)), (p['s5_d'], _bspec(S5_WIDTH))],
                outs=[(_sds((t, S5_WIDTH), BF16), _rspec(tm, S5_WIDTH), False)])[0]
    vg = _mm(y5g, p['w_glu'], name="s5_glu")
    ybin = _rows(lambda a, b: a * _sigmoid(b), name="s5_glu_act", nrow=nrow,
                 ins=[(vg, _rspec(tm, S5_WIDTH, 0)), (vg, _rspec(tm, S5_WIDTH, 1))],
                 outs=[(_sds((t, S5_WIDTH), BF16), _rspec(tm, S5_WIDTH), False)])[0]
    y_b = _unperm(_mm(ybin, p['w_b'], name="mix_b"))
    merged = _rows(lambda ga, gb, a, b: _sigmoid(ga) * a + _sigmoid(gb) * b, name="mix_merge", nrow=nrow,
                   ins=[(proj, _rspec(tm, D_MODEL, GA0 // D_MODEL)), (proj, _rspec(tm, D_MODEL, GB0 // D_MODEL)),
                        (y_a, _rspec(tm, D_MODEL)), (y_b, _rspec(tm, D_MODEL))],
                   outs=[(_sds((t, D_MODEL), BF16), _rspec(tm, D_MODEL), False)])[0]
    m = _mm(merged, p['w_out'], name="mix_out")
    out = _resid_fwd(h, m, p['post_g'], 1.0, name="mix_res", tm=tm)
    saved = dict(w_a=p['w_a'], w_b=p['w_b'], w_out=p['w_out'], w_glu=p['w_glu'], h=h, u=u, u_p=u_p, proj=proj, u5=u5, xc=xc, dt4=dt4, s_in=s_in, y_ssd=y_ssd, ya=ya, y_a=y_a,
                 bre=bre, bim=bim, cre=cre, cim=cim, lr=lr, li=li, sr=sr, si=si, y5=y5, y5g=y5g, vg=vg, ybin=ybin,
                 y_b=y_b, merged=merged, m=m)
    return out, saved


def _mixer_bwd(dh, p, s, tm, after_first):
    t = dh.shape[0]
    nrow = t // tm
    proj = s['proj']
    bufs = {}

    def grad_mm(a, b, wname, axis, name):
        bufs[wname] = _mm(a, b, ta=True, name=name, out_dtype=BF16, shards=axis)

    dm, dpost = _resid_bwd(s['m'], p['post_g'], dh, 1.0, name="mix_res_bwd", tm=tm)
    dm = after_first(dm)
    dmerged = _mm(dm, s['w_out'], tb=True, name="mix_out_dx")
    grad_mm(s['merged'], dm, 'w_out', 'rows', "mix_out_dw")

    def merge_bwd(ga, gb, a, b, d):
        _, vjp = jax.vjp(lambda ga_, gb_, a_, b_: _sigmoid(ga_) * a_ + _sigmoid(gb_) * b_, ga, gb, a, b)
        dga, dgb, da, db = vjp(d)
        return jnp.concatenate([dga, dgb], axis=1), da, db

    dgab, dy_a, dy_b = _rows(
        merge_bwd, name="mix_merge_bwd", nrow=nrow,
        ins=[(proj, _rspec(tm, D_MODEL, GA0 // D_MODEL)), (proj, _rspec(tm, D_MODEL, GB0 // D_MODEL)),
             (s['y_a'], _rspec(tm, D_MODEL)), (s['y_b'], _rspec(tm, D_MODEL)), (dmerged, _rspec(tm, D_MODEL))],
        outs=[(_sds((t, 2 * D_MODEL), BF16), _rspec(tm, 2 * D_MODEL), False),
              (_sds((t, D_MODEL), BF16), _rspec(tm, D_MODEL), False),
              (_sds((t, D_MODEL), BF16), _rspec(tm, D_MODEL), False)])
    dya = _mm(dy_a, s['w_a'], tb=True, name="mix_a_dx")
    grad_mm(s['ya'], dy_a, 'w_branch_a', 'rows', "mix_a_dw")
    gw = SSD_D_INNER // SSD_GROUPS

    def gate_bwd(y, z, g, d):
        _, vjp = jax.vjp(_gatenorm, y, z, g)
        return vjp(d)

    dy_ssd, dz, dnorm = _rows(
        gate_bwd, name="ssd_gate_bwd", nrow=nrow, ncol=SSD_GROUPS,
        ins=[(s['y_ssd'], _rspec(tm, gw, 0, True)), (proj, _rspec(tm, gw, Z0 // gw, True)),
             (p['norm_g'], _bspec(gw, 0, True)), (dya, _rspec(tm, gw, 0, True))],
        outs=[(_sds((t, SSD_D_INNER)), _rspec(tm, gw, 0, True), False),
              (_sds((t, SSD_D_INNER), BF16), _rspec(tm, gw, 0, True), False),
              (_sds((1, SSD_D_INNER)), _bspec(gw, 0, True), True)])
    dxs, dbm, dcm, ddt4, ddtb, dalog, ddsk = _ssd_bwd(s['xc'], s['dt4'], p['dt_bias8'], p['a_log8'], p['d8'],
                                                      s['s_in'], dy_ssd, name="ssd_scan_bwd")
    dxbc, dconv_w, dconv_b = _conv_bwd(proj, XBC0, p['conv_w'], p['conv_b'], (dxs, dbm, dcm), name="ssd_conv_bwd")
    ddtr = ddt4[:, :, :8].transpose(1, 0, 2).reshape(t, SSD_HEADS)
    dy_bp = _perm(dy_b)
    dybin = _mm(dy_bp, s['w_b'], tb=True, name="mix_b_dx")
    grad_mm(s['ybin'], dy_bp, 'w_branch_b', 'rows', "mix_b_dw")

    def glu_bwd(a, b, d):
        _, vjp = jax.vjp(lambda a_, b_: a_ * _sigmoid(b_), a, b)
        da, db = vjp(d)
        return jnp.concatenate([da, db], axis=1)

    dvg = _rows(glu_bwd, name="s5_glu_act_bwd", nrow=nrow,
                ins=[(s['vg'], _rspec(tm, S5_WIDTH, 0)), (s['vg'], _rspec(tm, S5_WIDTH, 1)), (dybin, _rspec(tm, S5_WIDTH))],
                outs=[(_sds((t, 2 * S5_WIDTH), BF16), _rspec(tm, 2 * S5_WIDTH), False)])[0]
    dy5g = _mm(dvg, s['w_glu'], tb=True, name="s5_glu_dx")
    grad_mm(s['y5g'], dvg, 's5_w_glu', 'cols', "s5_glu_dw")

    def act_bwd(a, b, d, g):
        _, vjp = jax.vjp(lambda a_, b_, d_: _gelu(a_ + d_ * b_), a, b, d)
        return vjp(g)

    dy5, du5_direct, ds5d = _rows(
        act_bwd, name="s5_act_bwd", nrow=nrow,
        ins=[(s['y5'], _rspec(tm, S5_WIDTH)), (s['u5'], _rspec(tm, S5_WIDTH)), (p['s5_d'], _bspec(S5_WIDTH)),
             (dy5g, _rspec(tm, S5_WIDTH))],
        outs=[(_sds((t, S5_WIDTH), BF16), _rspec(tm, S5_WIDTH), False), (_sds((t, S5_WIDTH)), _rspec(tm, S5_WIDTH), False),
              (_sds((1, S5_WIDTH)), _bspec(S5_WIDTH), True)])
    du5, dbre, dbim, dcre, dcim, dlr, dli = _s5_bwd(dy5, du5_direct, s['u5'], s['sr'], s['si'], s['bre'], s['bim'],
                                                     s['cre'], s['cim'], s['lr'], s['li'], name="s5_scan_bwd")
    du_p = _mm(du5, p['w_u5'], tb=True, name="mix_in_s5_dx")
    dw_u5 = _mm(s['u_p'], du5, ta=True, name="mix_in_s5_dw", out_dtype=BF16)
    ext_b = lambda m: _blockdiag_extract(m, S5_GROUP, S5_STATE).transpose(0, 2, 1).reshape(S5_GROUPS, S5_STATE * S5_GROUP)
    dlam_re, dlam_im, dlog_step, db_re, db_im = _s5_params_bwd(
        p['lam_re'], p['lam_im'], p['log_step'], p['b_re'], p['b_im'],
        (dlr.reshape(S5_GROUPS, S5_STATE), dli.reshape(S5_GROUPS, S5_STATE), ext_b(dbre), ext_b(dbim)), name="s5_par_bwd")
    dc_re = _blockdiag_extract(dcre, S5_STATE, S5_GROUP).transpose(0, 2, 1)
    dc_im = _blockdiag_extract(dcim, S5_STATE, S5_GROUP).transpose(0, 2, 1)
    dproj = jnp.concatenate([dz, dxbc, dgab], axis=1)
    du_big = _mm(dproj, p['w_big'], tb=True, name="mix_in_dx")
    du_dt = _mm(ddtr, p['w_dt'], tb=True, name="mix_in_dt_dx")
    dw_big = _mm(s['u'], dproj, ta=True, name="mix_in_dw", out_dtype=BF16)
    dw_dt = _mm(s['u'], ddtr, ta=True, name="mix_in_dt_dw", out_dtype=BF16)
    dh_in, dpre = _rms_bwd(s['h'], p['pre_g'], dh, [du_big, du_dt, _unperm(du_p)], name="mix_rms_bwd", tm=tm)
    dw_in = jnp.concatenate([dw_big[:, :GA0], dw_dt, dw_u5, dw_big[:, GA0:]], axis=1)
    bufs['w_in'] = dw_in.reshape(D_MODEL, N_DEV, -1).transpose(1, 0, 2)[:, None]
    grads = {
        'mix_pre_g': dpre, 'mix_post_g': dpost, 'ssd_conv_w': dconv_w, 'ssd_conv_b': dconv_b,
        'ssd_dt_bias': ddtb[:, 0, :8].reshape(-1), 'ssd_a_log': dalog[:, 0, :8].reshape(-1),
        'ssd_d': ddsk[:, 0, :8].reshape(-1), 'ssd_norm_g': dnorm,
        's5_lambda_re': dlam_re, 's5_lambda_im': dlam_im,
        's5_b_re': db_re.reshape(S5_GROUPS, S5_STATE, S5_GROUP), 's5_b_im': db_im.reshape(S5_GROUPS, S5_STATE, S5_GROUP),
        's5_c_re': dc_re, 's5_c_im': dc_im, 's5_log_step': dlog_step.reshape(-1), 's5_d': ds5d,
    }
    return dh_in, bufs, grads


HBM_SPEC = pl.BlockSpec(memory_space=pltpu.HBM)


def _place():
    return lax.axis_index("x"), lax.axis_index("y"), lax.axis_index("c")


GATHER_COLLECTIVE_ID = 1


def _all_gather(shards, *, name, on_sequencer=False):
    n = len(shards)

    def body(*refs):
        x_refs, out_refs = refs[:n], refs[n:2 * n]
        send_sems, recv_sems, local_sems = refs[2 * n:]
        x, y, c = _place()
        me, sibling = (x, y, c), (x, y, 1 - c)
        chips = [(1 - x, y), (x, 1 - y), (1 - x, 1 - y)]
        if on_sequencer:
            _handshake([sibling] + [(*chip, c) for chip in chips])

        def slot(o, px, py, pc):
            return out_refs[o].at[4 * px + 2 * py + pc]

        def copy(o, k, block, to, src=None):
            return pltpu.make_async_remote_copy(
                src_ref=slot(o, *block) if src is None else src, dst_ref=slot(o, *block),
                send_sem=send_sems.at[7 * o + k], recv_sem=recv_sems.at[7 * o + k], device_id=to, device_id_type=MESH)

        mine = [pltpu.make_async_copy(x_refs[o], slot(o, *me), local_sems.at[o]) for o in range(n)]
        for cp in mine:
            cp.start()
        first = []
        for j, chip in enumerate(chips):
            first += [copy(o, 1 + j, me, (*chip, c), src=x_refs[o]) for o in range(n)]
        first += [copy(o, 0, me, sibling, src=x_refs[o]) for o in range(n)]
        for cp in first:
            cp.start()
        passed = []
        for j, chip in enumerate(chips):
            for o in range(n):
                copy(o, 1 + j, (*chip, c), me).wait_recv()
                passed.append(copy(o, 4 + j, (*chip, c), sibling))
                passed[-1].start()
        for o in range(n):
            copy(o, 0, sibling, me).wait_recv()
        for j, chip in enumerate(chips):
            for o in range(n):
                copy(o, 4 + j, (*chip, 1 - c), me).wait_recv()
        for cp in first + passed:
            cp.wait_send()
        for cp in mine:
            cp.wait()

    out_shape = [jax.ShapeDtypeStruct((N_DEV,) + s.shape, s.dtype) for s in shards]
    sems = [pltpu.SemaphoreType.DMA((7 * n,)), pltpu.SemaphoreType.DMA((7 * n,)), pltpu.SemaphoreType.DMA((n,))]
    if on_sequencer:
        return _scall(body, name=name, out_type=out_shape, scratch_types=sems, collective_id=GATHER_COLLECTIVE_ID)(*shards)
    return _pcall(body, name=name, in_specs=[HBM_SPEC] * n, out_specs=[HBM_SPEC] * n, out_shape=out_shape,
                  scratch_shapes=sems)(*shards)


N_CHIPS = 4


SIBLING_COLLECTIVE_ID = 2
CHIPS_COLLECTIVE_ID = 3


def _handshake(peers):
    barrier = pltpu.get_barrier_semaphore()
    for peer in peers:
        pl.semaphore_signal(barrier, inc=1, device_id=peer, device_id_type=MESH)
    pl.semaphore_wait(barrier, len(peers))


def _exchange_sibling(grads, *, name):
    n = len(grads)

    def body(*refs):
        p_refs, q_refs = refs[:n], refs[n:2 * n]
        send_sems, recv_sems = refs[2 * n:]
        x, y, c = _place()
        _handshake([(x, y, 1 - c)])
        copies = [pltpu.make_async_remote_copy(
            src_ref=p_refs[o].at[k, 1 - c], dst_ref=q_refs[o].at[k], send_sem=send_sems.at[N_CHIPS * o + k],
            recv_sem=recv_sems.at[N_CHIPS * o + k], device_id=(x, y, 1 - c), device_id_type=MESH)
            for o in range(n) for k in range(N_CHIPS)]
        for cp in copies:
            cp.start()
        for cp in copies:
            cp.wait()

    return _scall(
        body, name=name, out_type=[jax.ShapeDtypeStruct((N_CHIPS,) + g.shape[2:], g.dtype) for g in grads],
        scratch_types=[pltpu.SemaphoreType.DMA((N_CHIPS * n,)), pltpu.SemaphoreType.DMA((N_CHIPS * n,))],
        collective_id=SIBLING_COLLECTIVE_ID,
    )(*grads)


def _pair_sum(own, got, *, name):
    _, _, r, l = own.shape
    tr = _tile(r, 512, 16)
    c = lax.axis_index("c").astype(jnp.int32).reshape(1)

    def body(c_ref, p_ref, q_ref, o_ref):
        o_ref[...] = (p_ref[...].astype(F32) + q_ref[...].astype(F32)).astype(o_ref.dtype)

    return _pcall(
        body, name=name,
        grid_spec=pltpu.PrefetchScalarGridSpec(
            num_scalar_prefetch=1, grid=(N_CHIPS, r // tr),
            in_specs=[pl.BlockSpec((None, None, tr, l), lambda k, i, cr: (k, cr[0], i, 0)),
                      pl.BlockSpec((None, tr, l), lambda k, i, cr: (k, i, 0))],
            out_specs=pl.BlockSpec((None, tr, l), lambda k, i, cr: (k, i, 0))),
        out_shape=jax.ShapeDtypeStruct((N_CHIPS, r, l), own.dtype),
        compiler_params=_params("parallel", "parallel"),
    )(c, own, got)


def _exchange_chips(parts, *, name):
    n = len(parts)

    def body(*refs):
        p_refs, g_refs = refs[:n], refs[n:2 * n]
        send_sems, recv_sems, local_sems = refs[2 * n:]
        x, y, c = _place()
        mine = 2 * x + y
        chips = [(1 - x, y), (x, 1 - y), (1 - x, 1 - y)]
        _handshake([(*chip, c) for chip in chips])
        own = [pltpu.make_async_copy(p_refs[o].at[mine], g_refs[o].at[mine], local_sems.at[o]) for o in range(n)]
        for cp in own:
            cp.start()
        copies = []
        for j, (px, py) in enumerate(chips):
            copies += [pltpu.make_async_remote_copy(
                src_ref=p_refs[o].at[2 * px + py], dst_ref=g_refs[o].at[mine], send_sem=send_sems.at[3 * o + j],
                recv_sem=recv_sems.at[3 * o + j], device_id=(px, py, c), device_id_type=MESH) for o in range(n)]
        for cp in copies:
            cp.start()
        for cp in copies:
            cp.wait()
        for cp in own:
            cp.wait()

    return _scall(
        body, name=name, out_type=[jax.ShapeDtypeStruct(p.shape, p.dtype) for p in parts],
        scratch_types=[pltpu.SemaphoreType.DMA((3 * n,)), pltpu.SemaphoreType.DMA((3 * n,)), pltpu.SemaphoreType.DMA((n,))],
        collective_id=CHIPS_COLLECTIVE_ID,
    )(*parts)


def _sum_slots(g, *, name):
    n, r, l = g.shape
    tr = _tile(r, 512, 16)

    def body(g_ref, o_ref):
        acc = g_ref[0].astype(F32)
        for k in range(1, n):
            acc = acc + g_ref[k].astype(F32)
        o_ref[...] = acc

    return _pcall(
        body, name=name, grid=(r // tr,), in_specs=[pl.BlockSpec((n, tr, l), lambda i: (0, i, 0))],
        out_specs=pl.BlockSpec((tr, l), lambda i: (i, 0)), out_shape=_sds((r, l)),
        compiler_params=_params("parallel"),
    )(g)


TRANSPOSED = ('ffn1_w_gate', 'ffn1_w_up', 'ffn2_w_gate', 'ffn2_w_up')
GATHER_CHUNKS = (('ffn1', ['ffn1_w_gate', 'ffn1_w_up']), ('ffn1_down', ['ffn1_w_down']),
                 ('mix_in', ['w_in', 'ssd_conv_w']), ('mix', ['w_branch_a', 's5_w_glu', 'w_branch_b', 'w_out']),
                 ('ffn2', ['ffn2_w_gate', 'ffn2_w_up']), ('ffn2_down', ['ffn2_w_down']))
LATE = ('ffn1_w_down', 'ffn2_w_down', 'w_branch_a', 's5_w_glu', 'w_branch_b', 'w_out')
SUBLAYERS = (('ffn1', ['ffn1_w_gate', 'ffn1_w_up', 'ffn1_w_down']),
             ('mix', ['w_in', 'ssd_conv_w', 'w_branch_a', 's5_w_glu', 'w_branch_b', 'w_out']),
             ('ffn2', ['ffn2_w_gate', 'ffn2_w_up', 'ffn2_w_down']))


def _gather_weights(w):
    layers, first = [], None
    for i in range(DEPTH):
        g = {}
        for tag, names in GATHER_CHUNKS:
            shards =[w[n][i:i + 1] if n == 'ssd_conv_w' else
                      (w[n][i:i + 1].transpose(0, 2, 1) if n in TRANSPOSED else w[n][i:i + 1]).astype(BF16) for n in names]
            if first is None:
                first = got = _all_gather(shards, name=f"gather_{tag}")
            else:
                shards, first = lax.optimization_barrier((shards, first))
                got = _all_gather(shards, name=f"gather_{tag}", on_sequencer=True)
            g.update(zip(names, got))
        layers.append(g)
    layers[0].update(zip(GATHER_CHUNKS[0][1], first))
    return layers


class _ReduceScatter:
    @staticmethod
    def sibling(tag, bufs):
        names = list(bufs)
        own = [bufs[n].reshape((N_CHIPS, 2) + bufs[n].shape[1:]) for n in names]
        return (tag, names), (own, _exchange_sibling(own, name=f"reduce_sibling_{tag}"))

    @staticmethod
    def chips(meta, arrays):
        (tag, names), (own, got) = meta, arrays
        flat = lambda a, lead: a.reshape(lead + (-1, a.shape[-1]))
        parts = [_pair_sum(flat(o, (N_CHIPS, 2)), flat(g, (N_CHIPS,)), name=f"reduce_pair_sum_{n}").reshape(g.shape)
                 for n, o, g in zip(names, own, got)]
        return names, _exchange_chips(parts, name=f"reduce_chips_{tag}")

    @staticmethod
    def done(names, slots):
        return dict(zip(names, slots))

    @staticmethod
    def small(grads):
        return _reduce_small(grads)


def _reduce_small(grads):
    flat = jnp.concatenate([g.astype(F32).reshape(-1) for g in grads.values()])
    pad = (-flat.shape[0]) % (8 * LANES)
    flat = jnp.concatenate([flat, jnp.zeros((pad,), F32)]).reshape(-1, LANES)
    gathered = _all_gather([flat], name="gather_small_grads", on_sequencer=True)[0]
    total = _sum_slots(gathered, name="sum_small_grads").reshape(-1)
    out, o = {}, 0
    for n, g in grads.items():
        out[n] = total[o:o + g.size].reshape(g.shape)
        o += g.size
    return out


def _adamw(w, g, m, v, *, name, slots=False):
    shape = w.shape
    if slots:
        lyr, rows, lanes = shape
        w2, m2, v2 = w, m, v
        tr = _tile(rows, 256, 16)
        nrt = rows // tr
        grid = (lyr, nrt)
        spec = pl.BlockSpec((None, tr, lanes), lambda l, i: (l, i, 0))
        g_specs = [pl.BlockSpec((N_CHIPS, None, tr, lanes),
                                lambda l, i, k=k: (0, 0, jnp.where(l == k, i, jnp.where(l > k, nrt - 1, 0)), 0))
                   for k in range(lyr)]
        g_args = list(g)
        out_shape = [_sds(shape)] * 4
    else:
        lanes = shape[-1] if (shape[-1] >= 128 or w.size % LANES) else LANES
        as2d = lambda a: a.reshape(-1, lanes)
        w2, m2, v2 = as2d(w), as2d(m), as2d(v)
        r = w2.shape[0]
        tr = _tile(r, 256, 8)
        grid = (1, r // tr)
        spec = pl.BlockSpec((tr, lanes), lambda l, i: (i, 0))
        g_specs, g_args = [spec], [as2d(g)]
        out_shape = [_sds((r, lanes))] * 4
    n_g = len(g_args)

    def body(w_ref, *rest):
        g_refs = rest[:n_g]
        m_ref, v_ref, go_ref, d_ref, mo_ref, vo_ref = rest[n_g:]
        if slots:
            gg = None
            for k, g_ref in enumerate(g_refs):
                tot = g_ref[0].astype(F32)
                for c in range(1, N_CHIPS):
                    tot = tot + g_ref[c].astype(F32)
                gg = tot if gg is None else jnp.where(pl.program_id(0) == k, tot, gg)
        else:
            gg = g_refs[0][...]
        go_ref[...] = gg
        mn = ADAM_B1 * m_ref[...] + (1.0 - ADAM_B1) * gg
        vn = ADAM_B2 * v_ref[...] + (1.0 - ADAM_B2) * (gg * gg)
        m_hat = mn / (1.0 - ADAM_B1 ** ADAM_STEP)
        v_hat = vn / (1.0 - ADAM_B2 ** ADAM_STEP)
        d_ref[...] = -ADAM_LR * (m_hat / (jnp.sqrt(v_hat) + ADAM_EPS) + ADAM_WD * w_ref[...])
        mo_ref[...] = mn
        vo_ref[...] = vn

    res = _pcall(
        body, name=name, grid=grid, in_specs=[spec] + g_specs + [spec, spec], out_specs=[spec] * 4,
        out_shape=out_shape, compiler_params=_params("arbitrary", "arbitrary"),
    )(w2, *g_args, m2, v2)
    return tuple(a.reshape(shape) for a in res)


def _sublayer_params(w, g, i, k, s5):
    row = lambda a: a.astype(F32).reshape(1, -1)
    if k != 'mix':
        return dict(layer=i, pre_g=row(w[f'{k}_pre_g'][i]), post_g=row(w[f'{k}_post_g'][i]),
                    w_gate=g[f'{k}_w_gate'], w_up=g[f'{k}_w_up'], w_down=g[f'{k}_w_down'])
    head8 = lambda a: jnp.broadcast_to(
        jnp.pad(a.astype(F32).reshape(SSD_GROUPS, 1, 8), ((0, 0), (0, 0), (0, PAD_HEADS - 8))), (SSD_GROUPS, 8, PAD_HEADS))
    by_rows = lambda n: g[n].reshape(-1, g[n].shape[-1])
    by_cols = lambda n: g[n][:, 0].transpose(1, 0, 2).reshape(g[n].shape[2], -1)
    w_in = by_cols('w_in')
    s = np.cumsum([SSD_D_INNER, SSD_CONV_DIM, SSD_HEADS, S5_WIDTH, D_MODEL])
    return dict(
        layer=i, s5=s5, pre_g=row(w['mix_pre_g'][i]), post_g=row(w['mix_post_g'][i]),
        w_big=jnp.concatenate([w_in[:, :s[1]], w_in[:, s[3]:]], axis=1), w_dt=w_in[:, s[1]:s[2]], w_u5=w_in[:, s[2]:s[3]],
        conv_w=by_cols('ssd_conv_w'), conv_b=row(w['ssd_conv_b'][i]),
        dt_bias8=head8(w['ssd_dt_bias'][i]), a_log8=head8(w['ssd_a_log'][i]), d8=head8(w['ssd_d'][i]),
        norm_g=row(w['ssd_norm_g'][i]), late={n: g[n] for n in SUBLAYERS[1][1] if n in LATE},
        lam_re=w['s5_lambda_re'][i], lam_im=w['s5_lambda_im'][i], log_step=w['s5_log_step'][i].reshape(S5_GROUPS, 1),
        b_re=w['s5_b_re'][i].reshape(S5_GROUPS, -1), b_im=w['s5_b_im'][i].reshape(S5_GROUPS, -1),
        c_re=w['s5_c_re'][i], c_im=w['s5_c_im'][i], s5_d=row(w['s5_d'][i]),
    )


def _loss_head(h, target, *, tm):
    t, d = h.shape

    def fn(y, tgt):
        err = y - tgt
        return err * (1.0 / d), jnp.sum(0.5 * jnp.sum(err * err, axis=-1, keepdims=True) * (1.0 / d), axis=0, keepdims=True)

    dy, loss = _rows(fn, name="loss_head", nrow=t // tm,
                     ins=[(h, _rspec(tm, d)), (target, _rspec(tm, d))],
                     outs=[(_sds((t, d)), _rspec(tm, d), False), (_sds((1, 128)), _bspec(128), True)])
    return dy, loss[0, 0]


def _forward_backward(h, target, w, g, rs):
    t = h.shape[0]
    tm = _tile(t, 512, 8)
    s5 = _s5_prepare(w)
    layers, saved = [], []
    for i in range(DEPTH):
        gi, ps, ss = dict(g[i]), [], []
        for tag, names in SUBLAYERS:
            early = [n for n in names if n not in LATE]
            tied, h = lax.optimization_barrier(([gi[n] for n in early], h))
            gi.update(zip(early, tied))
            p = _sublayer_params(w, gi, i, tag, s5[i])
            h, s = _mixer_fwd(h, p, tm) if tag == 'mix' else _ffn_fwd(h, p, tag, tm)
            ps.append(p)
            ss.append(s)
        layers.append(ps)
        saved.append(ss)
    dh, loss = _loss_head(h, target, tm=tm)
    reduced, small = [{} for _ in range(DEPTH)], [{} for _ in range(DEPTH)]
    in_sibling, in_chips = None, None

    def start_chips(x):
        nonlocal in_sibling, in_chips
        if in_sibling is not None:
            layer, meta, arrays = in_sibling
            arrays, x = lax.optimization_barrier((arrays, x))
            in_sibling, in_chips = None, (layer,) + tuple(rs.chips(meta, arrays))
        return x

    def finish_chips(x):
        nonlocal in_chips
        if in_chips is not None:
            layer, names, slots = in_chips
            slots, x = lax.optimization_barrier((slots, x))
            reduced[layer].update(rs.done(names, slots))
            in_chips = None
        return x

    for i in reversed(range(DEPTH)):
        for k in reversed(range(len(SUBLAYERS))):
            tag = SUBLAYERS[k][0]
            if tag == 'mix':
                dh, bufs, grads = _mixer_bwd(dh, layers[i][k], saved[i][k], tm, start_chips)
            else:
                dh, bufs, grads = _ffn_bwd(dh, layers[i][k], saved[i][k], tag, tm, start_chips)
            small[i].update(grads)
            dh = finish_chips(dh)
            in_sibling = (i,) + tuple(rs.sibling(tag, bufs))
            if tag == 'mix' and i + 1 < DEPTH:
                small[i + 1], dh = lax.optimization_barrier((small[i + 1], dh))
        if i == 0:
            small[i]['loss'] = loss.reshape(1)
        small[i] = rs.small(small[i])
    loss = small[0].pop('loss')[0]
    dh = finish_chips(start_chips(dh))
    shapes = {n: (w[n].shape[:-1] + (SSD_CONV_DIM,) if n == 'ssd_conv_w' else w[n].shape) for n in SMALL_ORDER}
    stacked = {n: jnp.stack([small[i][n].reshape(shapes[n][1:]) for i in range(DEPTH)]) for n in SMALL_ORDER}
    return loss, dh, reduced, stacked


def kernel(*args):
    n_w = len(WEIGHTS)
    x, target = args[0], args[1 + n_w]
    w = dict(zip(WEIGHTS, args[1:1 + n_w]))
    m = dict(zip(WEIGHTS, args[2 + n_w:2 + 2 * n_w]))
    v = dict(zip(WEIGHTS, args[2 + 2 * n_w:2 + 3 * n_w]))
    t = x.shape[1]

    g = _gather_weights(w)
    loss, dx, slots, small = _forward_backward(x.reshape(t, D_MODEL), target.reshape(t, D_MODEL), w, g, _ReduceScatter)
    me = 4 * lax.axis_index("x") + 2 * lax.axis_index("y") + lax.axis_index("c")
    cols = w['ssd_conv_w'].shape[-1]
    small['ssd_conv_w'] = lax.dynamic_slice_in_dim(small['ssd_conv_w'], me * cols, cols, axis=2)

    grad, delta, new_m, new_v = {}, {}, {}, {}
    for n in WEIGHTS:
        sharded = n in slots[0]
        view = (lambda a: a.transpose(0, 2, 1)) if n in TRANSPOSED else (lambda a: a)
        res = _adamw(view(w[n]), [slots[i][n] for i in range(DEPTH)] if sharded else small[n], view(m[n]), view(v[n]),
                     name=f"adamw_{n}", slots=sharded)
        grad[n], delta[n], new_m[n], new_v[n] = (view(a) for a in res)
    return (loss, dx.reshape(x.shape), *[grad[n] for n in WEIGHTS], *[delta[n] for n in WEIGHTS],
            *[new_m[n] for n in WEIGHTS], *[new_v[n] for n in WEIGHTS])
```

```python
import functools
import math

import numpy as np
import jax
import jax.numpy as jnp
from jax import lax
from jax.experimental import pallas as pl
from jax.experimental.pallas import tpu as pltpu
from jax.experimental.pallas import tpu_sc as plsc

F32 = jnp.float32
BF16 = jnp.bfloat16
MESH = pl.DeviceIdType.MESH
HIGHEST = lax.Precision.HIGHEST

D_MODEL = 1024
DEPTH = 2
FFN_HIDDEN = 2816
SSD_D_INNER = 2048
SSD_HEADS = 32
SSD_HEAD_DIM = 64
SSD_GROUPS = 4
SSD_STATE = 128
SSD_CHUNK = 128
SSD_CONV_DIM = 3072
SSD_CONV_WIDTH = 4
S5_WIDTH = 1024
S5_GROUP = 16
S5_GROUPS = 64
S5_STATE = 64
S5_MAX_REAL = -1e-4
S5_BLOCKS = 8
RMS_EPS = 1e-6
N_DEV = 8
LANES = 1024

ADAM_LR = 0.001
ADAM_B1 = 0.9
ADAM_B2 = 0.999
ADAM_EPS = 1e-08
ADAM_WD = 0.01
ADAM_STEP = 10

VMEM_LIMIT_BYTES = 48 * 1024 * 1024

WEIGHTS = ['ffn1_pre_g', 'ffn1_post_g', 'ffn1_w_gate', 'ffn1_w_up', 'ffn1_w_down', 'mix_pre_g', 'mix_post_g',
           'w_in', 'ssd_conv_w', 'ssd_conv_b', 'ssd_dt_bias', 'ssd_a_log', 'ssd_d', 'ssd_norm_g', 'w_branch_a',
           's5_lambda_re', 's5_lambda_im', 's5_b_re', 's5_b_im', 's5_c_re', 's5_c_im', 's5_log_step', 's5_d',
           's5_w_glu', 'w_branch_b', 'w_out', 'ffn2_pre_g', 'ffn2_post_g', 'ffn2_w_gate', 'ffn2_w_up',
           'ffn2_w_down']
SHARDED = {'ffn1_w_gate': 2, 'ffn1_w_up': 2, 'ffn1_w_down': 1, 'w_in': 2, 'ssd_conv_w': 2, 'w_branch_a': 1,
           's5_w_glu': 2, 'w_branch_b': 1, 'w_out': 1, 'ffn2_w_gate': 2, 'ffn2_w_up': 2, 'ffn2_w_down': 1}
SHARDED_ORDER = [n for n in WEIGHTS if n in SHARDED]
SMALL_ORDER = [n for n in WEIGHTS if n not in SHARDED or n == 'ssd_conv_w']


def _pcall(body, **kw):
    return pl.pallas_call(body, **kw)


def _scall(body, *, name, out_type, scratch_types, collective_id):
    return pl.kernel(body, out_type=out_type, mesh=plsc.ScalarSubcoreMesh(axis_name="sequencer", num_cores=1),
                     scratch_types=scratch_types, name=name,
                     compiler_params=pltpu.CompilerParams(collective_id=collective_id))


def _params(*sem):
    return pltpu.CompilerParams(dimension_semantics=sem, vmem_limit_bytes=VMEM_LIMIT_BYTES)


def _tile(n, pref, align=128):
    if n <= pref:
        return n
    t = (pref // align) * align
    while t >= align:
        if n % t == 0:
            return t
        t -= align
    return n


def _rms(x, g):
    return x * lax.rsqrt(jnp.mean(x * x, axis=-1, keepdims=True) + RMS_EPS) * g


def _sigmoid(x):
    return 1.0 / (1.0 + jnp.exp(-x))


def _silu(x):
    return x * _sigmoid(x)


def _gelu(x):
    return 0.5 * x * (1.0 + jnp.tanh(math.sqrt(2.0 / math.pi) * (x + 0.044715 * (x * x * x))))


def _softplus(x):
    return jnp.maximum(x, 0.0) + jnp.log(1.0 + jnp.exp(-jnp.abs(x)))


def _dot(a, b, dims):
    return lax.dot_general(a.astype(BF16), b.astype(BF16), (dims, ((), ())), preferred_element_type=F32)


_NN = ((1,), (0,))
_NT = ((1,), (1,))
_TN = ((0,), (0,))


@jax.custom_vjp
def _bdot_nn(a, b):
    return _dot(a, b, _NN)


_bdot_nn.defvjp(lambda a, b: (_dot(a, b, _NN), (a, b)),
                lambda r, g: (_dot(g, r[1], _NT), _dot(r[0], g, _TN)))


@jax.custom_vjp
def _bdot_nt(a, b):
    return _dot(a, b, _NT)


_bdot_nt.defvjp(lambda a, b: (_dot(a, b, _NT), (a, b)),
                lambda r, g: (_dot(g, r[1], _NN), _dot(g, r[0], _TN)))


@jax.custom_vjp
def _bdot_tn(a, b):
    return _dot(a, b, _TN)


_bdot_tn.defvjp(lambda a, b: (_dot(a, b, _TN), (a, b)),
                lambda r, g: (_dot(r[1], g, _NT), _dot(r[0], g, _NN)))


def _fdot(a, b, dims=_NN):
    return lax.dot_general(a, b, (dims, ((), ())), precision=HIGHEST, preferred_element_type=F32)


def _sel3(x, sel, dims, x_first):
    p1 = x.astype(BF16)
    r1 = x - p1.astype(F32)
    p2 = r1.astype(BF16)
    p3 = (r1 - p2.astype(F32)).astype(BF16)
    sel = sel.astype(BF16)
    out = None
    for piece in (p1, p2, p3):
        d = lax.dot_general(*((piece, sel) if x_first else (sel, piece)), (dims, ((), ())), preferred_element_type=F32)
        out = d if out is None else out + d
    return out


@jax.custom_vjp
def _sel_right(x, sel):
    return _sel3(x, sel, _NN, True)


_sel_right.defvjp(lambda x, sel: (_sel3(x, sel, _NN, True), sel),
                  lambda sel, g: (_sel3(g, sel, _NT, True), jnp.zeros_like(sel)))


@jax.custom_vjp
def _sel_left(sel, x):
    return _sel3(x, sel, _NN, False)


_sel_left.defvjp(lambda sel, x: (_sel3(x, sel, _NN, False), sel),
                 lambda sel, g: (jnp.zeros_like(sel), _sel3(g, sel, _TN, False)))


@jax.custom_vjp
def _sel_left_nt(sel, x):
    return _sel3(x, sel, _NT, False)


_sel_left_nt.defvjp(lambda sel, x: (_sel3(x, sel, _NT, False), sel),
                    lambda sel, g: (jnp.zeros_like(sel), _sel3(g, sel, _TN, True)))


def _mm(a, b, *, name, ta=False, tb=False, out_dtype=F32, tm=2048, tn=512, tk=2048, shards=None):
    m, k = (a.shape[1], a.shape[0]) if ta else a.shape
    n = b.shape[0] if tb else b.shape[1]
    assert k == (b.shape[1] if tb else b.shape[0]), (a.shape, b.shape, ta, tb)
    if shards == 'rows':
        tm = min(tm, m // N_DEV)
    if shards == 'cols':
        tn = n // N_DEV
    tm, tn, tk = _tile(m, tm), _tile(n, tn), _tile(k, tk)
    nk = k // tk
    a_spec = pl.BlockSpec((tk, tm), lambda i, j, kk: (kk, i)) if ta else pl.BlockSpec((tm, tk), lambda i, j, kk: (i, kk))
    b_spec = pl.BlockSpec((tn, tk), lambda i, j, kk: (j, kk)) if tb else pl.BlockSpec((tk, tn), lambda i, j, kk: (kk, j))
    dims = ((0 if ta else 1,), (1 if tb else 0,))
    out_spec = pl.BlockSpec((tm, tn), lambda i, j, kk: (i, j))
    out_shape = jax.ShapeDtypeStruct((m, n), out_dtype)
    if shards == 'rows':
        per = m // N_DEV // tm
        out_shape = jax.ShapeDtypeStruct((N_DEV, 1, m // N_DEV, n), out_dtype)
        out_spec = pl.BlockSpec((None, None, tm, tn), lambda i, j, kk: (i // per, 0, i % per, j))
    elif shards == 'cols':
        out_shape = jax.ShapeDtypeStruct((N_DEV, 1, m, n // N_DEV), out_dtype)
        out_spec = pl.BlockSpec((None, None, tm, tn), lambda i, j, kk: (j, 0, i, 0))

    def body(a_ref, b_ref, o_ref, acc_ref):
        kk = pl.program_id(2)

        @pl.when(kk == 0)
        def _():
            acc_ref[...] = jnp.zeros_like(acc_ref)

        acc_ref[...] += _dot(a_ref[...], b_ref[...], dims)

        @pl.when(kk == nk - 1)
        def _():
            o_ref[...] = acc_ref[...].astype(o_ref.dtype)

    return _pcall(
        body, name=name, grid=(m // tm, n // tn, nk),
        in_specs=[a_spec, b_spec], out_specs=out_spec, out_shape=out_shape,
        scratch_shapes=[pltpu.VMEM((tm, tn), F32)],
        compiler_params=_params("parallel", "parallel", "arbitrary"),
    )(a, b)


def _rspec(tm, w, cb=0, percol=False):
    return pl.BlockSpec((tm, w), (lambda j, i: (i, cb + j)) if percol else (lambda j, i: (i, cb)))


def _bspec(w, cb=0, percol=False, rows=1):
    return pl.BlockSpec((rows, w), (lambda j, i: (0, cb + j)) if percol else (lambda j, i: (0, cb)))


def _rows(fn, *, name, nrow, ncol=1, ins, outs):
    n_in = len(ins)
    accs = [o[2] for o in outs]

    def body(*refs):
        vals = fn(*[r[...] for r in refs[:n_in]])
        if not isinstance(vals, (tuple, list)):
            vals = (vals,)
        i = pl.program_id(1)
        for ref, val, acc in zip(refs[n_in:], vals, accs):
            if acc:
                @pl.when(i == 0)
                def _(ref=ref):
                    ref[...] = jnp.zeros_like(ref)

                ref[...] += jnp.broadcast_to(val, ref.shape).astype(ref.dtype)
            else:
                ref[...] = val.astype(ref.dtype)

    res = _pcall(
        body, name=name, grid=(ncol, nrow),
        in_specs=[s for _, s in ins], out_specs=[o[1] for o in outs], out_shape=[o[0] for o in outs],
        compiler_params=_params("parallel", "arbitrary"),
    )(*[a for a, _ in ins])
    return res


def _sds(shape, dtype=F32):
    return jax.ShapeDtypeStruct(shape, dtype)


def _rms_fwd(h, g, *, name, tm):
    t, d = h.shape
    return _rows(lambda x, gg: _rms(x, gg), name=name, nrow=t // tm,
                 ins=[(h, _rspec(tm, d)), (g, _bspec(d))],
                 outs=[(_sds((t, d), BF16), _rspec(tm, d), False)])[0]


def _resid_fwd(h, f, g, scale, *, name, tm):
    t, d = h.shape
    return _rows(lambda x, ff, gg: x + scale * _rms(ff, gg), name=name, nrow=t // tm,
                 ins=[(h, _rspec(tm, d)), (f, _rspec(tm, d)), (g, _bspec(d))],
                 outs=[(_sds((t, d)), _rspec(tm, d), False)])[0]


def _resid_bwd(f, g, dh, scale, *, name, tm):
    t, d = f.shape

    def fn(ff, gg, dd):
        _, vjp = jax.vjp(lambda a, b: scale * _rms(a, b), ff, gg)
        return vjp(dd)

    return _rows(fn, name=name, nrow=t // tm,
                 ins=[(f, _rspec(tm, d)), (g, _bspec(d)), (dh, _rspec(tm, d))],
                 outs=[(_sds((t, d), BF16), _rspec(tm, d), False), (_sds((1, d)), _bspec(d), True)])


def _rms_bwd(h, g, dh, dxns, *, name, tm):
    t, d = h.shape

    def fn(x, gg, dd, *dx):
        _, vjp = jax.vjp(_rms, x, gg)
        tot = dx[0]
        for more in dx[1:]:
            tot = tot + more
        dxx, dg = vjp(tot)
        return dd + dxx, dg

    return _rows(fn, name=name, nrow=t // tm,
                 ins=[(h, _rspec(tm, d)), (g, _bspec(d)), (dh, _rspec(tm, d))] + [(x, _rspec(tm, d)) for x in dxns],
                 outs=[(_sds((t, d)), _rspec(tm, d), False), (_sds((1, d)), _bspec(d), True)])


FFN_BLOCKS = 4
NB = FFN_HIDDEN // FFN_BLOCKS
MM_ROWS = 2048


def _ffn_up(xn, wg, wu, *, name):
    t = xn.shape[0]
    tm = _tile(t, MM_ROWS // 2)
    wspec = pl.BlockSpec((None, None, NB, D_MODEL), lambda i, j: (j, 0, 0, 0))

    def body(x_ref, g_ref, u_ref, ab_ref, hh_ref):
        x = x_ref[...]
        a, b = _dot(x, g_ref[...], _NT), _dot(x, u_ref[...], _NT)
        ab_ref[0] = a.astype(ab_ref.dtype)
        ab_ref[1] = b.astype(ab_ref.dtype)
        hh_ref[...] = (_silu(a) * b).astype(hh_ref.dtype)

    return _pcall(
        body, name=name, grid=(t // tm, FFN_BLOCKS),
        in_specs=[pl.BlockSpec((tm, D_MODEL), lambda i, j: (i, 0)), wspec, wspec],
        out_specs=[pl.BlockSpec((None, 2, tm, NB), lambda i, j: (j, 0, i, 0)),
                   pl.BlockSpec((None, tm, NB), lambda i, j: (j, i, 0))],
        out_shape=[_sds((FFN_BLOCKS, 2, t, NB), BF16), _sds((FFN_BLOCKS, t, NB), BF16)],
        compiler_params=_params("parallel", "parallel"),
    )(xn, wg, wu)


def _ffn_down(hh, wd, *, name):
    t = hh.shape[1]
    tm = _tile(t, 512)

    def body(h_ref, w_ref, o_ref):
        acc = _dot(h_ref[0], w_ref[0, 0], _NN)
        for k in range(1, FFN_BLOCKS):
            acc = acc + _dot(h_ref[k], w_ref[k, 0], _NN)
        o_ref[...] = acc

    return _pcall(
        body, name=name, grid=(t // tm,),
        in_specs=[pl.BlockSpec((FFN_BLOCKS, tm, NB), lambda i: (0, i, 0)),
                  pl.BlockSpec((FFN_BLOCKS, 1, NB, D_MODEL), lambda i: (0, 0, 0, 0))],
        out_specs=pl.BlockSpec((tm, D_MODEL), lambda i: (i, 0)), out_shape=_sds((t, D_MODEL)),
        compiler_params=_params("parallel"),
    )(hh, wd)


def _ffn_down_dx(df, wd, ab, *, name):
    t = df.shape[0]
    tm = _tile(t, MM_ROWS // 2)

    def body(d_ref, w_ref, ab_ref, o_ref):
        dhh = _dot(d_ref[...], w_ref[...], _NT)
        _, vjp = jax.vjp(lambda a, b: _silu(a) * b, ab_ref[0].astype(F32), ab_ref[1].astype(F32))
        da, db = vjp(dhh)
        o_ref[0] = da.astype(o_ref.dtype)
        o_ref[1] = db.astype(o_ref.dtype)

    blk = pl.BlockSpec((None, 2, tm, NB), lambda i, j: (j, 0, i, 0))
    return _pcall(
        body, name=name, grid=(t // tm, FFN_BLOCKS),
        in_specs=[pl.BlockSpec((tm, D_MODEL), lambda i, j: (i, 0)),
                  pl.BlockSpec((None, None, NB, D_MODEL), lambda i, j: (j, 0, 0, 0)), blk],
        out_specs=blk, out_shape=_sds((FFN_BLOCKS, 2, t, NB), BF16), compiler_params=_params("parallel", "parallel"),
    )(df, wd, ab)


def _ffn_down_dw(hh, df, *, name, tn=512):
    t = df.shape[0]
    tk = _tile(t, 2048)
    nk = t // tk

    def body(h_ref, d_ref, o_ref, acc_ref):
        kk = pl.program_id(2)

        @pl.when(kk == 0)
        def _():
            acc_ref[...] = jnp.zeros_like(acc_ref)

        acc_ref[...] += _dot(h_ref[...], d_ref[...], _TN)

        @pl.when(kk == nk - 1)
        def _():
            o_ref[...] = acc_ref[...].astype(o_ref.dtype)

    return _pcall(
        body, name=name, grid=(FFN_BLOCKS, D_MODEL // tn, nk),
        in_specs=[pl.BlockSpec((None, tk, NB), lambda j, n, kk: (j, kk, 0)),
                  pl.BlockSpec((tk, tn), lambda j, n, kk: (kk, n))],
        out_specs=pl.BlockSpec((None, None, NB, tn), lambda j, n, kk: (j, 0, 0, n)),
        out_shape=_sds((FFN_BLOCKS, 1, NB, D_MODEL), BF16),
        scratch_shapes=[pltpu.VMEM((NB, tn), F32)],
        compiler_params=_params("parallel", "parallel", "arbitrary"),
    )(hh, df)


def _ffn_up_dx(dab, wg, wu, *, name):
    t = dab.shape[2]
    tm = _tile(t, MM_ROWS // 2)
    wspec = pl.BlockSpec((None, None, NB, D_MODEL), lambda i, j: (j, 0, 0, 0))

    def body(d_ref, g_ref, u_ref, o_ref):
        @pl.when(pl.program_id(1) == 0)
        def _():
            o_ref[...] = jnp.zeros_like(o_ref)

        o_ref[...] += _dot(d_ref[0], g_ref[...], _NN) + _dot(d_ref[1], u_ref[...], _NN)

    return _pcall(
        body, name=name, grid=(t // tm, FFN_BLOCKS),
        in_specs=[pl.BlockSpec((None, 2, tm, NB), lambda i, j: (j, 0, i, 0)), wspec, wspec],
        out_specs=pl.BlockSpec((tm, D_MODEL), lambda i, j: (i, 0)), out_shape=_sds((t, D_MODEL)),
        compiler_params=_params("parallel", "arbitrary"),
    )(dab, wg, wu)


def _ffn_up_dw(xn, dab, *, name):
    t = xn.shape[0]

    def body(x_ref, d_ref, og_ref, ou_ref):
        x = x_ref[...]
        og_ref[...] = _dot(d_ref[0], x, _TN).astype(og_ref.dtype)
        ou_ref[...] = _dot(d_ref[1], x, _TN).astype(ou_ref.dtype)

    out = pl.BlockSpec((None, None, NB, D_MODEL), lambda j: (j, 0, 0, 0))
    return _pcall(
        body, name=name, grid=(FFN_BLOCKS,),
        in_specs=[pl.BlockSpec((t, D_MODEL), lambda j: (0, 0)), pl.BlockSpec((None, 2, t, NB), lambda j: (j, 0, 0, 0))],
        out_specs=[out, out], out_shape=[_sds((FFN_BLOCKS, 1, NB, D_MODEL), BF16)] * 2,
        compiler_params=_params("parallel"),
    )(xn, dab)


def _paired(a):
    return a.reshape(FFN_BLOCKS, 1, NB, D_MODEL)


def _ffn_fwd(h, p, tag, tm):
    xn = _rms_fwd(h, p['pre_g'], name=f"{tag}_rms", tm=tm)
    ab, hh = _ffn_up(xn, _paired(p['w_gate']), _paired(p['w_up']), name=f"{tag}_up")
    w_down, hh = lax.optimization_barrier((p['w_down'], hh))
    f = _ffn_down(hh, _paired(w_down), name=f"{tag}_down")
    out = _resid_fwd(h, f, p['post_g'], 0.5, name=f"{tag}_res", tm=tm)
    return out, (h, xn, ab, hh, f)


def _ffn_bwd(dh, p, saved, tag, tm, after_first):
    h, xn, ab, hh, f = saved
    df, dpost = _resid_bwd(f, p['post_g'], dh, 0.5, name=f"{tag}_res_bwd", tm=tm)
    df = after_first(df)
    dab = _ffn_down_dx(df, _paired(p['w_down']), ab, name=f"{tag}_down_dx")
    bufs = {f'{tag}_w_down': _ffn_down_dw(hh, df, name=f"{tag}_down_dw")}
    dxn = _ffn_up_dx(dab, _paired(p['w_gate']), _paired(p['w_up']), name=f"{tag}_up_dx")
    bufs[f'{tag}_w_gate'], bufs[f'{tag}_w_up'] = _ffn_up_dw(xn, dab, name=f"{tag}_up_dw")
    bufs = {n: a.reshape(N_DEV, 1, FFN_HIDDEN // N_DEV, D_MODEL) for n, a in bufs.items()}
    dh_in, dpre = _rms_bwd(h, p['pre_g'], dh, [dxn], name=f"{tag}_rms_bwd", tm=tm)
    return dh_in, bufs, {f'{tag}_pre_g': dpre, f'{tag}_post_g': dpost}


CONV_COLS = 256


def _shift_down(x, s):
    rows = lax.broadcasted_iota(jnp.int32, x.shape, 0)
    return jnp.where(rows >= s, pltpu.roll(x, s, axis=0), 0.0)


def _shift_up(x, s):
    t = x.shape[0]
    rows = lax.broadcasted_iota(jnp.int32, x.shape, 0)
    return jnp.where(rows < t - s, pltpu.roll(x, t - s, axis=0), 0.0)


def _conv_fwd(proj, col0, w, b, *, name):
    t = proj.shape[0]
    c = w.shape[1]
    cb0 = col0 // CONV_COLS

    def body(x_ref, w_ref, b_ref, o_ref):
        x = x_ref[...]
        acc = x * w_ref[3:4, :] + b_ref[...]
        for k in range(SSD_CONV_WIDTH - 1):
            acc = acc + _shift_down(x, SSD_CONV_WIDTH - 1 - k) * w_ref[k:k + 1, :]
        o_ref[...] = _silu(acc)

    return _pcall(
        body, name=name, grid=(c // CONV_COLS,),
        in_specs=[pl.BlockSpec((t, CONV_COLS), lambda j: (0, cb0 + j)),
                  pl.BlockSpec((SSD_CONV_WIDTH, CONV_COLS), lambda j: (0, j)),
                  pl.BlockSpec((1, CONV_COLS), lambda j: (0, j))],
        out_specs=pl.BlockSpec((t, CONV_COLS), lambda j: (0, j)),
        out_shape=_sds((t, c)), compiler_params=_params("parallel"),
    )(proj, w, b)


def _conv_bwd(proj, col0, w, b, douts, *, name):
    t = proj.shape[0]
    c = w.shape[1]
    cb0 = col0 // CONV_COLS
    first = np.cumsum([0] + [d.shape[1] // CONV_COLS for d in douts])

    def body(x_ref, w_ref, b_ref, *rest):
        d_refs, (dx_ref, dw_ref, db_ref) = rest[:len(douts)], rest[len(douts):]
        j = pl.program_id(0)
        dout = d_refs[-1][...]
        for k in range(len(douts) - 2, -1, -1):
            dout = jnp.where(j < int(first[k + 1]), d_refs[k][...], dout)
        x = x_ref[...]
        shifted = [_shift_down(x, SSD_CONV_WIDTH - 1 - k) for k in range(SSD_CONV_WIDTH - 1)] + [x]
        pre = b_ref[...] + shifted[3] * w_ref[3:4, :]
        for k in range(SSD_CONV_WIDTH - 1):
            pre = pre + shifted[k] * w_ref[k:k + 1, :]
        sg = _sigmoid(pre)
        dpre = dout * (sg * (1.0 + pre * (1.0 - sg)))
        dx = dpre * w_ref[3:4, :]
        for k in range(SSD_CONV_WIDTH - 1):
            dx = dx + _shift_up(dpre, SSD_CONV_WIDTH - 1 - k) * w_ref[k:k + 1, :]
        dx_ref[...] = dx.astype(dx_ref.dtype)
        for k in range(SSD_CONV_WIDTH):
            dw_ref[k:k + 1, :] = jnp.sum(dpre * shifted[k], axis=0, keepdims=True)
        db_ref[...] = jnp.sum(dpre, axis=0, keepdims=True)

    return _pcall(
        body, name=name, grid=(c // CONV_COLS,),
        in_specs=[pl.BlockSpec((t, CONV_COLS), lambda j: (0, cb0 + j)),
                  pl.BlockSpec((SSD_CONV_WIDTH, CONV_COLS), lambda j: (0, j)),
                  pl.BlockSpec((1, CONV_COLS), lambda j: (0, j))] +
                 [pl.BlockSpec((t, CONV_COLS), lambda j, lo=int(first[k]), hi=int(first[k + 1]): (0, jnp.clip(j, lo, hi - 1) - lo))
                  for k in range(len(douts))],
        out_specs=[pl.BlockSpec((t, CONV_COLS), lambda j: (0, j)),
                   pl.BlockSpec((SSD_CONV_WIDTH, CONV_COLS), lambda j: (0, j)),
                   pl.BlockSpec((1, CONV_COLS), lambda j: (0, j))],
        out_shape=[_sds((t, c), BF16), _sds((SSD_CONV_WIDTH, c)), _sds((1, c))],
        compiler_params=_params("arbitrary"),
    )(proj, w, b, *douts)


HALF = 256
HEADS_PER_HALF = 4
PAD_HEADS = 128


def _head_expanders():
    k = lax.broadcasted_iota(jnp.int32, (PAD_HEADS, HALF), 0)
    j = lax.broadcasted_iota(jnp.int32, (PAD_HEADS, HALF), 1)
    kt = lax.broadcasted_iota(jnp.int32, (HALF, PAD_HEADS), 1)
    jt = lax.broadcasted_iota(jnp.int32, (HALF, PAD_HEADS), 0)
    es, ets = [], []
    for half in range(2):
        es.append(jnp.where(k == j // SSD_HEAD_DIM + half * HEADS_PER_HALF, 1.0, 0.0).astype(F32))
        ets.append(jnp.where(kt == jt // SSD_HEAD_DIM + half * HEADS_PER_HALF, 1.0, 0.0).astype(F32))
    return es, ets


def _ssd_chunk(x_lo, x_hi, bm, cm, dtr, dtb8, alog8, dsk8, s_lo, s_hi):
    q = x_lo.shape[0]
    es, ets = _head_expanders()
    rowmean = lambda v: jnp.sum(v, axis=0, keepdims=True) * 0.125
    dt = _softplus(dtr + rowmean(dtb8))
    a = -jnp.exp(rowmean(alog8))
    adt = a * dt
    adt_tot8 = jnp.broadcast_to(jnp.sum(adt, axis=0, keepdims=True), (8, PAD_HEADS))
    ll = lax.broadcasted_iota(jnp.int32, (q, q), 0)
    ss = lax.broadcasted_iota(jnp.int32, (q, q), 1)
    ltri = jnp.where(ll >= ss, 1.0, 0.0).astype(F32)
    lane = lax.broadcasted_iota(jnp.int32, (1, HALF), 1)
    cb = _bdot_nt(cm, bm)
    outs = []
    for half, (x, s_in) in enumerate(((x_lo, s_lo), (x_hi, s_hi))):
        e, et = es[half], ets[half]
        dtf = _sel_right(dt, e)
        af = rowmean(_sel_right(jnp.broadcast_to(a, (8, PAD_HEADS)), e)) * dtf
        dskf = rowmean(_sel_right(dsk8, e))
        acum = _sel_left(ltri, af)
        alast = jnp.sum(af, axis=0, keepdims=True)
        xdt = x * dtf
        ydiag = jnp.zeros((q, HALF), F32)
        for r in range(HEADS_PER_HALF):
            sel = lane == r * SSD_HEAD_DIM
            ac_r = jnp.sum(jnp.where(sel, acum, 0.0), axis=1, keepdims=True)
            a_r = jnp.sum(jnp.where(sel, af, 0.0), axis=1, keepdims=True)
            arow = jnp.sum(jnp.where(ll <= ss, a_r, 0.0), axis=0, keepdims=True)
            decay = jnp.exp(jnp.where(ll >= ss, ac_r - arow, -jnp.inf))
            yh = _bdot_nn(cb * decay, xdt)
            ydiag = ydiag + jnp.where(lane // SSD_HEAD_DIM == r, yh, 0.0)
        st = _bdot_tn(xdt * jnp.exp(alast - acum), bm)
        yoff = _bdot_nt(cm, s_in) * jnp.exp(acum)
        y = ydiag + yoff + dskf * x
        alast_col = jnp.sum(_sel_left_nt(et, adt_tot8), axis=1, keepdims=True) * 0.125
        outs.append((y, jnp.exp(alast_col) * s_in + st))
    return outs[0][0], outs[1][0], outs[0][1], outs[1][1]


def _ssd_specs(t, rev):
    q = SSD_CHUNK
    nc = t // q
    ci = (lambda c: nc - 1 - c) if rev else (lambda c: c)
    xcol0 = SSD_D_INNER // SSD_STATE
    return dict(
        x_lo=pl.BlockSpec((q, HALF), lambda g, c: (ci(c), 2 * g)),
        x_hi=pl.BlockSpec((q, HALF), lambda g, c: (ci(c), 2 * g + 1)),
        bm=pl.BlockSpec((q, SSD_STATE), lambda g, c: (ci(c), xcol0 + g)),
        cm=pl.BlockSpec((q, SSD_STATE), lambda g, c: (ci(c), xcol0 + SSD_GROUPS + g)),
        dt=pl.BlockSpec((None, q, PAD_HEADS), lambda g, c: (g, ci(c), 0)),
        par=pl.BlockSpec((None, 8, PAD_HEADS), lambda g, c: (g, 0, 0)),
        st=pl.BlockSpec((None, None, 2, HALF, SSD_STATE), lambda g, c: (ci(c), g, 0, 0, 0)),
        y=pl.BlockSpec((q, 2 * HALF), lambda g, c: (ci(c), g)),
        grp=pl.BlockSpec((q, SSD_STATE), lambda g, c: (ci(c), g)),
    )


def _ssd_fwd(xc, dt4, dtb, alog, dsk, *, name):
    t = xc.shape[0]
    nc = t // SSD_CHUNK
    sp = _ssd_specs(t, False)

    def body(xl, xh, bm, cm, dt, p0, p1, p2, y_ref, sin_ref, st_ref):
        @pl.when(pl.program_id(1) == 0)
        def _():
            st_ref[...] = jnp.zeros_like(st_ref)

        sin_ref[...] = st_ref[...]
        y_lo, y_hi, so_lo, so_hi = _ssd_chunk(xl[...], xh[...], bm[...], cm[...], dt[...], p0[...], p1[...],
                                              p2[...], st_ref[0], st_ref[1])
        y_ref[:, :HALF] = y_lo
        y_ref[:, HALF:] = y_hi
        st_ref[0] = so_lo
        st_ref[1] = so_hi

    return _pcall(
        body, name=name, grid=(SSD_GROUPS, nc),
        in_specs=[sp['x_lo'], sp['x_hi'], sp['bm'], sp['cm'], sp['dt'], sp['par'], sp['par'], sp['par']],
        out_specs=[sp['y'], sp['st']],
        out_shape=[_sds((t, SSD_D_INNER)), _sds((nc, SSD_GROUPS, 2, HALF, SSD_STATE))],
        scratch_shapes=[pltpu.VMEM((2, HALF, SSD_STATE), F32)],
        compiler_params=_params("parallel", "arbitrary"),
    )(xc, xc, xc, xc, dt4, dtb, alog, dsk)


def _ssd_bwd(xc, dt4, dtb, alog, dsk, sin, dy, *, name):
    t = xc.shape[0]
    nc = t // SSD_CHUNK
    sp = _ssd_specs(t, True)

    def body(xl, xh, bm, cm, dt, p0, p1, p2, sin_ref, dy_ref,
             dx_ref, db_ref, dc_ref, ddt_ref, dp0, dp1, dp2, dst_ref):
        first = pl.program_id(1) == 0

        @pl.when(first)
        def _():
            dst_ref[...] = jnp.zeros_like(dst_ref)

        _, vjp = jax.vjp(_ssd_chunk, xl[...], xh[...], bm[...], cm[...], dt[...], p0[...], p1[...], p2[...],
                         sin_ref[0], sin_ref[1])
        dxl, dxh, dbm, dcm, ddt, g0, g1, g2, ds_lo, ds_hi = vjp(
            (dy_ref[:, :HALF], dy_ref[:, HALF:], dst_ref[0], dst_ref[1]))
        dx_ref[:, :HALF] = dxl
        dx_ref[:, HALF:] = dxh
        db_ref[...] = dbm
        dc_ref[...] = dcm
        ddt_ref[...] = ddt
        dst_ref[0] = ds_lo
        dst_ref[1] = ds_hi
        for ref, g in ((dp0, g0), (dp1, g1), (dp2, g2)):
            tot = jnp.broadcast_to(jnp.sum(g, axis=0, keepdims=True), ref.shape)

            @pl.when(first)
            def _(ref=ref):
                ref[...] = jnp.zeros_like(ref)

            ref[...] += tot

    return _pcall(
        body, name=name, grid=(SSD_GROUPS, nc),
        in_specs=[sp['x_lo'], sp['x_hi'], sp['bm'], sp['cm'], sp['dt'], sp['par'], sp['par'], sp['par'],
                  sp['st'], sp['y']],
        out_specs=[sp['y'], sp['grp'], sp['grp'], sp['dt'], sp['par'], sp['par'], sp['par']],
        out_shape=[_sds((t, SSD_D_INNER)), _sds((t, SSD_GROUPS * SSD_STATE)), _sds((t, SSD_GROUPS * SSD_STATE)),
                   _sds((SSD_GROUPS, t, PAD_HEADS))] + [_sds((SSD_GROUPS, 8, PAD_HEADS))] * 3,
        scratch_shapes=[pltpu.VMEM((2, HALF, SSD_STATE), F32)],
        compiler_params=_params("parallel", "arbitrary"),
    )(xc, xc, xc, xc, dt4, dtb, alog, dsk, sin, dy)


def _gatenorm(y, z, g):
    v = y * _silu(z)
    return v * lax.rsqrt(jnp.mean(v * v, axis=-1, keepdims=True) + RMS_EPS) * g


S5_CH = S5_WIDTH // S5_BLOCKS
S5_ST = S5_CH * S5_STATE // S5_GROUP
SCAN_UNROLL = 8


def _cmul(ar, ai, br, bi):
    return ar * br - ai * bi, ar * bi + ai * br


def _segment_power(ar, ai, n):
    assert n & (n - 1) == 0
    for _ in range(n.bit_length() - 1):
        ar, ai = _cmul(ar, ai, ar, ai)
    return ar, ai


def _carry_in(fr, fi, pr, pi, reverse):
    rows = lax.broadcasted_iota(jnp.int32, fr.shape, 0)
    cr = jnp.zeros_like(fr[0:1])
    ci = jnp.zeros_like(cr)
    outr = jnp.zeros_like(fr)
    outi = jnp.zeros_like(fr)
    order = range(6, -1, -1) if reverse else range(1, 8)
    for j in order:
        src = j + 1 if reverse else j - 1
        nr, ni = _cmul(pr[0:1], pi[0:1], cr, ci)
        cr, ci = nr + fr[src:src + 1], ni + fi[src:src + 1]
        outr = jnp.where(rows == j, cr, outr)
        outi = jnp.where(rows == j, ci, outi)
    return outr, outi


def _s5_specs(t):
    return dict(ch=pl.BlockSpec((t, S5_CH), lambda j: (0, j)), st=pl.BlockSpec((t, S5_ST), lambda j: (0, j)),
                lam=pl.BlockSpec((1, S5_ST), lambda j: (0, j)),
                b=pl.BlockSpec((None, S5_CH, S5_ST), lambda j: (j, 0, 0)),
                c=pl.BlockSpec((None, S5_ST, S5_CH), lambda j: (j, 0, 0)))


def _s5_fwd(u5, bre, bim, cre, cim, lr, li, *, name):
    t = u5.shape[0]
    nrt = t // 8

    def body(u_ref, bre_ref, bim_ref, cre_ref, cim_ref, lr_ref, li_ref, sr_ref, si_ref, y_ref, br_ref, bi_ref):
        u = u_ref[...]
        br_ref[...] = _dot(u, bre_ref[...], _NN)
        bi_ref[...] = _dot(u, bim_ref[...], _NN)
        ar = jnp.broadcast_to(lr_ref[...], (8, S5_ST))
        ai = jnp.broadcast_to(li_ref[...], (8, S5_ST))

        def step(r, s, store):
            rows = pl.ds(pl.multiple_of(r * 8, 8), 8)
            nr, ni = _cmul(ar, ai, s[0], s[1])
            nr, ni = nr + br_ref[rows, :], ni + bi_ref[rows, :]
            if store:
                sr_ref[rows, :] = nr
                si_ref[rows, :] = ni
            return nr, ni

        zero = (jnp.zeros((8, S5_ST), F32), jnp.zeros((8, S5_ST), F32))
        fr, fi = lax.fori_loop(0, nrt, lambda r, s: step(r, s, False), zero, unroll=SCAN_UNROLL)
        pr, pi = _segment_power(ar, ai, nrt)
        init = _carry_in(fr, fi, pr, pi, False)
        lax.fori_loop(0, nrt, lambda r, s: step(r, s, True), init, unroll=SCAN_UNROLL)
        y_ref[...] = _dot(sr_ref[...], cre_ref[...], _NN) - _dot(si_ref[...], cim_ref[...], _NN)

    sp = _s5_specs(t)
    w = S5_BLOCKS * S5_ST
    return _pcall(
        body, name=name, grid=(S5_BLOCKS,),
        in_specs=[sp['ch'], sp['b'], sp['b'], sp['c'], sp['c'], sp['lam'], sp['lam']],
        out_specs=[sp['st'], sp['st'], sp['ch']], out_shape=[_sds((t, w)), _sds((t, w)), _sds((t, S5_WIDTH))],
        scratch_shapes=[pltpu.VMEM((t, S5_ST), F32)] * 2, compiler_params=_params("parallel"),
    )(u5, bre, bim, cre, cim, lr, li)


def _s5_bwd(dy, du_direct, u5, sr, si, bre, bim, cre, cim, lr, li, *, name):
    t = u5.shape[0]
    nrt = t // 8

    def body(dy_ref, dd_ref, u_ref, sr_ref, si_ref, bre_ref, bim_ref, cre_ref, cim_ref, lr_ref, li_ref,
             du_ref, dbre_ref, dbim_ref, dcre_ref, dcim_ref, dlr_ref, dli_ref, gr_ref, gi_ref):
        dyv = dy_ref[...]
        gr_ref[...] = _dot(dyv, cre_ref[...], _NT)
        gi_ref[...] = -_dot(dyv, cim_ref[...], _NT)
        dcre_ref[...] = _dot(sr_ref[...], dyv, _TN)
        dcim_ref[...] = -_dot(si_ref[...], dyv, _TN)
        dr_ref, di_ref = gr_ref, gi_ref
        ar = jnp.broadcast_to(lr_ref[...], (8, S5_ST))
        ai = -jnp.broadcast_to(li_ref[...], (8, S5_ST))
        zero = jnp.zeros((8, S5_ST), F32)

        def step1(k, g):
            rows = pl.ds(pl.multiple_of((nrt - 1 - k) * 8, 8), 8)
            nr, ni = _cmul(ar, ai, g[0], g[1])
            return nr + dr_ref[rows, :], ni + di_ref[rows, :]

        fr, fi = lax.fori_loop(0, nrt, step1, (zero, zero), unroll=SCAN_UNROLL)
        pr, pi = _segment_power(ar, ai, nrt)
        init = _carry_in(fr, fi, pr, pi, True)

        def step2(k, carry):
            gr, gi, accr, acci = carry
            r = nrt - 1 - k
            rows = pl.ds(pl.multiple_of(r * 8, 8), 8)
            prev = pl.ds(pl.multiple_of(jnp.maximum(r - 1, 0) * 8, 8), 8)
            nr, ni = _cmul(ar, ai, gr, gi)
            nr, ni = nr + dr_ref[rows, :], ni + di_ref[rows, :]
            gr_ref[rows, :] = nr
            gi_ref[rows, :] = ni
            keep = jnp.where(r > 0, 1.0, 0.0)
            pr_, pi_ = sr_ref[prev, :] * keep, si_ref[prev, :] * keep
            return nr, ni, accr + (pr_ * nr + pi_ * ni), acci + (pr_ * ni - pi_ * nr)

        _, _, accr, acci = lax.fori_loop(0, nrt, step2, (init[0], init[1], zero, zero), unroll=SCAN_UNROLL)
        last = pl.ds((nrt - 1) * 8, 8)
        pr_, pi_ = _shift_down(sr_ref[last, :], 1), _shift_down(si_ref[last, :], 1)
        g0r, g0i = gr_ref[0:8, :], gi_ref[0:8, :]
        accr = accr + (pr_ * g0r + pi_ * g0i)
        acci = acci + (pr_ * g0i - pi_ * g0r)
        dlr_ref[...] = jnp.sum(accr, axis=0, keepdims=True)
        dli_ref[...] = jnp.sum(acci, axis=0, keepdims=True)
        u = u_ref[...]
        dbre_ref[...] = _dot(u, gr_ref[...], _TN)
        dbim_ref[...] = _dot(u, gi_ref[...], _TN)
        du = dd_ref[...] + _dot(gr_ref[...], bre_ref[...], _NT) + _dot(gi_ref[...], bim_ref[...], _NT)
        du_ref[...] = du.astype(du_ref.dtype)

    sp = _s5_specs(t)
    w = S5_BLOCKS * S5_ST
    return _pcall(
        body, name=name, grid=(S5_BLOCKS,),
        in_specs=[sp['ch'], sp['ch'], sp['ch'], sp['st'], sp['st'], sp['b'], sp['b'], sp['c'], sp['c'], sp['lam'], sp['lam']],
        out_specs=[sp['ch'], sp['b'], sp['b'], sp['c'], sp['c'], sp['lam'], sp['lam']],
        out_shape=[_sds((t, S5_WIDTH), BF16), _sds((S5_BLOCKS, S5_CH, S5_ST)), _sds((S5_BLOCKS, S5_CH, S5_ST)),
                   _sds((S5_BLOCKS, S5_ST, S5_CH)), _sds((S5_BLOCKS, S5_ST, S5_CH)), _sds((1, w)), _sds((1, w))],
        scratch_shapes=[pltpu.VMEM((t, S5_ST), F32)] * 2, compiler_params=_params("parallel"),
    )(dy, du_direct, u5, sr, si, bre, bim, cre, cim, lr, li)


def _s5_expander():
    n = lax.broadcasted_iota(jnp.int32, (S5_STATE, S5_STATE * S5_GROUP), 0)
    j = lax.broadcasted_iota(jnp.int32, (S5_STATE, S5_STATE * S5_GROUP), 1)
    return jnp.where(n == j // S5_GROUP, 1.0, 0.0).astype(F32)


def _s5_discretise(lam_re, lam_im, log_step, b_re, b_im):
    lr = jnp.minimum(lam_re, S5_MAX_REAL)
    step = jnp.exp(log_step)
    mag = jnp.exp(lr * step)
    ang = lam_im * step
    lbr, lbi = mag * jnp.cos(ang), mag * jnp.sin(ang)
    p, q = lbr - 1.0, lbi
    den = lr * lr + lam_im * lam_im
    cr, ci = (p * lr + q * lam_im) / den, (q * lr - p * lam_im) / den
    e = _s5_expander()
    cre, cie = _fdot(cr, e), _fdot(ci, e)
    return lbr, lbi, cre * b_re - cie * b_im, cre * b_im + cie * b_re


def _s5_params_fwd(lam_re, lam_im, log_step, b_re, b_im, *, name):
    g, n, w = lam_re.shape[0], S5_STATE, S5_STATE * S5_GROUP

    def body(a, b, c, d, e, o0, o1, o2, o3):
        for ref, val in zip((o0, o1, o2, o3), _s5_discretise(a[...], b[...], c[...], d[...], e[...])):
            ref[...] = val

    return _pcall(body, name=name, out_shape=[_sds((g, n)), _sds((g, n)), _sds((g, w)), _sds((g, w))])(
        lam_re, lam_im, log_step, b_re, b_im)


def _s5_params_bwd(lam_re, lam_im, log_step, b_re, b_im, cts, *, name):
    g, n, w = S5_GROUPS, S5_STATE, S5_STATE * S5_GROUP

    def body(a, b, c, d, e, c0, c1, c2, c3, o0, o1, o2, o3, o4):
        _, vjp = jax.vjp(_s5_discretise, a[...], b[...], c[...], d[...], e[...])
        for ref, val in zip((o0, o1, o2, o3, o4), vjp((c0[...], c1[...], c2[...], c3[...]))):
            ref[...] = val

    return _pcall(body, name=name,
                  out_shape=[_sds((g, n)), _sds((g, n)), _sds((g, 1)), _sds((g, w)), _sds((g, w))])(
        lam_re, lam_im, log_step, b_re, b_im, *cts)


def _s5_prepare(w):
    rows = DEPTH * S5_GROUPS
    lbr, lbi, bbr, bbi = _s5_params_fwd(
        w['s5_lambda_re'].reshape(rows, -1), w['s5_lambda_im'].reshape(rows, -1), w['s5_log_step'].reshape(rows, 1),
        w['s5_b_re'].reshape(rows, -1), w['s5_b_im'].reshape(rows, -1), name="s5_par")
    bd = lambda m: _blockdiag(m.reshape(rows, S5_STATE, S5_GROUP).transpose(0, 2, 1), S5_GROUP, S5_STATE).astype(BF16)
    cd = lambda m: _blockdiag(m.reshape(rows, S5_GROUP, S5_STATE).transpose(0, 2, 1), S5_STATE, S5_GROUP).astype(BF16)
    bre, bim, cre, cim = bd(bbr), bd(bbi), cd(w['s5_c_re']), cd(w['s5_c_im'])
    lr, li = lbr.reshape(DEPTH, 1, -1), lbi.reshape(DEPTH, 1, -1)
    blk = lambda a, i: a[i * S5_BLOCKS:(i + 1) * S5_BLOCKS]
    return [dict(bre=blk(bre, i), bim=blk(bim, i), cre=blk(cre, i), cim=blk(cim, i), lr=lr[i], li=li[i])
            for i in range(DEPTH)]


def _perm(a):
    t, c = a.shape
    return a.reshape(8, t // 8, c).transpose(1, 0, 2).reshape(t, c)


def _unperm(a):
    t, c = a.shape
    return a.reshape(t // 8, 8, c).transpose(1, 0, 2).reshape(t, c)


def _blockdiag(m, rows_inner, cols_inner):
    nblk = m.shape[0] // 8
    m = m.reshape(nblk, 8, rows_inner, cols_inner)
    eye = jnp.eye(8, dtype=m.dtype)
    out = m[:, :, :, None, :] * eye[None, :, None, :, None]
    return out.reshape(nblk, 8 * rows_inner, 8 * cols_inner)


def _blockdiag_extract(m, rows_inner, cols_inner):
    m = m.reshape(S5_BLOCKS, 8, rows_inner, 8, cols_inner)
    d = jnp.diagonal(m, axis1=1, axis2=3)
    return d.transpose(0, 3, 1, 2).reshape(S5_GROUPS, rows_inner, cols_inner)


Z0, XBC0, GA0, GB0 = 0, SSD_D_INNER, SSD_D_INNER + SSD_CONV_DIM, SSD_D_INNER + SSD_CONV_DIM + D_MODEL
BIG = GB0 + D_MODEL


def _mixer_fwd(h, p, tm):
    t = h.shape[0]
    nrow = t // tm
    u = _rms_fwd(h, p['pre_g'], name="mix_rms", tm=tm)
    u_p = _perm(u)
    proj = _mm(u, p['w_big'], name="mix_in")
    dtr = _mm(u, p['w_dt'], name="mix_in_dt")
    u5 = _mm(u_p, p['w_u5'], name="mix_in_s5")
    late, proj = lax.optimization_barrier((p['late'], proj))
    by_rows = lambda a: a.reshape(-1, a.shape[-1])
    p = dict(p, w_a=by_rows(late['w_branch_a']), w_b=by_rows(late['w_branch_b']), w_out=by_rows(late['w_out']),
             w_glu=late['s5_w_glu'][:, 0].transpose(1, 0, 2).reshape(late['s5_w_glu'].shape[2], -1))
    xc = _conv_fwd(proj, XBC0, p['conv_w'], p['conv_b'], name="ssd_conv")
    dt4 = jnp.pad(dtr.reshape(t, SSD_GROUPS, 8).transpose(1, 0, 2), ((0, 0), (0, 0), (0, PAD_HEADS - 8)))
    y_ssd, s_in = _ssd_fwd(xc, dt4, p['dt_bias8'], p['a_log8'], p['d8'], name="ssd_scan")
    gw = SSD_D_INNER // SSD_GROUPS
    ya = _rows(_gatenorm, name="ssd_gate", nrow=nrow, ncol=SSD_GROUPS,
               ins=[(y_ssd, _rspec(tm, gw, 0, True)), (proj, _rspec(tm, gw, Z0 // gw, True)),
                    (p['norm_g'], _bspec(gw, 0, True))],
               outs=[(_sds((t, SSD_D_INNER), BF16), _rspec(tm, gw, 0, True), False)])[0]
    y_a = _mm(ya, p['w_a'], name="mix_a")
    bre, bim, cre, cim, lr, li = (p['s5'][k] for k in ('bre', 'bim', 'cre', 'cim', 'lr', 'li'))
    sr, si, y5 = _s5_fwd(u5, bre, bim, cre, cim, lr, li, name="s5_scan")
    y5g = _rows(lambda a, b, d: _gelu(a + d * b), name="s5_act", nrow=nrow,
                ins=[(y5, _rspec(tm, S5_WIDTH)), (u5, _rspec(tm, S5_WIDTH)), (p['s5_d'], _bspec(S5_WIDTH))],
                outs=[(_sds((t, S5_WIDTH), BF16), _rspec(tm, S5_WIDTH), False)])[0]
    vg = _mm(y5g, p['w_glu'], name="s5_glu")
    ybin = _rows(lambda a, b: a * _sigmoid(b), name="s5_glu_act", nrow=nrow,
                 ins=[(vg, _rspec(tm, S5_WIDTH, 0)), (vg, _rspec(tm, S5_WIDTH, 1))],
                 outs=[(_sds((t, S5_WIDTH), BF16), _rspec(tm, S5_WIDTH), False)])[0]
    y_b = _unperm(_mm(ybin, p['w_b'], name="mix_b"))
    merged = _rows(lambda ga, gb, a, b: _sigmoid(ga) * a + _sigmoid(gb) * b, name="mix_merge", nrow=nrow,
                   ins=[(proj, _rspec(tm, D_MODEL, GA0 // D_MODEL)), (proj, _rspec(tm, D_MODEL, GB0 // D_MODEL)),
                        (y_a, _rspec(tm, D_MODEL)), (y_b, _rspec(tm, D_MODEL))],
                   outs=[(_sds((t, D_MODEL), BF16), _rspec(tm, D_MODEL), False)])[0]
    m = _mm(merged, p['w_out'], name="mix_out")
    out = _resid_fwd(h, m, p['post_g'], 1.0, name="mix_res", tm=tm)
    saved = dict(w_a=p['w_a'], w_b=p['w_b'], w_out=p['w_out'], w_glu=p['w_glu'], h=h, u=u, u_p=u_p, proj=proj, u5=u5, xc=xc, dt4=dt4, s_in=s_in, y_ssd=y_ssd, ya=ya, y_a=y_a,
                 bre=bre, bim=bim, cre=cre, cim=cim, lr=lr, li=li, sr=sr, si=si, y5=y5, y5g=y5g, vg=vg, ybin=ybin,
                 y_b=y_b, merged=merged, m=m)
    return out, saved


def _mixer_bwd(dh, p, s, tm, after_first):
    t = dh.shape[0]
    nrow = t // tm
    proj = s['proj']
    bufs = {}

    def grad_mm(a, b, wname, axis, name):
        bufs[wname] = _mm(a, b, ta=True, name=name, out_dtype=BF16, shards=axis)

    dm, dpost = _resid_bwd(s['m'], p['post_g'], dh, 1.0, name="mix_res_bwd", tm=tm)
    dm = after_first(dm)
    dmerged = _mm(dm, s['w_out'], tb=True, name="mix_out_dx")
    grad_mm(s['merged'], dm, 'w_out', 'rows', "mix_out_dw")

    def merge_bwd(ga, gb, a, b, d):
        _, vjp = jax.vjp(lambda ga_, gb_, a_, b_: _sigmoid(ga_) * a_ + _sigmoid(gb_) * b_, ga, gb, a, b)
        dga, dgb, da, db = vjp(d)
        return jnp.concatenate([dga, dgb], axis=1), da, db

    dgab, dy_a, dy_b = _rows(
        merge_bwd, name="mix_merge_bwd", nrow=nrow,
        ins=[(proj, _rspec(tm, D_MODEL, GA0 // D_MODEL)), (proj, _rspec(tm, D_MODEL, GB0 // D_MODEL)),
             (s['y_a'], _rspec(tm, D_MODEL)), (s['y_b'], _rspec(tm, D_MODEL)), (dmerged, _rspec(tm, D_MODEL))],
        outs=[(_sds((t, 2 * D_MODEL), BF16), _rspec(tm, 2 * D_MODEL), False),
              (_sds((t, D_MODEL), BF16), _rspec(tm, D_MODEL), False),
              (_sds((t, D_MODEL), BF16), _rspec(tm, D_MODEL), False)])
    dya = _mm(dy_a, s['w_a'], tb=True, name="mix_a_dx")
    grad_mm(s['ya'], dy_a, 'w_branch_a', 'rows', "mix_a_dw")
    gw = SSD_D_INNER // SSD_GROUPS

    def gate_bwd(y, z, g, d):
        _, vjp = jax.vjp(_gatenorm, y, z, g)
        return vjp(d)

    dy_ssd, dz, dnorm = _rows(
        gate_bwd, name="ssd_gate_bwd", nrow=nrow, ncol=SSD_GROUPS,
        ins=[(s['y_ssd'], _rspec(tm, gw, 0, True)), (proj, _rspec(tm, gw, Z0 // gw, True)),
             (p['norm_g'], _bspec(gw, 0, True)), (dya, _rspec(tm, gw, 0, True))],
        outs=[(_sds((t, SSD_D_INNER)), _rspec(tm, gw, 0, True), False),
              (_sds((t, SSD_D_INNER), BF16), _rspec(tm, gw, 0, True), False),
              (_sds((1, SSD_D_INNER)), _bspec(gw, 0, True), True)])
    dxs, dbm, dcm, ddt4, ddtb, dalog, ddsk = _ssd_bwd(s['xc'], s['dt4'], p['dt_bias8'], p['a_log8'], p['d8'],
                                                      s['s_in'], dy_ssd, name="ssd_scan_bwd")
    dxbc, dconv_w, dconv_b = _conv_bwd(proj, XBC0, p['conv_w'], p['conv_b'], (dxs, dbm, dcm), name="ssd_conv_bwd")
    ddtr = ddt4[:, :, :8].transpose(1, 0, 2).reshape(t, SSD_HEADS)
    dy_bp = _perm(dy_b)
    dybin = _mm(dy_bp, s['w_b'], tb=True, name="mix_b_dx")
    grad_mm(s['ybin'], dy_bp, 'w_branch_b', 'rows', "mix_b_dw")

    def glu_bwd(a, b, d):
        _, vjp = jax.vjp(lambda a_, b_: a_ * _sigmoid(b_), a, b)
        da, db = vjp(d)
        return jnp.concatenate([da, db], axis=1)

    dvg = _rows(glu_bwd, name="s5_glu_act_bwd", nrow=nrow,
                ins=[(s['vg'], _rspec(tm, S5_WIDTH, 0)), (s['vg'], _rspec(tm, S5_WIDTH, 1)), (dybin, _rspec(tm, S5_WIDTH))],
                outs=[(_sds((t, 2 * S5_WIDTH), BF16), _rspec(tm, 2 * S5_WIDTH), False)])[0]
    dy5g = _mm(dvg, s['w_glu'], tb=True, name="s5_glu_dx")
    grad_mm(s['y5g'], dvg, 's5_w_glu', 'cols', "s5_glu_dw")

    def act_bwd(a, b, d, g):
        _, vjp = jax.vjp(lambda a_, b_, d_: _gelu(a_ + d_ * b_), a, b, d)
        return vjp(g)

    dy5, du5_direct, ds5d = _rows(
        act_bwd, name="s5_act_bwd", nrow=nrow,
        ins=[(s['y5'], _rspec(tm, S5_WIDTH)), (s['u5'], _rspec(tm, S5_WIDTH)), (p['s5_d'], _bspec(S5_WIDTH)),
             (dy5g, _rspec(tm, S5_WIDTH))],
        outs=[(_sds((t, S5_WIDTH), BF16), _rspec(tm, S5_WIDTH), False), (_sds((t, S5_WIDTH)), _rspec(tm, S5_WIDTH), False),
              (_sds((1, S5_WIDTH)), _bspec(S5_WIDTH), True)])
    du5, dbre, dbim, dcre, dcim, dlr, dli = _s5_bwd(dy5, du5_direct, s['u5'], s['sr'], s['si'], s['bre'], s['bim'],
                                                     s['cre'], s['cim'], s['lr'], s['li'], name="s5_scan_bwd")
    du_p = _mm(du5, p['w_u5'], tb=True, name="mix_in_s5_dx")
    dw_u5 = _mm(s['u_p'], du5, ta=True, name="mix_in_s5_dw", out_dtype=BF16)
    ext_b = lambda m: _blockdiag_extract(m, S5_GROUP, S5_STATE).transpose(0, 2, 1).reshape(S5_GROUPS, S5_STATE * S5_GROUP)
    dlam_re, dlam_im, dlog_step, db_re, db_im = _s5_params_bwd(
        p['lam_re'], p['lam_im'], p['log_step'], p['b_re'], p['b_im'],
        (dlr.reshape(S5_GROUPS, S5_STATE), dli.reshape(S5_GROUPS, S5_STATE), ext_b(dbre), ext_b(dbim)), name="s5_par_bwd")
    dc_re = _blockdiag_extract(dcre, S5_STATE, S5_GROUP).transpose(0, 2, 1)
    dc_im = _blockdiag_extract(dcim, S5_STATE, S5_GROUP).transpose(0, 2, 1)
    dproj = jnp.concatenate([dz, dxbc, dgab], axis=1)
    du_big = _mm(dproj, p['w_big'], tb=True, name="mix_in_dx")
    du_dt = _mm(ddtr, p['w_dt'], tb=True, name="mix_in_dt_dx")
    dw_big = _mm(s['u'], dproj, ta=True, name="mix_in_dw", out_dtype=BF16)
    dw_dt = _mm(s['u'], ddtr, ta=True, name="mix_in_dt_dw", out_dtype=BF16)
    dh_in, dpre = _rms_bwd(s['h'], p['pre_g'], dh, [du_big, du_dt, _unperm(du_p)], name="mix_rms_bwd", tm=tm)
    dw_in = jnp.concatenate([dw_big[:, :GA0], dw_dt, dw_u5, dw_big[:, GA0:]], axis=1)
    bufs['w_in'] = dw_in.reshape(D_MODEL, N_DEV, -1).transpose(1, 0, 2)[:, None]
    grads = {
        'mix_pre_g': dpre, 'mix_post_g': dpost, 'ssd_conv_w': dconv_w, 'ssd_conv_b': dconv_b,
        'ssd_dt_bias': ddtb[:, 0, :8].reshape(-1), 'ssd_a_log': dalog[:, 0, :8].reshape(-1),
        'ssd_d': ddsk[:, 0, :8].reshape(-1), 'ssd_norm_g': dnorm,
        's5_lambda_re': dlam_re, 's5_lambda_im': dlam_im,
        's5_b_re': db_re.reshape(S5_GROUPS, S5_STATE, S5_GROUP), 's5_b_im': db_im.reshape(S5_GROUPS, S5_STATE, S5_GROUP),
        's5_c_re': dc_re, 's5_c_im': dc_im, 's5_log_step': dlog_step.reshape(-1), 's5_d': ds5d,
    }
    return dh_in, bufs, grads


HBM_SPEC = pl.BlockSpec(memory_space=pltpu.HBM)


def _place():
    return lax.axis_index("x"), lax.axis_index("y"), lax.axis_index("c")


GATHER_COLLECTIVE_ID = 1


def _all_gather(shards, *, name, on_sequencer=False):
    n = len(shards)

    def body(*refs):
        x_refs, out_refs = refs[:n], refs[n:2 * n]
        send_sems, recv_sems, local_sems = refs[2 * n:]
        x, y, c = _place()
        me, sibling = (x, y, c), (x, y, 1 - c)
        chips = [(1 - x, y), (x, 1 - y), (1 - x, 1 - y)]
        if on_sequencer:
            _handshake([sibling] + [(*chip, c) for chip in chips])

        def slot(o, px, py, pc):
            return out_refs[o].at[4 * px + 2 * py + pc]

        def copy(o, k, block, to, src=None):
            return pltpu.make_async_remote_copy(
                src_ref=slot(o, *block) if src is None else src, dst_ref=slot(o, *block),
                send_sem=send_sems.at[7 * o + k], recv_sem=recv_sems.at[7 * o + k], device_id=to, device_id_type=MESH)

        mine = [pltpu.make_async_copy(x_refs[o], slot(o, *me), local_sems.at[o]) for o in range(n)]
        for cp in mine:
            cp.start()
        first = []
        for j, chip in enumerate(chips):
            first += [copy(o, 1 + j, me, (*chip, c), src=x_refs[o]) for o in range(n)]
        first += [copy(o, 0, me, sibling, src=x_refs[o]) for o in range(n)]
        for cp in first:
            cp.start()
        passed = []
        for j, chip in enumerate(chips):
            for o in range(n):
                copy(o, 1 + j, (*chip, c), me).wait_recv()
                passed.append(copy(o, 4 + j, (*chip, c), sibling))
                passed[-1].start()
        for o in range(n):
            copy(o, 0, sibling, me).wait_recv()
        for j, chip in enumerate(chips):
            for o in range(n):
                copy(o, 4 + j, (*chip, 1 - c), me).wait_recv()
        for cp in first + passed:
            cp.wait_send()
        for cp in mine:
            cp.wait()

    out_shape = [jax.ShapeDtypeStruct((N_DEV,) + s.shape, s.dtype) for s in shards]
    sems = [pltpu.SemaphoreType.DMA((7 * n,)), pltpu.SemaphoreType.DMA((7 * n,)), pltpu.SemaphoreType.DMA((n,))]
    if on_sequencer:
        return _scall(body, name=name, out_type=out_shape, scratch_types=sems, collective_id=GATHER_COLLECTIVE_ID)(*shards)
    return _pcall(body, name=name, in_specs=[HBM_SPEC] * n, out_specs=[HBM_SPEC] * n, out_shape=out_shape,
                  scratch_shapes=sems)(*shards)


N_CHIPS = 4


SIBLING_COLLECTIVE_ID = 2
CHIPS_COLLECTIVE_ID = 3


def _handshake(peers):
    barrier = pltpu.get_barrier_semaphore()
    for peer in peers:
        pl.semaphore_signal(barrier, inc=1, device_id=peer, device_id_type=MESH)
    pl.semaphore_wait(barrier, len(peers))


def _exchange_sibling(grads, *, name):
    n = len(grads)

    def body(*refs):
        p_refs, q_refs = refs[:n], refs[n:2 * n]
        send_sems, recv_sems = refs[2 * n:]
        x, y, c = _place()
        _handshake([(x, y, 1 - c)])
        copies = [pltpu.make_async_remote_copy(
            src_ref=p_refs[o].at[k, 1 - c], dst_ref=q_refs[o].at[k], send_sem=send_sems.at[N_CHIPS * o + k],
            recv_sem=recv_sems.at[N_CHIPS * o + k], device_id=(x, y, 1 - c), device_id_type=MESH)
            for o in range(n) for k in range(N_CHIPS)]
        for cp in copies:
            cp.start()
        for cp in copies:
            cp.wait()

    return _scall(
        body, name=name, out_type=[jax.ShapeDtypeStruct((N_CHIPS,) + g.shape[2:], g.dtype) for g in grads],
        scratch_types=[pltpu.SemaphoreType.DMA((N_CHIPS * n,)), pltpu.SemaphoreType.DMA((N_CHIPS * n,))],
        collective_id=SIBLING_COLLECTIVE_ID,
    )(*grads)


def _pair_sum(own, got, *, name):
    _, _, r, l = own.shape
    tr = _tile(r, 512, 16)
    c = lax.axis_index("c").astype(jnp.int32).reshape(1)

    def body(c_ref, p_ref, q_ref, o_ref):
        o_ref[...] = (p_ref[...].astype(F32) + q_ref[...].astype(F32)).astype(o_ref.dtype)

    return _pcall(
        body, name=name,
        grid_spec=pltpu.PrefetchScalarGridSpec(
            num_scalar_prefetch=1, grid=(N_CHIPS, r // tr),
            in_specs=[pl.BlockSpec((None, None, tr, l), lambda k, i, cr: (k, cr[0], i, 0)),
                      pl.BlockSpec((None, tr, l), lambda k, i, cr: (k, i, 0))],
            out_specs=pl.BlockSpec((None, tr, l), lambda k, i, cr: (k, i, 0))),
        out_shape=jax.ShapeDtypeStruct((N_CHIPS, r, l), own.dtype),
        compiler_params=_params("parallel", "parallel"),
    )(c, own, got)


def _exchange_chips(parts, *, name):
    n = len(parts)

    def body(*refs):
        p_refs, g_refs = refs[:n], refs[n:2 * n]
        send_sems, recv_sems, local_sems = refs[2 * n:]
        x, y, c = _place()
        mine = 2 * x + y
        chips = [(1 - x, y), (x, 1 - y), (1 - x, 1 - y)]
        _handshake([(*chip, c) for chip in chips])
        own = [pltpu.make_async_copy(p_refs[o].at[mine], g_refs[o].at[mine], local_sems.at[o]) for o in range(n)]
        for cp in own:
            cp.start()
        copies = []
        for j, (px, py) in enumerate(chips):
            copies += [pltpu.make_async_remote_copy(
                src_ref=p_refs[o].at[2 * px + py], dst_ref=g_refs[o].at[mine], send_sem=send_sems.at[3 * o + j],
                recv_sem=recv_sems.at[3 * o + j], device_id=(px, py, c), device_id_type=MESH) for o in range(n)]
        for cp in copies:
            cp.start()
        for cp in copies:
            cp.wait()
        for cp in own:
            cp.wait()

    return _scall(
        body, name=name, out_type=[jax.ShapeDtypeStruct(p.shape, p.dtype) for p in parts],
        scratch_types=[pltpu.SemaphoreType.DMA((3 * n,)), pltpu.SemaphoreType.DMA((3 * n,)), pltpu.SemaphoreType.DMA((n,))],
        collective_id=CHIPS_COLLECTIVE_ID,
    )(*parts)


def _sum_slots(g, *, name):
    n, r, l = g.shape
    tr = _tile(r, 512, 16)

    def body(g_ref, o_ref):
        acc = g_ref[0].astype(F32)
        for k in range(1, n):
            acc = acc + g_ref[k].astype(F32)
        o_ref[...] = acc

    return _pcall(
        body, name=name, grid=(r // tr,), in_specs=[pl.BlockSpec((n, tr, l), lambda i: (0, i, 0))],
        out_specs=pl.BlockSpec((tr, l), lambda i: (i, 0)), out_shape=_sds((r, l)),
        compiler_params=_params("parallel"),
    )(g)


TRANSPOSED = ('ffn1_w_gate', 'ffn1_w_up', 'ffn2_w_gate', 'ffn2_w_up')
GATHER_CHUNKS = (('ffn1', ['ffn1_w_gate', 'ffn1_w_up']), ('ffn1_down', ['ffn1_w_down']),
                 ('mix_in', ['w_in', 'ssd_conv_w']), ('mix', ['w_branch_a', 's5_w_glu', 'w_branch_b', 'w_out']),
                 ('ffn2', ['ffn2_w_gate', 'ffn2_w_up']), ('ffn2_down', ['ffn2_w_down']))
LATE = ('ffn1_w_down', 'ffn2_w_down', 'w_branch_a', 's5_w_glu', 'w_branch_b', 'w_out')
SUBLAYERS = (('ffn1', ['ffn1_w_gate', 'ffn1_w_up', 'ffn1_w_down']),
             ('mix', ['w_in', 'ssd_conv_w', 'w_branch_a', 's5_w_glu', 'w_branch_b', 'w_out']),
             ('ffn2', ['ffn2_w_gate', 'ffn2_w_up', 'ffn2_w_down']))


def _gather_weights(w):
    layers, first = [], None
    for i in range(DEPTH):
        g = {}
        for tag, names in GATHER_CHUNKS:
            shards =[w[n][i:i + 1] if n == 'ssd_conv_w' else
                      (w[n][i:i + 1].transpose(0, 2, 1) if n in TRANSPOSED else w[n][i:i + 1]).astype(BF16) for n in names]
            if first is None:
                first = got = _all_gather(shards, name=f"gather_{tag}")
            else:
                shards, first = lax.optimization_barrier((shards, first))
                got = _all_gather(shards, name=f"gather_{tag}", on_sequencer=True)
            g.update(zip(names, got))
        layers.append(g)
    layers[0].update(zip(GATHER_CHUNKS[0][1], first))
    return layers


class _ReduceScatter:
    @staticmethod
    def sibling(tag, bufs):
        names = list(bufs)
        own = [bufs[n].reshape((N_CHIPS, 2) + bufs[n].shape[1:]) for n in names]
        return (tag, names), (own, _exchange_sibling(own, name=f"reduce_sibling_{tag}"))

    @staticmethod
    def chips(meta, arrays):
        (tag, names), (own, got) = meta, arrays
        flat = lambda a, lead: a.reshape(lead + (-1, a.shape[-1]))
        parts = [_pair_sum(flat(o, (N_CHIPS, 2)), flat(g, (N_CHIPS,)), name=f"reduce_pair_sum_{n}").reshape(g.shape)
                 for n, o, g in zip(names, own, got)]
        return names, _exchange_chips(parts, name=f"reduce_chips_{tag}")

    @staticmethod
    def done(names, slots):
        return dict(zip(names, slots))

    @staticmethod
    def small(grads):
        return _reduce_small(grads)


def _reduce_small(grads):
    flat = jnp.concatenate([g.astype(F32).reshape(-1) for g in grads.values()])
    pad = (-flat.shape[0]) % (8 * LANES)
    flat = jnp.concatenate([flat, jnp.zeros((pad,), F32)]).reshape(-1, LANES)
    gathered = _all_gather([flat], name="gather_small_grads", on_sequencer=True)[0]
    total = _sum_slots(gathered, name="sum_small_grads").reshape(-1)
    out, o = {}, 0
    for n, g in grads.items():
        out[n] = total[o:o + g.size].reshape(g.shape)
        o += g.size
    return out


def _adamw(w, g, m, v, *, name, slots=False):
    shape = w.shape
    if slots:
        lyr, rows, lanes = shape
        w2, m2, v2 = w, m, v
        tr = _tile(rows, 256, 16)
        nrt = rows // tr
        grid = (lyr, nrt)
        spec = pl.BlockSpec((None, tr, lanes), lambda l, i: (l, i, 0))
        g_specs = [pl.BlockSpec((N_CHIPS, None, tr, lanes),
                                lambda l, i, k=k: (0, 0, jnp.where(l == k, i, jnp.where(l > k, nrt - 1, 0)), 0))
                   for k in range(lyr)]
        g_args = list(g)
        out_shape = [_sds(shape)] * 4
    else:
        lanes = shape[-1] if (shape[-1] >= 128 or w.size % LANES) else LANES
        as2d = lambda a: a.reshape(-1, lanes)
        w2, m2, v2 = as2d(w), as2d(m), as2d(v)
        r = w2.shape[0]
        tr = _tile(r, 256, 8)
        grid = (1, r // tr)
        spec = pl.BlockSpec((tr, lanes), lambda l, i: (i, 0))
        g_specs, g_args = [spec], [as2d(g)]
        out_shape = [_sds((r, lanes))] * 4
    n_g = len(g_args)

    def body(w_ref, *rest):
        g_refs = rest[:n_g]
        m_ref, v_ref, go_ref, d_ref, mo_ref, vo_ref = rest[n_g:]
        if slots:
            gg = None
            for k, g_ref in enumerate(g_refs):
                tot = g_ref[0].astype(F32)
                for c in range(1, N_CHIPS):
                    tot = tot + g_ref[c].astype(F32)
                gg = tot if gg is None else jnp.where(pl.program_id(0) == k, tot, gg)
        else:
            gg = g_refs[0][...]
        go_ref[...] = gg
        mn = ADAM_B1 * m_ref[...] + (1.0 - ADAM_B1) * gg
        vn = ADAM_B2 * v_ref[...] + (1.0 - ADAM_B2) * (gg * gg)
        m_hat = mn / (1.0 - ADAM_B1 ** ADAM_STEP)
        v_hat = vn / (1.0 - ADAM_B2 ** ADAM_STEP)
        d_ref[...] = -ADAM_LR * (m_hat / (jnp.sqrt(v_hat) + ADAM_EPS) + ADAM_WD * w_ref[...])
        mo_ref[...] = mn
        vo_ref[...] = vn

    res = _pcall(
        body, name=name, grid=grid, in_specs=[spec] + g_specs + [spec, spec], out_specs=[spec] * 4,
        out_shape=out_shape, compiler_params=_params("arbitrary", "arbitrary"),
    )(w2, *g_args, m2, v2)
    return tuple(a.reshape(shape) for a in res)


def _sublayer_params(w, g, i, k, s5):
    row = lambda a: a.astype(F32).reshape(1, -1)
    if k != 'mix':
        return dict(layer=i, pre_g=row(w[f'{k}_pre_g'][i]), post_g=row(w[f'{k}_post_g'][i]),
                    w_gate=g[f'{k}_w_gate'], w_up=g[f'{k}_w_up'], w_down=g[f'{k}_w_down'])
    head8 = lambda a: jnp.broadcast_to(
        jnp.pad(a.astype(F32).reshape(SSD_GROUPS, 1, 8), ((0, 0), (0, 0), (0, PAD_HEADS - 8))), (SSD_GROUPS, 8, PAD_HEADS))
    by_rows = lambda n: g[n].reshape(-1, g[n].shape[-1])
    by_cols = lambda n: g[n][:, 0].transpose(1, 0, 2).reshape(g[n].shape[2], -1)
    w_in = by_cols('w_in')
    s = np.cumsum([SSD_D_INNER, SSD_CONV_DIM, SSD_HEADS, S5_WIDTH, D_MODEL])
    return dict(
        layer=i, s5=s5, pre_g=row(w['mix_pre_g'][i]), post_g=row(w['mix_post_g'][i]),
        w_big=jnp.concatenate([w_in[:, :s[1]], w_in[:, s[3]:]], axis=1), w_dt=w_in[:, s[1]:s[2]], w_u5=w_in[:, s[2]:s[3]],
        conv_w=by_cols('ssd_conv_w'), conv_b=row(w['ssd_conv_b'][i]),
        dt_bias8=head8(w['ssd_dt_bias'][i]), a_log8=head8(w['ssd_a_log'][i]), d8=head8(w['ssd_d'][i]),
        norm_g=row(w['ssd_norm_g'][i]), late={n: g[n] for n in SUBLAYERS[1][1] if n in LATE},
        lam_re=w['s5_lambda_re'][i], lam_im=w['s5_lambda_im'][i], log_step=w['s5_log_step'][i].reshape(S5_GROUPS, 1),
        b_re=w['s5_b_re'][i].reshape(S5_GROUPS, -1), b_im=w['s5_b_im'][i].reshape(S5_GROUPS, -1),
        c_re=w['s5_c_re'][i], c_im=w['s5_c_im'][i], s5_d=row(w['s5_d'][i]),
    )


def _loss_head(h, target, *, tm):
    t, d = h.shape

    def fn(y, tgt):
        err = y - tgt
        return err * (1.0 / d), jnp.sum(0.5 * jnp.sum(err * err, axis=-1, keepdims=True) * (1.0 / d), axis=0, keepdims=True)

    dy, loss = _rows(fn, name="loss_head", nrow=t // tm,
                     ins=[(h, _rspec(tm, d)), (target, _rspec(tm, d))],
                     outs=[(_sds((t, d)), _rspec(tm, d), False), (_sds((1, 128)), _bspec(128), True)])
    return dy, loss[0, 0]


def _forward_backward(h, target, w, g, rs):
    t = h.shape[0]
    tm = _tile(t, 512, 8)
    s5 = _s5_prepare(w)
    layers, saved = [], []
    for i in range(DEPTH):
        gi, ps, ss = dict(g[i]), [], []
        for tag, names in SUBLAYERS:
            early = [n for n in names if n not in LATE]
            tied, h = lax.optimization_barrier(([gi[n] for n in early], h))
            gi.update(zip(early, tied))
            p = _sublayer_params(w, gi, i, tag, s5[i])
            h, s = _mixer_fwd(h, p, tm) if tag == 'mix' else _ffn_fwd(h, p, tag, tm)
            ps.append(p)
            ss.append(s)
        layers.append(ps)
        saved.append(ss)
    dh, loss = _loss_head(h, target, tm=tm)
    reduced, small = [{} for _ in range(DEPTH)], [{} for _ in range(DEPTH)]
    in_sibling, in_chips = None, None

    def start_chips(x):
        nonlocal in_sibling, in_chips
        if in_sibling is not None:
            layer, meta, arrays = in_sibling
            arrays, x = lax.optimization_barrier((arrays, x))
            in_sibling, in_chips = None, (layer,) + tuple(rs.chips(meta, arrays))
        return x

    def finish_chips(x):
        nonlocal in_chips
        if in_chips is not None:
            layer, names, slots = in_chips
            slots, x = lax.optimization_barrier((slots, x))
            reduced[layer].update(rs.done(names, slots))
            in_chips = None
        return x

    for i in reversed(range(DEPTH)):
        for k in reversed(range(len(SUBLAYERS))):
            tag = SUBLAYERS[k][0]
            if tag == 'mix':
                dh, bufs, grads = _mixer_bwd(dh, layers[i][k], saved[i][k], tm, start_chips)
            else:
                dh, bufs, grads = _ffn_bwd(dh, layers[i][k], saved[i][k], tag, tm, start_chips)
            small[i].update(grads)
            dh = finish_chips(dh)
            in_sibling = (i,) + tuple(rs.sibling(tag, bufs))
            if tag == 'mix' and i + 1 < DEPTH:
                small[i + 1], dh = lax.optimization_barrier((small[i + 1], dh))
        if i == 0:
            small[i]['loss'] = loss.reshape(1)
        small[i] = rs.small(small[i])
    loss = small[0].pop('loss')[0]
    dh = finish_chips(start_chips(dh))
    shapes = {n: (w[n].shape[:-1] + (SSD_CONV_DIM,) if n == 'ssd_conv_w' else w[n].shape) for n in SMALL_ORDER}
    stacked = {n: jnp.stack([small[i][n].reshape(shapes[n][1:]) for i in range(DEPTH)]) for n in SMALL_ORDER}
    return loss, dh, reduced, stacked


def kernel(*args):
    n_w = len(WEIGHTS)
    x, target = args[0], args[1 + n_w]
    w = dict(zip(WEIGHTS, args[1:1 + n_w]))
    m = dict(zip(WEIGHTS, args[2 + n_w:2 + 2 * n_w]))
    v = dict(zip(WEIGHTS, args[2 + 2 * n_w:2 + 3 * n_w]))
    t = x.shape[1]

    g = _gather_weights(w)
    loss, dx, slots, small = _forward_backward(x.reshape(t, D_MODEL), target.reshape(t, D_MODEL), w, g, _ReduceScatter)
    me = 4 * lax.axis_index("x") + 2 * lax.axis_index("y") + lax.axis_index("c")
    cols = w['ssd_conv_w'].shape[-1]
    small['ssd_conv_w'] = lax.dynamic_slice_in_dim(small['ssd_conv_w'], me * cols, cols, axis=2)

    grad, delta, new_m, new_v = {}, {}, {}, {}
    for n in WEIGHTS:
        sharded = n in slots[0]
        view = (lambda a: a.transpose(0, 2, 1)) if n in TRANSPOSED else (lambda a: a)
        res = _adamw(view(w[n]), [slots[i][n] for i in range(DEPTH)] if sharded else small[n], view(m[n]), view(v[n]),
                     name=f"adamw_{n}", slots=sharded)
        grad[n], delta[n], new_m[n], new_v[n] = (view(a) for a in res)
    return (loss, dx.reshape(x.shape), *[grad[n] for n in WEIGHTS], *[delta[n] for n in WEIGHTS],
            *[new_m[n] for n in WEIGHTS], *[new_v[n] for n in WEIGHTS])
```

```python
import functools
import math

import numpy as np
import jax
import jax.numpy as jnp
from jax import lax
from jax.experimental import pallas as pl
from jax.experimental.pallas import tpu as pltpu
from jax.experimental.pallas import tpu_sc as plsc

F32 = jnp.float32
BF16 = jnp.bfloat16
MESH = pl.DeviceIdType.MESH
HIGHEST = lax.Precision.HIGHEST

D_MODEL = 1024
DEPTH = 2
FFN_HIDDEN = 2816
SSD_D_INNER = 2048
SSD_HEADS = 32
SSD_HEAD_DIM = 64
SSD_GROUPS = 4
SSD_STATE = 128
SSD_CHUNK = 128
SSD_CONV_DIM = 3072
SSD_CONV_WIDTH = 4
S5_WIDTH = 1024
S5_GROUP = 16
S5_GROUPS = 64
S5_STATE = 64
S5_MAX_REAL = -1e-4
S5_BLOCKS = 8
RMS_EPS = 1e-6
N_DEV = 8
LANES = 1024

ADAM_LR = 0.001
ADAM_B1 = 0.9
ADAM_B2 = 0.999
ADAM_EPS = 1e-08
ADAM_WD = 0.01
ADAM_STEP = 10

VMEM_LIMIT_BYTES = 48 * 1024 * 1024

WEIGHTS = ['ffn1_pre_g', 'ffn1_post_g', 'ffn1_w_gate', 'ffn1_w_up', 'ffn1_w_down', 'mix_pre_g', 'mix_post_g',
           'w_in', 'ssd_conv_w', 'ssd_conv_b', 'ssd_dt_bias', 'ssd_a_log', 'ssd_d', 'ssd_norm_g', 'w_branch_a',
           's5_lambda_re', 's5_lambda_im', 's5_b_re', 's5_b_im', 's5_c_re', 's5_c_im', 's5_log_step', 's5_d',
           's5_w_glu', 'w_branch_b', 'w_out', 'ffn2_pre_g', 'ffn2_post_g', 'ffn2_w_gate', 'ffn2_w_up',
           'ffn2_w_down']
SHARDED = {'ffn1_w_gate': 2, 'ffn1_w_up': 2, 'ffn1_w_down': 1, 'w_in': 2, 'ssd_conv_w': 2, 'w_branch_a': 1,
           's5_w_glu': 2, 'w_branch_b': 1, 'w_out': 1, 'ffn2_w_gate': 2, 'ffn2_w_up': 2, 'ffn2_w_down': 1}
SHARDED_ORDER = [n for n in WEIGHTS if n in SHARDED]
SMALL_ORDER = [n for n in WEIGHTS if n not in SHARDED or n == 'ssd_conv_w']


def _pcall(body, **kw):
    return pl.pallas_call(body, **kw)


def _scall(body, *, name, out_type, scratch_types, collective_id):
    return pl.kernel(body, out_type=out_type, mesh=plsc.ScalarSubcoreMesh(axis_name="sequencer", num_cores=1),
                     scratch_types=scratch_types, name=name,
                     compiler_params=pltpu.CompilerParams(collective_id=collective_id))


def _params(*sem):
    return pltpu.CompilerParams(dimension_semantics=sem, vmem_limit_bytes=VMEM_LIMIT_BYTES)


def _tile(n, pref, align=128):
    if n <= pref:
        return n
    t = (pref // align) * align
    while t >= align:
        if n % t == 0:
            return t
        t -= align
    return n


def _rms(x, g):
    return x * lax.rsqrt(jnp.mean(x * x, axis=-1, keepdims=True) + RMS_EPS) * g


def _sigmoid(x):
    return 1.0 / (1.0 + jnp.exp(-x))


def _silu(x):
    return x * _sigmoid(x)


def _gelu(x):
    return 0.5 * x * (1.0 + jnp.tanh(math.sqrt(2.0 / math.pi) * (x + 0.044715 * (x * x * x))))


def _softplus(x):
    return jnp.maximum(x, 0.0) + jnp.log(1.0 + jnp.exp(-jnp.abs(x)))


def _dot(a, b, dims):
    return lax.dot_general(a.astype(BF16), b.astype(BF16), (dims, ((), ())), preferred_element_type=F32)


_NN = ((1,), (0,))
_NT = ((1,), (1,))
_TN = ((0,), (0,))


@jax.custom_vjp
def _bdot_nn(a, b):
    return _dot(a, b, _NN)


_bdot_nn.defvjp(lambda a, b: (_dot(a, b, _NN), (a, b)),
                lambda r, g: (_dot(g, r[1], _NT), _dot(r[0], g, _TN)))


@jax.custom_vjp
def _bdot_nt(a, b):
    return _dot(a, b, _NT)


_bdot_nt.defvjp(lambda a, b: (_dot(a, b, _NT), (a, b)),
                lambda r, g: (_dot(g, r[1], _NN), _dot(g, r[0], _TN)))


@jax.custom_vjp
def _bdot_tn(a, b):
    return _dot(a, b, _TN)


_bdot_tn.defvjp(lambda a, b: (_dot(a, b, _TN), (a, b)),
                lambda r, g: (_dot(r[1], g, _NT), _dot(r[0], g, _NN)))


def _fdot(a, b, dims=_NN):
    return lax.dot_general(a, b, (dims, ((), ())), precision=HIGHEST, preferred_element_type=F32)


def _sel3(x, sel, dims, x_first):
    p1 = x.astype(BF16)
    r1 = x - p1.astype(F32)
    p2 = r1.astype(BF16)
    p3 = (r1 - p2.astype(F32)).astype(BF16)
    sel = sel.astype(BF16)
    out = None
    for piece in (p1, p2, p3):
        d = lax.dot_general(*((piece, sel) if x_first else (sel, piece)), (dims, ((), ())), preferred_element_type=F32)
        out = d if out is None else out + d
    return out


@jax.custom_vjp
def _sel_right(x, sel):
    return _sel3(x, sel, _NN, True)


_sel_right.defvjp(lambda x, sel: (_sel3(x, sel, _NN, True), sel),
                  lambda sel, g: (_sel3(g, sel, _NT, True), jnp.zeros_like(sel)))


@jax.custom_vjp
def _sel_left(sel, x):
    return _sel3(x, sel, _NN, False)


_sel_left.defvjp(lambda sel, x: (_sel3(x, sel, _NN, False), sel),
                 lambda sel, g: (jnp.zeros_like(sel), _sel3(g, sel, _TN, False)))


@jax.custom_vjp
def _sel_left_nt(sel, x):
    return _sel3(x, sel, _NT, False)


_sel_left_nt.defvjp(lambda sel, x: (_sel3(x, sel, _NT, False), sel),
                    lambda sel, g: (jnp.zeros_like(sel), _sel3(g, sel, _TN, True)))


def _mm(a, b, *, name, ta=False, tb=False, out_dtype=F32, tm=2048, tn=512, tk=2048, col_shards=False):
    m, k = (a.shape[1], a.shape[0]) if ta else a.shape
    n = b.shape[0] if tb else b.shape[1]
    assert k == (b.shape[1] if tb else b.shape[0]), (a.shape, b.shape, ta, tb)
    if col_shards:
        tn = n // N_DEV
    tm, tn, tk = _tile(m, tm), _tile(n, tn), _tile(k, tk)
    nk = k // tk
    a_spec = pl.BlockSpec((tk, tm), lambda i, j, kk: (kk, i)) if ta else pl.BlockSpec((tm, tk), lambda i, j, kk: (i, kk))
    b_spec = pl.BlockSpec((tn, tk), lambda i, j, kk: (j, kk)) if tb else pl.BlockSpec((tk, tn), lambda i, j, kk: (kk, j))
    dims = ((0 if ta else 1,), (1 if tb else 0,))
    out_spec = pl.BlockSpec((tm, tn), lambda i, j, kk: (i, j))
    out_shape = jax.ShapeDtypeStruct((m, n), out_dtype)
    if col_shards:
        out_shape = jax.ShapeDtypeStruct((N_DEV, 1, m, n // N_DEV), out_dtype)
        out_spec = pl.BlockSpec((None, None, tm, tn), lambda i, j, kk: (j, 0, i, 0))

    def body(a_ref, b_ref, o_ref, acc_ref):
        kk = pl.program_id(2)

        @pl.when(kk == 0)
        def _():
            acc_ref[...] = jnp.zeros_like(acc_ref)

        acc_ref[...] += _dot(a_ref[...], b_ref[...], dims)

        @pl.when(kk == nk - 1)
        def _():
            o_ref[...] = acc_ref[...].astype(o_ref.dtype)

    return _pcall(
        body, name=name, grid=(m // tm, n // tn, nk),
        in_specs=[a_spec, b_spec], out_specs=out_spec, out_shape=out_shape,
        scratch_shapes=[pltpu.VMEM((tm, tn), F32)],
        compiler_params=_params("parallel", "parallel", "arbitrary"),
    )(a, b)


def _rspec(tm, w, cb=0, percol=False):
    return pl.BlockSpec((tm, w), (lambda j, i: (i, cb + j)) if percol else (lambda j, i: (i, cb)))


def _bspec(w, cb=0, percol=False, rows=1):
    return pl.BlockSpec((rows, w), (lambda j, i: (0, cb + j)) if percol else (lambda j, i: (0, cb)))


def _rows(fn, *, name, nrow, ncol=1, ins, outs):
    n_in = len(ins)
    accs = [o[2] for o in outs]

    def body(*refs):
        vals = fn(*[r[...] for r in refs[:n_in]])
        if not isinstance(vals, (tuple, list)):
            vals = (vals,)
        i = pl.program_id(1)
        for ref, val, acc in zip(refs[n_in:], vals, accs):
            if acc:
                @pl.when(i == 0)
                def _(ref=ref):
                    ref[...] = jnp.zeros_like(ref)

                ref[...] += jnp.broadcast_to(val, ref.shape).astype(ref.dtype)
            else:
                ref[...] = val.astype(ref.dtype)

    res = _pcall(
        body, name=name, grid=(ncol, nrow),
        in_specs=[s for _, s in ins], out_specs=[o[1] for o in outs], out_shape=[o[0] for o in outs],
        compiler_params=_params("parallel", "arbitrary"),
    )(*[a for a, _ in ins])
    return res


def _sds(shape, dtype=F32):
    return jax.ShapeDtypeStruct(shape, dtype)


def _rms_fwd(h, g, *, name, tm):
    t, d = h.shape
    return _rows(lambda x, gg: _rms(x, gg), name=name, nrow=t // tm,
                 ins=[(h, _rspec(tm, d)), (g, _bspec(d))],
                 outs=[(_sds((t, d), BF16), _rspec(tm, d), False)])[0]


def _resid_fwd(h, f, g, scale, *, name, tm):
    t, d = h.shape
    return _rows(lambda x, ff, gg: x + scale * _rms(ff, gg), name=name, nrow=t // tm,
                 ins=[(h, _rspec(tm, d)), (f, _rspec(tm, d)), (g, _bspec(d))],
                 outs=[(_sds((t, d)), _rspec(tm, d), False)])[0]


def _resid_bwd(f, g, dh, scale, *, name, tm):
    t, d = f.shape

    def fn(ff, gg, dd):
        _, vjp = jax.vjp(lambda a, b: scale * _rms(a, b), ff, gg)
        return vjp(dd)

    return _rows(fn, name=name, nrow=t // tm,
                 ins=[(f, _rspec(tm, d)), (g, _bspec(d)), (dh, _rspec(tm, d))],
                 outs=[(_sds((t, d), BF16), _rspec(tm, d), False), (_sds((1, d)), _bspec(d), True)])


def _rms_bwd(h, g, dh, dxns, *, name, tm):
    t, d = h.shape

    def fn(x, gg, dd, *dx):
        _, vjp = jax.vjp(_rms, x, gg)
        tot = dx[0]
        for more in dx[1:]:
            tot = tot + more
        dxx, dg = vjp(tot)
        return dd + dxx, dg

    return _rows(fn, name=name, nrow=t // tm,
                 ins=[(h, _rspec(tm, d)), (g, _bspec(d)), (dh, _rspec(tm, d))] + [(x, _rspec(tm, d)) for x in dxns],
                 outs=[(_sds((t, d)), _rspec(tm, d), False), (_sds((1, d)), _bspec(d), True)])


FFN_BLOCKS = 4
NB = FFN_HIDDEN // FFN_BLOCKS
MM_ROWS = 2048


def _ffn_up(xn, wg, wu, *, name):
    t = xn.shape[0]
    tm = _tile(t, MM_ROWS // 2)
    wspec = pl.BlockSpec((None, None, NB, D_MODEL), lambda i, j: (j, 0, 0, 0))

    def body(x_ref, g_ref, u_ref, ab_ref, hh_ref):
        x = x_ref[...]
        a, b = _dot(x, g_ref[...], _NT), _dot(x, u_ref[...], _NT)
        ab_ref[0] = a.astype(ab_ref.dtype)
        ab_ref[1] = b.astype(ab_ref.dtype)
        hh_ref[...] = (_silu(a) * b).astype(hh_ref.dtype)

    return _pcall(
        body, name=name, grid=(t // tm, FFN_BLOCKS),
        in_specs=[pl.BlockSpec((tm, D_MODEL), lambda i, j: (i, 0)), wspec, wspec],
        out_specs=[pl.BlockSpec((None, 2, tm, NB), lambda i, j: (j, 0, i, 0)),
                   pl.BlockSpec((None, tm, NB), lambda i, j: (j, i, 0))],
        out_shape=[_sds((FFN_BLOCKS, 2, t, NB), BF16), _sds((FFN_BLOCKS, t, NB), BF16)],
        compiler_params=_params("parallel", "parallel"),
    )(xn, wg, wu)


def _ffn_down(hh, wd, *, name):
    t = hh.shape[1]
    tm = _tile(t, 512)

    def body(h_ref, w_ref, o_ref):
        acc = _dot(h_ref[0], w_ref[0, 0], _NN)
        for k in range(1, FFN_BLOCKS):
            acc = acc + _dot(h_ref[k], w_ref[k, 0], _NN)
        o_ref[...] = acc

    return _pcall(
        body, name=name, grid=(t // tm,),
        in_specs=[pl.BlockSpec((FFN_BLOCKS, tm, NB), lambda i: (0, i, 0)),
                  pl.BlockSpec((FFN_BLOCKS, 1, NB, D_MODEL), lambda i: (0, 0, 0, 0))],
        out_specs=pl.BlockSpec((tm, D_MODEL), lambda i: (i, 0)), out_shape=_sds((t, D_MODEL)),
        compiler_params=_params("parallel"),
    )(hh, wd)


def _ffn_down_dx(df, wd, ab, *, name):
    t = df.shape[0]
    tm = _tile(t, MM_ROWS // 2)

    def body(d_ref, w_ref, ab_ref, o_ref):
        dhh = _dot(d_ref[...], w_ref[...], _NT)
        _, vjp = jax.vjp(lambda a, b: _silu(a) * b, ab_ref[0].astype(F32), ab_ref[1].astype(F32))
        da, db = vjp(dhh)
        o_ref[0] = da.astype(o_ref.dtype)
        o_ref[1] = db.astype(o_ref.dtype)

    blk = pl.BlockSpec((None, 2, tm, NB), lambda i, j: (j, 0, i, 0))
    return _pcall(
        body, name=name, grid=(t // tm, FFN_BLOCKS),
        in_specs=[pl.BlockSpec((tm, D_MODEL), lambda i, j: (i, 0)),
                  pl.BlockSpec((None, None, NB, D_MODEL), lambda i, j: (j, 0, 0, 0)), blk],
        out_specs=blk, out_shape=_sds((FFN_BLOCKS, 2, t, NB), BF16), compiler_params=_params("parallel", "parallel"),
    )(df, wd, ab)


def _ffn_down_dw(hh, df, *, name, tn=512):
    t = df.shape[0]
    tk = _tile(t, 2048)
    nk = t // tk

    def body(h_ref, d_ref, o_ref, acc_ref):
        kk = pl.program_id(2)

        @pl.when(kk == 0)
        def _():
            acc_ref[...] = jnp.zeros_like(acc_ref)

        acc_ref[...] += _dot(h_ref[...], d_ref[...], _TN)

        @pl.when(kk == nk - 1)
        def _():
            o_ref[...] = acc_ref[...].astype(o_ref.dtype)

    return _pcall(
        body, name=name, grid=(FFN_BLOCKS, D_MODEL // tn, nk),
        in_specs=[pl.BlockSpec((None, tk, NB), lambda j, n, kk: (j, kk, 0)),
                  pl.BlockSpec((tk, tn), lambda j, n, kk: (kk, n))],
        out_specs=pl.BlockSpec((None, None, NB, tn), lambda j, n, kk: (j, 0, 0, n)),
        out_shape=_sds((FFN_BLOCKS, 1, NB, D_MODEL), BF16),
        scratch_shapes=[pltpu.VMEM((NB, tn), F32)],
        compiler_params=_params("parallel", "parallel", "arbitrary"),
    )(hh, df)


def _ffn_up_dx(dab, wg, wu, *, name):
    t = dab.shape[2]
    tm = _tile(t, MM_ROWS // 2)
    wspec = pl.BlockSpec((None, None, NB, D_MODEL), lambda i, j: (j, 0, 0, 0))

    def body(d_ref, g_ref, u_ref, o_ref):
        @pl.when(pl.program_id(1) == 0)
        def _():
            o_ref[...] = jnp.zeros_like(o_ref)

        o_ref[...] += _dot(d_ref[0], g_ref[...], _NN) + _dot(d_ref[1], u_ref[...], _NN)

    return _pcall(
        body, name=name, grid=(t // tm, FFN_BLOCKS),
        in_specs=[pl.BlockSpec((None, 2, tm, NB), lambda i, j: (j, 0, i, 0)), wspec, wspec],
        out_specs=pl.BlockSpec((tm, D_MODEL), lambda i, j: (i, 0)), out_shape=_sds((t, D_MODEL)),
        compiler_params=_params("parallel", "arbitrary"),
    )(dab, wg, wu)


def _ffn_up_dw(xn, dab, *, name):
    t = xn.shape[0]

    def body(x_ref, d_ref, og_ref, ou_ref):
        x = x_ref[...]
        og_ref[...] = _dot(d_ref[0], x, _TN).astype(og_ref.dtype)
        ou_ref[...] = _dot(d_ref[1], x, _TN).astype(ou_ref.dtype)

    out = pl.BlockSpec((None, None, NB, D_MODEL), lambda j: (j, 0, 0, 0))
    return _pcall(
        body, name=name, grid=(FFN_BLOCKS,),
        in_specs=[pl.BlockSpec((t, D_MODEL), lambda j: (0, 0)), pl.BlockSpec((None, 2, t, NB), lambda j: (j, 0, 0, 0))],
        out_specs=[out, out], out_shape=[_sds((FFN_BLOCKS, 1, NB, D_MODEL), BF16)] * 2,
        compiler_params=_params("parallel"),
    )(xn, dab)


def _paired(a):
    return a.reshape(FFN_BLOCKS, 1, NB, D_MODEL)


def _ffn_fwd(h, p, tag, tm):
    xn = _rms_fwd(h, p['pre_g'], name=f"{tag}_rms", tm=tm)
    ab, hh = _ffn_up(xn, _paired(p['w_gate']), _paired(p['w_up']), name=f"{tag}_up")
    w_down, hh = lax.optimization_barrier((p['w_down'], hh))
    f = _ffn_down(hh, _paired(w_down), name=f"{tag}_down")
    out = _resid_fwd(h, f, p['post_g'], 0.5, name=f"{tag}_res", tm=tm)
    return out, (h, xn, ab, hh, f)


def _ffn_bwd(dh, p, saved, tag, tm, after_first):
    h, xn, ab, hh, f = saved
    df, dpost = _resid_bwd(f, p['post_g'], dh, 0.5, name=f"{tag}_res_bwd", tm=tm)
    df = after_first(df)
    dab = _ffn_down_dx(df, _paired(p['w_down']), ab, name=f"{tag}_down_dx")
    bufs = {f'{tag}_w_down': _ffn_down_dw(hh, df, name=f"{tag}_down_dw")}
    dxn = _ffn_up_dx(dab, _paired(p['w_gate']), _paired(p['w_up']), name=f"{tag}_up_dx")
    bufs[f'{tag}_w_gate'], bufs[f'{tag}_w_up'] = _ffn_up_dw(xn, dab, name=f"{tag}_up_dw")
    bufs = {n: a.reshape(N_DEV, 1, FFN_HIDDEN // N_DEV, D_MODEL) for n, a in bufs.items()}
    dh_in, dpre = _rms_bwd(h, p['pre_g'], dh, [dxn], name=f"{tag}_rms_bwd", tm=tm)
    return dh_in, bufs, {f'{tag}_pre_g': dpre, f'{tag}_post_g': dpost}


CONV_COLS = 256


def _shift_down(x, s):
    rows = lax.broadcasted_iota(jnp.int32, x.shape, 0)
    return jnp.where(rows >= s, pltpu.roll(x, s, axis=0), 0.0)


def _shift_up(x, s):
    t = x.shape[0]
    rows = lax.broadcasted_iota(jnp.int32, x.shape, 0)
    return jnp.where(rows < t - s, pltpu.roll(x, t - s, axis=0), 0.0)


def _conv_fwd(proj, col0, w, b, *, name):
    t = proj.shape[0]
    c = w.shape[1]
    cb0 = col0 // CONV_COLS

    def body(x_ref, w_ref, b_ref, o_ref):
        x = x_ref[...]
        acc = x * w_ref[3:4, :] + b_ref[...]
        for k in range(SSD_CONV_WIDTH - 1):
            acc = acc + _shift_down(x, SSD_CONV_WIDTH - 1 - k) * w_ref[k:k + 1, :]
        o_ref[...] = _silu(acc)

    return _pcall(
        body, name=name, grid=(c // CONV_COLS,),
        in_specs=[pl.BlockSpec((t, CONV_COLS), lambda j: (0, cb0 + j)),
                  pl.BlockSpec((SSD_CONV_WIDTH, CONV_COLS), lambda j: (0, j)),
                  pl.BlockSpec((1, CONV_COLS), lambda j: (0, j))],
        out_specs=pl.BlockSpec((t, CONV_COLS), lambda j: (0, j)),
        out_shape=_sds((t, c)), compiler_params=_params("parallel"),
    )(proj, w, b)


def _conv_bwd(proj, col0, w, b, douts, *, name):
    t = proj.shape[0]
    c = w.shape[1]
    cb0 = col0 // CONV_COLS
    first = np.cumsum([0] + [d.shape[1] // CONV_COLS for d in douts])

    def body(x_ref, w_ref, b_ref, *rest):
        d_refs, (dx_ref, dw_ref, db_ref) = rest[:len(douts)], rest[len(douts):]
        j = pl.program_id(0)
        dout = d_refs[-1][...]
        for k in range(len(douts) - 2, -1, -1):
            dout = jnp.where(j < int(first[k + 1]), d_refs[k][...], dout)
        x = x_ref[...]
        shifted = [_shift_down(x, SSD_CONV_WIDTH - 1 - k) for k in range(SSD_CONV_WIDTH - 1)] + [x]
        pre = b_ref[...] + shifted[3] * w_ref[3:4, :]
        for k in range(SSD_CONV_WIDTH - 1):
            pre = pre + shifted[k] * w_ref[k:k + 1, :]
        sg = _sigmoid(pre)
        dpre = dout * (sg * (1.0 + pre * (1.0 - sg)))
        dx = dpre * w_ref[3:4, :]
        for k in range(SSD_CONV_WIDTH - 1):
            dx = dx + _shift_up(dpre, SSD_CONV_WIDTH - 1 - k) * w_ref[k:k + 1, :]
        dx_ref[...] = dx.astype(dx_ref.dtype)
        for k in range(SSD_CONV_WIDTH):
            dw_ref[k:k + 1, :] = jnp.sum(dpre * shifted[k], axis=0, keepdims=True)
        db_ref[...] = jnp.sum(dpre, axis=0, keepdims=True)

    return _pcall(
        body, name=name, grid=(c // CONV_COLS,),
        in_specs=[pl.BlockSpec((t, CONV_COLS), lambda j: (0, cb0 + j)),
                  pl.BlockSpec((SSD_CONV_WIDTH, CONV_COLS), lambda j: (0, j)),
                  pl.BlockSpec((1, CONV_COLS), lambda j: (0, j))] +
                 [pl.BlockSpec((t, CONV_COLS), lambda j, lo=int(first[k]), hi=int(first[k + 1]): (0, jnp.clip(j, lo, hi - 1) - lo))
                  for k in range(len(douts))],
        out_specs=[pl.BlockSpec((t, CONV_COLS), lambda j: (0, j)),
                   pl.BlockSpec((SSD_CONV_WIDTH, CONV_COLS), lambda j: (0, j)),
                   pl.BlockSpec((1, CONV_COLS), lambda j: (0, j))],
        out_shape=[_sds((t, c), BF16), _sds((SSD_CONV_WIDTH, c)), _sds((1, c))],
        compiler_params=_params("arbitrary"),
    )(proj, w, b, *douts)


HALF = 256
HEADS_PER_HALF = 4
PAD_HEADS = 128


def _head_expanders():
    k = lax.broadcasted_iota(jnp.int32, (PAD_HEADS, HALF), 0)
    j = lax.broadcasted_iota(jnp.int32, (PAD_HEADS, HALF), 1)
    kt = lax.broadcasted_iota(jnp.int32, (HALF, PAD_HEADS), 1)
    jt = lax.broadcasted_iota(jnp.int32, (HALF, PAD_HEADS), 0)
    es, ets = [], []
    for half in range(2):
        es.append(jnp.where(k == j // SSD_HEAD_DIM + half * HEADS_PER_HALF, 1.0, 0.0).astype(F32))
        ets.append(jnp.where(kt == jt // SSD_HEAD_DIM + half * HEADS_PER_HALF, 1.0, 0.0).astype(F32))
    return es, ets


def _ssd_chunk(x_lo, x_hi, bm, cm, dtr, dtb8, alog8, dsk8, s_lo, s_hi):
    q = x_lo.shape[0]
    es, ets = _head_expanders()
    rowmean = lambda v: jnp.sum(v, axis=0, keepdims=True) * 0.125
    dt = _softplus(dtr + rowmean(dtb8))
    a = -jnp.exp(rowmean(alog8))
    adt = a * dt
    adt_tot8 = jnp.broadcast_to(jnp.sum(adt, axis=0, keepdims=True), (8, PAD_HEADS))
    ll = lax.broadcasted_iota(jnp.int32, (q, q), 0)
    ss = lax.broadcasted_iota(jnp.int32, (q, q), 1)
    ltri = jnp.where(ll >= ss, 1.0, 0.0).astype(F32)
    lane = lax.broadcasted_iota(jnp.int32, (1, HALF), 1)
    cb = _bdot_nt(cm, bm)
    outs = []
    for half, (x, s_in) in enumerate(((x_lo, s_lo), (x_hi, s_hi))):
        e, et = es[half], ets[half]
        dtf = _sel_right(dt, e)
        af = rowmean(_sel_right(jnp.broadcast_to(a, (8, PAD_HEADS)), e)) * dtf
        dskf = rowmean(_sel_right(dsk8, e))
        acum = _sel_left(ltri, af)
        alast = jnp.sum(af, axis=0, keepdims=True)
        xdt = x * dtf
        ydiag = jnp.zeros((q, HALF), F32)
        for r in range(HEADS_PER_HALF):
            sel = lane == r * SSD_HEAD_DIM
            ac_r = jnp.sum(jnp.where(sel, acum, 0.0), axis=1, keepdims=True)
            a_r = jnp.sum(jnp.where(sel, af, 0.0), axis=1, keepdims=True)
            arow = jnp.sum(jnp.where(ll <= ss, a_r, 0.0), axis=0, keepdims=True)
            decay = jnp.exp(jnp.where(ll >= ss, ac_r - arow, -jnp.inf))
            yh = _bdot_nn(cb * decay, xdt)
            ydiag = ydiag + jnp.where(lane // SSD_HEAD_DIM == r, yh, 0.0)
        st = _bdot_tn(xdt * jnp.exp(alast - acum), bm)
        yoff = _bdot_nt(cm, s_in) * jnp.exp(acum)
        y = ydiag + yoff + dskf * x
        alast_col = jnp.sum(_sel_left_nt(et, adt_tot8), axis=1, keepdims=True) * 0.125
        outs.append((y, jnp.exp(alast_col) * s_in + st))
    return outs[0][0], outs[1][0], outs[0][1], outs[1][1]


def _ssd_specs(t, rev):
    q = SSD_CHUNK
    nc = t // q
    ci = (lambda c: nc - 1 - c) if rev else (lambda c: c)
    xcol0 = SSD_D_INNER // SSD_STATE
    return dict(
        x_lo=pl.BlockSpec((q, HALF), lambda g, c: (ci(c), 2 * g)),
        x_hi=pl.BlockSpec((q, HALF), lambda g, c: (ci(c), 2 * g + 1)),
        bm=pl.BlockSpec((q, SSD_STATE), lambda g, c: (ci(c), xcol0 + g)),
        cm=pl.BlockSpec((q, SSD_STATE), lambda g, c: (ci(c), xcol0 + SSD_GROUPS + g)),
        dt=pl.BlockSpec((None, q, PAD_HEADS), lambda g, c: (g, ci(c), 0)),
        par=pl.BlockSpec((None, 8, PAD_HEADS), lambda g, c: (g, 0, 0)),
        st=pl.BlockSpec((None, None, 2, HALF, SSD_STATE), lambda g, c: (ci(c), g, 0, 0, 0)),
        y=pl.BlockSpec((q, 2 * HALF), lambda g, c: (ci(c), g)),
        grp=pl.BlockSpec((q, SSD_STATE), lambda g, c: (ci(c), g)),
    )


def _ssd_fwd(xc, dt4, dtb, alog, dsk, *, name):
    t = xc.shape[0]
    nc = t // SSD_CHUNK
    sp = _ssd_specs(t, False)

    def body(xl, xh, bm, cm, dt, p0, p1, p2, y_ref, sin_ref, st_ref):
        @pl.when(pl.program_id(1) == 0)
        def _():
            st_ref[...] = jnp.zeros_like(st_ref)

        sin_ref[...] = st_ref[...]
        y_lo, y_hi, so_lo, so_hi = _ssd_chunk(xl[...], xh[...], bm[...], cm[...], dt[...], p0[...], p1[...],
                                              p2[...], st_ref[0], st_ref[1])
        y_ref[:, :HALF] = y_lo
        y_ref[:, HALF:] = y_hi
        st_ref[0] = so_lo
        st_ref[1] = so_hi

    return _pcall(
        body, name=name, grid=(SSD_GROUPS, nc),
        in_specs=[sp['x_lo'], sp['x_hi'], sp['bm'], sp['cm'], sp['dt'], sp['par'], sp['par'], sp['par']],
        out_specs=[sp['y'], sp['st']],
        out_shape=[_sds((t, SSD_D_INNER)), _sds((nc, SSD_GROUPS, 2, HALF, SSD_STATE))],
        scratch_shapes=[pltpu.VMEM((2, HALF, SSD_STATE), F32)],
        compiler_params=_params("parallel", "arbitrary"),
    )(xc, xc, xc, xc, dt4, dtb, alog, dsk)


def _ssd_bwd(xc, dt4, dtb, alog, dsk, sin, dy, *, name):
    t = xc.shape[0]
    nc = t // SSD_CHUNK
    sp = _ssd_specs(t, True)

    def body(xl, xh, bm, cm, dt, p0, p1, p2, sin_ref, dy_ref,
             dx_ref, db_ref, dc_ref, ddt_ref, dp0, dp1, dp2, dst_ref):
        first = pl.program_id(1) == 0

        @pl.when(first)
        def _():
            dst_ref[...] = jnp.zeros_like(dst_ref)

        _, vjp = jax.vjp(_ssd_chunk, xl[...], xh[...], bm[...], cm[...], dt[...], p0[...], p1[...], p2[...],
                         sin_ref[0], sin_ref[1])
        dxl, dxh, dbm, dcm, ddt, g0, g1, g2, ds_lo, ds_hi = vjp(
            (dy_ref[:, :HALF], dy_ref[:, HALF:], dst_ref[0], dst_ref[1]))
        dx_ref[:, :HALF] = dxl
        dx_ref[:, HALF:] = dxh
        db_ref[...] = dbm
        dc_ref[...] = dcm
        ddt_ref[...] = ddt
        dst_ref[0] = ds_lo
        dst_ref[1] = ds_hi
        for ref, g in ((dp0, g0), (dp1, g1), (dp2, g2)):
            tot = jnp.broadcast_to(jnp.sum(g, axis=0, keepdims=True), ref.shape)

            @pl.when(first)
            def _(ref=ref):
                ref[...] = jnp.zeros_like(ref)

            ref[...] += tot

    return _pcall(
        body, name=name, grid=(SSD_GROUPS, nc),
        in_specs=[sp['x_lo'], sp['x_hi'], sp['bm'], sp['cm'], sp['dt'], sp['par'], sp['par'], sp['par'],
                  sp['st'], sp['y']],
        out_specs=[sp['y'], sp['grp'], sp['grp'], sp['dt'], sp['par'], sp['par'], sp['par']],
        out_shape=[_sds((t, SSD_D_INNER)), _sds((t, SSD_GROUPS * SSD_STATE)), _sds((t, SSD_GROUPS * SSD_STATE)),
                   _sds((SSD_GROUPS, t, PAD_HEADS))] + [_sds((SSD_GROUPS, 8, PAD_HEADS))] * 3,
        scratch_shapes=[pltpu.VMEM((2, HALF, SSD_STATE), F32)],
        compiler_params=_params("parallel", "arbitrary"),
    )(xc, xc, xc, xc, dt4, dtb, alog, dsk, sin, dy)


def _gatenorm(y, z, g):
    v = y * _silu(z)
    return v * lax.rsqrt(jnp.mean(v * v, axis=-1, keepdims=True) + RMS_EPS) * g


S5_CH = S5_WIDTH // S5_BLOCKS
S5_ST = S5_CH * S5_STATE // S5_GROUP
SCAN_UNROLL = 8


def _cmul(ar, ai, br, bi):
    return ar * br - ai * bi, ar * bi + ai * br


def _segment_power(ar, ai, n):
    assert n & (n - 1) == 0
    for _ in range(n.bit_length() - 1):
        ar, ai = _cmul(ar, ai, ar, ai)
    return ar, ai


def _carry_in(fr, fi, pr, pi, reverse):
    rows = lax.broadcasted_iota(jnp.int32, fr.shape, 0)
    cr = jnp.zeros_like(fr[0:1])
    ci = jnp.zeros_like(cr)
    outr = jnp.zeros_like(fr)
    outi = jnp.zeros_like(fr)
    order = range(6, -1, -1) if reverse else range(1, 8)
    for j in order:
        src = j + 1 if reverse else j - 1
        nr, ni = _cmul(pr[0:1], pi[0:1], cr, ci)
        cr, ci = nr + fr[src:src + 1], ni + fi[src:src + 1]
        outr = jnp.where(rows == j, cr, outr)
        outi = jnp.where(rows == j, ci, outi)
    return outr, outi


def _s5_specs(t):
    return dict(ch=pl.BlockSpec((t, S5_CH), lambda j: (0, j)), st=pl.BlockSpec((t, S5_ST), lambda j: (0, j)),
                lam=pl.BlockSpec((1, S5_ST), lambda j: (0, j)),
                b=pl.BlockSpec((None, S5_CH, S5_ST), lambda j: (j, 0, 0)),
                c=pl.BlockSpec((None, S5_ST, S5_CH), lambda j: (j, 0, 0)))


def _s5_fwd(u5, bre, bim, cre, cim, lr, li, *, name):
    t = u5.shape[0]
    nrt = t // 8

    def body(u_ref, bre_ref, bim_ref, cre_ref, cim_ref, lr_ref, li_ref, sr_ref, si_ref, y_ref, br_ref, bi_ref):
        u = u_ref[...]
        br_ref[...] = _dot(u, bre_ref[...], _NN)
        bi_ref[...] = _dot(u, bim_ref[...], _NN)
        ar = jnp.broadcast_to(lr_ref[...], (8, S5_ST))
        ai = jnp.broadcast_to(li_ref[...], (8, S5_ST))

        def step(r, s, store):
            rows = pl.ds(pl.multiple_of(r * 8, 8), 8)
            nr, ni = _cmul(ar, ai, s[0], s[1])
            nr, ni = nr + br_ref[rows, :], ni + bi_ref[rows, :]
            if store:
                sr_ref[rows, :] = nr
                si_ref[rows, :] = ni
            return nr, ni

        zero = (jnp.zeros((8, S5_ST), F32), jnp.zeros((8, S5_ST), F32))
        fr, fi = lax.fori_loop(0, nrt, lambda r, s: step(r, s, False), zero, unroll=SCAN_UNROLL)
        pr, pi = _segment_power(ar, ai, nrt)
        init = _carry_in(fr, fi, pr, pi, False)
        lax.fori_loop(0, nrt, lambda r, s: step(r, s, True), init, unroll=SCAN_UNROLL)
        y_ref[...] = _dot(sr_ref[...], cre_ref[...], _NN) - _dot(si_ref[...], cim_ref[...], _NN)

    sp = _s5_specs(t)
    w = S5_BLOCKS * S5_ST
    return _pcall(
        body, name=name, grid=(S5_BLOCKS,),
        in_specs=[sp['ch'], sp['b'], sp['b'], sp['c'], sp['c'], sp['lam'], sp['lam']],
        out_specs=[sp['st'], sp['st'], sp['ch']], out_shape=[_sds((t, w)), _sds((t, w)), _sds((t, S5_WIDTH))],
        scratch_shapes=[pltpu.VMEM((t, S5_ST), F32)] * 2, compiler_params=_params("parallel"),
    )(u5, bre, bim, cre, cim, lr, li)


def _s5_bwd(dy, du_direct, u5, sr, si, bre, bim, cre, cim, lr, li, *, name):
    t = u5.shape[0]
    nrt = t // 8

    def body(dy_ref, dd_ref, u_ref, sr_ref, si_ref, bre_ref, bim_ref, cre_ref, cim_ref, lr_ref, li_ref,
             du_ref, dbre_ref, dbim_ref, dcre_ref, dcim_ref, dlr_ref, dli_ref, gr_ref, gi_ref):
        dyv = dy_ref[...]
        gr_ref[...] = _dot(dyv, cre_ref[...], _NT)
        gi_ref[...] = -_dot(dyv, cim_ref[...], _NT)
        dcre_ref[...] = _dot(sr_ref[...], dyv, _TN)
        dcim_ref[...] = -_dot(si_ref[...], dyv, _TN)
        dr_ref, di_ref = gr_ref, gi_ref
        ar = jnp.broadcast_to(lr_ref[...], (8, S5_ST))
        ai = -jnp.broadcast_to(li_ref[...], (8, S5_ST))
        zero = jnp.zeros((8, S5_ST), F32)

        def step1(k, g):
            rows = pl.ds(pl.multiple_of((nrt - 1 - k) * 8, 8), 8)
            nr, ni = _cmul(ar, ai, g[0], g[1])
            return nr + dr_ref[rows, :], ni + di_ref[rows, :]

        fr, fi = lax.fori_loop(0, nrt, step1, (zero, zero), unroll=SCAN_UNROLL)
        pr, pi = _segment_power(ar, ai, nrt)
        init = _carry_in(fr, fi, pr, pi, True)

        def step2(k, carry):
            gr, gi, accr, acci = carry
            r = nrt - 1 - k
            rows = pl.ds(pl.multiple_of(r * 8, 8), 8)
            prev = pl.ds(pl.multiple_of(jnp.maximum(r - 1, 0) * 8, 8), 8)
            nr, ni = _cmul(ar, ai, gr, gi)
            nr, ni = nr + dr_ref[rows, :], ni + di_ref[rows, :]
            gr_ref[rows, :] = nr
            gi_ref[rows, :] = ni
            keep = jnp.where(r > 0, 1.0, 0.0)
            pr_, pi_ = sr_ref[prev, :] * keep, si_ref[prev, :] * keep
            return nr, ni, accr + (pr_ * nr + pi_ * ni), acci + (pr_ * ni - pi_ * nr)

        _, _, accr, acci = lax.fori_loop(0, nrt, step2, (init[0], init[1], zero, zero), unroll=SCAN_UNROLL)
        last = pl.ds((nrt - 1) * 8, 8)
        pr_, pi_ = _shift_down(sr_ref[last, :], 1), _shift_down(si_ref[last, :], 1)
        g0r, g0i = gr_ref[0:8, :], gi_ref[0:8, :]
        accr = accr + (pr_ * g0r + pi_ * g0i)
        acci = acci + (pr_ * g0i - pi_ * g0r)
        dlr_ref[...] = jnp.sum(accr, axis=0, keepdims=True)
        dli_ref[...] = jnp.sum(acci, axis=0, keepdims=True)
        u = u_ref[...]
        dbre_ref[...] = _dot(u, gr_ref[...], _TN)
        dbim_ref[...] = _dot(u, gi_ref[...], _TN)
        du = dd_ref[...] + _dot(gr_ref[...], bre_ref[...], _NT) + _dot(gi_ref[...], bim_ref[...], _NT)
        du_ref[...] = du.astype(du_ref.dtype)

    sp = _s5_specs(t)
    w = S5_BLOCKS * S5_ST
    return _pcall(
        body, name=name, grid=(S5_BLOCKS,),
        in_specs=[sp['ch'], sp['ch'], sp['ch'], sp['st'], sp['st'], sp['b'], sp['b'], sp['c'], sp['c'], sp['lam'], sp['lam']],
        out_specs=[sp['ch'], sp['b'], sp['b'], sp['c'], sp['c'], sp['lam'], sp['lam']],
        out_shape=[_sds((t, S5_WIDTH), BF16), _sds((S5_BLOCKS, S5_CH, S5_ST)), _sds((S5_BLOCKS, S5_CH, S5_ST)),
                   _sds((S5_BLOCKS, S5_ST, S5_CH)), _sds((S5_BLOCKS, S5_ST, S5_CH)), _sds((1, w)), _sds((1, w))],
        scratch_shapes=[pltpu.VMEM((t, S5_ST), F32)] * 2, compiler_params=_params("parallel"),
    )(dy, du_direct, u5, sr, si, bre, bim, cre, cim, lr, li)


def _s5_expander():
    n = lax.broadcasted_iota(jnp.int32, (S5_STATE, S5_STATE * S5_GROUP), 0)
    j = lax.broadcasted_iota(jnp.int32, (S5_STATE, S5_STATE * S5_GROUP), 1)
    return jnp.where(n == j // S5_GROUP, 1.0, 0.0).astype(F32)


def _s5_discretise(lam_re, lam_im, log_step, b_re, b_im):
    lr = jnp.minimum(lam_re, S5_MAX_REAL)
    step = jnp.exp(log_step)
    mag = jnp.exp(lr * step)
    ang = lam_im * step
    lbr, lbi = mag * jnp.cos(ang), mag * jnp.sin(ang)
    p, q = lbr - 1.0, lbi
    den = lr * lr + lam_im * lam_im
    cr, ci = (p * lr + q * lam_im) / den, (q * lr - p * lam_im) / den
    e = _s5_expander()
    cre, cie = _fdot(cr, e), _fdot(ci, e)
    return lbr, lbi, cre * b_re - cie * b_im, cre * b_im + cie * b_re


def _s5_params_fwd(lam_re, lam_im, log_step, b_re, b_im, *, name):
    g, n, w = lam_re.shape[0], S5_STATE, S5_STATE * S5_GROUP

    def body(a, b, c, d, e, o0, o1, o2, o3):
        for ref, val in zip((o0, o1, o2, o3), _s5_discretise(a[...], b[...], c[...], d[...], e[...])):
            ref[...] = val

    return _pcall(body, name=name, out_shape=[_sds((g, n)), _sds((g, n)), _sds((g, w)), _sds((g, w))])(
        lam_re, lam_im, log_step, b_re, b_im)


def _s5_params_bwd(lam_re, lam_im, log_step, b_re, b_im, cts, *, name):
    g, n, w = S5_GROUPS, S5_STATE, S5_STATE * S5_GROUP

    def body(a, b, c, d, e, c0, c1, c2, c3, o0, o1, o2, o3, o4):
        _, vjp = jax.vjp(_s5_discretise, a[...], b[...], c[...], d[...], e[...])
        for ref, val in zip((o0, o1, o2, o3, o4), vjp((c0[...], c1[...], c2[...], c3[...]))):
            ref[...] = val

    return _pcall(body, name=name,
                  out_shape=[_sds((g, n)), _sds((g, n)), _sds((g, 1)), _sds((g, w)), _sds((g, w))])(
        lam_re, lam_im, log_step, b_re, b_im, *cts)


def _s5_prepare(w):
    rows = DEPTH * S5_GROUPS
    lbr, lbi, bbr, bbi = _s5_params_fwd(
        w['s5_lambda_re'].reshape(rows, -1), w['s5_lambda_im'].reshape(rows, -1), w['s5_log_step'].reshape(rows, 1),
        w['s5_b_re'].reshape(rows, -1), w['s5_b_im'].reshape(rows, -1), name="s5_par")
    bd = lambda m: _blockdiag(m.reshape(rows, S5_STATE, S5_GROUP).transpose(0, 2, 1), S5_GROUP, S5_STATE).astype(BF16)
    cd = lambda m: _blockdiag(m.reshape(rows, S5_GROUP, S5_STATE).transpose(0, 2, 1), S5_STATE, S5_GROUP).astype(BF16)
    bre, bim, cre, cim = bd(bbr), bd(bbi), cd(w['s5_c_re']), cd(w['s5_c_im'])
    lr, li = lbr.reshape(DEPTH, 1, -1), lbi.reshape(DEPTH, 1, -1)
    blk = lambda a, i: a[i * S5_BLOCKS:(i + 1) * S5_BLOCKS]
    return [dict(bre=blk(bre, i), bim=blk(bim, i), cre=blk(cre, i), cim=blk(cim, i), lr=lr[i], li=li[i])
            for i in range(DEPTH)]


def _perm(a):
    t, c = a.shape
    return a.reshape(8, t // 8, c).transpose(1, 0, 2).reshape(t, c)


def _unperm(a):
    t, c = a.shape
    return a.reshape(t // 8, 8, c).transpose(1, 0, 2).reshape(t, c)


def _blockdiag(m, rows_inner, cols_inner):
    nblk = m.shape[0] // 8
    m = m.reshape(nblk, 8, rows_inner, cols_inner)
    eye = jnp.eye(8, dtype=m.dtype)
    out = m[:, :, :, None, :] * eye[None, :, None, :, None]
    return out.reshape(nblk, 8 * rows_inner, 8 * cols_inner)


def _blockdiag_extract(m, rows_inner, cols_inner):
    m = m.reshape(S5_BLOCKS, 8, rows_inner, 8, cols_inner)
    d = jnp.diagonal(m, axis1=1, axis2=3)
    return d.transpose(0, 3, 1, 2).reshape(S5_GROUPS, rows_inner, cols_inner)


Z0, XBC0, GA0, GB0 = 0, SSD_D_INNER, SSD_D_INNER + SSD_CONV_DIM, SSD_D_INNER + SSD_CONV_DIM + D_MODEL
BIG = GB0 + D_MODEL


def _mixer_fwd(h, p, tm):
    t = h.shape[0]
    nrow = t // tm
    u = _rms_fwd(h, p['pre_g'], name="mix_rms", tm=tm)
    u_p = _perm(u)
    proj = _mm(u, p['w_big'], name="mix_in")
    dtr = _mm(u, p['w_dt'], name="mix_in_dt")
    u5 = _mm(u_p, p['w_u5'], name="mix_in_s5")
    late, proj = lax.optimization_barrier((p['late'], proj))
    by_rows = lambda a: a.reshape(-1, a.shape[-1])
    p = dict(p, w_a=by_rows(late['w_branch_a']), w_b=by_rows(late['w_branch_b']), w_out=by_rows(late['w_out']),
             w_glu=late['s5_w_glu'][:, 0].transpose(1, 0, 2).reshape(late['s5_w_glu'].shape[2], -1))
    xc = _conv_fwd(proj, XBC0, p['conv_w'], p['conv_b'], name="ssd_conv")
    dt4 = jnp.pad(dtr.reshape(t, SSD_GROUPS, 8).transpose(1, 0, 2), ((0, 0), (0, 0), (0, PAD_HEADS - 8)))
    y_ssd, s_in = _ssd_fwd(xc, dt4, p['dt_bias8'], p['a_log8'], p['d8'], name="ssd_scan")
    gw = SSD_D_INNER // SSD_GROUPS
    ya = _rows(_gatenorm, name="ssd_gate", nrow=nrow, ncol=SSD_GROUPS,
               ins=[(y_ssd, _rspec(tm, gw, 0, True)), (proj, _rspec(tm, gw, Z0 // gw, True)),
                    (p['norm_g'], _bspec(gw, 0, True))],
               outs=[(_sds((t, SSD_D_INNER), BF16), _rspec(tm, gw, 0, True), False)])[0]
    y_a = _mm(ya, p['w_a'], name="mix_a")
    bre, bim, cre, cim, lr, li = (p['s5'][k] for k in ('bre', 'bim', 'cre', 'cim', 'lr', 'li'))
    sr, si, y5 = _s5_fwd(u5, bre, bim, cre, cim, lr, li, name="s5_scan")
    y5g = _rows(lambda a, b, d: _gelu(a + d * b), name="s5_act", nrow=nrow,
                ins=[(y5, _rspec(tm, S5_WIDTH)), (u5, _rspec(tm, S5_WIDTH)), (p['s5_d'], _bspec(S5_WIDTH))],
                outs=[(_sds((t, S5_WIDTH), BF16), _rspec(tm, S5_WIDTH), False)])[0]
    vg = _mm(y5g, p['w_glu'], name="s5_glu")
    ybin = _rows(lambda a, b: a * _sigmoid(b), name="s5_glu_act", nrow=nrow,
                 ins=[(vg, _rspec(tm, S5_WIDTH, 0)), (vg, _rspec(tm, S5_WIDTH, 1))],
                 outs=[(_sds((t, S5_WIDTH), BF16), _rspec(tm, S5_WIDTH), False)])[0]
    y_b = _unperm(_mm(ybin, p['w_b'], name="mix_b"))
    merged = _rows(lambda ga, gb, a, b: _sigmoid(ga) * a + _sigmoid(gb) * b, name="mix_merge", nrow=nrow,
                   ins=[(proj, _rspec(tm, D_MODEL, GA0 // D_MODEL)), (proj, _rspec(tm, D_MODEL, GB0 // D_MODEL)),
                        (y_a, _rspec(tm, D_MODEL)), (y_b, _rspec(tm, D_MODEL))],
                   outs=[(_sds((t, D_MODEL), BF16), _rspec(tm, D_MODEL), False)])[0]
    m = _mm(merged, p['w_out'], name="mix_out")
    out = _resid_fwd(h, m, p['post_g'], 1.0, name="mix_res", tm=tm)
    saved = dict(w_a=p['w_a'], w_b=p['w_b'], w_out=p['w_out'], w_glu=p['w_glu'], h=h, u=u, u_p=u_p, proj=proj, u5=u5, xc=xc, dt4=dt4, s_in=s_in, y_ssd=y_ssd, ya=ya, y_a=y_a,
                 bre=bre, bim=bim, cre=cre, cim=cim, lr=lr, li=li, sr=sr, si=si, y5=y5, y5g=y5g, vg=vg, ybin=ybin,
                 y_b=y_b, merged=merged, m=m)
    return out, saved


def _mixer_bwd(dh, p, s, tm, after_first):
    t = dh.shape[0]
    nrow = t // tm
    proj = s['proj']
    bufs = {}

    def grad_mm(a, b, wname, axis, name):
        dw = _mm(a, b, ta=True, name=name, out_dtype=BF16, col_shards=axis == 'cols')
        bufs[wname] = dw if axis == 'cols' else dw.reshape(N_DEV, 1, dw.shape[0] // N_DEV, dw.shape[1])

    dm, dpost = _resid_bwd(s['m'], p['post_g'], dh, 1.0, name="mix_res_bwd", tm=tm)
    dm = after_first(dm)
    dmerged = _mm(dm, s['w_out'], tb=True, name="mix_out_dx")
    grad_mm(s['merged'], dm, 'w_out', 'rows', "mix_out_dw")

    def merge_bwd(ga, gb, a, b, d):
        _, vjp = jax.vjp(lambda ga_, gb_, a_, b_: _sigmoid(ga_) * a_ + _sigmoid(gb_) * b_, ga, gb, a, b)
        dga, dgb, da, db = vjp(d)
        return jnp.concatenate([dga, dgb], axis=1), da, db

    dgab, dy_a, dy_b = _rows(
        merge_bwd, name="mix_merge_bwd", nrow=nrow,
        ins=[(proj, _rspec(tm, D_MODEL, GA0 // D_MODEL)), (proj, _rspec(tm, D_MODEL, GB0 // D_MODEL)),
             (s['y_a'], _rspec(tm, D_MODEL)), (s['y_b'], _rspec(tm, D_MODEL)), (dmerged, _rspec(tm, D_MODEL))],
        outs=[(_sds((t, 2 * D_MODEL), BF16), _rspec(tm, 2 * D_MODEL), False),
              (_sds((t, D_MODEL), BF16), _rspec(tm, D_MODEL), False),
              (_sds((t, D_MODEL), BF16), _rspec(tm, D_MODEL), False)])
    dya = _mm(dy_a, s['w_a'], tb=True, name="mix_a_dx")
    grad_mm(s['ya'], dy_a, 'w_branch_a', 'rows', "mix_a_dw")
    gw = SSD_D_INNER // SSD_GROUPS

    def gate_bwd(y, z, g, d):
        _, vjp = jax.vjp(_gatenorm, y, z, g)
        return vjp(d)

    dy_ssd, dz, dnorm = _rows(
        gate_bwd, name="ssd_gate_bwd", nrow=nrow, ncol=SSD_GROUPS,
        ins=[(s['y_ssd'], _rspec(tm, gw, 0, True)), (proj, _rspec(tm, gw, Z0 // gw, True)),
             (p['norm_g'], _bspec(gw, 0, True)), (dya, _rspec(tm, gw, 0, True))],
        outs=[(_sds((t, SSD_D_INNER)), _rspec(tm, gw, 0, True), False),
              (_sds((t, SSD_D_INNER), BF16), _rspec(tm, gw, 0, True), False),
              (_sds((1, SSD_D_INNER)), _bspec(gw, 0, True), True)])
    dxs, dbm, dcm, ddt4, ddtb, dalog, ddsk = _ssd_bwd(s['xc'], s['dt4'], p['dt_bias8'], p['a_log8'], p['d8'],
                                                      s['s_in'], dy_ssd, name="ssd_scan_bwd")
    dxbc, dconv_w, dconv_b = _conv_bwd(proj, XBC0, p['conv_w'], p['conv_b'], (dxs, dbm, dcm), name="ssd_conv_bwd")
    ddtr = ddt4[:, :, :8].transpose(1, 0, 2).reshape(t, SSD_HEADS)
    dy_bp = _perm(dy_b)
    dybin = _mm(dy_bp, s['w_b'], tb=True, name="mix_b_dx")
    grad_mm(s['ybin'], dy_bp, 'w_branch_b', 'rows', "mix_b_dw")

    def glu_bwd(a, b, d):
        _, vjp = jax.vjp(lambda a_, b_: a_ * _sigmoid(b_), a, b)
        da, db = vjp(d)
        return jnp.concatenate([da, db], axis=1)

    dvg = _rows(glu_bwd, name="s5_glu_act_bwd", nrow=nrow,
                ins=[(s['vg'], _rspec(tm, S5_WIDTH, 0)), (s['vg'], _rspec(tm, S5_WIDTH, 1)), (dybin, _rspec(tm, S5_WIDTH))],
                outs=[(_sds((t, 2 * S5_WIDTH), BF16), _rspec(tm, 2 * S5_WIDTH), False)])[0]
    dy5g = _mm(dvg, s['w_glu'], tb=True, name="s5_glu_dx")
    grad_mm(s['y5g'], dvg, 's5_w_glu', 'cols', "s5_glu_dw")

    def act_bwd(a, b, d, g):
        _, vjp = jax.vjp(lambda a_, b_, d_: _gelu(a_ + d_ * b_), a, b, d)
        return vjp(g)

    dy5, du5_direct, ds5d = _rows(
        act_bwd, name="s5_act_bwd", nrow=nrow,
        ins=[(s['y5'], _rspec(tm, S5_WIDTH)), (s['u5'], _rspec(tm, S5_WIDTH)), (p['s5_d'], _bspec(S5_WIDTH)),
             (dy5g, _rspec(tm, S5_WIDTH))],
        outs=[(_sds((t, S5_WIDTH), BF16), _rspec(tm, S5_WIDTH), False), (_sds((t, S5_WIDTH)), _rspec(tm, S5_WIDTH), False),
              (_sds((1, S5_WIDTH)), _bspec(S5_WIDTH), True)])
    du5, dbre, dbim, dcre, dcim, dlr, dli = _s5_bwd(dy5, du5_direct, s['u5'], s['sr'], s['si'], s['bre'], s['bim'],
                                                     s['cre'], s['cim'], s['lr'], s['li'], name="s5_scan_bwd")
    du_p = _mm(du5, p['w_u5'], tb=True, name="mix_in_s5_dx")
    dw_u5 = _mm(s['u_p'], du5, ta=True, name="mix_in_s5_dw", out_dtype=BF16)
    ext_b = lambda m: _blockdiag_extract(m, S5_GROUP, S5_STATE).transpose(0, 2, 1).reshape(S5_GROUPS, S5_STATE * S5_GROUP)
    dlam_re, dlam_im, dlog_step, db_re, db_im = _s5_params_bwd(
        p['lam_re'], p['lam_im'], p['log_step'], p['b_re'], p['b_im'],
        (dlr.reshape(S5_GROUPS, S5_STATE), dli.reshape(S5_GROUPS, S5_STATE), ext_b(dbre), ext_b(dbim)), name="s5_par_bwd")
    dc_re = _blockdiag_extract(dcre, S5_STATE, S5_GROUP).transpose(0, 2, 1)
    dc_im = _blockdiag_extract(dcim, S5_STATE, S5_GROUP).transpose(0, 2, 1)
    dproj = jnp.concatenate([dz, dxbc, dgab], axis=1)
    du_big = _mm(dproj, p['w_big'], tb=True, name="mix_in_dx", tn=D_MODEL, tk=896)
    du_dt = _mm(ddtr, p['w_dt'], tb=True, name="mix_in_dt_dx")
    dw_big = _mm(s['u'], dproj, ta=True, name="mix_in_dw", out_dtype=BF16)
    dw_dt = _mm(s['u'], ddtr, ta=True, name="mix_in_dt_dw", out_dtype=BF16)
    dh_in, dpre = _rms_bwd(s['h'], p['pre_g'], dh, [du_big, du_dt, _unperm(du_p)], name="mix_rms_bwd", tm=tm)
    dw_in = jnp.concatenate([dw_big[:, :GA0], dw_dt, dw_u5, dw_big[:, GA0:]], axis=1)
    bufs['w_in'] = dw_in.reshape(D_MODEL, N_DEV, -1).transpose(1, 0, 2)[:, None]
    grads = {
        'mix_pre_g': dpre, 'mix_post_g': dpost, 'ssd_conv_w': dconv_w, 'ssd_conv_b': dconv_b,
        'ssd_dt_bias': ddtb[:, 0, :8].reshape(-1), 'ssd_a_log': dalog[:, 0, :8].reshape(-1),
        'ssd_d': ddsk[:, 0, :8].reshape(-1), 'ssd_norm_g': dnorm,
        's5_lambda_re': dlam_re, 's5_lambda_im': dlam_im,
        's5_b_re': db_re.reshape(S5_GROUPS, S5_STATE, S5_GROUP), 's5_b_im': db_im.reshape(S5_GROUPS, S5_STATE, S5_GROUP),
        's5_c_re': dc_re, 's5_c_im': dc_im, 's5_log_step': dlog_step.reshape(-1), 's5_d': ds5d,
    }
    return dh_in, bufs, grads


HBM_SPEC = pl.BlockSpec(memory_space=pltpu.HBM)


def _place():
    return lax.axis_index("x"), lax.axis_index("y"), lax.axis_index("c")


GATHER_COLLECTIVE_ID = 1


def _all_gather(shards, *, name, on_sequencer=False):
    n = len(shards)

    def body(*refs):
        x_refs, out_refs = refs[:n], refs[n:2 * n]
        send_sems, recv_sems, local_sems = refs[2 * n:]
        x, y, c = _place()
        me, sibling = (x, y, c), (x, y, 1 - c)
        chips = [(1 - x, y), (x, 1 - y), (1 - x, 1 - y)]
        if on_sequencer:
            _handshake([sibling] + [(*chip, c) for chip in chips])

        def slot(o, px, py, pc):
            return out_refs[o].at[4 * px + 2 * py + pc]

        def copy(o, k, block, to, src=None):
            return pltpu.make_async_remote_copy(
                src_ref=slot(o, *block) if src is None else src, dst_ref=slot(o, *block),
                send_sem=send_sems.at[7 * o + k], recv_sem=recv_sems.at[7 * o + k], device_id=to, device_id_type=MESH)

        mine = [pltpu.make_async_copy(x_refs[o], slot(o, *me), local_sems.at[o]) for o in range(n)]
        for cp in mine:
            cp.start()
        first = []
        for j, chip in enumerate(chips):
            first += [copy(o, 1 + j, me, (*chip, c), src=x_refs[o]) for o in range(n)]
        first += [copy(o, 0, me, sibling, src=x_refs[o]) for o in range(n)]
        for cp in first:
            cp.start()
        passed = []
        for j, chip in enumerate(chips):
            for o in range(n):
                copy(o, 1 + j, (*chip, c), me).wait_recv()
                passed.append(copy(o, 4 + j, (*chip, c), sibling))
                passed[-1].start()
        for o in range(n):
            copy(o, 0, sibling, me).wait_recv()
        for j, chip in enumerate(chips):
            for o in range(n):
                copy(o, 4 + j, (*chip, 1 - c), me).wait_recv()
        for cp in first + passed:
            cp.wait_send()
        for cp in mine:
            cp.wait()

    out_shape = [jax.ShapeDtypeStruct((N_DEV,) + s.shape, s.dtype) for s in shards]
    sems = [pltpu.SemaphoreType.DMA((7 * n,)), pltpu.SemaphoreType.DMA((7 * n,)), pltpu.SemaphoreType.DMA((n,))]
    if on_sequencer:
        return _scall(body, name=name, out_type=out_shape, scratch_types=sems, collective_id=GATHER_COLLECTIVE_ID)(*shards)
    return _pcall(body, name=name, in_specs=[HBM_SPEC] * n, out_specs=[HBM_SPEC] * n, out_shape=out_shape,
                  scratch_shapes=sems)(*shards)


N_CHIPS = 4


SIBLING_COLLECTIVE_ID = 2
CHIPS_COLLECTIVE_ID = 3


def _handshake(peers):
    barrier = pltpu.get_barrier_semaphore()
    for peer in peers:
        pl.semaphore_signal(barrier, inc=1, device_id=peer, device_id_type=MESH)
    pl.semaphore_wait(barrier, len(peers))


def _exchange_sibling(grads, *, name):
    n = len(grads)

    def body(*refs):
        p_refs, q_refs = refs[:n], refs[n:2 * n]
        send_sems, recv_sems = refs[2 * n:]
        x, y, c = _place()
        _handshake([(x, y, 1 - c)])
        copies = [pltpu.make_async_remote_copy(
            src_ref=p_refs[o].at[k, 1 - c], dst_ref=q_refs[o].at[k], send_sem=send_sems.at[N_CHIPS * o + k],
            recv_sem=recv_sems.at[N_CHIPS * o + k], device_id=(x, y, 1 - c), device_id_type=MESH)
            for o in range(n) for k in range(N_CHIPS)]
        for cp in copies:
            cp.start()
        for cp in copies:
            cp.wait()

    return _scall(
        body, name=name, out_type=[jax.ShapeDtypeStruct((N_CHIPS,) + g.shape[2:], g.dtype) for g in grads],
        scratch_types=[pltpu.SemaphoreType.DMA((N_CHIPS * n,)), pltpu.SemaphoreType.DMA((N_CHIPS * n,))],
        collective_id=SIBLING_COLLECTIVE_ID,
    )(*grads)


def _pair_sum(own, got, *, name):
    _, _, r, l = own.shape
    tr = _tile(r, 512, 16)
    c = lax.axis_index("c").astype(jnp.int32).reshape(1)

    def body(c_ref, p_ref, q_ref, o_ref):
        o_ref[...] = (p_ref[...].astype(F32) + q_ref[...].astype(F32)).astype(o_ref.dtype)

    return _pcall(
        body, name=name,
        grid_spec=pltpu.PrefetchScalarGridSpec(
            num_scalar_prefetch=1, grid=(N_CHIPS, r // tr),
            in_specs=[pl.BlockSpec((None, None, tr, l), lambda k, i, cr: (k, cr[0], i, 0)),
                      pl.BlockSpec((None, tr, l), lambda k, i, cr: (k, i, 0))],
            out_specs=pl.BlockSpec((None, tr, l), lambda k, i, cr: (k, i, 0))),
        out_shape=jax.ShapeDtypeStruct((N_CHIPS, r, l), own.dtype),
        compiler_params=_params("parallel", "parallel"),
    )(c, own, got)


def _exchange_chips(parts, *, name):
    n = len(parts)

    def body(*refs):
        p_refs, g_refs = refs[:n], refs[n:2 * n]
        send_sems, recv_sems, local_sems = refs[2 * n:]
        x, y, c = _place()
        mine = 2 * x + y
        chips = [(1 - x, y), (x, 1 - y), (1 - x, 1 - y)]
        _handshake([(*chip, c) for chip in chips])
        own = [pltpu.make_async_copy(p_refs[o].at[mine], g_refs[o].at[mine], local_sems.at[o]) for o in range(n)]
        for cp in own:
            cp.start()
        copies = []
        for j, (px, py) in enumerate(chips):
            copies += [pltpu.make_async_remote_copy(
                src_ref=p_refs[o].at[2 * px + py], dst_ref=g_refs[o].at[mine], send_sem=send_sems.at[3 * o + j],
                recv_sem=recv_sems.at[3 * o + j], device_id=(px, py, c), device_id_type=MESH) for o in range(n)]
        for cp in copies:
            cp.start()
        for cp in copies:
            cp.wait()
        for cp in own:
            cp.wait()

    return _scall(
        body, name=name, out_type=[jax.ShapeDtypeStruct(p.shape, p.dtype) for p in parts],
        scratch_types=[pltpu.SemaphoreType.DMA((3 * n,)), pltpu.SemaphoreType.DMA((3 * n,)), pltpu.SemaphoreType.DMA((n,))],
        collective_id=CHIPS_COLLECTIVE_ID,
    )(*parts)


def _sum_slots(g, *, name):
    n, r, l = g.shape
    tr = _tile(r, 512, 16)

    def body(g_ref, o_ref):
        acc = g_ref[0].astype(F32)
        for k in range(1, n):
            acc = acc + g_ref[k].astype(F32)
        o_ref[...] = acc

    return _pcall(
        body, name=name, grid=(r // tr,), in_specs=[pl.BlockSpec((n, tr, l), lambda i: (0, i, 0))],
        out_specs=pl.BlockSpec((tr, l), lambda i: (i, 0)), out_shape=_sds((r, l)),
        compiler_params=_params("parallel"),
    )(g)


TRANSPOSED = ('ffn1_w_gate', 'ffn1_w_up', 'ffn2_w_gate', 'ffn2_w_up')
GATHER_CHUNKS = (('ffn1', ['ffn1_w_gate', 'ffn1_w_up']), ('ffn1_down', ['ffn1_w_down']),
                 ('mix_in', ['w_in', 'ssd_conv_w']), ('mix', ['w_branch_a', 's5_w_glu', 'w_branch_b', 'w_out']),
                 ('ffn2', ['ffn2_w_gate', 'ffn2_w_up']), ('ffn2_down', ['ffn2_w_down']))
LATE = ('ffn1_w_down', 'ffn2_w_down', 'w_branch_a', 's5_w_glu', 'w_branch_b', 'w_out')
SUBLAYERS = (('ffn1', ['ffn1_w_gate', 'ffn1_w_up', 'ffn1_w_down']),
             ('mix', ['w_in', 'ssd_conv_w', 'w_branch_a', 's5_w_glu', 'w_branch_b', 'w_out']),
             ('ffn2', ['ffn2_w_gate', 'ffn2_w_up', 'ffn2_w_down']))


def _gather_weights(w):
    layers, first = [], None
    for i in range(DEPTH):
        g = {}
        for tag, names in GATHER_CHUNKS:
            shards =[w[n][i:i + 1] if n == 'ssd_conv_w' else
                      (w[n][i:i + 1].transpose(0, 2, 1) if n in TRANSPOSED else w[n][i:i + 1]).astype(BF16) for n in names]
            if first is None:
                first = got = _all_gather(shards, name=f"gather_{tag}")
            else:
                shards, first = lax.optimization_barrier((shards, first))
                got = _all_gather(shards, name=f"gather_{tag}", on_sequencer=True)
            g.update(zip(names, got))
        layers.append(g)
    layers[0].update(zip(GATHER_CHUNKS[0][1], first))
    return layers


class _ReduceScatter:
    @staticmethod
    def sibling(tag, bufs):
        names = list(bufs)
        own = [bufs[n].reshape((N_CHIPS, 2) + bufs[n].shape[1:]) for n in names]
        return (tag, names), (own, _exchange_sibling(own, name=f"reduce_sibling_{tag}"))

    @staticmethod
    def chips(meta, arrays):
        (tag, names), (own, got) = meta, arrays
        flat = lambda a, lead: a.reshape(lead + (-1, a.shape[-1]))
        parts = [_pair_sum(flat(o, (N_CHIPS, 2)), flat(g, (N_CHIPS,)), name=f"reduce_pair_sum_{n}").reshape(g.shape)
                 for n, o, g in zip(names, own, got)]
        return names, _exchange_chips(parts, name=f"reduce_chips_{tag}")

    @staticmethod
    def done(names, slots):
        return dict(zip(names, slots))

    @staticmethod
    def small(grads):
        return _reduce_small(grads)


def _reduce_small(grads):
    flat = jnp.concatenate([g.astype(F32).reshape(-1) for g in grads.values()])
    pad = (-flat.shape[0]) % (8 * LANES)
    flat = jnp.concatenate([flat, jnp.zeros((pad,), F32)]).reshape(-1, LANES)
    gathered = _all_gather([flat], name="gather_small_grads", on_sequencer=True)[0]
    total = _sum_slots(gathered, name="sum_small_grads").reshape(-1)
    out, o = {}, 0
    for n, g in grads.items():
        out[n] = total[o:o + g.size].reshape(g.shape)
        o += g.size
    return out


def _adamw(w, g, m, v, *, name, slots=False):
    shape = w.shape
    if slots:
        lyr, rows, lanes = shape
        w2, m2, v2 = w, m, v
        tr = _tile(rows, 256, 16)
        nrt = rows // tr
        grid = (lyr, nrt)
        spec = pl.BlockSpec((None, tr, lanes), lambda l, i: (l, i, 0))
        g_specs = [pl.BlockSpec((N_CHIPS, None, tr, lanes),
                                lambda l, i, k=k: (0, 0, jnp.where(l == k, i, jnp.where(l > k, nrt - 1, 0)), 0))
                   for k in range(lyr)]
        g_args = list(g)
        out_shape = [_sds(shape)] * 4
    else:
        lanes = shape[-1] if (shape[-1] >= 128 or w.size % LANES) else LANES
        as2d = lambda a: a.reshape(-1, lanes)
        w2, m2, v2 = as2d(w), as2d(m), as2d(v)
        r = w2.shape[0]
        tr = _tile(r, 256, 8)
        grid = (1, r // tr)
        spec = pl.BlockSpec((tr, lanes), lambda l, i: (i, 0))
        g_specs, g_args = [spec], [as2d(g)]
        out_shape = [_sds((r, lanes))] * 4
    n_g = len(g_args)

    def body(w_ref, *rest):
        g_refs = rest[:n_g]
        m_ref, v_ref, go_ref, d_ref, mo_ref, vo_ref = rest[n_g:]
        if slots:
            gg = None
            for k, g_ref in enumerate(g_refs):
                tot = g_ref[0].astype(F32)
                for c in range(1, N_CHIPS):
                    tot = tot + g_ref[c].astype(F32)
                gg = tot if gg is None else jnp.where(pl.program_id(0) == k, tot, gg)
        else:
            gg = g_refs[0][...]
        go_ref[...] = gg
        mn = ADAM_B1 * m_ref[...] + (1.0 - ADAM_B1) * gg
        vn = ADAM_B2 * v_ref[...] + (1.0 - ADAM_B2) * (gg * gg)
        m_hat = mn / (1.0 - ADAM_B1 ** ADAM_STEP)
        v_hat = vn / (1.0 - ADAM_B2 ** ADAM_STEP)
        d_ref[...] = -ADAM_LR * (m_hat / (jnp.sqrt(v_hat) + ADAM_EPS) + ADAM_WD * w_ref[...])
        mo_ref[...] = mn
        vo_ref[...] = vn

    res = _pcall(
        body, name=name, grid=grid, in_specs=[spec] + g_specs + [spec, spec], out_specs=[spec] * 4,
        out_shape=out_shape, compiler_params=_params("arbitrary", "arbitrary"),
    )(w2, *g_args, m2, v2)
    return tuple(a.reshape(shape) for a in res)


def _sublayer_params(w, g, i, k, s5):
    row = lambda a: a.astype(F32).reshape(1, -1)
    if k != 'mix':
        return dict(layer=i, pre_g=row(w[f'{k}_pre_g'][i]), post_g=row(w[f'{k}_post_g'][i]),
                    w_gate=g[f'{k}_w_gate'], w_up=g[f'{k}_w_up'], w_down=g[f'{k}_w_down'])
    head8 = lambda a: jnp.broadcast_to(
        jnp.pad(a.astype(F32).reshape(SSD_GROUPS, 1, 8), ((0, 0), (0, 0), (0, PAD_HEADS - 8))), (SSD_GROUPS, 8, PAD_HEADS))
    by_rows = lambda n: g[n].reshape(-1, g[n].shape[-1])
    by_cols = lambda n: g[n][:, 0].transpose(1, 0, 2).reshape(g[n].shape[2], -1)
    w_in = by_cols('w_in')
    s = np.cumsum([SSD_D_INNER, SSD_CONV_DIM, SSD_HEADS, S5_WIDTH, D_MODEL])
    return dict(
        layer=i, s5=s5, pre_g=row(w['mix_pre_g'][i]), post_g=row(w['mix_post_g'][i]),
        w_big=jnp.concatenate([w_in[:, :s[1]], w_in[:, s[3]:]], axis=1), w_dt=w_in[:, s[1]:s[2]], w_u5=w_in[:, s[2]:s[3]],
        conv_w=by_cols('ssd_conv_w'), conv_b=row(w['ssd_conv_b'][i]),
        dt_bias8=head8(w['ssd_dt_bias'][i]), a_log8=head8(w['ssd_a_log'][i]), d8=head8(w['ssd_d'][i]),
        norm_g=row(w['ssd_norm_g'][i]), late={n: g[n] for n in SUBLAYERS[1][1] if n in LATE},
        lam_re=w['s5_lambda_re'][i], lam_im=w['s5_lambda_im'][i], log_step=w['s5_log_step'][i].reshape(S5_GROUPS, 1),
        b_re=w['s5_b_re'][i].reshape(S5_GROUPS, -1), b_im=w['s5_b_im'][i].reshape(S5_GROUPS, -1),
        c_re=w['s5_c_re'][i], c_im=w['s5_c_im'][i], s5_d=row(w['s5_d'][i]),
    )


def _loss_head(h, target, *, tm):
    t, d = h.shape

    def fn(y, tgt):
        err = y - tgt
        return err * (1.0 / d), jnp.sum(0.5 * jnp.sum(err * err, axis=-1, keepdims=True) * (1.0 / d), axis=0, keepdims=True)

    dy, loss = _rows(fn, name="loss_head", nrow=t // tm,
                     ins=[(h, _rspec(tm, d)), (target, _rspec(tm, d))],
                     outs=[(_sds((t, d)), _rspec(tm, d), False), (_sds((1, 128)), _bspec(128), True)])
    return dy, loss[0, 0]


def _forward_backward(h, target, w, g, rs):
    t = h.shape[0]
    tm = _tile(t, 512, 8)
    s5 = _s5_prepare(w)
    layers, saved = [], []
    for i in range(DEPTH):
        gi, ps, ss = dict(g[i]), [], []
        for tag, names in SUBLAYERS:
            early = [n for n in names if n not in LATE]
            tied, h = lax.optimization_barrier(([gi[n] for n in early], h))
            gi.update(zip(early, tied))
            p = _sublayer_params(w, gi, i, tag, s5[i])
            h, s = _mixer_fwd(h, p, tm) if tag == 'mix' else _ffn_fwd(h, p, tag, tm)
            ps.append(p)
            ss.append(s)
        layers.append(ps)
        saved.append(ss)
    dh, loss = _loss_head(h, target, tm=tm)
    reduced, small = [{} for _ in range(DEPTH)], [{} for _ in range(DEPTH)]
    in_sibling, in_chips = None, None

    def start_chips(x):
        nonlocal in_sibling, in_chips
        if in_sibling is not None:
            layer, meta, arrays = in_sibling
            arrays, x = lax.optimization_barrier((arrays, x))
            in_sibling, in_chips = None, (layer,) + tuple(rs.chips(meta, arrays))
        return x

    def finish_chips(x):
        nonlocal in_chips
        if in_chips is not None:
            layer, names, slots = in_chips
            slots, x = lax.optimization_barrier((slots, x))
            reduced[layer].update(rs.done(names, slots))
            in_chips = None
        return x

    for i in reversed(range(DEPTH)):
        for k in reversed(range(len(SUBLAYERS))):
            tag = SUBLAYERS[k][0]
            if tag == 'mix':
                dh, bufs, grads = _mixer_bwd(dh, layers[i][k], saved[i][k], tm, start_chips)
            else:
                dh, bufs, grads = _ffn_bwd(dh, layers[i][k], saved[i][k], tag, tm, start_chips)
            small[i].update(grads)
            dh = finish_chips(dh)
            in_sibling = (i,) + tuple(rs.sibling(tag, bufs))
            if tag == 'mix' and i + 1 < DEPTH:
                small[i + 1], dh = lax.optimization_barrier((small[i + 1], dh))
        if i == 0:
            small[i]['loss'] = loss.reshape(1)
        small[i] = rs.small(small[i])
    loss = small[0].pop('loss')[0]
    dh = finish_chips(start_chips(dh))
    shapes = {n: (w[n].shape[:-1] + (SSD_CONV_DIM,) if n == 'ssd_conv_w' else w[n].shape) for n in SMALL_ORDER}
    stacked = {n: jnp.stack([small[i][n].reshape(shapes[n][1:]) for i in range(DEPTH)]) for n in SMALL_ORDER}
    return loss, dh, reduced, stacked


def kernel(*args):
    n_w = len(WEIGHTS)
    x, target = args[0], args[1 + n_w]
    w = dict(zip(WEIGHTS, args[1:1 + n_w]))
    m = dict(zip(WEIGHTS, args[2 + n_w:2 + 2 * n_w]))
    v = dict(zip(WEIGHTS, args[2 + 2 * n_w:2 + 3 * n_w]))
    t = x.shape[1]

    g = _gather_weights(w)
    loss, dx, slots, small = _forward_backward(x.reshape(t, D_MODEL), target.reshape(t, D_MODEL), w, g, _ReduceScatter)
    me = 4 * lax.axis_index("x") + 2 * lax.axis_index("y") + lax.axis_index("c")
    cols = w['ssd_conv_w'].shape[-1]
    small['ssd_conv_w'] = lax.dynamic_slice_in_dim(small['ssd_conv_w'], me * cols, cols, axis=2)

    grad, delta, new_m, new_v = {}, {}, {}, {}
    for n in WEIGHTS:
        sharded = n in slots[0]
        view = (lambda a: a.transpose(0, 2, 1)) if n in TRANSPOSED else (lambda a: a)
        res = _adamw(view(w[n]), [slots[i][n] for i in range(DEPTH)] if sharded else small[n], view(m[n]), view(v[n]),
                     name=f"adamw_{n}", slots=sharded)
        grad[n], delta[n], new_m[n], new_v[n] = (view(a) for a in res)
    return (loss, dx.reshape(x.shape), *[grad[n] for n in WEIGHTS], *[delta[n] for n in WEIGHTS],
            *[new_m[n] for n in WEIGHTS], *[new_v[n] for n in WEIGHTS])
```

```python
import math

import numpy as np
import jax
import jax.numpy as jnp
from jax import lax
from jax.experimental import pallas as pl
from jax.experimental.pallas import tpu as pltpu
from jax.experimental.pallas import tpu_sc as plsc

F32 = jnp.float32
BF16 = jnp.bfloat16
MESH = pl.DeviceIdType.MESH
HIGHEST = lax.Precision.HIGHEST

D_MODEL = 1024
DEPTH = 2
FFN_HIDDEN = 2816
SSD_D_INNER = 2048
SSD_HEADS = 32
SSD_HEAD_DIM = 64
SSD_GROUPS = 4
SSD_STATE = 128
SSD_CHUNK = 128
SSD_CONV_DIM = 3072
SSD_CONV_WIDTH = 4
S5_WIDTH = 1024
S5_GROUP = 16
S5_GROUPS = 64
S5_STATE = 64
S5_MAX_REAL = -1e-4
S5_BLOCKS = 8
RMS_EPS = 1e-6
N_DEV = 8
LANES = 1024

ADAM_LR = 0.001
ADAM_B1 = 0.9
ADAM_B2 = 0.999
ADAM_EPS = 1e-08
ADAM_WD = 0.01
ADAM_STEP = 10

VMEM_LIMIT_BYTES = 48 * 1024 * 1024

WEIGHTS = ['ffn1_pre_g', 'ffn1_post_g', 'ffn1_w_gate', 'ffn1_w_up', 'ffn1_w_down', 'mix_pre_g', 'mix_post_g',
           'w_in', 'ssd_conv_w', 'ssd_conv_b', 'ssd_dt_bias', 'ssd_a_log', 'ssd_d', 'ssd_norm_g', 'w_branch_a',
           's5_lambda_re', 's5_lambda_im', 's5_b_re', 's5_b_im', 's5_c_re', 's5_c_im', 's5_log_step', 's5_d',
           's5_w_glu', 'w_branch_b', 'w_out', 'ffn2_pre_g', 'ffn2_post_g', 'ffn2_w_gate', 'ffn2_w_up',
           'ffn2_w_down']
SHARDED = {'ffn1_w_gate': 2, 'ffn1_w_up': 2, 'ffn1_w_down': 1, 'w_in': 2, 'ssd_conv_w': 2, 'w_branch_a': 1,
           's5_w_glu': 2, 'w_branch_b': 1, 'w_out': 1, 'ffn2_w_gate': 2, 'ffn2_w_up': 2, 'ffn2_w_down': 1}
SHARDED_ORDER = [n for n in WEIGHTS if n in SHARDED]
SMALL_ORDER = [n for n in WEIGHTS if n not in SHARDED or n == 'ssd_conv_w']


def _pcall(body, **kw):
    return pl.pallas_call(body, **kw)


def _scall(body, *, name, out_type, scratch_types, collective_id):
    return pl.kernel(body, out_type=out_type, mesh=plsc.ScalarSubcoreMesh(axis_name="sequencer", num_cores=1),
                     scratch_types=scratch_types, name=name,
                     compiler_params=pltpu.CompilerParams(collective_id=collective_id))


def _params(*sem):
    return pltpu.CompilerParams(dimension_semantics=sem, vmem_limit_bytes=VMEM_LIMIT_BYTES)


def _tile(n, pref, align=128):
    if n <= pref:
        return n
    t = (pref // align) * align
    while t >= align:
        if n % t == 0:
            return t
        t -= align
    return n


def _rms(x, g):
    return x * lax.rsqrt(jnp.mean(x * x, axis=-1, keepdims=True) + RMS_EPS) * g


def _sigmoid(x):
    return 1.0 / (1.0 + jnp.exp(-x))


def _silu(x):
    return x * _sigmoid(x)


def _gelu(x):
    return 0.5 * x * (1.0 + jnp.tanh(math.sqrt(2.0 / math.pi) * (x + 0.044715 * (x * x * x))))


def _softplus(x):
    return jnp.maximum(x, 0.0) + jnp.log(1.0 + jnp.exp(-jnp.abs(x)))


def _dot(a, b, dims):
    return lax.dot_general(a.astype(BF16), b.astype(BF16), (dims, ((), ())), preferred_element_type=F32)


_NN = ((1,), (0,))
_NT = ((1,), (1,))
_TN = ((0,), (0,))


@jax.custom_vjp
def _bdot_nn(a, b):
    return _dot(a, b, _NN)


_bdot_nn.defvjp(lambda a, b: (_dot(a, b, _NN), (a, b)),
                lambda r, g: (_dot(g, r[1], _NT), _dot(r[0], g, _TN)))


@jax.custom_vjp
def _bdot_nt(a, b):
    return _dot(a, b, _NT)


_bdot_nt.defvjp(lambda a, b: (_dot(a, b, _NT), (a, b)),
                lambda r, g: (_dot(g, r[1], _NN), _dot(g, r[0], _TN)))


@jax.custom_vjp
def _bdot_tn(a, b):
    return _dot(a, b, _TN)


_bdot_tn.defvjp(lambda a, b: (_dot(a, b, _TN), (a, b)),
                lambda r, g: (_dot(r[1], g, _NT), _dot(r[0], g, _NN)))


def _fdot(a, b, dims=_NN):
    return lax.dot_general(a, b, (dims, ((), ())), precision=HIGHEST, preferred_element_type=F32)


def _sel3(x, sel, dims, x_first):
    p1 = x.astype(BF16)
    r1 = x - p1.astype(F32)
    p2 = r1.astype(BF16)
    p3 = (r1 - p2.astype(F32)).astype(BF16)
    sel = sel.astype(BF16)
    out = None
    for piece in (p1, p2, p3):
        d = lax.dot_general(*((piece, sel) if x_first else (sel, piece)), (dims, ((), ())), preferred_element_type=F32)
        out = d if out is None else out + d
    return out


@jax.custom_vjp
def _sel_right(x, sel):
    return _sel3(x, sel, _NN, True)


_sel_right.defvjp(lambda x, sel: (_sel3(x, sel, _NN, True), sel),
                  lambda sel, g: (_sel3(g, sel, _NT, True), jnp.zeros_like(sel)))


@jax.custom_vjp
def _sel_left(sel, x):
    return _sel3(x, sel, _NN, False)


_sel_left.defvjp(lambda sel, x: (_sel3(x, sel, _NN, False), sel),
                 lambda sel, g: (jnp.zeros_like(sel), _sel3(g, sel, _TN, False)))


@jax.custom_vjp
def _sel_left_nt(sel, x):
    return _sel3(x, sel, _NT, False)


_sel_left_nt.defvjp(lambda sel, x: (_sel3(x, sel, _NT, False), sel),
                    lambda sel, g: (jnp.zeros_like(sel), _sel3(g, sel, _TN, True)))


def _mm(a, b, *, name, ta=False, tb=False, out_dtype=F32, tm=2048, tn=512, tk=2048, col_shards=False):
    m, k = (a.shape[1], a.shape[0]) if ta else a.shape
    n = b.shape[0] if tb else b.shape[1]
    assert k == (b.shape[1] if tb else b.shape[0]), (a.shape, b.shape, ta, tb)
    if col_shards:
        tn = n // N_DEV
    tm, tn, tk = _tile(m, tm), _tile(n, tn), _tile(k, tk)
    nk = k // tk
    a_spec = pl.BlockSpec((tk, tm), lambda i, j, kk: (kk, i)) if ta else pl.BlockSpec((tm, tk), lambda i, j, kk: (i, kk))
    b_spec = pl.BlockSpec((tn, tk), lambda i, j, kk: (j, kk)) if tb else pl.BlockSpec((tk, tn), lambda i, j, kk: (kk, j))
    dims = ((0 if ta else 1,), (1 if tb else 0,))
    out_spec = pl.BlockSpec((tm, tn), lambda i, j, kk: (i, j))
    out_shape = jax.ShapeDtypeStruct((m, n), out_dtype)
    if col_shards:
        out_shape = jax.ShapeDtypeStruct((N_DEV, 1, m, n // N_DEV), out_dtype)
        out_spec = pl.BlockSpec((None, None, tm, tn), lambda i, j, kk: (j, 0, i, 0))

    def body(a_ref, b_ref, o_ref, acc_ref):
        kk = pl.program_id(2)

        @pl.when(kk == 0)
        def _():
            acc_ref[...] = jnp.zeros_like(acc_ref)

        acc_ref[...] += _dot(a_ref[...], b_ref[...], dims)

        @pl.when(kk == nk - 1)
        def _():
            o_ref[...] = acc_ref[...].astype(o_ref.dtype)

    return _pcall(
        body, name=name, grid=(m // tm, n // tn, nk),
        in_specs=[a_spec, b_spec], out_specs=out_spec, out_shape=out_shape,
        scratch_shapes=[pltpu.VMEM((tm, tn), F32)],
        compiler_params=_params("parallel", "parallel", "arbitrary"),
    )(a, b)


def _rspec(tm, w, cb=0, percol=False):
    return pl.BlockSpec((tm, w), (lambda j, i: (i, cb + j)) if percol else (lambda j, i: (i, cb)))


def _bspec(w, cb=0, percol=False, rows=1):
    return pl.BlockSpec((rows, w), (lambda j, i: (0, cb + j)) if percol else (lambda j, i: (0, cb)))


def _rows(fn, *, name, nrow, ncol=1, ins, outs):
    n_in = len(ins)
    accs = [o[2] for o in outs]

    def body(*refs):
        vals = fn(*[r[...] for r in refs[:n_in]])
        if not isinstance(vals, (tuple, list)):
            vals = (vals,)
        i = pl.program_id(1)
        for ref, val, acc in zip(refs[n_in:], vals, accs):
            if acc:
                @pl.when(i == 0)
                def _(ref=ref):
                    ref[...] = jnp.zeros_like(ref)

                ref[...] += jnp.broadcast_to(val, ref.shape).astype(ref.dtype)
            else:
                ref[...] = val.astype(ref.dtype)

    res = _pcall(
        body, name=name, grid=(ncol, nrow),
        in_specs=[s for _, s in ins], out_specs=[o[1] for o in outs], out_shape=[o[0] for o in outs],
        compiler_params=_params("parallel", "arbitrary"),
    )(*[a for a, _ in ins])
    return res


def _sds(shape, dtype=F32):
    return jax.ShapeDtypeStruct(shape, dtype)


def _rms_fwd(h, g, *, name, tm):
    t, d = h.shape
    return _rows(lambda x, gg: _rms(x, gg), name=name, nrow=t // tm,
                 ins=[(h, _rspec(tm, d)), (g, _bspec(d))],
                 outs=[(_sds((t, d), BF16), _rspec(tm, d), False)])[0]


def _resid_fwd(h, f, g, scale, *, name, tm):
    t, d = h.shape
    return _rows(lambda x, ff, gg: x + scale * _rms(ff, gg), name=name, nrow=t // tm,
                 ins=[(h, _rspec(tm, d)), (f, _rspec(tm, d)), (g, _bspec(d))],
                 outs=[(_sds((t, d)), _rspec(tm, d), False)])[0]


def _resid_bwd(f, g, dh, scale, *, name, tm):
    t, d = f.shape

    def fn(ff, gg, dd):
        _, vjp = jax.vjp(lambda a, b: scale * _rms(a, b), ff, gg)
        return vjp(dd)

    return _rows(fn, name=name, nrow=t // tm,
                 ins=[(f, _rspec(tm, d)), (g, _bspec(d)), (dh, _rspec(tm, d))],
                 outs=[(_sds((t, d), BF16), _rspec(tm, d), False), (_sds((1, d)), _bspec(d), True)])


def _rms_bwd(h, g, dh, dxns, *, name, tm):
    t, d = h.shape

    def fn(x, gg, dd, *dx):
        _, vjp = jax.vjp(_rms, x, gg)
        tot = dx[0]
        for more in dx[1:]:
            tot = tot + more
        dxx, dg = vjp(tot)
        return dd + dxx, dg

    return _rows(fn, name=name, nrow=t // tm,
                 ins=[(h, _rspec(tm, d)), (g, _bspec(d)), (dh, _rspec(tm, d))] + [(x, _rspec(tm, d)) for x in dxns],
                 outs=[(_sds((t, d)), _rspec(tm, d), False), (_sds((1, d)), _bspec(d), True)])


FFN_BLOCKS = 4
NB = FFN_HIDDEN // FFN_BLOCKS
MM_ROWS = 2048


def _ffn_up(xn, wg, wu, *, name):
    t = xn.shape[0]
    tm = _tile(t, MM_ROWS // 2)
    wspec = pl.BlockSpec((None, None, NB, D_MODEL), lambda i, j: (j, 0, 0, 0))

    def body(x_ref, g_ref, u_ref, ab_ref, hh_ref):
        x = x_ref[...]
        a, b = _dot(x, g_ref[...], _NT), _dot(x, u_ref[...], _NT)
        ab_ref[0] = a.astype(ab_ref.dtype)
        ab_ref[1] = b.astype(ab_ref.dtype)
        hh_ref[...] = (_silu(a) * b).astype(hh_ref.dtype)

    return _pcall(
        body, name=name, grid=(t // tm, FFN_BLOCKS),
        in_specs=[pl.BlockSpec((tm, D_MODEL), lambda i, j: (i, 0)), wspec, wspec],
        out_specs=[pl.BlockSpec((None, 2, tm, NB), lambda i, j: (j, 0, i, 0)),
                   pl.BlockSpec((None, tm, NB), lambda i, j: (j, i, 0))],
        out_shape=[_sds((FFN_BLOCKS, 2, t, NB), BF16), _sds((FFN_BLOCKS, t, NB), BF16)],
        compiler_params=_params("parallel", "parallel"),
    )(xn, wg, wu)


def _ffn_down(hh, wd, *, name):
    t = hh.shape[1]
    tm = _tile(t, 512)

    def body(h_ref, w_ref, o_ref):
        acc = _dot(h_ref[0], w_ref[0, 0], _NN)
        for k in range(1, FFN_BLOCKS):
            acc = acc + _dot(h_ref[k], w_ref[k, 0], _NN)
        o_ref[...] = acc

    return _pcall(
        body, name=name, grid=(t // tm,),
        in_specs=[pl.BlockSpec((FFN_BLOCKS, tm, NB), lambda i: (0, i, 0)),
                  pl.BlockSpec((FFN_BLOCKS, 1, NB, D_MODEL), lambda i: (0, 0, 0, 0))],
        out_specs=pl.BlockSpec((tm, D_MODEL), lambda i: (i, 0)), out_shape=_sds((t, D_MODEL)),
        compiler_params=_params("parallel"),
    )(hh, wd)


def _ffn_down_dx(df, wd, ab, *, name):
    t = df.shape[0]
    tm = _tile(t, MM_ROWS // 2)

    def body(d_ref, w_ref, ab_ref, o_ref):
        dhh = _dot(d_ref[...], w_ref[...], _NT)
        _, vjp = jax.vjp(lambda a, b: _silu(a) * b, ab_ref[0].astype(F32), ab_ref[1].astype(F32))
        da, db = vjp(dhh)
        o_ref[0] = da.astype(o_ref.dtype)
        o_ref[1] = db.astype(o_ref.dtype)

    blk = pl.BlockSpec((None, 2, tm, NB), lambda i, j: (j, 0, i, 0))
    return _pcall(
        body, name=name, grid=(t // tm, FFN_BLOCKS),
        in_specs=[pl.BlockSpec((tm, D_MODEL), lambda i, j: (i, 0)),
                  pl.BlockSpec((None, None, NB, D_MODEL), lambda i, j: (j, 0, 0, 0)), blk],
        out_specs=blk, out_shape=_sds((FFN_BLOCKS, 2, t, NB), BF16), compiler_params=_params("parallel", "parallel"),
    )(df, wd, ab)


def _ffn_down_dw(hh, df, *, name, tn=512):
    t = df.shape[0]
    tk = _tile(t, 2048)
    nk = t // tk

    def body(h_ref, d_ref, o_ref, acc_ref):
        kk = pl.program_id(2)

        @pl.when(kk == 0)
        def _():
            acc_ref[...] = jnp.zeros_like(acc_ref)

        acc_ref[...] += _dot(h_ref[...], d_ref[...], _TN)

        @pl.when(kk == nk - 1)
        def _():
            o_ref[...] = acc_ref[...].astype(o_ref.dtype)

    return _pcall(
        body, name=name, grid=(FFN_BLOCKS, D_MODEL // tn, nk),
        in_specs=[pl.BlockSpec((None, tk, NB), lambda j, n, kk: (j, kk, 0)),
                  pl.BlockSpec((tk, tn), lambda j, n, kk: (kk, n))],
        out_specs=pl.BlockSpec((None, None, NB, tn), lambda j, n, kk: (j, 0, 0, n)),
        out_shape=_sds((FFN_BLOCKS, 1, NB, D_MODEL), BF16),
        scratch_shapes=[pltpu.VMEM((NB, tn), F32)],
        compiler_params=_params("parallel", "parallel", "arbitrary"),
    )(hh, df)


def _ffn_up_dx(dab, wg, wu, *, name):
    t = dab.shape[2]
    tm = _tile(t, MM_ROWS // 2)
    wspec = pl.BlockSpec((None, None, NB, D_MODEL), lambda i, j: (j, 0, 0, 0))

    def body(d_ref, g_ref, u_ref, o_ref):
        @pl.when(pl.program_id(1) == 0)
        def _():
            o_ref[...] = jnp.zeros_like(o_ref)

        o_ref[...] += _dot(d_ref[0], g_ref[...], _NN) + _dot(d_ref[1], u_ref[...], _NN)

    return _pcall(
        body, name=name, grid=(t // tm, FFN_BLOCKS),
        in_specs=[pl.BlockSpec((None, 2, tm, NB), lambda i, j: (j, 0, i, 0)), wspec, wspec],
        out_specs=pl.BlockSpec((tm, D_MODEL), lambda i, j: (i, 0)), out_shape=_sds((t, D_MODEL)),
        compiler_params=_params("parallel", "arbitrary"),
    )(dab, wg, wu)


def _ffn_up_dw(xn, dab, *, name):
    t = xn.shape[0]

    def body(x_ref, d_ref, og_ref, ou_ref):
        x = x_ref[...]
        og_ref[...] = _dot(d_ref[0], x, _TN).astype(og_ref.dtype)
        ou_ref[...] = _dot(d_ref[1], x, _TN).astype(ou_ref.dtype)

    out = pl.BlockSpec((None, None, NB, D_MODEL), lambda j: (j, 0, 0, 0))
    return _pcall(
        body, name=name, grid=(FFN_BLOCKS,),
        in_specs=[pl.BlockSpec((t, D_MODEL), lambda j: (0, 0)), pl.BlockSpec((None, 2, t, NB), lambda j: (j, 0, 0, 0))],
        out_specs=[out, out], out_shape=[_sds((FFN_BLOCKS, 1, NB, D_MODEL), BF16)] * 2,
        compiler_params=_params("parallel"),
    )(xn, dab)


def _paired(a):
    return a.reshape(FFN_BLOCKS, 1, NB, D_MODEL)


def _ffn_fwd(h, p, tag, tm):
    xn = _rms_fwd(h, p['pre_g'], name=f"{tag}_rms", tm=tm)
    ab, hh = _ffn_up(xn, _paired(p['w_gate']), _paired(p['w_up']), name=f"{tag}_up")
    w_down, hh = lax.optimization_barrier((p['w_down'], hh))
    f = _ffn_down(hh, _paired(w_down), name=f"{tag}_down")
    out = _resid_fwd(h, f, p['post_g'], 0.5, name=f"{tag}_res", tm=tm)
    return out, (h, xn, ab, hh, f)


def _ffn_bwd(dh, p, saved, tag, tm, after_first):
    h, xn, ab, hh, f = saved
    df, dpost = _resid_bwd(f, p['post_g'], dh, 0.5, name=f"{tag}_res_bwd", tm=tm)
    df = after_first(df)
    dab = _ffn_down_dx(df, _paired(p['w_down']), ab, name=f"{tag}_down_dx")
    bufs = {f'{tag}_w_down': _ffn_down_dw(hh, df, name=f"{tag}_down_dw")}
    dxn = _ffn_up_dx(dab, _paired(p['w_gate']), _paired(p['w_up']), name=f"{tag}_up_dx")
    bufs[f'{tag}_w_gate'], bufs[f'{tag}_w_up'] = _ffn_up_dw(xn, dab, name=f"{tag}_up_dw")
    bufs = {n: a.reshape(N_DEV, 1, FFN_HIDDEN // N_DEV, D_MODEL) for n, a in bufs.items()}
    dh_in, dpre = _rms_bwd(h, p['pre_g'], dh, [dxn], name=f"{tag}_rms_bwd", tm=tm)
    return dh_in, bufs, {f'{tag}_pre_g': dpre, f'{tag}_post_g': dpost}


CONV_COLS = 256


def _shift_down(x, s):
    rows = lax.broadcasted_iota(jnp.int32, x.shape, 0)
    return jnp.where(rows >= s, pltpu.roll(x, s, axis=0), 0.0)


def _shift_up(x, s):
    t = x.shape[0]
    rows = lax.broadcasted_iota(jnp.int32, x.shape, 0)
    return jnp.where(rows < t - s, pltpu.roll(x, t - s, axis=0), 0.0)


def _conv_fwd(proj, col0, w, b, *, name):
    t = proj.shape[0]
    c = w.shape[1]
    cb0 = col0 // CONV_COLS

    def body(x_ref, w_ref, b_ref, o_ref):
        x = x_ref[...]
        acc = x * w_ref[3:4, :] + b_ref[...]
        for k in range(SSD_CONV_WIDTH - 1):
            acc = acc + _shift_down(x, SSD_CONV_WIDTH - 1 - k) * w_ref[k:k + 1, :]
        o_ref[...] = _silu(acc)

    return _pcall(
        body, name=name, grid=(c // CONV_COLS,),
        in_specs=[pl.BlockSpec((t, CONV_COLS), lambda j: (0, cb0 + j)),
                  pl.BlockSpec((SSD_CONV_WIDTH, CONV_COLS), lambda j: (0, j)),
                  pl.BlockSpec((1, CONV_COLS), lambda j: (0, j))],
        out_specs=pl.BlockSpec((t, CONV_COLS), lambda j: (0, j)),
        out_shape=_sds((t, c)), compiler_params=_params("parallel"),
    )(proj, w, b)


def _conv_bwd(proj, col0, w, b, douts, *, name):
    t = proj.shape[0]
    c = w.shape[1]
    cb0 = col0 // CONV_COLS
    first = np.cumsum([0] + [d.shape[1] // CONV_COLS for d in douts])

    def body(x_ref, w_ref, b_ref, *rest):
        d_refs, (dx_ref, dw_ref, db_ref) = rest[:len(douts)], rest[len(douts):]
        j = pl.program_id(0)
        dout = d_refs[-1][...]
        for k in range(len(douts) - 2, -1, -1):
            dout = jnp.where(j < int(first[k + 1]), d_refs[k][...], dout)
        x = x_ref[...]
        shifted = [_shift_down(x, SSD_CONV_WIDTH - 1 - k) for k in range(SSD_CONV_WIDTH - 1)] + [x]
        pre = b_ref[...] + shifted[3] * w_ref[3:4, :]
        for k in range(SSD_CONV_WIDTH - 1):
            pre = pre + shifted[k] * w_ref[k:k + 1, :]
        sg = _sigmoid(pre)
        dpre = dout * (sg * (1.0 + pre * (1.0 - sg)))
        dx = dpre * w_ref[3:4, :]
        for k in range(SSD_CONV_WIDTH - 1):
            dx = dx + _shift_up(dpre, SSD_CONV_WIDTH - 1 - k) * w_ref[k:k + 1, :]
        dx_ref[...] = dx.astype(dx_ref.dtype)
        for k in range(SSD_CONV_WIDTH):
            dw_ref[k:k + 1, :] = jnp.sum(dpre * shifted[k], axis=0, keepdims=True)
        db_ref[...] = jnp.sum(dpre, axis=0, keepdims=True)

    return _pcall(
        body, name=name, grid=(c // CONV_COLS,),
        in_specs=[pl.BlockSpec((t, CONV_COLS), lambda j: (0, cb0 + j)),
                  pl.BlockSpec((SSD_CONV_WIDTH, CONV_COLS), lambda j: (0, j)),
                  pl.BlockSpec((1, CONV_COLS), lambda j: (0, j))] +
                 [pl.BlockSpec((t, CONV_COLS), lambda j, lo=int(first[k]), hi=int(first[k + 1]): (0, jnp.clip(j, lo, hi - 1) - lo))
                  for k in range(len(douts))],
        out_specs=[pl.BlockSpec((t, CONV_COLS), lambda j: (0, j)),
                   pl.BlockSpec((SSD_CONV_WIDTH, CONV_COLS), lambda j: (0, j)),
                   pl.BlockSpec((1, CONV_COLS), lambda j: (0, j))],
        out_shape=[_sds((t, c), BF16), _sds((SSD_CONV_WIDTH, c)), _sds((1, c))],
        compiler_params=_params("arbitrary"),
    )(proj, w, b, *douts)


HALF = 256
HEADS_PER_HALF = 4
PAD_HEADS = 128


def _head_expanders():
    k = lax.broadcasted_iota(jnp.int32, (PAD_HEADS, HALF), 0)
    j = lax.broadcasted_iota(jnp.int32, (PAD_HEADS, HALF), 1)
    kt = lax.broadcasted_iota(jnp.int32, (HALF, PAD_HEADS), 1)
    jt = lax.broadcasted_iota(jnp.int32, (HALF, PAD_HEADS), 0)
    es, ets = [], []
    for half in range(2):
        es.append(jnp.where(k == j // SSD_HEAD_DIM + half * HEADS_PER_HALF, 1.0, 0.0).astype(F32))
        ets.append(jnp.where(kt == jt // SSD_HEAD_DIM + half * HEADS_PER_HALF, 1.0, 0.0).astype(F32))
    return es, ets


def _ssd_chunk(x_lo, x_hi, bm, cm, dtr, dtb8, alog8, dsk8, s_lo, s_hi):
    q = x_lo.shape[0]
    es, ets = _head_expanders()
    rowmean = lambda v: jnp.sum(v, axis=0, keepdims=True) * 0.125
    dt = _softplus(dtr + rowmean(dtb8))
    a = -jnp.exp(rowmean(alog8))
    adt = a * dt
    adt_tot8 = jnp.broadcast_to(jnp.sum(adt, axis=0, keepdims=True), (8, PAD_HEADS))
    ll = lax.broadcasted_iota(jnp.int32, (q, q), 0)
    ss = lax.broadcasted_iota(jnp.int32, (q, q), 1)
    ltri = jnp.where(ll >= ss, 1.0, 0.0).astype(F32)
    lane = lax.broadcasted_iota(jnp.int32, (1, HALF), 1)
    cb = _bdot_nt(cm, bm)
    outs = []
    for half, (x, s_in) in enumerate(((x_lo, s_lo), (x_hi, s_hi))):
        e, et = es[half], ets[half]
        dtf = _sel_right(dt, e)
        af = rowmean(_sel_right(jnp.broadcast_to(a, (8, PAD_HEADS)), e)) * dtf
        dskf = rowmean(_sel_right(dsk8, e))
        acum = _sel_left(ltri, af)
        alast = jnp.sum(af, axis=0, keepdims=True)
        xdt = x * dtf
        ydiag = jnp.zeros((q, HALF), F32)
        for r in range(HEADS_PER_HALF):
            sel = lane == r * SSD_HEAD_DIM
            ac_r = jnp.sum(jnp.where(sel, acum, 0.0), axis=1, keepdims=True)
            a_r = jnp.sum(jnp.where(sel, af, 0.0), axis=1, keepdims=True)
            arow = jnp.sum(jnp.where(ll <= ss, a_r, 0.0), axis=0, keepdims=True)
            decay = jnp.exp(jnp.where(ll >= ss, ac_r - arow, -jnp.inf))
            yh = _bdot_nn(cb * decay, xdt)
            ydiag = ydiag + jnp.where(lane // SSD_HEAD_DIM == r, yh, 0.0)
        st = _bdot_tn(xdt * jnp.exp(alast - acum), bm)
        yoff = _bdot_nt(cm, s_in) * jnp.exp(acum)
        y = ydiag + yoff + dskf * x
        alast_col = jnp.sum(_sel_left_nt(et, adt_tot8), axis=1, keepdims=True) * 0.125
        outs.append((y, jnp.exp(alast_col) * s_in + st))
    return outs[0][0], outs[1][0], outs[0][1], outs[1][1]


def _ssd_specs(t, rev):
    q = SSD_CHUNK
    nc = t // q
    ci = (lambda c: nc - 1 - c) if rev else (lambda c: c)
    xcol0 = SSD_D_INNER // SSD_STATE
    return dict(
        x_lo=pl.BlockSpec((q, HALF), lambda g, c: (ci(c), 2 * g)),
        x_hi=pl.BlockSpec((q, HALF), lambda g, c: (ci(c), 2 * g + 1)),
        bm=pl.BlockSpec((q, SSD_STATE), lambda g, c: (ci(c), xcol0 + g)),
        cm=pl.BlockSpec((q, SSD_STATE), lambda g, c: (ci(c), xcol0 + SSD_GROUPS + g)),
        dt=pl.BlockSpec((None, q, PAD_HEADS), lambda g, c: (g, ci(c), 0)),
        par=pl.BlockSpec((None, 8, PAD_HEADS), lambda g, c: (g, 0, 0)),
        st=pl.BlockSpec((None, None, 2, HALF, SSD_STATE), lambda g, c: (ci(c), g, 0, 0, 0)),
        y=pl.BlockSpec((q, 2 * HALF), lambda g, c: (ci(c), g)),
        grp=pl.BlockSpec((q, SSD_STATE), lambda g, c: (ci(c), g)),
    )


def _ssd_fwd(xc, dt4, dtb, alog, dsk, *, name):
    t = xc.shape[0]
    nc = t // SSD_CHUNK
    sp = _ssd_specs(t, False)

    def body(xl, xh, bm, cm, dt, p0, p1, p2, y_ref, sin_ref, st_ref):
        @pl.when(pl.program_id(1) == 0)
        def _():
            st_ref[...] = jnp.zeros_like(st_ref)

        sin_ref[...] = st_ref[...]
        y_lo, y_hi, so_lo, so_hi = _ssd_chunk(xl[...], xh[...], bm[...], cm[...], dt[...], p0[...], p1[...],
                                              p2[...], st_ref[0], st_ref[1])
        y_ref[:, :HALF] = y_lo
        y_ref[:, HALF:] = y_hi
        st_ref[0] = so_lo
        st_ref[1] = so_hi

    return _pcall(
        body, name=name, grid=(SSD_GROUPS, nc),
        in_specs=[sp['x_lo'], sp['x_hi'], sp['bm'], sp['cm'], sp['dt'], sp['par'], sp['par'], sp['par']],
        out_specs=[sp['y'], sp['st']],
        out_shape=[_sds((t, SSD_D_INNER)), _sds((nc, SSD_GROUPS, 2, HALF, SSD_STATE))],
        scratch_shapes=[pltpu.VMEM((2, HALF, SSD_STATE), F32)],
        compiler_params=_params("parallel", "arbitrary"),
    )(xc, xc, xc, xc, dt4, dtb, alog, dsk)


def _ssd_bwd(xc, dt4, dtb, alog, dsk, sin, dy, *, name):
    t = xc.shape[0]
    nc = t // SSD_CHUNK
    sp = _ssd_specs(t, True)

    def body(xl, xh, bm, cm, dt, p0, p1, p2, sin_ref, dy_ref,
             dx_ref, db_ref, dc_ref, ddt_ref, dp0, dp1, dp2, dst_ref):
        first = pl.program_id(1) == 0

        @pl.when(first)
        def _():
            dst_ref[...] = jnp.zeros_like(dst_ref)

        _, vjp = jax.vjp(_ssd_chunk, xl[...], xh[...], bm[...], cm[...], dt[...], p0[...], p1[...], p2[...],
                         sin_ref[0], sin_ref[1])
        dxl, dxh, dbm, dcm, ddt, g0, g1, g2, ds_lo, ds_hi = vjp(
            (dy_ref[:, :HALF], dy_ref[:, HALF:], dst_ref[0], dst_ref[1]))
        dx_ref[:, :HALF] = dxl
        dx_ref[:, HALF:] = dxh
        db_ref[...] = dbm
        dc_ref[...] = dcm
        ddt_ref[...] = ddt
        dst_ref[0] = ds_lo
        dst_ref[1] = ds_hi
        for ref, g in ((dp0, g0), (dp1, g1), (dp2, g2)):
            tot = jnp.broadcast_to(jnp.sum(g, axis=0, keepdims=True), ref.shape)

            @pl.when(first)
            def _(ref=ref):
                ref[...] = jnp.zeros_like(ref)

            ref[...] += tot

    return _pcall(
        body, name=name, grid=(SSD_GROUPS, nc),
        in_specs=[sp['x_lo'], sp['x_hi'], sp['bm'], sp['cm'], sp['dt'], sp['par'], sp['par'], sp['par'],
                  sp['st'], sp['y']],
        out_specs=[sp['y'], sp['grp'], sp['grp'], sp['dt'], sp['par'], sp['par'], sp['par']],
        out_shape=[_sds((t, SSD_D_INNER)), _sds((t, SSD_GROUPS * SSD_STATE)), _sds((t, SSD_GROUPS * SSD_STATE)),
                   _sds((SSD_GROUPS, t, PAD_HEADS))] + [_sds((SSD_GROUPS, 8, PAD_HEADS))] * 3,
        scratch_shapes=[pltpu.VMEM((2, HALF, SSD_STATE), F32)],
        compiler_params=_params("parallel", "arbitrary"),
    )(xc, xc, xc, xc, dt4, dtb, alog, dsk, sin, dy)


def _gatenorm(y, z, g):
    v = y * _silu(z)
    return v * lax.rsqrt(jnp.mean(v * v, axis=-1, keepdims=True) + RMS_EPS) * g


S5_CH = S5_WIDTH // S5_BLOCKS
S5_ST = S5_CH * S5_STATE // S5_GROUP
SCAN_UNROLL = 8


def _cmul(ar, ai, br, bi):
    return ar * br - ai * bi, ar * bi + ai * br


def _segment_power(ar, ai, n):
    assert n & (n - 1) == 0
    for _ in range(n.bit_length() - 1):
        ar, ai = _cmul(ar, ai, ar, ai)
    return ar, ai


def _carry_in(fr, fi, pr, pi, reverse):
    rows = lax.broadcasted_iota(jnp.int32, fr.shape, 0)
    cr = jnp.zeros_like(fr[0:1])
    ci = jnp.zeros_like(cr)
    outr = jnp.zeros_like(fr)
    outi = jnp.zeros_like(fr)
    order = range(6, -1, -1) if reverse else range(1, 8)
    for j in order:
        src = j + 1 if reverse else j - 1
        nr, ni = _cmul(pr[0:1], pi[0:1], cr, ci)
        cr, ci = nr + fr[src:src + 1], ni + fi[src:src + 1]
        outr = jnp.where(rows == j, cr, outr)
        outi = jnp.where(rows == j, ci, outi)
    return outr, outi


def _s5_specs(t):
    return dict(ch=pl.BlockSpec((t, S5_CH), lambda j: (0, j)), st=pl.BlockSpec((t, S5_ST), lambda j: (0, j)),
                lam=pl.BlockSpec((1, S5_ST), lambda j: (0, j)),
                b=pl.BlockSpec((None, S5_CH, S5_ST), lambda j: (j, 0, 0)),
                c=pl.BlockSpec((None, S5_ST, S5_CH), lambda j: (j, 0, 0)))


def _s5_fwd(u5, bre, bim, cre, cim, lr, li, *, name):
    t = u5.shape[0]
    nrt = t // 8

    def body(u_ref, bre_ref, bim_ref, cre_ref, cim_ref, lr_ref, li_ref, sr_ref, si_ref, y_ref, br_ref, bi_ref):
        u = u_ref[...]
        br_ref[...] = _dot(u, bre_ref[...], _NN)
        bi_ref[...] = _dot(u, bim_ref[...], _NN)
        ar = jnp.broadcast_to(lr_ref[...], (8, S5_ST))
        ai = jnp.broadcast_to(li_ref[...], (8, S5_ST))

        def step(r, s, store):
            rows = pl.ds(pl.multiple_of(r * 8, 8), 8)
            nr, ni = _cmul(ar, ai, s[0], s[1])
            nr, ni = nr + br_ref[rows, :], ni + bi_ref[rows, :]
            if store:
                sr_ref[rows, :] = nr
                si_ref[rows, :] = ni
            return nr, ni

        zero = (jnp.zeros((8, S5_ST), F32), jnp.zeros((8, S5_ST), F32))
        fr, fi = lax.fori_loop(0, nrt, lambda r, s: step(r, s, False), zero, unroll=SCAN_UNROLL)
        pr, pi = _segment_power(ar, ai, nrt)
        init = _carry_in(fr, fi, pr, pi, False)
        lax.fori_loop(0, nrt, lambda r, s: step(r, s, True), init, unroll=SCAN_UNROLL)
        y_ref[...] = _dot(sr_ref[...], cre_ref[...], _NN) - _dot(si_ref[...], cim_ref[...], _NN)

    sp = _s5_specs(t)
    w = S5_BLOCKS * S5_ST
    return _pcall(
        body, name=name, grid=(S5_BLOCKS,),
        in_specs=[sp['ch'], sp['b'], sp['b'], sp['c'], sp['c'], sp['lam'], sp['lam']],
        out_specs=[sp['st'], sp['st'], sp['ch']], out_shape=[_sds((t, w)), _sds((t, w)), _sds((t, S5_WIDTH))],
        scratch_shapes=[pltpu.VMEM((t, S5_ST), F32)] * 2, compiler_params=_params("parallel"),
    )(u5, bre, bim, cre, cim, lr, li)


def _s5_bwd(dy, du_direct, u5, sr, si, bre, bim, cre, cim, lr, li, *, name):
    t = u5.shape[0]
    nrt = t // 8

    def body(dy_ref, dd_ref, u_ref, sr_ref, si_ref, bre_ref, bim_ref, cre_ref, cim_ref, lr_ref, li_ref,
             du_ref, dbre_ref, dbim_ref, dcre_ref, dcim_ref, dlr_ref, dli_ref, gr_ref, gi_ref):
        dyv = dy_ref[...]
        gr_ref[...] = _dot(dyv, cre_ref[...], _NT)
        gi_ref[...] = -_dot(dyv, cim_ref[...], _NT)
        dcre_ref[...] = _dot(sr_ref[...], dyv, _TN)
        dcim_ref[...] = -_dot(si_ref[...], dyv, _TN)
        dr_ref, di_ref = gr_ref, gi_ref
        ar = jnp.broadcast_to(lr_ref[...], (8, S5_ST))
        ai = -jnp.broadcast_to(li_ref[...], (8, S5_ST))
        zero = jnp.zeros((8, S5_ST), F32)

        def step1(k, g):
            rows = pl.ds(pl.multiple_of((nrt - 1 - k) * 8, 8), 8)
            nr, ni = _cmul(ar, ai, g[0], g[1])
            return nr + dr_ref[rows, :], ni + di_ref[rows, :]

        fr, fi = lax.fori_loop(0, nrt, step1, (zero, zero), unroll=SCAN_UNROLL)
        pr, pi = _segment_power(ar, ai, nrt)
        init = _carry_in(fr, fi, pr, pi, True)

        def step2(k, carry):
            gr, gi, accr, acci = carry
            r = nrt - 1 - k
            rows = pl.ds(pl.multiple_of(r * 8, 8), 8)
            prev = pl.ds(pl.multiple_of(jnp.maximum(r - 1, 0) * 8, 8), 8)
            nr, ni = _cmul(ar, ai, gr, gi)
            nr, ni = nr + dr_ref[rows, :], ni + di_ref[rows, :]
            gr_ref[rows, :] = nr
            gi_ref[rows, :] = ni
            keep = jnp.where(r > 0, 1.0, 0.0)
            pr_, pi_ = sr_ref[prev, :] * keep, si_ref[prev, :] * keep
            return nr, ni, accr + (pr_ * nr + pi_ * ni), acci + (pr_ * ni - pi_ * nr)

        _, _, accr, acci = lax.fori_loop(0, nrt, step2, (init[0], init[1], zero, zero), unroll=SCAN_UNROLL)
        last = pl.ds((nrt - 1) * 8, 8)
        pr_, pi_ = _shift_down(sr_ref[last, :], 1), _shift_down(si_ref[last, :], 1)
        g0r, g0i = gr_ref[0:8, :], gi_ref[0:8, :]
        accr = accr + (pr_ * g0r + pi_ * g0i)
        acci = acci + (pr_ * g0i - pi_ * g0r)
        dlr_ref[...] = jnp.sum(accr, axis=0, keepdims=True)
        dli_ref[...] = jnp.sum(acci, axis=0, keepdims=True)
        u = u_ref[...]
        dbre_ref[...] = _dot(u, gr_ref[...], _TN)
        dbim_ref[...] = _dot(u, gi_ref[...], _TN)
        du = dd_ref[...] + _dot(gr_ref[...], bre_ref[...], _NT) + _dot(gi_ref[...], bim_ref[...], _NT)
        du_ref[...] = du.astype(du_ref.dtype)

    sp = _s5_specs(t)
    w = S5_BLOCKS * S5_ST
    return _pcall(
        body, name=name, grid=(S5_BLOCKS,),
        in_specs=[sp['ch'], sp['ch'], sp['ch'], sp['st'], sp['st'], sp['b'], sp['b'], sp['c'], sp['c'], sp['lam'], sp['lam']],
        out_specs=[sp['ch'], sp['b'], sp['b'], sp['c'], sp['c'], sp['lam'], sp['lam']],
        out_shape=[_sds((t, S5_WIDTH), BF16), _sds((S5_BLOCKS, S5_CH, S5_ST)), _sds((S5_BLOCKS, S5_CH, S5_ST)),
                   _sds((S5_BLOCKS, S5_ST, S5_CH)), _sds((S5_BLOCKS, S5_ST, S5_CH)), _sds((1, w)), _sds((1, w))],
        scratch_shapes=[pltpu.VMEM((t, S5_ST), F32)] * 2, compiler_params=_params("parallel"),
    )(dy, du_direct, u5, sr, si, bre, bim, cre, cim, lr, li)


def _s5_expander():
    n = lax.broadcasted_iota(jnp.int32, (S5_STATE, S5_STATE * S5_GROUP), 0)
    j = lax.broadcasted_iota(jnp.int32, (S5_STATE, S5_STATE * S5_GROUP), 1)
    return jnp.where(n == j // S5_GROUP, 1.0, 0.0).astype(F32)


def _s5_discretise(lam_re, lam_im, log_step, b_re, b_im):
    lr = jnp.minimum(lam_re, S5_MAX_REAL)
    step = jnp.exp(log_step)
    mag = jnp.exp(lr * step)
    ang = lam_im * step
    lbr, lbi = mag * jnp.cos(ang), mag * jnp.sin(ang)
    p, q = lbr - 1.0, lbi
    den = lr * lr + lam_im * lam_im
    cr, ci = (p * lr + q * lam_im) / den, (q * lr - p * lam_im) / den
    e = _s5_expander()
    cre, cie = _fdot(cr, e), _fdot(ci, e)
    return lbr, lbi, cre * b_re - cie * b_im, cre * b_im + cie * b_re


def _s5_params_fwd(lam_re, lam_im, log_step, b_re, b_im, *, name):
    g, n, w = lam_re.shape[0], S5_STATE, S5_STATE * S5_GROUP

    def body(a, b, c, d, e, o0, o1, o2, o3):
        for ref, val in zip((o0, o1, o2, o3), _s5_discretise(a[...], b[...], c[...], d[...], e[...])):
            ref[...] = val

    return _pcall(body, name=name, out_shape=[_sds((g, n)), _sds((g, n)), _sds((g, w)), _sds((g, w))])(
        lam_re, lam_im, log_step, b_re, b_im)


def _s5_params_bwd(lam_re, lam_im, log_step, b_re, b_im, cts, *, name):
    g, n, w = S5_GROUPS, S5_STATE, S5_STATE * S5_GROUP

    def body(a, b, c, d, e, c0, c1, c2, c3, o0, o1, o2, o3, o4):
        _, vjp = jax.vjp(_s5_discretise, a[...], b[...], c[...], d[...], e[...])
        for ref, val in zip((o0, o1, o2, o3, o4), vjp((c0[...], c1[...], c2[...], c3[...]))):
            ref[...] = val

    return _pcall(body, name=name,
                  out_shape=[_sds((g, n)), _sds((g, n)), _sds((g, 1)), _sds((g, w)), _sds((g, w))])(
        lam_re, lam_im, log_step, b_re, b_im, *cts)


def _s5_prepare(w):
    rows = DEPTH * S5_GROUPS
    lbr, lbi, bbr, bbi = _s5_params_fwd(
        w['s5_lambda_re'].reshape(rows, -1), w['s5_lambda_im'].reshape(rows, -1), w['s5_log_step'].reshape(rows, 1),
        w['s5_b_re'].reshape(rows, -1), w['s5_b_im'].reshape(rows, -1), name="s5_par")
    bd = lambda m: _blockdiag(m.reshape(rows, S5_STATE, S5_GROUP).transpose(0, 2, 1), S5_GROUP, S5_STATE).astype(BF16)
    cd = lambda m: _blockdiag(m.reshape(rows, S5_GROUP, S5_STATE).transpose(0, 2, 1), S5_STATE, S5_GROUP).astype(BF16)
    bre, bim, cre, cim = bd(bbr), bd(bbi), cd(w['s5_c_re']), cd(w['s5_c_im'])
    lr, li = lbr.reshape(DEPTH, 1, -1), lbi.reshape(DEPTH, 1, -1)
    blk = lambda a, i: a[i * S5_BLOCKS:(i + 1) * S5_BLOCKS]
    return [dict(bre=blk(bre, i), bim=blk(bim, i), cre=blk(cre, i), cim=blk(cim, i), lr=lr[i], li=li[i])
            for i in range(DEPTH)]


def _perm(a):
    t, c = a.shape
    return a.reshape(8, t // 8, c).transpose(1, 0, 2).reshape(t, c)


def _unperm(a):
    t, c = a.shape
    return a.reshape(t // 8, 8, c).transpose(1, 0, 2).reshape(t, c)


def _blockdiag(m, rows_inner, cols_inner):
    nblk = m.shape[0] // 8
    m = m.reshape(nblk, 8, rows_inner, cols_inner)
    eye = jnp.eye(8, dtype=m.dtype)
    out = m[:, :, :, None, :] * eye[None, :, None, :, None]
    return out.reshape(nblk, 8 * rows_inner, 8 * cols_inner)


def _blockdiag_extract(m, rows_inner, cols_inner):
    m = m.reshape(S5_BLOCKS, 8, rows_inner, 8, cols_inner)
    d = jnp.diagonal(m, axis1=1, axis2=3)
    return d.transpose(0, 3, 1, 2).reshape(S5_GROUPS, rows_inner, cols_inner)


Z0, XBC0, GA0, GB0 = 0, SSD_D_INNER, SSD_D_INNER + SSD_CONV_DIM, SSD_D_INNER + SSD_CONV_DIM + D_MODEL
BIG = GB0 + D_MODEL


def _mixer_fwd(h, p, tm):
    t = h.shape[0]
    nrow = t // tm
    u = _rms_fwd(h, p['pre_g'], name="mix_rms", tm=tm)
    u_p = _perm(u)
    proj = _mm(u, p['w_big'], name="mix_in")
    dtr = _mm(u, p['w_dt'], name="mix_in_dt")
    u5 = _mm(u_p, p['w_u5'], name="mix_in_s5")
    late, proj = lax.optimization_barrier((p['late'], proj))
    by_rows = lambda a: a.reshape(-1, a.shape[-1])
    p = dict(p, w_a=by_rows(late['w_branch_a']), w_b=by_rows(late['w_branch_b']), w_out=by_rows(late['w_out']),
             w_glu=late['s5_w_glu'][:, 0].transpose(1, 0, 2).reshape(late['s5_w_glu'].shape[2], -1))
    xc = _conv_fwd(proj, XBC0, p['conv_w'], p['conv_b'], name="ssd_conv")
    dt4 = jnp.pad(dtr.reshape(t, SSD_GROUPS, 8).transpose(1, 0, 2), ((0, 0), (0, 0), (0, PAD_HEADS - 8)))
    y_ssd, s_in = _ssd_fwd(xc, dt4, p['dt_bias8'], p['a_log8'], p['d8'], name="ssd_scan")
    gw = SSD_D_INNER // SSD_GROUPS
    ya = _rows(_gatenorm, name="ssd_gate", nrow=nrow, ncol=SSD_GROUPS,
               ins=[(y_ssd, _rspec(tm, gw, 0, True)), (proj, _rspec(tm, gw, Z0 // gw, True)),
                    (p['norm_g'], _bspec(gw, 0, True))],
               outs=[(_sds((t, SSD_D_INNER), BF16), _rspec(tm, gw, 0, True), False)])[0]
    y_a = _mm(ya, p['w_a'], name="mix_a")
    bre, bim, cre, cim, lr, li = (p['s5'][k] for k in ('bre', 'bim', 'cre', 'cim', 'lr', 'li'))
    sr, si, y5 = _s5_fwd(u5, bre, bim, cre, cim, lr, li, name="s5_scan")
    y5g = _rows(lambda a, b, d: _gelu(a + d * b), name="s5_act", nrow=nrow,
                ins=[(y5, _rspec(tm, S5_WIDTH)), (u5, _rspec(tm, S5_WIDTH)), (p['s5_d'], _bspec(S5_WIDTH))],
                outs=[(_sds((t, S5_WIDTH), BF16), _rspec(tm, S5_WIDTH), False)])[0]
    vg = _mm(y5g, p['w_glu'], name="s5_glu")
    ybin = _rows(lambda a, b: a * _sigmoid(b), name="s5_glu_act", nrow=nrow,
                 ins=[(vg, _rspec(tm, S5_WIDTH, 0)), (vg, _rspec(tm, S5_WIDTH, 1))],
                 outs=[(_sds((t, S5_WIDTH), BF16), _rspec(tm, S5_WIDTH), False)])[0]
    y_b = _unperm(_mm(ybin, p['w_b'], name="mix_b"))
    merged = _rows(lambda ga, gb, a, b: _sigmoid(ga) * a + _sigmoid(gb) * b, name="mix_merge", nrow=nrow,
                   ins=[(proj, _rspec(tm, D_MODEL, GA0 // D_MODEL)), (proj, _rspec(tm, D_MODEL, GB0 // D_MODEL)),
                        (y_a, _rspec(tm, D_MODEL)), (y_b, _rspec(tm, D_MODEL))],
                   outs=[(_sds((t, D_MODEL), BF16), _rspec(tm, D_MODEL), False)])[0]
    m = _mm(merged, p['w_out'], name="mix_out")
    out = _resid_fwd(h, m, p['post_g'], 1.0, name="mix_res", tm=tm)
    saved = dict(w_a=p['w_a'], w_b=p['w_b'], w_out=p['w_out'], w_glu=p['w_glu'], h=h, u=u, u_p=u_p, proj=proj, u5=u5, xc=xc, dt4=dt4, s_in=s_in, y_ssd=y_ssd, ya=ya, y_a=y_a,
                 bre=bre, bim=bim, cre=cre, cim=cim, lr=lr, li=li, sr=sr, si=si, y5=y5, y5g=y5g, vg=vg, ybin=ybin,
                 y_b=y_b, merged=merged, m=m)
    return out, saved


def _mixer_bwd(dh, p, s, tm, after_first):
    t = dh.shape[0]
    nrow = t // tm
    proj = s['proj']
    bufs = {}

    def grad_mm(a, b, wname, axis, name):
        dw = _mm(a, b, ta=True, name=name, out_dtype=BF16, col_shards=axis == 'cols')
        bufs[wname] = dw if axis == 'cols' else dw.reshape(N_DEV, 1, dw.shape[0] // N_DEV, dw.shape[1])

    dm, dpost = _resid_bwd(s['m'], p['post_g'], dh, 1.0, name="mix_res_bwd", tm=tm)
    dm = after_first(dm)
    dmerged = _mm(dm, s['w_out'], tb=True, name="mix_out_dx")
    grad_mm(s['merged'], dm, 'w_out', 'rows', "mix_out_dw")

    def merge_bwd(ga, gb, a, b, d):
        _, vjp = jax.vjp(lambda ga_, gb_, a_, b_: _sigmoid(ga_) * a_ + _sigmoid(gb_) * b_, ga, gb, a, b)
        dga, dgb, da, db = vjp(d)
        return jnp.concatenate([dga, dgb], axis=1), da, db

    dgab, dy_a, dy_b = _rows(
        merge_bwd, name="mix_merge_bwd", nrow=nrow,
        ins=[(proj, _rspec(tm, D_MODEL, GA0 // D_MODEL)), (proj, _rspec(tm, D_MODEL, GB0 // D_MODEL)),
             (s['y_a'], _rspec(tm, D_MODEL)), (s['y_b'], _rspec(tm, D_MODEL)), (dmerged, _rspec(tm, D_MODEL))],
        outs=[(_sds((t, 2 * D_MODEL), BF16), _rspec(tm, 2 * D_MODEL), False),
              (_sds((t, D_MODEL), BF16), _rspec(tm, D_MODEL), False),
              (_sds((t, D_MODEL), BF16), _rspec(tm, D_MODEL), False)])
    dya = _mm(dy_a, s['w_a'], tb=True, name="mix_a_dx")
    grad_mm(s['ya'], dy_a, 'w_branch_a', 'rows', "mix_a_dw")
    gw = SSD_D_INNER // SSD_GROUPS

    def gate_bwd(y, z, g, d):
        _, vjp = jax.vjp(_gatenorm, y, z, g)
        return vjp(d)

    dy_ssd, dz, dnorm = _rows(
        gate_bwd, name="ssd_gate_bwd", nrow=nrow, ncol=SSD_GROUPS,
        ins=[(s['y_ssd'], _rspec(tm, gw, 0, True)), (proj, _rspec(tm, gw, Z0 // gw, True)),
             (p['norm_g'], _bspec(gw, 0, True)), (dya, _rspec(tm, gw, 0, True))],
        outs=[(_sds((t, SSD_D_INNER)), _rspec(tm, gw, 0, True), False),
              (_sds((t, SSD_D_INNER), BF16), _rspec(tm, gw, 0, True), False),
              (_sds((1, SSD_D_INNER)), _bspec(gw, 0, True), True)])
    dxs, dbm, dcm, ddt4, ddtb, dalog, ddsk = _ssd_bwd(s['xc'], s['dt4'], p['dt_bias8'], p['a_log8'], p['d8'],
                                                      s['s_in'], dy_ssd, name="ssd_scan_bwd")
    dxbc, dconv_w, dconv_b = _conv_bwd(proj, XBC0, p['conv_w'], p['conv_b'], (dxs, dbm, dcm), name="ssd_conv_bwd")
    ddtr = ddt4[:, :, :8].transpose(1, 0, 2).reshape(t, SSD_HEADS)
    dy_bp = _perm(dy_b)
    dybin = _mm(dy_bp, s['w_b'], tb=True, name="mix_b_dx")
    grad_mm(s['ybin'], dy_bp, 'w_branch_b', 'rows', "mix_b_dw")

    def glu_bwd(a, b, d):
        _, vjp = jax.vjp(lambda a_, b_: a_ * _sigmoid(b_), a, b)
        da, db = vjp(d)
        return jnp.concatenate([da, db], axis=1)

    dvg = _rows(glu_bwd, name="s5_glu_act_bwd", nrow=nrow,
                ins=[(s['vg'], _rspec(tm, S5_WIDTH, 0)), (s['vg'], _rspec(tm, S5_WIDTH, 1)), (dybin, _rspec(tm, S5_WIDTH))],
                outs=[(_sds((t, 2 * S5_WIDTH), BF16), _rspec(tm, 2 * S5_WIDTH), False)])[0]
    dy5g = _mm(dvg, s['w_glu'], tb=True, name="s5_glu_dx")
    grad_mm(s['y5g'], dvg, 's5_w_glu', 'cols', "s5_glu_dw")

    def act_bwd(a, b, d, g):
        _, vjp = jax.vjp(lambda a_, b_, d_: _gelu(a_ + d_ * b_), a, b, d)
        return vjp(g)

    dy5, du5_direct, ds5d = _rows(
        act_bwd, name="s5_act_bwd", nrow=nrow,
        ins=[(s['y5'], _rspec(tm, S5_WIDTH)), (s['u5'], _rspec(tm, S5_WIDTH)), (p['s5_d'], _bspec(S5_WIDTH)),
             (dy5g, _rspec(tm, S5_WIDTH))],
        outs=[(_sds((t, S5_WIDTH), BF16), _rspec(tm, S5_WIDTH), False), (_sds((t, S5_WIDTH)), _rspec(tm, S5_WIDTH), False),
              (_sds((1, S5_WIDTH)), _bspec(S5_WIDTH), True)])
    du5, dbre, dbim, dcre, dcim, dlr, dli = _s5_bwd(dy5, du5_direct, s['u5'], s['sr'], s['si'], s['bre'], s['bim'],
                                                     s['cre'], s['cim'], s['lr'], s['li'], name="s5_scan_bwd")
    du_p = _mm(du5, p['w_u5'], tb=True, name="mix_in_s5_dx")
    dw_u5 = _mm(s['u_p'], du5, ta=True, name="mix_in_s5_dw", out_dtype=BF16)
    ext_b = lambda m: _blockdiag_extract(m, S5_GROUP, S5_STATE).transpose(0, 2, 1).reshape(S5_GROUPS, S5_STATE * S5_GROUP)
    dlam_re, dlam_im, dlog_step, db_re, db_im = _s5_params_bwd(
        p['lam_re'], p['lam_im'], p['log_step'], p['b_re'], p['b_im'],
        (dlr.reshape(S5_GROUPS, S5_STATE), dli.reshape(S5_GROUPS, S5_STATE), ext_b(dbre), ext_b(dbim)), name="s5_par_bwd")
    dc_re = _blockdiag_extract(dcre, S5_STATE, S5_GROUP).transpose(0, 2, 1)
    dc_im = _blockdiag_extract(dcim, S5_STATE, S5_GROUP).transpose(0, 2, 1)
    dproj = jnp.concatenate([dz, dxbc, dgab], axis=1)
    du_big = _mm(dproj, p['w_big'], tb=True, name="mix_in_dx")
    du_dt = _mm(ddtr, p['w_dt'], tb=True, name="mix_in_dt_dx")
    dw_big = _mm(s['u'], dproj, ta=True, name="mix_in_dw", out_dtype=BF16)
    dw_dt = _mm(s['u'], ddtr, ta=True, name="mix_in_dt_dw", out_dtype=BF16)
    dh_in, dpre = _rms_bwd(s['h'], p['pre_g'], dh, [du_big, du_dt, _unperm(du_p)], name="mix_rms_bwd", tm=tm)
    dw_in = jnp.concatenate([dw_big[:, :GA0], dw_dt, dw_u5, dw_big[:, GA0:]], axis=1)
    bufs['w_in'] = dw_in.reshape(D_MODEL, N_DEV, -1).transpose(1, 0, 2)[:, None]
    grads = {
        'mix_pre_g': dpre, 'mix_post_g': dpost, 'ssd_conv_w': dconv_w, 'ssd_conv_b': dconv_b,
        'ssd_dt_bias': ddtb[:, 0, :8].reshape(-1), 'ssd_a_log': dalog[:, 0, :8].reshape(-1),
        'ssd_d': ddsk[:, 0, :8].reshape(-1), 'ssd_norm_g': dnorm,
        's5_lambda_re': dlam_re, 's5_lambda_im': dlam_im,
        's5_b_re': db_re.reshape(S5_GROUPS, S5_STATE, S5_GROUP), 's5_b_im': db_im.reshape(S5_GROUPS, S5_STATE, S5_GROUP),
        's5_c_re': dc_re, 's5_c_im': dc_im, 's5_log_step': dlog_step.reshape(-1), 's5_d': ds5d,
    }
    return dh_in, bufs, grads


HBM_SPEC = pl.BlockSpec(memory_space=pltpu.HBM)


def _place():
    return lax.axis_index("x"), lax.axis_index("y"), lax.axis_index("c")


GATHER_COLLECTIVE_ID = 1


def _all_gather(shards, *, name, on_sequencer=False):
    n = len(shards)

    def body(*refs):
        x_refs, out_refs = refs[:n], refs[n:2 * n]
        send_sems, recv_sems, local_sems = refs[2 * n:]
        x, y, c = _place()
        me, sibling = (x, y, c), (x, y, 1 - c)
        chips = [(1 - x, y), (x, 1 - y), (1 - x, 1 - y)]
        if on_sequencer:
            _handshake([sibling] + [(*chip, c) for chip in chips])

        def slot(o, px, py, pc):
            return out_refs[o].at[4 * px + 2 * py + pc]

        def copy(o, k, block, to, src=None):
            return pltpu.make_async_remote_copy(
                src_ref=slot(o, *block) if src is None else src, dst_ref=slot(o, *block),
                send_sem=send_sems.at[7 * o + k], recv_sem=recv_sems.at[7 * o + k], device_id=to, device_id_type=MESH)

        mine = [pltpu.make_async_copy(x_refs[o], slot(o, *me), local_sems.at[o]) for o in range(n)]
        for cp in mine:
            cp.start()
        first = []
        for j, chip in enumerate(chips):
            first += [copy(o, 1 + j, me, (*chip, c), src=x_refs[o]) for o in range(n)]
        first += [copy(o, 0, me, sibling, src=x_refs[o]) for o in range(n)]
        for cp in first:
            cp.start()
        passed = []
        for j, chip in enumerate(chips):
            for o in range(n):
                copy(o, 1 + j, (*chip, c), me).wait_recv()
                passed.append(copy(o, 4 + j, (*chip, c), sibling))
                passed[-1].start()
        for o in range(n):
            copy(o, 0, sibling, me).wait_recv()
        for j, chip in enumerate(chips):
            for o in range(n):
                copy(o, 4 + j, (*chip, 1 - c), me).wait_recv()
        for cp in first + passed:
            cp.wait_send()
        for cp in mine:
            cp.wait()

    out_shape = [jax.ShapeDtypeStruct((N_DEV,) + s.shape, s.dtype) for s in shards]
    sems = [pltpu.SemaphoreType.DMA((7 * n,)), pltpu.SemaphoreType.DMA((7 * n,)), pltpu.SemaphoreType.DMA((n,))]
    if on_sequencer:
        return _scall(body, name=name, out_type=out_shape, scratch_types=sems, collective_id=GATHER_COLLECTIVE_ID)(*shards)
    return _pcall(body, name=name, in_specs=[HBM_SPEC] * n, out_specs=[HBM_SPEC] * n, out_shape=out_shape,
                  scratch_shapes=sems)(*shards)


N_CHIPS = 4


SIBLING_COLLECTIVE_ID = 2
CHIPS_COLLECTIVE_ID = 3


def _handshake(peers):
    barrier = pltpu.get_barrier_semaphore()
    for peer in peers:
        pl.semaphore_signal(barrier, inc=1, device_id=peer, device_id_type=MESH)
    pl.semaphore_wait(barrier, len(peers))


def _exchange_sibling(grads, *, name):
    n = len(grads)

    def body(*refs):
        p_refs, q_refs = refs[:n], refs[n:2 * n]
        send_sems, recv_sems = refs[2 * n:]
        x, y, c = _place()
        _handshake([(x, y, 1 - c)])
        copies = [pltpu.make_async_remote_copy(
            src_ref=p_refs[o].at[k, 1 - c], dst_ref=q_refs[o].at[k], send_sem=send_sems.at[N_CHIPS * o + k],
            recv_sem=recv_sems.at[N_CHIPS * o + k], device_id=(x, y, 1 - c), device_id_type=MESH)
            for o in range(n) for k in range(N_CHIPS)]
        for cp in copies:
            cp.start()
        for cp in copies:
            cp.wait()

    return _scall(
        body, name=name, out_type=[jax.ShapeDtypeStruct((N_CHIPS,) + g.shape[2:], g.dtype) for g in grads],
        scratch_types=[pltpu.SemaphoreType.DMA((N_CHIPS * n,)), pltpu.SemaphoreType.DMA((N_CHIPS * n,))],
        collective_id=SIBLING_COLLECTIVE_ID,
    )(*grads)


def _pair_sum(own, got, *, name):
    _, _, r, l = own.shape
    tr = _tile(r, 512, 16)
    c = lax.axis_index("c").astype(jnp.int32).reshape(1)

    def body(c_ref, p_ref, q_ref, o_ref):
        o_ref[...] = (p_ref[...].astype(F32) + q_ref[...].astype(F32)).astype(o_ref.dtype)

    return _pcall(
        body, name=name,
        grid_spec=pltpu.PrefetchScalarGridSpec(
            num_scalar_prefetch=1, grid=(N_CHIPS, r // tr),
            in_specs=[pl.BlockSpec((None, None, tr, l), lambda k, i, cr: (k, cr[0], i, 0)),
                      pl.BlockSpec((None, tr, l), lambda k, i, cr: (k, i, 0))],
            out_specs=pl.BlockSpec((None, tr, l), lambda k, i, cr: (k, i, 0))),
        out_shape=jax.ShapeDtypeStruct((N_CHIPS, r, l), own.dtype),
        compiler_params=_params("parallel", "parallel"),
    )(c, own, got)


def _exchange_chips(parts, *, name):
    n = len(parts)

    def body(*refs):
        p_refs, g_refs = refs[:n], refs[n:2 * n]
        send_sems, recv_sems, local_sems = refs[2 * n:]
        x, y, c = _place()
        mine = 2 * x + y
        chips = [(1 - x, y), (x, 1 - y), (1 - x, 1 - y)]
        _handshake([(*chip, c) for chip in chips])
        own = [pltpu.make_async_copy(p_refs[o].at[mine], g_refs[o].at[mine], local_sems.at[o]) for o in range(n)]
        for cp in own:
            cp.start()
        copies = []
        for j, (px, py) in enumerate(chips):
            copies += [pltpu.make_async_remote_copy(
                src_ref=p_refs[o].at[2 * px + py], dst_ref=g_refs[o].at[mine], send_sem=send_sems.at[3 * o + j],
                recv_sem=recv_sems.at[3 * o + j], device_id=(px, py, c), device_id_type=MESH) for o in range(n)]
        for cp in copies:
            cp.start()
        for cp in copies:
            cp.wait()
        for cp in own:
            cp.wait()

    return _scall(
        body, name=name, out_type=[jax.ShapeDtypeStruct(p.shape, p.dtype) for p in parts],
        scratch_types=[pltpu.SemaphoreType.DMA((3 * n,)), pltpu.SemaphoreType.DMA((3 * n,)), pltpu.SemaphoreType.DMA((n,))],
        collective_id=CHIPS_COLLECTIVE_ID,
    )(*parts)


def _sum_slots(g, *, name):
    n, r, l = g.shape
    tr = _tile(r, 512, 16)

    def body(g_ref, o_ref):
        acc = g_ref[0].astype(F32)
        for k in range(1, n):
            acc = acc + g_ref[k].astype(F32)
        o_ref[...] = acc

    return _pcall(
        body, name=name, grid=(r // tr,), in_specs=[pl.BlockSpec((n, tr, l), lambda i: (0, i, 0))],
        out_specs=pl.BlockSpec((tr, l), lambda i: (i, 0)), out_shape=_sds((r, l)),
        compiler_params=_params("parallel"),
    )(g)


TRANSPOSED = ('ffn1_w_gate', 'ffn1_w_up', 'ffn2_w_gate', 'ffn2_w_up')
GATHER_CHUNKS = (('ffn1', ['ffn1_w_gate', 'ffn1_w_up']), ('ffn1_down', ['ffn1_w_down']),
                 ('mix_in', ['w_in', 'ssd_conv_w']), ('mix', ['w_branch_a', 's5_w_glu', 'w_branch_b', 'w_out']),
                 ('ffn2', ['ffn2_w_gate', 'ffn2_w_up']), ('ffn2_down', ['ffn2_w_down']))
LATE = ('ffn1_w_down', 'ffn2_w_down', 'w_branch_a', 's5_w_glu', 'w_branch_b', 'w_out')
SUBLAYERS = (('ffn1', ['ffn1_w_gate', 'ffn1_w_up', 'ffn1_w_down']),
             ('mix', ['w_in', 'ssd_conv_w', 'w_branch_a', 's5_w_glu', 'w_branch_b', 'w_out']),
             ('ffn2', ['ffn2_w_gate', 'ffn2_w_up', 'ffn2_w_down']))


def _gather_weights(w):
    layers, first = [], None
    for i in range(DEPTH):
        g = {}
        for tag, names in GATHER_CHUNKS:
            shards =[w[n][i:i + 1] if n == 'ssd_conv_w' else
                      (w[n][i:i + 1].transpose(0, 2, 1) if n in TRANSPOSED else w[n][i:i + 1]).astype(BF16) for n in names]
            if first is None:
                first = got = _all_gather(shards, name=f"gather_{tag}")
            else:
                shards, first = lax.optimization_barrier((shards, first))
                got = _all_gather(shards, name=f"gather_{tag}", on_sequencer=True)
            g.update(zip(names, got))
        layers.append(g)
    layers[0].update(zip(GATHER_CHUNKS[0][1], first))
    return layers


class _ReduceScatter:
    @staticmethod
    def sibling(tag, bufs):
        names = list(bufs)
        own = [bufs[n].reshape((N_CHIPS, 2) + bufs[n].shape[1:]) for n in names]
        return (tag, names), (own, _exchange_sibling(own, name=f"reduce_sibling_{tag}"))

    @staticmethod
    def chips(meta, arrays):
        (tag, names), (own, got) = meta, arrays
        flat = lambda a, lead: a.reshape(lead + (-1, a.shape[-1]))
        parts = [_pair_sum(flat(o, (N_CHIPS, 2)), flat(g, (N_CHIPS,)), name=f"reduce_pair_sum_{n}").reshape(g.shape)
                 for n, o, g in zip(names, own, got)]
        return names, _exchange_chips(parts, name=f"reduce_chips_{tag}")

    @staticmethod
    def done(names, slots):
        return dict(zip(names, slots))

    @staticmethod
    def small(grads):
        return _reduce_small(grads)


def _reduce_small(grads):
    flat = jnp.concatenate([g.astype(F32).reshape(-1) for g in grads.values()])
    pad = (-flat.shape[0]) % (8 * LANES)
    flat = jnp.concatenate([flat, jnp.zeros((pad,), F32)]).reshape(-1, LANES)
    gathered = _all_gather([flat], name="gather_small_grads", on_sequencer=True)[0]
    total = _sum_slots(gathered, name="sum_small_grads").reshape(-1)
    out, o = {}, 0
    for n, g in grads.items():
        out[n] = total[o:o + g.size].reshape(g.shape)
        o += g.size
    return out


def _adamw(w, g, m, v, *, name, slots=False):
    shape = w.shape
    if slots:
        lyr, rows, lanes = shape
        w2, m2, v2 = w, m, v
        tr = _tile(rows, 256, 16)
        nrt = rows // tr
        grid = (lyr, nrt)
        spec = pl.BlockSpec((None, tr, lanes), lambda l, i: (l, i, 0))
        g_specs = [pl.BlockSpec((N_CHIPS, None, tr, lanes),
                                lambda l, i, k=k: (0, 0, jnp.where(l == k, i, jnp.where(l > k, nrt - 1, 0)), 0))
                   for k in range(lyr)]
        g_args = list(g)
        out_shape = [_sds(shape)] * 4
    else:
        lanes = shape[-1] if (shape[-1] >= 128 or w.size % LANES) else LANES
        as2d = lambda a: a.reshape(-1, lanes)
        w2, m2, v2 = as2d(w), as2d(m), as2d(v)
        r = w2.shape[0]
        tr = _tile(r, 256, 8)
        grid = (1, r // tr)
        spec = pl.BlockSpec((tr, lanes), lambda l, i: (i, 0))
        g_specs, g_args = [spec], [as2d(g)]
        out_shape = [_sds((r, lanes))] * 4
    n_g = len(g_args)

    def body(w_ref, *rest):
        g_refs = rest[:n_g]
        m_ref, v_ref, go_ref, d_ref, mo_ref, vo_ref = rest[n_g:]
        if slots:
            gg = None
            for k, g_ref in enumerate(g_refs):
                tot = g_ref[0].astype(F32)
                for c in range(1, N_CHIPS):
                    tot = tot + g_ref[c].astype(F32)
                gg = tot if gg is None else jnp.where(pl.program_id(0) == k, tot, gg)
        else:
            gg = g_refs[0][...]
        go_ref[...] = gg
        mn = ADAM_B1 * m_ref[...] + (1.0 - ADAM_B1) * gg
        vn = ADAM_B2 * v_ref[...] + (1.0 - ADAM_B2) * (gg * gg)
        m_hat = mn / (1.0 - ADAM_B1 ** ADAM_STEP)
        v_hat = vn / (1.0 - ADAM_B2 ** ADAM_STEP)
        d_ref[...] = -ADAM_LR * (m_hat / (jnp.sqrt(v_hat) + ADAM_EPS) + ADAM_WD * w_ref[...])
        mo_ref[...] = mn
        vo_ref[...] = vn

    res = _pcall(
        body, name=name, grid=grid, in_specs=[spec] + g_specs + [spec, spec], out_specs=[spec] * 4,
        out_shape=out_shape, compiler_params=_params("arbitrary", "arbitrary"),
    )(w2, *g_args, m2, v2)
    return tuple(a.reshape(shape) for a in res)


def _sublayer_params(w, g, i, k, s5):
    row = lambda a: a.astype(F32).reshape(1, -1)
    if k != 'mix':
        return dict(layer=i, pre_g=row(w[f'{k}_pre_g'][i]), post_g=row(w[f'{k}_post_g'][i]),
                    w_gate=g[f'{k}_w_gate'], w_up=g[f'{k}_w_up'], w_down=g[f'{k}_w_down'])
    head8 = lambda a: jnp.broadcast_to(
        jnp.pad(a.astype(F32).reshape(SSD_GROUPS, 1, 8), ((0, 0), (0, 0), (0, PAD_HEADS - 8))), (SSD_GROUPS, 8, PAD_HEADS))
    by_cols = lambda n: g[n][:, 0].transpose(1, 0, 2).reshape(g[n].shape[2], -1)
    w_in = by_cols('w_in')
    s = np.cumsum([SSD_D_INNER, SSD_CONV_DIM, SSD_HEADS, S5_WIDTH, D_MODEL])
    return dict(
        layer=i, s5=s5, pre_g=row(w['mix_pre_g'][i]), post_g=row(w['mix_post_g'][i]),
        w_big=jnp.concatenate([w_in[:, :s[1]], w_in[:, s[3]:]], axis=1), w_dt=w_in[:, s[1]:s[2]], w_u5=w_in[:, s[2]:s[3]],
        conv_w=by_cols('ssd_conv_w'), conv_b=row(w['ssd_conv_b'][i]),
        dt_bias8=head8(w['ssd_dt_bias'][i]), a_log8=head8(w['ssd_a_log'][i]), d8=head8(w['ssd_d'][i]),
        norm_g=row(w['ssd_norm_g'][i]), late={n: g[n] for n in SUBLAYERS[1][1] if n in LATE},
        lam_re=w['s5_lambda_re'][i], lam_im=w['s5_lambda_im'][i], log_step=w['s5_log_step'][i].reshape(S5_GROUPS, 1),
        b_re=w['s5_b_re'][i].reshape(S5_GROUPS, -1), b_im=w['s5_b_im'][i].reshape(S5_GROUPS, -1),
        c_re=w['s5_c_re'][i], c_im=w['s5_c_im'][i], s5_d=row(w['s5_d'][i]),
    )


def _loss_head(h, target, *, tm):
    t, d = h.shape

    def fn(y, tgt):
        err = y - tgt
        return err * (1.0 / d), jnp.sum(0.5 * jnp.sum(err * err, axis=-1, keepdims=True) * (1.0 / d), axis=0, keepdims=True)

    dy, loss = _rows(fn, name="loss_head", nrow=t // tm,
                     ins=[(h, _rspec(tm, d)), (target, _rspec(tm, d))],
                     outs=[(_sds((t, d)), _rspec(tm, d), False), (_sds((1, 128)), _bspec(128), True)])
    return dy, loss[0, 0]


def _forward_backward(h, target, w, g, rs):
    t = h.shape[0]
    tm = _tile(t, 512, 8)
    s5 = None
    layers, saved = [], []
    for i in range(DEPTH):
        gi, ps, ss = dict(g[i]), [], []
        for tag, names in SUBLAYERS:
            if tag == 'mix' and s5 is None:
                mine = {n: w[n] for n in WEIGHTS if n.startswith('s5_') and n not in SHARDED}
                mine, h = lax.optimization_barrier((mine, h))
                s5 = _s5_prepare(mine)
            early = [n for n in names if n not in LATE]
            tied, h = lax.optimization_barrier(([gi[n] for n in early], h))
            gi.update(zip(early, tied))
            p = _sublayer_params(w, gi, i, tag, s5[i] if tag == 'mix' else None)
            h, s = _mixer_fwd(h, p, tm) if tag == 'mix' else _ffn_fwd(h, p, tag, tm)
            ps.append(p)
            ss.append(s)
        layers.append(ps)
        saved.append(ss)
    dh, loss = _loss_head(h, target, tm=tm)
    reduced, small = [{} for _ in range(DEPTH)], [{} for _ in range(DEPTH)]
    in_sibling, in_chips = None, None

    def start_chips(x):
        nonlocal in_sibling, in_chips
        if in_sibling is not None:
            layer, meta, arrays = in_sibling
            arrays, x = lax.optimization_barrier((arrays, x))
            in_sibling, in_chips = None, (layer,) + tuple(rs.chips(meta, arrays))
        return x

    def finish_chips(x):
        nonlocal in_chips
        if in_chips is not None:
            layer, names, slots = in_chips
            slots, x = lax.optimization_barrier((slots, x))
            reduced[layer].update(rs.done(names, slots))
            in_chips = None
        return x

    for i in reversed(range(DEPTH)):
        for k in reversed(range(len(SUBLAYERS))):
            tag = SUBLAYERS[k][0]
            if tag == 'mix':
                dh, bufs, grads = _mixer_bwd(dh, layers[i][k], saved[i][k], tm, start_chips)
            else:
                dh, bufs, grads = _ffn_bwd(dh, layers[i][k], saved[i][k], tag, tm, start_chips)
            small[i].update(grads)
            dh = finish_chips(dh)
            in_sibling = (i,) + tuple(rs.sibling(tag, bufs))
            if tag == 'mix' and i + 1 < DEPTH:
                small[i + 1], dh = lax.optimization_barrier((small[i + 1], dh))
        if i == 0:
            small[i]['loss'] = loss.reshape(1)
        small[i] = rs.small(small[i])
    loss = small[0].pop('loss')[0]
    dh = finish_chips(start_chips(dh))
    shapes = {n: (w[n].shape[:-1] + (SSD_CONV_DIM,) if n == 'ssd_conv_w' else w[n].shape) for n in SMALL_ORDER}
    stacked = {n: jnp.stack([small[i][n].reshape(shapes[n][1:]) for i in range(DEPTH)]) for n in SMALL_ORDER}
    return loss, dh, reduced, stacked


def kernel(*args):
    n_w = len(WEIGHTS)
    x, target = args[0], args[1 + n_w]
    w = dict(zip(WEIGHTS, args[1:1 + n_w]))
    m = dict(zip(WEIGHTS, args[2 + n_w:2 + 2 * n_w]))
    v = dict(zip(WEIGHTS, args[2 + 2 * n_w:2 + 3 * n_w]))
    t = x.shape[1]

    g = _gather_weights(w)
    loss, dx, slots, small = _forward_backward(x.reshape(t, D_MODEL), target.reshape(t, D_MODEL), w, g, _ReduceScatter)
    me = 4 * lax.axis_index("x") + 2 * lax.axis_index("y") + lax.axis_index("c")
    cols = w['ssd_conv_w'].shape[-1]
    small['ssd_conv_w'] = lax.dynamic_slice_in_dim(small['ssd_conv_w'], me * cols, cols, axis=2)

    grad, delta, new_m, new_v = {}, {}, {}, {}
    for n in WEIGHTS:
        sharded = n in slots[0]
        view = (lambda a: a.transpose(0, 2, 1)) if n in TRANSPOSED else (lambda a: a)
        res = _adamw(view(w[n]), [slots[i][n] for i in range(DEPTH)] if sharded else small[n], view(m[n]), view(v[n]),
                     name=f"adamw_{n}", slots=sharded)
        grad[n], delta[n], new_m[n], new_v[n] = (view(a) for a in res)
    return (loss, dx.reshape(x.shape), *[grad[n] for n in WEIGHTS], *[delta[n] for n in WEIGHTS],
            *[new_m[n] for n in WEIGHTS], *[new_v[n] for n in WEIGHTS])
```

```python
import math

import numpy as np
import jax
import jax.numpy as jnp
from jax import lax
from jax.experimental import pallas as pl
from jax.experimental.pallas import tpu as pltpu
from jax.experimental.pallas import tpu_sc as plsc

F32 = jnp.float32
BF16 = jnp.bfloat16
MESH = pl.DeviceIdType.MESH
HIGHEST = lax.Precision.HIGHEST

D_MODEL = 1024
DEPTH = 2
FFN_HIDDEN = 2816
SSD_D_INNER = 2048
SSD_HEADS = 32
SSD_HEAD_DIM = 64
SSD_GROUPS = 4
SSD_STATE = 128
SSD_CHUNK = 128
SSD_CONV_DIM = 3072
SSD_CONV_WIDTH = 4
S5_WIDTH = 1024
S5_GROUP = 16
S5_GROUPS = 64
S5_STATE = 64
S5_MAX_REAL = -1e-4
S5_BLOCKS = 8
RMS_EPS = 1e-6
N_DEV = 8
LANES = 1024

ADAM_LR = 0.001
ADAM_B1 = 0.9
ADAM_B2 = 0.999
ADAM_EPS = 1e-08
ADAM_WD = 0.01
ADAM_STEP = 10

VMEM_LIMIT_BYTES = 48 * 1024 * 1024

WEIGHTS = ['ffn1_pre_g', 'ffn1_post_g', 'ffn1_w_gate', 'ffn1_w_up', 'ffn1_w_down', 'mix_pre_g', 'mix_post_g',
           'w_in', 'ssd_conv_w', 'ssd_conv_b', 'ssd_dt_bias', 'ssd_a_log', 'ssd_d', 'ssd_norm_g', 'w_branch_a',
           's5_lambda_re', 's5_lambda_im', 's5_b_re', 's5_b_im', 's5_c_re', 's5_c_im', 's5_log_step', 's5_d',
           's5_w_glu', 'w_branch_b', 'w_out', 'ffn2_pre_g', 'ffn2_post_g', 'ffn2_w_gate', 'ffn2_w_up',
           'ffn2_w_down']
SHARDED = {'ffn1_w_gate': 2, 'ffn1_w_up': 2, 'ffn1_w_down': 1, 'w_in': 2, 'ssd_conv_w': 2, 'w_branch_a': 1,
           's5_w_glu': 2, 'w_branch_b': 1, 'w_out': 1, 'ffn2_w_gate': 2, 'ffn2_w_up': 2, 'ffn2_w_down': 1}
SHARDED_ORDER = [n for n in WEIGHTS if n in SHARDED]
SMALL_ORDER = [n for n in WEIGHTS if n not in SHARDED or n == 'ssd_conv_w']


def _pcall(body, **kw):
    return pl.pallas_call(body, **kw)


def _scall(body, *, name, out_type, scratch_types, collective_id):
    return pl.kernel(body, out_type=out_type, mesh=plsc.ScalarSubcoreMesh(axis_name="sequencer", num_cores=1),
                     scratch_types=scratch_types, name=name,
                     compiler_params=pltpu.CompilerParams(collective_id=collective_id))


def _params(*sem):
    return pltpu.CompilerParams(dimension_semantics=sem, vmem_limit_bytes=VMEM_LIMIT_BYTES)


def _tile(n, pref, align=128):
    if n <= pref:
        return n
    t = (pref // align) * align
    while t >= align:
        if n % t == 0:
            return t
        t -= align
    return n


def _rms(x, g):
    return x * lax.rsqrt(jnp.mean(x * x, axis=-1, keepdims=True) + RMS_EPS) * g


def _sigmoid(x):
    return 1.0 / (1.0 + jnp.exp(-x))


def _silu(x):
    return x * _sigmoid(x)


def _gelu(x):
    return 0.5 * x * (1.0 + jnp.tanh(math.sqrt(2.0 / math.pi) * (x + 0.044715 * (x * x * x))))


def _softplus(x):
    return jnp.maximum(x, 0.0) + jnp.log(1.0 + jnp.exp(-jnp.abs(x)))


def _dot(a, b, dims):
    return lax.dot_general(a.astype(BF16), b.astype(BF16), (dims, ((), ())), preferred_element_type=F32)


_NN = ((1,), (0,))
_NT = ((1,), (1,))
_TN = ((0,), (0,))


@jax.custom_vjp
def _bdot_nn(a, b):
    return _dot(a, b, _NN)


_bdot_nn.defvjp(lambda a, b: (_dot(a, b, _NN), (a, b)),
                lambda r, g: (_dot(g, r[1], _NT), _dot(r[0], g, _TN)))


@jax.custom_vjp
def _bdot_nt(a, b):
    return _dot(a, b, _NT)


_bdot_nt.defvjp(lambda a, b: (_dot(a, b, _NT), (a, b)),
                lambda r, g: (_dot(g, r[1], _NN), _dot(g, r[0], _TN)))


@jax.custom_vjp
def _bdot_tn(a, b):
    return _dot(a, b, _TN)


_bdot_tn.defvjp(lambda a, b: (_dot(a, b, _TN), (a, b)),
                lambda r, g: (_dot(r[1], g, _NT), _dot(r[0], g, _NN)))


def _fdot(a, b, dims=_NN):
    return lax.dot_general(a, b, (dims, ((), ())), precision=HIGHEST, preferred_element_type=F32)


def _sel3(x, sel, dims, x_first):
    p1 = x.astype(BF16)
    r1 = x - p1.astype(F32)
    p2 = r1.astype(BF16)
    p3 = (r1 - p2.astype(F32)).astype(BF16)
    sel = sel.astype(BF16)
    out = None
    for piece in (p1, p2, p3):
        d = lax.dot_general(*((piece, sel) if x_first else (sel, piece)), (dims, ((), ())), preferred_element_type=F32)
        out = d if out is None else out + d
    return out


@jax.custom_vjp
def _sel_right(x, sel):
    return _sel3(x, sel, _NN, True)


_sel_right.defvjp(lambda x, sel: (_sel3(x, sel, _NN, True), sel),
                  lambda sel, g: (_sel3(g, sel, _NT, True), jnp.zeros_like(sel)))


@jax.custom_vjp
def _sel_left(sel, x):
    return _sel3(x, sel, _NN, False)


_sel_left.defvjp(lambda sel, x: (_sel3(x, sel, _NN, False), sel),
                 lambda sel, g: (jnp.zeros_like(sel), _sel3(g, sel, _TN, False)))


@jax.custom_vjp
def _sel_left_nt(sel, x):
    return _sel3(x, sel, _NT, False)


_sel_left_nt.defvjp(lambda sel, x: (_sel3(x, sel, _NT, False), sel),
                    lambda sel, g: (jnp.zeros_like(sel), _sel3(g, sel, _TN, True)))


def _mm(a, b, *, name, ta=False, tb=False, out_dtype=F32, tm=2048, tn=512, tk=2048, col_shards=False):
    m, k = (a.shape[1], a.shape[0]) if ta else a.shape
    n = b.shape[0] if tb else b.shape[1]
    assert k == (b.shape[1] if tb else b.shape[0]), (a.shape, b.shape, ta, tb)
    if col_shards:
        tn = n // N_DEV
    tm, tn, tk = _tile(m, tm), _tile(n, tn), _tile(k, tk)
    nk = k // tk
    a_spec = pl.BlockSpec((tk, tm), lambda i, j, kk: (kk, i)) if ta else pl.BlockSpec((tm, tk), lambda i, j, kk: (i, kk))
    b_spec = pl.BlockSpec((tn, tk), lambda i, j, kk: (j, kk)) if tb else pl.BlockSpec((tk, tn), lambda i, j, kk: (kk, j))
    dims = ((0 if ta else 1,), (1 if tb else 0,))
    out_spec = pl.BlockSpec((tm, tn), lambda i, j, kk: (i, j))
    out_shape = jax.ShapeDtypeStruct((m, n), out_dtype)
    if col_shards:
        out_shape = jax.ShapeDtypeStruct((N_DEV, 1, m, n // N_DEV), out_dtype)
        out_spec = pl.BlockSpec((None, None, tm, tn), lambda i, j, kk: (j, 0, i, 0))

    def body(a_ref, b_ref, o_ref, acc_ref):
        kk = pl.program_id(2)

        @pl.when(kk == 0)
        def _():
            acc_ref[...] = jnp.zeros_like(acc_ref)

        acc_ref[...] += _dot(a_ref[...], b_ref[...], dims)

        @pl.when(kk == nk - 1)
        def _():
            o_ref[...] = acc_ref[...].astype(o_ref.dtype)

    return _pcall(
        body, name=name, grid=(m // tm, n // tn, nk),
        in_specs=[a_spec, b_spec], out_specs=out_spec, out_shape=out_shape,
        scratch_shapes=[pltpu.VMEM((tm, tn), F32)],
        compiler_params=_params("parallel", "parallel", "arbitrary"),
    )(a, b)


def _rspec(tm, w, cb=0, percol=False):
    return pl.BlockSpec((tm, w), (lambda j, i: (i, cb + j)) if percol else (lambda j, i: (i, cb)))


def _bspec(w, cb=0, percol=False, rows=1):
    return pl.BlockSpec((rows, w), (lambda j, i: (0, cb + j)) if percol else (lambda j, i: (0, cb)))


def _rows(fn, *, name, nrow, ncol=1, ins, outs):
    n_in = len(ins)
    accs = [o[2] for o in outs]

    def body(*refs):
        vals = fn(*[r[...] for r in refs[:n_in]])
        if not isinstance(vals, (tuple, list)):
            vals = (vals,)
        i = pl.program_id(1)
        for ref, val, acc in zip(refs[n_in:], vals, accs):
            if acc:
                @pl.when(i == 0)
                def _(ref=ref):
                    ref[...] = jnp.zeros_like(ref)

                ref[...] += jnp.broadcast_to(val, ref.shape).astype(ref.dtype)
            else:
                ref[...] = val.astype(ref.dtype)

    res = _pcall(
        body, name=name, grid=(ncol, nrow),
        in_specs=[s for _, s in ins], out_specs=[o[1] for o in outs], out_shape=[o[0] for o in outs],
        compiler_params=_params("parallel", "arbitrary"),
    )(*[a for a, _ in ins])
    return res


def _sds(shape, dtype=F32):
    return jax.ShapeDtypeStruct(shape, dtype)


def _rms_fwd(h, g, *, name, tm):
    t, d = h.shape
    return _rows(lambda x, gg: _rms(x, gg), name=name, nrow=t // tm,
                 ins=[(h, _rspec(tm, d)), (g, _bspec(d))],
                 outs=[(_sds((t, d), BF16), _rspec(tm, d), False)])[0]


def _resid_fwd(h, f, g, scale, *, name, tm):
    t, d = h.shape
    return _rows(lambda x, ff, gg: x + scale * _rms(ff, gg), name=name, nrow=t // tm,
                 ins=[(h, _rspec(tm, d)), (f, _rspec(tm, d)), (g, _bspec(d))],
                 outs=[(_sds((t, d)), _rspec(tm, d), False)])[0]


def _resid_bwd(f, g, dh, scale, *, name, tm):
    t, d = f.shape

    def fn(ff, gg, dd):
        _, vjp = jax.vjp(lambda a, b: scale * _rms(a, b), ff, gg)
        return vjp(dd)

    return _rows(fn, name=name, nrow=t // tm,
                 ins=[(f, _rspec(tm, d)), (g, _bspec(d)), (dh, _rspec(tm, d))],
                 outs=[(_sds((t, d), BF16), _rspec(tm, d), False), (_sds((1, d)), _bspec(d), True)])


def _rms_bwd(h, g, dh, dxns, *, name, tm):
    t, d = h.shape

    def fn(x, gg, dd, *dx):
        _, vjp = jax.vjp(_rms, x, gg)
        tot = dx[0]
        for more in dx[1:]:
            tot = tot + more
        dxx, dg = vjp(tot)
        return dd + dxx, dg

    return _rows(fn, name=name, nrow=t // tm,
                 ins=[(h, _rspec(tm, d)), (g, _bspec(d)), (dh, _rspec(tm, d))] + [(x, _rspec(tm, d)) for x in dxns],
                 outs=[(_sds((t, d)), _rspec(tm, d), False), (_sds((1, d)), _bspec(d), True)])


FFN_BLOCKS = 4
NB = FFN_HIDDEN // FFN_BLOCKS
MM_ROWS = 2048


def _ffn_up(xn, wg, wu, *, name):
    t = xn.shape[0]
    tm = _tile(t, MM_ROWS // 2)
    wspec = pl.BlockSpec((None, None, NB, D_MODEL), lambda i, j: (j, 0, 0, 0))

    def body(x_ref, g_ref, u_ref, ab_ref, hh_ref):
        x = x_ref[...]
        a, b = _dot(x, g_ref[...], _NT), _dot(x, u_ref[...], _NT)
        ab_ref[0] = a.astype(ab_ref.dtype)
        ab_ref[1] = b.astype(ab_ref.dtype)
        hh_ref[...] = (_silu(a) * b).astype(hh_ref.dtype)

    return _pcall(
        body, name=name, grid=(t // tm, FFN_BLOCKS),
        in_specs=[pl.BlockSpec((tm, D_MODEL), lambda i, j: (i, 0)), wspec, wspec],
        out_specs=[pl.BlockSpec((None, 2, tm, NB), lambda i, j: (j, 0, i, 0)),
                   pl.BlockSpec((None, tm, NB), lambda i, j: (j, i, 0))],
        out_shape=[_sds((FFN_BLOCKS, 2, t, NB), BF16), _sds((FFN_BLOCKS, t, NB), BF16)],
        compiler_params=_params("parallel", "parallel"),
    )(xn, wg, wu)


def _ffn_down(hh, wd, *, name):
    t = hh.shape[1]
    tm = _tile(t, 512)

    def body(h_ref, w_ref, o_ref):
        acc = _dot(h_ref[0], w_ref[0, 0], _NN)
        for k in range(1, FFN_BLOCKS):
            acc = acc + _dot(h_ref[k], w_ref[k, 0], _NN)
        o_ref[...] = acc

    return _pcall(
        body, name=name, grid=(t // tm,),
        in_specs=[pl.BlockSpec((FFN_BLOCKS, tm, NB), lambda i: (0, i, 0)),
                  pl.BlockSpec((FFN_BLOCKS, 1, NB, D_MODEL), lambda i: (0, 0, 0, 0))],
        out_specs=pl.BlockSpec((tm, D_MODEL), lambda i: (i, 0)), out_shape=_sds((t, D_MODEL)),
        compiler_params=_params("parallel"),
    )(hh, wd)


def _ffn_down_dx(df, wd, ab, *, name):
    t = df.shape[0]
    tm = _tile(t, MM_ROWS // 2)

    def body(d_ref, w_ref, ab_ref, o_ref):
        dhh = _dot(d_ref[...], w_ref[...], _NT)
        _, vjp = jax.vjp(lambda a, b: _silu(a) * b, ab_ref[0].astype(F32), ab_ref[1].astype(F32))
        da, db = vjp(dhh)
        o_ref[0] = da.astype(o_ref.dtype)
        o_ref[1] = db.astype(o_ref.dtype)

    blk = pl.BlockSpec((None, 2, tm, NB), lambda i, j: (j, 0, i, 0))
    return _pcall(
        body, name=name, grid=(t // tm, FFN_BLOCKS),
        in_specs=[pl.BlockSpec((tm, D_MODEL), lambda i, j: (i, 0)),
                  pl.BlockSpec((None, None, NB, D_MODEL), lambda i, j: (j, 0, 0, 0)), blk],
        out_specs=blk, out_shape=_sds((FFN_BLOCKS, 2, t, NB), BF16), compiler_params=_params("parallel", "parallel"),
    )(df, wd, ab)


def _ffn_down_dw(hh, df, *, name, tn=512):
    t = df.shape[0]
    tk = _tile(t, 2048)
    nk = t // tk

    def body(h_ref, d_ref, o_ref, acc_ref):
        kk = pl.program_id(2)

        @pl.when(kk == 0)
        def _():
            acc_ref[...] = jnp.zeros_like(acc_ref)

        acc_ref[...] += _dot(h_ref[...], d_ref[...], _TN)

        @pl.when(kk == nk - 1)
        def _():
            o_ref[...] = acc_ref[...].astype(o_ref.dtype)

    return _pcall(
        body, name=name, grid=(FFN_BLOCKS, D_MODEL // tn, nk),
        in_specs=[pl.BlockSpec((None, tk, NB), lambda j, n, kk: (j, kk, 0)),
                  pl.BlockSpec((tk, tn), lambda j, n, kk: (kk, n))],
        out_specs=pl.BlockSpec((None, None, NB, tn), lambda j, n, kk: (j, 0, 0, n)),
        out_shape=_sds((FFN_BLOCKS, 1, NB, D_MODEL), BF16),
        scratch_shapes=[pltpu.VMEM((NB, tn), F32)],
        compiler_params=_params("parallel", "parallel", "arbitrary"),
    )(hh, df)


def _ffn_up_dx(dab, wg, wu, *, name):
    t = dab.shape[2]
    tm = _tile(t, MM_ROWS // 2)
    wspec = pl.BlockSpec((None, None, NB, D_MODEL), lambda i, j: (j, 0, 0, 0))

    def body(d_ref, g_ref, u_ref, o_ref):
        @pl.when(pl.program_id(1) == 0)
        def _():
            o_ref[...] = jnp.zeros_like(o_ref)

        o_ref[...] += _dot(d_ref[0], g_ref[...], _NN) + _dot(d_ref[1], u_ref[...], _NN)

    return _pcall(
        body, name=name, grid=(t // tm, FFN_BLOCKS),
        in_specs=[pl.BlockSpec((None, 2, tm, NB), lambda i, j: (j, 0, i, 0)), wspec, wspec],
        out_specs=pl.BlockSpec((tm, D_MODEL), lambda i, j: (i, 0)), out_shape=_sds((t, D_MODEL)),
        compiler_params=_params("parallel", "arbitrary"),
    )(dab, wg, wu)


def _ffn_up_dw(xn, dab, *, name):
    t = xn.shape[0]

    def body(x_ref, d_ref, og_ref, ou_ref):
        x = x_ref[...]
        og_ref[...] = _dot(d_ref[0], x, _TN).astype(og_ref.dtype)
        ou_ref[...] = _dot(d_ref[1], x, _TN).astype(ou_ref.dtype)

    out = pl.BlockSpec((None, None, NB, D_MODEL), lambda j: (j, 0, 0, 0))
    return _pcall(
        body, name=name, grid=(FFN_BLOCKS,),
        in_specs=[pl.BlockSpec((t, D_MODEL), lambda j: (0, 0)), pl.BlockSpec((None, 2, t, NB), lambda j: (j, 0, 0, 0))],
        out_specs=[out, out], out_shape=[_sds((FFN_BLOCKS, 1, NB, D_MODEL), BF16)] * 2,
        compiler_params=_params("parallel"),
    )(xn, dab)


def _paired(a):
    return a.reshape(FFN_BLOCKS, 1, NB, D_MODEL)


def _ffn_fwd(h, p, tag, tm):
    xn = _rms_fwd(h, p['pre_g'], name=f"{tag}_rms", tm=tm)
    ab, hh = _ffn_up(xn, _paired(p['w_gate']), _paired(p['w_up']), name=f"{tag}_up")
    w_down, hh = lax.optimization_barrier((p['w_down'], hh))
    f = _ffn_down(hh, _paired(w_down), name=f"{tag}_down")
    out = _resid_fwd(h, f, p['post_g'], 0.5, name=f"{tag}_res", tm=tm)
    return out, (h, xn, ab, hh, f)


def _ffn_bwd(dh, p, saved, tag, tm, after_first):
    h, xn, ab, hh, f = saved
    df, dpost = _resid_bwd(f, p['post_g'], dh, 0.5, name=f"{tag}_res_bwd", tm=tm)
    df = after_first(df)
    dab = _ffn_down_dx(df, _paired(p['w_down']), ab, name=f"{tag}_down_dx")
    bufs = {f'{tag}_w_down': _ffn_down_dw(hh, df, name=f"{tag}_down_dw")}
    dxn = _ffn_up_dx(dab, _paired(p['w_gate']), _paired(p['w_up']), name=f"{tag}_up_dx")
    bufs[f'{tag}_w_gate'], bufs[f'{tag}_w_up'] = _ffn_up_dw(xn, dab, name=f"{tag}_up_dw")
    bufs = {n: a.reshape(N_DEV, 1, FFN_HIDDEN // N_DEV, D_MODEL) for n, a in bufs.items()}
    dh_in, dpre = _rms_bwd(h, p['pre_g'], dh, [dxn], name=f"{tag}_rms_bwd", tm=tm)
    return dh_in, bufs, {f'{tag}_pre_g': dpre, f'{tag}_post_g': dpost}


CONV_COLS = 256


def _shift_down(x, s):
    rows = lax.broadcasted_iota(jnp.int32, x.shape, 0)
    return jnp.where(rows >= s, pltpu.roll(x, s, axis=0), 0.0)


def _shift_up(x, s):
    t = x.shape[0]
    rows = lax.broadcasted_iota(jnp.int32, x.shape, 0)
    return jnp.where(rows < t - s, pltpu.roll(x, t - s, axis=0), 0.0)


def _conv_fwd(proj, col0, w, b, *, name):
    t = proj.shape[0]
    c = w.shape[1]
    cb0 = col0 // CONV_COLS

    def body(x_ref, w_ref, b_ref, o_ref):
        x = x_ref[...]
        acc = x * w_ref[3:4, :] + b_ref[...]
        for k in range(SSD_CONV_WIDTH - 1):
            acc = acc + _shift_down(x, SSD_CONV_WIDTH - 1 - k) * w_ref[k:k + 1, :]
        o_ref[...] = _silu(acc)

    return _pcall(
        body, name=name, grid=(c // CONV_COLS,),
        in_specs=[pl.BlockSpec((t, CONV_COLS), lambda j: (0, cb0 + j)),
                  pl.BlockSpec((SSD_CONV_WIDTH, CONV_COLS), lambda j: (0, j)),
                  pl.BlockSpec((1, CONV_COLS), lambda j: (0, j))],
        out_specs=pl.BlockSpec((t, CONV_COLS), lambda j: (0, j)),
        out_shape=_sds((t, c)), compiler_params=_params("parallel"),
    )(proj, w, b)


def _conv_bwd(proj, col0, w, b, douts, *, name):
    t = proj.shape[0]
    c = w.shape[1]
    cb0 = col0 // CONV_COLS
    first = np.cumsum([0] + [d.shape[1] // CONV_COLS for d in douts])

    def body(x_ref, w_ref, b_ref, *rest):
        d_refs, (dx_ref, dw_ref, db_ref) = rest[:len(douts)], rest[len(douts):]
        j = pl.program_id(0)
        dout = d_refs[-1][...]
        for k in range(len(douts) - 2, -1, -1):
            dout = jnp.where(j < int(first[k + 1]), d_refs[k][...], dout)
        x = x_ref[...]
        shifted = [_shift_down(x, SSD_CONV_WIDTH - 1 - k) for k in range(SSD_CONV_WIDTH - 1)] + [x]
        pre = b_ref[...] + shifted[3] * w_ref[3:4, :]
        for k in range(SSD_CONV_WIDTH - 1):
            pre = pre + shifted[k] * w_ref[k:k + 1, :]
        sg = _sigmoid(pre)
        dpre = dout * (sg * (1.0 + pre * (1.0 - sg)))
        dx = dpre * w_ref[3:4, :]
        for k in range(SSD_CONV_WIDTH - 1):
            dx = dx + _shift_up(dpre, SSD_CONV_WIDTH - 1 - k) * w_ref[k:k + 1, :]
        dx_ref[...] = dx.astype(dx_ref.dtype)
        for k in range(SSD_CONV_WIDTH):
            dw_ref[k:k + 1, :] = jnp.sum(dpre * shifted[k], axis=0, keepdims=True)
        db_ref[...] = jnp.sum(dpre, axis=0, keepdims=True)

    return _pcall(
        body, name=name, grid=(c // CONV_COLS,),
        in_specs=[pl.BlockSpec((t, CONV_COLS), lambda j: (0, cb0 + j)),
                  pl.BlockSpec((SSD_CONV_WIDTH, CONV_COLS), lambda j: (0, j)),
                  pl.BlockSpec((1, CONV_COLS), lambda j: (0, j))] +
                 [pl.BlockSpec((t, CONV_COLS), lambda j, lo=int(first[k]), hi=int(first[k + 1]): (0, jnp.clip(j, lo, hi - 1) - lo))
                  for k in range(len(douts))],
        out_specs=[pl.BlockSpec((t, CONV_COLS), lambda j: (0, j)),
                   pl.BlockSpec((SSD_CONV_WIDTH, CONV_COLS), lambda j: (0, j)),
                   pl.BlockSpec((1, CONV_COLS), lambda j: (0, j))],
        out_shape=[_sds((t, c), BF16), _sds((SSD_CONV_WIDTH, c)), _sds((1, c))],
        compiler_params=_params("arbitrary"),
    )(proj, w, b, *douts)


HALF = 256
HEADS_PER_HALF = 4
PAD_HEADS = 128


def _head_expanders():
    k = lax.broadcasted_iota(jnp.int32, (PAD_HEADS, HALF), 0)
    j = lax.broadcasted_iota(jnp.int32, (PAD_HEADS, HALF), 1)
    kt = lax.broadcasted_iota(jnp.int32, (HALF, PAD_HEADS), 1)
    jt = lax.broadcasted_iota(jnp.int32, (HALF, PAD_HEADS), 0)
    es, ets = [], []
    for half in range(2):
        es.append(jnp.where(k == j // SSD_HEAD_DIM + half * HEADS_PER_HALF, 1.0, 0.0).astype(F32))
        ets.append(jnp.where(kt == jt // SSD_HEAD_DIM + half * HEADS_PER_HALF, 1.0, 0.0).astype(F32))
    return es, ets


def _ssd_chunk(x_lo, x_hi, bm, cm, dtr, dtb8, alog8, dsk8, s_lo, s_hi):
    q = x_lo.shape[0]
    es, ets = _head_expanders()
    rowmean = lambda v: jnp.sum(v, axis=0, keepdims=True) * 0.125
    dt = _softplus(dtr + rowmean(dtb8))
    a = -jnp.exp(rowmean(alog8))
    adt = a * dt
    adt_tot8 = jnp.broadcast_to(jnp.sum(adt, axis=0, keepdims=True), (8, PAD_HEADS))
    ll = lax.broadcasted_iota(jnp.int32, (q, q), 0)
    ss = lax.broadcasted_iota(jnp.int32, (q, q), 1)
    ltri = jnp.where(ll >= ss, 1.0, 0.0).astype(F32)
    lane = lax.broadcasted_iota(jnp.int32, (1, HALF), 1)
    cb = _bdot_nt(cm, bm)
    outs = []
    for half, (x, s_in) in enumerate(((x_lo, s_lo), (x_hi, s_hi))):
        e, et = es[half], ets[half]
        dtf = _sel_right(dt, e)
        af = rowmean(_sel_right(jnp.broadcast_to(a, (8, PAD_HEADS)), e)) * dtf
        dskf = rowmean(_sel_right(dsk8, e))
        acum = _sel_left(ltri, af)
        alast = jnp.sum(af, axis=0, keepdims=True)
        xdt = x * dtf
        ydiag = jnp.zeros((q, HALF), F32)
        for r in range(HEADS_PER_HALF):
            sel = lane == r * SSD_HEAD_DIM
            ac_r = jnp.sum(jnp.where(sel, acum, 0.0), axis=1, keepdims=True)
            a_r = jnp.sum(jnp.where(sel, af, 0.0), axis=1, keepdims=True)
            arow = jnp.sum(jnp.where(ll <= ss, a_r, 0.0), axis=0, keepdims=True)
            decay = jnp.exp(jnp.where(ll >= ss, ac_r - arow, -jnp.inf))
            yh = _bdot_nn(cb * decay, xdt)
            ydiag = ydiag + jnp.where(lane // SSD_HEAD_DIM == r, yh, 0.0)
        st = _bdot_tn(xdt * jnp.exp(alast - acum), bm)
        yoff = _bdot_nt(cm, s_in) * jnp.exp(acum)
        y = ydiag + yoff + dskf * x
        alast_col = jnp.sum(_sel_left_nt(et, adt_tot8), axis=1, keepdims=True) * 0.125
        outs.append((y, jnp.exp(alast_col) * s_in + st))
    return outs[0][0], outs[1][0], outs[0][1], outs[1][1]


SSD_GP = 2


def _ssd_specs(t, rev):
    q, n = SSD_CHUNK, SSD_GP
    nc = t // q
    ci = (lambda c: nc - 1 - c) if rev else (lambda c: c)
    bcol0 = SSD_D_INNER // (n * SSD_STATE)
    return dict(
        x=pl.BlockSpec((q, n * 2 * HALF), lambda g, c: (ci(c), g)),
        bm=pl.BlockSpec((q, n * SSD_STATE), lambda g, c: (ci(c), bcol0 + g)),
        cm=pl.BlockSpec((q, n * SSD_STATE), lambda g, c: (ci(c), bcol0 + SSD_GROUPS // n + g)),
        dt=pl.BlockSpec((n, q, PAD_HEADS), lambda g, c: (g, ci(c), 0)),
        par=pl.BlockSpec((n, 8, PAD_HEADS), lambda g, c: (g, 0, 0)),
        st=pl.BlockSpec((None, n, 2, HALF, SSD_STATE), lambda g, c: (ci(c), g, 0, 0, 0)),
        grp=pl.BlockSpec((q, n * SSD_STATE), lambda g, c: (ci(c), g)),
    )


def _group_cols(k):
    lo = k * 2 * HALF
    return slice(lo, lo + HALF), slice(lo + HALF, lo + 2 * HALF), slice(k * SSD_STATE, (k + 1) * SSD_STATE)


def _ssd_fwd(xc, dt4, dtb, alog, dsk, *, name):
    t = xc.shape[0]
    nc = t // SSD_CHUNK
    sp = _ssd_specs(t, False)

    def body(x, bm, cm, dt, p0, p1, p2, y_ref, sin_ref, st_ref):
        @pl.when(pl.program_id(1) == 0)
        def _():
            st_ref[...] = jnp.zeros_like(st_ref)

        sin_ref[...] = st_ref[...]
        for k in range(SSD_GP):
            lo, hi, bc = _group_cols(k)
            y_lo, y_hi, so_lo, so_hi = _ssd_chunk(x[:, lo], x[:, hi], bm[:, bc], cm[:, bc], dt[k], p0[k], p1[k], p2[k],
                                                  st_ref[k, 0], st_ref[k, 1])
            y_ref[:, lo] = y_lo
            y_ref[:, hi] = y_hi
            st_ref[k, 0] = so_lo
            st_ref[k, 1] = so_hi

    return _pcall(
        body, name=name, grid=(SSD_GROUPS // SSD_GP, nc),
        in_specs=[sp['x'], sp['bm'], sp['cm'], sp['dt'], sp['par'], sp['par'], sp['par']],
        out_specs=[sp['x'], sp['st']],
        out_shape=[_sds((t, SSD_D_INNER)), _sds((nc, SSD_GROUPS, 2, HALF, SSD_STATE))],
        scratch_shapes=[pltpu.VMEM((SSD_GP, 2, HALF, SSD_STATE), F32)],
        compiler_params=_params("parallel", "arbitrary"),
    )(xc, xc, xc, dt4, dtb, alog, dsk)


def _ssd_bwd(xc, dt4, dtb, alog, dsk, sin, dy, *, name):
    t = xc.shape[0]
    nc = t // SSD_CHUNK
    sp = _ssd_specs(t, True)

    def body(x, bm, cm, dt, p0, p1, p2, sin_ref, dy_ref, dx_ref, db_ref, dc_ref, ddt_ref, dp0, dp1, dp2, dst_ref):
        @pl.when(pl.program_id(1) == 0)
        def _():
            dst_ref[...] = jnp.zeros_like(dst_ref)
            for ref in (dp0, dp1, dp2):
                ref[...] = jnp.zeros_like(ref)

        for k in range(SSD_GP):
            lo, hi, bc = _group_cols(k)
            _, vjp = jax.vjp(_ssd_chunk, x[:, lo], x[:, hi], bm[:, bc], cm[:, bc], dt[k], p0[k], p1[k], p2[k],
                             sin_ref[k, 0], sin_ref[k, 1])
            dxl, dxh, dbm, dcm, ddt, g0, g1, g2, ds_lo, ds_hi = vjp(
                (dy_ref[:, lo], dy_ref[:, hi], dst_ref[k, 0], dst_ref[k, 1]))
            dx_ref[:, lo] = dxl
            dx_ref[:, hi] = dxh
            db_ref[:, bc] = dbm
            dc_ref[:, bc] = dcm
            ddt_ref[k] = ddt
            dst_ref[k, 0] = ds_lo
            dst_ref[k, 1] = ds_hi
            for ref, g in ((dp0, g0), (dp1, g1), (dp2, g2)):
                ref[k] += jnp.broadcast_to(jnp.sum(g, axis=0, keepdims=True), g.shape)

    return _pcall(
        body, name=name, grid=(SSD_GROUPS // SSD_GP, nc),
        in_specs=[sp['x'], sp['bm'], sp['cm'], sp['dt'], sp['par'], sp['par'], sp['par'], sp['st'], sp['x']],
        out_specs=[sp['x'], sp['grp'], sp['grp'], sp['dt'], sp['par'], sp['par'], sp['par']],
        out_shape=[_sds((t, SSD_D_INNER)), _sds((t, SSD_GROUPS * SSD_STATE)), _sds((t, SSD_GROUPS * SSD_STATE)),
                   _sds((SSD_GROUPS, t, PAD_HEADS))] + [_sds((SSD_GROUPS, 8, PAD_HEADS))] * 3,
        scratch_shapes=[pltpu.VMEM((SSD_GP, 2, HALF, SSD_STATE), F32)],
        compiler_params=_params("parallel", "arbitrary"),
    )(xc, xc, xc, dt4, dtb, alog, dsk, sin, dy)


def _gatenorm(y, z, g):
    v = y * _silu(z)
    return v * lax.rsqrt(jnp.mean(v * v, axis=-1, keepdims=True) + RMS_EPS) * g


S5_CH = S5_WIDTH // S5_BLOCKS
S5_ST = S5_CH * S5_STATE // S5_GROUP
SCAN_UNROLL = 8


def _cmul(ar, ai, br, bi):
    return ar * br - ai * bi, ar * bi + ai * br


def _segment_power(ar, ai, n):
    assert n & (n - 1) == 0
    for _ in range(n.bit_length() - 1):
        ar, ai = _cmul(ar, ai, ar, ai)
    return ar, ai


def _carry_in(fr, fi, pr, pi, reverse):
    rows = lax.broadcasted_iota(jnp.int32, fr.shape, 0)
    cr = jnp.zeros_like(fr[0:1])
    ci = jnp.zeros_like(cr)
    outr = jnp.zeros_like(fr)
    outi = jnp.zeros_like(fr)
    order = range(6, -1, -1) if reverse else range(1, 8)
    for j in order:
        src = j + 1 if reverse else j - 1
        nr, ni = _cmul(pr[0:1], pi[0:1], cr, ci)
        cr, ci = nr + fr[src:src + 1], ni + fi[src:src + 1]
        outr = jnp.where(rows == j, cr, outr)
        outi = jnp.where(rows == j, ci, outi)
    return outr, outi


def _s5_specs(t):
    return dict(ch=pl.BlockSpec((t, S5_CH), lambda j: (0, j)), st=pl.BlockSpec((t, S5_ST), lambda j: (0, j)),
                lam=pl.BlockSpec((1, S5_ST), lambda j: (0, j)),
                b=pl.BlockSpec((None, S5_CH, S5_ST), lambda j: (j, 0, 0)),
                c=pl.BlockSpec((None, S5_ST, S5_CH), lambda j: (j, 0, 0)))


def _s5_fwd(u5, bre, bim, cre, cim, lr, li, *, name):
    t = u5.shape[0]
    nrt = t // 8

    def body(u_ref, bre_ref, bim_ref, cre_ref, cim_ref, lr_ref, li_ref, sr_ref, si_ref, y_ref, br_ref, bi_ref):
        u = u_ref[...]
        br_ref[...] = _dot(u, bre_ref[...], _NN)
        bi_ref[...] = _dot(u, bim_ref[...], _NN)
        ar = jnp.broadcast_to(lr_ref[...], (8, S5_ST))
        ai = jnp.broadcast_to(li_ref[...], (8, S5_ST))

        def step(r, s, store):
            rows = pl.ds(pl.multiple_of(r * 8, 8), 8)
            nr, ni = _cmul(ar, ai, s[0], s[1])
            nr, ni = nr + br_ref[rows, :], ni + bi_ref[rows, :]
            if store:
                sr_ref[rows, :] = nr
                si_ref[rows, :] = ni
            return nr, ni

        zero = (jnp.zeros((8, S5_ST), F32), jnp.zeros((8, S5_ST), F32))
        fr, fi = lax.fori_loop(0, nrt, lambda r, s: step(r, s, False), zero, unroll=SCAN_UNROLL)
        pr, pi = _segment_power(ar, ai, nrt)
        init = _carry_in(fr, fi, pr, pi, False)
        lax.fori_loop(0, nrt, lambda r, s: step(r, s, True), init, unroll=SCAN_UNROLL)
        y_ref[...] = _dot(sr_ref[...], cre_ref[...], _NN) - _dot(si_ref[...], cim_ref[...], _NN)

    sp = _s5_specs(t)
    w = S5_BLOCKS * S5_ST
    return _pcall(
        body, name=name, grid=(S5_BLOCKS,),
        in_specs=[sp['ch'], sp['b'], sp['b'], sp['c'], sp['c'], sp['lam'], sp['lam']],
        out_specs=[sp['st'], sp['st'], sp['ch']], out_shape=[_sds((t, w)), _sds((t, w)), _sds((t, S5_WIDTH))],
        scratch_shapes=[pltpu.VMEM((t, S5_ST), F32)] * 2, compiler_params=_params("parallel"),
    )(u5, bre, bim, cre, cim, lr, li)


def _s5_bwd(dy, du_direct, u5, sr, si, bre, bim, cre, cim, lr, li, *, name):
    t = u5.shape[0]
    nrt = t // 8

    def body(dy_ref, dd_ref, u_ref, sr_ref, si_ref, bre_ref, bim_ref, cre_ref, cim_ref, lr_ref, li_ref,
             du_ref, dbre_ref, dbim_ref, dcre_ref, dcim_ref, dlr_ref, dli_ref, gr_ref, gi_ref):
        dyv = dy_ref[...]
        gr_ref[...] = _dot(dyv, cre_ref[...], _NT)
        gi_ref[...] = -_dot(dyv, cim_ref[...], _NT)
        dcre_ref[...] = _dot(sr_ref[...], dyv, _TN)
        dcim_ref[...] = -_dot(si_ref[...], dyv, _TN)
        dr_ref, di_ref = gr_ref, gi_ref
        ar = jnp.broadcast_to(lr_ref[...], (8, S5_ST))
        ai = -jnp.broadcast_to(li_ref[...], (8, S5_ST))
        zero = jnp.zeros((8, S5_ST), F32)

        def step1(k, g):
            rows = pl.ds(pl.multiple_of((nrt - 1 - k) * 8, 8), 8)
            nr, ni = _cmul(ar, ai, g[0], g[1])
            return nr + dr_ref[rows, :], ni + di_ref[rows, :]

        fr, fi = lax.fori_loop(0, nrt, step1, (zero, zero), unroll=SCAN_UNROLL)
        pr, pi = _segment_power(ar, ai, nrt)
        init = _carry_in(fr, fi, pr, pi, True)

        def step2(k, carry):
            gr, gi, accr, acci = carry
            r = nrt - 1 - k
            rows = pl.ds(pl.multiple_of(r * 8, 8), 8)
            prev = pl.ds(pl.multiple_of(jnp.maximum(r - 1, 0) * 8, 8), 8)
            nr, ni = _cmul(ar, ai, gr, gi)
            nr, ni = nr + dr_ref[rows, :], ni + di_ref[rows, :]
            gr_ref[rows, :] = nr
            gi_ref[rows, :] = ni
            keep = jnp.where(r > 0, 1.0, 0.0)
            pr_, pi_ = sr_ref[prev, :] * keep, si_ref[prev, :] * keep
            return nr, ni, accr + (pr_ * nr + pi_ * ni), acci + (pr_ * ni - pi_ * nr)

        _, _, accr, acci = lax.fori_loop(0, nrt, step2, (init[0], init[1], zero, zero), unroll=SCAN_UNROLL)
        last = pl.ds((nrt - 1) * 8, 8)
        pr_, pi_ = _shift_down(sr_ref[last, :], 1), _shift_down(si_ref[last, :], 1)
        g0r, g0i = gr_ref[0:8, :], gi_ref[0:8, :]
        accr = accr + (pr_ * g0r + pi_ * g0i)
        acci = acci + (pr_ * g0i - pi_ * g0r)
        dlr_ref[...] = jnp.sum(accr, axis=0, keepdims=True)
        dli_ref[...] = jnp.sum(acci, axis=0, keepdims=True)
        u = u_ref[...]
        dbre_ref[...] = _dot(u, gr_ref[...], _TN)
        dbim_ref[...] = _dot(u, gi_ref[...], _TN)
        du = dd_ref[...] + _dot(gr_ref[...], bre_ref[...], _NT) + _dot(gi_ref[...], bim_ref[...], _NT)
        du_ref[...] = du.astype(du_ref.dtype)

    sp = _s5_specs(t)
    w = S5_BLOCKS * S5_ST
    return _pcall(
        body, name=name, grid=(S5_BLOCKS,),
        in_specs=[sp['ch'], sp['ch'], sp['ch'], sp['st'], sp['st'], sp['b'], sp['b'], sp['c'], sp['c'], sp['lam'], sp['lam']],
        out_specs=[sp['ch'], sp['b'], sp['b'], sp['c'], sp['c'], sp['lam'], sp['lam']],
        out_shape=[_sds((t, S5_WIDTH), BF16), _sds((S5_BLOCKS, S5_CH, S5_ST)), _sds((S5_BLOCKS, S5_CH, S5_ST)),
                   _sds((S5_BLOCKS, S5_ST, S5_CH)), _sds((S5_BLOCKS, S5_ST, S5_CH)), _sds((1, w)), _sds((1, w))],
        scratch_shapes=[pltpu.VMEM((t, S5_ST), F32)] * 2, compiler_params=_params("parallel"),
    )(dy, du_direct, u5, sr, si, bre, bim, cre, cim, lr, li)


def _s5_expander():
    n = lax.broadcasted_iota(jnp.int32, (S5_STATE, S5_STATE * S5_GROUP), 0)
    j = lax.broadcasted_iota(jnp.int32, (S5_STATE, S5_STATE * S5_GROUP), 1)
    return jnp.where(n == j // S5_GROUP, 1.0, 0.0).astype(F32)


def _s5_discretise(lam_re, lam_im, log_step, b_re, b_im):
    lr = jnp.minimum(lam_re, S5_MAX_REAL)
    step = jnp.exp(log_step)
    mag = jnp.exp(lr * step)
    ang = lam_im * step
    lbr, lbi = mag * jnp.cos(ang), mag * jnp.sin(ang)
    p, q = lbr - 1.0, lbi
    den = lr * lr + lam_im * lam_im
    cr, ci = (p * lr + q * lam_im) / den, (q * lr - p * lam_im) / den
    e = _s5_expander()
    cre, cie = _fdot(cr, e), _fdot(ci, e)
    return lbr, lbi, cre * b_re - cie * b_im, cre * b_im + cie * b_re


def _s5_params_fwd(lam_re, lam_im, log_step, b_re, b_im, *, name):
    g, n, w = lam_re.shape[0], S5_STATE, S5_STATE * S5_GROUP

    def body(a, b, c, d, e, o0, o1, o2, o3):
        for ref, val in zip((o0, o1, o2, o3), _s5_discretise(a[...], b[...], c[...], d[...], e[...])):
            ref[...] = val

    return _pcall(body, name=name, out_shape=[_sds((g, n)), _sds((g, n)), _sds((g, w)), _sds((g, w))])(
        lam_re, lam_im, log_step, b_re, b_im)


def _s5_params_bwd(lam_re, lam_im, log_step, b_re, b_im, cts, *, name):
    g, n, w = S5_GROUPS, S5_STATE, S5_STATE * S5_GROUP

    def body(a, b, c, d, e, c0, c1, c2, c3, o0, o1, o2, o3, o4):
        _, vjp = jax.vjp(_s5_discretise, a[...], b[...], c[...], d[...], e[...])
        for ref, val in zip((o0, o1, o2, o3, o4), vjp((c0[...], c1[...], c2[...], c3[...]))):
            ref[...] = val

    return _pcall(body, name=name,
                  out_shape=[_sds((g, n)), _sds((g, n)), _sds((g, 1)), _sds((g, w)), _sds((g, w))])(
        lam_re, lam_im, log_step, b_re, b_im, *cts)


def _s5_prepare(w):
    rows = DEPTH * S5_GROUPS
    lbr, lbi, bbr, bbi = _s5_params_fwd(
        w['s5_lambda_re'].reshape(rows, -1), w['s5_lambda_im'].reshape(rows, -1), w['s5_log_step'].reshape(rows, 1),
        w['s5_b_re'].reshape(rows, -1), w['s5_b_im'].reshape(rows, -1), name="s5_par")
    bd = lambda m: _blockdiag(m.reshape(rows, S5_STATE, S5_GROUP).transpose(0, 2, 1), S5_GROUP, S5_STATE).astype(BF16)
    cd = lambda m: _blockdiag(m.reshape(rows, S5_GROUP, S5_STATE).transpose(0, 2, 1), S5_STATE, S5_GROUP).astype(BF16)
    bre, bim, cre, cim = bd(bbr), bd(bbi), cd(w['s5_c_re']), cd(w['s5_c_im'])
    lr, li = lbr.reshape(DEPTH, 1, -1), lbi.reshape(DEPTH, 1, -1)
    blk = lambda a, i: a[i * S5_BLOCKS:(i + 1) * S5_BLOCKS]
    return [dict(bre=blk(bre, i), bim=blk(bim, i), cre=blk(cre, i), cim=blk(cim, i), lr=lr[i], li=li[i])
            for i in range(DEPTH)]


def _perm(a):
    t, c = a.shape
    return a.reshape(8, t // 8, c).transpose(1, 0, 2).reshape(t, c)


def _unperm(a):
    t, c = a.shape
    return a.reshape(t // 8, 8, c).transpose(1, 0, 2).reshape(t, c)


def _blockdiag(m, rows_inner, cols_inner):
    nblk = m.shape[0] // 8
    m = m.reshape(nblk, 8, rows_inner, cols_inner)
    eye = jnp.eye(8, dtype=m.dtype)
    out = m[:, :, :, None, :] * eye[None, :, None, :, None]
    return out.reshape(nblk, 8 * rows_inner, 8 * cols_inner)


def _blockdiag_extract(m, rows_inner, cols_inner):
    m = m.reshape(S5_BLOCKS, 8, rows_inner, 8, cols_inner)
    d = jnp.diagonal(m, axis1=1, axis2=3)
    return d.transpose(0, 3, 1, 2).reshape(S5_GROUPS, rows_inner, cols_inner)


Z0, XBC0, GA0, GB0 = 0, SSD_D_INNER, SSD_D_INNER + SSD_CONV_DIM, SSD_D_INNER + SSD_CONV_DIM + D_MODEL
BIG = GB0 + D_MODEL


def _mixer_fwd(h, p, tm):
    t = h.shape[0]
    nrow = t // tm
    u = _rms_fwd(h, p['pre_g'], name="mix_rms", tm=tm)
    u_p = _perm(u)
    proj = _mm(u, p['w_big'], name="mix_in")
    dtr = _mm(u, p['w_dt'], name="mix_in_dt")
    u5 = _mm(u_p, p['w_u5'], name="mix_in_s5")
    late, proj = lax.optimization_barrier((p['late'], proj))
    by_rows = lambda a: a.reshape(-1, a.shape[-1])
    p = dict(p, w_a=by_rows(late['w_branch_a']), w_b=by_rows(late['w_branch_b']), w_out=by_rows(late['w_out']),
             w_glu=late['s5_w_glu'][:, 0].transpose(1, 0, 2).reshape(late['s5_w_glu'].shape[2], -1))
    xc = _conv_fwd(proj, XBC0, p['conv_w'], p['conv_b'], name="ssd_conv")
    dt4 = jnp.pad(dtr.reshape(t, SSD_GROUPS, 8).transpose(1, 0, 2), ((0, 0), (0, 0), (0, PAD_HEADS - 8)))
    y_ssd, s_in = _ssd_fwd(xc, dt4, p['dt_bias8'], p['a_log8'], p['d8'], name="ssd_scan")
    gw = SSD_D_INNER // SSD_GROUPS
    ya = _rows(_gatenorm, name="ssd_gate", nrow=nrow, ncol=SSD_GROUPS,
               ins=[(y_ssd, _rspec(tm, gw, 0, True)), (proj, _rspec(tm, gw, Z0 // gw, True)),
                    (p['norm_g'], _bspec(gw, 0, True))],
               outs=[(_sds((t, SSD_D_INNER), BF16), _rspec(tm, gw, 0, True), False)])[0]
    y_a = _mm(ya, p['w_a'], name="mix_a")
    bre, bim, cre, cim, lr, li = (p['s5'][k] for k in ('bre', 'bim', 'cre', 'cim', 'lr', 'li'))
    sr, si, y5 = _s5_fwd(u5, bre, bim, cre, cim, lr, li, name="s5_scan")
    y5g = _rows(lambda a, b, d: _gelu(a + d * b), name="s5_act", nrow=nrow,
                ins=[(y5, _rspec(tm, S5_WIDTH)), (u5, _rspec(tm, S5_WIDTH)), (p['s5_d'], _bspec(S5_WIDTH))],
                outs=[(_sds((t, S5_WIDTH), BF16), _rspec(tm, S5_WIDTH), False)])[0]
    vg = _mm(y5g, p['w_glu'], name="s5_glu")
    ybin = _rows(lambda a, b: a * _sigmoid(b), name="s5_glu_act", nrow=nrow,
                 ins=[(vg, _rspec(tm, S5_WIDTH, 0)), (vg, _rspec(tm, S5_WIDTH, 1))],
                 outs=[(_sds((t, S5_WIDTH), BF16), _rspec(tm, S5_WIDTH), False)])[0]
    y_b = _unperm(_mm(ybin, p['w_b'], name="mix_b"))
    merged = _rows(lambda ga, gb, a, b: _sigmoid(ga) * a + _sigmoid(gb) * b, name="mix_merge", nrow=nrow,
                   ins=[(proj, _rspec(tm, D_MODEL, GA0 // D_MODEL)), (proj, _rspec(tm, D_MODEL, GB0 // D_MODEL)),
                        (y_a, _rspec(tm, D_MODEL)), (y_b, _rspec(tm, D_MODEL))],
                   outs=[(_sds((t, D_MODEL), BF16), _rspec(tm, D_MODEL), False)])[0]
    m = _mm(merged, p['w_out'], name="mix_out")
    out = _resid_fwd(h, m, p['post_g'], 1.0, name="mix_res", tm=tm)
    saved = dict(w_a=p['w_a'], w_b=p['w_b'], w_out=p['w_out'], w_glu=p['w_glu'], h=h, u=u, u_p=u_p, proj=proj, u5=u5, xc=xc, dt4=dt4, s_in=s_in, y_ssd=y_ssd, ya=ya, y_a=y_a,
                 bre=bre, bim=bim, cre=cre, cim=cim, lr=lr, li=li, sr=sr, si=si, y5=y5, y5g=y5g, vg=vg, ybin=ybin,
                 y_b=y_b, merged=merged, m=m)
    return out, saved


def _mixer_bwd(dh, p, s, tm, after_first):
    t = dh.shape[0]
    nrow = t // tm
    proj = s['proj']
    bufs = {}

    def grad_mm(a, b, wname, axis, name):
        dw = _mm(a, b, ta=True, name=name, out_dtype=BF16, col_shards=axis == 'cols')
        bufs[wname] = dw if axis == 'cols' else dw.reshape(N_DEV, 1, dw.shape[0] // N_DEV, dw.shape[1])

    dm, dpost = _resid_bwd(s['m'], p['post_g'], dh, 1.0, name="mix_res_bwd", tm=tm)
    dm = after_first(dm)
    dmerged = _mm(dm, s['w_out'], tb=True, name="mix_out_dx")
    grad_mm(s['merged'], dm, 'w_out', 'rows', "mix_out_dw")

    def merge_bwd(ga, gb, a, b, d):
        _, vjp = jax.vjp(lambda ga_, gb_, a_, b_: _sigmoid(ga_) * a_ + _sigmoid(gb_) * b_, ga, gb, a, b)
        dga, dgb, da, db = vjp(d)
        return jnp.concatenate([dga, dgb], axis=1), da, db

    dgab, dy_a, dy_b = _rows(
        merge_bwd, name="mix_merge_bwd", nrow=nrow,
        ins=[(proj, _rspec(tm, D_MODEL, GA0 // D_MODEL)), (proj, _rspec(tm, D_MODEL, GB0 // D_MODEL)),
             (s['y_a'], _rspec(tm, D_MODEL)), (s['y_b'], _rspec(tm, D_MODEL)), (dmerged, _rspec(tm, D_MODEL))],
        outs=[(_sds((t, 2 * D_MODEL), BF16), _rspec(tm, 2 * D_MODEL), False),
              (_sds((t, D_MODEL), BF16), _rspec(tm, D_MODEL), False),
              (_sds((t, D_MODEL), BF16), _rspec(tm, D_MODEL), False)])
    dya = _mm(dy_a, s['w_a'], tb=True, name="mix_a_dx")
    grad_mm(s['ya'], dy_a, 'w_branch_a', 'rows', "mix_a_dw")
    gw = SSD_D_INNER // SSD_GROUPS

    def gate_bwd(y, z, g, d):
        _, vjp = jax.vjp(_gatenorm, y, z, g)
        return vjp(d)

    dy_ssd, dz, dnorm = _rows(
        gate_bwd, name="ssd_gate_bwd", nrow=nrow, ncol=SSD_GROUPS,
        ins=[(s['y_ssd'], _rspec(tm, gw, 0, True)), (proj, _rspec(tm, gw, Z0 // gw, True)),
             (p['norm_g'], _bspec(gw, 0, True)), (dya, _rspec(tm, gw, 0, True))],
        outs=[(_sds((t, SSD_D_INNER)), _rspec(tm, gw, 0, True), False),
              (_sds((t, SSD_D_INNER), BF16), _rspec(tm, gw, 0, True), False),
              (_sds((1, SSD_D_INNER)), _bspec(gw, 0, True), True)])
    dxs, dbm, dcm, ddt4, ddtb, dalog, ddsk = _ssd_bwd(s['xc'], s['dt4'], p['dt_bias8'], p['a_log8'], p['d8'],
                                                      s['s_in'], dy_ssd, name="ssd_scan_bwd")
    dxbc, dconv_w, dconv_b = _conv_bwd(proj, XBC0, p['conv_w'], p['conv_b'], (dxs, dbm, dcm), name="ssd_conv_bwd")
    ddtr = ddt4[:, :, :8].transpose(1, 0, 2).reshape(t, SSD_HEADS)
    dy_bp = _perm(dy_b)
    dybin = _mm(dy_bp, s['w_b'], tb=True, name="mix_b_dx")
    grad_mm(s['ybin'], dy_bp, 'w_branch_b', 'rows', "mix_b_dw")

    def glu_bwd(a, b, d):
        _, vjp = jax.vjp(lambda a_, b_: a_ * _sigmoid(b_), a, b)
        da, db = vjp(d)
        return jnp.concatenate([da, db], axis=1)

    dvg = _rows(glu_bwd, name="s5_glu_act_bwd", nrow=nrow,
                ins=[(s['vg'], _rspec(tm, S5_WIDTH, 0)), (s['vg'], _rspec(tm, S5_WIDTH, 1)), (dybin, _rspec(tm, S5_WIDTH))],
                outs=[(_sds((t, 2 * S5_WIDTH), BF16), _rspec(tm, 2 * S5_WIDTH), False)])[0]
    dy5g = _mm(dvg, s['w_glu'], tb=True, name="s5_glu_dx")
    grad_mm(s['y5g'], dvg, 's5_w_glu', 'cols', "s5_glu_dw")

    def act_bwd(a, b, d, g):
        _, vjp = jax.vjp(lambda a_, b_, d_: _gelu(a_ + d_ * b_), a, b, d)
        return vjp(g)

    dy5, du5_direct, ds5d = _rows(
        act_bwd, name="s5_act_bwd", nrow=nrow,
        ins=[(s['y5'], _rspec(tm, S5_WIDTH)), (s['u5'], _rspec(tm, S5_WIDTH)), (p['s5_d'], _bspec(S5_WIDTH)),
             (dy5g, _rspec(tm, S5_WIDTH))],
        outs=[(_sds((t, S5_WIDTH), BF16), _rspec(tm, S5_WIDTH), False), (_sds((t, S5_WIDTH)), _rspec(tm, S5_WIDTH), False),
              (_sds((1, S5_WIDTH)), _bspec(S5_WIDTH), True)])
    du5, dbre, dbim, dcre, dcim, dlr, dli = _s5_bwd(dy5, du5_direct, s['u5'], s['sr'], s['si'], s['bre'], s['bim'],
                                                     s['cre'], s['cim'], s['lr'], s['li'], name="s5_scan_bwd")
    du_p = _mm(du5, p['w_u5'], tb=True, name="mix_in_s5_dx")
    dw_u5 = _mm(s['u_p'], du5, ta=True, name="mix_in_s5_dw", out_dtype=BF16)
    ext_b = lambda m: _blockdiag_extract(m, S5_GROUP, S5_STATE).transpose(0, 2, 1).reshape(S5_GROUPS, S5_STATE * S5_GROUP)
    dlam_re, dlam_im, dlog_step, db_re, db_im = _s5_params_bwd(
        p['lam_re'], p['lam_im'], p['log_step'], p['b_re'], p['b_im'],
        (dlr.reshape(S5_GROUPS, S5_STATE), dli.reshape(S5_GROUPS, S5_STATE), ext_b(dbre), ext_b(dbim)), name="s5_par_bwd")
    dc_re = _blockdiag_extract(dcre, S5_STATE, S5_GROUP).transpose(0, 2, 1)
    dc_im = _blockdiag_extract(dcim, S5_STATE, S5_GROUP).transpose(0, 2, 1)
    dproj = jnp.concatenate([dz, dxbc, dgab], axis=1)
    du_big = _mm(dproj, p['w_big'], tb=True, name="mix_in_dx")
    du_dt = _mm(ddtr, p['w_dt'], tb=True, name="mix_in_dt_dx")
    dw_big = _mm(s['u'], dproj, ta=True, name="mix_in_dw", out_dtype=BF16)
    dw_dt = _mm(s['u'], ddtr, ta=True, name="mix_in_dt_dw", out_dtype=BF16)
    dh_in, dpre = _rms_bwd(s['h'], p['pre_g'], dh, [du_big, du_dt, _unperm(du_p)], name="mix_rms_bwd", tm=tm)
    dw_in = jnp.concatenate([dw_big[:, :GA0], dw_dt, dw_u5, dw_big[:, GA0:]], axis=1)
    bufs['w_in'] = dw_in.reshape(D_MODEL, N_DEV, -1).transpose(1, 0, 2)[:, None]
    grads = {
        'mix_pre_g': dpre, 'mix_post_g': dpost, 'ssd_conv_w': dconv_w, 'ssd_conv_b': dconv_b,
        'ssd_dt_bias': ddtb[:, 0, :8].reshape(-1), 'ssd_a_log': dalog[:, 0, :8].reshape(-1),
        'ssd_d': ddsk[:, 0, :8].reshape(-1), 'ssd_norm_g': dnorm,
        's5_lambda_re': dlam_re, 's5_lambda_im': dlam_im,
        's5_b_re': db_re.reshape(S5_GROUPS, S5_STATE, S5_GROUP), 's5_b_im': db_im.reshape(S5_GROUPS, S5_STATE, S5_GROUP),
        's5_c_re': dc_re, 's5_c_im': dc_im, 's5_log_step': dlog_step.reshape(-1), 's5_d': ds5d,
    }
    return dh_in, bufs, grads


HBM_SPEC = pl.BlockSpec(memory_space=pltpu.HBM)


def _place():
    return lax.axis_index("x"), lax.axis_index("y"), lax.axis_index("c")


GATHER_COLLECTIVE_ID = 1


def _all_gather(shards, *, name, on_sequencer=False):
    n = len(shards)

    def body(*refs):
        x_refs, out_refs = refs[:n], refs[n:2 * n]
        send_sems, recv_sems, local_sems = refs[2 * n:]
        x, y, c = _place()
        me, sibling = (x, y, c), (x, y, 1 - c)
        chips = [(1 - x, y), (x, 1 - y), (1 - x, 1 - y)]
        if on_sequencer:
            _handshake([sibling] + [(*chip, c) for chip in chips])

        def slot(o, px, py, pc):
            return out_refs[o].at[4 * px + 2 * py + pc]

        def copy(o, k, block, to, src=None):
            return pltpu.make_async_remote_copy(
                src_ref=slot(o, *block) if src is None else src, dst_ref=slot(o, *block),
                send_sem=send_sems.at[7 * o + k], recv_sem=recv_sems.at[7 * o + k], device_id=to, device_id_type=MESH)

        mine = [pltpu.make_async_copy(x_refs[o], slot(o, *me), local_sems.at[o]) for o in range(n)]
        for cp in mine:
            cp.start()
        first = []
        for j, chip in enumerate(chips):
            first += [copy(o, 1 + j, me, (*chip, c), src=x_refs[o]) for o in range(n)]
        first += [copy(o, 0, me, sibling, src=x_refs[o]) for o in range(n)]
        for cp in first:
            cp.start()
        passed = []
        for j, chip in enumerate(chips):
            for o in range(n):
                copy(o, 1 + j, (*chip, c), me).wait_recv()
                passed.append(copy(o, 4 + j, (*chip, c), sibling))
                passed[-1].start()
        for o in range(n):
            copy(o, 0, sibling, me).wait_recv()
        for j, chip in enumerate(chips):
            for o in range(n):
                copy(o, 4 + j, (*chip, 1 - c), me).wait_recv()
        for cp in first + passed:
            cp.wait_send()
        for cp in mine:
            cp.wait()

    out_shape = [jax.ShapeDtypeStruct((N_DEV,) + s.shape, s.dtype) for s in shards]
    sems = [pltpu.SemaphoreType.DMA((7 * n,)), pltpu.SemaphoreType.DMA((7 * n,)), pltpu.SemaphoreType.DMA((n,))]
    if on_sequencer:
        return _scall(body, name=name, out_type=out_shape, scratch_types=sems, collective_id=GATHER_COLLECTIVE_ID)(*shards)
    return _pcall(body, name=name, in_specs=[HBM_SPEC] * n, out_specs=[HBM_SPEC] * n, out_shape=out_shape,
                  scratch_shapes=sems)(*shards)


N_CHIPS = 4


SIBLING_COLLECTIVE_ID = 2
CHIPS_COLLECTIVE_ID = 3


def _handshake(peers):
    barrier = pltpu.get_barrier_semaphore()
    for peer in peers:
        pl.semaphore_signal(barrier, inc=1, device_id=peer, device_id_type=MESH)
    pl.semaphore_wait(barrier, len(peers))


def _exchange_sibling(grads, *, name):
    n = len(grads)

    def body(*refs):
        p_refs, q_refs = refs[:n], refs[n:2 * n]
        send_sems, recv_sems = refs[2 * n:]
        x, y, c = _place()
        _handshake([(x, y, 1 - c)])
        copies = [pltpu.make_async_remote_copy(
            src_ref=p_refs[o].at[k, 1 - c], dst_ref=q_refs[o].at[k], send_sem=send_sems.at[N_CHIPS * o + k],
            recv_sem=recv_sems.at[N_CHIPS * o + k], device_id=(x, y, 1 - c), device_id_type=MESH)
            for o in range(n) for k in range(N_CHIPS)]
        for cp in copies:
            cp.start()
        for cp in copies:
            cp.wait()

    return _scall(
        body, name=name, out_type=[jax.ShapeDtypeStruct((N_CHIPS,) + g.shape[2:], g.dtype) for g in grads],
        scratch_types=[pltpu.SemaphoreType.DMA((N_CHIPS * n,)), pltpu.SemaphoreType.DMA((N_CHIPS * n,))],
        collective_id=SIBLING_COLLECTIVE_ID,
    )(*grads)


def _pair_sum(own, got, *, name):
    _, _, r, l = own.shape
    tr = _tile(r, 512, 16)
    c = lax.axis_index("c").astype(jnp.int32).reshape(1)

    def body(c_ref, p_ref, q_ref, o_ref):
        o_ref[...] = (p_ref[...].astype(F32) + q_ref[...].astype(F32)).astype(o_ref.dtype)

    return _pcall(
        body, name=name,
        grid_spec=pltpu.PrefetchScalarGridSpec(
            num_scalar_prefetch=1, grid=(N_CHIPS, r // tr),
            in_specs=[pl.BlockSpec((None, None, tr, l), lambda k, i, cr: (k, cr[0], i, 0)),
                      pl.BlockSpec((None, tr, l), lambda k, i, cr: (k, i, 0))],
            out_specs=pl.BlockSpec((None, tr, l), lambda k, i, cr: (k, i, 0))),
        out_shape=jax.ShapeDtypeStruct((N_CHIPS, r, l), own.dtype),
        compiler_params=_params("parallel", "parallel"),
    )(c, own, got)


def _exchange_chips(parts, *, name):
    n = len(parts)

    def body(*refs):
        p_refs, g_refs = refs[:n], refs[n:2 * n]
        send_sems, recv_sems, local_sems = refs[2 * n:]
        x, y, c = _place()
        mine = 2 * x + y
        chips = [(1 - x, y), (x, 1 - y), (1 - x, 1 - y)]
        _handshake([(*chip, c) for chip in chips])
        own = [pltpu.make_async_copy(p_refs[o].at[mine], g_refs[o].at[mine], local_sems.at[o]) for o in range(n)]
        for cp in own:
            cp.start()
        copies = []
        for j, (px, py) in enumerate(chips):
            copies += [pltpu.make_async_remote_copy(
                src_ref=p_refs[o].at[2 * px + py], dst_ref=g_refs[o].at[mine], send_sem=send_sems.at[3 * o + j],
                recv_sem=recv_sems.at[3 * o + j], device_id=(px, py, c), device_id_type=MESH) for o in range(n)]
        for cp in copies:
            cp.start()
        for cp in copies:
            cp.wait()
        for cp in own:
            cp.wait()

    return _scall(
        body, name=name, out_type=[jax.ShapeDtypeStruct(p.shape, p.dtype) for p in parts],
        scratch_types=[pltpu.SemaphoreType.DMA((3 * n,)), pltpu.SemaphoreType.DMA((3 * n,)), pltpu.SemaphoreType.DMA((n,))],
        collective_id=CHIPS_COLLECTIVE_ID,
    )(*parts)


def _sum_slots(g, *, name):
    n, r, l = g.shape
    tr = _tile(r, 512, 16)

    def body(g_ref, o_ref):
        acc = g_ref[0].astype(F32)
        for k in range(1, n):
            acc = acc + g_ref[k].astype(F32)
        o_ref[...] = acc

    return _pcall(
        body, name=name, grid=(r // tr,), in_specs=[pl.BlockSpec((n, tr, l), lambda i: (0, i, 0))],
        out_specs=pl.BlockSpec((tr, l), lambda i: (i, 0)), out_shape=_sds((r, l)),
        compiler_params=_params("parallel"),
    )(g)


TRANSPOSED = ('ffn1_w_gate', 'ffn1_w_up', 'ffn2_w_gate', 'ffn2_w_up')
GATHER_CHUNKS = (('ffn1', ['ffn1_w_gate', 'ffn1_w_up']), ('ffn1_down', ['ffn1_w_down']),
                 ('mix_in', ['w_in', 'ssd_conv_w']), ('mix', ['w_branch_a', 's5_w_glu', 'w_branch_b', 'w_out']),
                 ('ffn2', ['ffn2_w_gate', 'ffn2_w_up']), ('ffn2_down', ['ffn2_w_down']))
LATE = ('ffn1_w_down', 'ffn2_w_down', 'w_branch_a', 's5_w_glu', 'w_branch_b', 'w_out')
SUBLAYERS = (('ffn1', ['ffn1_w_gate', 'ffn1_w_up', 'ffn1_w_down']),
             ('mix', ['w_in', 'ssd_conv_w', 'w_branch_a', 's5_w_glu', 'w_branch_b', 'w_out']),
             ('ffn2', ['ffn2_w_gate', 'ffn2_w_up', 'ffn2_w_down']))


def _gather_weights(w):
    layers, first = [], None
    for i in range(DEPTH):
        g = {}
        for tag, names in GATHER_CHUNKS:
            shards =[w[n][i:i + 1] if n == 'ssd_conv_w' else
                      (w[n][i:i + 1].transpose(0, 2, 1) if n in TRANSPOSED else w[n][i:i + 1]).astype(BF16) for n in names]
            if first is None:
                first = got = _all_gather(shards, name=f"gather_{tag}")
            else:
                shards, first = lax.optimization_barrier((shards, first))
                got = _all_gather(shards, name=f"gather_{tag}", on_sequencer=True)
            g.update(zip(names, got))
        layers.append(g)
    layers[0].update(zip(GATHER_CHUNKS[0][1], first))
    return layers


class _ReduceScatter:
    @staticmethod
    def sibling(tag, bufs):
        names = list(bufs)
        own = [bufs[n].reshape((N_CHIPS, 2) + bufs[n].shape[1:]) for n in names]
        return (tag, names), (own, _exchange_sibling(own, name=f"reduce_sibling_{tag}"))

    @staticmethod
    def chips(meta, arrays):
        (tag, names), (own, got) = meta, arrays
        flat = lambda a, lead: a.reshape(lead + (-1, a.shape[-1]))
        parts = [_pair_sum(flat(o, (N_CHIPS, 2)), flat(g, (N_CHIPS,)), name=f"reduce_pair_sum_{n}").reshape(g.shape)
                 for n, o, g in zip(names, own, got)]
        return names, _exchange_chips(parts, name=f"reduce_chips_{tag}")

    @staticmethod
    def done(names, slots):
        return dict(zip(names, slots))

    @staticmethod
    def small(grads):
        return _reduce_small(grads)


def _reduce_small(grads):
    flat = jnp.concatenate([g.astype(F32).reshape(-1) for g in grads.values()])
    pad = (-flat.shape[0]) % (8 * LANES)
    flat = jnp.concatenate([flat, jnp.zeros((pad,), F32)]).reshape(-1, LANES)
    gathered = _all_gather([flat], name="gather_small_grads", on_sequencer=True)[0]
    total = _sum_slots(gathered, name="sum_small_grads").reshape(-1)
    out, o = {}, 0
    for n, g in grads.items():
        out[n] = total[o:o + g.size].reshape(g.shape)
        o += g.size
    return out


def _adamw(w, g, m, v, *, name, slots=False):
    shape = w.shape
    if slots:
        lyr, rows, lanes = shape
        w2, m2, v2 = w, m, v
        tr = _tile(rows, 256, 16)
        nrt = rows // tr
        grid = (lyr, nrt)
        spec = pl.BlockSpec((None, tr, lanes), lambda l, i: (l, i, 0))
        g_specs = [pl.BlockSpec((N_CHIPS, None, tr, lanes),
                                lambda l, i, k=k: (0, 0, jnp.where(l == k, i, jnp.where(l > k, nrt - 1, 0)), 0))
                   for k in range(lyr)]
        g_args = list(g)
        out_shape = [_sds(shape)] * 4
    else:
        lanes = shape[-1] if (shape[-1] >= 128 or w.size % LANES) else LANES
        as2d = lambda a: a.reshape(-1, lanes)
        w2, m2, v2 = as2d(w), as2d(m), as2d(v)
        r = w2.shape[0]
        tr = _tile(r, 256, 8)
        grid = (1, r // tr)
        spec = pl.BlockSpec((tr, lanes), lambda l, i: (i, 0))
        g_specs, g_args = [spec], [as2d(g)]
        out_shape = [_sds((r, lanes))] * 4
    n_g = len(g_args)

    def body(w_ref, *rest):
        g_refs = rest[:n_g]
        m_ref, v_ref, go_ref, d_ref, mo_ref, vo_ref = rest[n_g:]
        if slots:
            gg = None
            for k, g_ref in enumerate(g_refs):
                tot = g_ref[0].astype(F32)
                for c in range(1, N_CHIPS):
                    tot = tot + g_ref[c].astype(F32)
                gg = tot if gg is None else jnp.where(pl.program_id(0) == k, tot, gg)
        else:
            gg = g_refs[0][...]
        go_ref[...] = gg
        mn = ADAM_B1 * m_ref[...] + (1.0 - ADAM_B1) * gg
        vn = ADAM_B2 * v_ref[...] + (1.0 - ADAM_B2) * (gg * gg)
        m_hat = mn / (1.0 - ADAM_B1 ** ADAM_STEP)
        v_hat = vn / (1.0 - ADAM_B2 ** ADAM_STEP)
        d_ref[...] = -ADAM_LR * (m_hat / (jnp.sqrt(v_hat) + ADAM_EPS) + ADAM_WD * w_ref[...])
        mo_ref[...] = mn
        vo_ref[...] = vn

    res = _pcall(
        body, name=name, grid=grid, in_specs=[spec] + g_specs + [spec, spec], out_specs=[spec] * 4,
        out_shape=out_shape, compiler_params=_params("arbitrary", "arbitrary"),
    )(w2, *g_args, m2, v2)
    return tuple(a.reshape(shape) for a in res)


def _sublayer_params(w, g, i, k, s5):
    row = lambda a: a.astype(F32).reshape(1, -1)
    if k != 'mix':
        return dict(layer=i, pre_g=row(w[f'{k}_pre_g'][i]), post_g=row(w[f'{k}_post_g'][i]),
                    w_gate=g[f'{k}_w_gate'], w_up=g[f'{k}_w_up'], w_down=g[f'{k}_w_down'])
    head8 = lambda a: jnp.broadcast_to(
        jnp.pad(a.astype(F32).reshape(SSD_GROUPS, 1, 8), ((0, 0), (0, 0), (0, PAD_HEADS - 8))), (SSD_GROUPS, 8, PAD_HEADS))
    by_cols = lambda n: g[n][:, 0].transpose(1, 0, 2).reshape(g[n].shape[2], -1)
    w_in = by_cols('w_in')
    s = np.cumsum([SSD_D_INNER, SSD_CONV_DIM, SSD_HEADS, S5_WIDTH, D_MODEL])
    return dict(
        layer=i, s5=s5, pre_g=row(w['mix_pre_g'][i]), post_g=row(w['mix_post_g'][i]),
        w_big=jnp.concatenate([w_in[:, :s[1]], w_in[:, s[3]:]], axis=1), w_dt=w_in[:, s[1]:s[2]], w_u5=w_in[:, s[2]:s[3]],
        conv_w=by_cols('ssd_conv_w'), conv_b=row(w['ssd_conv_b'][i]),
        dt_bias8=head8(w['ssd_dt_bias'][i]), a_log8=head8(w['ssd_a_log'][i]), d8=head8(w['ssd_d'][i]),
        norm_g=row(w['ssd_norm_g'][i]), late={n: g[n] for n in SUBLAYERS[1][1] if n in LATE},
        lam_re=w['s5_lambda_re'][i], lam_im=w['s5_lambda_im'][i], log_step=w['s5_log_step'][i].reshape(S5_GROUPS, 1),
        b_re=w['s5_b_re'][i].reshape(S5_GROUPS, -1), b_im=w['s5_b_im'][i].reshape(S5_GROUPS, -1),
        c_re=w['s5_c_re'][i], c_im=w['s5_c_im'][i], s5_d=row(w['s5_d'][i]),
    )


def _loss_head(h, target, *, tm):
    t, d = h.shape

    def fn(y, tgt):
        err = y - tgt
        return err * (1.0 / d), jnp.sum(0.5 * jnp.sum(err * err, axis=-1, keepdims=True) * (1.0 / d), axis=0, keepdims=True)

    dy, loss = _rows(fn, name="loss_head", nrow=t // tm,
                     ins=[(h, _rspec(tm, d)), (target, _rspec(tm, d))],
                     outs=[(_sds((t, d)), _rspec(tm, d), False), (_sds((1, 128)), _bspec(128), True)])
    return dy, loss[0, 0]


def _forward_backward(h, target, w, g, rs):
    t = h.shape[0]
    tm = _tile(t, 512, 8)
    s5 = None
    layers, saved = [], []
    for i in range(DEPTH):
        gi, ps, ss = dict(g[i]), [], []
        for tag, names in SUBLAYERS:
            if tag == 'mix' and s5 is None:
                mine = {n: w[n] for n in WEIGHTS if n.startswith('s5_') and n not in SHARDED}
                mine, h = lax.optimization_barrier((mine, h))
                s5 = _s5_prepare(mine)
            early = [n for n in names if n not in LATE]
            tied, h, s5 = lax.optimization_barrier(([gi[n] for n in early], h, s5))
            gi.update(zip(early, tied))
            p = _sublayer_params(w, gi, i, tag, s5[i] if tag == 'mix' else None)
            h, s = _mixer_fwd(h, p, tm) if tag == 'mix' else _ffn_fwd(h, p, tag, tm)
            ps.append(p)
            ss.append(s)
        layers.append(ps)
        saved.append(ss)
    dh, loss = _loss_head(h, target, tm=tm)
    reduced, small = [{} for _ in range(DEPTH)], [{} for _ in range(DEPTH)]
    in_sibling, in_chips = None, None

    def start_chips(x):
        nonlocal in_sibling, in_chips
        if in_sibling is not None:
            layer, meta, arrays = in_sibling
            arrays, x = lax.optimization_barrier((arrays, x))
            in_sibling, in_chips = None, (layer,) + tuple(rs.chips(meta, arrays))
        return x

    def finish_chips(x):
        nonlocal in_chips
        if in_chips is not None:
            layer, names, slots = in_chips
            slots, x = lax.optimization_barrier((slots, x))
            reduced[layer].update(rs.done(names, slots))
            in_chips = None
        return x

    for i in reversed(range(DEPTH)):
        for k in reversed(range(len(SUBLAYERS))):
            tag = SUBLAYERS[k][0]
            if tag == 'mix':
                dh, bufs, grads = _mixer_bwd(dh, layers[i][k], saved[i][k], tm, start_chips)
            else:
                dh, bufs, grads = _ffn_bwd(dh, layers[i][k], saved[i][k], tag, tm, start_chips)
            small[i].update(grads)
            dh = finish_chips(dh)
            in_sibling = (i,) + tuple(rs.sibling(tag, bufs))
            if tag == 'mix' and i + 1 < DEPTH:
                small[i + 1], dh = lax.optimization_barrier((small[i + 1], dh))
        if i == 0:
            small[i]['loss'] = loss.reshape(1)
        small[i] = rs.small(small[i])
    loss = small[0].pop('loss')[0]
    dh = finish_chips(start_chips(dh))
    shapes = {n: (w[n].shape[:-1] + (SSD_CONV_DIM,) if n == 'ssd_conv_w' else w[n].shape) for n in SMALL_ORDER}
    stacked = {n: jnp.stack([small[i][n].reshape(shapes[n][1:]) for i in range(DEPTH)]) for n in SMALL_ORDER}
    return loss, dh, reduced, stacked


def kernel(*args):
    n_w = len(WEIGHTS)
    x, target = args[0], args[1 + n_w]
    w = dict(zip(WEIGHTS, args[1:1 + n_w]))
    m = dict(zip(WEIGHTS, args[2 + n_w:2 + 2 * n_w]))
    v = dict(zip(WEIGHTS, args[2 + 2 * n_w:2 + 3 * n_w]))
    t = x.shape[1]

    g = _gather_weights(w)
    loss, dx, slots, small = _forward_backward(x.reshape(t, D_MODEL), target.reshape(t, D_MODEL), w, g, _ReduceScatter)
    me = 4 * lax.axis_index("x") + 2 * lax.axis_index("y") + lax.axis_index("c")
    cols = w['ssd_conv_w'].shape[-1]
    small['ssd_conv_w'] = lax.dynamic_slice_in_dim(small['ssd_conv_w'], me * cols, cols, axis=2)

    grad, delta, new_m, new_v = {}, {}, {}, {}
    for n in WEIGHTS:
        sharded = n in slots[0]
        view = (lambda a: a.transpose(0, 2, 1)) if n in TRANSPOSED else (lambda a: a)
        res = _adamw(view(w[n]), [slots[i][n] for i in range(DEPTH)] if sharded else small[n], view(m[n]), view(v[n]),
                     name=f"adamw_{n}", slots=sharded)
        grad[n], delta[n], new_m[n], new_v[n] = (view(a) for a in res)
    return (loss, dx.reshape(x.shape), *[grad[n] for n in WEIGHTS], *[delta[n] for n in WEIGHTS],
            *[new_m[n] for n in WEIGHTS], *[new_v[n] for n in WEIGHTS])
```

```python
import math

import numpy as np
import jax
import jax.numpy as jnp
from jax import lax
from jax.experimental import pallas as pl
from jax.experimental.pallas import tpu as pltpu
from jax.experimental.pallas import tpu_sc as plsc

F32 = jnp.float32
BF16 = jnp.bfloat16
MESH = pl.DeviceIdType.MESH
HIGHEST = lax.Precision.HIGHEST

D_MODEL = 1024
DEPTH = 2
FFN_HIDDEN = 2816
SSD_D_INNER = 2048
SSD_HEADS = 32
SSD_HEAD_DIM = 64
SSD_GROUPS = 4
SSD_STATE = 128
SSD_CHUNK = 128
SSD_CONV_DIM = 3072
SSD_CONV_WIDTH = 4
S5_WIDTH = 1024
S5_GROUP = 16
S5_GROUPS = 64
S5_STATE = 64
S5_MAX_REAL = -1e-4
S5_BLOCKS = 8
RMS_EPS = 1e-6
N_DEV = 8
LANES = 1024

ADAM_LR = 0.001
ADAM_B1 = 0.9
ADAM_B2 = 0.999
ADAM_EPS = 1e-08
ADAM_WD = 0.01
ADAM_STEP = 10

VMEM_LIMIT_BYTES = 48 * 1024 * 1024

WEIGHTS = ['ffn1_pre_g', 'ffn1_post_g', 'ffn1_w_gate', 'ffn1_w_up', 'ffn1_w_down', 'mix_pre_g', 'mix_post_g',
           'w_in', 'ssd_conv_w', 'ssd_conv_b', 'ssd_dt_bias', 'ssd_a_log', 'ssd_d', 'ssd_norm_g', 'w_branch_a',
           's5_lambda_re', 's5_lambda_im', 's5_b_re', 's5_b_im', 's5_c_re', 's5_c_im', 's5_log_step', 's5_d',
           's5_w_glu', 'w_branch_b', 'w_out', 'ffn2_pre_g', 'ffn2_post_g', 'ffn2_w_gate', 'ffn2_w_up',
           'ffn2_w_down']
SHARDED = {'ffn1_w_gate': 2, 'ffn1_w_up': 2, 'ffn1_w_down': 1, 'w_in': 2, 'ssd_conv_w': 2, 'w_branch_a': 1,
           's5_w_glu': 2, 'w_branch_b': 1, 'w_out': 1, 'ffn2_w_gate': 2, 'ffn2_w_up': 2, 'ffn2_w_down': 1}
SHARDED_ORDER = [n for n in WEIGHTS if n in SHARDED]
SMALL_ORDER = [n for n in WEIGHTS if n not in SHARDED or n == 'ssd_conv_w']


def _pcall(body, **kw):
    return pl.pallas_call(body, **kw)


def _scall(body, *, name, out_type, scratch_types, collective_id):
    return pl.kernel(body, out_type=out_type, mesh=plsc.ScalarSubcoreMesh(axis_name="sequencer", num_cores=1),
                     scratch_types=scratch_types, name=name,
                     compiler_params=pltpu.CompilerParams(collective_id=collective_id))


def _params(*sem):
    return pltpu.CompilerParams(dimension_semantics=sem, vmem_limit_bytes=VMEM_LIMIT_BYTES)


def _tile(n, pref, align=128):
    if n <= pref:
        return n
    t = (pref // align) * align
    while t >= align:
        if n % t == 0:
            return t
        t -= align
    return n


def _rms(x, g):
    return x * lax.rsqrt(jnp.mean(x * x, axis=-1, keepdims=True) + RMS_EPS) * g


def _sigmoid(x):
    return 1.0 / (1.0 + jnp.exp(-x))


def _silu(x):
    return x * _sigmoid(x)


def _gelu(x):
    return 0.5 * x * (1.0 + jnp.tanh(math.sqrt(2.0 / math.pi) * (x + 0.044715 * (x * x * x))))


def _softplus(x):
    return jnp.maximum(x, 0.0) + jnp.log(1.0 + jnp.exp(-jnp.abs(x)))


def _dot(a, b, dims):
    return lax.dot_general(a.astype(BF16), b.astype(BF16), (dims, ((), ())), preferred_element_type=F32)


_NN = ((1,), (0,))
_NT = ((1,), (1,))
_TN = ((0,), (0,))


@jax.custom_vjp
def _bdot_nn(a, b):
    return _dot(a, b, _NN)


_bdot_nn.defvjp(lambda a, b: (_dot(a, b, _NN), (a, b)),
                lambda r, g: (_dot(g, r[1], _NT), _dot(r[0], g, _TN)))


@jax.custom_vjp
def _bdot_nt(a, b):
    return _dot(a, b, _NT)


_bdot_nt.defvjp(lambda a, b: (_dot(a, b, _NT), (a, b)),
                lambda r, g: (_dot(g, r[1], _NN), _dot(g, r[0], _TN)))


@jax.custom_vjp
def _bdot_tn(a, b):
    return _dot(a, b, _TN)


_bdot_tn.defvjp(lambda a, b: (_dot(a, b, _TN), (a, b)),
                lambda r, g: (_dot(r[1], g, _NT), _dot(r[0], g, _NN)))


def _fdot(a, b, dims=_NN):
    return lax.dot_general(a, b, (dims, ((), ())), precision=HIGHEST, preferred_element_type=F32)


def _sel3(x, sel, dims, x_first):
    p1 = x.astype(BF16)
    r1 = x - p1.astype(F32)
    p2 = r1.astype(BF16)
    p3 = (r1 - p2.astype(F32)).astype(BF16)
    sel = sel.astype(BF16)
    out = None
    for piece in (p1, p2, p3):
        d = lax.dot_general(*((piece, sel) if x_first else (sel, piece)), (dims, ((), ())), preferred_element_type=F32)
        out = d if out is None else out + d
    return out


@jax.custom_vjp
def _sel_right(x, sel):
    return _sel3(x, sel, _NN, True)


_sel_right.defvjp(lambda x, sel: (_sel3(x, sel, _NN, True), sel),
                  lambda sel, g: (_sel3(g, sel, _NT, True), jnp.zeros_like(sel)))


@jax.custom_vjp
def _sel_left(sel, x):
    return _sel3(x, sel, _NN, False)


_sel_left.defvjp(lambda sel, x: (_sel3(x, sel, _NN, False), sel),
                 lambda sel, g: (jnp.zeros_like(sel), _sel3(g, sel, _TN, False)))


@jax.custom_vjp
def _sel_left_nt(sel, x):
    return _sel3(x, sel, _NT, False)


_sel_left_nt.defvjp(lambda sel, x: (_sel3(x, sel, _NT, False), sel),
                    lambda sel, g: (jnp.zeros_like(sel), _sel3(g, sel, _TN, True)))


def _mm(a, b, *, name, ta=False, tb=False, out_dtype=F32, tm=2048, tn=512, tk=2048, col_shards=False):
    m, k = (a.shape[1], a.shape[0]) if ta else a.shape
    n = b.shape[0] if tb else b.shape[1]
    assert k == (b.shape[1] if tb else b.shape[0]), (a.shape, b.shape, ta, tb)
    if col_shards:
        tn = n // N_DEV
    tm, tn, tk = _tile(m, tm), _tile(n, tn), _tile(k, tk)
    nk = k // tk
    a_spec = pl.BlockSpec((tk, tm), lambda i, j, kk: (kk, i)) if ta else pl.BlockSpec((tm, tk), lambda i, j, kk: (i, kk))
    b_spec = pl.BlockSpec((tn, tk), lambda i, j, kk: (j, kk)) if tb else pl.BlockSpec((tk, tn), lambda i, j, kk: (kk, j))
    dims = ((0 if ta else 1,), (1 if tb else 0,))
    out_spec = pl.BlockSpec((tm, tn), lambda i, j, kk: (i, j))
    out_shape = jax.ShapeDtypeStruct((m, n), out_dtype)
    if col_shards:
        out_shape = jax.ShapeDtypeStruct((N_DEV, 1, m, n // N_DEV), out_dtype)
        out_spec = pl.BlockSpec((None, None, tm, tn), lambda i, j, kk: (j, 0, i, 0))

    def body(a_ref, b_ref, o_ref, acc_ref):
        kk = pl.program_id(2)

        @pl.when(kk == 0)
        def _():
            acc_ref[...] = jnp.zeros_like(acc_ref)

        acc_ref[...] += _dot(a_ref[...], b_ref[...], dims)

        @pl.when(kk == nk - 1)
        def _():
            o_ref[...] = acc_ref[...].astype(o_ref.dtype)

    return _pcall(
        body, name=name, grid=(m // tm, n // tn, nk),
        in_specs=[a_spec, b_spec], out_specs=out_spec, out_shape=out_shape,
        scratch_shapes=[pltpu.VMEM((tm, tn), F32)],
        compiler_params=_params("parallel", "parallel", "arbitrary"),
    )(a, b)


def _rspec(tm, w, cb=0, percol=False):
    return pl.BlockSpec((tm, w), (lambda j, i: (i, cb + j)) if percol else (lambda j, i: (i, cb)))


def _bspec(w, cb=0, percol=False, rows=1):
    return pl.BlockSpec((rows, w), (lambda j, i: (0, cb + j)) if percol else (lambda j, i: (0, cb)))


def _rows(fn, *, name, nrow, ncol=1, ins, outs):
    n_in = len(ins)
    accs = [o[2] for o in outs]

    def body(*refs):
        vals = fn(*[r[...] for r in refs[:n_in]])
        if not isinstance(vals, (tuple, list)):
            vals = (vals,)
        i = pl.program_id(1)
        for ref, val, acc in zip(refs[n_in:], vals, accs):
            if acc:
                @pl.when(i == 0)
                def _(ref=ref):
                    ref[...] = jnp.zeros_like(ref)

                ref[...] += jnp.broadcast_to(val, ref.shape).astype(ref.dtype)
            else:
                ref[...] = val.astype(ref.dtype)

    res = _pcall(
        body, name=name, grid=(ncol, nrow),
        in_specs=[s for _, s in ins], out_specs=[o[1] for o in outs], out_shape=[o[0] for o in outs],
        compiler_params=_params("parallel", "arbitrary"),
    )(*[a for a, _ in ins])
    return res


def _sds(shape, dtype=F32):
    return jax.ShapeDtypeStruct(shape, dtype)


def _rms_fwd(h, g, *, name, tm):
    t, d = h.shape
    return _rows(lambda x, gg: _rms(x, gg), name=name, nrow=t // tm,
                 ins=[(h, _rspec(tm, d)), (g, _bspec(d))],
                 outs=[(_sds((t, d), BF16), _rspec(tm, d), False)])[0]


def _resid_fwd(h, f, g, scale, *, name, tm):
    t, d = h.shape
    return _rows(lambda x, ff, gg: x + scale * _rms(ff, gg), name=name, nrow=t // tm,
                 ins=[(h, _rspec(tm, d)), (f, _rspec(tm, d)), (g, _bspec(d))],
                 outs=[(_sds((t, d)), _rspec(tm, d), False)])[0]


def _resid_bwd(f, g, dh, scale, *, name, tm):
    t, d = f.shape

    def fn(ff, gg, dd):
        _, vjp = jax.vjp(lambda a, b: scale * _rms(a, b), ff, gg)
        return vjp(dd)

    return _rows(fn, name=name, nrow=t // tm,
                 ins=[(f, _rspec(tm, d)), (g, _bspec(d)), (dh, _rspec(tm, d))],
                 outs=[(_sds((t, d), BF16), _rspec(tm, d), False), (_sds((1, d)), _bspec(d), True)])


def _rms_bwd(h, g, dh, dxns, *, name, tm):
    t, d = h.shape

    def fn(x, gg, dd, *dx):
        _, vjp = jax.vjp(_rms, x, gg)
        tot = dx[0]
        for more in dx[1:]:
            tot = tot + more
        dxx, dg = vjp(tot)
        return dd + dxx, dg

    return _rows(fn, name=name, nrow=t // tm,
                 ins=[(h, _rspec(tm, d)), (g, _bspec(d)), (dh, _rspec(tm, d))] + [(x, _rspec(tm, d)) for x in dxns],
                 outs=[(_sds((t, d)), _rspec(tm, d), False), (_sds((1, d)), _bspec(d), True)])


FFN_BLOCKS = 4
NB = FFN_HIDDEN // FFN_BLOCKS
MM_ROWS = 2048


def _ffn_up(xn, wg, wu, *, name):
    t = xn.shape[0]
    tm = _tile(t, MM_ROWS // 2)
    wspec = pl.BlockSpec((None, None, NB, D_MODEL), lambda i, j: (j, 0, 0, 0))

    def body(x_ref, g_ref, u_ref, ab_ref, hh_ref):
        x = x_ref[...]
        a, b = _dot(x, g_ref[...], _NT), _dot(x, u_ref[...], _NT)
        ab_ref[0] = a.astype(ab_ref.dtype)
        ab_ref[1] = b.astype(ab_ref.dtype)
        hh_ref[...] = (_silu(a) * b).astype(hh_ref.dtype)

    return _pcall(
        body, name=name, grid=(t // tm, FFN_BLOCKS),
        in_specs=[pl.BlockSpec((tm, D_MODEL), lambda i, j: (i, 0)), wspec, wspec],
        out_specs=[pl.BlockSpec((None, 2, tm, NB), lambda i, j: (j, 0, i, 0)),
                   pl.BlockSpec((None, tm, NB), lambda i, j: (j, i, 0))],
        out_shape=[_sds((FFN_BLOCKS, 2, t, NB), BF16), _sds((FFN_BLOCKS, t, NB), BF16)],
        compiler_params=_params("parallel", "parallel"),
    )(xn, wg, wu)


def _ffn_down(hh, wd, *, name):
    t = hh.shape[1]
    tm = _tile(t, 512)

    def body(h_ref, w_ref, o_ref):
        acc = _dot(h_ref[0], w_ref[0, 0], _NN)
        for k in range(1, FFN_BLOCKS):
            acc = acc + _dot(h_ref[k], w_ref[k, 0], _NN)
        o_ref[...] = acc

    return _pcall(
        body, name=name, grid=(t // tm,),
        in_specs=[pl.BlockSpec((FFN_BLOCKS, tm, NB), lambda i: (0, i, 0)),
                  pl.BlockSpec((FFN_BLOCKS, 1, NB, D_MODEL), lambda i: (0, 0, 0, 0))],
        out_specs=pl.BlockSpec((tm, D_MODEL), lambda i: (i, 0)), out_shape=_sds((t, D_MODEL)),
        compiler_params=_params("parallel"),
    )(hh, wd)


def _ffn_down_dx(df, wd, ab, *, name):
    t = df.shape[0]
    tm = _tile(t, MM_ROWS // 2)

    def body(d_ref, w_ref, ab_ref, o_ref):
        dhh = _dot(d_ref[...], w_ref[...], _NT)
        _, vjp = jax.vjp(lambda a, b: _silu(a) * b, ab_ref[0].astype(F32), ab_ref[1].astype(F32))
        da, db = vjp(dhh)
        o_ref[0] = da.astype(o_ref.dtype)
        o_ref[1] = db.astype(o_ref.dtype)

    blk = pl.BlockSpec((None, 2, tm, NB), lambda i, j: (j, 0, i, 0))
    return _pcall(
        body, name=name, grid=(t // tm, FFN_BLOCKS),
        in_specs=[pl.BlockSpec((tm, D_MODEL), lambda i, j: (i, 0)),
                  pl.BlockSpec((None, None, NB, D_MODEL), lambda i, j: (j, 0, 0, 0)), blk],
        out_specs=blk, out_shape=_sds((FFN_BLOCKS, 2, t, NB), BF16), compiler_params=_params("parallel", "parallel"),
    )(df, wd, ab)


def _ffn_down_dw(hh, df, *, name, tn=512):
    t = df.shape[0]
    tk = _tile(t, 2048)
    nk = t // tk

    def body(h_ref, d_ref, o_ref, acc_ref):
        kk = pl.program_id(2)

        @pl.when(kk == 0)
        def _():
            acc_ref[...] = jnp.zeros_like(acc_ref)

        acc_ref[...] += _dot(h_ref[...], d_ref[...], _TN)

        @pl.when(kk == nk - 1)
        def _():
            o_ref[...] = acc_ref[...].astype(o_ref.dtype)

    return _pcall(
        body, name=name, grid=(FFN_BLOCKS, D_MODEL // tn, nk),
        in_specs=[pl.BlockSpec((None, tk, NB), lambda j, n, kk: (j, kk, 0)),
                  pl.BlockSpec((tk, tn), lambda j, n, kk: (kk, n))],
        out_specs=pl.BlockSpec((None, None, NB, tn), lambda j, n, kk: (j, 0, 0, n)),
        out_shape=_sds((FFN_BLOCKS, 1, NB, D_MODEL), BF16),
        scratch_shapes=[pltpu.VMEM((NB, tn), F32)],
        compiler_params=_params("parallel", "parallel", "arbitrary"),
    )(hh, df)


def _ffn_up_dx(dab, wg, wu, *, name):
    t = dab.shape[2]
    tm = _tile(t, MM_ROWS // 2)
    wspec = pl.BlockSpec((None, None, NB, D_MODEL), lambda i, j: (j, 0, 0, 0))

    def body(d_ref, g_ref, u_ref, o_ref):
        @pl.when(pl.program_id(1) == 0)
        def _():
            o_ref[...] = jnp.zeros_like(o_ref)

        o_ref[...] += _dot(d_ref[0], g_ref[...], _NN) + _dot(d_ref[1], u_ref[...], _NN)

    return _pcall(
        body, name=name, grid=(t // tm, FFN_BLOCKS),
        in_specs=[pl.BlockSpec((None, 2, tm, NB), lambda i, j: (j, 0, i, 0)), wspec, wspec],
        out_specs=pl.BlockSpec((tm, D_MODEL), lambda i, j: (i, 0)), out_shape=_sds((t, D_MODEL)),
        compiler_params=_params("parallel", "arbitrary"),
    )(dab, wg, wu)


def _ffn_up_dw(xn, dab, *, name):
    t = xn.shape[0]

    def body(x_ref, d_ref, og_ref, ou_ref):
        x = x_ref[...]
        og_ref[...] = _dot(d_ref[0], x, _TN).astype(og_ref.dtype)
        ou_ref[...] = _dot(d_ref[1], x, _TN).astype(ou_ref.dtype)

    out = pl.BlockSpec((None, None, NB, D_MODEL), lambda j: (j, 0, 0, 0))
    return _pcall(
        body, name=name, grid=(FFN_BLOCKS,),
        in_specs=[pl.BlockSpec((t, D_MODEL), lambda j: (0, 0)), pl.BlockSpec((None, 2, t, NB), lambda j: (j, 0, 0, 0))],
        out_specs=[out, out], out_shape=[_sds((FFN_BLOCKS, 1, NB, D_MODEL), BF16)] * 2,
        compiler_params=_params("parallel"),
    )(xn, dab)


def _paired(a):
    return a.reshape(FFN_BLOCKS, 1, NB, D_MODEL)


def _ffn_fwd(h, p, tag, tm):
    xn = _rms_fwd(h, p['pre_g'], name=f"{tag}_rms", tm=tm)
    ab, hh = _ffn_up(xn, _paired(p['w_gate']), _paired(p['w_up']), name=f"{tag}_up")
    w_down, hh = lax.optimization_barrier((p['w_down'], hh))
    f = _ffn_down(hh, _paired(w_down), name=f"{tag}_down")
    out = _resid_fwd(h, f, p['post_g'], 0.5, name=f"{tag}_res", tm=tm)
    return out, (h, xn, ab, hh, f)


def _ffn_bwd(dh, p, saved, tag, tm, after_first):
    h, xn, ab, hh, f = saved
    df, dpost = _resid_bwd(f, p['post_g'], dh, 0.5, name=f"{tag}_res_bwd", tm=tm)
    df = after_first(df)
    dab = _ffn_down_dx(df, _paired(p['w_down']), ab, name=f"{tag}_down_dx")
    bufs = {f'{tag}_w_down': _ffn_down_dw(hh, df, name=f"{tag}_down_dw")}
    dxn = _ffn_up_dx(dab, _paired(p['w_gate']), _paired(p['w_up']), name=f"{tag}_up_dx")
    bufs[f'{tag}_w_gate'], bufs[f'{tag}_w_up'] = _ffn_up_dw(xn, dab, name=f"{tag}_up_dw")
    bufs = {n: a.reshape(N_DEV, 1, FFN_HIDDEN // N_DEV, D_MODEL) for n, a in bufs.items()}
    dh_in, dpre = _rms_bwd(h, p['pre_g'], dh, [dxn], name=f"{tag}_rms_bwd", tm=tm)
    return dh_in, bufs, {f'{tag}_pre_g': dpre, f'{tag}_post_g': dpost}


CONV_COLS = 256


def _shift_down(x, s):
    rows = lax.broadcasted_iota(jnp.int32, x.shape, 0)
    return jnp.where(rows >= s, pltpu.roll(x, s, axis=0), 0.0)


def _shift_up(x, s):
    t = x.shape[0]
    rows = lax.broadcasted_iota(jnp.int32, x.shape, 0)
    return jnp.where(rows < t - s, pltpu.roll(x, t - s, axis=0), 0.0)


def _conv_fwd(proj, col0, w, b, *, name):
    t = proj.shape[0]
    c = w.shape[1]
    cb0 = col0 // CONV_COLS

    def body(x_ref, w_ref, b_ref, o_ref):
        x = x_ref[...]
        acc = x * w_ref[3:4, :] + b_ref[...]
        for k in range(SSD_CONV_WIDTH - 1):
            acc = acc + _shift_down(x, SSD_CONV_WIDTH - 1 - k) * w_ref[k:k + 1, :]
        o_ref[...] = _silu(acc)

    return _pcall(
        body, name=name, grid=(c // CONV_COLS,),
        in_specs=[pl.BlockSpec((t, CONV_COLS), lambda j: (0, cb0 + j)),
                  pl.BlockSpec((SSD_CONV_WIDTH, CONV_COLS), lambda j: (0, j)),
                  pl.BlockSpec((1, CONV_COLS), lambda j: (0, j))],
        out_specs=pl.BlockSpec((t, CONV_COLS), lambda j: (0, j)),
        out_shape=_sds((t, c)), compiler_params=_params("parallel"),
    )(proj, w, b)


def _conv_bwd(proj, col0, w, b, douts, *, name):
    t = proj.shape[0]
    c = w.shape[1]
    cb0 = col0 // CONV_COLS
    first = np.cumsum([0] + [d.shape[1] // CONV_COLS for d in douts])

    def body(x_ref, w_ref, b_ref, *rest):
        d_refs, (dx_ref, dw_ref, db_ref) = rest[:len(douts)], rest[len(douts):]
        j = pl.program_id(0)
        dout = d_refs[-1][...]
        for k in range(len(douts) - 2, -1, -1):
            dout = jnp.where(j < int(first[k + 1]), d_refs[k][...], dout)
        x = x_ref[...]
        shifted = [_shift_down(x, SSD_CONV_WIDTH - 1 - k) for k in range(SSD_CONV_WIDTH - 1)] + [x]
        pre = b_ref[...] + shifted[3] * w_ref[3:4, :]
        for k in range(SSD_CONV_WIDTH - 1):
            pre = pre + shifted[k] * w_ref[k:k + 1, :]
        sg = _sigmoid(pre)
        dpre = dout * (sg * (1.0 + pre * (1.0 - sg)))
        dx = dpre * w_ref[3:4, :]
        for k in range(SSD_CONV_WIDTH - 1):
            dx = dx + _shift_up(dpre, SSD_CONV_WIDTH - 1 - k) * w_ref[k:k + 1, :]
        dx_ref[...] = dx.astype(dx_ref.dtype)
        for k in range(SSD_CONV_WIDTH):
            dw_ref[k:k + 1, :] = jnp.sum(dpre * shifted[k], axis=0, keepdims=True)
        db_ref[...] = jnp.sum(dpre, axis=0, keepdims=True)

    return _pcall(
        body, name=name, grid=(c // CONV_COLS,),
        in_specs=[pl.BlockSpec((t, CONV_COLS), lambda j: (0, cb0 + j)),
                  pl.BlockSpec((SSD_CONV_WIDTH, CONV_COLS), lambda j: (0, j)),
                  pl.BlockSpec((1, CONV_COLS), lambda j: (0, j))] +
                 [pl.BlockSpec((t, CONV_COLS), lambda j, lo=int(first[k]), hi=int(first[k + 1]): (0, jnp.clip(j, lo, hi - 1) - lo))
                  for k in range(len(douts))],
        out_specs=[pl.BlockSpec((t, CONV_COLS), lambda j: (0, j)),
                   pl.BlockSpec((SSD_CONV_WIDTH, CONV_COLS), lambda j: (0, j)),
                   pl.BlockSpec((1, CONV_COLS), lambda j: (0, j))],
        out_shape=[_sds((t, c), BF16), _sds((SSD_CONV_WIDTH, c)), _sds((1, c))],
        compiler_params=_params("arbitrary"),
    )(proj, w, b, *douts)


HALF = 256
HEADS_PER_HALF = 4
PAD_HEADS = 128


def _head_expanders():
    k = lax.broadcasted_iota(jnp.int32, (PAD_HEADS, HALF), 0)
    j = lax.broadcasted_iota(jnp.int32, (PAD_HEADS, HALF), 1)
    kt = lax.broadcasted_iota(jnp.int32, (HALF, PAD_HEADS), 1)
    jt = lax.broadcasted_iota(jnp.int32, (HALF, PAD_HEADS), 0)
    es, ets = [], []
    for half in range(2):
        es.append(jnp.where(k == j // SSD_HEAD_DIM + half * HEADS_PER_HALF, 1.0, 0.0).astype(F32))
        ets.append(jnp.where(kt == jt // SSD_HEAD_DIM + half * HEADS_PER_HALF, 1.0, 0.0).astype(F32))
    return es, ets


def _ssd_chunk(x_lo, x_hi, bm, cm, dtr, dtb8, alog8, dsk8, s_lo, s_hi):
    q = x_lo.shape[0]
    es, ets = _head_expanders()
    rowmean = lambda v: jnp.sum(v, axis=0, keepdims=True) * 0.125
    dt = _softplus(dtr + rowmean(dtb8))
    a = -jnp.exp(rowmean(alog8))
    adt = a * dt
    adt_tot8 = jnp.broadcast_to(jnp.sum(adt, axis=0, keepdims=True), (8, PAD_HEADS))
    ll = lax.broadcasted_iota(jnp.int32, (q, q), 0)
    ss = lax.broadcasted_iota(jnp.int32, (q, q), 1)
    ltri = jnp.where(ll >= ss, 1.0, 0.0).astype(F32)
    lane = lax.broadcasted_iota(jnp.int32, (1, HALF), 1)
    cb = _bdot_nt(cm, bm)
    outs = []
    for half, (x, s_in) in enumerate(((x_lo, s_lo), (x_hi, s_hi))):
        e, et = es[half], ets[half]
        dtf = _sel_right(dt, e)
        af = rowmean(_sel_right(jnp.broadcast_to(a, (8, PAD_HEADS)), e)) * dtf
        dskf = rowmean(_sel_right(dsk8, e))
        acum = _sel_left(ltri, af)
        alast = jnp.sum(af, axis=0, keepdims=True)
        xdt = x * dtf
        ydiag = jnp.zeros((q, HALF), F32)
        for r in range(HEADS_PER_HALF):
            sel = lane == r * SSD_HEAD_DIM
            ac_r = jnp.sum(jnp.where(sel, acum, 0.0), axis=1, keepdims=True)
            a_r = jnp.sum(jnp.where(sel, af, 0.0), axis=1, keepdims=True)
            arow = jnp.sum(jnp.where(ll <= ss, a_r, 0.0), axis=0, keepdims=True)
            decay = jnp.exp(jnp.where(ll >= ss, ac_r - arow, -jnp.inf))
            yh = _bdot_nn(cb * decay, xdt)
            ydiag = ydiag + jnp.where(lane // SSD_HEAD_DIM == r, yh, 0.0)
        st = _bdot_tn(xdt * jnp.exp(alast - acum), bm)
        yoff = _bdot_nt(cm, s_in) * jnp.exp(acum)
        y = ydiag + yoff + dskf * x
        alast_col = jnp.sum(_sel_left_nt(et, adt_tot8), axis=1, keepdims=True) * 0.125
        outs.append((y, jnp.exp(alast_col) * s_in + st))
    return outs[0][0], outs[1][0], outs[0][1], outs[1][1]


SSD_GP = 4


def _ssd_specs(t, rev):
    q, n = SSD_CHUNK, SSD_GP
    nc = t // q
    ci = (lambda c: nc - 1 - c) if rev else (lambda c: c)
    bcol0 = SSD_D_INNER // (n * SSD_STATE)
    return dict(
        x=pl.BlockSpec((q, n * 2 * HALF), lambda g, c: (ci(c), g)),
        bm=pl.BlockSpec((q, n * SSD_STATE), lambda g, c: (ci(c), bcol0 + g)),
        cm=pl.BlockSpec((q, n * SSD_STATE), lambda g, c: (ci(c), bcol0 + SSD_GROUPS // n + g)),
        dt=pl.BlockSpec((n, q, PAD_HEADS), lambda g, c: (g, ci(c), 0)),
        par=pl.BlockSpec((n, 8, PAD_HEADS), lambda g, c: (g, 0, 0)),
        st=pl.BlockSpec((None, n, 2, HALF, SSD_STATE), lambda g, c: (ci(c), g, 0, 0, 0)),
        grp=pl.BlockSpec((q, n * SSD_STATE), lambda g, c: (ci(c), g)),
    )


def _group_cols(k):
    lo = k * 2 * HALF
    return slice(lo, lo + HALF), slice(lo + HALF, lo + 2 * HALF), slice(k * SSD_STATE, (k + 1) * SSD_STATE)


def _ssd_fwd(xc, dt4, dtb, alog, dsk, *, name):
    t = xc.shape[0]
    nc = t // SSD_CHUNK
    sp = _ssd_specs(t, False)

    def body(x, bm, cm, dt, p0, p1, p2, y_ref, sin_ref, st_ref):
        @pl.when(pl.program_id(1) == 0)
        def _():
            st_ref[...] = jnp.zeros_like(st_ref)

        sin_ref[...] = st_ref[...]
        for k in range(SSD_GP):
            lo, hi, bc = _group_cols(k)
            y_lo, y_hi, so_lo, so_hi = _ssd_chunk(x[:, lo], x[:, hi], bm[:, bc], cm[:, bc], dt[k], p0[k], p1[k], p2[k],
                                                  st_ref[k, 0], st_ref[k, 1])
            y_ref[:, lo] = y_lo
            y_ref[:, hi] = y_hi
            st_ref[k, 0] = so_lo
            st_ref[k, 1] = so_hi

    return _pcall(
        body, name=name, grid=(SSD_GROUPS // SSD_GP, nc),
        in_specs=[sp['x'], sp['bm'], sp['cm'], sp['dt'], sp['par'], sp['par'], sp['par']],
        out_specs=[sp['x'], sp['st']],
        out_shape=[_sds((t, SSD_D_INNER)), _sds((nc, SSD_GROUPS, 2, HALF, SSD_STATE))],
        scratch_shapes=[pltpu.VMEM((SSD_GP, 2, HALF, SSD_STATE), F32)],
        compiler_params=_params("parallel", "arbitrary"),
    )(xc, xc, xc, dt4, dtb, alog, dsk)


def _ssd_bwd(xc, dt4, dtb, alog, dsk, sin, dy, *, name):
    t = xc.shape[0]
    nc = t // SSD_CHUNK
    sp = _ssd_specs(t, True)

    def body(x, bm, cm, dt, p0, p1, p2, sin_ref, dy_ref, dx_ref, db_ref, dc_ref, ddt_ref, dp0, dp1, dp2, dst_ref):
        @pl.when(pl.program_id(1) == 0)
        def _():
            dst_ref[...] = jnp.zeros_like(dst_ref)
            for ref in (dp0, dp1, dp2):
                ref[...] = jnp.zeros_like(ref)

        for k in range(SSD_GP):
            lo, hi, bc = _group_cols(k)
            _, vjp = jax.vjp(_ssd_chunk, x[:, lo], x[:, hi], bm[:, bc], cm[:, bc], dt[k], p0[k], p1[k], p2[k],
                             sin_ref[k, 0], sin_ref[k, 1])
            dxl, dxh, dbm, dcm, ddt, g0, g1, g2, ds_lo, ds_hi = vjp(
                (dy_ref[:, lo], dy_ref[:, hi], dst_ref[k, 0], dst_ref[k, 1]))
            dx_ref[:, lo] = dxl
            dx_ref[:, hi] = dxh
            db_ref[:, bc] = dbm
            dc_ref[:, bc] = dcm
            ddt_ref[k] = ddt
            dst_ref[k, 0] = ds_lo
            dst_ref[k, 1] = ds_hi
            for ref, g in ((dp0, g0), (dp1, g1), (dp2, g2)):
                ref[k] += jnp.broadcast_to(jnp.sum(g, axis=0, keepdims=True), g.shape)

    return _pcall(
        body, name=name, grid=(SSD_GROUPS // SSD_GP, nc),
        in_specs=[sp['x'], sp['bm'], sp['cm'], sp['dt'], sp['par'], sp['par'], sp['par'], sp['st'], sp['x']],
        out_specs=[sp['x'], sp['grp'], sp['grp'], sp['dt'], sp['par'], sp['par'], sp['par']],
        out_shape=[_sds((t, SSD_D_INNER)), _sds((t, SSD_GROUPS * SSD_STATE)), _sds((t, SSD_GROUPS * SSD_STATE)),
                   _sds((SSD_GROUPS, t, PAD_HEADS))] + [_sds((SSD_GROUPS, 8, PAD_HEADS))] * 3,
        scratch_shapes=[pltpu.VMEM((SSD_GP, 2, HALF, SSD_STATE), F32)],
        compiler_params=_params("parallel", "arbitrary"),
    )(xc, xc, xc, dt4, dtb, alog, dsk, sin, dy)


def _gatenorm(y, z, g):
    v = y * _silu(z)
    return v * lax.rsqrt(jnp.mean(v * v, axis=-1, keepdims=True) + RMS_EPS) * g


S5_CH = S5_WIDTH // S5_BLOCKS
S5_ST = S5_CH * S5_STATE // S5_GROUP
SCAN_UNROLL = 8


def _cmul(ar, ai, br, bi):
    return ar * br - ai * bi, ar * bi + ai * br


def _segment_power(ar, ai, n):
    assert n & (n - 1) == 0
    for _ in range(n.bit_length() - 1):
        ar, ai = _cmul(ar, ai, ar, ai)
    return ar, ai


def _carry_in(fr, fi, pr, pi, reverse):
    rows = lax.broadcasted_iota(jnp.int32, fr.shape, 0)
    cr = jnp.zeros_like(fr[0:1])
    ci = jnp.zeros_like(cr)
    outr = jnp.zeros_like(fr)
    outi = jnp.zeros_like(fr)
    order = range(6, -1, -1) if reverse else range(1, 8)
    for j in order:
        src = j + 1 if reverse else j - 1
        nr, ni = _cmul(pr[0:1], pi[0:1], cr, ci)
        cr, ci = nr + fr[src:src + 1], ni + fi[src:src + 1]
        outr = jnp.where(rows == j, cr, outr)
        outi = jnp.where(rows == j, ci, outi)
    return outr, outi


def _s5_specs(t):
    return dict(ch=pl.BlockSpec((t, S5_CH), lambda j: (0, j)), st=pl.BlockSpec((t, S5_ST), lambda j: (0, j)),
                lam=pl.BlockSpec((1, S5_ST), lambda j: (0, j)),
                b=pl.BlockSpec((None, S5_CH, S5_ST), lambda j: (j, 0, 0)),
                c=pl.BlockSpec((None, S5_ST, S5_CH), lambda j: (j, 0, 0)))


def _s5_fwd(u5, bre, bim, cre, cim, lr, li, *, name):
    t = u5.shape[0]
    nrt = t // 8

    def body(u_ref, bre_ref, bim_ref, cre_ref, cim_ref, lr_ref, li_ref, sr_ref, si_ref, y_ref, br_ref, bi_ref):
        u = u_ref[...]
        br_ref[...] = _dot(u, bre_ref[...], _NN)
        bi_ref[...] = _dot(u, bim_ref[...], _NN)
        ar = jnp.broadcast_to(lr_ref[...], (8, S5_ST))
        ai = jnp.broadcast_to(li_ref[...], (8, S5_ST))

        def step(r, s, store):
            rows = pl.ds(pl.multiple_of(r * 8, 8), 8)
            nr, ni = _cmul(ar, ai, s[0], s[1])
            nr, ni = nr + br_ref[rows, :], ni + bi_ref[rows, :]
            if store:
                sr_ref[rows, :] = nr
                si_ref[rows, :] = ni
            return nr, ni

        zero = (jnp.zeros((8, S5_ST), F32), jnp.zeros((8, S5_ST), F32))
        fr, fi = lax.fori_loop(0, nrt, lambda r, s: step(r, s, False), zero, unroll=SCAN_UNROLL)
        pr, pi = _segment_power(ar, ai, nrt)
        init = _carry_in(fr, fi, pr, pi, False)
        lax.fori_loop(0, nrt, lambda r, s: step(r, s, True), init, unroll=SCAN_UNROLL)
        y_ref[...] = _dot(sr_ref[...], cre_ref[...], _NN) - _dot(si_ref[...], cim_ref[...], _NN)

    sp = _s5_specs(t)
    w = S5_BLOCKS * S5_ST
    return _pcall(
        body, name=name, grid=(S5_BLOCKS,),
        in_specs=[sp['ch'], sp['b'], sp['b'], sp['c'], sp['c'], sp['lam'], sp['lam']],
        out_specs=[sp['st'], sp['st'], sp['ch']], out_shape=[_sds((t, w)), _sds((t, w)), _sds((t, S5_WIDTH))],
        scratch_shapes=[pltpu.VMEM((t, S5_ST), F32)] * 2, compiler_params=_params("parallel"),
    )(u5, bre, bim, cre, cim, lr, li)


def _s5_bwd(dy, du_direct, u5, sr, si, bre, bim, cre, cim, lr, li, *, name):
    t = u5.shape[0]
    nrt = t // 8

    def body(dy_ref, dd_ref, u_ref, sr_ref, si_ref, bre_ref, bim_ref, cre_ref, cim_ref, lr_ref, li_ref,
             du_ref, dbre_ref, dbim_ref, dcre_ref, dcim_ref, dlr_ref, dli_ref, gr_ref, gi_ref):
        dyv = dy_ref[...]
        gr_ref[...] = _dot(dyv, cre_ref[...], _NT)
        gi_ref[...] = -_dot(dyv, cim_ref[...], _NT)
        dcre_ref[...] = _dot(sr_ref[...], dyv, _TN)
        dcim_ref[...] = -_dot(si_ref[...], dyv, _TN)
        dr_ref, di_ref = gr_ref, gi_ref
        ar = jnp.broadcast_to(lr_ref[...], (8, S5_ST))
        ai = -jnp.broadcast_to(li_ref[...], (8, S5_ST))
        zero = jnp.zeros((8, S5_ST), F32)

        def step1(k, g):
            rows = pl.ds(pl.multiple_of((nrt - 1 - k) * 8, 8), 8)
            nr, ni = _cmul(ar, ai, g[0], g[1])
            return nr + dr_ref[rows, :], ni + di_ref[rows, :]

        fr, fi = lax.fori_loop(0, nrt, step1, (zero, zero), unroll=SCAN_UNROLL)
        pr, pi = _segment_power(ar, ai, nrt)
        init = _carry_in(fr, fi, pr, pi, True)

        def step2(k, carry):
            gr, gi, accr, acci = carry
            r = nrt - 1 - k
            rows = pl.ds(pl.multiple_of(r * 8, 8), 8)
            prev = pl.ds(pl.multiple_of(jnp.maximum(r - 1, 0) * 8, 8), 8)
            nr, ni = _cmul(ar, ai, gr, gi)
            nr, ni = nr + dr_ref[rows, :], ni + di_ref[rows, :]
            gr_ref[rows, :] = nr
            gi_ref[rows, :] = ni
            keep = jnp.where(r > 0, 1.0, 0.0)
            pr_, pi_ = sr_ref[prev, :] * keep, si_ref[prev, :] * keep
            return nr, ni, accr + (pr_ * nr + pi_ * ni), acci + (pr_ * ni - pi_ * nr)

        _, _, accr, acci = lax.fori_loop(0, nrt, step2, (init[0], init[1], zero, zero), unroll=SCAN_UNROLL)
        last = pl.ds((nrt - 1) * 8, 8)
        pr_, pi_ = _shift_down(sr_ref[last, :], 1), _shift_down(si_ref[last, :], 1)
        g0r, g0i = gr_ref[0:8, :], gi_ref[0:8, :]
        accr = accr + (pr_ * g0r + pi_ * g0i)
        acci = acci + (pr_ * g0i - pi_ * g0r)
        dlr_ref[...] = jnp.sum(accr, axis=0, keepdims=True)
        dli_ref[...] = jnp.sum(acci, axis=0, keepdims=True)
        u = u_ref[...]
        dbre_ref[...] = _dot(u, gr_ref[...], _TN)
        dbim_ref[...] = _dot(u, gi_ref[...], _TN)
        du = dd_ref[...] + _dot(gr_ref[...], bre_ref[...], _NT) + _dot(gi_ref[...], bim_ref[...], _NT)
        du_ref[...] = du.astype(du_ref.dtype)

    sp = _s5_specs(t)
    w = S5_BLOCKS * S5_ST
    return _pcall(
        body, name=name, grid=(S5_BLOCKS,),
        in_specs=[sp['ch'], sp['ch'], sp['ch'], sp['st'], sp['st'], sp['b'], sp['b'], sp['c'], sp['c'], sp['lam'], sp['lam']],
        out_specs=[sp['ch'], sp['b'], sp['b'], sp['c'], sp['c'], sp['lam'], sp['lam']],
        out_shape=[_sds((t, S5_WIDTH), BF16), _sds((S5_BLOCKS, S5_CH, S5_ST)), _sds((S5_BLOCKS, S5_CH, S5_ST)),
                   _sds((S5_BLOCKS, S5_ST, S5_CH)), _sds((S5_BLOCKS, S5_ST, S5_CH)), _sds((1, w)), _sds((1, w))],
        scratch_shapes=[pltpu.VMEM((t, S5_ST), F32)] * 2, compiler_params=_params("parallel"),
    )(dy, du_direct, u5, sr, si, bre, bim, cre, cim, lr, li)


def _s5_expander():
    n = lax.broadcasted_iota(jnp.int32, (S5_STATE, S5_STATE * S5_GROUP), 0)
    j = lax.broadcasted_iota(jnp.int32, (S5_STATE, S5_STATE * S5_GROUP), 1)
    return jnp.where(n == j // S5_GROUP, 1.0, 0.0).astype(F32)


def _s5_discretise(lam_re, lam_im, log_step, b_re, b_im):
    lr = jnp.minimum(lam_re, S5_MAX_REAL)
    step = jnp.exp(log_step)
    mag = jnp.exp(lr * step)
    ang = lam_im * step
    lbr, lbi = mag * jnp.cos(ang), mag * jnp.sin(ang)
    p, q = lbr - 1.0, lbi
    den = lr * lr + lam_im * lam_im
    cr, ci = (p * lr + q * lam_im) / den, (q * lr - p * lam_im) / den
    e = _s5_expander()
    cre, cie = _fdot(cr, e), _fdot(ci, e)
    return lbr, lbi, cre * b_re - cie * b_im, cre * b_im + cie * b_re


def _s5_params_fwd(lam_re, lam_im, log_step, b_re, b_im, *, name):
    g, n, w = lam_re.shape[0], S5_STATE, S5_STATE * S5_GROUP

    def body(a, b, c, d, e, o0, o1, o2, o3):
        for ref, val in zip((o0, o1, o2, o3), _s5_discretise(a[...], b[...], c[...], d[...], e[...])):
            ref[...] = val

    return _pcall(body, name=name, out_shape=[_sds((g, n)), _sds((g, n)), _sds((g, w)), _sds((g, w))])(
        lam_re, lam_im, log_step, b_re, b_im)


def _s5_params_bwd(lam_re, lam_im, log_step, b_re, b_im, cts, *, name):
    g, n, w = S5_GROUPS, S5_STATE, S5_STATE * S5_GROUP

    def body(a, b, c, d, e, c0, c1, c2, c3, o0, o1, o2, o3, o4):
        _, vjp = jax.vjp(_s5_discretise, a[...], b[...], c[...], d[...], e[...])
        for ref, val in zip((o0, o1, o2, o3, o4), vjp((c0[...], c1[...], c2[...], c3[...]))):
            ref[...] = val

    return _pcall(body, name=name,
                  out_shape=[_sds((g, n)), _sds((g, n)), _sds((g, 1)), _sds((g, w)), _sds((g, w))])(
        lam_re, lam_im, log_step, b_re, b_im, *cts)


def _s5_prepare(w):
    rows = DEPTH * S5_GROUPS
    lbr, lbi, bbr, bbi = _s5_params_fwd(
        w['s5_lambda_re'].reshape(rows, -1), w['s5_lambda_im'].reshape(rows, -1), w['s5_log_step'].reshape(rows, 1),
        w['s5_b_re'].reshape(rows, -1), w['s5_b_im'].reshape(rows, -1), name="s5_par")
    bd = lambda m: _blockdiag(m.reshape(rows, S5_STATE, S5_GROUP).transpose(0, 2, 1), S5_GROUP, S5_STATE).astype(BF16)
    cd = lambda m: _blockdiag(m.reshape(rows, S5_GROUP, S5_STATE).transpose(0, 2, 1), S5_STATE, S5_GROUP).astype(BF16)
    bre, bim, cre, cim = bd(bbr), bd(bbi), cd(w['s5_c_re']), cd(w['s5_c_im'])
    lr, li = lbr.reshape(DEPTH, 1, -1), lbi.reshape(DEPTH, 1, -1)
    blk = lambda a, i: a[i * S5_BLOCKS:(i + 1) * S5_BLOCKS]
    return [dict(bre=blk(bre, i), bim=blk(bim, i), cre=blk(cre, i), cim=blk(cim, i), lr=lr[i], li=li[i])
            for i in range(DEPTH)]


def _perm(a):
    t, c = a.shape
    return a.reshape(8, t // 8, c).transpose(1, 0, 2).reshape(t, c)


def _unperm(a):
    t, c = a.shape
    return a.reshape(t // 8, 8, c).transpose(1, 0, 2).reshape(t, c)


def _blockdiag(m, rows_inner, cols_inner):
    nblk = m.shape[0] // 8
    m = m.reshape(nblk, 8, rows_inner, cols_inner)
    eye = jnp.eye(8, dtype=m.dtype)
    out = m[:, :, :, None, :] * eye[None, :, None, :, None]
    return out.reshape(nblk, 8 * rows_inner, 8 * cols_inner)


def _blockdiag_extract(m, rows_inner, cols_inner):
    m = m.reshape(S5_BLOCKS, 8, rows_inner, 8, cols_inner)
    d = jnp.diagonal(m, axis1=1, axis2=3)
    return d.transpose(0, 3, 1, 2).reshape(S5_GROUPS, rows_inner, cols_inner)


Z0, XBC0, GA0, GB0 = 0, SSD_D_INNER, SSD_D_INNER + SSD_CONV_DIM, SSD_D_INNER + SSD_CONV_DIM + D_MODEL
BIG = GB0 + D_MODEL


def _mixer_fwd(h, p, tm):
    t = h.shape[0]
    nrow = t // tm
    u = _rms_fwd(h, p['pre_g'], name="mix_rms", tm=tm)
    u_p = _perm(u)
    proj = _mm(u, p['w_big'], name="mix_in")
    dtr = _mm(u, p['w_dt'], name="mix_in_dt")
    u5 = _mm(u_p, p['w_u5'], name="mix_in_s5")
    late, proj = lax.optimization_barrier((p['late'], proj))
    by_rows = lambda a: a.reshape(-1, a.shape[-1])
    p = dict(p, w_a=by_rows(late['w_branch_a']), w_b=by_rows(late['w_branch_b']), w_out=by_rows(late['w_out']),
             w_glu=late['s5_w_glu'][:, 0].transpose(1, 0, 2).reshape(late['s5_w_glu'].shape[2], -1))
    xc = _conv_fwd(proj, XBC0, p['conv_w'], p['conv_b'], name="ssd_conv")
    dt4 = jnp.pad(dtr.reshape(t, SSD_GROUPS, 8).transpose(1, 0, 2), ((0, 0), (0, 0), (0, PAD_HEADS - 8)))
    y_ssd, s_in = _ssd_fwd(xc, dt4, p['dt_bias8'], p['a_log8'], p['d8'], name="ssd_scan")
    gw = SSD_D_INNER // SSD_GROUPS
    ya = _rows(_gatenorm, name="ssd_gate", nrow=nrow, ncol=SSD_GROUPS,
               ins=[(y_ssd, _rspec(tm, gw, 0, True)), (proj, _rspec(tm, gw, Z0 // gw, True)),
                    (p['norm_g'], _bspec(gw, 0, True))],
               outs=[(_sds((t, SSD_D_INNER), BF16), _rspec(tm, gw, 0, True), False)])[0]
    y_a = _mm(ya, p['w_a'], name="mix_a")
    bre, bim, cre, cim, lr, li = (p['s5'][k] for k in ('bre', 'bim', 'cre', 'cim', 'lr', 'li'))
    sr, si, y5 = _s5_fwd(u5, bre, bim, cre, cim, lr, li, name="s5_scan")
    y5g = _rows(lambda a, b, d: _gelu(a + d * b), name="s5_act", nrow=nrow,
                ins=[(y5, _rspec(tm, S5_WIDTH)), (u5, _rspec(tm, S5_WIDTH)), (p['s5_d'], _bspec(S5_WIDTH))],
                outs=[(_sds((t, S5_WIDTH), BF16), _rspec(tm, S5_WIDTH), False)])[0]
    vg = _mm(y5g, p['w_glu'], name="s5_glu")
    ybin = _rows(lambda a, b: a * _sigmoid(b), name="s5_glu_act", nrow=nrow,
                 ins=[(vg, _rspec(tm, S5_WIDTH, 0)), (vg, _rspec(tm, S5_WIDTH, 1))],
                 outs=[(_sds((t, S5_WIDTH), BF16), _rspec(tm, S5_WIDTH), False)])[0]
    y_b = _unperm(_mm(ybin, p['w_b'], name="mix_b"))
    merged = _rows(lambda ga, gb, a, b: _sigmoid(ga) * a + _sigmoid(gb) * b, name="mix_merge", nrow=nrow,
                   ins=[(proj, _rspec(tm, D_MODEL, GA0 // D_MODEL)), (proj, _rspec(tm, D_MODEL, GB0 // D_MODEL)),
                        (y_a, _rspec(tm, D_MODEL)), (y_b, _rspec(tm, D_MODEL))],
                   outs=[(_sds((t, D_MODEL), BF16), _rspec(tm, D_MODEL), False)])[0]
    m = _mm(merged, p['w_out'], name="mix_out")
    out = _resid_fwd(h, m, p['post_g'], 1.0, name="mix_res", tm=tm)
    saved = dict(w_a=p['w_a'], w_b=p['w_b'], w_out=p['w_out'], w_glu=p['w_glu'], h=h, u=u, u_p=u_p, proj=proj, u5=u5, xc=xc, dt4=dt4, s_in=s_in, y_ssd=y_ssd, ya=ya, y_a=y_a,
                 bre=bre, bim=bim, cre=cre, cim=cim, lr=lr, li=li, sr=sr, si=si, y5=y5, y5g=y5g, vg=vg, ybin=ybin,
                 y_b=y_b, merged=merged, m=m)
    return out, saved


def _mixer_bwd(dh, p, s, tm, after_first):
    t = dh.shape[0]
    nrow = t // tm
    proj = s['proj']
    bufs = {}

    def grad_mm(a, b, wname, axis, name):
        dw = _mm(a, b, ta=True, name=name, out_dtype=BF16, col_shards=axis == 'cols')
        bufs[wname] = dw if axis == 'cols' else dw.reshape(N_DEV, 1, dw.shape[0] // N_DEV, dw.shape[1])

    dm, dpost = _resid_bwd(s['m'], p['post_g'], dh, 1.0, name="mix_res_bwd", tm=tm)
    dm = after_first(dm)
    dmerged = _mm(dm, s['w_out'], tb=True, name="mix_out_dx")
    grad_mm(s['merged'], dm, 'w_out', 'rows', "mix_out_dw")

    def merge_bwd(ga, gb, a, b, d):
        _, vjp = jax.vjp(lambda ga_, gb_, a_, b_: _sigmoid(ga_) * a_ + _sigmoid(gb_) * b_, ga, gb, a, b)
        dga, dgb, da, db = vjp(d)
        return jnp.concatenate([dga, dgb], axis=1), da, db

    dgab, dy_a, dy_b = _rows(
        merge_bwd, name="mix_merge_bwd", nrow=nrow,
        ins=[(proj, _rspec(tm, D_MODEL, GA0 // D_MODEL)), (proj, _rspec(tm, D_MODEL, GB0 // D_MODEL)),
             (s['y_a'], _rspec(tm, D_MODEL)), (s['y_b'], _rspec(tm, D_MODEL)), (dmerged, _rspec(tm, D_MODEL))],
        outs=[(_sds((t, 2 * D_MODEL), BF16), _rspec(tm, 2 * D_MODEL), False),
              (_sds((t, D_MODEL), BF16), _rspec(tm, D_MODEL), False),
              (_sds((t, D_MODEL), BF16), _rspec(tm, D_MODEL), False)])
    dya = _mm(dy_a, s['w_a'], tb=True, name="mix_a_dx")
    grad_mm(s['ya'], dy_a, 'w_branch_a', 'rows', "mix_a_dw")
    gw = SSD_D_INNER // SSD_GROUPS

    def gate_bwd(y, z, g, d):
        _, vjp = jax.vjp(_gatenorm, y, z, g)
        return vjp(d)

    dy_ssd, dz, dnorm = _rows(
        gate_bwd, name="ssd_gate_bwd", nrow=nrow, ncol=SSD_GROUPS,
        ins=[(s['y_ssd'], _rspec(tm, gw, 0, True)), (proj, _rspec(tm, gw, Z0 // gw, True)),
             (p['norm_g'], _bspec(gw, 0, True)), (dya, _rspec(tm, gw, 0, True))],
        outs=[(_sds((t, SSD_D_INNER)), _rspec(tm, gw, 0, True), False),
              (_sds((t, SSD_D_INNER), BF16), _rspec(tm, gw, 0, True), False),
              (_sds((1, SSD_D_INNER)), _bspec(gw, 0, True), True)])
    dxs, dbm, dcm, ddt4, ddtb, dalog, ddsk = _ssd_bwd(s['xc'], s['dt4'], p['dt_bias8'], p['a_log8'], p['d8'],
                                                      s['s_in'], dy_ssd, name="ssd_scan_bwd")
    dxbc, dconv_w, dconv_b = _conv_bwd(proj, XBC0, p['conv_w'], p['conv_b'], (dxs, dbm, dcm), name="ssd_conv_bwd")
    ddtr = ddt4[:, :, :8].transpose(1, 0, 2).reshape(t, SSD_HEADS)
    dy_bp = _perm(dy_b)
    dybin = _mm(dy_bp, s['w_b'], tb=True, name="mix_b_dx")
    grad_mm(s['ybin'], dy_bp, 'w_branch_b', 'rows', "mix_b_dw")

    def glu_bwd(a, b, d):
        _, vjp = jax.vjp(lambda a_, b_: a_ * _sigmoid(b_), a, b)
        da, db = vjp(d)
        return jnp.concatenate([da, db], axis=1)

    dvg = _rows(glu_bwd, name="s5_glu_act_bwd", nrow=nrow,
                ins=[(s['vg'], _rspec(tm, S5_WIDTH, 0)), (s['vg'], _rspec(tm, S5_WIDTH, 1)), (dybin, _rspec(tm, S5_WIDTH))],
                outs=[(_sds((t, 2 * S5_WIDTH), BF16), _rspec(tm, 2 * S5_WIDTH), False)])[0]
    dy5g = _mm(dvg, s['w_glu'], tb=True, name="s5_glu_dx")
    grad_mm(s['y5g'], dvg, 's5_w_glu', 'cols', "s5_glu_dw")

    def act_bwd(a, b, d, g):
        _, vjp = jax.vjp(lambda a_, b_, d_: _gelu(a_ + d_ * b_), a, b, d)
        return vjp(g)

    dy5, du5_direct, ds5d = _rows(
        act_bwd, name="s5_act_bwd", nrow=nrow,
        ins=[(s['y5'], _rspec(tm, S5_WIDTH)), (s['u5'], _rspec(tm, S5_WIDTH)), (p['s5_d'], _bspec(S5_WIDTH)),
             (dy5g, _rspec(tm, S5_WIDTH))],
        outs=[(_sds((t, S5_WIDTH), BF16), _rspec(tm, S5_WIDTH), False), (_sds((t, S5_WIDTH)), _rspec(tm, S5_WIDTH), False),
              (_sds((1, S5_WIDTH)), _bspec(S5_WIDTH), True)])
    du5, dbre, dbim, dcre, dcim, dlr, dli = _s5_bwd(dy5, du5_direct, s['u5'], s['sr'], s['si'], s['bre'], s['bim'],
                                                     s['cre'], s['cim'], s['lr'], s['li'], name="s5_scan_bwd")
    du_p = _mm(du5, p['w_u5'], tb=True, name="mix_in_s5_dx")
    dw_u5 = _mm(s['u_p'], du5, ta=True, name="mix_in_s5_dw", out_dtype=BF16)
    ext_b = lambda m: _blockdiag_extract(m, S5_GROUP, S5_STATE).transpose(0, 2, 1).reshape(S5_GROUPS, S5_STATE * S5_GROUP)
    dlam_re, dlam_im, dlog_step, db_re, db_im = _s5_params_bwd(
        p['lam_re'], p['lam_im'], p['log_step'], p['b_re'], p['b_im'],
        (dlr.reshape(S5_GROUPS, S5_STATE), dli.reshape(S5_GROUPS, S5_STATE), ext_b(dbre), ext_b(dbim)), name="s5_par_bwd")
    dc_re = _blockdiag_extract(dcre, S5_STATE, S5_GROUP).transpose(0, 2, 1)
    dc_im = _blockdiag_extract(dcim, S5_STATE, S5_GROUP).transpose(0, 2, 1)
    dproj = jnp.concatenate([dz, dxbc, dgab], axis=1)
    du_big = _mm(dproj, p['w_big'], tb=True, name="mix_in_dx")
    du_dt = _mm(ddtr, p['w_dt'], tb=True, name="mix_in_dt_dx")
    dw_big = _mm(s['u'], dproj, ta=True, name="mix_in_dw", out_dtype=BF16)
    dw_dt = _mm(s['u'], ddtr, ta=True, name="mix_in_dt_dw", out_dtype=BF16)
    dh_in, dpre = _rms_bwd(s['h'], p['pre_g'], dh, [du_big, du_dt, _unperm(du_p)], name="mix_rms_bwd", tm=tm)
    dw_in = jnp.concatenate([dw_big[:, :GA0], dw_dt, dw_u5, dw_big[:, GA0:]], axis=1)
    bufs['w_in'] = dw_in.reshape(D_MODEL, N_DEV, -1).transpose(1, 0, 2)[:, None]
    grads = {
        'mix_pre_g': dpre, 'mix_post_g': dpost, 'ssd_conv_w': dconv_w, 'ssd_conv_b': dconv_b,
        'ssd_dt_bias': ddtb[:, 0, :8].reshape(-1), 'ssd_a_log': dalog[:, 0, :8].reshape(-1),
        'ssd_d': ddsk[:, 0, :8].reshape(-1), 'ssd_norm_g': dnorm,
        's5_lambda_re': dlam_re, 's5_lambda_im': dlam_im,
        's5_b_re': db_re.reshape(S5_GROUPS, S5_STATE, S5_GROUP), 's5_b_im': db_im.reshape(S5_GROUPS, S5_STATE, S5_GROUP),
        's5_c_re': dc_re, 's5_c_im': dc_im, 's5_log_step': dlog_step.reshape(-1), 's5_d': ds5d,
    }
    return dh_in, bufs, grads


HBM_SPEC = pl.BlockSpec(memory_space=pltpu.HBM)


def _place():
    return lax.axis_index("x"), lax.axis_index("y"), lax.axis_index("c")


GATHER_COLLECTIVE_ID = 1


def _all_gather(shards, *, name, on_sequencer=False):
    n = len(shards)

    def body(*refs):
        x_refs, out_refs = refs[:n], refs[n:2 * n]
        send_sems, recv_sems, local_sems = refs[2 * n:]
        x, y, c = _place()
        me, sibling = (x, y, c), (x, y, 1 - c)
        chips = [(1 - x, y), (x, 1 - y), (1 - x, 1 - y)]
        if on_sequencer:
            _handshake([sibling] + [(*chip, c) for chip in chips])

        def slot(o, px, py, pc):
            return out_refs[o].at[4 * px + 2 * py + pc]

        def copy(o, k, block, to, src=None):
            return pltpu.make_async_remote_copy(
                src_ref=slot(o, *block) if src is None else src, dst_ref=slot(o, *block),
                send_sem=send_sems.at[7 * o + k], recv_sem=recv_sems.at[7 * o + k], device_id=to, device_id_type=MESH)

        mine = [pltpu.make_async_copy(x_refs[o], slot(o, *me), local_sems.at[o]) for o in range(n)]
        for cp in mine:
            cp.start()
        first = []
        for j, chip in enumerate(chips):
            first += [copy(o, 1 + j, me, (*chip, c), src=x_refs[o]) for o in range(n)]
        first += [copy(o, 0, me, sibling, src=x_refs[o]) for o in range(n)]
        for cp in first:
            cp.start()
        passed = []
        for j, chip in enumerate(chips):
            for o in range(n):
                copy(o, 1 + j, (*chip, c), me).wait_recv()
                passed.append(copy(o, 4 + j, (*chip, c), sibling))
                passed[-1].start()
        for o in range(n):
            copy(o, 0, sibling, me).wait_recv()
        for j, chip in enumerate(chips):
            for o in range(n):
                copy(o, 4 + j, (*chip, 1 - c), me).wait_recv()
        for cp in first + passed:
            cp.wait_send()
        for cp in mine:
            cp.wait()

    out_shape = [jax.ShapeDtypeStruct((N_DEV,) + s.shape, s.dtype) for s in shards]
    sems = [pltpu.SemaphoreType.DMA((7 * n,)), pltpu.SemaphoreType.DMA((7 * n,)), pltpu.SemaphoreType.DMA((n,))]
    if on_sequencer:
        return _scall(body, name=name, out_type=out_shape, scratch_types=sems, collective_id=GATHER_COLLECTIVE_ID)(*shards)
    return _pcall(body, name=name, in_specs=[HBM_SPEC] * n, out_specs=[HBM_SPEC] * n, out_shape=out_shape,
                  scratch_shapes=sems)(*shards)


N_CHIPS = 4


SIBLING_COLLECTIVE_ID = 2
CHIPS_COLLECTIVE_ID = 3


def _handshake(peers):
    barrier = pltpu.get_barrier_semaphore()
    for peer in peers:
        pl.semaphore_signal(barrier, inc=1, device_id=peer, device_id_type=MESH)
    pl.semaphore_wait(barrier, len(peers))


def _exchange_sibling(grads, *, name):
    n = len(grads)

    def body(*refs):
        p_refs, q_refs = refs[:n], refs[n:2 * n]
        send_sems, recv_sems = refs[2 * n:]
        x, y, c = _place()
        _handshake([(x, y, 1 - c)])
        copies = [pltpu.make_async_remote_copy(
            src_ref=p_refs[o].at[k, 1 - c], dst_ref=q_refs[o].at[k], send_sem=send_sems.at[N_CHIPS * o + k],
            recv_sem=recv_sems.at[N_CHIPS * o + k], device_id=(x, y, 1 - c), device_id_type=MESH)
            for o in range(n) for k in range(N_CHIPS)]
        for cp in copies:
            cp.start()
        for cp in copies:
            cp.wait()

    return _scall(
        body, name=name, out_type=[jax.ShapeDtypeStruct((N_CHIPS,) + g.shape[2:], g.dtype) for g in grads],
        scratch_types=[pltpu.SemaphoreType.DMA((N_CHIPS * n,)), pltpu.SemaphoreType.DMA((N_CHIPS * n,))],
        collective_id=SIBLING_COLLECTIVE_ID,
    )(*grads)


def _pair_sum(own, got, *, name):
    _, _, r, l = own.shape
    tr = _tile(r, 512, 16)
    c = lax.axis_index("c").astype(jnp.int32).reshape(1)

    def body(c_ref, p_ref, q_ref, o_ref):
        o_ref[...] = (p_ref[...].astype(F32) + q_ref[...].astype(F32)).astype(o_ref.dtype)

    return _pcall(
        body, name=name,
        grid_spec=pltpu.PrefetchScalarGridSpec(
            num_scalar_prefetch=1, grid=(N_CHIPS, r // tr),
            in_specs=[pl.BlockSpec((None, None, tr, l), lambda k, i, cr: (k, cr[0], i, 0)),
                      pl.BlockSpec((None, tr, l), lambda k, i, cr: (k, i, 0))],
            out_specs=pl.BlockSpec((None, tr, l), lambda k, i, cr: (k, i, 0))),
        out_shape=jax.ShapeDtypeStruct((N_CHIPS, r, l), own.dtype),
        compiler_params=_params("parallel", "parallel"),
    )(c, own, got)


def _exchange_chips(parts, *, name):
    n = len(parts)

    def body(*refs):
        p_refs, g_refs = refs[:n], refs[n:2 * n]
        send_sems, recv_sems, local_sems = refs[2 * n:]
        x, y, c = _place()
        mine = 2 * x + y
        chips = [(1 - x, y), (x, 1 - y), (1 - x, 1 - y)]
        _handshake([(*chip, c) for chip in chips])
        own = [pltpu.make_async_copy(p_refs[o].at[mine], g_refs[o].at[mine], local_sems.at[o]) for o in range(n)]
        for cp in own:
            cp.start()
        copies = []
        for j, (px, py) in enumerate(chips):
            copies += [pltpu.make_async_remote_copy(
                src_ref=p_refs[o].at[2 * px + py], dst_ref=g_refs[o].at[mine], send_sem=send_sems.at[3 * o + j],
                recv_sem=recv_sems.at[3 * o + j], device_id=(px, py, c), device_id_type=MESH) for o in range(n)]
        for cp in copies:
            cp.start()
        for cp in copies:
            cp.wait()
        for cp in own:
            cp.wait()

    return _scall(
        body, name=name, out_type=[jax.ShapeDtypeStruct(p.shape, p.dtype) for p in parts],
        scratch_types=[pltpu.SemaphoreType.DMA((3 * n,)), pltpu.SemaphoreType.DMA((3 * n,)), pltpu.SemaphoreType.DMA((n,))],
        collective_id=CHIPS_COLLECTIVE_ID,
    )(*parts)


def _sum_slots(g, *, name):
    n, r, l = g.shape
    tr = _tile(r, 512, 16)

    def body(g_ref, o_ref):
        acc = g_ref[0].astype(F32)
        for k in range(1, n):
            acc = acc + g_ref[k].astype(F32)
        o_ref[...] = acc

    return _pcall(
        body, name=name, grid=(r // tr,), in_specs=[pl.BlockSpec((n, tr, l), lambda i: (0, i, 0))],
        out_specs=pl.BlockSpec((tr, l), lambda i: (i, 0)), out_shape=_sds((r, l)),
        compiler_params=_params("parallel"),
    )(g)


TRANSPOSED = ('ffn1_w_gate', 'ffn1_w_up', 'ffn2_w_gate', 'ffn2_w_up')
GATHER_CHUNKS = (('ffn1', ['ffn1_w_gate', 'ffn1_w_up']), ('ffn1_down', ['ffn1_w_down']),
                 ('mix_in', ['w_in', 'ssd_conv_w']), ('mix', ['w_branch_a', 's5_w_glu', 'w_branch_b', 'w_out']),
                 ('ffn2', ['ffn2_w_gate', 'ffn2_w_up']), ('ffn2_down', ['ffn2_w_down']))
LATE = ('ffn1_w_down', 'ffn2_w_down', 'w_branch_a', 's5_w_glu', 'w_branch_b', 'w_out')
SUBLAYERS = (('ffn1', ['ffn1_w_gate', 'ffn1_w_up', 'ffn1_w_down']),
             ('mix', ['w_in', 'ssd_conv_w', 'w_branch_a', 's5_w_glu', 'w_branch_b', 'w_out']),
             ('ffn2', ['ffn2_w_gate', 'ffn2_w_up', 'ffn2_w_down']))


def _gather_weights(w):
    layers, first = [], None
    for i in range(DEPTH):
        g = {}
        for tag, names in GATHER_CHUNKS:
            shards =[w[n][i:i + 1] if n == 'ssd_conv_w' else
                      (w[n][i:i + 1].transpose(0, 2, 1) if n in TRANSPOSED else w[n][i:i + 1]).astype(BF16) for n in names]
            if first is None:
                first = got = _all_gather(shards, name=f"gather_{tag}")
            else:
                shards, first = lax.optimization_barrier((shards, first))
                got = _all_gather(shards, name=f"gather_{tag}", on_sequencer=True)
            g.update(zip(names, got))
        layers.append(g)
    layers[0].update(zip(GATHER_CHUNKS[0][1], first))
    return layers


class _ReduceScatter:
    @staticmethod
    def sibling(tag, bufs):
        names = list(bufs)
        own = [bufs[n].reshape((N_CHIPS, 2) + bufs[n].shape[1:]) for n in names]
        return (tag, names), (own, _exchange_sibling(own, name=f"reduce_sibling_{tag}"))

    @staticmethod
    def chips(meta, arrays):
        (tag, names), (own, got) = meta, arrays
        flat = lambda a, lead: a.reshape(lead + (-1, a.shape[-1]))
        parts = [_pair_sum(flat(o, (N_CHIPS, 2)), flat(g, (N_CHIPS,)), name=f"reduce_pair_sum_{n}").reshape(g.shape)
                 for n, o, g in zip(names, own, got)]
        return names, _exchange_chips(parts, name=f"reduce_chips_{tag}")

    @staticmethod
    def done(names, slots):
        return dict(zip(names, slots))

    @staticmethod
    def small(grads):
        return _reduce_small(grads)


def _reduce_small(grads):
    flat = jnp.concatenate([g.astype(F32).reshape(-1) for g in grads.values()])
    pad = (-flat.shape[0]) % (8 * LANES)
    flat = jnp.concatenate([flat, jnp.zeros((pad,), F32)]).reshape(-1, LANES)
    gathered = _all_gather([flat], name="gather_small_grads", on_sequencer=True)[0]
    total = _sum_slots(gathered, name="sum_small_grads").reshape(-1)
    out, o = {}, 0
    for n, g in grads.items():
        out[n] = total[o:o + g.size].reshape(g.shape)
        o += g.size
    return out


def _adamw(w, g, m, v, *, name, slots=False):
    shape = w.shape
    if slots:
        lyr, rows, lanes = shape
        w2, m2, v2 = w, m, v
        tr = _tile(rows, 256, 16)
        nrt = rows // tr
        grid = (lyr, nrt)
        spec = pl.BlockSpec((None, tr, lanes), lambda l, i: (l, i, 0))
        g_specs = [pl.BlockSpec((N_CHIPS, None, tr, lanes),
                                lambda l, i, k=k: (0, 0, jnp.where(l == k, i, jnp.where(l > k, nrt - 1, 0)), 0))
                   for k in range(lyr)]
        g_args = list(g)
        out_shape = [_sds(shape)] * 4
    else:
        lanes = shape[-1] if (shape[-1] >= 128 or w.size % LANES) else LANES
        as2d = lambda a: a.reshape(-1, lanes)
        w2, m2, v2 = as2d(w), as2d(m), as2d(v)
        r = w2.shape[0]
        tr = _tile(r, 256, 8)
        grid = (1, r // tr)
        spec = pl.BlockSpec((tr, lanes), lambda l, i: (i, 0))
        g_specs, g_args = [spec], [as2d(g)]
        out_shape = [_sds((r, lanes))] * 4
    n_g = len(g_args)

    def body(w_ref, *rest):
        g_refs = rest[:n_g]
        m_ref, v_ref, go_ref, d_ref, mo_ref, vo_ref = rest[n_g:]
        if slots:
            gg = None
            for k, g_ref in enumerate(g_refs):
                tot = g_ref[0].astype(F32)
                for c in range(1, N_CHIPS):
                    tot = tot + g_ref[c].astype(F32)
                gg = tot if gg is None else jnp.where(pl.program_id(0) == k, tot, gg)
        else:
            gg = g_refs[0][...]
        go_ref[...] = gg
        mn = ADAM_B1 * m_ref[...] + (1.0 - ADAM_B1) * gg
        vn = ADAM_B2 * v_ref[...] + (1.0 - ADAM_B2) * (gg * gg)
        m_hat = mn / (1.0 - ADAM_B1 ** ADAM_STEP)
        v_hat = vn / (1.0 - ADAM_B2 ** ADAM_STEP)
        d_ref[...] = -ADAM_LR * (m_hat / (jnp.sqrt(v_hat) + ADAM_EPS) + ADAM_WD * w_ref[...])
        mo_ref[...] = mn
        vo_ref[...] = vn

    res = _pcall(
        body, name=name, grid=grid, in_specs=[spec] + g_specs + [spec, spec], out_specs=[spec] * 4,
        out_shape=out_shape, compiler_params=_params("arbitrary", "arbitrary"),
    )(w2, *g_args, m2, v2)
    return tuple(a.reshape(shape) for a in res)


def _sublayer_params(w, g, i, k, s5):
    row = lambda a: a.astype(F32).reshape(1, -1)
    if k != 'mix':
        return dict(layer=i, pre_g=row(w[f'{k}_pre_g'][i]), post_g=row(w[f'{k}_post_g'][i]),
                    w_gate=g[f'{k}_w_gate'], w_up=g[f'{k}_w_up'], w_down=g[f'{k}_w_down'])
    head8 = lambda a: jnp.broadcast_to(
        jnp.pad(a.astype(F32).reshape(SSD_GROUPS, 1, 8), ((0, 0), (0, 0), (0, PAD_HEADS - 8))), (SSD_GROUPS, 8, PAD_HEADS))
    by_cols = lambda n: g[n][:, 0].transpose(1, 0, 2).reshape(g[n].shape[2], -1)
    w_in = by_cols('w_in')
    s = np.cumsum([SSD_D_INNER, SSD_CONV_DIM, SSD_HEADS, S5_WIDTH, D_MODEL])
    return dict(
        layer=i, s5=s5, pre_g=row(w['mix_pre_g'][i]), post_g=row(w['mix_post_g'][i]),
        w_big=jnp.concatenate([w_in[:, :s[1]], w_in[:, s[3]:]], axis=1), w_dt=w_in[:, s[1]:s[2]], w_u5=w_in[:, s[2]:s[3]],
        conv_w=by_cols('ssd_conv_w'), conv_b=row(w['ssd_conv_b'][i]),
        dt_bias8=head8(w['ssd_dt_bias'][i]), a_log8=head8(w['ssd_a_log'][i]), d8=head8(w['ssd_d'][i]),
        norm_g=row(w['ssd_norm_g'][i]), late={n: g[n] for n in SUBLAYERS[1][1] if n in LATE},
        lam_re=w['s5_lambda_re'][i], lam_im=w['s5_lambda_im'][i], log_step=w['s5_log_step'][i].reshape(S5_GROUPS, 1),
        b_re=w['s5_b_re'][i].reshape(S5_GROUPS, -1), b_im=w['s5_b_im'][i].reshape(S5_GROUPS, -1),
        c_re=w['s5_c_re'][i], c_im=w['s5_c_im'][i], s5_d=row(w['s5_d'][i]),
    )


def _loss_head(h, target, *, tm):
    t, d = h.shape

    def fn(y, tgt):
        err = y - tgt
        return err * (1.0 / d), jnp.sum(0.5 * jnp.sum(err * err, axis=-1, keepdims=True) * (1.0 / d), axis=0, keepdims=True)

    dy, loss = _rows(fn, name="loss_head", nrow=t // tm,
                     ins=[(h, _rspec(tm, d)), (target, _rspec(tm, d))],
                     outs=[(_sds((t, d)), _rspec(tm, d), False), (_sds((1, 128)), _bspec(128), True)])
    return dy, loss[0, 0]


def _forward_backward(h, target, w, g, rs):
    t = h.shape[0]
    tm = _tile(t, 512, 8)
    s5 = None
    layers, saved = [], []
    for i in range(DEPTH):
        gi, ps, ss = dict(g[i]), [], []
        for tag, names in SUBLAYERS:
            if tag == 'mix' and s5 is None:
                mine = {n: w[n] for n in WEIGHTS if n.startswith('s5_') and n not in SHARDED}
                mine, h = lax.optimization_barrier((mine, h))
                s5 = _s5_prepare(mine)
            early = [n for n in names if n not in LATE]
            tied, h, s5 = lax.optimization_barrier(([gi[n] for n in early], h, s5))
            gi.update(zip(early, tied))
            p = _sublayer_params(w, gi, i, tag, s5[i] if tag == 'mix' else None)
            h, s = _mixer_fwd(h, p, tm) if tag == 'mix' else _ffn_fwd(h, p, tag, tm)
            ps.append(p)
            ss.append(s)
        layers.append(ps)
        saved.append(ss)
    dh, loss = _loss_head(h, target, tm=tm)
    reduced, small = [{} for _ in range(DEPTH)], [{} for _ in range(DEPTH)]
    in_sibling, in_chips = None, None

    def start_chips(x):
        nonlocal in_sibling, in_chips
        if in_sibling is not None:
            layer, meta, arrays = in_sibling
            arrays, x = lax.optimization_barrier((arrays, x))
            in_sibling, in_chips = None, (layer,) + tuple(rs.chips(meta, arrays))
        return x

    def finish_chips(x):
        nonlocal in_chips
        if in_chips is not None:
            layer, names, slots = in_chips
            slots, x = lax.optimization_barrier((slots, x))
            reduced[layer].update(rs.done(names, slots))
            in_chips = None
        return x

    for i in reversed(range(DEPTH)):
        for k in reversed(range(len(SUBLAYERS))):
            tag = SUBLAYERS[k][0]
            if tag == 'mix':
                dh, bufs, grads = _mixer_bwd(dh, layers[i][k], saved[i][k], tm, start_chips)
            else:
                dh, bufs, grads = _ffn_bwd(dh, layers[i][k], saved[i][k], tag, tm, start_chips)
            small[i].update(grads)
            dh = finish_chips(dh)
            in_sibling = (i,) + tuple(rs.sibling(tag, bufs))
            if tag == 'mix' and i + 1 < DEPTH:
                small[i + 1], dh = lax.optimization_barrier((small[i + 1], dh))
        if i == 0:
            small[i]['loss'] = loss.reshape(1)
        small[i] = rs.small(small[i])
    loss = small[0].pop('loss')[0]
    dh = finish_chips(start_chips(dh))
    shapes = {n: (w[n].shape[:-1] + (SSD_CONV_DIM,) if n == 'ssd_conv_w' else w[n].shape) for n in SMALL_ORDER}
    stacked = {n: jnp.stack([small[i][n].reshape(shapes[n][1:]) for i in range(DEPTH)]) for n in SMALL_ORDER}
    return loss, dh, reduced, stacked


def kernel(*args):
    n_w = len(WEIGHTS)
    x, target = args[0], args[1 + n_w]
    w = dict(zip(WEIGHTS, args[1:1 + n_w]))
    m = dict(zip(WEIGHTS, args[2 + n_w:2 + 2 * n_w]))
    v = dict(zip(WEIGHTS, args[2 + 2 * n_w:2 + 3 * n_w]))
    t = x.shape[1]

    g = _gather_weights(w)
    loss, dx, slots, small = _forward_backward(x.reshape(t, D_MODEL), target.reshape(t, D_MODEL), w, g, _ReduceScatter)
    me = 4 * lax.axis_index("x") + 2 * lax.axis_index("y") + lax.axis_index("c")
    cols = w['ssd_conv_w'].shape[-1]
    small['ssd_conv_w'] = lax.dynamic_slice_in_dim(small['ssd_conv_w'], me * cols, cols, axis=2)

    grad, delta, new_m, new_v = {}, {}, {}, {}
    for n in WEIGHTS:
        sharded = n in slots[0]
        view = (lambda a: a.transpose(0, 2, 1)) if n in TRANSPOSED else (lambda a: a)
        res = _adamw(view(w[n]), [slots[i][n] for i in range(DEPTH)] if sharded else small[n], view(m[n]), view(v[n]),
                     name=f"adamw_{n}", slots=sharded)
        grad[n], delta[n], new_m[n], new_v[n] = (view(a) for a in res)
    return (loss, dx.reshape(x.shape), *[grad[n] for n in WEIGHTS], *[delta[n] for n in WEIGHTS],
            *[new_m[n] for n in WEIGHTS], *[new_v[n] for n in WEIGHTS])
```

```python
import math

import numpy as np
import jax
import jax.numpy as jnp
from jax import lax
from jax.experimental import pallas as pl
from jax.experimental.pallas import tpu as pltpu
from jax.experimental.pallas import tpu_sc as plsc

F32 = jnp.float32
BF16 = jnp.bfloat16
MESH = pl.DeviceIdType.MESH
HIGHEST = lax.Precision.HIGHEST

D_MODEL = 1024
DEPTH = 2
FFN_HIDDEN = 2816
SSD_D_INNER = 2048
SSD_HEADS = 32
SSD_HEAD_DIM = 64
SSD_GROUPS = 4
SSD_STATE = 128
SSD_CHUNK = 128
SSD_CONV_DIM = 3072
SSD_CONV_WIDTH = 4
S5_WIDTH = 1024
S5_GROUP = 16
S5_GROUPS = 64
S5_STATE = 64
S5_MAX_REAL = -1e-4
S5_BLOCKS = 8
RMS_EPS = 1e-6
N_DEV = 8
LANES = 1024

ADAM_LR = 0.001
ADAM_B1 = 0.9
ADAM_B2 = 0.999
ADAM_EPS = 1e-08
ADAM_WD = 0.01
ADAM_STEP = 10

VMEM_LIMIT_BYTES = 48 * 1024 * 1024

WEIGHTS = ['ffn1_pre_g', 'ffn1_post_g', 'ffn1_w_gate', 'ffn1_w_up', 'ffn1_w_down', 'mix_pre_g', 'mix_post_g',
           'w_in', 'ssd_conv_w', 'ssd_conv_b', 'ssd_dt_bias', 'ssd_a_log', 'ssd_d', 'ssd_norm_g', 'w_branch_a',
           's5_lambda_re', 's5_lambda_im', 's5_b_re', 's5_b_im', 's5_c_re', 's5_c_im', 's5_log_step', 's5_d',
           's5_w_glu', 'w_branch_b', 'w_out', 'ffn2_pre_g', 'ffn2_post_g', 'ffn2_w_gate', 'ffn2_w_up',
           'ffn2_w_down']
SHARDED = {'ffn1_w_gate': 2, 'ffn1_w_up': 2, 'ffn1_w_down': 1, 'w_in': 2, 'ssd_conv_w': 2, 'w_branch_a': 1,
           's5_w_glu': 2, 'w_branch_b': 1, 'w_out': 1, 'ffn2_w_gate': 2, 'ffn2_w_up': 2, 'ffn2_w_down': 1}
SHARDED_ORDER = [n for n in WEIGHTS if n in SHARDED]
SMALL_ORDER = [n for n in WEIGHTS if n not in SHARDED or n == 'ssd_conv_w']


def _pcall(body, **kw):
    return pl.pallas_call(body, **kw)


def _scall(body, *, name, out_type, scratch_types, collective_id):
    return pl.kernel(body, out_type=out_type, mesh=plsc.ScalarSubcoreMesh(axis_name="sequencer", num_cores=1),
                     scratch_types=scratch_types, name=name,
                     compiler_params=pltpu.CompilerParams(collective_id=collective_id))


def _params(*sem):
    return pltpu.CompilerParams(dimension_semantics=sem, vmem_limit_bytes=VMEM_LIMIT_BYTES)


def _tile(n, pref, align=128):
    if n <= pref:
        return n
    t = (pref // align) * align
    while t >= align:
        if n % t == 0:
            return t
        t -= align
    return n


def _rms(x, g):
    return x * lax.rsqrt(jnp.mean(x * x, axis=-1, keepdims=True) + RMS_EPS) * g


def _sigmoid(x):
    return 1.0 / (1.0 + jnp.exp(-x))


def _silu(x):
    return x * _sigmoid(x)


def _gelu(x):
    return 0.5 * x * (1.0 + jnp.tanh(math.sqrt(2.0 / math.pi) * (x + 0.044715 * (x * x * x))))


def _softplus(x):
    return jnp.maximum(x, 0.0) + jnp.log(1.0 + jnp.exp(-jnp.abs(x)))


def _dot(a, b, dims):
    return lax.dot_general(a.astype(BF16), b.astype(BF16), (dims, ((), ())), preferred_element_type=F32)


_NN = ((1,), (0,))
_NT = ((1,), (1,))
_TN = ((0,), (0,))


@jax.custom_vjp
def _bdot_nn(a, b):
    return _dot(a, b, _NN)


_bdot_nn.defvjp(lambda a, b: (_dot(a, b, _NN), (a, b)),
                lambda r, g: (_dot(g, r[1], _NT), _dot(r[0], g, _TN)))


@jax.custom_vjp
def _bdot_nt(a, b):
    return _dot(a, b, _NT)


_bdot_nt.defvjp(lambda a, b: (_dot(a, b, _NT), (a, b)),
                lambda r, g: (_dot(g, r[1], _NN), _dot(g, r[0], _TN)))


@jax.custom_vjp
def _bdot_tn(a, b):
    return _dot(a, b, _TN)


_bdot_tn.defvjp(lambda a, b: (_dot(a, b, _TN), (a, b)),
                lambda r, g: (_dot(r[1], g, _NT), _dot(r[0], g, _NN)))


def _fdot(a, b, dims=_NN):
    return lax.dot_general(a, b, (dims, ((), ())), precision=HIGHEST, preferred_element_type=F32)


def _sel3(x, sel, dims, x_first):
    p1 = x.astype(BF16)
    r1 = x - p1.astype(F32)
    p2 = r1.astype(BF16)
    p3 = (r1 - p2.astype(F32)).astype(BF16)
    sel = sel.astype(BF16)
    out = None
    for piece in (p1, p2, p3):
        d = lax.dot_general(*((piece, sel) if x_first else (sel, piece)), (dims, ((), ())), preferred_element_type=F32)
        out = d if out is None else out + d
    return out


@jax.custom_vjp
def _sel_right(x, sel):
    return _sel3(x, sel, _NN, True)


_sel_right.defvjp(lambda x, sel: (_sel3(x, sel, _NN, True), sel),
                  lambda sel, g: (_sel3(g, sel, _NT, True), jnp.zeros_like(sel)))


@jax.custom_vjp
def _sel_left(sel, x):
    return _sel3(x, sel, _NN, False)


_sel_left.defvjp(lambda sel, x: (_sel3(x, sel, _NN, False), sel),
                 lambda sel, g: (jnp.zeros_like(sel), _sel3(g, sel, _TN, False)))


@jax.custom_vjp
def _sel_left_nt(sel, x):
    return _sel3(x, sel, _NT, False)


_sel_left_nt.defvjp(lambda sel, x: (_sel3(x, sel, _NT, False), sel),
                    lambda sel, g: (jnp.zeros_like(sel), _sel3(g, sel, _TN, True)))


def _mm(a, b, *, name, ta=False, tb=False, out_dtype=F32, tm=2048, tn=512, tk=2048, col_shards=False):
    m, k = (a.shape[1], a.shape[0]) if ta else a.shape
    n = b.shape[0] if tb else b.shape[1]
    assert k == (b.shape[1] if tb else b.shape[0]), (a.shape, b.shape, ta, tb)
    if col_shards:
        tn = n // N_DEV
    tm, tn, tk = _tile(m, tm), _tile(n, tn), _tile(k, tk)
    nk = k // tk
    a_spec = pl.BlockSpec((tk, tm), lambda i, j, kk: (kk, i)) if ta else pl.BlockSpec((tm, tk), lambda i, j, kk: (i, kk))
    b_spec = pl.BlockSpec((tn, tk), lambda i, j, kk: (j, kk)) if tb else pl.BlockSpec((tk, tn), lambda i, j, kk: (kk, j))
    dims = ((0 if ta else 1,), (1 if tb else 0,))
    out_spec = pl.BlockSpec((tm, tn), lambda i, j, kk: (i, j))
    out_shape = jax.ShapeDtypeStruct((m, n), out_dtype)
    if col_shards:
        out_shape = jax.ShapeDtypeStruct((N_DEV, 1, m, n // N_DEV), out_dtype)
        out_spec = pl.BlockSpec((None, None, tm, tn), lambda i, j, kk: (j, 0, i, 0))

    def body(a_ref, b_ref, o_ref, acc_ref):
        kk = pl.program_id(2)

        @pl.when(kk == 0)
        def _():
            acc_ref[...] = jnp.zeros_like(acc_ref)

        acc_ref[...] += _dot(a_ref[...], b_ref[...], dims)

        @pl.when(kk == nk - 1)
        def _():
            o_ref[...] = acc_ref[...].astype(o_ref.dtype)

    return _pcall(
        body, name=name, grid=(m // tm, n // tn, nk),
        in_specs=[a_spec, b_spec], out_specs=out_spec, out_shape=out_shape,
        scratch_shapes=[pltpu.VMEM((tm, tn), F32)],
        compiler_params=_params("parallel", "parallel", "arbitrary"),
    )(a, b)


def _rspec(tm, w, cb=0, percol=False):
    return pl.BlockSpec((tm, w), (lambda j, i: (i, cb + j)) if percol else (lambda j, i: (i, cb)))


def _bspec(w, cb=0, percol=False, rows=1):
    return pl.BlockSpec((rows, w), (lambda j, i: (0, cb + j)) if percol else (lambda j, i: (0, cb)))


def _rows(fn, *, name, nrow, ncol=1, ins, outs):
    n_in = len(ins)
    accs = [o[2] for o in outs]

    def body(*refs):
        vals = fn(*[r[...] for r in refs[:n_in]])
        if not isinstance(vals, (tuple, list)):
            vals = (vals,)
        i = pl.program_id(1)
        for ref, val, acc in zip(refs[n_in:], vals, accs):
            if acc:
                @pl.when(i == 0)
                def _(ref=ref):
                    ref[...] = jnp.zeros_like(ref)

                ref[...] += jnp.broadcast_to(val, ref.shape).astype(ref.dtype)
            else:
                ref[...] = val.astype(ref.dtype)

    res = _pcall(
        body, name=name, grid=(ncol, nrow),
        in_specs=[s for _, s in ins], out_specs=[o[1] for o in outs], out_shape=[o[0] for o in outs],
        compiler_params=_params("parallel", "arbitrary"),
    )(*[a for a, _ in ins])
    return res


def _sds(shape, dtype=F32):
    return jax.ShapeDtypeStruct(shape, dtype)


def _rms_fwd(h, g, *, name, tm):
    t, d = h.shape
    return _rows(lambda x, gg: _rms(x, gg), name=name, nrow=t // tm,
                 ins=[(h, _rspec(tm, d)), (g, _bspec(d))],
                 outs=[(_sds((t, d), BF16), _rspec(tm, d), False)])[0]


def _resid_fwd(h, f, g, scale, *, name, tm):
    t, d = h.shape
    return _rows(lambda x, ff, gg: x + scale * _rms(ff, gg), name=name, nrow=t // tm,
                 ins=[(h, _rspec(tm, d)), (f, _rspec(tm, d)), (g, _bspec(d))],
                 outs=[(_sds((t, d)), _rspec(tm, d), False)])[0]


def _resid_bwd(f, g, dh, scale, *, name, tm):
    t, d = f.shape

    def fn(ff, gg, dd):
        _, vjp = jax.vjp(lambda a, b: scale * _rms(a, b), ff, gg)
        return vjp(dd)

    return _rows(fn, name=name, nrow=t // tm,
                 ins=[(f, _rspec(tm, d)), (g, _bspec(d)), (dh, _rspec(tm, d))],
                 outs=[(_sds((t, d), BF16), _rspec(tm, d), False), (_sds((1, d)), _bspec(d), True)])


def _rms_bwd(h, g, dh, dxns, *, name, tm):
    t, d = h.shape

    def fn(x, gg, dd, *dx):
        _, vjp = jax.vjp(_rms, x, gg)
        tot = dx[0]
        for more in dx[1:]:
            tot = tot + more
        dxx, dg = vjp(tot)
        return dd + dxx, dg

    return _rows(fn, name=name, nrow=t // tm,
                 ins=[(h, _rspec(tm, d)), (g, _bspec(d)), (dh, _rspec(tm, d))] + [(x, _rspec(tm, d)) for x in dxns],
                 outs=[(_sds((t, d)), _rspec(tm, d), False), (_sds((1, d)), _bspec(d), True)])


FFN_BLOCKS = 4
NB = FFN_HIDDEN // FFN_BLOCKS
MM_ROWS = 2048


def _ffn_up(xn, wg, wu, *, name):
    t = xn.shape[0]
    tm = _tile(t, MM_ROWS // 2)
    wspec = pl.BlockSpec((None, None, NB, D_MODEL), lambda i, j: (j, 0, 0, 0))

    def body(x_ref, g_ref, u_ref, ab_ref, hh_ref):
        x = x_ref[...]
        a, b = _dot(x, g_ref[...], _NT), _dot(x, u_ref[...], _NT)
        ab_ref[0] = a.astype(ab_ref.dtype)
        ab_ref[1] = b.astype(ab_ref.dtype)
        hh_ref[...] = (_silu(a) * b).astype(hh_ref.dtype)

    return _pcall(
        body, name=name, grid=(t // tm, FFN_BLOCKS),
        in_specs=[pl.BlockSpec((tm, D_MODEL), lambda i, j: (i, 0)), wspec, wspec],
        out_specs=[pl.BlockSpec((None, 2, tm, NB), lambda i, j: (j, 0, i, 0)),
                   pl.BlockSpec((None, tm, NB), lambda i, j: (j, i, 0))],
        out_shape=[_sds((FFN_BLOCKS, 2, t, NB), BF16), _sds((FFN_BLOCKS, t, NB), BF16)],
        compiler_params=_params("parallel", "parallel"),
    )(xn, wg, wu)


def _ffn_down(hh, wd, *, name):
    t = hh.shape[1]
    tm = _tile(t, 512)

    def body(h_ref, w_ref, o_ref):
        acc = _dot(h_ref[0], w_ref[0, 0], _NN)
        for k in range(1, FFN_BLOCKS):
            acc = acc + _dot(h_ref[k], w_ref[k, 0], _NN)
        o_ref[...] = acc

    return _pcall(
        body, name=name, grid=(t // tm,),
        in_specs=[pl.BlockSpec((FFN_BLOCKS, tm, NB), lambda i: (0, i, 0)),
                  pl.BlockSpec((FFN_BLOCKS, 1, NB, D_MODEL), lambda i: (0, 0, 0, 0))],
        out_specs=pl.BlockSpec((tm, D_MODEL), lambda i: (i, 0)), out_shape=_sds((t, D_MODEL)),
        compiler_params=_params("parallel"),
    )(hh, wd)


def _ffn_down_dx(df, wd, ab, *, name):
    t = df.shape[0]
    tm = _tile(t, MM_ROWS // 2)

    def body(d_ref, w_ref, ab_ref, o_ref):
        dhh = _dot(d_ref[...], w_ref[...], _NT)
        _, vjp = jax.vjp(lambda a, b: _silu(a) * b, ab_ref[0].astype(F32), ab_ref[1].astype(F32))
        da, db = vjp(dhh)
        o_ref[0] = da.astype(o_ref.dtype)
        o_ref[1] = db.astype(o_ref.dtype)

    blk = pl.BlockSpec((None, 2, tm, NB), lambda i, j: (j, 0, i, 0))
    return _pcall(
        body, name=name, grid=(t // tm, FFN_BLOCKS),
        in_specs=[pl.BlockSpec((tm, D_MODEL), lambda i, j: (i, 0)),
                  pl.BlockSpec((None, None, NB, D_MODEL), lambda i, j: (j, 0, 0, 0)), blk],
        out_specs=blk, out_shape=_sds((FFN_BLOCKS, 2, t, NB), BF16), compiler_params=_params("parallel", "parallel"),
    )(df, wd, ab)


def _ffn_down_dw(hh, df, *, name, tn=512):
    t = df.shape[0]
    tk = _tile(t, 2048)
    nk = t // tk

    def body(h_ref, d_ref, o_ref, acc_ref):
        kk = pl.program_id(2)

        @pl.when(kk == 0)
        def _():
            acc_ref[...] = jnp.zeros_like(acc_ref)

        acc_ref[...] += _dot(h_ref[...], d_ref[...], _TN)

        @pl.when(kk == nk - 1)
        def _():
            o_ref[...] = acc_ref[...].astype(o_ref.dtype)

    return _pcall(
        body, name=name, grid=(FFN_BLOCKS, D_MODEL // tn, nk),
        in_specs=[pl.BlockSpec((None, tk, NB), lambda j, n, kk: (j, kk, 0)),
                  pl.BlockSpec((tk, tn), lambda j, n, kk: (kk, n))],
        out_specs=pl.BlockSpec((None, None, NB, tn), lambda j, n, kk: (j, 0, 0, n)),
        out_shape=_sds((FFN_BLOCKS, 1, NB, D_MODEL), BF16),
        scratch_shapes=[pltpu.VMEM((NB, tn), F32)],
        compiler_params=_params("parallel", "parallel", "arbitrary"),
    )(hh, df)


def _ffn_up_dx(dab, wg, wu, *, name):
    t = dab.shape[2]
    tm = _tile(t, MM_ROWS // 2)
    wspec = pl.BlockSpec((None, None, NB, D_MODEL), lambda i, j: (j, 0, 0, 0))

    def body(d_ref, g_ref, u_ref, o_ref):
        @pl.when(pl.program_id(1) == 0)
        def _():
            o_ref[...] = jnp.zeros_like(o_ref)

        o_ref[...] += _dot(d_ref[0], g_ref[...], _NN) + _dot(d_ref[1], u_ref[...], _NN)

    return _pcall(
        body, name=name, grid=(t // tm, FFN_BLOCKS),
        in_specs=[pl.BlockSpec((None, 2, tm, NB), lambda i, j: (j, 0, i, 0)), wspec, wspec],
        out_specs=pl.BlockSpec((tm, D_MODEL), lambda i, j: (i, 0)), out_shape=_sds((t, D_MODEL)),
        compiler_params=_params("parallel", "arbitrary"),
    )(dab, wg, wu)


def _ffn_up_dw(xn, dab, *, name):
    t = xn.shape[0]

    def body(x_ref, d_ref, og_ref, ou_ref):
        x = x_ref[...]
        og_ref[...] = _dot(d_ref[0], x, _TN).astype(og_ref.dtype)
        ou_ref[...] = _dot(d_ref[1], x, _TN).astype(ou_ref.dtype)

    out = pl.BlockSpec((None, None, NB, D_MODEL), lambda j: (j, 0, 0, 0))
    return _pcall(
        body, name=name, grid=(FFN_BLOCKS,),
        in_specs=[pl.BlockSpec((t, D_MODEL), lambda j: (0, 0)), pl.BlockSpec((None, 2, t, NB), lambda j: (j, 0, 0, 0))],
        out_specs=[out, out], out_shape=[_sds((FFN_BLOCKS, 1, NB, D_MODEL), BF16)] * 2,
        compiler_params=_params("parallel"),
    )(xn, dab)


def _paired(a):
    return a.reshape(FFN_BLOCKS, 1, NB, D_MODEL)


def _ffn_fwd(h, p, tag, tm):
    xn = _rms_fwd(h, p['pre_g'], name=f"{tag}_rms", tm=tm)
    ab, hh = _ffn_up(xn, _paired(p['w_gate']), _paired(p['w_up']), name=f"{tag}_up")
    w_down, hh = lax.optimization_barrier((p['w_down'], hh))
    f = _ffn_down(hh, _paired(w_down), name=f"{tag}_down")
    out = _resid_fwd(h, f, p['post_g'], 0.5, name=f"{tag}_res", tm=tm)
    return out, (h, xn, ab, hh, f)


def _ffn_bwd(dh, p, saved, tag, tm, after_first):
    h, xn, ab, hh, f = saved
    df, dpost = _resid_bwd(f, p['post_g'], dh, 0.5, name=f"{tag}_res_bwd", tm=tm)
    df = after_first(df)
    dab = _ffn_down_dx(df, _paired(p['w_down']), ab, name=f"{tag}_down_dx")
    bufs = {f'{tag}_w_down': _ffn_down_dw(hh, df, name=f"{tag}_down_dw")}
    dxn = _ffn_up_dx(dab, _paired(p['w_gate']), _paired(p['w_up']), name=f"{tag}_up_dx")
    bufs[f'{tag}_w_gate'], bufs[f'{tag}_w_up'] = _ffn_up_dw(xn, dab, name=f"{tag}_up_dw")
    bufs = {n: a.reshape(N_DEV, 1, FFN_HIDDEN // N_DEV, D_MODEL) for n, a in bufs.items()}
    dh_in, dpre = _rms_bwd(h, p['pre_g'], dh, [dxn], name=f"{tag}_rms_bwd", tm=tm)
    return dh_in, bufs, {f'{tag}_pre_g': dpre, f'{tag}_post_g': dpost}


CONV_COLS = 256


def _shift_down(x, s):
    rows = lax.broadcasted_iota(jnp.int32, x.shape, 0)
    return jnp.where(rows >= s, pltpu.roll(x, s, axis=0), 0.0)


def _shift_up(x, s):
    t = x.shape[0]
    rows = lax.broadcasted_iota(jnp.int32, x.shape, 0)
    return jnp.where(rows < t - s, pltpu.roll(x, t - s, axis=0), 0.0)


def _conv_fwd(proj, col0, w, b, *, name):
    t = proj.shape[0]
    c = w.shape[1]
    cb0 = col0 // CONV_COLS

    def body(x_ref, w_ref, b_ref, o_ref):
        x = x_ref[...]
        acc = x * w_ref[3:4, :] + b_ref[...]
        for k in range(SSD_CONV_WIDTH - 1):
            acc = acc + _shift_down(x, SSD_CONV_WIDTH - 1 - k) * w_ref[k:k + 1, :]
        o_ref[...] = _silu(acc)

    return _pcall(
        body, name=name, grid=(c // CONV_COLS,),
        in_specs=[pl.BlockSpec((t, CONV_COLS), lambda j: (0, cb0 + j)),
                  pl.BlockSpec((SSD_CONV_WIDTH, CONV_COLS), lambda j: (0, j)),
                  pl.BlockSpec((1, CONV_COLS), lambda j: (0, j))],
        out_specs=pl.BlockSpec((t, CONV_COLS), lambda j: (0, j)),
        out_shape=_sds((t, c)), compiler_params=_params("parallel"),
    )(proj, w, b)


def _conv_bwd(proj, col0, w, b, douts, *, name):
    t = proj.shape[0]
    c = w.shape[1]
    cb0 = col0 // CONV_COLS
    first = np.cumsum([0] + [d.shape[1] // CONV_COLS for d in douts])

    def body(x_ref, w_ref, b_ref, *rest):
        d_refs, (dx_ref, dw_ref, db_ref) = rest[:len(douts)], rest[len(douts):]
        j = pl.program_id(0)
        dout = d_refs[-1][...]
        for k in range(len(douts) - 2, -1, -1):
            dout = jnp.where(j < int(first[k + 1]), d_refs[k][...], dout)
        x = x_ref[...]
        shifted = [_shift_down(x, SSD_CONV_WIDTH - 1 - k) for k in range(SSD_CONV_WIDTH - 1)] + [x]
        pre = b_ref[...] + shifted[3] * w_ref[3:4, :]
        for k in range(SSD_CONV_WIDTH - 1):
            pre = pre + shifted[k] * w_ref[k:k + 1, :]
        sg = _sigmoid(pre)
        dpre = dout * (sg * (1.0 + pre * (1.0 - sg)))
        dx = dpre * w_ref[3:4, :]
        for k in range(SSD_CONV_WIDTH - 1):
            dx = dx + _shift_up(dpre, SSD_CONV_WIDTH - 1 - k) * w_ref[k:k + 1, :]
        dx_ref[...] = dx.astype(dx_ref.dtype)
        for k in range(SSD_CONV_WIDTH):
            dw_ref[k:k + 1, :] = jnp.sum(dpre * shifted[k], axis=0, keepdims=True)
        db_ref[...] = jnp.sum(dpre, axis=0, keepdims=True)

    return _pcall(
        body, name=name, grid=(c // CONV_COLS,),
        in_specs=[pl.BlockSpec((t, CONV_COLS), lambda j: (0, cb0 + j)),
                  pl.BlockSpec((SSD_CONV_WIDTH, CONV_COLS), lambda j: (0, j)),
                  pl.BlockSpec((1, CONV_COLS), lambda j: (0, j))] +
                 [pl.BlockSpec((t, CONV_COLS), lambda j, lo=int(first[k]), hi=int(first[k + 1]): (0, jnp.clip(j, lo, hi - 1) - lo))
                  for k in range(len(douts))],
        out_specs=[pl.BlockSpec((t, CONV_COLS), lambda j: (0, j)),
                   pl.BlockSpec((SSD_CONV_WIDTH, CONV_COLS), lambda j: (0, j)),
                   pl.BlockSpec((1, CONV_COLS), lambda j: (0, j))],
        out_shape=[_sds((t, c), BF16), _sds((SSD_CONV_WIDTH, c)), _sds((1, c))],
        compiler_params=_params("arbitrary"),
    )(proj, w, b, *douts)


HALF = 256
HEADS_PER_HALF = 4
PAD_HEADS = 128


def _head_expanders():
    k = lax.broadcasted_iota(jnp.int32, (PAD_HEADS, HALF), 0)
    j = lax.broadcasted_iota(jnp.int32, (PAD_HEADS, HALF), 1)
    kt = lax.broadcasted_iota(jnp.int32, (HALF, PAD_HEADS), 1)
    jt = lax.broadcasted_iota(jnp.int32, (HALF, PAD_HEADS), 0)
    es, ets = [], []
    for half in range(2):
        es.append(jnp.where(k == j // SSD_HEAD_DIM + half * HEADS_PER_HALF, 1.0, 0.0).astype(F32))
        ets.append(jnp.where(kt == jt // SSD_HEAD_DIM + half * HEADS_PER_HALF, 1.0, 0.0).astype(F32))
    return es, ets


def _ssd_chunk(x_lo, x_hi, bm, cm, dtr, dtb8, alog8, dsk8, s_lo, s_hi):
    q = x_lo.shape[0]
    es, ets = _head_expanders()
    rowmean = lambda v: jnp.sum(v, axis=0, keepdims=True) * 0.125
    dt = _softplus(dtr + rowmean(dtb8))
    a = -jnp.exp(rowmean(alog8))
    adt = a * dt
    adt_tot8 = jnp.broadcast_to(jnp.sum(adt, axis=0, keepdims=True), (8, PAD_HEADS))
    ll = lax.broadcasted_iota(jnp.int32, (q, q), 0)
    ss = lax.broadcasted_iota(jnp.int32, (q, q), 1)
    ltri = jnp.where(ll >= ss, 1.0, 0.0).astype(F32)
    lane = lax.broadcasted_iota(jnp.int32, (1, HALF), 1)
    cb = _bdot_nt(cm, bm)
    outs = []
    for half, (x, s_in) in enumerate(((x_lo, s_lo), (x_hi, s_hi))):
        e, et = es[half], ets[half]
        dtf = _sel_right(dt, e)
        af = rowmean(_sel_right(jnp.broadcast_to(a, (8, PAD_HEADS)), e)) * dtf
        dskf = rowmean(_sel_right(dsk8, e))
        acum = _sel_left(ltri, af)
        alast = jnp.sum(af, axis=0, keepdims=True)
        xdt = x * dtf
        ydiag = jnp.zeros((q, HALF), F32)
        for r in range(HEADS_PER_HALF):
            sel = lane == r * SSD_HEAD_DIM
            ac_r = jnp.sum(jnp.where(sel, acum, 0.0), axis=1, keepdims=True)
            a_r = jnp.sum(jnp.where(sel, af, 0.0), axis=1, keepdims=True)
            arow = jnp.sum(jnp.where(ll <= ss, a_r, 0.0), axis=0, keepdims=True)
            decay = jnp.exp(jnp.where(ll >= ss, ac_r - arow, -jnp.inf))
            yh = _bdot_nn(cb * decay, xdt)
            ydiag = ydiag + jnp.where(lane // SSD_HEAD_DIM == r, yh, 0.0)
        st = _bdot_tn(xdt * jnp.exp(alast - acum), bm)
        yoff = _bdot_nt(cm, s_in) * jnp.exp(acum)
        y = ydiag + yoff + dskf * x
        alast_col = jnp.sum(_sel_left_nt(et, adt_tot8), axis=1, keepdims=True) * 0.125
        outs.append((y, jnp.exp(alast_col) * s_in + st))
    return outs[0][0], outs[1][0], outs[0][1], outs[1][1]


SSD_GP = 4


def _ssd_specs(t, rev):
    q, n = SSD_CHUNK, SSD_GP
    nc = t // q
    ci = (lambda c: nc - 1 - c) if rev else (lambda c: c)
    bcol0 = SSD_D_INNER // (n * SSD_STATE)
    return dict(
        x=pl.BlockSpec((q, n * 2 * HALF), lambda g, c: (ci(c), g)),
        bm=pl.BlockSpec((q, n * SSD_STATE), lambda g, c: (ci(c), bcol0 + g)),
        cm=pl.BlockSpec((q, n * SSD_STATE), lambda g, c: (ci(c), bcol0 + SSD_GROUPS // n + g)),
        dt=pl.BlockSpec((n, q, PAD_HEADS), lambda g, c: (g, ci(c), 0)),
        par=pl.BlockSpec((n, 8, PAD_HEADS), lambda g, c: (g, 0, 0)),
        st=pl.BlockSpec((None, n, 2, HALF, SSD_STATE), lambda g, c: (ci(c), g, 0, 0, 0)),
        grp=pl.BlockSpec((q, n * SSD_STATE), lambda g, c: (ci(c), g)),
    )


def _group_cols(k):
    lo = k * 2 * HALF
    return slice(lo, lo + HALF), slice(lo + HALF, lo + 2 * HALF), slice(k * SSD_STATE, (k + 1) * SSD_STATE)


def _ssd_fwd(xc, dt4, dtb, alog, dsk, *, name):
    t = xc.shape[0]
    nc = t // SSD_CHUNK
    sp = _ssd_specs(t, False)

    def body(x, bm, cm, dt, p0, p1, p2, y_ref, sin_ref, st_ref):
        @pl.when(pl.program_id(1) == 0)
        def _():
            st_ref[...] = jnp.zeros_like(st_ref)

        sin_ref[...] = st_ref[...]
        for k in range(SSD_GP):
            lo, hi, bc = _group_cols(k)
            y_lo, y_hi, so_lo, so_hi = _ssd_chunk(x[:, lo], x[:, hi], bm[:, bc], cm[:, bc], dt[k], p0[k], p1[k], p2[k],
                                                  st_ref[k, 0], st_ref[k, 1])
            y_ref[:, lo] = y_lo
            y_ref[:, hi] = y_hi
            st_ref[k, 0] = so_lo
            st_ref[k, 1] = so_hi

    return _pcall(
        body, name=name, grid=(SSD_GROUPS // SSD_GP, nc),
        in_specs=[sp['x'], sp['bm'], sp['cm'], sp['dt'], sp['par'], sp['par'], sp['par']],
        out_specs=[sp['x'], sp['st']],
        out_shape=[_sds((t, SSD_D_INNER)), _sds((nc, SSD_GROUPS, 2, HALF, SSD_STATE))],
        scratch_shapes=[pltpu.VMEM((SSD_GP, 2, HALF, SSD_STATE), F32)],
        compiler_params=_params("parallel", "arbitrary"),
    )(xc, xc, xc, dt4, dtb, alog, dsk)


def _ssd_bwd(xc, dt4, dtb, alog, dsk, sin, dy, *, name):
    t = xc.shape[0]
    nc = t // SSD_CHUNK
    sp = _ssd_specs(t, True)

    def body(x, bm, cm, dt, p0, p1, p2, sin_ref, dy_ref, dx_ref, db_ref, dc_ref, ddt_ref, dp0, dp1, dp2, dst_ref):
        @pl.when(pl.program_id(1) == 0)
        def _():
            dst_ref[...] = jnp.zeros_like(dst_ref)
            for ref in (dp0, dp1, dp2):
                ref[...] = jnp.zeros_like(ref)

        for k in range(SSD_GP):
            lo, hi, bc = _group_cols(k)
            _, vjp = jax.vjp(_ssd_chunk, x[:, lo], x[:, hi], bm[:, bc], cm[:, bc], dt[k], p0[k], p1[k], p2[k],
                             sin_ref[k, 0], sin_ref[k, 1])
            dxl, dxh, dbm, dcm, ddt, g0, g1, g2, ds_lo, ds_hi = vjp(
                (dy_ref[:, lo], dy_ref[:, hi], dst_ref[k, 0], dst_ref[k, 1]))
            dx_ref[:, lo] = dxl
            dx_ref[:, hi] = dxh
            db_ref[:, bc] = dbm
            dc_ref[:, bc] = dcm
            ddt_ref[k] = ddt
            dst_ref[k, 0] = ds_lo
            dst_ref[k, 1] = ds_hi
            for ref, g in ((dp0, g0), (dp1, g1), (dp2, g2)):
                ref[k] += jnp.broadcast_to(jnp.sum(g, axis=0, keepdims=True), g.shape)

    return _pcall(
        body, name=name, grid=(SSD_GROUPS // SSD_GP, nc),
        in_specs=[sp['x'], sp['bm'], sp['cm'], sp['dt'], sp['par'], sp['par'], sp['par'], sp['st'], sp['x']],
        out_specs=[sp['x'], sp['grp'], sp['grp'], sp['dt'], sp['par'], sp['par'], sp['par']],
        out_shape=[_sds((t, SSD_D_INNER)), _sds((t, SSD_GROUPS * SSD_STATE)), _sds((t, SSD_GROUPS * SSD_STATE)),
                   _sds((SSD_GROUPS, t, PAD_HEADS))] + [_sds((SSD_GROUPS, 8, PAD_HEADS))] * 3,
        scratch_shapes=[pltpu.VMEM((SSD_GP, 2, HALF, SSD_STATE), F32)],
        compiler_params=_params("parallel", "arbitrary"),
    )(xc, xc, xc, dt4, dtb, alog, dsk, sin, dy)


def _gatenorm(y, z, g):
    v = y * _silu(z)
    return v * lax.rsqrt(jnp.mean(v * v, axis=-1, keepdims=True) + RMS_EPS) * g


S5_CH = S5_WIDTH // S5_BLOCKS
S5_ST = S5_CH * S5_STATE // S5_GROUP
SCAN_UNROLL = 16


def _cmul(ar, ai, br, bi):
    return ar * br - ai * bi, ar * bi + ai * br


def _segment_power(ar, ai, n):
    assert n & (n - 1) == 0
    for _ in range(n.bit_length() - 1):
        ar, ai = _cmul(ar, ai, ar, ai)
    return ar, ai


def _carry_in(fr, fi, pr, pi, reverse):
    rows = lax.broadcasted_iota(jnp.int32, fr.shape, 0)
    cr = jnp.zeros_like(fr[0:1])
    ci = jnp.zeros_like(cr)
    outr = jnp.zeros_like(fr)
    outi = jnp.zeros_like(fr)
    order = range(6, -1, -1) if reverse else range(1, 8)
    for j in order:
        src = j + 1 if reverse else j - 1
        nr, ni = _cmul(pr[0:1], pi[0:1], cr, ci)
        cr, ci = nr + fr[src:src + 1], ni + fi[src:src + 1]
        outr = jnp.where(rows == j, cr, outr)
        outi = jnp.where(rows == j, ci, outi)
    return outr, outi


def _s5_specs(t):
    return dict(ch=pl.BlockSpec((t, S5_CH), lambda j: (0, j)), st=pl.BlockSpec((t, S5_ST), lambda j: (0, j)),
                lam=pl.BlockSpec((1, S5_ST), lambda j: (0, j)),
                b=pl.BlockSpec((None, S5_CH, S5_ST), lambda j: (j, 0, 0)),
                c=pl.BlockSpec((None, S5_ST, S5_CH), lambda j: (j, 0, 0)))


def _s5_fwd(u5, bre, bim, cre, cim, lr, li, *, name):
    t = u5.shape[0]
    nrt = t // 8

    def body(u_ref, bre_ref, bim_ref, cre_ref, cim_ref, lr_ref, li_ref, sr_ref, si_ref, y_ref, br_ref, bi_ref):
        u = u_ref[...]
        br_ref[...] = _dot(u, bre_ref[...], _NN)
        bi_ref[...] = _dot(u, bim_ref[...], _NN)
        ar = jnp.broadcast_to(lr_ref[...], (8, S5_ST))
        ai = jnp.broadcast_to(li_ref[...], (8, S5_ST))

        def step(r, s, store):
            rows = pl.ds(pl.multiple_of(r * 8, 8), 8)
            nr, ni = _cmul(ar, ai, s[0], s[1])
            nr, ni = nr + br_ref[rows, :], ni + bi_ref[rows, :]
            if store:
                sr_ref[rows, :] = nr
                si_ref[rows, :] = ni
            return nr, ni

        zero = (jnp.zeros((8, S5_ST), F32), jnp.zeros((8, S5_ST), F32))
        fr, fi = lax.fori_loop(0, nrt, lambda r, s: step(r, s, False), zero, unroll=SCAN_UNROLL)
        pr, pi = _segment_power(ar, ai, nrt)
        init = _carry_in(fr, fi, pr, pi, False)
        lax.fori_loop(0, nrt, lambda r, s: step(r, s, True), init, unroll=SCAN_UNROLL)
        y_ref[...] = _dot(sr_ref[...], cre_ref[...], _NN) - _dot(si_ref[...], cim_ref[...], _NN)

    sp = _s5_specs(t)
    w = S5_BLOCKS * S5_ST
    return _pcall(
        body, name=name, grid=(S5_BLOCKS,),
        in_specs=[sp['ch'], sp['b'], sp['b'], sp['c'], sp['c'], sp['lam'], sp['lam']],
        out_specs=[sp['st'], sp['st'], sp['ch']], out_shape=[_sds((t, w)), _sds((t, w)), _sds((t, S5_WIDTH))],
        scratch_shapes=[pltpu.VMEM((t, S5_ST), F32)] * 2, compiler_params=_params("parallel"),
    )(u5, bre, bim, cre, cim, lr, li)


def _s5_bwd(dy, du_direct, u5, sr, si, bre, bim, cre, cim, lr, li, *, name):
    t = u5.shape[0]
    nrt = t // 8

    def body(dy_ref, dd_ref, u_ref, sr_ref, si_ref, bre_ref, bim_ref, cre_ref, cim_ref, lr_ref, li_ref,
             du_ref, dbre_ref, dbim_ref, dcre_ref, dcim_ref, dlr_ref, dli_ref, gr_ref, gi_ref):
        dyv = dy_ref[...]
        gr_ref[...] = _dot(dyv, cre_ref[...], _NT)
        gi_ref[...] = -_dot(dyv, cim_ref[...], _NT)
        dcre_ref[...] = _dot(sr_ref[...], dyv, _TN)
        dcim_ref[...] = -_dot(si_ref[...], dyv, _TN)
        dr_ref, di_ref = gr_ref, gi_ref
        ar = jnp.broadcast_to(lr_ref[...], (8, S5_ST))
        ai = -jnp.broadcast_to(li_ref[...], (8, S5_ST))
        zero = jnp.zeros((8, S5_ST), F32)

        def step1(k, g):
            rows = pl.ds(pl.multiple_of((nrt - 1 - k) * 8, 8), 8)
            nr, ni = _cmul(ar, ai, g[0], g[1])
            return nr + dr_ref[rows, :], ni + di_ref[rows, :]

        fr, fi = lax.fori_loop(0, nrt, step1, (zero, zero), unroll=SCAN_UNROLL)
        pr, pi = _segment_power(ar, ai, nrt)
        init = _carry_in(fr, fi, pr, pi, True)

        def step2(k, carry):
            gr, gi, accr, acci = carry
            r = nrt - 1 - k
            rows = pl.ds(pl.multiple_of(r * 8, 8), 8)
            prev = pl.ds(pl.multiple_of(jnp.maximum(r - 1, 0) * 8, 8), 8)
            nr, ni = _cmul(ar, ai, gr, gi)
            nr, ni = nr + dr_ref[rows, :], ni + di_ref[rows, :]
            gr_ref[rows, :] = nr
            gi_ref[rows, :] = ni
            keep = jnp.where(r > 0, 1.0, 0.0)
            pr_, pi_ = sr_ref[prev, :] * keep, si_ref[prev, :] * keep
            return nr, ni, accr + (pr_ * nr + pi_ * ni), acci + (pr_ * ni - pi_ * nr)

        _, _, accr, acci = lax.fori_loop(0, nrt, step2, (init[0], init[1], zero, zero), unroll=SCAN_UNROLL)
        last = pl.ds((nrt - 1) * 8, 8)
        pr_, pi_ = _shift_down(sr_ref[last, :], 1), _shift_down(si_ref[last, :], 1)
        g0r, g0i = gr_ref[0:8, :], gi_ref[0:8, :]
        accr = accr + (pr_ * g0r + pi_ * g0i)
        acci = acci + (pr_ * g0i - pi_ * g0r)
        dlr_ref[...] = jnp.sum(accr, axis=0, keepdims=True)
        dli_ref[...] = jnp.sum(acci, axis=0, keepdims=True)
        u = u_ref[...]
        dbre_ref[...] = _dot(u, gr_ref[...], _TN)
        dbim_ref[...] = _dot(u, gi_ref[...], _TN)
        du = dd_ref[...] + _dot(gr_ref[...], bre_ref[...], _NT) + _dot(gi_ref[...], bim_ref[...], _NT)
        du_ref[...] = du.astype(du_ref.dtype)

    sp = _s5_specs(t)
    w = S5_BLOCKS * S5_ST
    return _pcall(
        body, name=name, grid=(S5_BLOCKS,),
        in_specs=[sp['ch'], sp['ch'], sp['ch'], sp['st'], sp['st'], sp['b'], sp['b'], sp['c'], sp['c'], sp['lam'], sp['lam']],
        out_specs=[sp['ch'], sp['b'], sp['b'], sp['c'], sp['c'], sp['lam'], sp['lam']],
        out_shape=[_sds((t, S5_WIDTH), BF16), _sds((S5_BLOCKS, S5_CH, S5_ST)), _sds((S5_BLOCKS, S5_CH, S5_ST)),
                   _sds((S5_BLOCKS, S5_ST, S5_CH)), _sds((S5_BLOCKS, S5_ST, S5_CH)), _sds((1, w)), _sds((1, w))],
        scratch_shapes=[pltpu.VMEM((t, S5_ST), F32)] * 2, compiler_params=_params("parallel"),
    )(dy, du_direct, u5, sr, si, bre, bim, cre, cim, lr, li)


def _s5_expander():
    n = lax.broadcasted_iota(jnp.int32, (S5_STATE, S5_STATE * S5_GROUP), 0)
    j = lax.broadcasted_iota(jnp.int32, (S5_STATE, S5_STATE * S5_GROUP), 1)
    return jnp.where(n == j // S5_GROUP, 1.0, 0.0).astype(F32)


def _s5_discretise(lam_re, lam_im, log_step, b_re, b_im):
    lr = jnp.minimum(lam_re, S5_MAX_REAL)
    step = jnp.exp(log_step)
    mag = jnp.exp(lr * step)
    ang = lam_im * step
    lbr, lbi = mag * jnp.cos(ang), mag * jnp.sin(ang)
    p, q = lbr - 1.0, lbi
    den = lr * lr + lam_im * lam_im
    cr, ci = (p * lr + q * lam_im) / den, (q * lr - p * lam_im) / den
    e = _s5_expander()
    cre, cie = _fdot(cr, e), _fdot(ci, e)
    return lbr, lbi, cre * b_re - cie * b_im, cre * b_im + cie * b_re


def _s5_params_fwd(lam_re, lam_im, log_step, b_re, b_im, *, name):
    g, n, w = lam_re.shape[0], S5_STATE, S5_STATE * S5_GROUP

    def body(a, b, c, d, e, o0, o1, o2, o3):
        for ref, val in zip((o0, o1, o2, o3), _s5_discretise(a[...], b[...], c[...], d[...], e[...])):
            ref[...] = val

    return _pcall(body, name=name, out_shape=[_sds((g, n)), _sds((g, n)), _sds((g, w)), _sds((g, w))])(
        lam_re, lam_im, log_step, b_re, b_im)


def _s5_params_bwd(lam_re, lam_im, log_step, b_re, b_im, cts, *, name):
    g, n, w = S5_GROUPS, S5_STATE, S5_STATE * S5_GROUP

    def body(a, b, c, d, e, c0, c1, c2, c3, o0, o1, o2, o3, o4):
        _, vjp = jax.vjp(_s5_discretise, a[...], b[...], c[...], d[...], e[...])
        for ref, val in zip((o0, o1, o2, o3, o4), vjp((c0[...], c1[...], c2[...], c3[...]))):
            ref[...] = val

    return _pcall(body, name=name,
                  out_shape=[_sds((g, n)), _sds((g, n)), _sds((g, 1)), _sds((g, w)), _sds((g, w))])(
        lam_re, lam_im, log_step, b_re, b_im, *cts)


def _s5_prepare(w):
    rows = DEPTH * S5_GROUPS
    lbr, lbi, bbr, bbi = _s5_params_fwd(
        w['s5_lambda_re'].reshape(rows, -1), w['s5_lambda_im'].reshape(rows, -1), w['s5_log_step'].reshape(rows, 1),
        w['s5_b_re'].reshape(rows, -1), w['s5_b_im'].reshape(rows, -1), name="s5_par")
    bd = lambda m: _blockdiag(m.reshape(rows, S5_STATE, S5_GROUP).transpose(0, 2, 1), S5_GROUP, S5_STATE).astype(BF16)
    cd = lambda m: _blockdiag(m.reshape(rows, S5_GROUP, S5_STATE).transpose(0, 2, 1), S5_STATE, S5_GROUP).astype(BF16)
    bre, bim, cre, cim = bd(bbr), bd(bbi), cd(w['s5_c_re']), cd(w['s5_c_im'])
    lr, li = lbr.reshape(DEPTH, 1, -1), lbi.reshape(DEPTH, 1, -1)
    blk = lambda a, i: a[i * S5_BLOCKS:(i + 1) * S5_BLOCKS]
    return [dict(bre=blk(bre, i), bim=blk(bim, i), cre=blk(cre, i), cim=blk(cim, i), lr=lr[i], li=li[i])
            for i in range(DEPTH)]


def _perm(a):
    t, c = a.shape
    return a.reshape(8, t // 8, c).transpose(1, 0, 2).reshape(t, c)


def _unperm(a):
    t, c = a.shape
    return a.reshape(t // 8, 8, c).transpose(1, 0, 2).reshape(t, c)


def _blockdiag(m, rows_inner, cols_inner):
    nblk = m.shape[0] // 8
    m = m.reshape(nblk, 8, rows_inner, cols_inner)
    eye = jnp.eye(8, dtype=m.dtype)
    out = m[:, :, :, None, :] * eye[None, :, None, :, None]
    return out.reshape(nblk, 8 * rows_inner, 8 * cols_inner)


def _blockdiag_extract(m, rows_inner, cols_inner):
    m = m.reshape(S5_BLOCKS, 8, rows_inner, 8, cols_inner)
    d = jnp.diagonal(m, axis1=1, axis2=3)
    return d.transpose(0, 3, 1, 2).reshape(S5_GROUPS, rows_inner, cols_inner)


Z0, XBC0, GA0, GB0 = 0, SSD_D_INNER, SSD_D_INNER + SSD_CONV_DIM, SSD_D_INNER + SSD_CONV_DIM + D_MODEL
BIG = GB0 + D_MODEL


def _mixer_fwd(h, p, tm):
    t = h.shape[0]
    nrow = t // tm
    u = _rms_fwd(h, p['pre_g'], name="mix_rms", tm=tm)
    u_p = _perm(u)
    proj = _mm(u, p['w_big'], name="mix_in")
    dtr = _mm(u, p['w_dt'], name="mix_in_dt")
    u5 = _mm(u_p, p['w_u5'], name="mix_in_s5")
    late, proj = lax.optimization_barrier((p['late'], proj))
    by_rows = lambda a: a.reshape(-1, a.shape[-1])
    p = dict(p, w_a=by_rows(late['w_branch_a']), w_b=by_rows(late['w_branch_b']), w_out=by_rows(late['w_out']),
             w_glu=late['s5_w_glu'][:, 0].transpose(1, 0, 2).reshape(late['s5_w_glu'].shape[2], -1))
    xc = _conv_fwd(proj, XBC0, p['conv_w'], p['conv_b'], name="ssd_conv")
    dt4 = jnp.pad(dtr.reshape(t, SSD_GROUPS, 8).transpose(1, 0, 2), ((0, 0), (0, 0), (0, PAD_HEADS - 8)))
    y_ssd, s_in = _ssd_fwd(xc, dt4, p['dt_bias8'], p['a_log8'], p['d8'], name="ssd_scan")
    gw = SSD_D_INNER // SSD_GROUPS
    ya = _rows(_gatenorm, name="ssd_gate", nrow=nrow, ncol=SSD_GROUPS,
               ins=[(y_ssd, _rspec(tm, gw, 0, True)), (proj, _rspec(tm, gw, Z0 // gw, True)),
                    (p['norm_g'], _bspec(gw, 0, True))],
               outs=[(_sds((t, SSD_D_INNER), BF16), _rspec(tm, gw, 0, True), False)])[0]
    y_a = _mm(ya, p['w_a'], name="mix_a")
    bre, bim, cre, cim, lr, li = (p['s5'][k] for k in ('bre', 'bim', 'cre', 'cim', 'lr', 'li'))
    sr, si, y5 = _s5_fwd(u5, bre, bim, cre, cim, lr, li, name="s5_scan")
    y5g = _rows(lambda a, b, d: _gelu(a + d * b), name="s5_act", nrow=nrow,
                ins=[(y5, _rspec(tm, S5_WIDTH)), (u5, _rspec(tm, S5_WIDTH)), (p['s5_d'], _bspec(S5_WIDTH))],
                outs=[(_sds((t, S5_WIDTH), BF16), _rspec(tm, S5_WIDTH), False)])[0]
    vg = _mm(y5g, p['w_glu'], name="s5_glu")
    ybin = _rows(lambda a, b: a * _sigmoid(b), name="s5_glu_act", nrow=nrow,
                 ins=[(vg, _rspec(tm, S5_WIDTH, 0)), (vg, _rspec(tm, S5_WIDTH, 1))],
                 outs=[(_sds((t, S5_WIDTH), BF16), _rspec(tm, S5_WIDTH), False)])[0]
    y_b = _unperm(_mm(ybin, p['w_b'], name="mix_b"))
    merged = _rows(lambda ga, gb, a, b: _sigmoid(ga) * a + _sigmoid(gb) * b, name="mix_merge", nrow=nrow,
                   ins=[(proj, _rspec(tm, D_MODEL, GA0 // D_MODEL)), (proj, _rspec(tm, D_MODEL, GB0 // D_MODEL)),
                        (y_a, _rspec(tm, D_MODEL)), (y_b, _rspec(tm, D_MODEL))],
                   outs=[(_sds((t, D_MODEL), BF16), _rspec(tm, D_MODEL), False)])[0]
    m = _mm(merged, p['w_out'], name="mix_out")
    out = _resid_fwd(h, m, p['post_g'], 1.0, name="mix_res", tm=tm)
    saved = dict(w_a=p['w_a'], w_b=p['w_b'], w_out=p['w_out'], w_glu=p['w_glu'], h=h, u=u, u_p=u_p, proj=proj, u5=u5, xc=xc, dt4=dt4, s_in=s_in, y_ssd=y_ssd, ya=ya, y_a=y_a,
                 bre=bre, bim=bim, cre=cre, cim=cim, lr=lr, li=li, sr=sr, si=si, y5=y5, y5g=y5g, vg=vg, ybin=ybin,
                 y_b=y_b, merged=merged, m=m)
    return out, saved


def _mixer_bwd(dh, p, s, tm, after_first):
    t = dh.shape[0]
    nrow = t // tm
    proj = s['proj']
    bufs = {}

    def grad_mm(a, b, wname, axis, name):
        dw = _mm(a, b, ta=True, name=name, out_dtype=BF16, col_shards=axis == 'cols')
        bufs[wname] = dw if axis == 'cols' else dw.reshape(N_DEV, 1, dw.shape[0] // N_DEV, dw.shape[1])

    dm, dpost = _resid_bwd(s['m'], p['post_g'], dh, 1.0, name="mix_res_bwd", tm=tm)
    dm = after_first(dm)
    dmerged = _mm(dm, s['w_out'], tb=True, name="mix_out_dx")
    grad_mm(s['merged'], dm, 'w_out', 'rows', "mix_out_dw")

    def merge_bwd(ga, gb, a, b, d):
        _, vjp = jax.vjp(lambda ga_, gb_, a_, b_: _sigmoid(ga_) * a_ + _sigmoid(gb_) * b_, ga, gb, a, b)
        dga, dgb, da, db = vjp(d)
        return jnp.concatenate([dga, dgb], axis=1), da, db

    dgab, dy_a, dy_b = _rows(
        merge_bwd, name="mix_merge_bwd", nrow=nrow,
        ins=[(proj, _rspec(tm, D_MODEL, GA0 // D_MODEL)), (proj, _rspec(tm, D_MODEL, GB0 // D_MODEL)),
             (s['y_a'], _rspec(tm, D_MODEL)), (s['y_b'], _rspec(tm, D_MODEL)), (dmerged, _rspec(tm, D_MODEL))],
        outs=[(_sds((t, 2 * D_MODEL), BF16), _rspec(tm, 2 * D_MODEL), False),
              (_sds((t, D_MODEL), BF16), _rspec(tm, D_MODEL), False),
              (_sds((t, D_MODEL), BF16), _rspec(tm, D_MODEL), False)])
    dya = _mm(dy_a, s['w_a'], tb=True, name="mix_a_dx")
    grad_mm(s['ya'], dy_a, 'w_branch_a', 'rows', "mix_a_dw")
    gw = SSD_D_INNER // SSD_GROUPS

    def gate_bwd(y, z, g, d):
        _, vjp = jax.vjp(_gatenorm, y, z, g)
        return vjp(d)

    dy_ssd, dz, dnorm = _rows(
        gate_bwd, name="ssd_gate_bwd", nrow=nrow, ncol=SSD_GROUPS,
        ins=[(s['y_ssd'], _rspec(tm, gw, 0, True)), (proj, _rspec(tm, gw, Z0 // gw, True)),
             (p['norm_g'], _bspec(gw, 0, True)), (dya, _rspec(tm, gw, 0, True))],
        outs=[(_sds((t, SSD_D_INNER)), _rspec(tm, gw, 0, True), False),
              (_sds((t, SSD_D_INNER), BF16), _rspec(tm, gw, 0, True), False),
              (_sds((1, SSD_D_INNER)), _bspec(gw, 0, True), True)])
    dxs, dbm, dcm, ddt4, ddtb, dalog, ddsk = _ssd_bwd(s['xc'], s['dt4'], p['dt_bias8'], p['a_log8'], p['d8'],
                                                      s['s_in'], dy_ssd, name="ssd_scan_bwd")
    dxbc, dconv_w, dconv_b = _conv_bwd(proj, XBC0, p['conv_w'], p['conv_b'], (dxs, dbm, dcm), name="ssd_conv_bwd")
    ddtr = ddt4[:, :, :8].transpose(1, 0, 2).reshape(t, SSD_HEADS)
    dy_bp = _perm(dy_b)
    dybin = _mm(dy_bp, s['w_b'], tb=True, name="mix_b_dx")
    grad_mm(s['ybin'], dy_bp, 'w_branch_b', 'rows', "mix_b_dw")

    def glu_bwd(a, b, d):
        _, vjp = jax.vjp(lambda a_, b_: a_ * _sigmoid(b_), a, b)
        da, db = vjp(d)
        return jnp.concatenate([da, db], axis=1)

    dvg = _rows(glu_bwd, name="s5_glu_act_bwd", nrow=nrow,
                ins=[(s['vg'], _rspec(tm, S5_WIDTH, 0)), (s['vg'], _rspec(tm, S5_WIDTH, 1)), (dybin, _rspec(tm, S5_WIDTH))],
                outs=[(_sds((t, 2 * S5_WIDTH), BF16), _rspec(tm, 2 * S5_WIDTH), False)])[0]
    dy5g = _mm(dvg, s['w_glu'], tb=True, name="s5_glu_dx")
    grad_mm(s['y5g'], dvg, 's5_w_glu', 'cols', "s5_glu_dw")

    def act_bwd(a, b, d, g):
        _, vjp = jax.vjp(lambda a_, b_, d_: _gelu(a_ + d_ * b_), a, b, d)
        return vjp(g)

    dy5, du5_direct, ds5d = _rows(
        act_bwd, name="s5_act_bwd", nrow=nrow,
        ins=[(s['y5'], _rspec(tm, S5_WIDTH)), (s['u5'], _rspec(tm, S5_WIDTH)), (p['s5_d'], _bspec(S5_WIDTH)),
             (dy5g, _rspec(tm, S5_WIDTH))],
        outs=[(_sds((t, S5_WIDTH), BF16), _rspec(tm, S5_WIDTH), False), (_sds((t, S5_WIDTH)), _rspec(tm, S5_WIDTH), False),
              (_sds((1, S5_WIDTH)), _bspec(S5_WIDTH), True)])
    du5, dbre, dbim, dcre, dcim, dlr, dli = _s5_bwd(dy5, du5_direct, s['u5'], s['sr'], s['si'], s['bre'], s['bim'],
                                                     s['cre'], s['cim'], s['lr'], s['li'], name="s5_scan_bwd")
    du_p = _mm(du5, p['w_u5'], tb=True, name="mix_in_s5_dx")
    dw_u5 = _mm(s['u_p'], du5, ta=True, name="mix_in_s5_dw", out_dtype=BF16)
    ext_b = lambda m: _blockdiag_extract(m, S5_GROUP, S5_STATE).transpose(0, 2, 1).reshape(S5_GROUPS, S5_STATE * S5_GROUP)
    dlam_re, dlam_im, dlog_step, db_re, db_im = _s5_params_bwd(
        p['lam_re'], p['lam_im'], p['log_step'], p['b_re'], p['b_im'],
        (dlr.reshape(S5_GROUPS, S5_STATE), dli.reshape(S5_GROUPS, S5_STATE), ext_b(dbre), ext_b(dbim)), name="s5_par_bwd")
    dc_re = _blockdiag_extract(dcre, S5_STATE, S5_GROUP).transpose(0, 2, 1)
    dc_im = _blockdiag_extract(dcim, S5_STATE, S5_GROUP).transpose(0, 2, 1)
    dproj = jnp.concatenate([dz, dxbc, dgab], axis=1)
    du_big = _mm(dproj, p['w_big'], tb=True, name="mix_in_dx")
    du_dt = _mm(ddtr, p['w_dt'], tb=True, name="mix_in_dt_dx")
    dw_big = _mm(s['u'], dproj, ta=True, name="mix_in_dw", out_dtype=BF16)
    dw_dt = _mm(s['u'], ddtr, ta=True, name="mix_in_dt_dw", out_dtype=BF16)
    dh_in, dpre = _rms_bwd(s['h'], p['pre_g'], dh, [du_big, du_dt, _unperm(du_p)], name="mix_rms_bwd", tm=tm)
    dw_in = jnp.concatenate([dw_big[:, :GA0], dw_dt, dw_u5, dw_big[:, GA0:]], axis=1)
    bufs['w_in'] = dw_in.reshape(D_MODEL, N_DEV, -1).transpose(1, 0, 2)[:, None]
    grads = {
        'mix_pre_g': dpre, 'mix_post_g': dpost, 'ssd_conv_w': dconv_w, 'ssd_conv_b': dconv_b,
        'ssd_dt_bias': ddtb[:, 0, :8].reshape(-1), 'ssd_a_log': dalog[:, 0, :8].reshape(-1),
        'ssd_d': ddsk[:, 0, :8].reshape(-1), 'ssd_norm_g': dnorm,
        's5_lambda_re': dlam_re, 's5_lambda_im': dlam_im,
        's5_b_re': db_re.reshape(S5_GROUPS, S5_STATE, S5_GROUP), 's5_b_im': db_im.reshape(S5_GROUPS, S5_STATE, S5_GROUP),
        's5_c_re': dc_re, 's5_c_im': dc_im, 's5_log_step': dlog_step.reshape(-1), 's5_d': ds5d,
    }
    return dh_in, bufs, grads


HBM_SPEC = pl.BlockSpec(memory_space=pltpu.HBM)


def _place():
    return lax.axis_index("x"), lax.axis_index("y"), lax.axis_index("c")


GATHER_COLLECTIVE_ID = 1


def _all_gather(shards, *, name, on_sequencer=False):
    n = len(shards)

    def body(*refs):
        x_refs, out_refs = refs[:n], refs[n:2 * n]
        send_sems, recv_sems, local_sems = refs[2 * n:]
        x, y, c = _place()
        me, sibling = (x, y, c), (x, y, 1 - c)
        chips = [(1 - x, y), (x, 1 - y), (1 - x, 1 - y)]
        if on_sequencer:
            _handshake([sibling] + [(*chip, c) for chip in chips])

        def slot(o, px, py, pc):
            return out_refs[o].at[4 * px + 2 * py + pc]

        def copy(o, k, block, to, src=None):
            return pltpu.make_async_remote_copy(
                src_ref=slot(o, *block) if src is None else src, dst_ref=slot(o, *block),
                send_sem=send_sems.at[7 * o + k], recv_sem=recv_sems.at[7 * o + k], device_id=to, device_id_type=MESH)

        mine = [pltpu.make_async_copy(x_refs[o], slot(o, *me), local_sems.at[o]) for o in range(n)]
        for cp in mine:
            cp.start()
        first = []
        for j, chip in enumerate(chips):
            first += [copy(o, 1 + j, me, (*chip, c), src=x_refs[o]) for o in range(n)]
        first += [copy(o, 0, me, sibling, src=x_refs[o]) for o in range(n)]
        for cp in first:
            cp.start()
        passed = []
        for j, chip in enumerate(chips):
            for o in range(n):
                copy(o, 1 + j, (*chip, c), me).wait_recv()
                passed.append(copy(o, 4 + j, (*chip, c), sibling))
                passed[-1].start()
        for o in range(n):
            copy(o, 0, sibling, me).wait_recv()
        for j, chip in enumerate(chips):
            for o in range(n):
                copy(o, 4 + j, (*chip, 1 - c), me).wait_recv()
        for cp in first + passed:
            cp.wait_send()
        for cp in mine:
            cp.wait()

    out_shape = [jax.ShapeDtypeStruct((N_DEV,) + s.shape, s.dtype) for s in shards]
    sems = [pltpu.SemaphoreType.DMA((7 * n,)), pltpu.SemaphoreType.DMA((7 * n,)), pltpu.SemaphoreType.DMA((n,))]
    if on_sequencer:
        return _scall(body, name=name, out_type=out_shape, scratch_types=sems, collective_id=GATHER_COLLECTIVE_ID)(*shards)
    return _pcall(body, name=name, in_specs=[HBM_SPEC] * n, out_specs=[HBM_SPEC] * n, out_shape=out_shape,
                  scratch_shapes=sems)(*shards)


N_CHIPS = 4


SIBLING_COLLECTIVE_ID = 2
CHIPS_COLLECTIVE_ID = 3


def _handshake(peers):
    barrier = pltpu.get_barrier_semaphore()
    for peer in peers:
        pl.semaphore_signal(barrier, inc=1, device_id=peer, device_id_type=MESH)
    pl.semaphore_wait(barrier, len(peers))


def _exchange_sibling(grads, *, name):
    n = len(grads)

    def body(*refs):
        p_refs, q_refs = refs[:n], refs[n:2 * n]
        send_sems, recv_sems = refs[2 * n:]
        x, y, c = _place()
        _handshake([(x, y, 1 - c)])
        copies = [pltpu.make_async_remote_copy(
            src_ref=p_refs[o].at[k, 1 - c], dst_ref=q_refs[o].at[k], send_sem=send_sems.at[N_CHIPS * o + k],
            recv_sem=recv_sems.at[N_CHIPS * o + k], device_id=(x, y, 1 - c), device_id_type=MESH)
            for o in range(n) for k in range(N_CHIPS)]
        for cp in copies:
            cp.start()
        for cp in copies:
            cp.wait()

    return _scall(
        body, name=name, out_type=[jax.ShapeDtypeStruct((N_CHIPS,) + g.shape[2:], g.dtype) for g in grads],
        scratch_types=[pltpu.SemaphoreType.DMA((N_CHIPS * n,)), pltpu.SemaphoreType.DMA((N_CHIPS * n,))],
        collective_id=SIBLING_COLLECTIVE_ID,
    )(*grads)


def _pair_sum(own, got, *, name):
    _, _, r, l = own.shape
    tr = _tile(r, 512, 16)
    c = lax.axis_index("c").astype(jnp.int32).reshape(1)

    def body(c_ref, p_ref, q_ref, o_ref):
        o_ref[...] = (p_ref[...].astype(F32) + q_ref[...].astype(F32)).astype(o_ref.dtype)

    return _pcall(
        body, name=name,
        grid_spec=pltpu.PrefetchScalarGridSpec(
            num_scalar_prefetch=1, grid=(N_CHIPS, r // tr),
            in_specs=[pl.BlockSpec((None, None, tr, l), lambda k, i, cr: (k, cr[0], i, 0)),
                      pl.BlockSpec((None, tr, l), lambda k, i, cr: (k, i, 0))],
            out_specs=pl.BlockSpec((None, tr, l), lambda k, i, cr: (k, i, 0))),
        out_shape=jax.ShapeDtypeStruct((N_CHIPS, r, l), own.dtype),
        compiler_params=_params("parallel", "parallel"),
    )(c, own, got)


def _exchange_chips(parts, *, name):
    n = len(parts)

    def body(*refs):
        p_refs, g_refs = refs[:n], refs[n:2 * n]
        send_sems, recv_sems, local_sems = refs[2 * n:]
        x, y, c = _place()
        mine = 2 * x + y
        chips = [(1 - x, y), (x, 1 - y), (1 - x, 1 - y)]
        _handshake([(*chip, c) for chip in chips])
        own = [pltpu.make_async_copy(p_refs[o].at[mine], g_refs[o].at[mine], local_sems.at[o]) for o in range(n)]
        for cp in own:
            cp.start()
        copies = []
        for j, (px, py) in enumerate(chips):
            copies += [pltpu.make_async_remote_copy(
                src_ref=p_refs[o].at[2 * px + py], dst_ref=g_refs[o].at[mine], send_sem=send_sems.at[3 * o + j],
                recv_sem=recv_sems.at[3 * o + j], device_id=(px, py, c), device_id_type=MESH) for o in range(n)]
        for cp in copies:
            cp.start()
        for cp in copies:
            cp.wait()
        for cp in own:
            cp.wait()

    return _scall(
        body, name=name, out_type=[jax.ShapeDtypeStruct(p.shape, p.dtype) for p in parts],
        scratch_types=[pltpu.SemaphoreType.DMA((3 * n,)), pltpu.SemaphoreType.DMA((3 * n,)), pltpu.SemaphoreType.DMA((n,))],
        collective_id=CHIPS_COLLECTIVE_ID,
    )(*parts)


def _sum_slots(g, *, name):
    n, r, l = g.shape
    tr = _tile(r, 512, 16)

    def body(g_ref, o_ref):
        acc = g_ref[0].astype(F32)
        for k in range(1, n):
            acc = acc + g_ref[k].astype(F32)
        o_ref[...] = acc

    return _pcall(
        body, name=name, grid=(r // tr,), in_specs=[pl.BlockSpec((n, tr, l), lambda i: (0, i, 0))],
        out_specs=pl.BlockSpec((tr, l), lambda i: (i, 0)), out_shape=_sds((r, l)),
        compiler_params=_params("parallel"),
    )(g)


TRANSPOSED = ('ffn1_w_gate', 'ffn1_w_up', 'ffn2_w_gate', 'ffn2_w_up')
GATHER_CHUNKS = (('ffn1', ['ffn1_w_gate', 'ffn1_w_up']), ('ffn1_down', ['ffn1_w_down']),
                 ('mix_in', ['w_in', 'ssd_conv_w']), ('mix', ['w_branch_a', 's5_w_glu', 'w_branch_b', 'w_out']),
                 ('ffn2', ['ffn2_w_gate', 'ffn2_w_up']), ('ffn2_down', ['ffn2_w_down']))
LATE = ('ffn1_w_down', 'ffn2_w_down', 'w_branch_a', 's5_w_glu', 'w_branch_b', 'w_out')
SUBLAYERS = (('ffn1', ['ffn1_w_gate', 'ffn1_w_up', 'ffn1_w_down']),
             ('mix', ['w_in', 'ssd_conv_w', 'w_branch_a', 's5_w_glu', 'w_branch_b', 'w_out']),
             ('ffn2', ['ffn2_w_gate', 'ffn2_w_up', 'ffn2_w_down']))


def _gather_weights(w):
    layers, first = [], None
    for i in range(DEPTH):
        g = {}
        for tag, names in GATHER_CHUNKS:
            shards =[w[n][i:i + 1] if n == 'ssd_conv_w' else
                      (w[n][i:i + 1].transpose(0, 2, 1) if n in TRANSPOSED else w[n][i:i + 1]).astype(BF16) for n in names]
            if first is None:
                first = got = _all_gather(shards, name=f"gather_{tag}")
            else:
                shards, first = lax.optimization_barrier((shards, first))
                got = _all_gather(shards, name=f"gather_{tag}", on_sequencer=True)
            g.update(zip(names, got))
        layers.append(g)
    layers[0].update(zip(GATHER_CHUNKS[0][1], first))
    return layers


class _ReduceScatter:
    @staticmethod
    def sibling(tag, bufs):
        names = list(bufs)
        own = [bufs[n].reshape((N_CHIPS, 2) + bufs[n].shape[1:]) for n in names]
        return (tag, names), (own, _exchange_sibling(own, name=f"reduce_sibling_{tag}"))

    @staticmethod
    def chips(meta, arrays):
        (tag, names), (own, got) = meta, arrays
        flat = lambda a, lead: a.reshape(lead + (-1, a.shape[-1]))
        parts = [_pair_sum(flat(o, (N_CHIPS, 2)), flat(g, (N_CHIPS,)), name=f"reduce_pair_sum_{n}").reshape(g.shape)
                 for n, o, g in zip(names, own, got)]
        return names, _exchange_chips(parts, name=f"reduce_chips_{tag}")

    @staticmethod
    def done(names, slots):
        return dict(zip(names, slots))

    @staticmethod
    def small(grads):
        return _reduce_small(grads)


def _reduce_small(grads):
    flat = jnp.concatenate([g.astype(F32).reshape(-1) for g in grads.values()])
    pad = (-flat.shape[0]) % (8 * LANES)
    flat = jnp.concatenate([flat, jnp.zeros((pad,), F32)]).reshape(-1, LANES)
    gathered = _all_gather([flat], name="gather_small_grads", on_sequencer=True)[0]
    total = _sum_slots(gathered, name="sum_small_grads").reshape(-1)
    out, o = {}, 0
    for n, g in grads.items():
        out[n] = total[o:o + g.size].reshape(g.shape)
        o += g.size
    return out


def _adamw(w, g, m, v, *, name, slots=False):
    shape = w.shape
    if slots:
        lyr, rows, lanes = shape
        w2, m2, v2 = w, m, v
        tr = _tile(rows, 256, 16)
        nrt = rows // tr
        grid = (lyr, nrt)
        spec = pl.BlockSpec((None, tr, lanes), lambda l, i: (l, i, 0))
        g_specs = [pl.BlockSpec((N_CHIPS, None, tr, lanes),
                                lambda l, i, k=k: (0, 0, jnp.where(l == k, i, jnp.where(l > k, nrt - 1, 0)), 0))
                   for k in range(lyr)]
        g_args = list(g)
        out_shape = [_sds(shape)] * 4
    else:
        lanes = shape[-1] if (shape[-1] >= 128 or w.size % LANES) else LANES
        as2d = lambda a: a.reshape(-1, lanes)
        w2, m2, v2 = as2d(w), as2d(m), as2d(v)
        r = w2.shape[0]
        tr = _tile(r, 256, 8)
        grid = (1, r // tr)
        spec = pl.BlockSpec((tr, lanes), lambda l, i: (i, 0))
        g_specs, g_args = [spec], [as2d(g)]
        out_shape = [_sds((r, lanes))] * 4
    n_g = len(g_args)

    def body(w_ref, *rest):
        g_refs = rest[:n_g]
        m_ref, v_ref, go_ref, d_ref, mo_ref, vo_ref = rest[n_g:]
        if slots:
            gg = None
            for k, g_ref in enumerate(g_refs):
                tot = g_ref[0].astype(F32)
                for c in range(1, N_CHIPS):
                    tot = tot + g_ref[c].astype(F32)
                gg = tot if gg is None else jnp.where(pl.program_id(0) == k, tot, gg)
        else:
            gg = g_refs[0][...]
        go_ref[...] = gg
        mn = ADAM_B1 * m_ref[...] + (1.0 - ADAM_B1) * gg
        vn = ADAM_B2 * v_ref[...] + (1.0 - ADAM_B2) * (gg * gg)
        m_hat = mn / (1.0 - ADAM_B1 ** ADAM_STEP)
        v_hat = vn / (1.0 - ADAM_B2 ** ADAM_STEP)
        d_ref[...] = -ADAM_LR * (m_hat / (jnp.sqrt(v_hat) + ADAM_EPS) + ADAM_WD * w_ref[...])
        mo_ref[...] = mn
        vo_ref[...] = vn

    res = _pcall(
        body, name=name, grid=grid, in_specs=[spec] + g_specs + [spec, spec], out_specs=[spec] * 4,
        out_shape=out_shape, compiler_params=_params("arbitrary", "arbitrary"),
    )(w2, *g_args, m2, v2)
    return tuple(a.reshape(shape) for a in res)


def _sublayer_params(w, g, i, k, s5):
    row = lambda a: a.astype(F32).reshape(1, -1)
    if k != 'mix':
        return dict(layer=i, pre_g=row(w[f'{k}_pre_g'][i]), post_g=row(w[f'{k}_post_g'][i]),
                    w_gate=g[f'{k}_w_gate'], w_up=g[f'{k}_w_up'], w_down=g[f'{k}_w_down'])
    head8 = lambda a: jnp.broadcast_to(
        jnp.pad(a.astype(F32).reshape(SSD_GROUPS, 1, 8), ((0, 0), (0, 0), (0, PAD_HEADS - 8))), (SSD_GROUPS, 8, PAD_HEADS))
    by_cols = lambda n: g[n][:, 0].transpose(1, 0, 2).reshape(g[n].shape[2], -1)
    w_in = by_cols('w_in')
    s = np.cumsum([SSD_D_INNER, SSD_CONV_DIM, SSD_HEADS, S5_WIDTH, D_MODEL])
    return dict(
        layer=i, s5=s5, pre_g=row(w['mix_pre_g'][i]), post_g=row(w['mix_post_g'][i]),
        w_big=jnp.concatenate([w_in[:, :s[1]], w_in[:, s[3]:]], axis=1), w_dt=w_in[:, s[1]:s[2]], w_u5=w_in[:, s[2]:s[3]],
        conv_w=by_cols('ssd_conv_w'), conv_b=row(w['ssd_conv_b'][i]),
        dt_bias8=head8(w['ssd_dt_bias'][i]), a_log8=head8(w['ssd_a_log'][i]), d8=head8(w['ssd_d'][i]),
        norm_g=row(w['ssd_norm_g'][i]), late={n: g[n] for n in SUBLAYERS[1][1] if n in LATE},
        lam_re=w['s5_lambda_re'][i], lam_im=w['s5_lambda_im'][i], log_step=w['s5_log_step'][i].reshape(S5_GROUPS, 1),
        b_re=w['s5_b_re'][i].reshape(S5_GROUPS, -1), b_im=w['s5_b_im'][i].reshape(S5_GROUPS, -1),
        c_re=w['s5_c_re'][i], c_im=w['s5_c_im'][i], s5_d=row(w['s5_d'][i]),
    )


def _loss_head(h, target, *, tm):
    t, d = h.shape

    def fn(y, tgt):
        err = y - tgt
        return err * (1.0 / d), jnp.sum(0.5 * jnp.sum(err * err, axis=-1, keepdims=True) * (1.0 / d), axis=0, keepdims=True)

    dy, loss = _rows(fn, name="loss_head", nrow=t // tm,
                     ins=[(h, _rspec(tm, d)), (target, _rspec(tm, d))],
                     outs=[(_sds((t, d)), _rspec(tm, d), False), (_sds((1, 128)), _bspec(128), True)])
    return dy, loss[0, 0]


def _forward_backward(h, target, w, g, rs):
    t = h.shape[0]
    tm = _tile(t, 512, 8)
    s5 = None
    layers, saved = [], []
    for i in range(DEPTH):
        gi, ps, ss = dict(g[i]), [], []
        for tag, names in SUBLAYERS:
            if tag == 'mix' and s5 is None:
                mine = {n: w[n] for n in WEIGHTS if n.startswith('s5_') and n not in SHARDED}
                mine, h = lax.optimization_barrier((mine, h))
                s5 = _s5_prepare(mine)
            early = [n for n in names if n not in LATE]
            tied, h, s5 = lax.optimization_barrier(([gi[n] for n in early], h, s5))
            gi.update(zip(early, tied))
            p = _sublayer_params(w, gi, i, tag, s5[i] if tag == 'mix' else None)
            h, s = _mixer_fwd(h, p, tm) if tag == 'mix' else _ffn_fwd(h, p, tag, tm)
            ps.append(p)
            ss.append(s)
        layers.append(ps)
        saved.append(ss)
    dh, loss = _loss_head(h, target, tm=tm)
    reduced, small = [{} for _ in range(DEPTH)], [{} for _ in range(DEPTH)]
    in_sibling, in_chips = None, None

    def start_chips(x):
        nonlocal in_sibling, in_chips
        if in_sibling is not None:
            layer, meta, arrays = in_sibling
            arrays, x = lax.optimization_barrier((arrays, x))
            in_sibling, in_chips = None, (layer,) + tuple(rs.chips(meta, arrays))
        return x

    def finish_chips(x):
        nonlocal in_chips
        if in_chips is not None:
            layer, names, slots = in_chips
            slots, x = lax.optimization_barrier((slots, x))
            reduced[layer].update(rs.done(names, slots))
            in_chips = None
        return x

    for i in reversed(range(DEPTH)):
        for k in reversed(range(len(SUBLAYERS))):
            tag = SUBLAYERS[k][0]
            if tag == 'mix':
                dh, bufs, grads = _mixer_bwd(dh, layers[i][k], saved[i][k], tm, start_chips)
            else:
                dh, bufs, grads = _ffn_bwd(dh, layers[i][k], saved[i][k], tag, tm, start_chips)
            small[i].update(grads)
            dh = finish_chips(dh)
            in_sibling = (i,) + tuple(rs.sibling(tag, bufs))
            if tag == 'mix' and i + 1 < DEPTH:
                small[i + 1], dh = lax.optimization_barrier((small[i + 1], dh))
        if i == 0:
            small[i]['loss'] = loss.reshape(1)
        small[i] = rs.small(small[i])
    loss = small[0].pop('loss')[0]
    dh = finish_chips(start_chips(dh))
    shapes = {n: (w[n].shape[:-1] + (SSD_CONV_DIM,) if n == 'ssd_conv_w' else w[n].shape) for n in SMALL_ORDER}
    stacked = {n: jnp.stack([small[i][n].reshape(shapes[n][1:]) for i in range(DEPTH)]) for n in SMALL_ORDER}
    return loss, dh, reduced, stacked


def kernel(*args):
    n_w = len(WEIGHTS)
    x, target = args[0], args[1 + n_w]
    w = dict(zip(WEIGHTS, args[1:1 + n_w]))
    m = dict(zip(WEIGHTS, args[2 + n_w:2 + 2 * n_w]))
    v = dict(zip(WEIGHTS, args[2 + 2 * n_w:2 + 3 * n_w]))
    t = x.shape[1]

    g = _gather_weights(w)
    loss, dx, slots, small = _forward_backward(x.reshape(t, D_MODEL), target.reshape(t, D_MODEL), w, g, _ReduceScatter)
    me = 4 * lax.axis_index("x") + 2 * lax.axis_index("y") + lax.axis_index("c")
    cols = w['ssd_conv_w'].shape[-1]
    small['ssd_conv_w'] = lax.dynamic_slice_in_dim(small['ssd_conv_w'], me * cols, cols, axis=2)

    grad, delta, new_m, new_v = {}, {}, {}, {}
    for n in WEIGHTS:
        sharded = n in slots[0]
        view = (lambda a: a.transpose(0, 2, 1)) if n in TRANSPOSED else (lambda a: a)
        res = _adamw(view(w[n]), [slots[i][n] for i in range(DEPTH)] if sharded else small[n], view(m[n]), view(v[n]),
                     name=f"adamw_{n}", slots=sharded)
        grad[n], delta[n], new_m[n], new_v[n] = (view(a) for a in res)
    return (loss, dx.reshape(x.shape), *[grad[n] for n in WEIGHTS], *[delta[n] for n in WEIGHTS],
            *[new_m[n] for n in WEIGHTS], *[new_v[n] for n in WEIGHTS])
```

```python
import math

import numpy as np
import jax
import jax.numpy as jnp
from jax import lax
from jax.experimental import pallas as pl
from jax.experimental.pallas import tpu as pltpu
from jax.experimental.pallas import tpu_sc as plsc

F32 = jnp.float32
BF16 = jnp.bfloat16
MESH = pl.DeviceIdType.MESH
HIGHEST = lax.Precision.HIGHEST

D_MODEL = 1024
DEPTH = 2
FFN_HIDDEN = 2816
SSD_D_INNER = 2048
SSD_HEADS = 32
SSD_HEAD_DIM = 64
SSD_GROUPS = 4
SSD_STATE = 128
SSD_CHUNK = 256
SSD_CONV_DIM = 3072
SSD_CONV_WIDTH = 4
S5_WIDTH = 1024
S5_GROUP = 16
S5_GROUPS = 64
S5_STATE = 64
S5_MAX_REAL = -1e-4
S5_BLOCKS = 8
RMS_EPS = 1e-6
N_DEV = 8
LANES = 1024

ADAM_LR = 0.001
ADAM_B1 = 0.9
ADAM_B2 = 0.999
ADAM_EPS = 1e-08
ADAM_WD = 0.01
ADAM_STEP = 10

VMEM_LIMIT_BYTES = 48 * 1024 * 1024

WEIGHTS = ['ffn1_pre_g', 'ffn1_post_g', 'ffn1_w_gate', 'ffn1_w_up', 'ffn1_w_down', 'mix_pre_g', 'mix_post_g',
           'w_in', 'ssd_conv_w', 'ssd_conv_b', 'ssd_dt_bias', 'ssd_a_log', 'ssd_d', 'ssd_norm_g', 'w_branch_a',
           's5_lambda_re', 's5_lambda_im', 's5_b_re', 's5_b_im', 's5_c_re', 's5_c_im', 's5_log_step', 's5_d',
           's5_w_glu', 'w_branch_b', 'w_out', 'ffn2_pre_g', 'ffn2_post_g', 'ffn2_w_gate', 'ffn2_w_up',
           'ffn2_w_down']
SHARDED = {'ffn1_w_gate': 2, 'ffn1_w_up': 2, 'ffn1_w_down': 1, 'w_in': 2, 'ssd_conv_w': 2, 'w_branch_a': 1,
           's5_w_glu': 2, 'w_branch_b': 1, 'w_out': 1, 'ffn2_w_gate': 2, 'ffn2_w_up': 2, 'ffn2_w_down': 1}
SHARDED_ORDER = [n for n in WEIGHTS if n in SHARDED]
SMALL_ORDER = [n for n in WEIGHTS if n not in SHARDED or n == 'ssd_conv_w']


def _pcall(body, **kw):
    return pl.pallas_call(body, **kw)


def _scall(body, *, name, out_type, scratch_types, collective_id):
    return pl.kernel(body, out_type=out_type, mesh=plsc.ScalarSubcoreMesh(axis_name="sequencer", num_cores=1),
                     scratch_types=scratch_types, name=name,
                     compiler_params=pltpu.CompilerParams(collective_id=collective_id))


def _params(*sem):
    return pltpu.CompilerParams(dimension_semantics=sem, vmem_limit_bytes=VMEM_LIMIT_BYTES)


def _tile(n, pref, align=128):
    if n <= pref:
        return n
    t = (pref // align) * align
    while t >= align:
        if n % t == 0:
            return t
        t -= align
    return n


def _rms(x, g):
    return x * lax.rsqrt(jnp.mean(x * x, axis=-1, keepdims=True) + RMS_EPS) * g


def _sigmoid(x):
    return 1.0 / (1.0 + jnp.exp(-x))


def _silu(x):
    return x * _sigmoid(x)


def _gelu(x):
    return 0.5 * x * (1.0 + jnp.tanh(math.sqrt(2.0 / math.pi) * (x + 0.044715 * (x * x * x))))


def _softplus(x):
    return jnp.maximum(x, 0.0) + jnp.log(1.0 + jnp.exp(-jnp.abs(x)))


def _dot(a, b, dims):
    return lax.dot_general(a.astype(BF16), b.astype(BF16), (dims, ((), ())), preferred_element_type=F32)


_NN = ((1,), (0,))
_NT = ((1,), (1,))
_TN = ((0,), (0,))


@jax.custom_vjp
def _bdot_nn(a, b):
    return _dot(a, b, _NN)


_bdot_nn.defvjp(lambda a, b: (_dot(a, b, _NN), (a, b)),
                lambda r, g: (_dot(g, r[1], _NT), _dot(r[0], g, _TN)))


@jax.custom_vjp
def _bdot_nt(a, b):
    return _dot(a, b, _NT)


_bdot_nt.defvjp(lambda a, b: (_dot(a, b, _NT), (a, b)),
                lambda r, g: (_dot(g, r[1], _NN), _dot(g, r[0], _TN)))


@jax.custom_vjp
def _bdot_tn(a, b):
    return _dot(a, b, _TN)


_bdot_tn.defvjp(lambda a, b: (_dot(a, b, _TN), (a, b)),
                lambda r, g: (_dot(r[1], g, _NT), _dot(r[0], g, _NN)))


def _fdot(a, b, dims=_NN):
    return lax.dot_general(a, b, (dims, ((), ())), precision=HIGHEST, preferred_element_type=F32)


def _sel3(x, sel, dims, x_first):
    p1 = x.astype(BF16)
    r1 = x - p1.astype(F32)
    p2 = r1.astype(BF16)
    p3 = (r1 - p2.astype(F32)).astype(BF16)
    sel = sel.astype(BF16)
    out = None
    for piece in (p1, p2, p3):
        d = lax.dot_general(*((piece, sel) if x_first else (sel, piece)), (dims, ((), ())), preferred_element_type=F32)
        out = d if out is None else out + d
    return out


@jax.custom_vjp
def _sel_right(x, sel):
    return _sel3(x, sel, _NN, True)


_sel_right.defvjp(lambda x, sel: (_sel3(x, sel, _NN, True), sel),
                  lambda sel, g: (_sel3(g, sel, _NT, True), jnp.zeros_like(sel)))


@jax.custom_vjp
def _sel_left(sel, x):
    return _sel3(x, sel, _NN, False)


_sel_left.defvjp(lambda sel, x: (_sel3(x, sel, _NN, False), sel),
                 lambda sel, g: (jnp.zeros_like(sel), _sel3(g, sel, _TN, False)))


@jax.custom_vjp
def _sel_left_nt(sel, x):
    return _sel3(x, sel, _NT, False)


_sel_left_nt.defvjp(lambda sel, x: (_sel3(x, sel, _NT, False), sel),
                    lambda sel, g: (jnp.zeros_like(sel), _sel3(g, sel, _TN, True)))


def _mm(a, b, *, name, ta=False, tb=False, out_dtype=F32, tm=2048, tn=512, tk=2048, col_shards=False):
    m, k = (a.shape[1], a.shape[0]) if ta else a.shape
    n = b.shape[0] if tb else b.shape[1]
    assert k == (b.shape[1] if tb else b.shape[0]), (a.shape, b.shape, ta, tb)
    if col_shards:
        tn = n // N_DEV
    tm, tn, tk = _tile(m, tm), _tile(n, tn), _tile(k, tk)
    nk = k // tk
    a_spec = pl.BlockSpec((tk, tm), lambda i, j, kk: (kk, i)) if ta else pl.BlockSpec((tm, tk), lambda i, j, kk: (i, kk))
    b_spec = pl.BlockSpec((tn, tk), lambda i, j, kk: (j, kk)) if tb else pl.BlockSpec((tk, tn), lambda i, j, kk: (kk, j))
    dims = ((0 if ta else 1,), (1 if tb else 0,))
    out_spec = pl.BlockSpec((tm, tn), lambda i, j, kk: (i, j))
    out_shape = jax.ShapeDtypeStruct((m, n), out_dtype)
    if col_shards:
        out_shape = jax.ShapeDtypeStruct((N_DEV, 1, m, n // N_DEV), out_dtype)
        out_spec = pl.BlockSpec((None, None, tm, tn), lambda i, j, kk: (j, 0, i, 0))

    def body(a_ref, b_ref, o_ref, acc_ref):
        kk = pl.program_id(2)

        @pl.when(kk == 0)
        def _():
            acc_ref[...] = jnp.zeros_like(acc_ref)

        acc_ref[...] += _dot(a_ref[...], b_ref[...], dims)

        @pl.when(kk == nk - 1)
        def _():
            o_ref[...] = acc_ref[...].astype(o_ref.dtype)

    return _pcall(
        body, name=name, grid=(m // tm, n // tn, nk),
        in_specs=[a_spec, b_spec], out_specs=out_spec, out_shape=out_shape,
        scratch_shapes=[pltpu.VMEM((tm, tn), F32)],
        compiler_params=_params("parallel", "parallel", "arbitrary"),
    )(a, b)


def _rspec(tm, w, cb=0, percol=False):
    return pl.BlockSpec((tm, w), (lambda j, i: (i, cb + j)) if percol else (lambda j, i: (i, cb)))


def _bspec(w, cb=0, percol=False, rows=1):
    return pl.BlockSpec((rows, w), (lambda j, i: (0, cb + j)) if percol else (lambda j, i: (0, cb)))


def _rows(fn, *, name, nrow, ncol=1, ins, outs):
    n_in = len(ins)
    accs = [o[2] for o in outs]

    def body(*refs):
        vals = fn(*[r[...] for r in refs[:n_in]])
        if not isinstance(vals, (tuple, list)):
            vals = (vals,)
        i = pl.program_id(1)
        for ref, val, acc in zip(refs[n_in:], vals, accs):
            if acc:
                @pl.when(i == 0)
                def _(ref=ref):
                    ref[...] = jnp.zeros_like(ref)

                ref[...] += jnp.broadcast_to(val, ref.shape).astype(ref.dtype)
            else:
                ref[...] = val.astype(ref.dtype)

    res = _pcall(
        body, name=name, grid=(ncol, nrow),
        in_specs=[s for _, s in ins], out_specs=[o[1] for o in outs], out_shape=[o[0] for o in outs],
        compiler_params=_params("parallel", "arbitrary"),
    )(*[a for a, _ in ins])
    return res


def _sds(shape, dtype=F32):
    return jax.ShapeDtypeStruct(shape, dtype)


def _rms_fwd(h, g, *, name, tm):
    t, d = h.shape
    return _rows(lambda x, gg: _rms(x, gg), name=name, nrow=t // tm,
                 ins=[(h, _rspec(tm, d)), (g, _bspec(d))],
                 outs=[(_sds((t, d), BF16), _rspec(tm, d), False)])[0]


def _resid_fwd(h, f, g, scale, *, name, tm):
    t, d = h.shape
    return _rows(lambda x, ff, gg: x + scale * _rms(ff, gg), name=name, nrow=t // tm,
                 ins=[(h, _rspec(tm, d)), (f, _rspec(tm, d)), (g, _bspec(d))],
                 outs=[(_sds((t, d)), _rspec(tm, d), False)])[0]


def _resid_bwd(f, g, dh, scale, *, name, tm):
    t, d = f.shape

    def fn(ff, gg, dd):
        _, vjp = jax.vjp(lambda a, b: scale * _rms(a, b), ff, gg)
        return vjp(dd)

    return _rows(fn, name=name, nrow=t // tm,
                 ins=[(f, _rspec(tm, d)), (g, _bspec(d)), (dh, _rspec(tm, d))],
                 outs=[(_sds((t, d), BF16), _rspec(tm, d), False), (_sds((1, d)), _bspec(d), True)])


def _rms_bwd(h, g, dh, dxns, *, name, tm):
    t, d = h.shape

    def fn(x, gg, dd, *dx):
        _, vjp = jax.vjp(_rms, x, gg)
        tot = dx[0]
        for more in dx[1:]:
            tot = tot + more
        dxx, dg = vjp(tot)
        return dd + dxx, dg

    return _rows(fn, name=name, nrow=t // tm,
                 ins=[(h, _rspec(tm, d)), (g, _bspec(d)), (dh, _rspec(tm, d))] + [(x, _rspec(tm, d)) for x in dxns],
                 outs=[(_sds((t, d)), _rspec(tm, d), False), (_sds((1, d)), _bspec(d), True)])


FFN_BLOCKS = 4
NB = FFN_HIDDEN // FFN_BLOCKS
MM_ROWS = 2048


def _ffn_up(xn, wg, wu, *, name):
    t = xn.shape[0]
    tm = _tile(t, MM_ROWS // 2)
    wspec = pl.BlockSpec((None, None, NB, D_MODEL), lambda i, j: (j, 0, 0, 0))

    def body(x_ref, g_ref, u_ref, ab_ref, hh_ref):
        x = x_ref[...]
        a, b = _dot(x, g_ref[...], _NT), _dot(x, u_ref[...], _NT)
        ab_ref[0] = a.astype(ab_ref.dtype)
        ab_ref[1] = b.astype(ab_ref.dtype)
        hh_ref[...] = (_silu(a) * b).astype(hh_ref.dtype)

    return _pcall(
        body, name=name, grid=(t // tm, FFN_BLOCKS),
        in_specs=[pl.BlockSpec((tm, D_MODEL), lambda i, j: (i, 0)), wspec, wspec],
        out_specs=[pl.BlockSpec((None, 2, tm, NB), lambda i, j: (j, 0, i, 0)),
                   pl.BlockSpec((None, tm, NB), lambda i, j: (j, i, 0))],
        out_shape=[_sds((FFN_BLOCKS, 2, t, NB), BF16), _sds((FFN_BLOCKS, t, NB), BF16)],
        compiler_params=_params("parallel", "parallel"),
    )(xn, wg, wu)


def _ffn_down(hh, wd, *, name):
    t = hh.shape[1]
    tm = _tile(t, 512)

    def body(h_ref, w_ref, o_ref):
        acc = _dot(h_ref[0], w_ref[0, 0], _NN)
        for k in range(1, FFN_BLOCKS):
            acc = acc + _dot(h_ref[k], w_ref[k, 0], _NN)
        o_ref[...] = acc

    return _pcall(
        body, name=name, grid=(t // tm,),
        in_specs=[pl.BlockSpec((FFN_BLOCKS, tm, NB), lambda i: (0, i, 0)),
                  pl.BlockSpec((FFN_BLOCKS, 1, NB, D_MODEL), lambda i: (0, 0, 0, 0))],
        out_specs=pl.BlockSpec((tm, D_MODEL), lambda i: (i, 0)), out_shape=_sds((t, D_MODEL)),
        compiler_params=_params("parallel"),
    )(hh, wd)


def _ffn_down_dx(df, wd, ab, *, name):
    t = df.shape[0]
    tm = _tile(t, MM_ROWS // 2)

    def body(d_ref, w_ref, ab_ref, o_ref):
        dhh = _dot(d_ref[...], w_ref[...], _NT)
        _, vjp = jax.vjp(lambda a, b: _silu(a) * b, ab_ref[0].astype(F32), ab_ref[1].astype(F32))
        da, db = vjp(dhh)
        o_ref[0] = da.astype(o_ref.dtype)
        o_ref[1] = db.astype(o_ref.dtype)

    blk = pl.BlockSpec((None, 2, tm, NB), lambda i, j: (j, 0, i, 0))
    return _pcall(
        body, name=name, grid=(t // tm, FFN_BLOCKS),
        in_specs=[pl.BlockSpec((tm, D_MODEL), lambda i, j: (i, 0)),
                  pl.BlockSpec((None, None, NB, D_MODEL), lambda i, j: (j, 0, 0, 0)), blk],
        out_specs=blk, out_shape=_sds((FFN_BLOCKS, 2, t, NB), BF16), compiler_params=_params("parallel", "parallel"),
    )(df, wd, ab)


def _ffn_down_dw(hh, df, *, name, tn=512):
    t = df.shape[0]
    tk = _tile(t, 2048)
    nk = t // tk

    def body(h_ref, d_ref, o_ref, acc_ref):
        kk = pl.program_id(2)

        @pl.when(kk == 0)
        def _():
            acc_ref[...] = jnp.zeros_like(acc_ref)

        acc_ref[...] += _dot(h_ref[...], d_ref[...], _TN)

        @pl.when(kk == nk - 1)
        def _():
            o_ref[...] = acc_ref[...].astype(o_ref.dtype)

    return _pcall(
        body, name=name, grid=(FFN_BLOCKS, D_MODEL // tn, nk),
        in_specs=[pl.BlockSpec((None, tk, NB), lambda j, n, kk: (j, kk, 0)),
                  pl.BlockSpec((tk, tn), lambda j, n, kk: (kk, n))],
        out_specs=pl.BlockSpec((None, None, NB, tn), lambda j, n, kk: (j, 0, 0, n)),
        out_shape=_sds((FFN_BLOCKS, 1, NB, D_MODEL), BF16),
        scratch_shapes=[pltpu.VMEM((NB, tn), F32)],
        compiler_params=_params("parallel", "parallel", "arbitrary"),
    )(hh, df)


def _ffn_up_dx(dab, wg, wu, *, name):
    t = dab.shape[2]
    tm = _tile(t, MM_ROWS // 2)
    wspec = pl.BlockSpec((None, None, NB, D_MODEL), lambda i, j: (j, 0, 0, 0))

    def body(d_ref, g_ref, u_ref, o_ref):
        @pl.when(pl.program_id(1) == 0)
        def _():
            o_ref[...] = jnp.zeros_like(o_ref)

        o_ref[...] += _dot(d_ref[0], g_ref[...], _NN) + _dot(d_ref[1], u_ref[...], _NN)

    return _pcall(
        body, name=name, grid=(t // tm, FFN_BLOCKS),
        in_specs=[pl.BlockSpec((None, 2, tm, NB), lambda i, j: (j, 0, i, 0)), wspec, wspec],
        out_specs=pl.BlockSpec((tm, D_MODEL), lambda i, j: (i, 0)), out_shape=_sds((t, D_MODEL)),
        compiler_params=_params("parallel", "arbitrary"),
    )(dab, wg, wu)


def _ffn_up_dw(xn, dab, *, name):
    t = xn.shape[0]

    def body(x_ref, d_ref, og_ref, ou_ref):
        x = x_ref[...]
        og_ref[...] = _dot(d_ref[0], x, _TN).astype(og_ref.dtype)
        ou_ref[...] = _dot(d_ref[1], x, _TN).astype(ou_ref.dtype)

    out = pl.BlockSpec((None, None, NB, D_MODEL), lambda j: (j, 0, 0, 0))
    return _pcall(
        body, name=name, grid=(FFN_BLOCKS,),
        in_specs=[pl.BlockSpec((t, D_MODEL), lambda j: (0, 0)), pl.BlockSpec((None, 2, t, NB), lambda j: (j, 0, 0, 0))],
        out_specs=[out, out], out_shape=[_sds((FFN_BLOCKS, 1, NB, D_MODEL), BF16)] * 2,
        compiler_params=_params("parallel"),
    )(xn, dab)


def _paired(a):
    return a.reshape(FFN_BLOCKS, 1, NB, D_MODEL)


def _ffn_fwd(h, p, tag, tm):
    xn = _rms_fwd(h, p['pre_g'], name=f"{tag}_rms", tm=tm)
    ab, hh = _ffn_up(xn, _paired(p['w_gate']), _paired(p['w_up']), name=f"{tag}_up")
    w_down, hh = lax.optimization_barrier((p['w_down'], hh))
    f = _ffn_down(hh, _paired(w_down), name=f"{tag}_down")
    out = _resid_fwd(h, f, p['post_g'], 0.5, name=f"{tag}_res", tm=tm)
    return out, (h, xn, ab, hh, f)


def _ffn_bwd(dh, p, saved, tag, tm, after_first):
    h, xn, ab, hh, f = saved
    df, dpost = _resid_bwd(f, p['post_g'], dh, 0.5, name=f"{tag}_res_bwd", tm=tm)
    df = after_first(df)
    dab = _ffn_down_dx(df, _paired(p['w_down']), ab, name=f"{tag}_down_dx")
    bufs = {f'{tag}_w_down': _ffn_down_dw(hh, df, name=f"{tag}_down_dw")}
    dxn = _ffn_up_dx(dab, _paired(p['w_gate']), _paired(p['w_up']), name=f"{tag}_up_dx")
    bufs[f'{tag}_w_gate'], bufs[f'{tag}_w_up'] = _ffn_up_dw(xn, dab, name=f"{tag}_up_dw")
    bufs = {n: a.reshape(N_DEV, 1, FFN_HIDDEN // N_DEV, D_MODEL) for n, a in bufs.items()}
    dh_in, dpre = _rms_bwd(h, p['pre_g'], dh, [dxn], name=f"{tag}_rms_bwd", tm=tm)
    return dh_in, bufs, {f'{tag}_pre_g': dpre, f'{tag}_post_g': dpost}


CONV_COLS = 256


def _shift_down(x, s):
    rows = lax.broadcasted_iota(jnp.int32, x.shape, 0)
    return jnp.where(rows >= s, pltpu.roll(x, s, axis=0), 0.0)


def _shift_up(x, s):
    t = x.shape[0]
    rows = lax.broadcasted_iota(jnp.int32, x.shape, 0)
    return jnp.where(rows < t - s, pltpu.roll(x, t - s, axis=0), 0.0)


def _conv_fwd(proj, col0, w, b, *, name):
    t = proj.shape[0]
    c = w.shape[1]
    cb0 = col0 // CONV_COLS

    def body(x_ref, w_ref, b_ref, o_ref):
        x = x_ref[...]
        acc = x * w_ref[3:4, :] + b_ref[...]
        for k in range(SSD_CONV_WIDTH - 1):
            acc = acc + _shift_down(x, SSD_CONV_WIDTH - 1 - k) * w_ref[k:k + 1, :]
        o_ref[...] = _silu(acc)

    return _pcall(
        body, name=name, grid=(c // CONV_COLS,),
        in_specs=[pl.BlockSpec((t, CONV_COLS), lambda j: (0, cb0 + j)),
                  pl.BlockSpec((SSD_CONV_WIDTH, CONV_COLS), lambda j: (0, j)),
                  pl.BlockSpec((1, CONV_COLS), lambda j: (0, j))],
        out_specs=pl.BlockSpec((t, CONV_COLS), lambda j: (0, j)),
        out_shape=_sds((t, c)), compiler_params=_params("parallel"),
    )(proj, w, b)


def _conv_bwd(proj, col0, w, b, douts, *, name):
    t = proj.shape[0]
    c = w.shape[1]
    cb0 = col0 // CONV_COLS
    first = np.cumsum([0] + [d.shape[1] // CONV_COLS for d in douts])

    def body(x_ref, w_ref, b_ref, *rest):
        d_refs, (dx_ref, dw_ref, db_ref) = rest[:len(douts)], rest[len(douts):]
        j = pl.program_id(0)
        dout = d_refs[-1][...]
        for k in range(len(douts) - 2, -1, -1):
            dout = jnp.where(j < int(first[k + 1]), d_refs[k][...], dout)
        x = x_ref[...]
        shifted = [_shift_down(x, SSD_CONV_WIDTH - 1 - k) for k in range(SSD_CONV_WIDTH - 1)] + [x]
        pre = b_ref[...] + shifted[3] * w_ref[3:4, :]
        for k in range(SSD_CONV_WIDTH - 1):
            pre = pre + shifted[k] * w_ref[k:k + 1, :]
        sg = _sigmoid(pre)
        dpre = dout * (sg * (1.0 + pre * (1.0 - sg)))
        dx = dpre * w_ref[3:4, :]
        for k in range(SSD_CONV_WIDTH - 1):
            dx = dx + _shift_up(dpre, SSD_CONV_WIDTH - 1 - k) * w_ref[k:k + 1, :]
        dx_ref[...] = dx.astype(dx_ref.dtype)
        for k in range(SSD_CONV_WIDTH):
            dw_ref[k:k + 1, :] = jnp.sum(dpre * shifted[k], axis=0, keepdims=True)
        db_ref[...] = jnp.sum(dpre, axis=0, keepdims=True)

    return _pcall(
        body, name=name, grid=(c // CONV_COLS,),
        in_specs=[pl.BlockSpec((t, CONV_COLS), lambda j: (0, cb0 + j)),
                  pl.BlockSpec((SSD_CONV_WIDTH, CONV_COLS), lambda j: (0, j)),
                  pl.BlockSpec((1, CONV_COLS), lambda j: (0, j))] +
                 [pl.BlockSpec((t, CONV_COLS), lambda j, lo=int(first[k]), hi=int(first[k + 1]): (0, jnp.clip(j, lo, hi - 1) - lo))
                  for k in range(len(douts))],
        out_specs=[pl.BlockSpec((t, CONV_COLS), lambda j: (0, j)),
                   pl.BlockSpec((SSD_CONV_WIDTH, CONV_COLS), lambda j: (0, j)),
                   pl.BlockSpec((1, CONV_COLS), lambda j: (0, j))],
        out_shape=[_sds((t, c), BF16), _sds((SSD_CONV_WIDTH, c)), _sds((1, c))],
        compiler_params=_params("arbitrary"),
    )(proj, w, b, *douts)


HALF = 256
HEADS_PER_HALF = 4
PAD_HEADS = 128


def _head_expanders():
    k = lax.broadcasted_iota(jnp.int32, (PAD_HEADS, HALF), 0)
    j = lax.broadcasted_iota(jnp.int32, (PAD_HEADS, HALF), 1)
    kt = lax.broadcasted_iota(jnp.int32, (HALF, PAD_HEADS), 1)
    jt = lax.broadcasted_iota(jnp.int32, (HALF, PAD_HEADS), 0)
    es, ets = [], []
    for half in range(2):
        es.append(jnp.where(k == j // SSD_HEAD_DIM + half * HEADS_PER_HALF, 1.0, 0.0).astype(F32))
        ets.append(jnp.where(kt == jt // SSD_HEAD_DIM + half * HEADS_PER_HALF, 1.0, 0.0).astype(F32))
    return es, ets


def _ssd_chunk(x_lo, x_hi, bm, cm, dtr, dtb8, alog8, dsk8, s_lo, s_hi):
    q = x_lo.shape[0]
    es, ets = _head_expanders()
    rowmean = lambda v: jnp.sum(v, axis=0, keepdims=True) * 0.125
    dt = _softplus(dtr + rowmean(dtb8))
    a = -jnp.exp(rowmean(alog8))
    adt = a * dt
    adt_tot8 = jnp.broadcast_to(jnp.sum(adt, axis=0, keepdims=True), (8, PAD_HEADS))
    ll = lax.broadcasted_iota(jnp.int32, (q, q), 0)
    ss = lax.broadcasted_iota(jnp.int32, (q, q), 1)
    ltri = jnp.where(ll >= ss, 1.0, 0.0).astype(F32)
    lane = lax.broadcasted_iota(jnp.int32, (1, HALF), 1)
    cb = _bdot_nt(cm, bm)
    outs = []
    for half, (x, s_in) in enumerate(((x_lo, s_lo), (x_hi, s_hi))):
        e, et = es[half], ets[half]
        dtf = _sel_right(dt, e)
        af = rowmean(_sel_right(jnp.broadcast_to(a, (8, PAD_HEADS)), e)) * dtf
        dskf = rowmean(_sel_right(dsk8, e))
        acum = _sel_left(ltri, af)
        alast = jnp.sum(af, axis=0, keepdims=True)
        xdt = x * dtf
        ydiag = jnp.zeros((q, HALF), F32)
        for r in range(HEADS_PER_HALF):
            sel = lane == r * SSD_HEAD_DIM
            ac_r = jnp.sum(jnp.where(sel, acum, 0.0), axis=1, keepdims=True)
            a_r = jnp.sum(jnp.where(sel, af, 0.0), axis=1, keepdims=True)
            arow = jnp.sum(jnp.where(ll <= ss, a_r, 0.0), axis=0, keepdims=True)
            decay = jnp.exp(jnp.where(ll >= ss, ac_r - arow, -jnp.inf))
            yh = _bdot_nn(cb * decay, xdt)
            ydiag = ydiag + jnp.where(lane // SSD_HEAD_DIM == r, yh, 0.0)
        st = _bdot_tn(xdt * jnp.exp(alast - acum), bm)
        yoff = _bdot_nt(cm, s_in) * jnp.exp(acum)
        y = ydiag + yoff + dskf * x
        alast_col = jnp.sum(_sel_left_nt(et, adt_tot8), axis=1, keepdims=True) * 0.125
        outs.append((y, jnp.exp(alast_col) * s_in + st))
    return outs[0][0], outs[1][0], outs[0][1], outs[1][1]


SSD_GP = 4


def _ssd_specs(t, rev):
    q, n = SSD_CHUNK, SSD_GP
    nc = t // q
    ci = (lambda c: nc - 1 - c) if rev else (lambda c: c)
    bcol0 = SSD_D_INNER // (n * SSD_STATE)
    return dict(
        x=pl.BlockSpec((q, n * 2 * HALF), lambda g, c: (ci(c), g)),
        bm=pl.BlockSpec((q, n * SSD_STATE), lambda g, c: (ci(c), bcol0 + g)),
        cm=pl.BlockSpec((q, n * SSD_STATE), lambda g, c: (ci(c), bcol0 + SSD_GROUPS // n + g)),
        dt=pl.BlockSpec((n, q, PAD_HEADS), lambda g, c: (g, ci(c), 0)),
        par=pl.BlockSpec((n, 8, PAD_HEADS), lambda g, c: (g, 0, 0)),
        st=pl.BlockSpec((None, n, 2, HALF, SSD_STATE), lambda g, c: (ci(c), g, 0, 0, 0)),
        grp=pl.BlockSpec((q, n * SSD_STATE), lambda g, c: (ci(c), g)),
    )


def _group_cols(k):
    lo = k * 2 * HALF
    return slice(lo, lo + HALF), slice(lo + HALF, lo + 2 * HALF), slice(k * SSD_STATE, (k + 1) * SSD_STATE)


def _ssd_fwd(xc, dt4, dtb, alog, dsk, *, name):
    t = xc.shape[0]
    nc = t // SSD_CHUNK
    sp = _ssd_specs(t, False)

    def body(x, bm, cm, dt, p0, p1, p2, y_ref, sin_ref, st_ref):
        @pl.when(pl.program_id(1) == 0)
        def _():
            st_ref[...] = jnp.zeros_like(st_ref)

        sin_ref[...] = st_ref[...]
        for k in range(SSD_GP):
            lo, hi, bc = _group_cols(k)
            y_lo, y_hi, so_lo, so_hi = _ssd_chunk(x[:, lo], x[:, hi], bm[:, bc], cm[:, bc], dt[k], p0[k], p1[k], p2[k],
                                                  st_ref[k, 0], st_ref[k, 1])
            y_ref[:, lo] = y_lo
            y_ref[:, hi] = y_hi
            st_ref[k, 0] = so_lo
            st_ref[k, 1] = so_hi

    return _pcall(
        body, name=name, grid=(SSD_GROUPS // SSD_GP, nc),
        in_specs=[sp['x'], sp['bm'], sp['cm'], sp['dt'], sp['par'], sp['par'], sp['par']],
        out_specs=[sp['x'], sp['st']],
        out_shape=[_sds((t, SSD_D_INNER)), _sds((nc, SSD_GROUPS, 2, HALF, SSD_STATE))],
        scratch_shapes=[pltpu.VMEM((SSD_GP, 2, HALF, SSD_STATE), F32)],
        compiler_params=_params("parallel", "arbitrary"),
    )(xc, xc, xc, dt4, dtb, alog, dsk)


def _ssd_bwd(xc, dt4, dtb, alog, dsk, sin, dy, *, name):
    t = xc.shape[0]
    nc = t // SSD_CHUNK
    sp = _ssd_specs(t, True)

    def body(x, bm, cm, dt, p0, p1, p2, sin_ref, dy_ref, dx_ref, db_ref, dc_ref, ddt_ref, dp0, dp1, dp2, dst_ref):
        @pl.when(pl.program_id(1) == 0)
        def _():
            dst_ref[...] = jnp.zeros_like(dst_ref)
            for ref in (dp0, dp1, dp2):
                ref[...] = jnp.zeros_like(ref)

        for k in range(SSD_GP):
            lo, hi, bc = _group_cols(k)
            _, vjp = jax.vjp(_ssd_chunk, x[:, lo], x[:, hi], bm[:, bc], cm[:, bc], dt[k], p0[k], p1[k], p2[k],
                             sin_ref[k, 0], sin_ref[k, 1])
            dxl, dxh, dbm, dcm, ddt, g0, g1, g2, ds_lo, ds_hi = vjp(
                (dy_ref[:, lo], dy_ref[:, hi], dst_ref[k, 0], dst_ref[k, 1]))
            dx_ref[:, lo] = dxl
            dx_ref[:, hi] = dxh
            db_ref[:, bc] = dbm
            dc_ref[:, bc] = dcm
            ddt_ref[k] = ddt
            dst_ref[k, 0] = ds_lo
            dst_ref[k, 1] = ds_hi
            for ref, g in ((dp0, g0), (dp1, g1), (dp2, g2)):
                ref[k] += jnp.broadcast_to(jnp.sum(g, axis=0, keepdims=True), g.shape)

    return _pcall(
        body, name=name, grid=(SSD_GROUPS // SSD_GP, nc),
        in_specs=[sp['x'], sp['bm'], sp['cm'], sp['dt'], sp['par'], sp['par'], sp['par'], sp['st'], sp['x']],
        out_specs=[sp['x'], sp['grp'], sp['grp'], sp['dt'], sp['par'], sp['par'], sp['par']],
        out_shape=[_sds((t, SSD_D_INNER)), _sds((t, SSD_GROUPS * SSD_STATE)), _sds((t, SSD_GROUPS * SSD_STATE)),
                   _sds((SSD_GROUPS, t, PAD_HEADS))] + [_sds((SSD_GROUPS, 8, PAD_HEADS))] * 3,
        scratch_shapes=[pltpu.VMEM((SSD_GP, 2, HALF, SSD_STATE), F32)],
        compiler_params=_params("parallel", "arbitrary"),
    )(xc, xc, xc, dt4, dtb, alog, dsk, sin, dy)


def _gatenorm(y, z, g):
    v = y * _silu(z)
    return v * lax.rsqrt(jnp.mean(v * v, axis=-1, keepdims=True) + RMS_EPS) * g


S5_CH = S5_WIDTH // S5_BLOCKS
S5_ST = S5_CH * S5_STATE // S5_GROUP
SCAN_UNROLL = 16


def _cmul(ar, ai, br, bi):
    return ar * br - ai * bi, ar * bi + ai * br


def _segment_power(ar, ai, n):
    assert n & (n - 1) == 0
    for _ in range(n.bit_length() - 1):
        ar, ai = _cmul(ar, ai, ar, ai)
    return ar, ai


def _carry_in(fr, fi, pr, pi, reverse):
    rows = lax.broadcasted_iota(jnp.int32, fr.shape, 0)
    cr = jnp.zeros_like(fr[0:1])
    ci = jnp.zeros_like(cr)
    outr = jnp.zeros_like(fr)
    outi = jnp.zeros_like(fr)
    order = range(6, -1, -1) if reverse else range(1, 8)
    for j in order:
        src = j + 1 if reverse else j - 1
        nr, ni = _cmul(pr[0:1], pi[0:1], cr, ci)
        cr, ci = nr + fr[src:src + 1], ni + fi[src:src + 1]
        outr = jnp.where(rows == j, cr, outr)
        outi = jnp.where(rows == j, ci, outi)
    return outr, outi


def _s5_specs(t):
    return dict(ch=pl.BlockSpec((t, S5_CH), lambda j: (0, j)), st=pl.BlockSpec((t, S5_ST), lambda j: (0, j)),
                lam=pl.BlockSpec((1, S5_ST), lambda j: (0, j)),
                b=pl.BlockSpec((None, S5_CH, S5_ST), lambda j: (j, 0, 0)),
                c=pl.BlockSpec((None, S5_ST, S5_CH), lambda j: (j, 0, 0)))


def _s5_fwd(u5, bre, bim, cre, cim, lr, li, *, name):
    t = u5.shape[0]
    nrt = t // 8

    def body(u_ref, bre_ref, bim_ref, cre_ref, cim_ref, lr_ref, li_ref, sr_ref, si_ref, y_ref, br_ref, bi_ref):
        u = u_ref[...]
        br_ref[...] = _dot(u, bre_ref[...], _NN)
        bi_ref[...] = _dot(u, bim_ref[...], _NN)
        ar = jnp.broadcast_to(lr_ref[...], (8, S5_ST))
        ai = jnp.broadcast_to(li_ref[...], (8, S5_ST))

        def step(r, s, store):
            rows = pl.ds(pl.multiple_of(r * 8, 8), 8)
            nr, ni = _cmul(ar, ai, s[0], s[1])
            nr, ni = nr + br_ref[rows, :], ni + bi_ref[rows, :]
            if store:
                sr_ref[rows, :] = nr
                si_ref[rows, :] = ni
            return nr, ni

        zero = (jnp.zeros((8, S5_ST), F32), jnp.zeros((8, S5_ST), F32))
        fr, fi = lax.fori_loop(0, nrt, lambda r, s: step(r, s, False), zero, unroll=SCAN_UNROLL)
        pr, pi = _segment_power(ar, ai, nrt)
        init = _carry_in(fr, fi, pr, pi, False)
        lax.fori_loop(0, nrt, lambda r, s: step(r, s, True), init, unroll=SCAN_UNROLL)
        y_ref[...] = _dot(sr_ref[...], cre_ref[...], _NN) - _dot(si_ref[...], cim_ref[...], _NN)

    sp = _s5_specs(t)
    w = S5_BLOCKS * S5_ST
    return _pcall(
        body, name=name, grid=(S5_BLOCKS,),
        in_specs=[sp['ch'], sp['b'], sp['b'], sp['c'], sp['c'], sp['lam'], sp['lam']],
        out_specs=[sp['st'], sp['st'], sp['ch']], out_shape=[_sds((t, w)), _sds((t, w)), _sds((t, S5_WIDTH))],
        scratch_shapes=[pltpu.VMEM((t, S5_ST), F32)] * 2, compiler_params=_params("parallel"),
    )(u5, bre, bim, cre, cim, lr, li)


def _s5_bwd(dy, du_direct, u5, sr, si, bre, bim, cre, cim, lr, li, *, name):
    t = u5.shape[0]
    nrt = t // 8

    def body(dy_ref, dd_ref, u_ref, sr_ref, si_ref, bre_ref, bim_ref, cre_ref, cim_ref, lr_ref, li_ref,
             du_ref, dbre_ref, dbim_ref, dcre_ref, dcim_ref, dlr_ref, dli_ref, gr_ref, gi_ref):
        dyv = dy_ref[...]
        gr_ref[...] = _dot(dyv, cre_ref[...], _NT)
        gi_ref[...] = -_dot(dyv, cim_ref[...], _NT)
        dcre_ref[...] = _dot(sr_ref[...], dyv, _TN)
        dcim_ref[...] = -_dot(si_ref[...], dyv, _TN)
        dr_ref, di_ref = gr_ref, gi_ref
        ar = jnp.broadcast_to(lr_ref[...], (8, S5_ST))
        ai = -jnp.broadcast_to(li_ref[...], (8, S5_ST))
        zero = jnp.zeros((8, S5_ST), F32)

        def step1(k, g):
            rows = pl.ds(pl.multiple_of((nrt - 1 - k) * 8, 8), 8)
            nr, ni = _cmul(ar, ai, g[0], g[1])
            return nr + dr_ref[rows, :], ni + di_ref[rows, :]

        fr, fi = lax.fori_loop(0, nrt, step1, (zero, zero), unroll=SCAN_UNROLL)
        pr, pi = _segment_power(ar, ai, nrt)
        init = _carry_in(fr, fi, pr, pi, True)

        def step2(k, carry):
            gr, gi, accr, acci = carry
            r = nrt - 1 - k
            rows = pl.ds(pl.multiple_of(r * 8, 8), 8)
            prev = pl.ds(pl.multiple_of(jnp.maximum(r - 1, 0) * 8, 8), 8)
            nr, ni = _cmul(ar, ai, gr, gi)
            nr, ni = nr + dr_ref[rows, :], ni + di_ref[rows, :]
            gr_ref[rows, :] = nr
            gi_ref[rows, :] = ni
            keep = jnp.where(r > 0, 1.0, 0.0)
            pr_, pi_ = sr_ref[prev, :] * keep, si_ref[prev, :] * keep
            return nr, ni, accr + (pr_ * nr + pi_ * ni), acci + (pr_ * ni - pi_ * nr)

        _, _, accr, acci = lax.fori_loop(0, nrt, step2, (init[0], init[1], zero, zero), unroll=SCAN_UNROLL)
        last = pl.ds((nrt - 1) * 8, 8)
        pr_, pi_ = _shift_down(sr_ref[last, :], 1), _shift_down(si_ref[last, :], 1)
        g0r, g0i = gr_ref[0:8, :], gi_ref[0:8, :]
        accr = accr + (pr_ * g0r + pi_ * g0i)
        acci = acci + (pr_ * g0i - pi_ * g0r)
        dlr_ref[...] = jnp.sum(accr, axis=0, keepdims=True)
        dli_ref[...] = jnp.sum(acci, axis=0, keepdims=True)
        u = u_ref[...]
        dbre_ref[...] = _dot(u, gr_ref[...], _TN)
        dbim_ref[...] = _dot(u, gi_ref[...], _TN)
        du = dd_ref[...] + _dot(gr_ref[...], bre_ref[...], _NT) + _dot(gi_ref[...], bim_ref[...], _NT)
        du_ref[...] = du.astype(du_ref.dtype)

    sp = _s5_specs(t)
    w = S5_BLOCKS * S5_ST
    return _pcall(
        body, name=name, grid=(S5_BLOCKS,),
        in_specs=[sp['ch'], sp['ch'], sp['ch'], sp['st'], sp['st'], sp['b'], sp['b'], sp['c'], sp['c'], sp['lam'], sp['lam']],
        out_specs=[sp['ch'], sp['b'], sp['b'], sp['c'], sp['c'], sp['lam'], sp['lam']],
        out_shape=[_sds((t, S5_WIDTH), BF16), _sds((S5_BLOCKS, S5_CH, S5_ST)), _sds((S5_BLOCKS, S5_CH, S5_ST)),
                   _sds((S5_BLOCKS, S5_ST, S5_CH)), _sds((S5_BLOCKS, S5_ST, S5_CH)), _sds((1, w)), _sds((1, w))],
        scratch_shapes=[pltpu.VMEM((t, S5_ST), F32)] * 2, compiler_params=_params("parallel"),
    )(dy, du_direct, u5, sr, si, bre, bim, cre, cim, lr, li)


def _s5_expander():
    n = lax.broadcasted_iota(jnp.int32, (S5_STATE, S5_STATE * S5_GROUP), 0)
    j = lax.broadcasted_iota(jnp.int32, (S5_STATE, S5_STATE * S5_GROUP), 1)
    return jnp.where(n == j // S5_GROUP, 1.0, 0.0).astype(F32)


def _s5_discretise(lam_re, lam_im, log_step, b_re, b_im):
    lr = jnp.minimum(lam_re, S5_MAX_REAL)
    step = jnp.exp(log_step)
    mag = jnp.exp(lr * step)
    ang = lam_im * step
    lbr, lbi = mag * jnp.cos(ang), mag * jnp.sin(ang)
    p, q = lbr - 1.0, lbi
    den = lr * lr + lam_im * lam_im
    cr, ci = (p * lr + q * lam_im) / den, (q * lr - p * lam_im) / den
    e = _s5_expander()
    cre, cie = _fdot(cr, e), _fdot(ci, e)
    return lbr, lbi, cre * b_re - cie * b_im, cre * b_im + cie * b_re


def _s5_params_fwd(lam_re, lam_im, log_step, b_re, b_im, *, name):
    g, n, w = lam_re.shape[0], S5_STATE, S5_STATE * S5_GROUP

    def body(a, b, c, d, e, o0, o1, o2, o3):
        for ref, val in zip((o0, o1, o2, o3), _s5_discretise(a[...], b[...], c[...], d[...], e[...])):
            ref[...] = val

    return _pcall(body, name=name, out_shape=[_sds((g, n)), _sds((g, n)), _sds((g, w)), _sds((g, w))])(
        lam_re, lam_im, log_step, b_re, b_im)


def _s5_params_bwd(lam_re, lam_im, log_step, b_re, b_im, cts, *, name):
    g, n, w = S5_GROUPS, S5_STATE, S5_STATE * S5_GROUP

    def body(a, b, c, d, e, c0, c1, c2, c3, o0, o1, o2, o3, o4):
        _, vjp = jax.vjp(_s5_discretise, a[...], b[...], c[...], d[...], e[...])
        for ref, val in zip((o0, o1, o2, o3, o4), vjp((c0[...], c1[...], c2[...], c3[...]))):
            ref[...] = val

    return _pcall(body, name=name,
                  out_shape=[_sds((g, n)), _sds((g, n)), _sds((g, 1)), _sds((g, w)), _sds((g, w))])(
        lam_re, lam_im, log_step, b_re, b_im, *cts)


def _s5_prepare(w):
    rows = DEPTH * S5_GROUPS
    lbr, lbi, bbr, bbi = _s5_params_fwd(
        w['s5_lambda_re'].reshape(rows, -1), w['s5_lambda_im'].reshape(rows, -1), w['s5_log_step'].reshape(rows, 1),
        w['s5_b_re'].reshape(rows, -1), w['s5_b_im'].reshape(rows, -1), name="s5_par")
    bd = lambda m: _blockdiag(m.reshape(rows, S5_STATE, S5_GROUP).transpose(0, 2, 1), S5_GROUP, S5_STATE).astype(BF16)
    cd = lambda m: _blockdiag(m.reshape(rows, S5_GROUP, S5_STATE).transpose(0, 2, 1), S5_STATE, S5_GROUP).astype(BF16)
    bre, bim, cre, cim = bd(bbr), bd(bbi), cd(w['s5_c_re']), cd(w['s5_c_im'])
    lr, li = lbr.reshape(DEPTH, 1, -1), lbi.reshape(DEPTH, 1, -1)
    blk = lambda a, i: a[i * S5_BLOCKS:(i + 1) * S5_BLOCKS]
    return [dict(bre=blk(bre, i), bim=blk(bim, i), cre=blk(cre, i), cim=blk(cim, i), lr=lr[i], li=li[i])
            for i in range(DEPTH)]


def _perm(a):
    t, c = a.shape
    return a.reshape(8, t // 8, c).transpose(1, 0, 2).reshape(t, c)


def _unperm(a):
    t, c = a.shape
    return a.reshape(t // 8, 8, c).transpose(1, 0, 2).reshape(t, c)


def _blockdiag(m, rows_inner, cols_inner):
    nblk = m.shape[0] // 8
    m = m.reshape(nblk, 8, rows_inner, cols_inner)
    eye = jnp.eye(8, dtype=m.dtype)
    out = m[:, :, :, None, :] * eye[None, :, None, :, None]
    return out.reshape(nblk, 8 * rows_inner, 8 * cols_inner)


def _blockdiag_extract(m, rows_inner, cols_inner):
    m = m.reshape(S5_BLOCKS, 8, rows_inner, 8, cols_inner)
    d = jnp.diagonal(m, axis1=1, axis2=3)
    return d.transpose(0, 3, 1, 2).reshape(S5_GROUPS, rows_inner, cols_inner)


Z0, XBC0, GA0, GB0 = 0, SSD_D_INNER, SSD_D_INNER + SSD_CONV_DIM, SSD_D_INNER + SSD_CONV_DIM + D_MODEL
BIG = GB0 + D_MODEL


def _mixer_fwd(h, p, tm):
    t = h.shape[0]
    nrow = t // tm
    u = _rms_fwd(h, p['pre_g'], name="mix_rms", tm=tm)
    u_p = _perm(u)
    proj = _mm(u, p['w_big'], name="mix_in")
    dtr = _mm(u, p['w_dt'], name="mix_in_dt")
    u5 = _mm(u_p, p['w_u5'], name="mix_in_s5")
    late, proj = lax.optimization_barrier((p['late'], proj))
    by_rows = lambda a: a.reshape(-1, a.shape[-1])
    p = dict(p, w_a=by_rows(late['w_branch_a']), w_b=by_rows(late['w_branch_b']), w_out=by_rows(late['w_out']),
             w_glu=late['s5_w_glu'][:, 0].transpose(1, 0, 2).reshape(late['s5_w_glu'].shape[2], -1))
    xc = _conv_fwd(proj, XBC0, p['conv_w'], p['conv_b'], name="ssd_conv")
    dt4 = jnp.pad(dtr.reshape(t, SSD_GROUPS, 8).transpose(1, 0, 2), ((0, 0), (0, 0), (0, PAD_HEADS - 8)))
    y_ssd, s_in = _ssd_fwd(xc, dt4, p['dt_bias8'], p['a_log8'], p['d8'], name="ssd_scan")
    gw = SSD_D_INNER // SSD_GROUPS
    ya = _rows(_gatenorm, name="ssd_gate", nrow=nrow, ncol=SSD_GROUPS,
               ins=[(y_ssd, _rspec(tm, gw, 0, True)), (proj, _rspec(tm, gw, Z0 // gw, True)),
                    (p['norm_g'], _bspec(gw, 0, True))],
               outs=[(_sds((t, SSD_D_INNER), BF16), _rspec(tm, gw, 0, True), False)])[0]
    y_a = _mm(ya, p['w_a'], name="mix_a")
    bre, bim, cre, cim, lr, li = (p['s5'][k] for k in ('bre', 'bim', 'cre', 'cim', 'lr', 'li'))
    sr, si, y5 = _s5_fwd(u5, bre, bim, cre, cim, lr, li, name="s5_scan")
    y5g = _rows(lambda a, b, d: _gelu(a + d * b), name="s5_act", nrow=nrow,
                ins=[(y5, _rspec(tm, S5_WIDTH)), (u5, _rspec(tm, S5_WIDTH)), (p['s5_d'], _bspec(S5_WIDTH))],
                outs=[(_sds((t, S5_WIDTH), BF16), _rspec(tm, S5_WIDTH), False)])[0]
    vg = _mm(y5g, p['w_glu'], name="s5_glu")
    ybin = _rows(lambda a, b: a * _sigmoid(b), name="s5_glu_act", nrow=nrow,
                 ins=[(vg, _rspec(tm, S5_WIDTH, 0)), (vg, _rspec(tm, S5_WIDTH, 1))],
                 outs=[(_sds((t, S5_WIDTH), BF16), _rspec(tm, S5_WIDTH), False)])[0]
    y_b = _unperm(_mm(ybin, p['w_b'], name="mix_b"))
    merged = _rows(lambda ga, gb, a, b: _sigmoid(ga) * a + _sigmoid(gb) * b, name="mix_merge", nrow=nrow,
                   ins=[(proj, _rspec(tm, D_MODEL, GA0 // D_MODEL)), (proj, _rspec(tm, D_MODEL, GB0 // D_MODEL)),
                        (y_a, _rspec(tm, D_MODEL)), (y_b, _rspec(tm, D_MODEL))],
                   outs=[(_sds((t, D_MODEL), BF16), _rspec(tm, D_MODEL), False)])[0]
    m = _mm(merged, p['w_out'], name="mix_out")
    out = _resid_fwd(h, m, p['post_g'], 1.0, name="mix_res", tm=tm)
    saved = dict(w_a=p['w_a'], w_b=p['w_b'], w_out=p['w_out'], w_glu=p['w_glu'], h=h, u=u, u_p=u_p, proj=proj, u5=u5, xc=xc, dt4=dt4, s_in=s_in, y_ssd=y_ssd, ya=ya, y_a=y_a,
                 bre=bre, bim=bim, cre=cre, cim=cim, lr=lr, li=li, sr=sr, si=si, y5=y5, y5g=y5g, vg=vg, ybin=ybin,
                 y_b=y_b, merged=merged, m=m)
    return out, saved


def _mixer_bwd(dh, p, s, tm, after_first):
    t = dh.shape[0]
    nrow = t // tm
    proj = s['proj']
    bufs = {}

    def grad_mm(a, b, wname, axis, name):
        dw = _mm(a, b, ta=True, name=name, out_dtype=BF16, col_shards=axis == 'cols')
        bufs[wname] = dw if axis == 'cols' else dw.reshape(N_DEV, 1, dw.shape[0] // N_DEV, dw.shape[1])

    dm, dpost = _resid_bwd(s['m'], p['post_g'], dh, 1.0, name="mix_res_bwd", tm=tm)
    dm = after_first(dm)
    dmerged = _mm(dm, s['w_out'], tb=True, name="mix_out_dx")
    grad_mm(s['merged'], dm, 'w_out', 'rows', "mix_out_dw")

    def merge_bwd(ga, gb, a, b, d):
        _, vjp = jax.vjp(lambda ga_, gb_, a_, b_: _sigmoid(ga_) * a_ + _sigmoid(gb_) * b_, ga, gb, a, b)
        dga, dgb, da, db = vjp(d)
        return jnp.concatenate([dga, dgb], axis=1), da, db

    dgab, dy_a, dy_b = _rows(
        merge_bwd, name="mix_merge_bwd", nrow=nrow,
        ins=[(proj, _rspec(tm, D_MODEL, GA0 // D_MODEL)), (proj, _rspec(tm, D_MODEL, GB0 // D_MODEL)),
             (s['y_a'], _rspec(tm, D_MODEL)), (s['y_b'], _rspec(tm, D_MODEL)), (dmerged, _rspec(tm, D_MODEL))],
        outs=[(_sds((t, 2 * D_MODEL), BF16), _rspec(tm, 2 * D_MODEL), False),
              (_sds((t, D_MODEL), BF16), _rspec(tm, D_MODEL), False),
              (_sds((t, D_MODEL), BF16), _rspec(tm, D_MODEL), False)])
    dya = _mm(dy_a, s['w_a'], tb=True, name="mix_a_dx")
    grad_mm(s['ya'], dy_a, 'w_branch_a', 'rows', "mix_a_dw")
    gw = SSD_D_INNER // SSD_GROUPS

    def gate_bwd(y, z, g, d):
        _, vjp = jax.vjp(_gatenorm, y, z, g)
        return vjp(d)

    dy_ssd, dz, dnorm = _rows(
        gate_bwd, name="ssd_gate_bwd", nrow=nrow, ncol=SSD_GROUPS,
        ins=[(s['y_ssd'], _rspec(tm, gw, 0, True)), (proj, _rspec(tm, gw, Z0 // gw, True)),
             (p['norm_g'], _bspec(gw, 0, True)), (dya, _rspec(tm, gw, 0, True))],
        outs=[(_sds((t, SSD_D_INNER)), _rspec(tm, gw, 0, True), False),
              (_sds((t, SSD_D_INNER), BF16), _rspec(tm, gw, 0, True), False),
              (_sds((1, SSD_D_INNER)), _bspec(gw, 0, True), True)])
    dxs, dbm, dcm, ddt4, ddtb, dalog, ddsk = _ssd_bwd(s['xc'], s['dt4'], p['dt_bias8'], p['a_log8'], p['d8'],
                                                      s['s_in'], dy_ssd, name="ssd_scan_bwd")
    dxbc, dconv_w, dconv_b = _conv_bwd(proj, XBC0, p['conv_w'], p['conv_b'], (dxs, dbm, dcm), name="ssd_conv_bwd")
    ddtr = ddt4[:, :, :8].transpose(1, 0, 2).reshape(t, SSD_HEADS)
    dy_bp = _perm(dy_b)
    dybin = _mm(dy_bp, s['w_b'], tb=True, name="mix_b_dx")
    grad_mm(s['ybin'], dy_bp, 'w_branch_b', 'rows', "mix_b_dw")

    def glu_bwd(a, b, d):
        _, vjp = jax.vjp(lambda a_, b_: a_ * _sigmoid(b_), a, b)
        da, db = vjp(d)
        return jnp.concatenate([da, db], axis=1)

    dvg = _rows(glu_bwd, name="s5_glu_act_bwd", nrow=nrow,
                ins=[(s['vg'], _rspec(tm, S5_WIDTH, 0)), (s['vg'], _rspec(tm, S5_WIDTH, 1)), (dybin, _rspec(tm, S5_WIDTH))],
                outs=[(_sds((t, 2 * S5_WIDTH), BF16), _rspec(tm, 2 * S5_WIDTH), False)])[0]
    dy5g = _mm(dvg, s['w_glu'], tb=True, name="s5_glu_dx")
    grad_mm(s['y5g'], dvg, 's5_w_glu', 'cols', "s5_glu_dw")

    def act_bwd(a, b, d, g):
        _, vjp = jax.vjp(lambda a_, b_, d_: _gelu(a_ + d_ * b_), a, b, d)
        return vjp(g)

    dy5, du5_direct, ds5d = _rows(
        act_bwd, name="s5_act_bwd", nrow=nrow,
        ins=[(s['y5'], _rspec(tm, S5_WIDTH)), (s['u5'], _rspec(tm, S5_WIDTH)), (p['s5_d'], _bspec(S5_WIDTH)),
             (dy5g, _rspec(tm, S5_WIDTH))],
        outs=[(_sds((t, S5_WIDTH), BF16), _rspec(tm, S5_WIDTH), False), (_sds((t, S5_WIDTH)), _rspec(tm, S5_WIDTH), False),
              (_sds((1, S5_WIDTH)), _bspec(S5_WIDTH), True)])
    du5, dbre, dbim, dcre, dcim, dlr, dli = _s5_bwd(dy5, du5_direct, s['u5'], s['sr'], s['si'], s['bre'], s['bim'],
                                                     s['cre'], s['cim'], s['lr'], s['li'], name="s5_scan_bwd")
    du_p = _mm(du5, p['w_u5'], tb=True, name="mix_in_s5_dx")
    dw_u5 = _mm(s['u_p'], du5, ta=True, name="mix_in_s5_dw", out_dtype=BF16)
    ext_b = lambda m: _blockdiag_extract(m, S5_GROUP, S5_STATE).transpose(0, 2, 1).reshape(S5_GROUPS, S5_STATE * S5_GROUP)
    dlam_re, dlam_im, dlog_step, db_re, db_im = _s5_params_bwd(
        p['lam_re'], p['lam_im'], p['log_step'], p['b_re'], p['b_im'],
        (dlr.reshape(S5_GROUPS, S5_STATE), dli.reshape(S5_GROUPS, S5_STATE), ext_b(dbre), ext_b(dbim)), name="s5_par_bwd")
    dc_re = _blockdiag_extract(dcre, S5_STATE, S5_GROUP).transpose(0, 2, 1)
    dc_im = _blockdiag_extract(dcim, S5_STATE, S5_GROUP).transpose(0, 2, 1)
    dproj = jnp.concatenate([dz, dxbc, dgab], axis=1)
    du_big = _mm(dproj, p['w_big'], tb=True, name="mix_in_dx")
    du_dt = _mm(ddtr, p['w_dt'], tb=True, name="mix_in_dt_dx")
    dw_big = _mm(s['u'], dproj, ta=True, name="mix_in_dw", out_dtype=BF16)
    dw_dt = _mm(s['u'], ddtr, ta=True, name="mix_in_dt_dw", out_dtype=BF16)
    dh_in, dpre = _rms_bwd(s['h'], p['pre_g'], dh, [du_big, du_dt, _unperm(du_p)], name="mix_rms_bwd", tm=tm)
    dw_in = jnp.concatenate([dw_big[:, :GA0], dw_dt, dw_u5, dw_big[:, GA0:]], axis=1)
    bufs['w_in'] = dw_in.reshape(D_MODEL, N_DEV, -1).transpose(1, 0, 2)[:, None]
    grads = {
        'mix_pre_g': dpre, 'mix_post_g': dpost, 'ssd_conv_w': dconv_w, 'ssd_conv_b': dconv_b,
        'ssd_dt_bias': ddtb[:, 0, :8].reshape(-1), 'ssd_a_log': dalog[:, 0, :8].reshape(-1),
        'ssd_d': ddsk[:, 0, :8].reshape(-1), 'ssd_norm_g': dnorm,
        's5_lambda_re': dlam_re, 's5_lambda_im': dlam_im,
        's5_b_re': db_re.reshape(S5_GROUPS, S5_STATE, S5_GROUP), 's5_b_im': db_im.reshape(S5_GROUPS, S5_STATE, S5_GROUP),
        's5_c_re': dc_re, 's5_c_im': dc_im, 's5_log_step': dlog_step.reshape(-1), 's5_d': ds5d,
    }
    return dh_in, bufs, grads


HBM_SPEC = pl.BlockSpec(memory_space=pltpu.HBM)


def _place():
    return lax.axis_index("x"), lax.axis_index("y"), lax.axis_index("c")


GATHER_COLLECTIVE_ID = 1


def _all_gather(shards, *, name, on_sequencer=False):
    n = len(shards)

    def body(*refs):
        x_refs, out_refs = refs[:n], refs[n:2 * n]
        send_sems, recv_sems, local_sems = refs[2 * n:]
        x, y, c = _place()
        me, sibling = (x, y, c), (x, y, 1 - c)
        chips = [(1 - x, y), (x, 1 - y), (1 - x, 1 - y)]
        if on_sequencer:
            _handshake([sibling] + [(*chip, c) for chip in chips])

        def slot(o, px, py, pc):
            return out_refs[o].at[4 * px + 2 * py + pc]

        def copy(o, k, block, to, src=None):
            return pltpu.make_async_remote_copy(
                src_ref=slot(o, *block) if src is None else src, dst_ref=slot(o, *block),
                send_sem=send_sems.at[7 * o + k], recv_sem=recv_sems.at[7 * o + k], device_id=to, device_id_type=MESH)

        mine = [pltpu.make_async_copy(x_refs[o], slot(o, *me), local_sems.at[o]) for o in range(n)]
        for cp in mine:
            cp.start()
        first = []
        for j, chip in enumerate(chips):
            first += [copy(o, 1 + j, me, (*chip, c), src=x_refs[o]) for o in range(n)]
        first += [copy(o, 0, me, sibling, src=x_refs[o]) for o in range(n)]
        for cp in first:
            cp.start()
        passed = []
        for j, chip in enumerate(chips):
            for o in range(n):
                copy(o, 1 + j, (*chip, c), me).wait_recv()
                passed.append(copy(o, 4 + j, (*chip, c), sibling))
                passed[-1].start()
        for o in range(n):
            copy(o, 0, sibling, me).wait_recv()
        for j, chip in enumerate(chips):
            for o in range(n):
                copy(o, 4 + j, (*chip, 1 - c), me).wait_recv()
        for cp in first + passed:
            cp.wait_send()
        for cp in mine:
            cp.wait()

    out_shape = [jax.ShapeDtypeStruct((N_DEV,) + s.shape, s.dtype) for s in shards]
    sems = [pltpu.SemaphoreType.DMA((7 * n,)), pltpu.SemaphoreType.DMA((7 * n,)), pltpu.SemaphoreType.DMA((n,))]
    if on_sequencer:
        return _scall(body, name=name, out_type=out_shape, scratch_types=sems, collective_id=GATHER_COLLECTIVE_ID)(*shards)
    return _pcall(body, name=name, in_specs=[HBM_SPEC] * n, out_specs=[HBM_SPEC] * n, out_shape=out_shape,
                  scratch_shapes=sems)(*shards)


N_CHIPS = 4


SIBLING_COLLECTIVE_ID = 2
CHIPS_COLLECTIVE_ID = 3


def _handshake(peers):
    barrier = pltpu.get_barrier_semaphore()
    for peer in peers:
        pl.semaphore_signal(barrier, inc=1, device_id=peer, device_id_type=MESH)
    pl.semaphore_wait(barrier, len(peers))


def _exchange_sibling(grads, *, name):
    n = len(grads)

    def body(*refs):
        p_refs, q_refs = refs[:n], refs[n:2 * n]
        send_sems, recv_sems = refs[2 * n:]
        x, y, c = _place()
        _handshake([(x, y, 1 - c)])
        copies = [pltpu.make_async_remote_copy(
            src_ref=p_refs[o].at[k, 1 - c], dst_ref=q_refs[o].at[k], send_sem=send_sems.at[N_CHIPS * o + k],
            recv_sem=recv_sems.at[N_CHIPS * o + k], device_id=(x, y, 1 - c), device_id_type=MESH)
            for o in range(n) for k in range(N_CHIPS)]
        for cp in copies:
            cp.start()
        for cp in copies:
            cp.wait()

    return _scall(
        body, name=name, out_type=[jax.ShapeDtypeStruct((N_CHIPS,) + g.shape[2:], g.dtype) for g in grads],
        scratch_types=[pltpu.SemaphoreType.DMA((N_CHIPS * n,)), pltpu.SemaphoreType.DMA((N_CHIPS * n,))],
        collective_id=SIBLING_COLLECTIVE_ID,
    )(*grads)


def _pair_sum(own, got, *, name):
    _, _, r, l = own.shape
    tr = _tile(r, 512, 16)
    c = lax.axis_index("c").astype(jnp.int32).reshape(1)

    def body(c_ref, p_ref, q_ref, o_ref):
        o_ref[...] = (p_ref[...].astype(F32) + q_ref[...].astype(F32)).astype(o_ref.dtype)

    return _pcall(
        body, name=name,
        grid_spec=pltpu.PrefetchScalarGridSpec(
            num_scalar_prefetch=1, grid=(N_CHIPS, r // tr),
            in_specs=[pl.BlockSpec((None, None, tr, l), lambda k, i, cr: (k, cr[0], i, 0)),
                      pl.BlockSpec((None, tr, l), lambda k, i, cr: (k, i, 0))],
            out_specs=pl.BlockSpec((None, tr, l), lambda k, i, cr: (k, i, 0))),
        out_shape=jax.ShapeDtypeStruct((N_CHIPS, r, l), own.dtype),
        compiler_params=_params("parallel", "parallel"),
    )(c, own, got)


def _exchange_chips(parts, *, name):
    n = len(parts)

    def body(*refs):
        p_refs, g_refs = refs[:n], refs[n:2 * n]
        send_sems, recv_sems, local_sems = refs[2 * n:]
        x, y, c = _place()
        mine = 2 * x + y
        chips = [(1 - x, y), (x, 1 - y), (1 - x, 1 - y)]
        _handshake([(*chip, c) for chip in chips])
        own = [pltpu.make_async_copy(p_refs[o].at[mine], g_refs[o].at[mine], local_sems.at[o]) for o in range(n)]
        for cp in own:
            cp.start()
        copies = []
        for j, (px, py) in enumerate(chips):
            copies += [pltpu.make_async_remote_copy(
                src_ref=p_refs[o].at[2 * px + py], dst_ref=g_refs[o].at[mine], send_sem=send_sems.at[3 * o + j],
                recv_sem=recv_sems.at[3 * o + j], device_id=(px, py, c), device_id_type=MESH) for o in range(n)]
        for cp in copies:
            cp.start()
        for cp in copies:
            cp.wait()
        for cp in own:
            cp.wait()

    return _scall(
        body, name=name, out_type=[jax.ShapeDtypeStruct(p.shape, p.dtype) for p in parts],
        scratch_types=[pltpu.SemaphoreType.DMA((3 * n,)), pltpu.SemaphoreType.DMA((3 * n,)), pltpu.SemaphoreType.DMA((n,))],
        collective_id=CHIPS_COLLECTIVE_ID,
    )(*parts)


def _sum_slots(g, *, name):
    n, r, l = g.shape
    tr = _tile(r, 512, 16)

    def body(g_ref, o_ref):
        acc = g_ref[0].astype(F32)
        for k in range(1, n):
            acc = acc + g_ref[k].astype(F32)
        o_ref[...] = acc

    return _pcall(
        body, name=name, grid=(r // tr,), in_specs=[pl.BlockSpec((n, tr, l), lambda i: (0, i, 0))],
        out_specs=pl.BlockSpec((tr, l), lambda i: (i, 0)), out_shape=_sds((r, l)),
        compiler_params=_params("parallel"),
    )(g)


TRANSPOSED = ('ffn1_w_gate', 'ffn1_w_up', 'ffn2_w_gate', 'ffn2_w_up')
GATHER_CHUNKS = (('ffn1', ['ffn1_w_gate', 'ffn1_w_up']), ('ffn1_down', ['ffn1_w_down']),
                 ('mix_in', ['w_in', 'ssd_conv_w']), ('mix', ['w_branch_a', 's5_w_glu', 'w_branch_b', 'w_out']),
                 ('ffn2', ['ffn2_w_gate', 'ffn2_w_up']), ('ffn2_down', ['ffn2_w_down']))
LATE = ('ffn1_w_down', 'ffn2_w_down', 'w_branch_a', 's5_w_glu', 'w_branch_b', 'w_out')
SUBLAYERS = (('ffn1', ['ffn1_w_gate', 'ffn1_w_up', 'ffn1_w_down']),
             ('mix', ['w_in', 'ssd_conv_w', 'w_branch_a', 's5_w_glu', 'w_branch_b', 'w_out']),
             ('ffn2', ['ffn2_w_gate', 'ffn2_w_up', 'ffn2_w_down']))


def _gather_weights(w):
    layers, first = [], None
    for i in range(DEPTH):
        g = {}
        for tag, names in GATHER_CHUNKS:
            shards =[w[n][i:i + 1] if n == 'ssd_conv_w' else
                      (w[n][i:i + 1].transpose(0, 2, 1) if n in TRANSPOSED else w[n][i:i + 1]).astype(BF16) for n in names]
            if first is None:
                first = got = _all_gather(shards, name=f"gather_{tag}")
            else:
                shards, first = lax.optimization_barrier((shards, first))
                got = _all_gather(shards, name=f"gather_{tag}", on_sequencer=True)
            g.update(zip(names, got))
        layers.append(g)
    layers[0].update(zip(GATHER_CHUNKS[0][1], first))
    return layers


class _ReduceScatter:
    @staticmethod
    def sibling(tag, bufs):
        names = list(bufs)
        own = [bufs[n].reshape((N_CHIPS, 2) + bufs[n].shape[1:]) for n in names]
        return (tag, names), (own, _exchange_sibling(own, name=f"reduce_sibling_{tag}"))

    @staticmethod
    def chips(meta, arrays):
        (tag, names), (own, got) = meta, arrays
        flat = lambda a, lead: a.reshape(lead + (-1, a.shape[-1]))
        parts = [_pair_sum(flat(o, (N_CHIPS, 2)), flat(g, (N_CHIPS,)), name=f"reduce_pair_sum_{n}").reshape(g.shape)
                 for n, o, g in zip(names, own, got)]
        return names, _exchange_chips(parts, name=f"reduce_chips_{tag}")

    @staticmethod
    def done(names, slots):
        return dict(zip(names, slots))

    @staticmethod
    def small(grads):
        return _reduce_small(grads)


def _reduce_small(grads):
    flat = jnp.concatenate([g.astype(F32).reshape(-1) for g in grads.values()])
    pad = (-flat.shape[0]) % (8 * LANES)
    flat = jnp.concatenate([flat, jnp.zeros((pad,), F32)]).reshape(-1, LANES)
    gathered = _all_gather([flat], name="gather_small_grads", on_sequencer=True)[0]
    total = _sum_slots(gathered, name="sum_small_grads").reshape(-1)
    out, o = {}, 0
    for n, g in grads.items():
        out[n] = total[o:o + g.size].reshape(g.shape)
        o += g.size
    return out


def _adamw(w, g, m, v, *, name, slots=False):
    shape = w.shape
    if slots:
        lyr, rows, lanes = shape
        w2, m2, v2 = w, m, v
        tr = _tile(rows, 256, 16)
        nrt = rows // tr
        grid = (lyr, nrt)
        spec = pl.BlockSpec((None, tr, lanes), lambda l, i: (l, i, 0))
        g_specs = [pl.BlockSpec((N_CHIPS, None, tr, lanes),
                                lambda l, i, k=k: (0, 0, jnp.where(l == k, i, jnp.where(l > k, nrt - 1, 0)), 0))
                   for k in range(lyr)]
        g_args = list(g)
        out_shape = [_sds(shape)] * 4
    else:
        lanes = shape[-1] if (shape[-1] >= 128 or w.size % LANES) else LANES
        as2d = lambda a: a.reshape(-1, lanes)
        w2, m2, v2 = as2d(w), as2d(m), as2d(v)
        r = w2.shape[0]
        tr = _tile(r, 256, 8)
        grid = (1, r // tr)
        spec = pl.BlockSpec((tr, lanes), lambda l, i: (i, 0))
        g_specs, g_args = [spec], [as2d(g)]
        out_shape = [_sds((r, lanes))] * 4
    n_g = len(g_args)

    def body(w_ref, *rest):
        g_refs = rest[:n_g]
        m_ref, v_ref, go_ref, d_ref, mo_ref, vo_ref = rest[n_g:]
        if slots:
            gg = None
            for k, g_ref in enumerate(g_refs):
                tot = g_ref[0].astype(F32)
                for c in range(1, N_CHIPS):
                    tot = tot + g_ref[c].astype(F32)
                gg = tot if gg is None else jnp.where(pl.program_id(0) == k, tot, gg)
        else:
            gg = g_refs[0][...]
        go_ref[...] = gg
        mn = ADAM_B1 * m_ref[...] + (1.0 - ADAM_B1) * gg
        vn = ADAM_B2 * v_ref[...] + (1.0 - ADAM_B2) * (gg * gg)
        m_hat = mn / (1.0 - ADAM_B1 ** ADAM_STEP)
        v_hat = vn / (1.0 - ADAM_B2 ** ADAM_STEP)
        d_ref[...] = -ADAM_LR * (m_hat / (jnp.sqrt(v_hat) + ADAM_EPS) + ADAM_WD * w_ref[...])
        mo_ref[...] = mn
        vo_ref[...] = vn

    res = _pcall(
        body, name=name, grid=grid, in_specs=[spec] + g_specs + [spec, spec], out_specs=[spec] * 4,
        out_shape=out_shape, compiler_params=_params("arbitrary", "arbitrary"),
    )(w2, *g_args, m2, v2)
    return tuple(a.reshape(shape) for a in res)


def _sublayer_params(w, g, i, k, s5):
    row = lambda a: a.astype(F32).reshape(1, -1)
    if k != 'mix':
        return dict(layer=i, pre_g=row(w[f'{k}_pre_g'][i]), post_g=row(w[f'{k}_post_g'][i]),
                    w_gate=g[f'{k}_w_gate'], w_up=g[f'{k}_w_up'], w_down=g[f'{k}_w_down'])
    head8 = lambda a: jnp.broadcast_to(
        jnp.pad(a.astype(F32).reshape(SSD_GROUPS, 1, 8), ((0, 0), (0, 0), (0, PAD_HEADS - 8))), (SSD_GROUPS, 8, PAD_HEADS))
    by_cols = lambda n: g[n][:, 0].transpose(1, 0, 2).reshape(g[n].shape[2], -1)
    w_in = by_cols('w_in')
    s = np.cumsum([SSD_D_INNER, SSD_CONV_DIM, SSD_HEADS, S5_WIDTH, D_MODEL])
    return dict(
        layer=i, s5=s5, pre_g=row(w['mix_pre_g'][i]), post_g=row(w['mix_post_g'][i]),
        w_big=jnp.concatenate([w_in[:, :s[1]], w_in[:, s[3]:]], axis=1), w_dt=w_in[:, s[1]:s[2]], w_u5=w_in[:, s[2]:s[3]],
        conv_w=by_cols('ssd_conv_w'), conv_b=row(w['ssd_conv_b'][i]),
        dt_bias8=head8(w['ssd_dt_bias'][i]), a_log8=head8(w['ssd_a_log'][i]), d8=head8(w['ssd_d'][i]),
        norm_g=row(w['ssd_norm_g'][i]), late={n: g[n] for n in SUBLAYERS[1][1] if n in LATE},
        lam_re=w['s5_lambda_re'][i], lam_im=w['s5_lambda_im'][i], log_step=w['s5_log_step'][i].reshape(S5_GROUPS, 1),
        b_re=w['s5_b_re'][i].reshape(S5_GROUPS, -1), b_im=w['s5_b_im'][i].reshape(S5_GROUPS, -1),
        c_re=w['s5_c_re'][i], c_im=w['s5_c_im'][i], s5_d=row(w['s5_d'][i]),
    )


def _loss_head(h, target, *, tm):
    t, d = h.shape

    def fn(y, tgt):
        err = y - tgt
        return err * (1.0 / d), jnp.sum(0.5 * jnp.sum(err * err, axis=-1, keepdims=True) * (1.0 / d), axis=0, keepdims=True)

    dy, loss = _rows(fn, name="loss_head", nrow=t // tm,
                     ins=[(h, _rspec(tm, d)), (target, _rspec(tm, d))],
                     outs=[(_sds((t, d)), _rspec(tm, d), False), (_sds((1, 128)), _bspec(128), True)])
    return dy, loss[0, 0]


def _forward_backward(h, target, w, g, rs):
    t = h.shape[0]
    tm = _tile(t, 512, 8)
    s5 = None
    layers, saved = [], []
    for i in range(DEPTH):
        gi, ps, ss = dict(g[i]), [], []
        for tag, names in SUBLAYERS:
            if tag == 'mix' and s5 is None:
                mine = {n: w[n] for n in WEIGHTS if n.startswith('s5_') and n not in SHARDED}
                mine, h = lax.optimization_barrier((mine, h))
                s5 = _s5_prepare(mine)
            early = [n for n in names if n not in LATE]
            tied, h, s5 = lax.optimization_barrier(([gi[n] for n in early], h, s5))
            gi.update(zip(early, tied))
            p = _sublayer_params(w, gi, i, tag, s5[i] if tag == 'mix' else None)
            h, s = _mixer_fwd(h, p, tm) if tag == 'mix' else _ffn_fwd(h, p, tag, tm)
            ps.append(p)
            ss.append(s)
        layers.append(ps)
        saved.append(ss)
    dh, loss = _loss_head(h, target, tm=tm)
    reduced, small = [{} for _ in range(DEPTH)], [{} for _ in range(DEPTH)]
    in_sibling, in_chips = None, None

    def start_chips(x):
        nonlocal in_sibling, in_chips
        if in_sibling is not None:
            layer, meta, arrays = in_sibling
            arrays, x = lax.optimization_barrier((arrays, x))
            in_sibling, in_chips = None, (layer,) + tuple(rs.chips(meta, arrays))
        return x

    def finish_chips(x):
        nonlocal in_chips
        if in_chips is not None:
            layer, names, slots = in_chips
            slots, x = lax.optimization_barrier((slots, x))
            reduced[layer].update(rs.done(names, slots))
            in_chips = None
        return x

    for i in reversed(range(DEPTH)):
        for k in reversed(range(len(SUBLAYERS))):
            tag = SUBLAYERS[k][0]
            if tag == 'mix':
                dh, bufs, grads = _mixer_bwd(dh, layers[i][k], saved[i][k], tm, start_chips)
            else:
                dh, bufs, grads = _ffn_bwd(dh, layers[i][k], saved[i][k], tag, tm, start_chips)
            small[i].update(grads)
            dh = finish_chips(dh)
            in_sibling = (i,) + tuple(rs.sibling(tag, bufs))
            if tag == 'mix' and i + 1 < DEPTH:
                small[i + 1], dh = lax.optimization_barrier((small[i + 1], dh))
        if i == 0:
            small[i]['loss'] = loss.reshape(1)
        small[i] = rs.small(small[i])
    loss = small[0].pop('loss')[0]
    dh = finish_chips(start_chips(dh))
    shapes = {n: (w[n].shape[:-1] + (SSD_CONV_DIM,) if n == 'ssd_conv_w' else w[n].shape) for n in SMALL_ORDER}
    stacked = {n: jnp.stack([small[i][n].reshape(shapes[n][1:]) for i in range(DEPTH)]) for n in SMALL_ORDER}
    return loss, dh, reduced, stacked


def kernel(*args):
    n_w = len(WEIGHTS)
    x, target = args[0], args[1 + n_w]
    w = dict(zip(WEIGHTS, args[1:1 + n_w]))
    m = dict(zip(WEIGHTS, args[2 + n_w:2 + 2 * n_w]))
    v = dict(zip(WEIGHTS, args[2 + 2 * n_w:2 + 3 * n_w]))
    t = x.shape[1]

    g = _gather_weights(w)
    loss, dx, slots, small = _forward_backward(x.reshape(t, D_MODEL), target.reshape(t, D_MODEL), w, g, _ReduceScatter)
    me = 4 * lax.axis_index("x") + 2 * lax.axis_index("y") + lax.axis_index("c")
    cols = w['ssd_conv_w'].shape[-1]
    small['ssd_conv_w'] = lax.dynamic_slice_in_dim(small['ssd_conv_w'], me * cols, cols, axis=2)

    grad, delta, new_m, new_v = {}, {}, {}, {}
    for n in WEIGHTS:
        sharded = n in slots[0]
        view = (lambda a: a.transpose(0, 2, 1)) if n in TRANSPOSED else (lambda a: a)
        res = _adamw(view(w[n]), [slots[i][n] for i in range(DEPTH)] if sharded else small[n], view(m[n]), view(v[n]),
                     name=f"adamw_{n}", slots=sharded)
        grad[n], delta[n], new_m[n], new_v[n] = (view(a) for a in res)
    return (loss, dx.reshape(x.shape), *[grad[n] for n in WEIGHTS], *[delta[n] for n in WEIGHTS],
            *[new_m[n] for n in WEIGHTS], *[new_v[n] for n in WEIGHTS])
```

```python
import math

import numpy as np
import jax
import jax.numpy as jnp
from jax import lax
from jax.experimental import pallas as pl
from jax.experimental.pallas import tpu as pltpu
from jax.experimental.pallas import tpu_sc as plsc

F32 = jnp.float32
BF16 = jnp.bfloat16
MESH = pl.DeviceIdType.MESH
HIGHEST = lax.Precision.HIGHEST

D_MODEL = 1024
DEPTH = 2
FFN_HIDDEN = 2816
SSD_D_INNER = 2048
SSD_HEADS = 32
SSD_HEAD_DIM = 64
SSD_GROUPS = 4
SSD_STATE = 128
SSD_CHUNK = 256
SSD_CONV_DIM = 3072
SSD_CONV_WIDTH = 4
S5_WIDTH = 1024
S5_GROUP = 16
S5_GROUPS = 64
S5_STATE = 64
S5_MAX_REAL = -1e-4
S5_BLOCKS = 8
RMS_EPS = 1e-6
N_DEV = 8
LANES = 1024

ADAM_LR = 0.001
ADAM_B1 = 0.9
ADAM_B2 = 0.999
ADAM_EPS = 1e-08
ADAM_WD = 0.01
ADAM_STEP = 10

VMEM_LIMIT_BYTES = 48 * 1024 * 1024

WEIGHTS = ['ffn1_pre_g', 'ffn1_post_g', 'ffn1_w_gate', 'ffn1_w_up', 'ffn1_w_down', 'mix_pre_g', 'mix_post_g',
           'w_in', 'ssd_conv_w', 'ssd_conv_b', 'ssd_dt_bias', 'ssd_a_log', 'ssd_d', 'ssd_norm_g', 'w_branch_a',
           's5_lambda_re', 's5_lambda_im', 's5_b_re', 's5_b_im', 's5_c_re', 's5_c_im', 's5_log_step', 's5_d',
           's5_w_glu', 'w_branch_b', 'w_out', 'ffn2_pre_g', 'ffn2_post_g', 'ffn2_w_gate', 'ffn2_w_up',
           'ffn2_w_down']
SHARDED = {'ffn1_w_gate': 2, 'ffn1_w_up': 2, 'ffn1_w_down': 1, 'w_in': 2, 'ssd_conv_w': 2, 'w_branch_a': 1,
           's5_w_glu': 2, 'w_branch_b': 1, 'w_out': 1, 'ffn2_w_gate': 2, 'ffn2_w_up': 2, 'ffn2_w_down': 1}
SHARDED_ORDER = [n for n in WEIGHTS if n in SHARDED]
SMALL_ORDER = [n for n in WEIGHTS if n not in SHARDED or n == 'ssd_conv_w']


def _pcall(body, **kw):
    return pl.pallas_call(body, **kw)


def _scall(body, *, name, out_type, scratch_types, collective_id):
    return pl.kernel(body, out_type=out_type, mesh=plsc.ScalarSubcoreMesh(axis_name="sequencer", num_cores=1),
                     scratch_types=scratch_types, name=name,
                     compiler_params=pltpu.CompilerParams(collective_id=collective_id))


def _params(*sem):
    return pltpu.CompilerParams(dimension_semantics=sem, vmem_limit_bytes=VMEM_LIMIT_BYTES)


def _tile(n, pref, align=128):
    if n <= pref:
        return n
    t = (pref // align) * align
    while t >= align:
        if n % t == 0:
            return t
        t -= align
    return n


def _rms(x, g):
    return x * lax.rsqrt(jnp.mean(x * x, axis=-1, keepdims=True) + RMS_EPS) * g


def _sigmoid(x):
    return 1.0 / (1.0 + jnp.exp(-x))


def _silu(x):
    return x * _sigmoid(x)


def _gelu(x):
    return 0.5 * x * (1.0 + jnp.tanh(math.sqrt(2.0 / math.pi) * (x + 0.044715 * (x * x * x))))


def _softplus(x):
    return jnp.maximum(x, 0.0) + jnp.log(1.0 + jnp.exp(-jnp.abs(x)))


def _dot(a, b, dims):
    return lax.dot_general(a.astype(BF16), b.astype(BF16), (dims, ((), ())), preferred_element_type=F32)


_NN = ((1,), (0,))
_NT = ((1,), (1,))
_TN = ((0,), (0,))


@jax.custom_vjp
def _bdot_nn(a, b):
    return _dot(a, b, _NN)


_bdot_nn.defvjp(lambda a, b: (_dot(a, b, _NN), (a, b)),
                lambda r, g: (_dot(g, r[1], _NT), _dot(r[0], g, _TN)))


@jax.custom_vjp
def _bdot_nt(a, b):
    return _dot(a, b, _NT)


_bdot_nt.defvjp(lambda a, b: (_dot(a, b, _NT), (a, b)),
                lambda r, g: (_dot(g, r[1], _NN), _dot(g, r[0], _TN)))


@jax.custom_vjp
def _bdot_tn(a, b):
    return _dot(a, b, _TN)


_bdot_tn.defvjp(lambda a, b: (_dot(a, b, _TN), (a, b)),
                lambda r, g: (_dot(r[1], g, _NT), _dot(r[0], g, _NN)))


def _fdot(a, b, dims=_NN):
    return lax.dot_general(a, b, (dims, ((), ())), precision=HIGHEST, preferred_element_type=F32)


def _sel3(x, sel, dims, x_first):
    p1 = x.astype(BF16)
    r1 = x - p1.astype(F32)
    p2 = r1.astype(BF16)
    p3 = (r1 - p2.astype(F32)).astype(BF16)
    sel = sel.astype(BF16)
    out = None
    for piece in (p1, p2, p3):
        d = lax.dot_general(*((piece, sel) if x_first else (sel, piece)), (dims, ((), ())), preferred_element_type=F32)
        out = d if out is None else out + d
    return out


@jax.custom_vjp
def _sel_right(x, sel):
    return _sel3(x, sel, _NN, True)


_sel_right.defvjp(lambda x, sel: (_sel3(x, sel, _NN, True), sel),
                  lambda sel, g: (_sel3(g, sel, _NT, True), jnp.zeros_like(sel)))


@jax.custom_vjp
def _sel_left(sel, x):
    return _sel3(x, sel, _NN, False)


_sel_left.defvjp(lambda sel, x: (_sel3(x, sel, _NN, False), sel),
                 lambda sel, g: (jnp.zeros_like(sel), _sel3(g, sel, _TN, False)))


@jax.custom_vjp
def _sel_left_nt(sel, x):
    return _sel3(x, sel, _NT, False)


_sel_left_nt.defvjp(lambda sel, x: (_sel3(x, sel, _NT, False), sel),
                    lambda sel, g: (jnp.zeros_like(sel), _sel3(g, sel, _TN, True)))


def _mm(a, b, *, name, ta=False, tb=False, out_dtype=F32, tm=2048, tn=512, tk=2048, col_shards=False):
    m, k = (a.shape[1], a.shape[0]) if ta else a.shape
    n = b.shape[0] if tb else b.shape[1]
    assert k == (b.shape[1] if tb else b.shape[0]), (a.shape, b.shape, ta, tb)
    if col_shards:
        tn = n // N_DEV
    tm, tn, tk = _tile(m, tm), _tile(n, tn), _tile(k, tk)
    nk = k // tk
    a_spec = pl.BlockSpec((tk, tm), lambda i, j, kk: (kk, i)) if ta else pl.BlockSpec((tm, tk), lambda i, j, kk: (i, kk))
    b_spec = pl.BlockSpec((tn, tk), lambda i, j, kk: (j, kk)) if tb else pl.BlockSpec((tk, tn), lambda i, j, kk: (kk, j))
    dims = ((0 if ta else 1,), (1 if tb else 0,))
    out_spec = pl.BlockSpec((tm, tn), lambda i, j, kk: (i, j))
    out_shape = jax.ShapeDtypeStruct((m, n), out_dtype)
    if col_shards:
        out_shape = jax.ShapeDtypeStruct((N_DEV, 1, m, n // N_DEV), out_dtype)
        out_spec = pl.BlockSpec((None, None, tm, tn), lambda i, j, kk: (j, 0, i, 0))

    def body(a_ref, b_ref, o_ref, acc_ref):
        kk = pl.program_id(2)

        @pl.when(kk == 0)
        def _():
            acc_ref[...] = jnp.zeros_like(acc_ref)

        acc_ref[...] += _dot(a_ref[...], b_ref[...], dims)

        @pl.when(kk == nk - 1)
        def _():
            o_ref[...] = acc_ref[...].astype(o_ref.dtype)

    return _pcall(
        body, name=name, grid=(m // tm, n // tn, nk),
        in_specs=[a_spec, b_spec], out_specs=out_spec, out_shape=out_shape,
        scratch_shapes=[pltpu.VMEM((tm, tn), F32)],
        compiler_params=_params("parallel", "parallel", "arbitrary"),
    )(a, b)


def _rspec(tm, w, cb=0, percol=False):
    return pl.BlockSpec((tm, w), (lambda j, i: (i, cb + j)) if percol else (lambda j, i: (i, cb)))


def _bspec(w, cb=0, percol=False, rows=1):
    return pl.BlockSpec((rows, w), (lambda j, i: (0, cb + j)) if percol else (lambda j, i: (0, cb)))


def _rows(fn, *, name, nrow, ncol=1, ins, outs):
    n_in = len(ins)
    accs = [o[2] for o in outs]

    def body(*refs):
        vals = fn(*[r[...] for r in refs[:n_in]])
        if not isinstance(vals, (tuple, list)):
            vals = (vals,)
        i = pl.program_id(1)
        for ref, val, acc in zip(refs[n_in:], vals, accs):
            if acc:
                @pl.when(i == 0)
                def _(ref=ref):
                    ref[...] = jnp.zeros_like(ref)

                ref[...] += jnp.broadcast_to(val, ref.shape).astype(ref.dtype)
            else:
                ref[...] = val.astype(ref.dtype)

    res = _pcall(
        body, name=name, grid=(ncol, nrow),
        in_specs=[s for _, s in ins], out_specs=[o[1] for o in outs], out_shape=[o[0] for o in outs],
        compiler_params=_params("parallel", "arbitrary"),
    )(*[a for a, _ in ins])
    return res


def _sds(shape, dtype=F32):
    return jax.ShapeDtypeStruct(shape, dtype)


def _rms_fwd(h, g, *, name, tm):
    t, d = h.shape
    return _rows(lambda x, gg: _rms(x, gg), name=name, nrow=t // tm,
                 ins=[(h, _rspec(tm, d)), (g, _bspec(d))],
                 outs=[(_sds((t, d), BF16), _rspec(tm, d), False)])[0]


def _resid_fwd(h, f, g, scale, *, name, tm):
    t, d = h.shape
    return _rows(lambda x, ff, gg: x + scale * _rms(ff, gg), name=name, nrow=t // tm,
                 ins=[(h, _rspec(tm, d)), (f, _rspec(tm, d)), (g, _bspec(d))],
                 outs=[(_sds((t, d)), _rspec(tm, d), False)])[0]


def _resid_bwd(f, g, dh, scale, *, name, tm):
    t, d = f.shape

    def fn(ff, gg, dd):
        _, vjp = jax.vjp(lambda a, b: scale * _rms(a, b), ff, gg)
        return vjp(dd)

    return _rows(fn, name=name, nrow=t // tm,
                 ins=[(f, _rspec(tm, d)), (g, _bspec(d)), (dh, _rspec(tm, d))],
                 outs=[(_sds((t, d), BF16), _rspec(tm, d), False), (_sds((1, d)), _bspec(d), True)])


def _rms_bwd(h, g, dh, dxns, *, name, tm):
    t, d = h.shape

    def fn(x, gg, dd, *dx):
        _, vjp = jax.vjp(_rms, x, gg)
        tot = dx[0]
        for more in dx[1:]:
            tot = tot + more
        dxx, dg = vjp(tot)
        return dd + dxx, dg

    return _rows(fn, name=name, nrow=t // tm,
                 ins=[(h, _rspec(tm, d)), (g, _bspec(d)), (dh, _rspec(tm, d))] + [(x, _rspec(tm, d)) for x in dxns],
                 outs=[(_sds((t, d)), _rspec(tm, d), False), (_sds((1, d)), _bspec(d), True)])


FFN_BLOCKS = 4
NB = FFN_HIDDEN // FFN_BLOCKS
MM_ROWS = 2048


def _ffn_up(xn, wg, wu, *, name):
    t = xn.shape[0]
    tm = _tile(t, MM_ROWS // 2)
    wspec = pl.BlockSpec((None, None, NB, D_MODEL), lambda i, j: (j, 0, 0, 0))

    def body(x_ref, g_ref, u_ref, ab_ref, hh_ref):
        x = x_ref[...]
        a, b = _dot(x, g_ref[...], _NT), _dot(x, u_ref[...], _NT)
        ab_ref[0] = a.astype(ab_ref.dtype)
        ab_ref[1] = b.astype(ab_ref.dtype)
        hh_ref[...] = (_silu(a) * b).astype(hh_ref.dtype)

    return _pcall(
        body, name=name, grid=(t // tm, FFN_BLOCKS),
        in_specs=[pl.BlockSpec((tm, D_MODEL), lambda i, j: (i, 0)), wspec, wspec],
        out_specs=[pl.BlockSpec((None, 2, tm, NB), lambda i, j: (j, 0, i, 0)),
                   pl.BlockSpec((None, tm, NB), lambda i, j: (j, i, 0))],
        out_shape=[_sds((FFN_BLOCKS, 2, t, NB), BF16), _sds((FFN_BLOCKS, t, NB), BF16)],
        compiler_params=_params("parallel", "parallel"),
    )(xn, wg, wu)


def _ffn_down(hh, wd, *, name):
    t = hh.shape[1]
    tm = _tile(t, 512)

    def body(h_ref, w_ref, o_ref):
        acc = _dot(h_ref[0], w_ref[0, 0], _NN)
        for k in range(1, FFN_BLOCKS):
            acc = acc + _dot(h_ref[k], w_ref[k, 0], _NN)
        o_ref[...] = acc

    return _pcall(
        body, name=name, grid=(t // tm,),
        in_specs=[pl.BlockSpec((FFN_BLOCKS, tm, NB), lambda i: (0, i, 0)),
                  pl.BlockSpec((FFN_BLOCKS, 1, NB, D_MODEL), lambda i: (0, 0, 0, 0))],
        out_specs=pl.BlockSpec((tm, D_MODEL), lambda i: (i, 0)), out_shape=_sds((t, D_MODEL)),
        compiler_params=_params("parallel"),
    )(hh, wd)


def _ffn_down_dx(df, wd, ab, *, name):
    t = df.shape[0]
    tm = _tile(t, MM_ROWS // 2)

    def body(d_ref, w_ref, ab_ref, o_ref):
        dhh = _dot(d_ref[...], w_ref[...], _NT)
        _, vjp = jax.vjp(lambda a, b: _silu(a) * b, ab_ref[0].astype(F32), ab_ref[1].astype(F32))
        da, db = vjp(dhh)
        o_ref[0] = da.astype(o_ref.dtype)
        o_ref[1] = db.astype(o_ref.dtype)

    blk = pl.BlockSpec((None, 2, tm, NB), lambda i, j: (j, 0, i, 0))
    return _pcall(
        body, name=name, grid=(t // tm, FFN_BLOCKS),
        in_specs=[pl.BlockSpec((tm, D_MODEL), lambda i, j: (i, 0)),
                  pl.BlockSpec((None, None, NB, D_MODEL), lambda i, j: (j, 0, 0, 0)), blk],
        out_specs=blk, out_shape=_sds((FFN_BLOCKS, 2, t, NB), BF16), compiler_params=_params("parallel", "parallel"),
    )(df, wd, ab)


def _ffn_down_dw(hh, df, *, name, tn=512):
    t = df.shape[0]
    tk = _tile(t, 2048)
    nk = t // tk

    def body(h_ref, d_ref, o_ref, acc_ref):
        kk = pl.program_id(2)

        @pl.when(kk == 0)
        def _():
            acc_ref[...] = jnp.zeros_like(acc_ref)

        acc_ref[...] += _dot(h_ref[...], d_ref[...], _TN)

        @pl.when(kk == nk - 1)
        def _():
            o_ref[...] = acc_ref[...].astype(o_ref.dtype)

    return _pcall(
        body, name=name, grid=(FFN_BLOCKS, D_MODEL // tn, nk),
        in_specs=[pl.BlockSpec((None, tk, NB), lambda j, n, kk: (j, kk, 0)),
                  pl.BlockSpec((tk, tn), lambda j, n, kk: (kk, n))],
        out_specs=pl.BlockSpec((None, None, NB, tn), lambda j, n, kk: (j, 0, 0, n)),
        out_shape=_sds((FFN_BLOCKS, 1, NB, D_MODEL), BF16),
        scratch_shapes=[pltpu.VMEM((NB, tn), F32)],
        compiler_params=_params("parallel", "parallel", "arbitrary"),
    )(hh, df)


def _ffn_up_dx(dab, wg, wu, *, name):
    t = dab.shape[2]
    tm = _tile(t, MM_ROWS // 2)
    wspec = pl.BlockSpec((None, None, NB, D_MODEL), lambda i, j: (j, 0, 0, 0))

    def body(d_ref, g_ref, u_ref, o_ref):
        @pl.when(pl.program_id(1) == 0)
        def _():
            o_ref[...] = jnp.zeros_like(o_ref)

        o_ref[...] += _dot(d_ref[0], g_ref[...], _NN) + _dot(d_ref[1], u_ref[...], _NN)

    return _pcall(
        body, name=name, grid=(t // tm, FFN_BLOCKS),
        in_specs=[pl.BlockSpec((None, 2, tm, NB), lambda i, j: (j, 0, i, 0)), wspec, wspec],
        out_specs=pl.BlockSpec((tm, D_MODEL), lambda i, j: (i, 0)), out_shape=_sds((t, D_MODEL)),
        compiler_params=_params("parallel", "arbitrary"),
    )(dab, wg, wu)


def _ffn_up_dw(xn, dab, *, name):
    t = xn.shape[0]

    def body(x_ref, d_ref, og_ref, ou_ref):
        x = x_ref[...]
        og_ref[...] = _dot(d_ref[0], x, _TN).astype(og_ref.dtype)
        ou_ref[...] = _dot(d_ref[1], x, _TN).astype(ou_ref.dtype)

    out = pl.BlockSpec((None, None, NB, D_MODEL), lambda j: (j, 0, 0, 0))
    return _pcall(
        body, name=name, grid=(FFN_BLOCKS,),
        in_specs=[pl.BlockSpec((t, D_MODEL), lambda j: (0, 0)), pl.BlockSpec((None, 2, t, NB), lambda j: (j, 0, 0, 0))],
        out_specs=[out, out], out_shape=[_sds((FFN_BLOCKS, 1, NB, D_MODEL), BF16)] * 2,
        compiler_params=_params("parallel"),
    )(xn, dab)


def _paired(a):
    return a.reshape(FFN_BLOCKS, 1, NB, D_MODEL)


def _ffn_fwd(h, p, tag, tm):
    xn = _rms_fwd(h, p['pre_g'], name=f"{tag}_rms", tm=tm)
    ab, hh = _ffn_up(xn, _paired(p['w_gate']), _paired(p['w_up']), name=f"{tag}_up")
    w_down, hh = lax.optimization_barrier((p['w_down'], hh))
    f = _ffn_down(hh, _paired(w_down), name=f"{tag}_down")
    out = _resid_fwd(h, f, p['post_g'], 0.5, name=f"{tag}_res", tm=tm)
    return out, (h, xn, ab, hh, f)


def _ffn_bwd(dh, p, saved, tag, tm, after_first):
    h, xn, ab, hh, f = saved
    df, dpost = _resid_bwd(f, p['post_g'], dh, 0.5, name=f"{tag}_res_bwd", tm=tm)
    df = after_first(df)
    dab = _ffn_down_dx(df, _paired(p['w_down']), ab, name=f"{tag}_down_dx")
    bufs = {f'{tag}_w_down': _ffn_down_dw(hh, df, name=f"{tag}_down_dw")}
    dxn = _ffn_up_dx(dab, _paired(p['w_gate']), _paired(p['w_up']), name=f"{tag}_up_dx")
    bufs[f'{tag}_w_gate'], bufs[f'{tag}_w_up'] = _ffn_up_dw(xn, dab, name=f"{tag}_up_dw")
    bufs = {n: a.reshape(N_DEV, 1, FFN_HIDDEN // N_DEV, D_MODEL) for n, a in bufs.items()}
    dh_in, dpre = _rms_bwd(h, p['pre_g'], dh, [dxn], name=f"{tag}_rms_bwd", tm=tm)
    return dh_in, bufs, {f'{tag}_pre_g': dpre, f'{tag}_post_g': dpost}


CONV_COLS = 256


def _shift_down(x, s):
    rows = lax.broadcasted_iota(jnp.int32, x.shape, 0)
    return jnp.where(rows >= s, pltpu.roll(x, s, axis=0), 0.0)


def _shift_up(x, s):
    t = x.shape[0]
    rows = lax.broadcasted_iota(jnp.int32, x.shape, 0)
    return jnp.where(rows < t - s, pltpu.roll(x, t - s, axis=0), 0.0)


def _conv_fwd(proj, col0, w, b, *, name):
    t = proj.shape[0]
    c = w.shape[1]
    cb0 = col0 // CONV_COLS

    def body(x_ref, w_ref, b_ref, o_ref):
        x = x_ref[...]
        acc = x * w_ref[3:4, :] + b_ref[...]
        for k in range(SSD_CONV_WIDTH - 1):
            acc = acc + _shift_down(x, SSD_CONV_WIDTH - 1 - k) * w_ref[k:k + 1, :]
        o_ref[...] = _silu(acc)

    return _pcall(
        body, name=name, grid=(c // CONV_COLS,),
        in_specs=[pl.BlockSpec((t, CONV_COLS), lambda j: (0, cb0 + j)),
                  pl.BlockSpec((SSD_CONV_WIDTH, CONV_COLS), lambda j: (0, j)),
                  pl.BlockSpec((1, CONV_COLS), lambda j: (0, j))],
        out_specs=pl.BlockSpec((t, CONV_COLS), lambda j: (0, j)),
        out_shape=_sds((t, c)), compiler_params=_params("parallel"),
    )(proj, w, b)


def _conv_bwd(proj, col0, w, b, douts, *, name):
    t = proj.shape[0]
    c = w.shape[1]
    cb0 = col0 // CONV_COLS
    first = np.cumsum([0] + [d.shape[1] // CONV_COLS for d in douts])

    def body(x_ref, w_ref, b_ref, *rest):
        d_refs, (dx_ref, dw_ref, db_ref) = rest[:len(douts)], rest[len(douts):]
        j = pl.program_id(0)
        dout = d_refs[-1][...]
        for k in range(len(douts) - 2, -1, -1):
            dout = jnp.where(j < int(first[k + 1]), d_refs[k][...], dout)
        x = x_ref[...]
        shifted = [_shift_down(x, SSD_CONV_WIDTH - 1 - k) for k in range(SSD_CONV_WIDTH - 1)] + [x]
        pre = b_ref[...] + shifted[3] * w_ref[3:4, :]
        for k in range(SSD_CONV_WIDTH - 1):
            pre = pre + shifted[k] * w_ref[k:k + 1, :]
        sg = _sigmoid(pre)
        dpre = dout * (sg * (1.0 + pre * (1.0 - sg)))
        dx = dpre * w_ref[3:4, :]
        for k in range(SSD_CONV_WIDTH - 1):
            dx = dx + _shift_up(dpre, SSD_CONV_WIDTH - 1 - k) * w_ref[k:k + 1, :]
        dx_ref[...] = dx.astype(dx_ref.dtype)
        for k in range(SSD_CONV_WIDTH):
            dw_ref[k:k + 1, :] = jnp.sum(dpre * shifted[k], axis=0, keepdims=True)
        db_ref[...] = jnp.sum(dpre, axis=0, keepdims=True)

    return _pcall(
        body, name=name, grid=(c // CONV_COLS,),
        in_specs=[pl.BlockSpec((t, CONV_COLS), lambda j: (0, cb0 + j)),
                  pl.BlockSpec((SSD_CONV_WIDTH, CONV_COLS), lambda j: (0, j)),
                  pl.BlockSpec((1, CONV_COLS), lambda j: (0, j))] +
                 [pl.BlockSpec((t, CONV_COLS), lambda j, lo=int(first[k]), hi=int(first[k + 1]): (0, jnp.clip(j, lo, hi - 1) - lo))
                  for k in range(len(douts))],
        out_specs=[pl.BlockSpec((t, CONV_COLS), lambda j: (0, j)),
                   pl.BlockSpec((SSD_CONV_WIDTH, CONV_COLS), lambda j: (0, j)),
                   pl.BlockSpec((1, CONV_COLS), lambda j: (0, j))],
        out_shape=[_sds((t, c), BF16), _sds((SSD_CONV_WIDTH, c)), _sds((1, c))],
        compiler_params=_params("arbitrary"),
    )(proj, w, b, *douts)


HALF = 256
HEADS_PER_HALF = 4
PAD_HEADS = 128


def _head_expanders():
    k = lax.broadcasted_iota(jnp.int32, (PAD_HEADS, HALF), 0)
    j = lax.broadcasted_iota(jnp.int32, (PAD_HEADS, HALF), 1)
    kt = lax.broadcasted_iota(jnp.int32, (HALF, PAD_HEADS), 1)
    jt = lax.broadcasted_iota(jnp.int32, (HALF, PAD_HEADS), 0)
    es, ets = [], []
    for half in range(2):
        es.append(jnp.where(k == j // SSD_HEAD_DIM + half * HEADS_PER_HALF, 1.0, 0.0).astype(F32))
        ets.append(jnp.where(kt == jt // SSD_HEAD_DIM + half * HEADS_PER_HALF, 1.0, 0.0).astype(F32))
    return es, ets


def _ssd_chunk(x_lo, x_hi, bm, cm, dtr, dtb8, alog8, dsk8, s_lo, s_hi):
    q = x_lo.shape[0]
    es, ets = _head_expanders()
    rowmean = lambda v: jnp.sum(v, axis=0, keepdims=True) * 0.125
    dt = _softplus(dtr + rowmean(dtb8))
    a = -jnp.exp(rowmean(alog8))
    adt = a * dt
    adt_tot8 = jnp.broadcast_to(jnp.sum(adt, axis=0, keepdims=True), (8, PAD_HEADS))
    ll = lax.broadcasted_iota(jnp.int32, (q, q), 0)
    ss = lax.broadcasted_iota(jnp.int32, (q, q), 1)
    ltri = jnp.where(ll >= ss, 1.0, 0.0).astype(F32)
    lane = lax.broadcasted_iota(jnp.int32, (1, HALF), 1)
    cb = _bdot_nt(cm, bm)
    outs = []
    for half, (x, s_in) in enumerate(((x_lo, s_lo), (x_hi, s_hi))):
        e, et = es[half], ets[half]
        dtf = _sel_right(dt, e)
        af = rowmean(_sel_right(jnp.broadcast_to(a, (8, PAD_HEADS)), e)) * dtf
        dskf = rowmean(_sel_right(dsk8, e))
        acum = _sel_left(ltri, af)
        alast = jnp.sum(af, axis=0, keepdims=True)
        xdt = x * dtf
        ydiag = jnp.zeros((q, HALF), F32)
        for r in range(HEADS_PER_HALF):
            sel = lane == r * SSD_HEAD_DIM
            ac_r = jnp.sum(jnp.where(sel, acum, 0.0), axis=1, keepdims=True)
            a_r = jnp.sum(jnp.where(sel, af, 0.0), axis=1, keepdims=True)
            arow = jnp.sum(jnp.where(ll <= ss, a_r, 0.0), axis=0, keepdims=True)
            decay = jnp.exp(jnp.where(ll >= ss, ac_r - arow, -jnp.inf))
            yh = _bdot_nn(cb * decay, xdt)
            ydiag = ydiag + jnp.where(lane // SSD_HEAD_DIM == r, yh, 0.0)
        st = _bdot_tn(xdt * jnp.exp(alast - acum), bm)
        yoff = _bdot_nt(cm, s_in) * jnp.exp(acum)
        y = ydiag + yoff + dskf * x
        alast_col = jnp.sum(_sel_left_nt(et, adt_tot8), axis=1, keepdims=True) * 0.125
        outs.append((y, jnp.exp(alast_col) * s_in + st))
    return outs[0][0], outs[1][0], outs[0][1], outs[1][1]


SSD_GP = 4


def _ssd_specs(t, rev):
    q, n = SSD_CHUNK, SSD_GP
    nc = t // q
    ci = (lambda c: nc - 1 - c) if rev else (lambda c: c)
    bcol0 = SSD_D_INNER // (n * SSD_STATE)
    return dict(
        x=pl.BlockSpec((q, n * 2 * HALF), lambda g, c: (ci(c), g)),
        bm=pl.BlockSpec((q, n * SSD_STATE), lambda g, c: (ci(c), bcol0 + g)),
        cm=pl.BlockSpec((q, n * SSD_STATE), lambda g, c: (ci(c), bcol0 + SSD_GROUPS // n + g)),
        dt=pl.BlockSpec((n, q, PAD_HEADS), lambda g, c: (g, ci(c), 0)),
        par=pl.BlockSpec((n, 8, PAD_HEADS), lambda g, c: (g, 0, 0)),
        st=pl.BlockSpec((None, n, 2, HALF, SSD_STATE), lambda g, c: (ci(c), g, 0, 0, 0)),
        grp=pl.BlockSpec((q, n * SSD_STATE), lambda g, c: (ci(c), g)),
    )


def _group_cols(k):
    lo = k * 2 * HALF
    return slice(lo, lo + HALF), slice(lo + HALF, lo + 2 * HALF), slice(k * SSD_STATE, (k + 1) * SSD_STATE)


def _ssd_fwd(xc, dt4, dtb, alog, dsk, *, name):
    t = xc.shape[0]
    nc = t // SSD_CHUNK
    sp = _ssd_specs(t, False)

    def body(x, bm, cm, dt, p0, p1, p2, y_ref, sin_ref, st_ref):
        @pl.when(pl.program_id(1) == 0)
        def _():
            st_ref[...] = jnp.zeros_like(st_ref)

        sin_ref[...] = st_ref[...]
        for k in range(SSD_GP):
            lo, hi, bc = _group_cols(k)
            y_lo, y_hi, so_lo, so_hi = _ssd_chunk(x[:, lo], x[:, hi], bm[:, bc], cm[:, bc], dt[k], p0[k], p1[k], p2[k],
                                                  st_ref[k, 0], st_ref[k, 1])
            y_ref[:, lo] = y_lo
            y_ref[:, hi] = y_hi
            st_ref[k, 0] = so_lo
            st_ref[k, 1] = so_hi

    return _pcall(
        body, name=name, grid=(SSD_GROUPS // SSD_GP, nc),
        in_specs=[sp['x'], sp['bm'], sp['cm'], sp['dt'], sp['par'], sp['par'], sp['par']],
        out_specs=[sp['x'], sp['st']],
        out_shape=[_sds((t, SSD_D_INNER)), _sds((nc, SSD_GROUPS, 2, HALF, SSD_STATE))],
        scratch_shapes=[pltpu.VMEM((SSD_GP, 2, HALF, SSD_STATE), F32)],
        compiler_params=_params("parallel", "arbitrary"),
    )(xc, xc, xc, dt4, dtb, alog, dsk)


def _ssd_bwd(xc, dt4, dtb, alog, dsk, sin, dy, *, name):
    t = xc.shape[0]
    nc = t // SSD_CHUNK
    sp = _ssd_specs(t, True)

    def body(x, bm, cm, dt, p0, p1, p2, sin_ref, dy_ref, dx_ref, db_ref, dc_ref, ddt_ref, dp0, dp1, dp2, dst_ref):
        @pl.when(pl.program_id(1) == 0)
        def _():
            dst_ref[...] = jnp.zeros_like(dst_ref)
            for ref in (dp0, dp1, dp2):
                ref[...] = jnp.zeros_like(ref)

        for k in range(SSD_GP):
            lo, hi, bc = _group_cols(k)
            _, vjp = jax.vjp(_ssd_chunk, x[:, lo], x[:, hi], bm[:, bc], cm[:, bc], dt[k], p0[k], p1[k], p2[k],
                             sin_ref[k, 0], sin_ref[k, 1])
            dxl, dxh, dbm, dcm, ddt, g0, g1, g2, ds_lo, ds_hi = vjp(
                (dy_ref[:, lo], dy_ref[:, hi], dst_ref[k, 0], dst_ref[k, 1]))
            dx_ref[:, lo] = dxl
            dx_ref[:, hi] = dxh
            db_ref[:, bc] = dbm
            dc_ref[:, bc] = dcm
            ddt_ref[k] = ddt
            dst_ref[k, 0] = ds_lo
            dst_ref[k, 1] = ds_hi
            for ref, g in ((dp0, g0), (dp1, g1), (dp2, g2)):
                ref[k] += jnp.broadcast_to(jnp.sum(g, axis=0, keepdims=True), g.shape)

    return _pcall(
        body, name=name, grid=(SSD_GROUPS // SSD_GP, nc),
        in_specs=[sp['x'], sp['bm'], sp['cm'], sp['dt'], sp['par'], sp['par'], sp['par'], sp['st'], sp['x']],
        out_specs=[sp['x'], sp['grp'], sp['grp'], sp['dt'], sp['par'], sp['par'], sp['par']],
        out_shape=[_sds((t, SSD_D_INNER)), _sds((t, SSD_GROUPS * SSD_STATE)), _sds((t, SSD_GROUPS * SSD_STATE)),
                   _sds((SSD_GROUPS, t, PAD_HEADS))] + [_sds((SSD_GROUPS, 8, PAD_HEADS))] * 3,
        scratch_shapes=[pltpu.VMEM((SSD_GP, 2, HALF, SSD_STATE), F32)],
        compiler_params=_params("parallel", "arbitrary"),
    )(xc, xc, xc, dt4, dtb, alog, dsk, sin, dy)


def _gatenorm(y, z, g):
    v = y * _silu(z)
    return v * lax.rsqrt(jnp.mean(v * v, axis=-1, keepdims=True) + RMS_EPS) * g


S5_CH = S5_WIDTH // S5_BLOCKS
S5_ST = S5_CH * S5_STATE // S5_GROUP
SCAN_UNROLL = 16


def _cmul(ar, ai, br, bi):
    return ar * br - ai * bi, ar * bi + ai * br


def _segment_power(ar, ai, n):
    assert n & (n - 1) == 0
    for _ in range(n.bit_length() - 1):
        ar, ai = _cmul(ar, ai, ar, ai)
    return ar, ai


def _carry_in(fr, fi, pr, pi, reverse):
    rows = lax.broadcasted_iota(jnp.int32, fr.shape, 0)
    cr = jnp.zeros_like(fr[0:1])
    ci = jnp.zeros_like(cr)
    outr = jnp.zeros_like(fr)
    outi = jnp.zeros_like(fr)
    order = range(6, -1, -1) if reverse else range(1, 8)
    for j in order:
        src = j + 1 if reverse else j - 1
        nr, ni = _cmul(pr[0:1], pi[0:1], cr, ci)
        cr, ci = nr + fr[src:src + 1], ni + fi[src:src + 1]
        outr = jnp.where(rows == j, cr, outr)
        outi = jnp.where(rows == j, ci, outi)
    return outr, outi


def _s5_specs(t):
    return dict(ch=pl.BlockSpec((t, S5_CH), lambda j: (0, j)), st=pl.BlockSpec((t, S5_ST), lambda j: (0, j)),
                lam=pl.BlockSpec((1, S5_ST), lambda j: (0, j)),
                b=pl.BlockSpec((None, S5_CH, S5_ST), lambda j: (j, 0, 0)),
                c=pl.BlockSpec((None, S5_ST, S5_CH), lambda j: (j, 0, 0)))


def _s5_fwd(u5, bre, bim, cre, cim, lr, li, skip, *, name):
    t = u5.shape[0]
    nrt = t // 8

    def body(u_ref, bre_ref, bim_ref, cre_ref, cim_ref, lr_ref, li_ref, d_ref, sr_ref, si_ref, y_ref, g_ref,
             br_ref, bi_ref):
        u = u_ref[...]
        br_ref[...] = _dot(u, bre_ref[...], _NN)
        bi_ref[...] = _dot(u, bim_ref[...], _NN)
        ar = jnp.broadcast_to(lr_ref[...], (8, S5_ST))
        ai = jnp.broadcast_to(li_ref[...], (8, S5_ST))

        def step(r, s, store):
            rows = pl.ds(pl.multiple_of(r * 8, 8), 8)
            nr, ni = _cmul(ar, ai, s[0], s[1])
            nr, ni = nr + br_ref[rows, :], ni + bi_ref[rows, :]
            if store:
                sr_ref[rows, :] = nr
                si_ref[rows, :] = ni
            return nr, ni

        zero = (jnp.zeros((8, S5_ST), F32), jnp.zeros((8, S5_ST), F32))
        fr, fi = lax.fori_loop(0, nrt, lambda r, s: step(r, s, False), zero, unroll=SCAN_UNROLL)
        pr, pi = _segment_power(ar, ai, nrt)
        init = _carry_in(fr, fi, pr, pi, False)
        lax.fori_loop(0, nrt, lambda r, s: step(r, s, True), init, unroll=SCAN_UNROLL)
        y = _dot(sr_ref[...], cre_ref[...], _NN) - _dot(si_ref[...], cim_ref[...], _NN)
        y_ref[...] = y
        g_ref[...] = _gelu(y + d_ref[...] * u).astype(g_ref.dtype)

    sp = _s5_specs(t)
    w = S5_BLOCKS * S5_ST
    return _pcall(
        body, name=name, grid=(S5_BLOCKS,),
        in_specs=[sp['ch'], sp['b'], sp['b'], sp['c'], sp['c'], sp['lam'], sp['lam'],
                  pl.BlockSpec((1, S5_CH), lambda j: (0, j))],
        out_specs=[sp['st'], sp['st'], sp['ch'], sp['ch']],
        out_shape=[_sds((t, w)), _sds((t, w)), _sds((t, S5_WIDTH)), _sds((t, S5_WIDTH), BF16)],
        scratch_shapes=[pltpu.VMEM((t, S5_ST), F32)] * 2, compiler_params=_params("parallel"),
    )(u5, bre, bim, cre, cim, lr, li, skip)


def _s5_bwd(dy, du_direct, u5, sr, si, bre, bim, cre, cim, lr, li, *, name):
    t = u5.shape[0]
    nrt = t // 8

    def body(dy_ref, dd_ref, u_ref, sr_ref, si_ref, bre_ref, bim_ref, cre_ref, cim_ref, lr_ref, li_ref,
             du_ref, dbre_ref, dbim_ref, dcre_ref, dcim_ref, dlr_ref, dli_ref, gr_ref, gi_ref):
        dyv = dy_ref[...]
        gr_ref[...] = _dot(dyv, cre_ref[...], _NT)
        gi_ref[...] = -_dot(dyv, cim_ref[...], _NT)
        dcre_ref[...] = _dot(sr_ref[...], dyv, _TN)
        dcim_ref[...] = -_dot(si_ref[...], dyv, _TN)
        dr_ref, di_ref = gr_ref, gi_ref
        ar = jnp.broadcast_to(lr_ref[...], (8, S5_ST))
        ai = -jnp.broadcast_to(li_ref[...], (8, S5_ST))
        zero = jnp.zeros((8, S5_ST), F32)

        def step1(k, g):
            rows = pl.ds(pl.multiple_of((nrt - 1 - k) * 8, 8), 8)
            nr, ni = _cmul(ar, ai, g[0], g[1])
            return nr + dr_ref[rows, :], ni + di_ref[rows, :]

        fr, fi = lax.fori_loop(0, nrt, step1, (zero, zero), unroll=SCAN_UNROLL)
        pr, pi = _segment_power(ar, ai, nrt)
        init = _carry_in(fr, fi, pr, pi, True)

        def step2(k, carry):
            gr, gi, accr, acci = carry
            r = nrt - 1 - k
            rows = pl.ds(pl.multiple_of(r * 8, 8), 8)
            prev = pl.ds(pl.multiple_of(jnp.maximum(r - 1, 0) * 8, 8), 8)
            nr, ni = _cmul(ar, ai, gr, gi)
            nr, ni = nr + dr_ref[rows, :], ni + di_ref[rows, :]
            gr_ref[rows, :] = nr
            gi_ref[rows, :] = ni
            keep = jnp.where(r > 0, 1.0, 0.0)
            pr_, pi_ = sr_ref[prev, :] * keep, si_ref[prev, :] * keep
            return nr, ni, accr + (pr_ * nr + pi_ * ni), acci + (pr_ * ni - pi_ * nr)

        _, _, accr, acci = lax.fori_loop(0, nrt, step2, (init[0], init[1], zero, zero), unroll=SCAN_UNROLL)
        last = pl.ds((nrt - 1) * 8, 8)
        pr_, pi_ = _shift_down(sr_ref[last, :], 1), _shift_down(si_ref[last, :], 1)
        g0r, g0i = gr_ref[0:8, :], gi_ref[0:8, :]
        accr = accr + (pr_ * g0r + pi_ * g0i)
        acci = acci + (pr_ * g0i - pi_ * g0r)
        dlr_ref[...] = jnp.sum(accr, axis=0, keepdims=True)
        dli_ref[...] = jnp.sum(acci, axis=0, keepdims=True)
        u = u_ref[...]
        dbre_ref[...] = _dot(u, gr_ref[...], _TN)
        dbim_ref[...] = _dot(u, gi_ref[...], _TN)
        du = dd_ref[...] + _dot(gr_ref[...], bre_ref[...], _NT) + _dot(gi_ref[...], bim_ref[...], _NT)
        du_ref[...] = du.astype(du_ref.dtype)

    sp = _s5_specs(t)
    w = S5_BLOCKS * S5_ST
    return _pcall(
        body, name=name, grid=(S5_BLOCKS,),
        in_specs=[sp['ch'], sp['ch'], sp['ch'], sp['st'], sp['st'], sp['b'], sp['b'], sp['c'], sp['c'], sp['lam'], sp['lam']],
        out_specs=[sp['ch'], sp['b'], sp['b'], sp['c'], sp['c'], sp['lam'], sp['lam']],
        out_shape=[_sds((t, S5_WIDTH), BF16), _sds((S5_BLOCKS, S5_CH, S5_ST)), _sds((S5_BLOCKS, S5_CH, S5_ST)),
                   _sds((S5_BLOCKS, S5_ST, S5_CH)), _sds((S5_BLOCKS, S5_ST, S5_CH)), _sds((1, w)), _sds((1, w))],
        scratch_shapes=[pltpu.VMEM((t, S5_ST), F32)] * 2, compiler_params=_params("parallel"),
    )(dy, du_direct, u5, sr, si, bre, bim, cre, cim, lr, li)


def _s5_expander():
    n = lax.broadcasted_iota(jnp.int32, (S5_STATE, S5_STATE * S5_GROUP), 0)
    j = lax.broadcasted_iota(jnp.int32, (S5_STATE, S5_STATE * S5_GROUP), 1)
    return jnp.where(n == j // S5_GROUP, 1.0, 0.0).astype(F32)


def _s5_discretise(lam_re, lam_im, log_step, b_re, b_im):
    lr = jnp.minimum(lam_re, S5_MAX_REAL)
    step = jnp.exp(log_step)
    mag = jnp.exp(lr * step)
    ang = lam_im * step
    lbr, lbi = mag * jnp.cos(ang), mag * jnp.sin(ang)
    p, q = lbr - 1.0, lbi
    den = lr * lr + lam_im * lam_im
    cr, ci = (p * lr + q * lam_im) / den, (q * lr - p * lam_im) / den
    e = _s5_expander()
    cre, cie = _fdot(cr, e), _fdot(ci, e)
    return lbr, lbi, cre * b_re - cie * b_im, cre * b_im + cie * b_re


def _s5_params_fwd(lam_re, lam_im, log_step, b_re, b_im, *, name):
    g, n, w = lam_re.shape[0], S5_STATE, S5_STATE * S5_GROUP

    def body(a, b, c, d, e, o0, o1, o2, o3):
        for ref, val in zip((o0, o1, o2, o3), _s5_discretise(a[...], b[...], c[...], d[...], e[...])):
            ref[...] = val

    return _pcall(body, name=name, out_shape=[_sds((g, n)), _sds((g, n)), _sds((g, w)), _sds((g, w))])(
        lam_re, lam_im, log_step, b_re, b_im)


def _s5_params_bwd(lam_re, lam_im, log_step, b_re, b_im, cts, *, name):
    g, n, w = S5_GROUPS, S5_STATE, S5_STATE * S5_GROUP

    def body(a, b, c, d, e, c0, c1, c2, c3, o0, o1, o2, o3, o4):
        _, vjp = jax.vjp(_s5_discretise, a[...], b[...], c[...], d[...], e[...])
        for ref, val in zip((o0, o1, o2, o3, o4), vjp((c0[...], c1[...], c2[...], c3[...]))):
            ref[...] = val

    return _pcall(body, name=name,
                  out_shape=[_sds((g, n)), _sds((g, n)), _sds((g, 1)), _sds((g, w)), _sds((g, w))])(
        lam_re, lam_im, log_step, b_re, b_im, *cts)


def _s5_prepare(w):
    rows = DEPTH * S5_GROUPS
    lbr, lbi, bbr, bbi = _s5_params_fwd(
        w['s5_lambda_re'].reshape(rows, -1), w['s5_lambda_im'].reshape(rows, -1), w['s5_log_step'].reshape(rows, 1),
        w['s5_b_re'].reshape(rows, -1), w['s5_b_im'].reshape(rows, -1), name="s5_par")
    bd = lambda m: _blockdiag(m.reshape(rows, S5_STATE, S5_GROUP).transpose(0, 2, 1), S5_GROUP, S5_STATE).astype(BF16)
    cd = lambda m: _blockdiag(m.reshape(rows, S5_GROUP, S5_STATE).transpose(0, 2, 1), S5_STATE, S5_GROUP).astype(BF16)
    bre, bim, cre, cim = bd(bbr), bd(bbi), cd(w['s5_c_re']), cd(w['s5_c_im'])
    lr, li = lbr.reshape(DEPTH, 1, -1), lbi.reshape(DEPTH, 1, -1)
    blk = lambda a, i: a[i * S5_BLOCKS:(i + 1) * S5_BLOCKS]
    return [dict(bre=blk(bre, i), bim=blk(bim, i), cre=blk(cre, i), cim=blk(cim, i), lr=lr[i], li=li[i])
            for i in range(DEPTH)]


def _perm(a):
    t, c = a.shape
    return a.reshape(8, t // 8, c).transpose(1, 0, 2).reshape(t, c)


def _unperm(a):
    t, c = a.shape
    return a.reshape(t // 8, 8, c).transpose(1, 0, 2).reshape(t, c)


def _blockdiag(m, rows_inner, cols_inner):
    nblk = m.shape[0] // 8
    m = m.reshape(nblk, 8, rows_inner, cols_inner)
    eye = jnp.eye(8, dtype=m.dtype)
    out = m[:, :, :, None, :] * eye[None, :, None, :, None]
    return out.reshape(nblk, 8 * rows_inner, 8 * cols_inner)


def _blockdiag_extract(m, rows_inner, cols_inner):
    m = m.reshape(S5_BLOCKS, 8, rows_inner, 8, cols_inner)
    d = jnp.diagonal(m, axis1=1, axis2=3)
    return d.transpose(0, 3, 1, 2).reshape(S5_GROUPS, rows_inner, cols_inner)


Z0, XBC0, GA0, GB0 = 0, SSD_D_INNER, SSD_D_INNER + SSD_CONV_DIM, SSD_D_INNER + SSD_CONV_DIM + D_MODEL
BIG = GB0 + D_MODEL


def _mixer_fwd(h, p, tm):
    t = h.shape[0]
    nrow = t // tm
    u = _rms_fwd(h, p['pre_g'], name="mix_rms", tm=tm)
    u_p = _perm(u)
    proj = _mm(u, p['w_big'], name="mix_in")
    dtr = _mm(u, p['w_dt'], name="mix_in_dt")
    u5 = _mm(u_p, p['w_u5'], name="mix_in_s5")
    late, proj = lax.optimization_barrier((p['late'], proj))
    by_rows = lambda a: a.reshape(-1, a.shape[-1])
    p = dict(p, w_a=by_rows(late['w_branch_a']), w_b=by_rows(late['w_branch_b']), w_out=by_rows(late['w_out']),
             w_glu=late['s5_w_glu'][:, 0].transpose(1, 0, 2).reshape(late['s5_w_glu'].shape[2], -1))
    xc = _conv_fwd(proj, XBC0, p['conv_w'], p['conv_b'], name="ssd_conv")
    dt4 = jnp.pad(dtr.reshape(t, SSD_GROUPS, 8).transpose(1, 0, 2), ((0, 0), (0, 0), (0, PAD_HEADS - 8)))
    y_ssd, s_in = _ssd_fwd(xc, dt4, p['dt_bias8'], p['a_log8'], p['d8'], name="ssd_scan")
    gw = SSD_D_INNER // SSD_GROUPS
    ya = _rows(_gatenorm, name="ssd_gate", nrow=nrow, ncol=SSD_GROUPS,
               ins=[(y_ssd, _rspec(tm, gw, 0, True)), (proj, _rspec(tm, gw, Z0 // gw, True)),
                    (p['norm_g'], _bspec(gw, 0, True))],
               outs=[(_sds((t, SSD_D_INNER), BF16), _rspec(tm, gw, 0, True), False)])[0]
    y_a = _mm(ya, p['w_a'], name="mix_a")
    bre, bim, cre, cim, lr, li = (p['s5'][k] for k in ('bre', 'bim', 'cre', 'cim', 'lr', 'li'))
    sr, si, y5, y5g = _s5_fwd(u5, bre, bim, cre, cim, lr, li, p['s5_d'], name="s5_scan")
    vg = _mm(y5g, p['w_glu'], name="s5_glu")
    ybin = _rows(lambda a, b: a * _sigmoid(b), name="s5_glu_act", nrow=nrow,
                 ins=[(vg, _rspec(tm, S5_WIDTH, 0)), (vg, _rspec(tm, S5_WIDTH, 1))],
                 outs=[(_sds((t, S5_WIDTH), BF16), _rspec(tm, S5_WIDTH), False)])[0]
    y_b = _unperm(_mm(ybin, p['w_b'], name="mix_b"))
    merged = _rows(lambda ga, gb, a, b: _sigmoid(ga) * a + _sigmoid(gb) * b, name="mix_merge", nrow=nrow,
                   ins=[(proj, _rspec(tm, D_MODEL, GA0 // D_MODEL)), (proj, _rspec(tm, D_MODEL, GB0 // D_MODEL)),
                        (y_a, _rspec(tm, D_MODEL)), (y_b, _rspec(tm, D_MODEL))],
                   outs=[(_sds((t, D_MODEL), BF16), _rspec(tm, D_MODEL), False)])[0]
    m = _mm(merged, p['w_out'], name="mix_out")
    out = _resid_fwd(h, m, p['post_g'], 1.0, name="mix_res", tm=tm)
    saved = dict(w_a=p['w_a'], w_b=p['w_b'], w_out=p['w_out'], w_glu=p['w_glu'], h=h, u=u, u_p=u_p, proj=proj, u5=u5, xc=xc, dt4=dt4, s_in=s_in, y_ssd=y_ssd, ya=ya, y_a=y_a,
                 bre=bre, bim=bim, cre=cre, cim=cim, lr=lr, li=li, sr=sr, si=si, y5=y5, y5g=y5g, vg=vg, ybin=ybin,
                 y_b=y_b, merged=merged, m=m)
    return out, saved


def _mixer_bwd(dh, p, s, tm, after_first):
    t = dh.shape[0]
    nrow = t // tm
    proj = s['proj']
    bufs = {}

    def grad_mm(a, b, wname, axis, name):
        dw = _mm(a, b, ta=True, name=name, out_dtype=BF16, col_shards=axis == 'cols')
        bufs[wname] = dw if axis == 'cols' else dw.reshape(N_DEV, 1, dw.shape[0] // N_DEV, dw.shape[1])

    dm, dpost = _resid_bwd(s['m'], p['post_g'], dh, 1.0, name="mix_res_bwd", tm=tm)
    dm = after_first(dm)
    dmerged = _mm(dm, s['w_out'], tb=True, name="mix_out_dx")
    grad_mm(s['merged'], dm, 'w_out', 'rows', "mix_out_dw")

    def merge_bwd(ga, gb, a, b, d):
        _, vjp = jax.vjp(lambda ga_, gb_, a_, b_: _sigmoid(ga_) * a_ + _sigmoid(gb_) * b_, ga, gb, a, b)
        dga, dgb, da, db = vjp(d)
        return jnp.concatenate([dga, dgb], axis=1), da, db

    dgab, dy_a, dy_b = _rows(
        merge_bwd, name="mix_merge_bwd", nrow=nrow,
        ins=[(proj, _rspec(tm, D_MODEL, GA0 // D_MODEL)), (proj, _rspec(tm, D_MODEL, GB0 // D_MODEL)),
             (s['y_a'], _rspec(tm, D_MODEL)), (s['y_b'], _rspec(tm, D_MODEL)), (dmerged, _rspec(tm, D_MODEL))],
        outs=[(_sds((t, 2 * D_MODEL), BF16), _rspec(tm, 2 * D_MODEL), False),
              (_sds((t, D_MODEL), BF16), _rspec(tm, D_MODEL), False),
              (_sds((t, D_MODEL), BF16), _rspec(tm, D_MODEL), False)])
    dya = _mm(dy_a, s['w_a'], tb=True, name="mix_a_dx")
    grad_mm(s['ya'], dy_a, 'w_branch_a', 'rows', "mix_a_dw")
    gw = SSD_D_INNER // SSD_GROUPS

    def gate_bwd(y, z, g, d):
        _, vjp = jax.vjp(_gatenorm, y, z, g)
        return vjp(d)

    dy_ssd, dz, dnorm = _rows(
        gate_bwd, name="ssd_gate_bwd", nrow=nrow, ncol=SSD_GROUPS,
        ins=[(s['y_ssd'], _rspec(tm, gw, 0, True)), (proj, _rspec(tm, gw, Z0 // gw, True)),
             (p['norm_g'], _bspec(gw, 0, True)), (dya, _rspec(tm, gw, 0, True))],
        outs=[(_sds((t, SSD_D_INNER)), _rspec(tm, gw, 0, True), False),
              (_sds((t, SSD_D_INNER), BF16), _rspec(tm, gw, 0, True), False),
              (_sds((1, SSD_D_INNER)), _bspec(gw, 0, True), True)])
    dxs, dbm, dcm, ddt4, ddtb, dalog, ddsk = _ssd_bwd(s['xc'], s['dt4'], p['dt_bias8'], p['a_log8'], p['d8'],
                                                      s['s_in'], dy_ssd, name="ssd_scan_bwd")
    dxbc, dconv_w, dconv_b = _conv_bwd(proj, XBC0, p['conv_w'], p['conv_b'], (dxs, dbm, dcm), name="ssd_conv_bwd")
    ddtr = ddt4[:, :, :8].transpose(1, 0, 2).reshape(t, SSD_HEADS)
    dy_bp = _perm(dy_b)
    dybin = _mm(dy_bp, s['w_b'], tb=True, name="mix_b_dx")
    grad_mm(s['ybin'], dy_bp, 'w_branch_b', 'rows', "mix_b_dw")

    def glu_bwd(a, b, d):
        _, vjp = jax.vjp(lambda a_, b_: a_ * _sigmoid(b_), a, b)
        da, db = vjp(d)
        return jnp.concatenate([da, db], axis=1)

    dvg = _rows(glu_bwd, name="s5_glu_act_bwd", nrow=nrow,
                ins=[(s['vg'], _rspec(tm, S5_WIDTH, 0)), (s['vg'], _rspec(tm, S5_WIDTH, 1)), (dybin, _rspec(tm, S5_WIDTH))],
                outs=[(_sds((t, 2 * S5_WIDTH), BF16), _rspec(tm, 2 * S5_WIDTH), False)])[0]
    dy5g = _mm(dvg, s['w_glu'], tb=True, name="s5_glu_dx")
    grad_mm(s['y5g'], dvg, 's5_w_glu', 'cols', "s5_glu_dw")

    def act_bwd(a, b, d, g):
        _, vjp = jax.vjp(lambda a_, b_, d_: _gelu(a_ + d_ * b_), a, b, d)
        return vjp(g)

    dy5, du5_direct, ds5d = _rows(
        act_bwd, name="s5_act_bwd", nrow=nrow,
        ins=[(s['y5'], _rspec(tm, S5_WIDTH)), (s['u5'], _rspec(tm, S5_WIDTH)), (p['s5_d'], _bspec(S5_WIDTH)),
             (dy5g, _rspec(tm, S5_WIDTH))],
        outs=[(_sds((t, S5_WIDTH), BF16), _rspec(tm, S5_WIDTH), False), (_sds((t, S5_WIDTH)), _rspec(tm, S5_WIDTH), False),
              (_sds((1, S5_WIDTH)), _bspec(S5_WIDTH), True)])
    du5, dbre, dbim, dcre, dcim, dlr, dli = _s5_bwd(dy5, du5_direct, s['u5'], s['sr'], s['si'], s['bre'], s['bim'],
                                                     s['cre'], s['cim'], s['lr'], s['li'], name="s5_scan_bwd")
    du_p = _mm(du5, p['w_u5'], tb=True, name="mix_in_s5_dx")
    dw_u5 = _mm(s['u_p'], du5, ta=True, name="mix_in_s5_dw", out_dtype=BF16)
    ext_b = lambda m: _blockdiag_extract(m, S5_GROUP, S5_STATE).transpose(0, 2, 1).reshape(S5_GROUPS, S5_STATE * S5_GROUP)
    dlam_re, dlam_im, dlog_step, db_re, db_im = _s5_params_bwd(
        p['lam_re'], p['lam_im'], p['log_step'], p['b_re'], p['b_im'],
        (dlr.reshape(S5_GROUPS, S5_STATE), dli.reshape(S5_GROUPS, S5_STATE), ext_b(dbre), ext_b(dbim)), name="s5_par_bwd")
    dc_re = _blockdiag_extract(dcre, S5_STATE, S5_GROUP).transpose(0, 2, 1)
    dc_im = _blockdiag_extract(dcim, S5_STATE, S5_GROUP).transpose(0, 2, 1)
    dproj = jnp.concatenate([dz, dxbc, dgab], axis=1)
    du_big = _mm(dproj, p['w_big'], tb=True, name="mix_in_dx")
    du_dt = _mm(ddtr, p['w_dt'], tb=True, name="mix_in_dt_dx")
    dw_big = _mm(s['u'], dproj, ta=True, name="mix_in_dw", out_dtype=BF16)
    dw_dt = _mm(s['u'], ddtr, ta=True, name="mix_in_dt_dw", out_dtype=BF16)
    dh_in, dpre = _rms_bwd(s['h'], p['pre_g'], dh, [du_big, du_dt, _unperm(du_p)], name="mix_rms_bwd", tm=tm)
    dw_in = jnp.concatenate([dw_big[:, :GA0], dw_dt, dw_u5, dw_big[:, GA0:]], axis=1)
    bufs['w_in'] = dw_in.reshape(D_MODEL, N_DEV, -1).transpose(1, 0, 2)[:, None]
    grads = {
        'mix_pre_g': dpre, 'mix_post_g': dpost, 'ssd_conv_w': dconv_w, 'ssd_conv_b': dconv_b,
        'ssd_dt_bias': ddtb[:, 0, :8].reshape(-1), 'ssd_a_log': dalog[:, 0, :8].reshape(-1),
        'ssd_d': ddsk[:, 0, :8].reshape(-1), 'ssd_norm_g': dnorm,
        's5_lambda_re': dlam_re, 's5_lambda_im': dlam_im,
        's5_b_re': db_re.reshape(S5_GROUPS, S5_STATE, S5_GROUP), 's5_b_im': db_im.reshape(S5_GROUPS, S5_STATE, S5_GROUP),
        's5_c_re': dc_re, 's5_c_im': dc_im, 's5_log_step': dlog_step.reshape(-1), 's5_d': ds5d,
    }
    return dh_in, bufs, grads


HBM_SPEC = pl.BlockSpec(memory_space=pltpu.HBM)


def _place():
    return lax.axis_index("x"), lax.axis_index("y"), lax.axis_index("c")


GATHER_COLLECTIVE_ID = 1


def _all_gather(shards, *, name, on_sequencer=False):
    n = len(shards)

    def body(*refs):
        x_refs, out_refs = refs[:n], refs[n:2 * n]
        send_sems, recv_sems, local_sems = refs[2 * n:]
        x, y, c = _place()
        me, sibling = (x, y, c), (x, y, 1 - c)
        chips = [(1 - x, y), (x, 1 - y), (1 - x, 1 - y)]
        if on_sequencer:
            _handshake([sibling] + [(*chip, c) for chip in chips])

        def slot(o, px, py, pc):
            return out_refs[o].at[4 * px + 2 * py + pc]

        def copy(o, k, block, to, src=None):
            return pltpu.make_async_remote_copy(
                src_ref=slot(o, *block) if src is None else src, dst_ref=slot(o, *block),
                send_sem=send_sems.at[7 * o + k], recv_sem=recv_sems.at[7 * o + k], device_id=to, device_id_type=MESH)

        mine = [pltpu.make_async_copy(x_refs[o], slot(o, *me), local_sems.at[o]) for o in range(n)]
        for cp in mine:
            cp.start()
        first = []
        for j, chip in enumerate(chips):
            first += [copy(o, 1 + j, me, (*chip, c), src=x_refs[o]) for o in range(n)]
        first += [copy(o, 0, me, sibling, src=x_refs[o]) for o in range(n)]
        for cp in first:
            cp.start()
        passed = []
        for j, chip in enumerate(chips):
            for o in range(n):
                copy(o, 1 + j, (*chip, c), me).wait_recv()
                passed.append(copy(o, 4 + j, (*chip, c), sibling))
                passed[-1].start()
        for o in range(n):
            copy(o, 0, sibling, me).wait_recv()
        for j, chip in enumerate(chips):
            for o in range(n):
                copy(o, 4 + j, (*chip, 1 - c), me).wait_recv()
        for cp in first + passed:
            cp.wait_send()
        for cp in mine:
            cp.wait()

    out_shape = [jax.ShapeDtypeStruct((N_DEV,) + s.shape, s.dtype) for s in shards]
    sems = [pltpu.SemaphoreType.DMA((7 * n,)), pltpu.SemaphoreType.DMA((7 * n,)), pltpu.SemaphoreType.DMA((n,))]
    if on_sequencer:
        return _scall(body, name=name, out_type=out_shape, scratch_types=sems, collective_id=GATHER_COLLECTIVE_ID)(*shards)
    return _pcall(body, name=name, in_specs=[HBM_SPEC] * n, out_specs=[HBM_SPEC] * n, out_shape=out_shape,
                  scratch_shapes=sems)(*shards)


N_CHIPS = 4


SIBLING_COLLECTIVE_ID = 2
CHIPS_COLLECTIVE_ID = 3


def _handshake(peers):
    barrier = pltpu.get_barrier_semaphore()
    for peer in peers:
        pl.semaphore_signal(barrier, inc=1, device_id=peer, device_id_type=MESH)
    pl.semaphore_wait(barrier, len(peers))


def _exchange_sibling(grads, *, name):
    n = len(grads)

    def body(*refs):
        p_refs, q_refs = refs[:n], refs[n:2 * n]
        send_sems, recv_sems = refs[2 * n:]
        x, y, c = _place()
        _handshake([(x, y, 1 - c)])
        copies = [pltpu.make_async_remote_copy(
            src_ref=p_refs[o].at[k, 1 - c], dst_ref=q_refs[o].at[k], send_sem=send_sems.at[N_CHIPS * o + k],
            recv_sem=recv_sems.at[N_CHIPS * o + k], device_id=(x, y, 1 - c), device_id_type=MESH)
            for o in range(n) for k in range(N_CHIPS)]
        for cp in copies:
            cp.start()
        for cp in copies:
            cp.wait()

    return _scall(
        body, name=name, out_type=[jax.ShapeDtypeStruct((N_CHIPS,) + g.shape[2:], g.dtype) for g in grads],
        scratch_types=[pltpu.SemaphoreType.DMA((N_CHIPS * n,)), pltpu.SemaphoreType.DMA((N_CHIPS * n,))],
        collective_id=SIBLING_COLLECTIVE_ID,
    )(*grads)


def _pair_sum(own, got, *, name):
    _, _, r, l = own.shape
    tr = _tile(r, 512, 16)
    c = lax.axis_index("c").astype(jnp.int32).reshape(1)

    def body(c_ref, p_ref, q_ref, o_ref):
        o_ref[...] = (p_ref[...].astype(F32) + q_ref[...].astype(F32)).astype(o_ref.dtype)

    return _pcall(
        body, name=name,
        grid_spec=pltpu.PrefetchScalarGridSpec(
            num_scalar_prefetch=1, grid=(N_CHIPS, r // tr),
            in_specs=[pl.BlockSpec((None, None, tr, l), lambda k, i, cr: (k, cr[0], i, 0)),
                      pl.BlockSpec((None, tr, l), lambda k, i, cr: (k, i, 0))],
            out_specs=pl.BlockSpec((None, tr, l), lambda k, i, cr: (k, i, 0))),
        out_shape=jax.ShapeDtypeStruct((N_CHIPS, r, l), own.dtype),
        compiler_params=_params("parallel", "parallel"),
    )(c, own, got)


def _exchange_chips(parts, *, name):
    n = len(parts)

    def body(*refs):
        p_refs, g_refs = refs[:n], refs[n:2 * n]
        send_sems, recv_sems, local_sems = refs[2 * n:]
        x, y, c = _place()
        mine = 2 * x + y
        chips = [(1 - x, y), (x, 1 - y), (1 - x, 1 - y)]
        _handshake([(*chip, c) for chip in chips])
        own = [pltpu.make_async_copy(p_refs[o].at[mine], g_refs[o].at[mine], local_sems.at[o]) for o in range(n)]
        for cp in own:
            cp.start()
        copies = []
        for j, (px, py) in enumerate(chips):
            copies += [pltpu.make_async_remote_copy(
                src_ref=p_refs[o].at[2 * px + py], dst_ref=g_refs[o].at[mine], send_sem=send_sems.at[3 * o + j],
                recv_sem=recv_sems.at[3 * o + j], device_id=(px, py, c), device_id_type=MESH) for o in range(n)]
        for cp in copies:
            cp.start()
        for cp in copies:
            cp.wait()
        for cp in own:
            cp.wait()

    return _scall(
        body, name=name, out_type=[jax.ShapeDtypeStruct(p.shape, p.dtype) for p in parts],
        scratch_types=[pltpu.SemaphoreType.DMA((3 * n,)), pltpu.SemaphoreType.DMA((3 * n,)), pltpu.SemaphoreType.DMA((n,))],
        collective_id=CHIPS_COLLECTIVE_ID,
    )(*parts)


def _sum_slots(g, *, name):
    n, r, l = g.shape
    tr = _tile(r, 512, 16)

    def body(g_ref, o_ref):
        acc = g_ref[0].astype(F32)
        for k in range(1, n):
            acc = acc + g_ref[k].astype(F32)
        o_ref[...] = acc

    return _pcall(
        body, name=name, grid=(r // tr,), in_specs=[pl.BlockSpec((n, tr, l), lambda i: (0, i, 0))],
        out_specs=pl.BlockSpec((tr, l), lambda i: (i, 0)), out_shape=_sds((r, l)),
        compiler_params=_params("parallel"),
    )(g)


TRANSPOSED = ('ffn1_w_gate', 'ffn1_w_up', 'ffn2_w_gate', 'ffn2_w_up')
GATHER_CHUNKS = (('ffn1', ['ffn1_w_gate', 'ffn1_w_up']), ('ffn1_down', ['ffn1_w_down']),
                 ('mix_in', ['w_in', 'ssd_conv_w']), ('mix', ['w_branch_a', 's5_w_glu', 'w_branch_b', 'w_out']),
                 ('ffn2', ['ffn2_w_gate', 'ffn2_w_up']), ('ffn2_down', ['ffn2_w_down']))
LATE = ('ffn1_w_down', 'ffn2_w_down', 'w_branch_a', 's5_w_glu', 'w_branch_b', 'w_out')
SUBLAYERS = (('ffn1', ['ffn1_w_gate', 'ffn1_w_up', 'ffn1_w_down']),
             ('mix', ['w_in', 'ssd_conv_w', 'w_branch_a', 's5_w_glu', 'w_branch_b', 'w_out']),
             ('ffn2', ['ffn2_w_gate', 'ffn2_w_up', 'ffn2_w_down']))


def _gather_weights(w):
    layers, first = [], None
    for i in range(DEPTH):
        g = {}
        for tag, names in GATHER_CHUNKS:
            shards =[w[n][i:i + 1] if n == 'ssd_conv_w' else
                      (w[n][i:i + 1].transpose(0, 2, 1) if n in TRANSPOSED else w[n][i:i + 1]).astype(BF16) for n in names]
            if first is None:
                first = got = _all_gather(shards, name=f"gather_{tag}")
            else:
                shards, first = lax.optimization_barrier((shards, first))
                got = _all_gather(shards, name=f"gather_{tag}", on_sequencer=True)
            g.update(zip(names, got))
        layers.append(g)
    layers[0].update(zip(GATHER_CHUNKS[0][1], first))
    return layers


class _ReduceScatter:
    @staticmethod
    def sibling(tag, bufs):
        names = list(bufs)
        own = [bufs[n].reshape((N_CHIPS, 2) + bufs[n].shape[1:]) for n in names]
        return (tag, names), (own, _exchange_sibling(own, name=f"reduce_sibling_{tag}"))

    @staticmethod
    def chips(meta, arrays):
        (tag, names), (own, got) = meta, arrays
        flat = lambda a, lead: a.reshape(lead + (-1, a.shape[-1]))
        parts = [_pair_sum(flat(o, (N_CHIPS, 2)), flat(g, (N_CHIPS,)), name=f"reduce_pair_sum_{n}").reshape(g.shape)
                 for n, o, g in zip(names, own, got)]
        return names, _exchange_chips(parts, name=f"reduce_chips_{tag}")

    @staticmethod
    def done(names, slots):
        return dict(zip(names, slots))

    @staticmethod
    def small(grads):
        return _reduce_small(grads)


def _reduce_small(grads):
    flat = jnp.concatenate([g.astype(F32).reshape(-1) for g in grads.values()])
    pad = (-flat.shape[0]) % (8 * LANES)
    flat = jnp.concatenate([flat, jnp.zeros((pad,), F32)]).reshape(-1, LANES)
    gathered = _all_gather([flat], name="gather_small_grads", on_sequencer=True)[0]
    total = _sum_slots(gathered, name="sum_small_grads").reshape(-1)
    out, o = {}, 0
    for n, g in grads.items():
        out[n] = total[o:o + g.size].reshape(g.shape)
        o += g.size
    return out


def _adamw(w, g, m, v, *, name, slots=False):
    shape = w.shape
    if slots:
        lyr, rows, lanes = shape
        w2, m2, v2 = w, m, v
        tr = _tile(rows, 256, 16)
        nrt = rows // tr
        grid = (lyr, nrt)
        spec = pl.BlockSpec((None, tr, lanes), lambda l, i: (l, i, 0))
        g_specs = [pl.BlockSpec((N_CHIPS, None, tr, lanes),
                                lambda l, i, k=k: (0, 0, jnp.where(l == k, i, jnp.where(l > k, nrt - 1, 0)), 0))
                   for k in range(lyr)]
        g_args = list(g)
        out_shape = [_sds(shape)] * 4
    else:
        lanes = shape[-1] if (shape[-1] >= 128 or w.size % LANES) else LANES
        as2d = lambda a: a.reshape(-1, lanes)
        w2, m2, v2 = as2d(w), as2d(m), as2d(v)
        r = w2.shape[0]
        tr = _tile(r, 256, 8)
        grid = (1, r // tr)
        spec = pl.BlockSpec((tr, lanes), lambda l, i: (i, 0))
        g_specs, g_args = [spec], [as2d(g)]
        out_shape = [_sds((r, lanes))] * 4
    n_g = len(g_args)

    def body(w_ref, *rest):
        g_refs = rest[:n_g]
        m_ref, v_ref, go_ref, d_ref, mo_ref, vo_ref = rest[n_g:]
        if slots:
            gg = None
            for k, g_ref in enumerate(g_refs):
                tot = g_ref[0].astype(F32)
                for c in range(1, N_CHIPS):
                    tot = tot + g_ref[c].astype(F32)
                gg = tot if gg is None else jnp.where(pl.program_id(0) == k, tot, gg)
        else:
            gg = g_refs[0][...]
        go_ref[...] = gg
        mn = ADAM_B1 * m_ref[...] + (1.0 - ADAM_B1) * gg
        vn = ADAM_B2 * v_ref[...] + (1.0 - ADAM_B2) * (gg * gg)
        m_hat = mn / (1.0 - ADAM_B1 ** ADAM_STEP)
        v_hat = vn / (1.0 - ADAM_B2 ** ADAM_STEP)
        d_ref[...] = -ADAM_LR * (m_hat / (jnp.sqrt(v_hat) + ADAM_EPS) + ADAM_WD * w_ref[...])
        mo_ref[...] = mn
        vo_ref[...] = vn

    res = _pcall(
        body, name=name, grid=grid, in_specs=[spec] + g_specs + [spec, spec], out_specs=[spec] * 4,
        out_shape=out_shape, compiler_params=_params("arbitrary", "arbitrary"),
    )(w2, *g_args, m2, v2)
    return tuple(a.reshape(shape) for a in res)


def _sublayer_params(w, g, i, k, s5):
    row = lambda a: a.astype(F32).reshape(1, -1)
    if k != 'mix':
        return dict(layer=i, pre_g=row(w[f'{k}_pre_g'][i]), post_g=row(w[f'{k}_post_g'][i]),
                    w_gate=g[f'{k}_w_gate'], w_up=g[f'{k}_w_up'], w_down=g[f'{k}_w_down'])
    head8 = lambda a: jnp.broadcast_to(
        jnp.pad(a.astype(F32).reshape(SSD_GROUPS, 1, 8), ((0, 0), (0, 0), (0, PAD_HEADS - 8))), (SSD_GROUPS, 8, PAD_HEADS))
    by_cols = lambda n: g[n][:, 0].transpose(1, 0, 2).reshape(g[n].shape[2], -1)
    w_in = by_cols('w_in')
    s = np.cumsum([SSD_D_INNER, SSD_CONV_DIM, SSD_HEADS, S5_WIDTH, D_MODEL])
    return dict(
        layer=i, s5=s5, pre_g=row(w['mix_pre_g'][i]), post_g=row(w['mix_post_g'][i]),
        w_big=jnp.concatenate([w_in[:, :s[1]], w_in[:, s[3]:]], axis=1), w_dt=w_in[:, s[1]:s[2]], w_u5=w_in[:, s[2]:s[3]],
        conv_w=by_cols('ssd_conv_w'), conv_b=row(w['ssd_conv_b'][i]),
        dt_bias8=head8(w['ssd_dt_bias'][i]), a_log8=head8(w['ssd_a_log'][i]), d8=head8(w['ssd_d'][i]),
        norm_g=row(w['ssd_norm_g'][i]), late={n: g[n] for n in SUBLAYERS[1][1] if n in LATE},
        lam_re=w['s5_lambda_re'][i], lam_im=w['s5_lambda_im'][i], log_step=w['s5_log_step'][i].reshape(S5_GROUPS, 1),
        b_re=w['s5_b_re'][i].reshape(S5_GROUPS, -1), b_im=w['s5_b_im'][i].reshape(S5_GROUPS, -1),
        c_re=w['s5_c_re'][i], c_im=w['s5_c_im'][i], s5_d=row(w['s5_d'][i]),
    )


def _loss_head(h, target, *, tm):
    t, d = h.shape

    def fn(y, tgt):
        err = y - tgt
        return err * (1.0 / d), jnp.sum(0.5 * jnp.sum(err * err, axis=-1, keepdims=True) * (1.0 / d), axis=0, keepdims=True)

    dy, loss = _rows(fn, name="loss_head", nrow=t // tm,
                     ins=[(h, _rspec(tm, d)), (target, _rspec(tm, d))],
                     outs=[(_sds((t, d)), _rspec(tm, d), False), (_sds((1, 128)), _bspec(128), True)])
    return dy, loss[0, 0]


def _forward_backward(h, target, w, g, rs):
    t = h.shape[0]
    tm = _tile(t, 512, 8)
    s5 = None
    layers, saved = [], []
    for i in range(DEPTH):
        gi, ps, ss = dict(g[i]), [], []
        for tag, names in SUBLAYERS:
            if tag == 'mix' and s5 is None:
                mine = {n: w[n] for n in WEIGHTS if n.startswith('s5_') and n not in SHARDED}
                mine, h = lax.optimization_barrier((mine, h))
                s5 = _s5_prepare(mine)
            early = [n for n in names if n not in LATE]
            tied, h, s5 = lax.optimization_barrier(([gi[n] for n in early], h, s5))
            gi.update(zip(early, tied))
            p = _sublayer_params(w, gi, i, tag, s5[i] if tag == 'mix' else None)
            h, s = _mixer_fwd(h, p, tm) if tag == 'mix' else _ffn_fwd(h, p, tag, tm)
            ps.append(p)
            ss.append(s)
        layers.append(ps)
        saved.append(ss)
    dh, loss = _loss_head(h, target, tm=tm)
    reduced, small = [{} for _ in range(DEPTH)], [{} for _ in range(DEPTH)]
    in_sibling, in_chips = None, None

    def start_chips(x):
        nonlocal in_sibling, in_chips
        if in_sibling is not None:
            layer, meta, arrays = in_sibling
            arrays, x = lax.optimization_barrier((arrays, x))
            in_sibling, in_chips = None, (layer,) + tuple(rs.chips(meta, arrays))
        return x

    def finish_chips(x):
        nonlocal in_chips
        if in_chips is not None:
            layer, names, slots = in_chips
            slots, x = lax.optimization_barrier((slots, x))
            reduced[layer].update(rs.done(names, slots))
            in_chips = None
        return x

    for i in reversed(range(DEPTH)):
        for k in reversed(range(len(SUBLAYERS))):
            tag = SUBLAYERS[k][0]
            if tag == 'mix':
                dh, bufs, grads = _mixer_bwd(dh, layers[i][k], saved[i][k], tm, start_chips)
            else:
                dh, bufs, grads = _ffn_bwd(dh, layers[i][k], saved[i][k], tag, tm, start_chips)
            small[i].update(grads)
            dh = finish_chips(dh)
            in_sibling = (i,) + tuple(rs.sibling(tag, bufs))
            if tag == 'mix' and i + 1 < DEPTH:
                small[i + 1], dh = lax.optimization_barrier((small[i + 1], dh))
        if i == 0:
            small[i]['loss'] = loss.reshape(1)
        small[i] = rs.small(small[i])
    loss = small[0].pop('loss')[0]
    dh = finish_chips(start_chips(dh))
    shapes = {n: (w[n].shape[:-1] + (SSD_CONV_DIM,) if n == 'ssd_conv_w' else w[n].shape) for n in SMALL_ORDER}
    stacked = {n: jnp.stack([small[i][n].reshape(shapes[n][1:]) for i in range(DEPTH)]) for n in SMALL_ORDER}
    return loss, dh, reduced, stacked


def kernel(*args):
    n_w = len(WEIGHTS)
    x, target = args[0], args[1 + n_w]
    w = dict(zip(WEIGHTS, args[1:1 + n_w]))
    m = dict(zip(WEIGHTS, args[2 + n_w:2 + 2 * n_w]))
    v = dict(zip(WEIGHTS, args[2 + 2 * n_w:2 + 3 * n_w]))
    t = x.shape[1]

    g = _gather_weights(w)
    loss, dx, slots, small = _forward_backward(x.reshape(t, D_MODEL), target.reshape(t, D_MODEL), w, g, _ReduceScatter)
    me = 4 * lax.axis_index("x") + 2 * lax.axis_index("y") + lax.axis_index("c")
    cols = w['ssd_conv_w'].shape[-1]
    small['ssd_conv_w'] = lax.dynamic_slice_in_dim(small['ssd_conv_w'], me * cols, cols, axis=2)

    grad, delta, new_m, new_v = {}, {}, {}, {}
    for n in WEIGHTS:
        sharded = n in slots[0]
        view = (lambda a: a.transpose(0, 2, 1)) if n in TRANSPOSED else (lambda a: a)
        res = _adamw(view(w[n]), [slots[i][n] for i in range(DEPTH)] if sharded else small[n], view(m[n]), view(v[n]),
                     name=f"adamw_{n}", slots=sharded)
        grad[n], delta[n], new_m[n], new_v[n] = (view(a) for a in res)
    return (loss, dx.reshape(x.shape), *[grad[n] for n in WEIGHTS], *[delta[n] for n in WEIGHTS],
            *[new_m[n] for n in WEIGHTS], *[new_v[n] for n in WEIGHTS])
```
